```python
import math
import jax, jax.numpy as jnp
from jax import lax
import numpy as np

D_MODEL = 1024
BATCH = 8
SEQ = 4096
DEPTH = 4

DN_HEADS = 8
DN_HEAD_DIM = 128
DN_WIDTH = DN_HEADS * DN_HEAD_DIM
CONV_K = 4
CHUNK = 64
SSM_WIDTH = D_MODEL
SSM_GROUP = 16
SSM_GROUPS = SSM_WIDTH // SSM_GROUP
SSM_STATE = 64
DT_MIN = 0.001
DT_MAX = 0.1
D_IN = 3 * DN_WIDTH + DN_WIDTH + 2 * DN_HEADS + 2 * SSM_WIDTH + 2 * D_MODEL
EPS = 1e-6

kernel_name = "hybrid_deltanet_s5_gated_merge"


def _in_split_points():
    sizes = (3 * DN_WIDTH, DN_WIDTH, DN_HEADS, DN_HEADS, SSM_WIDTH, SSM_WIDTH, D_MODEL, D_MODEL)
    return [int(p) for p in np.cumsum(sizes)[:-1]]


def rms_norm(x, gain):
    xf = x.astype(jnp.float32)
    xf = xf * lax.rsqrt(jnp.mean(xf * xf, axis=-1, keepdims=True) + EPS)
    return (xf * gain.astype(jnp.float32)).astype(x.dtype)


def l2_normalize(x):
    xf = x.astype(jnp.float32)
    return xf * lax.rsqrt(jnp.sum(xf * xf, axis=-1, keepdims=True) + EPS)


def causal_depthwise_conv(x, w):
    return lax.conv_general_dilated(
        x, w[:, None, :].astype(x.dtype), window_strides=(1,), padding=[(CONV_K - 1, 0)],
        dimension_numbers=("NWC", "WIO", "NWC"), feature_group_count=x.shape[-1])


def chunked_gated_delta_rule(q, k, v, g, beta):
    b, s, h, d = q.shape
    n = s // CHUNK

    def chunks(t):
        t = jnp.moveaxis(t, 2, 1)
        return t.reshape((b, h, n, CHUNK) + t.shape[3:])

    q = chunks(q) * (d ** -0.5)
    k = chunks(k)
    v = chunks(v)
    g = jnp.cumsum(chunks(g), axis=-1)
    beta = chunks(beta)[..., None]
    idx = jnp.arange(CHUNK)
    causal = idx[:, None] >= idx[None, :]
    strict = idx[:, None] > idx[None, :]
    decay = jnp.exp(jnp.where(causal, g[..., :, None] - g[..., None, :], -jnp.inf))
    kb = k * beta
    l_mat = jnp.where(strict, jnp.einsum("bhncd,bhnjd->bhncj", kb, k) * decay, 0.0)
    rhs = jnp.concatenate([v * beta, kb * jnp.exp(g)[..., None]], axis=-1)
    sol = lax.linalg.triangular_solve(l_mat, rhs, left_side=True, lower=True, unit_diagonal=True)
    u_val, w = sol[..., :d], sol[..., d:]
    a_qk = jnp.einsum("bhncd,bhnjd->bhncj", q, k) * decay
    q_dec = q * jnp.exp(g)[..., None]
    g_last = g[..., -1]
    k_dec = k * jnp.exp(g_last[..., None] - g)[..., None]

    def step(state, xs):
        u_i, w_i, q_i, k_i, a_i, gl_i = xs
        v_new = u_i - jnp.einsum("bhcd,bhde->bhce", w_i, state)
        o_i = jnp.einsum("bhcd,bhde->bhce", q_i, state) + jnp.einsum("bhcj,bhje->bhce", a_i, v_new)
        state = state * jnp.exp(gl_i)[..., None, None] + jnp.einsum("bhcd,bhce->bhde", k_i, v_new)
        return state, o_i

    xs = tuple(jnp.moveaxis(t, 2, 0) for t in (u_val, w, q_dec, k_dec, a_qk, g_last))
    state0 = jnp.zeros((b, h, d, v.shape[-1]), jnp.float32)
    _, o = lax.scan(step, state0, xs)
    o = jnp.moveaxis(o, 0, 2).reshape(b, h, s, -1)
    return jnp.moveaxis(o, 1, 2)


def s5_ssm(u, a_re, a_im, log_dt, b_re, b_im, c_re, c_im, d_skip):
    bsz, s, _ = u.shape
    uf = u.astype(jnp.float32).reshape(bsz, s, SSM_GROUPS, SSM_GROUP)
    dt = jnp.exp(log_dt.astype(jnp.float32))[:, None]
    ar = a_re.astype(jnp.float32)
    ai = a_im.astype(jnp.float32)
    mag = jnp.exp(ar * dt)
    lr, li = mag * jnp.cos(ai * dt), mag * jnp.sin(ai * dt)
    den = ar * ar + ai * ai
    fr = ((lr - 1.0) * ar + li * ai) / den
    fi = (li * ar - (lr - 1.0) * ai) / den
    br, bi = b_re.astype(jnp.float32), b_im.astype(jnp.float32)
    bbr = fr[..., None] * br - fi[..., None] * bi
    bbi = fr[..., None] * bi + fi[..., None] * br
    xr = jnp.einsum("bsgc,gnc->bsgn", uf, bbr)
    xi = jnp.einsum("bsgc,gnc->bsgn", uf, bbi)
    lam_r = jnp.broadcast_to(lr, (s,) + lr.shape)[None]
    lam_i = jnp.broadcast_to(li, (s,) + li.shape)[None]

    def combine(e1, e2):
        a1r, a1i, b1r, b1i = e1
        a2r, a2i, b2r, b2i = e2
        return (a2r * a1r - a2i * a1i, a2r * a1i + a2i * a1r,
                a2r * b1r - a2i * b1i + b2r, a2r * b1i + a2i * b1r + b2i)

    _, _, hr, hi = lax.associative_scan(combine, (lam_r, lam_i, xr, xi), axis=1)
    y = (jnp.einsum("bsgn,gcn->bsgc", hr, c_re.astype(jnp.float32))
         - jnp.einsum("bsgn,gcn->bsgc", hi, c_im.astype(jnp.float32)))
    return y.reshape(bsz, s, SSM_WIDTH) + d_skip.astype(jnp.float32) * u.astype(jnp.float32)


def hybrid_layer(x, norm_pre, w_in, conv_w, a_log, dt_bias, head_norm,
                 ssm_a_re, ssm_a_im, ssm_log_dt, ssm_b_re, ssm_b_im, ssm_c_re, ssm_c_im, ssm_d,
                 w_glu, b_glu, w_out, norm_post):
    bsz, s, _ = x.shape
    h = rms_norm(x, norm_pre)
    proj = h @ w_in
    qkv, z_a, beta_logit, decay_logit, u, z_b, r_a, r_b = jnp.split(proj, _in_split_points(), axis=-1)

    qkv = jax.nn.silu(causal_depthwise_conv(qkv, conv_w))
    q, k, v = jnp.split(qkv, 3, axis=-1)
    heads = lambda t: t.reshape(bsz, s, DN_HEADS, DN_HEAD_DIM)
    q, k = l2_normalize(heads(q)), l2_normalize(heads(k))
    v = heads(v).astype(jnp.float32)
    beta = jax.nn.sigmoid(beta_logit.astype(jnp.float32))
    g = -jnp.exp(a_log.astype(jnp.float32)) * jax.nn.softplus(
        decay_logit.astype(jnp.float32) + dt_bias.astype(jnp.float32))
    o = chunked_gated_delta_rule(q, k, v, g, beta)
    y_a = rms_norm(o, head_norm).reshape(bsz, s, DN_WIDTH).astype(x.dtype) * jax.nn.silu(z_a)

    y = jax.nn.gelu(s5_ssm(u, ssm_a_re, ssm_a_im, ssm_log_dt, ssm_b_re, ssm_b_im,
                           ssm_c_re, ssm_c_im, ssm_d))
    y = y * jax.nn.sigmoid(y @ w_glu.astype(jnp.float32) + b_glu.astype(jnp.float32))
    y_b = y.astype(x.dtype) * jax.nn.silu(z_b)

    merged = jax.nn.sigmoid(r_a) * y_a + jax.nn.sigmoid(r_b) * y_b
    return x + rms_norm(merged @ w_out, norm_post)


def _fwd_setup_inputs(seed: int = 0) -> dict:
    key = jax.random.key(seed)
    ks = jax.random.split(key, 20)
    f32 = jnp.float32
    nrm = lambda k, shape, scale: jax.random.normal(k, shape, f32) * scale
    gain = lambda k, shape: 1.0 + 0.02 * jax.random.normal(k, shape, f32)
    dt = jnp.exp(jax.random.uniform(ks[4], (DEPTH, DN_HEADS), f32, math.log(DT_MIN), math.log(DT_MAX)))
    return {
        "x": jax.random.normal(ks[0], (BATCH, SEQ, D_MODEL), f32),
        "norm_pre": gain(ks[1], (DEPTH, D_MODEL)),
        "w_in": nrm(ks[2], (DEPTH, D_MODEL, D_IN), D_MODEL ** -0.5),
        "conv_w": nrm(ks[3], (DEPTH, CONV_K, 3 * DN_WIDTH), CONV_K ** -0.5),
        "a_log": jnp.log(jax.random.uniform(ks[5], (DEPTH, DN_HEADS), f32, 1.0, 16.0)),
        "dt_bias": dt + jnp.log(-jnp.expm1(-dt)),
        "head_norm": gain(ks[6], (DEPTH, DN_HEAD_DIM)),
        "ssm_a_re": -0.5 + 0.01 * jax.random.normal(ks[7], (DEPTH, SSM_GROUPS, SSM_STATE), f32),
        "ssm_a_im": math.pi * jnp.arange(SSM_STATE, dtype=f32)
                    + 0.01 * jax.random.normal(ks[8], (DEPTH, SSM_GROUPS, SSM_STATE), f32),
        "ssm_log_dt": jax.random.uniform(ks[9], (DEPTH, SSM_GROUPS), f32, math.log(DT_MIN), math.log(DT_MAX)),
        "ssm_b_re": nrm(ks[10], (DEPTH, SSM_GROUPS, SSM_STATE, SSM_GROUP), (2 * SSM_GROUP) ** -0.5),
        "ssm_b_im": nrm(ks[11], (DEPTH, SSM_GROUPS, SSM_STATE, SSM_GROUP), (2 * SSM_GROUP) ** -0.5),
        "ssm_c_re": nrm(ks[12], (DEPTH, SSM_GROUPS, SSM_GROUP, SSM_STATE), (2 * SSM_STATE) ** -0.5),
        "ssm_c_im": nrm(ks[13], (DEPTH, SSM_GROUPS, SSM_GROUP, SSM_STATE), (2 * SSM_STATE) ** -0.5),
        "ssm_d": nrm(ks[14], (DEPTH, SSM_WIDTH), 1.0),
        "w_glu": nrm(ks[15], (DEPTH, SSM_WIDTH, SSM_WIDTH), SSM_WIDTH ** -0.5),
        "b_glu": nrm(ks[16], (DEPTH, SSM_WIDTH), 0.01),
        "w_out": nrm(ks[17], (DEPTH, D_MODEL, D_MODEL), D_MODEL ** -0.5),
        "norm_post": gain(ks[18], (DEPTH, D_MODEL)),
    }


def _fwd_reference(x, norm_pre, w_in, conv_w, a_log, dt_bias, head_norm,
              ssm_a_re, ssm_a_im, ssm_log_dt, ssm_b_re, ssm_b_im, ssm_c_re, ssm_c_im, ssm_d,
              w_glu, b_glu, w_out, norm_post):
    for i in range(DEPTH):
        x = hybrid_layer(x, norm_pre[i], w_in[i], conv_w[i], a_log[i], dt_bias[i], head_norm[i],
                         ssm_a_re[i], ssm_a_im[i], ssm_log_dt[i], ssm_b_re[i], ssm_b_im[i],
                         ssm_c_re[i], ssm_c_im[i], ssm_d[i], w_glu[i], b_glu[i], w_out[i], norm_post[i])
    return x


import jax as _jax
import jax.numpy as _jnp

TWIN_FORMAT = 'train_step'
FWD_PARAMS = ['x', 'norm_pre', 'w_in', 'conv_w', 'a_log', 'dt_bias', 'head_norm', 'ssm_a_re', 'ssm_a_im', 'ssm_log_dt', 'ssm_b_re', 'ssm_b_im', 'ssm_c_re', 'ssm_c_im', 'ssm_d', 'w_glu', 'b_glu', 'w_out', 'norm_post']
TWIN_WEIGHTS = ['norm_pre', 'w_in', 'conv_w', 'a_log', 'dt_bias', 'head_norm', 'ssm_a_re', 'ssm_a_im', 'ssm_log_dt', 'ssm_b_re', 'ssm_b_im', 'ssm_c_re', 'ssm_c_im', 'ssm_d', 'w_glu', 'b_glu', 'w_out', 'norm_post']
TWIN_DIFF_INPUT = 'x'
TWIN_INPUTS = ['x', 'norm_pre', 'w_in', 'conv_w', 'a_log', 'dt_bias', 'head_norm', 'ssm_a_re', 'ssm_a_im', 'ssm_log_dt', 'ssm_b_re', 'ssm_b_im', 'ssm_c_re', 'ssm_c_im', 'ssm_d', 'w_glu', 'b_glu', 'w_out', 'norm_post', 'loss_target', 'm_norm_pre', 'm_w_in', 'm_conv_w', 'm_a_log', 'm_dt_bias', 'm_head_norm', 'm_ssm_a_re', 'm_ssm_a_im', 'm_ssm_log_dt', 'm_ssm_b_re', 'm_ssm_b_im', 'm_ssm_c_re', 'm_ssm_c_im', 'm_ssm_d', 'm_w_glu', 'm_b_glu', 'm_w_out', 'm_norm_post', 'v_norm_pre', 'v_w_in', 'v_conv_w', 'v_a_log', 'v_dt_bias', 'v_head_norm', 'v_ssm_a_re', 'v_ssm_a_im', 'v_ssm_log_dt', 'v_ssm_b_re', 'v_ssm_b_im', 'v_ssm_c_re', 'v_ssm_c_im', 'v_ssm_d', 'v_w_glu', 'v_b_glu', 'v_w_out', 'v_norm_post']
TWIN_OUTPUTS = ['loss', 'grad_x', 'grad_norm_pre', 'grad_w_in', 'grad_conv_w', 'grad_a_log', 'grad_dt_bias', 'grad_head_norm', 'grad_ssm_a_re', 'grad_ssm_a_im', 'grad_ssm_log_dt', 'grad_ssm_b_re', 'grad_ssm_b_im', 'grad_ssm_c_re', 'grad_ssm_c_im', 'grad_ssm_d', 'grad_w_glu', 'grad_b_glu', 'grad_w_out', 'grad_norm_post', 'delta_norm_pre', 'delta_w_in', 'delta_conv_w', 'delta_a_log', 'delta_dt_bias', 'delta_head_norm', 'delta_ssm_a_re', 'delta_ssm_a_im', 'delta_ssm_log_dt', 'delta_ssm_b_re', 'delta_ssm_b_im', 'delta_ssm_c_re', 'delta_ssm_c_im', 'delta_ssm_d', 'delta_w_glu', 'delta_b_glu', 'delta_w_out', 'delta_norm_post', 'new_m_norm_pre', 'new_m_w_in', 'new_m_conv_w', 'new_m_a_log', 'new_m_dt_bias', 'new_m_head_norm', 'new_m_ssm_a_re', 'new_m_ssm_a_im', 'new_m_ssm_log_dt', 'new_m_ssm_b_re', 'new_m_ssm_b_im', 'new_m_ssm_c_re', 'new_m_ssm_c_im', 'new_m_ssm_d', 'new_m_w_glu', 'new_m_b_glu', 'new_m_w_out', 'new_m_norm_post', 'new_v_norm_pre', 'new_v_w_in', 'new_v_conv_w', 'new_v_a_log', 'new_v_dt_bias', 'new_v_head_norm', 'new_v_ssm_a_re', 'new_v_ssm_a_im', 'new_v_ssm_log_dt', 'new_v_ssm_b_re', 'new_v_ssm_b_im', 'new_v_ssm_c_re', 'new_v_ssm_c_im', 'new_v_ssm_d', 'new_v_w_glu', 'new_v_b_glu', 'new_v_w_out', 'new_v_norm_post']
TWIN_LEAF_KINDS = {'loss': 'loss', 'grad_x': 'grad_x', 'grad_norm_pre': 'grad_w', 'grad_w_in': 'grad_w', 'grad_conv_w': 'grad_w', 'grad_a_log': 'grad_w', 'grad_dt_bias': 'grad_w', 'grad_head_norm': 'grad_w', 'grad_ssm_a_re': 'grad_w', 'grad_ssm_a_im': 'grad_w', 'grad_ssm_log_dt': 'grad_w', 'grad_ssm_b_re': 'grad_w', 'grad_ssm_b_im': 'grad_w', 'grad_ssm_c_re': 'grad_w', 'grad_ssm_c_im': 'grad_w', 'grad_ssm_d': 'grad_w', 'grad_w_glu': 'grad_w', 'grad_b_glu': 'grad_w', 'grad_w_out': 'grad_w', 'grad_norm_post': 'grad_w', 'delta_norm_pre': 'delta_w', 'delta_w_in': 'delta_w', 'delta_conv_w': 'delta_w', 'delta_a_log': 'delta_w', 'delta_dt_bias': 'delta_w', 'delta_head_norm': 'delta_w', 'delta_ssm_a_re': 'delta_w', 'delta_ssm_a_im': 'delta_w', 'delta_ssm_log_dt': 'delta_w', 'delta_ssm_b_re': 'delta_w', 'delta_ssm_b_im': 'delta_w', 'delta_ssm_c_re': 'delta_w', 'delta_ssm_c_im': 'delta_w', 'delta_ssm_d': 'delta_w', 'delta_w_glu': 'delta_w', 'delta_b_glu': 'delta_w', 'delta_w_out': 'delta_w', 'delta_norm_post': 'delta_w', 'new_m_norm_pre': 'new_m', 'new_m_w_in': 'new_m', 'new_m_conv_w': 'new_m', 'new_m_a_log': 'new_m', 'new_m_dt_bias': 'new_m', 'new_m_head_norm': 'new_m', 'new_m_ssm_a_re': 'new_m', 'new_m_ssm_a_im': 'new_m', 'new_m_ssm_log_dt': 'new_m', 'new_m_ssm_b_re': 'new_m', 'new_m_ssm_b_im': 'new_m', 'new_m_ssm_c_re': 'new_m', 'new_m_ssm_c_im': 'new_m', 'new_m_ssm_d': 'new_m', 'new_m_w_glu': 'new_m', 'new_m_b_glu': 'new_m', 'new_m_w_out': 'new_m', 'new_m_norm_post': 'new_m', 'new_v_norm_pre': 'new_v', 'new_v_w_in': 'new_v', 'new_v_conv_w': 'new_v', 'new_v_a_log': 'new_v', 'new_v_dt_bias': 'new_v', 'new_v_head_norm': 'new_v', 'new_v_ssm_a_re': 'new_v', 'new_v_ssm_a_im': 'new_v', 'new_v_ssm_log_dt': 'new_v', 'new_v_ssm_b_re': 'new_v', 'new_v_ssm_b_im': 'new_v', 'new_v_ssm_c_re': 'new_v', 'new_v_ssm_c_im': 'new_v', 'new_v_ssm_d': 'new_v', 'new_v_w_glu': 'new_v', 'new_v_b_glu': 'new_v', 'new_v_w_out': 'new_v', 'new_v_norm_post': 'new_v'}


def _forward(args):
    return _fwd_reference(*[args[k] for k in FWD_PARAMS])


def _output_shape():
    def fwd():
        inp = _fwd_setup_inputs(0)
        return _fwd_reference(*[inp[k] for k in FWD_PARAMS])
    out = _jax.eval_shape(fwd)
    return out.shape, out.dtype

N_MICROBATCH = 1
ADAM_LR = 0.001
ADAM_B1 = 0.9
ADAM_B2 = 0.999
ADAM_EPS = 1e-08
ADAM_WD = 0.01
ADAM_STEP = 10
PER_EXAMPLE_BATCH_AXIS = {'x': 0, 'loss_target': 0}
SHARED_INPUTS = []
_WEIGHT_DTYPES = {'norm_pre': _jnp.float32, 'w_in': _jnp.float32, 'conv_w': _jnp.float32, 'a_log': _jnp.float32, 'dt_bias': _jnp.float32, 'head_norm': _jnp.float32, 'ssm_a_re': _jnp.float32, 'ssm_a_im': _jnp.float32, 'ssm_log_dt': _jnp.float32, 'ssm_b_re': _jnp.float32, 'ssm_b_im': _jnp.float32, 'ssm_c_re': _jnp.float32, 'ssm_c_im': _jnp.float32, 'ssm_d': _jnp.float32, 'w_glu': _jnp.float32, 'b_glu': _jnp.float32, 'w_out': _jnp.float32, 'norm_post': _jnp.float32}
MOMENT_SCALE = {'norm_pre': 1.914230e+00, 'w_in': 6.458941e-01, 'conv_w': 1.170816e+00, 'a_log': 3.258374e+00, 'dt_bias': 3.195520e+00, 'head_norm': 7.519144e+00, 'ssm_a_re': 2.677945e-02, 'ssm_a_im': 3.069973e-02, 'ssm_log_dt': 1.169104e+01, 'ssm_b_re': 1.462497e-02, 'ssm_b_im': 1.597457e-02, 'ssm_c_re': 2.927671e-02, 'ssm_c_im': 2.971829e-02, 'ssm_d': 8.121209e-01, 'w_glu': 1.233234e-01, 'b_glu': 3.284618e-01, 'w_out': 3.196557e+00, 'norm_post': 3.225332e+01}


def _to_microbatches(a, axis):
    t = _jnp.moveaxis(a, axis, 0)
    t = t.reshape((N_MICROBATCH, t.shape[0] // N_MICROBATCH) + t.shape[1:])
    return _jnp.moveaxis(t, 1, axis + 1)


def setup_inputs(seed: int = 0) -> dict:
    inp = _fwd_setup_inputs(seed)
    key = _jax.random.fold_in(_jax.random.key(seed), 7919)
    shape, _ = _output_shape()
    out = dict(inp)
    out["loss_target"] = _jax.random.normal(_jax.random.fold_in(key, 0), shape, _jnp.float32)
    for i, name in enumerate(TWIN_WEIGHTS):
        w = inp[name].astype(_jnp.float32)
        if MOMENT_SCALE is None:
            s = _jnp.sqrt(_jnp.mean(_jnp.square(w)) + 1e-30)
        else:
            s = MOMENT_SCALE[name]
        km, kv = _jax.random.split(_jax.random.fold_in(key, i + 1))
        out[name] = w
        out["m_" + name] = s * _jax.random.normal(km, w.shape, _jnp.float32)
        out["v_" + name] = (s * s) * _jax.random.uniform(kv, w.shape, _jnp.float32, 0.5, 1.5)
    if N_MICROBATCH > 1:
        for name, axis in PER_EXAMPLE_BATCH_AXIS.items():
            out[name] = _to_microbatches(out[name], axis)
    return {'x': out['x'], 'norm_pre': out['norm_pre'], 'w_in': out['w_in'], 'conv_w': out['conv_w'], 'a_log': out['a_log'], 'dt_bias': out['dt_bias'], 'head_norm': out['head_norm'], 'ssm_a_re': out['ssm_a_re'], 'ssm_a_im': out['ssm_a_im'], 'ssm_log_dt': out['ssm_log_dt'], 'ssm_b_re': out['ssm_b_re'], 'ssm_b_im': out['ssm_b_im'], 'ssm_c_re': out['ssm_c_re'], 'ssm_c_im': out['ssm_c_im'], 'ssm_d': out['ssm_d'], 'w_glu': out['w_glu'], 'b_glu': out['b_glu'], 'w_out': out['w_out'], 'norm_post': out['norm_post'], 'loss_target': out['loss_target'], 'm_norm_pre': out['m_norm_pre'], 'm_w_in': out['m_w_in'], 'm_conv_w': out['m_conv_w'], 'm_a_log': out['m_a_log'], 'm_dt_bias': out['m_dt_bias'], 'm_head_norm': out['m_head_norm'], 'm_ssm_a_re': out['m_ssm_a_re'], 'm_ssm_a_im': out['m_ssm_a_im'], 'm_ssm_log_dt': out['m_ssm_log_dt'], 'm_ssm_b_re': out['m_ssm_b_re'], 'm_ssm_b_im': out['m_ssm_b_im'], 'm_ssm_c_re': out['m_ssm_c_re'], 'm_ssm_c_im': out['m_ssm_c_im'], 'm_ssm_d': out['m_ssm_d'], 'm_w_glu': out['m_w_glu'], 'm_b_glu': out['m_b_glu'], 'm_w_out': out['m_w_out'], 'm_norm_post': out['m_norm_post'], 'v_norm_pre': out['v_norm_pre'], 'v_w_in': out['v_w_in'], 'v_conv_w': out['v_conv_w'], 'v_a_log': out['v_a_log'], 'v_dt_bias': out['v_dt_bias'], 'v_head_norm': out['v_head_norm'], 'v_ssm_a_re': out['v_ssm_a_re'], 'v_ssm_a_im': out['v_ssm_a_im'], 'v_ssm_log_dt': out['v_ssm_log_dt'], 'v_ssm_b_re': out['v_ssm_b_re'], 'v_ssm_b_im': out['v_ssm_b_im'], 'v_ssm_c_re': out['v_ssm_c_re'], 'v_ssm_c_im': out['v_ssm_c_im'], 'v_ssm_d': out['v_ssm_d'], 'v_w_glu': out['v_w_glu'], 'v_b_glu': out['v_b_glu'], 'v_w_out': out['v_w_out'], 'v_norm_post': out['v_norm_post']}


def _loss(weights, diff, rest, loss_target):
    with _jax.named_scope("forward"):
        args = {**rest, TWIN_DIFF_INPUT: diff, **{k: w.astype(_WEIGHT_DTYPES[k]) for k, w in weights.items()}}
        y = _forward(args)
    with _jax.named_scope("loss_head"):
        err = _jnp.square(y.astype(_jnp.float32) - loss_target)
        return 0.5 * _jnp.sum(_jnp.mean(err, axis=-1)) if err.ndim else 0.5 * err


def _adamw(w, g, m, v):
    m = ADAM_B1 * m + (1.0 - ADAM_B1) * g
    v = ADAM_B2 * v + (1.0 - ADAM_B2) * _jnp.square(g)
    m_hat = m / (1.0 - ADAM_B1 ** ADAM_STEP)
    v_hat = v / (1.0 - ADAM_B2 ** ADAM_STEP)
    delta = -ADAM_LR * (m_hat / (_jnp.sqrt(v_hat) + ADAM_EPS) + ADAM_WD * w)
    return delta, m, v


def reference(x, norm_pre, w_in, conv_w, a_log, dt_bias, head_norm, ssm_a_re, ssm_a_im, ssm_log_dt, ssm_b_re, ssm_b_im, ssm_c_re, ssm_c_im, ssm_d, w_glu, b_glu, w_out, norm_post, loss_target, m_norm_pre, m_w_in, m_conv_w, m_a_log, m_dt_bias, m_head_norm, m_ssm_a_re, m_ssm_a_im, m_ssm_log_dt, m_ssm_b_re, m_ssm_b_im, m_ssm_c_re, m_ssm_c_im, m_ssm_d, m_w_glu, m_b_glu, m_w_out, m_norm_post, v_norm_pre, v_w_in, v_conv_w, v_a_log, v_dt_bias, v_head_norm, v_ssm_a_re, v_ssm_a_im, v_ssm_log_dt, v_ssm_b_re, v_ssm_b_im, v_ssm_c_re, v_ssm_c_im, v_ssm_d, v_w_glu, v_b_glu, v_w_out, v_norm_post):
    given = dict(x=x, norm_pre=norm_pre, w_in=w_in, conv_w=conv_w, a_log=a_log, dt_bias=dt_bias, head_norm=head_norm, ssm_a_re=ssm_a_re, ssm_a_im=ssm_a_im, ssm_log_dt=ssm_log_dt, ssm_b_re=ssm_b_re, ssm_b_im=ssm_b_im, ssm_c_re=ssm_c_re, ssm_c_im=ssm_c_im, ssm_d=ssm_d, w_glu=w_glu, b_glu=b_glu, w_out=w_out, norm_post=norm_post, loss_target=loss_target, m_norm_pre=m_norm_pre, m_w_in=m_w_in, m_conv_w=m_conv_w, m_a_log=m_a_log, m_dt_bias=m_dt_bias, m_head_norm=m_head_norm, m_ssm_a_re=m_ssm_a_re, m_ssm_a_im=m_ssm_a_im, m_ssm_log_dt=m_ssm_log_dt, m_ssm_b_re=m_ssm_b_re, m_ssm_b_im=m_ssm_b_im, m_ssm_c_re=m_ssm_c_re, m_ssm_c_im=m_ssm_c_im, m_ssm_d=m_ssm_d, m_w_glu=m_w_glu, m_b_glu=m_b_glu, m_w_out=m_w_out, m_norm_post=m_norm_post, v_norm_pre=v_norm_pre, v_w_in=v_w_in, v_conv_w=v_conv_w, v_a_log=v_a_log, v_dt_bias=v_dt_bias, v_head_norm=v_head_norm, v_ssm_a_re=v_ssm_a_re, v_ssm_a_im=v_ssm_a_im, v_ssm_log_dt=v_ssm_log_dt, v_ssm_b_re=v_ssm_b_re, v_ssm_b_im=v_ssm_b_im, v_ssm_c_re=v_ssm_c_re, v_ssm_c_im=v_ssm_c_im, v_ssm_d=v_ssm_d, v_w_glu=v_w_glu, v_b_glu=v_b_glu, v_w_out=v_w_out, v_norm_post=v_norm_post)
    weights = {n: given[n] for n in TWIN_WEIGHTS}
    shared = {n: given[n] for n in SHARED_INPUTS}
    per_example = {n: given[n] for n in ['x']}
    grad_fn = _jax.value_and_grad(_loss, argnums=(0, 1))

    def one_microbatch(ex, loss_target):
        ex = dict(ex)
        diff = ex.pop(TWIN_DIFF_INPUT)
        return grad_fn(weights, diff, {**shared, **ex}, loss_target)

    if N_MICROBATCH == 1:
        loss, (grad_w, grad_x) = one_microbatch(per_example, given["loss_target"])
    else:
        def body(carry, xs):
            loss_sum, grad_sum = carry
            l_k, (gw_k, gx_k) = one_microbatch(xs[0], xs[1])
            with _jax.named_scope("update"):
                return (loss_sum + l_k, _jax.tree.map(_jnp.add, grad_sum, gw_k)), gx_k

        init = (_jnp.zeros((), _jnp.float32), _jax.tree.map(_jnp.zeros_like, weights))
        (loss, grad_w), grad_x = _jax.lax.scan(body, init, (per_example, given["loss_target"]))
    with _jax.named_scope("update"):
        delta_w, new_m, new_v = {}, {}, {}
        for n in TWIN_WEIGHTS:
            delta_w[n], new_m[n], new_v[n] = _adamw(weights[n], grad_w[n], given["m_" + n], given["v_" + n])
    return (loss, grad_x, *[grad_w[n] for n in TWIN_WEIGHTS], *[delta_w[n] for n in TWIN_WEIGHTS],
            *[new_m[n] for n in TWIN_WEIGHTS], *[new_v[n] for n in TWIN_WEIGHTS])
```

```python
import functools
import math

import jax
import jax.numpy as jnp
from jax import lax
from jax.experimental import pallas as pl
from jax.experimental.pallas import tpu as pltpu

f32 = jnp.float32
bf16 = jnp.bfloat16

D_MODEL = 1024
DEPTH = 4
N_DEV = 8
DN_HEADS = 8
HEAD_DIM = 128
CHUNK = 64
CONV_K = 4
SSM_GROUPS = 64
SSM_GROUP = 16
SSM_STATE = 64
SUPER = 8
N_SUPER = SSM_GROUPS // SUPER
SG_STATE = SUPER * SSM_STATE
EPS = 1e-6
W_COLS = 8208
W_PAD = 8320
ADAM_LR, ADAM_B1, ADAM_B2, ADAM_EPS, ADAM_WD, ADAM_STEP = 0.001, 0.9, 0.999, 1e-08, 0.01, 10
VMEM_LIMIT = 56 * 1024 * 1024
MESH = pl.DeviceIdType.MESH
HIGH = lax.Precision.HIGH


def _call(body, name, out_shape, grid=None, in_specs=None, out_specs=None, scratch=(), **kw):
    args = dict(out_shape=out_shape, name=name, scratch_shapes=list(scratch),
                compiler_params=pltpu.CompilerParams(vmem_limit_bytes=VMEM_LIMIT, **kw))
    if grid is not None:
        args.update(grid=grid, in_specs=in_specs, out_specs=out_specs)
    else:
        if in_specs is not None:
            args.update(in_specs=in_specs)
        if out_specs is not None:
            args.update(out_specs=out_specs)
    return pl.pallas_call(body, **args)


def _sds(shape, dtype=f32):
    return jax.ShapeDtypeStruct(tuple(shape), dtype)


def _sigmoid(x):
    return 1.0 / (1.0 + jnp.exp(-x))


def _silu(x):
    return x * _sigmoid(x)


def _dsilu(x):
    s = _sigmoid(x)
    return s * (1.0 + x * (1.0 - s))


_GELU_C = math.sqrt(2.0 / math.pi)


def _gelu(x):
    return 0.5 * x * (1.0 + jnp.tanh(_GELU_C * (x + 0.044715 * x * x * x)))


def _dgelu(x):
    t = jnp.tanh(_GELU_C * (x + 0.044715 * x * x * x))
    return 0.5 * (1.0 + t) + 0.5 * x * (1.0 - t * t) * _GELU_C * (1.0 + 3 * 0.044715 * x * x)


def _softplus(x):
    return jnp.maximum(x, 0.0) + jnp.log(1.0 + jnp.exp(-jnp.abs(x)))


def _bdot(a, b, dn):
    return lax.dot_general(a.astype(bf16), b.astype(bf16), (dn, ((), ())), preferred_element_type=f32)


def _nn(a, b):
    return _bdot(a, b, ((1,), (0,)))


def _nt(a, b):
    return _bdot(a, b, ((1,), (1,)))


def _tn(a, b):
    return _bdot(a, b, ((0,), (0,)))


def _hdot(a, b, dn=((1,), (0,))):
    return lax.dot_general(a, b, (dn, ((), ())), precision=HIGH, preferred_element_type=f32)


def _matmul(name, a, b, *, dn, grid, a_spec, b_spec, o_spec, o_shape):
    nk = grid[-1]

    def body(a_ref, b_ref, o_ref, acc_ref):
        p = _bdot(a_ref[...], b_ref[...], dn)
        if nk == 1:
            o_ref[...] = p
        else:
            k = pl.program_id(len(grid) - 1)

            @pl.when(k == 0)
            def _():
                acc_ref[...] = p

            @pl.when(k > 0)
            def _():
                acc_ref[...] += p

            @pl.when(k == nk - 1)
            def _():
                o_ref[...] = acc_ref[...]

    blk = tuple(d for d in o_spec.block_shape if d is not None)
    return _call(body, name, _sds(o_shape), grid, [a_spec, b_spec], o_spec,
                 scratch=[pltpu.VMEM(blk if nk > 1 else (8, 128), f32)])(a, b)


def _tile(n, pref):
    for t in pref:
        if n % t == 0:
            return t
    return n


def mm_nn(name, a, b):
    m, k = a.shape
    n = b.shape[1]
    tm, tn, tk = _tile(m, (512, 256)), _tile(n, (1664, 1024, 512)), _tile(k, (1664, 1024))
    return _matmul(name, a, b, dn=((1,), (0,)), grid=(m // tm, n // tn, k // tk),
                   a_spec=pl.BlockSpec((tm, tk), lambda i, j, l: (i, l)),
                   b_spec=pl.BlockSpec((tk, tn), lambda i, j, l: (l, j)),
                   o_spec=pl.BlockSpec((tm, tn), lambda i, j, l: (i, j)), o_shape=(m, n))


def mm_nt(name, a, b):
    m, k = a.shape
    n = b.shape[0]
    tm, tn, tk = _tile(m, (512, 256)), _tile(n, (1024, 512)), _tile(k, (1664, 1024))
    return _matmul(name, a, b, dn=((1,), (1,)), grid=(m // tm, n // tn, k // tk),
                   a_spec=pl.BlockSpec((tm, tk), lambda i, j, l: (i, l)),
                   b_spec=pl.BlockSpec((tn, tk), lambda i, j, l: (j, l)),
                   o_spec=pl.BlockSpec((tm, tn), lambda i, j, l: (i, j)), o_shape=(m, n))


def mm_tn(name, a, b):
    k, m = a.shape
    n = b.shape[1]
    tm, tn, tk = _tile(m, (512,)), _tile(n, (1664, 1024, 512)), _tile(k, (512, 256))
    return _matmul(name, a, b, dn=((0,), (0,)), grid=(m // tm, n // tn, k // tk),
                   a_spec=pl.BlockSpec((tk, tm), lambda i, j, l: (l, i)),
                   b_spec=pl.BlockSpec((tk, tn), lambda i, j, l: (l, j)),
                   o_spec=pl.BlockSpec((tm, tn), lambda i, j, l: (i, j)), o_shape=(m, n))


def sg_in(name, u, bbd):
    t = u.shape[0]
    tm = _tile(t, (512, 256))
    return _matmul(name, u, bbd, dn=((1,), (0,)), grid=(t // tm, N_SUPER, 1),
                   a_spec=pl.BlockSpec((tm, 128), lambda i, s, l: (i, s)),
                   b_spec=pl.BlockSpec((None, 128, 1024), lambda i, s, l: (s, 0, 0)),
                   o_spec=pl.BlockSpec((tm, 1024), lambda i, s, l: (i, s)), o_shape=(t, 8192))


def sg_out(name, h, cbd):
    t = h.shape[0]
    tm = _tile(t, (512, 256))
    return _matmul(name, h, cbd, dn=((1,), (0,)), grid=(t // tm, N_SUPER, 1),
                   a_spec=pl.BlockSpec((tm, 1024), lambda i, s, l: (i, s)),
                   b_spec=pl.BlockSpec((None, 1024, 128), lambda i, s, l: (s, 0, 0)),
                   o_spec=pl.BlockSpec((tm, 128), lambda i, s, l: (i, s)), o_shape=(t, 1024))


def sg_in_t(name, a, bbd):
    t = a.shape[0]
    tm = _tile(t, (512, 256))
    return _matmul(name, a, bbd, dn=((1,), (1,)), grid=(t // tm, N_SUPER, 1),
                   a_spec=pl.BlockSpec((tm, 1024), lambda i, s, l: (i, s)),
                   b_spec=pl.BlockSpec((None, 128, 1024), lambda i, s, l: (s, 0, 0)),
                   o_spec=pl.BlockSpec((tm, 128), lambda i, s, l: (i, s)), o_shape=(t, 1024))


def sg_out_t(name, dy, cbd):
    t = dy.shape[0]
    tm = _tile(t, (512, 256))
    return _matmul(name, dy, cbd, dn=((1,), (1,)), grid=(t // tm, N_SUPER, 1),
                   a_spec=pl.BlockSpec((tm, 128), lambda i, s, l: (i, s)),
                   b_spec=pl.BlockSpec((None, 1024, 128), lambda i, s, l: (s, 0, 0)),
                   o_spec=pl.BlockSpec((tm, 1024), lambda i, s, l: (i, s)), o_shape=(t, 8192))


def sg_grad(name, a, b, wa, wb):
    t = a.shape[0]
    tk = _tile(t, (512, 256))
    return _matmul(name, a, b, dn=((0,), (0,)), grid=(N_SUPER, t // tk),
                   a_spec=pl.BlockSpec((tk, wa), lambda s, l: (l, s)),
                   b_spec=pl.BlockSpec((tk, wb), lambda s, l: (l, s)),
                   o_spec=pl.BlockSpec((None, wa, wb), lambda s, l: (s, 0, 0)), o_shape=(N_SUPER, wa, wb))


def _rows(t):
    return _tile(t, (256,))


def _row_spec(tm, w):
    return pl.BlockSpec((tm, w), lambda i: (i, 0))


def _acc_spec(r, w):
    return pl.BlockSpec((r, w), lambda i: (0, 0))


def _accumulate(ref, val):
    @pl.when(pl.program_id(0) == 0)
    def _():
        ref[...] = val

    @pl.when(pl.program_id(0) > 0)
    def _():
        ref[...] += val


def rmsnorm_fwd(name, x, gain):
    t, d = x.shape
    tm = _rows(t)

    def body(x_ref, g_ref, o_ref):
        xv = x_ref[...]
        r = lax.rsqrt(jnp.mean(xv * xv, axis=-1, keepdims=True) + EPS)
        o_ref[...] = xv * r * g_ref[...]

    return _call(body, name, _sds((t, d)), (t // tm,), [_row_spec(tm, d), _acc_spec(1, d)], _row_spec(tm, d))(x, gain)


def rmsnorm_bwd(name, x, gain, dn, dres):
    t, d = x.shape
    tm = _rows(t)

    def body(x_ref, g_ref, dn_ref, dr_ref, dx_ref, dg_ref):
        xv = x_ref[...]
        r = lax.rsqrt(jnp.mean(xv * xv, axis=-1, keepdims=True) + EPS)
        n = xv * r
        dnv = dn_ref[...]
        _accumulate(dg_ref, jnp.sum(dnv * n, axis=0, keepdims=True))
        dng = dnv * g_ref[...]
        dx_ref[...] = dr_ref[...] + r * (dng - n * jnp.mean(dng * n, axis=-1, keepdims=True))

    return _call(body, name, (_sds((t, d)), _sds((1, d))), (t // tm,),
                 [_row_spec(tm, d), _acc_spec(1, d), _row_spec(tm, d), _row_spec(tm, d)],
                 (_row_spec(tm, d), _acc_spec(1, d)))(x, gain, dn, dres)


def residual_norm_fwd(name, x, out, gain):
    t, d = x.shape
    tm = _rows(t)

    def body(x_ref, o_ref, g_ref, y_ref):
        ov = o_ref[...]
        r = lax.rsqrt(jnp.mean(ov * ov, axis=-1, keepdims=True) + EPS)
        y_ref[...] = x_ref[...] + ov * r * g_ref[...]

    return _call(body, name, _sds((t, d)), (t // tm,), [_row_spec(tm, d), _row_spec(tm, d), _acc_spec(1, d)],
                 _row_spec(tm, d))(x, out, gain)


def post_norm_bwd(name, out, gain, dy):
    t, d = out.shape
    tm = _rows(t)

    def body(o_ref, g_ref, dy_ref, do_ref, dg_ref):
        ov = o_ref[...]
        r = lax.rsqrt(jnp.mean(ov * ov, axis=-1, keepdims=True) + EPS)
        n = ov * r
        dyv = dy_ref[...]
        _accumulate(dg_ref, jnp.sum(dyv * n, axis=0, keepdims=True))
        dng = dyv * g_ref[...]
        do_ref[...] = r * (dng - n * jnp.mean(dng * n, axis=-1, keepdims=True))

    return _call(body, name, (_sds((t, d)), _sds((1, d))), (t // tm,),
                 [_row_spec(tm, d), _acc_spec(1, d), _row_spec(tm, d)],
                 (_row_spec(tm, d), _acc_spec(1, d)))(out, gain, dy)


def loss_head(name, y, target):
    t, d = y.shape
    tm = _rows(t)

    def body(y_ref, t_ref, l_ref, dy_ref):
        e = y_ref[...] - t_ref[...]
        dy_ref[...] = e * (1.0 / d)
        s = jnp.sum(jnp.sum(e * e, axis=1, keepdims=True), axis=0, keepdims=True) * (0.5 / d)
        _accumulate(l_ref, s)

    return _call(body, name, (_sds((1, 1)), _sds((t, d))), (t // tm,),
                 [_row_spec(tm, d), _row_spec(tm, d)], (_acc_spec(1, 1), _row_spec(tm, d)))(y, target)


def _prev_spec(tm, w):
    return pl.BlockSpec((8, w), lambda i: (jnp.maximum(i * (tm // 8) - 1, 0), 0))


def _next_spec(tm, w, t):
    return pl.BlockSpec((8, w), lambda i: (jnp.minimum((i + 1) * (tm // 8), t // 8 - 1), 0))


def _fill_pad(pad_ref, prev_ref, cur_ref, tm):
    keep = (pl.program_id(0) > 0).astype(f32)
    pad_ref[0:8, :] = prev_ref[...] * keep
    pad_ref[8:8 + tm, :] = cur_ref[...]


def _conv_block(pad_ref, w_ref, cb, tm):
    cs = slice(cb * 128, (cb + 1) * 128)
    acc = pad_ref[pl.ds(8 - (CONV_K - 1), tm), cs] * w_ref[0:1, cs]
    for j in range(1, CONV_K):
        acc = acc + pad_ref[pl.ds(8 - (CONV_K - 1) + j, tm), cs] * w_ref[j:j + 1, cs]
    return acc


def conv_qkv_fwd(name, proj, conv_w, a_log, dt_bias):
    t = proj.shape[0]
    tm = _rows(t)
    scale = HEAD_DIM ** -0.5

    def body(cur_ref, prev_ref, w_ref, bd_ref, al_ref, db_ref, q_ref, k_ref, v_ref, bg_ref, pad_ref):
        _fill_pad(pad_ref, prev_ref, cur_ref, tm)
        for cb in range(3 * DN_HEADS):
            s = _silu(_conv_block(pad_ref, w_ref, cb, tm))
            hs = slice((cb % DN_HEADS) * 128, (cb % DN_HEADS + 1) * 128)
            if cb < DN_HEADS:
                q_ref[:, hs] = s * (lax.rsqrt(jnp.sum(s * s, axis=-1, keepdims=True) + EPS) * scale)
            elif cb < 2 * DN_HEADS:
                k_ref[:, hs] = s * lax.rsqrt(jnp.sum(s * s, axis=-1, keepdims=True) + EPS)
            else:
                v_ref[:, hs] = s
        bd = bd_ref[...]
        beta = _sigmoid(bd)
        g = -jnp.exp(al_ref[...]) * _softplus(bd + db_ref[...])
        lane = lax.broadcasted_iota(jnp.int32, bd.shape, 1)
        bg_ref[...] = jnp.where(lane < DN_HEADS, beta, jnp.where(lane < 2 * DN_HEADS, g, 0.0))

    w3 = 3 * D_MODEL
    return _call(body, name, (_sds((t, D_MODEL)),) * 3 + (_sds((t, 128)),), (t // tm,),
                 [pl.BlockSpec((tm, w3), lambda i: (i, 0)), _prev_spec(tm, w3), _acc_spec(CONV_K, w3),
                  pl.BlockSpec((tm, 128), lambda i: (i, 8192 // 128)), _acc_spec(1, 128), _acc_spec(1, 128)],
                 (_row_spec(tm, D_MODEL),) * 3 + (_row_spec(tm, 128),),
                 scratch=[pltpu.VMEM((tm + 8, w3), f32)])(proj, proj, conv_w, proj, a_log, dt_bias)


def conv_qkv_bwd(name, proj, conv_w, a_log, dt_bias, dq, dk, dv, dbg):
    t = proj.shape[0]
    tm = _rows(t)
    scale = HEAD_DIM ** -0.5

    def body(cur_ref, prev_ref, w_ref, bd_ref, al_ref, db_ref, dq_ref, dk_ref, dv_ref, dbg_ref,
             dc_ref, dw_ref, dbd_ref, dab_ref, pad_ref):
        _fill_pad(pad_ref, prev_ref, cur_ref, tm)

        @pl.when(pl.program_id(0) == 0)
        def _():
            dw_ref[...] = jnp.zeros_like(dw_ref)

        for cb in range(3 * DN_HEADS):
            cs = slice(cb * 128, (cb + 1) * 128)
            hs = slice((cb % DN_HEADS) * 128, (cb % DN_HEADS + 1) * 128)
            c = _conv_block(pad_ref, w_ref, cb, tm)
            s = _silu(c)
            if cb < 2 * DN_HEADS:
                dn = (dq_ref[:, hs] * scale) if cb < DN_HEADS else dk_ref[:, hs]
                r = lax.rsqrt(jnp.sum(s * s, axis=-1, keepdims=True) + EPS)
                ds = r * dn - s * (r * r * r) * jnp.sum(dn * s, axis=-1, keepdims=True)
            else:
                ds = dv_ref[:, hs]
            dc = ds * _dsilu(c)
            dc_ref[:, cs] = dc
            for j in range(CONV_K):
                dw_ref[j:j + 1, cs] += jnp.sum(dc * pad_ref[pl.ds(8 - (CONV_K - 1) + j, tm), cs], axis=0,
                                               keepdims=True)
        bd = bd_ref[...]
        dbg_v = dbg_ref[...]
        lane = lax.broadcasted_iota(jnp.int32, bd.shape, 1)
        sg = _sigmoid(bd)
        ea = jnp.exp(al_ref[...])
        z = bd + db_ref[...]
        sp = _softplus(z)
        is_b = lane < DN_HEADS
        is_g = jnp.logical_and(lane >= DN_HEADS, lane < 2 * DN_HEADS)
        d_z = jnp.where(is_g, dbg_v * (-ea) * _sigmoid(z), 0.0)
        dbd_ref[...] = jnp.where(is_b, dbg_v * sg * (1.0 - sg), d_z)
        d_al = jnp.sum(jnp.where(is_g, dbg_v * (-ea) * sp, 0.0), axis=0, keepdims=True)
        d_db = jnp.sum(d_z, axis=0, keepdims=True)
        _accumulate(dab_ref, jnp.concatenate([d_al, d_db] + [jnp.zeros_like(d_al)] * 6, axis=0))

    w3 = 3 * D_MODEL
    return _call(body, name, (_sds((t, w3)), _sds((CONV_K, w3)), _sds((t, 128)), _sds((8, 128))), (t // tm,),
                 [pl.BlockSpec((tm, w3), lambda i: (i, 0)), _prev_spec(tm, w3), _acc_spec(CONV_K, w3),
                  pl.BlockSpec((tm, 128), lambda i: (i, 8192 // 128)), _acc_spec(1, 128), _acc_spec(1, 128),
                  _row_spec(tm, D_MODEL), _row_spec(tm, D_MODEL), _row_spec(tm, D_MODEL), _row_spec(tm, 128)],
                 (_row_spec(tm, w3), _acc_spec(CONV_K, w3), _row_spec(tm, 128), _acc_spec(8, 128)),
                 scratch=[pltpu.VMEM((tm + 8, w3), f32)])(proj, proj, conv_w, proj, a_log, dt_bias, dq, dk, dv, dbg)


def conv_transpose(name, dc, conv_w):
    t, w3 = dc.shape
    tm = _rows(t)
    nt = t // tm

    def body(cur_ref, nxt_ref, w_ref, o_ref, pad_ref):
        keep = (pl.program_id(0) < nt - 1).astype(f32)
        pad_ref[0:tm, :] = cur_ref[...]
        pad_ref[tm:tm + 8, :] = nxt_ref[...] * keep
        for cb in range(w3 // 128):
            cs = slice(cb * 128, (cb + 1) * 128)
            acc = pad_ref[pl.ds(CONV_K - 1, tm), cs] * w_ref[0:1, cs]
            for j in range(1, CONV_K):
                acc = acc + pad_ref[pl.ds(CONV_K - 1 - j, tm), cs] * w_ref[j:j + 1, cs]
            o_ref[:, cs] = acc

    return _call(body, name, _sds((t, w3)), (nt,),
                 [_row_spec(tm, w3), _next_spec(tm, w3, t), _acc_spec(CONV_K, w3)], _row_spec(tm, w3),
                 scratch=[pltpu.VMEM((tm + 8, w3), f32)])(dc, dc, conv_w)


def _delta_local(q, k, v, g_row, b_row):
    c = CHUNK
    ii = lax.broadcasted_iota(jnp.int32, (c, c), 0)
    jj = lax.broadcasted_iota(jnp.int32, (c, c), 1)
    eye, lower, strict = ii == jj, ii >= jj, ii > jj
    g_b = jnp.broadcast_to(g_row, (c, c))
    gc_col = jnp.sum(jnp.where(lower, g_b, 0.0), axis=1, keepdims=True)
    gc_row = jnp.sum(jnp.where(eye, jnp.broadcast_to(gc_col, (c, c)), 0.0), axis=0, keepdims=True)
    b_col = jnp.sum(jnp.where(eye, jnp.broadcast_to(b_row, (c, c)), 0.0), axis=1, keepdims=True)
    gl = jnp.sum(g_row, axis=1, keepdims=True)
    decay = jnp.exp(jnp.where(lower, gc_col - gc_row, -1e30))
    kb = k * b_col
    kk = _nt(kb, k)
    lmat = jnp.where(strict, kk * decay, 0.0)
    eye_f = eye.astype(f32)
    tinv = eye_f - lmat
    pw = lmat
    for _ in range(5):
        pw = _hdot(pw, pw)
        tinv = tinv + _hdot(tinv, pw)
    egc = jnp.exp(gc_col)
    rhs_u = v * b_col
    rhs_w = kb * egc
    u = _hdot(tinv, rhs_u)
    w = _hdot(tinv, rhs_w)
    qk = _nt(q, k)
    amat = jnp.where(lower, qk * decay, 0.0)
    q_dec = q * egc
    kdf = jnp.exp(gl - gc_col)
    k_dec = k * kdf
    return dict(eye=eye, lower=lower, strict=strict, ii=ii, jj=jj, gc_col=gc_col, b_col=b_col, gl=gl, decay=decay,
                kb=kb, kk=kk, tinv=tinv, egc=egc, rhs_w=rhs_w, u=u, w=w, qk=qk, amat=amat, q_dec=q_dec, kdf=kdf,
                k_dec=k_dec)


def delta_fwd(name, q, k, v, g_rows, b_rows):
    t = q.shape[0]
    nc = t // CHUNK

    def body(q_ref, k_ref, v_ref, g_ref, b_ref, o_ref, s_ref, state):
        n = pl.program_id(1)

        @pl.when(n == 0)
        def _():
            state[...] = jnp.zeros_like(state)

        qv, kv, vv = q_ref[...], k_ref[...], v_ref[...]
        loc = _delta_local(qv, kv, vv, g_ref[pl.ds(n, 1), :], b_ref[pl.ds(n, 1), :])
        s0 = state[...]
        s_ref[...] = s0
        v_new = loc["u"] - _nn(loc["w"], s0)
        o_ref[...] = _nn(loc["q_dec"], s0) + _nn(loc["amat"], v_new)
        state[...] = s0 * jnp.exp(loc["gl"]) + _tn(loc["k_dec"], v_new)

    tok = pl.BlockSpec((CHUNK, HEAD_DIM), lambda h, n: (n, h))
    row = pl.BlockSpec((None, nc, CHUNK), lambda h, n: (h, 0, 0))
    return _call(body, name, (_sds((t, D_MODEL)), _sds((DN_HEADS, nc, HEAD_DIM, HEAD_DIM))), (DN_HEADS, nc),
                 [tok, tok, tok, row, row],
                 (tok, pl.BlockSpec((None, None, HEAD_DIM, HEAD_DIM), lambda h, n: (h, n, 0, 0))),
                 scratch=[pltpu.VMEM((HEAD_DIM, HEAD_DIM), f32)])(q, k, v, g_rows, b_rows)


def delta_bwd(name, q, k, v, g_rows, b_rows, s_all, do):
    t = q.shape[0]
    nc = t // CHUNK
    c = CHUNK

    def body(q_ref, k_ref, v_ref, g_ref, b_ref, s_ref, do_ref, dq_ref, dk_ref, dv_ref, dg_ref, db_ref, dstate):
        step = pl.program_id(1)
        n = nc - 1 - step

        @pl.when(step == 0)
        def _():
            dstate[...] = jnp.zeros_like(dstate)

        qv, kv, vv = q_ref[...], k_ref[...], v_ref[...]
        L = _delta_local(qv, kv, vv, g_ref[pl.ds(n, 1), :], b_ref[pl.ds(n, 1), :])
        eye, lower, strict, ii, jj = L["eye"], L["lower"], L["strict"], L["ii"], L["jj"]
        s0 = s_ref[...]
        dov = do_ref[...]
        ds = dstate[...]
        eg = jnp.exp(L["gl"])
        v_new = L["u"] - _nn(L["w"], s0)
        d_k_dec = _nt(v_new, ds)
        d_v_new = _nn(L["k_dec"], ds) + _tn(L["amat"], dov)
        d_eg = jnp.sum(jnp.sum(ds * s0, axis=1, keepdims=True), axis=0, keepdims=True)
        d_q_dec = _nt(dov, s0)
        d_a = _nt(dov, v_new)
        d_w = -_nt(d_v_new, s0)
        dstate[...] = ds * eg + _tn(L["q_dec"], dov) - _tn(L["w"], d_v_new)
        d_am = jnp.where(lower, d_a * L["decay"], 0.0)
        dq = _nn(d_am, kv) + d_q_dec * L["egc"]
        dk = _tn(d_am, qv) + d_k_dec * L["kdf"]
        e_col = jnp.sum(d_k_dec * L["k_dec"], axis=1, keepdims=True)
        d_gc_col = jnp.sum(d_q_dec * L["q_dec"], axis=1, keepdims=True) - e_col
        d_gl = jnp.sum(e_col, axis=0, keepdims=True) + d_eg * eg
        tinv = L["tinv"]
        d_rhs_u = _hdot(tinv, d_v_new, ((0,), (0,)))
        d_rhs_w = _hdot(tinv, d_w, ((0,), (0,)))
        d_l = -(_hdot(d_rhs_u, L["u"], ((1,), (1,))) + _hdot(d_rhs_w, L["w"], ((1,), (1,))))
        dv_ref[...] = d_rhs_u * L["b_col"]
        d_b_col = jnp.sum(d_rhs_u * vv, axis=1, keepdims=True)
        d_gc_col = d_gc_col + jnp.sum(d_rhs_w * L["rhs_w"], axis=1, keepdims=True)
        d_lm = jnp.where(strict, d_l * L["decay"], 0.0)
        d_kb = d_rhs_w * L["egc"] + _nn(d_lm, kv)
        dk = dk + _tn(d_lm, L["kb"]) + d_kb * L["b_col"]
        d_b_col = d_b_col + jnp.sum(d_kb * kv, axis=1, keepdims=True)
        m = d_am * L["qk"] + d_lm * L["kk"]
        d_gc_col = d_gc_col + jnp.sum(m, axis=1, keepdims=True)
        d_gc_row = (jnp.sum(jnp.where(eye, jnp.broadcast_to(d_gc_col, (c, c)), 0.0), axis=0, keepdims=True)
                    - jnp.sum(m, axis=0, keepdims=True))
        lane = lax.broadcasted_iota(jnp.int32, (1, c), 1)
        d_gc_row = d_gc_row + jnp.where(lane == c - 1, d_gl, 0.0)
        d_gc_tot = jnp.sum(jnp.where(eye, jnp.broadcast_to(d_gc_row, (c, c)), 0.0), axis=1, keepdims=True)
        dg_ref[pl.ds(n, 1), :] = jnp.sum(jnp.where(lower, jnp.broadcast_to(d_gc_tot, (c, c)), 0.0), axis=0,
                                         keepdims=True)
        db_ref[pl.ds(n, 1), :] = jnp.sum(jnp.where(eye, jnp.broadcast_to(d_b_col, (c, c)), 0.0), axis=0,
                                         keepdims=True)
        dq_ref[...] = dq
        dk_ref[...] = dk

    tok = pl.BlockSpec((CHUNK, HEAD_DIM), lambda h, s: (nc - 1 - s, h))
    row = pl.BlockSpec((None, nc, CHUNK), lambda h, s: (h, 0, 0))
    return _call(body, name, (_sds((t, D_MODEL)),) * 3 + (_sds((DN_HEADS, nc, CHUNK)),) * 2, (DN_HEADS, nc),
                 [tok, tok, tok, row, row,
                  pl.BlockSpec((None, None, HEAD_DIM, HEAD_DIM), lambda h, s: (h, nc - 1 - s, 0, 0)), tok],
                 (tok, tok, tok, row, row),
                 scratch=[pltpu.VMEM((HEAD_DIM, HEAD_DIM), f32)])(q, k, v, g_rows, b_rows, s_all, do)


def s5_scan(name, x, lam, *, reverse=False, h_for_dlam=None):
    t = x.shape[0]
    tt = _tile(t, (512, 256))
    nt = t // tt
    w = SG_STATE
    with_dlam = h_for_dlam is not None

    def body(*refs):
        if with_dlam:
            x_ref, lam_ref, h_ref, hprev_ref, o_ref, dl_ref, carry, dl_acc = refs
        else:
            x_ref, lam_ref, o_ref, carry = refs
        i = pl.program_id(1)

        @pl.when(i == 0)
        def _():
            carry[...] = jnp.zeros_like(carry)
            if with_dlam:
                dl_acc[...] = jnp.zeros_like(dl_acc)

        lr = lam_ref[:, 0:w]
        li = lam_ref[:, w:2 * w]

        def step(s, st):
            tpos = (tt - 1 - s) if reverse else s
            row = x_ref[pl.ds(tpos, 1), :]
            hr, hi = st[0], st[1]
            nr = lr * hr - li * hi + row[:, 0:w]
            ni = lr * hi + li * hr + row[:, w:2 * w]
            o_ref[pl.ds(tpos, 1), 0:w] = nr
            o_ref[pl.ds(tpos, 1), w:2 * w] = ni
            if not with_dlam:
                return nr, ni
            hp = jnp.where(tpos > 0, h_ref[pl.ds(jnp.maximum(tpos - 1, 0), 1), :], hprev_ref[7:8, :])
            pr, pi = hp[:, 0:w], hp[:, w:2 * w]
            return nr, ni, st[2] + nr * pr + ni * pi, st[3] + ni * pr - nr * pi

        init = (carry[:, 0:w], carry[:, w:2 * w])
        if with_dlam:
            init = init + (dl_acc[:, 0:w], dl_acc[:, w:2 * w])
        fin = lax.fori_loop(0, tt, step, init)
        carry[:, 0:w] = fin[0]
        carry[:, w:2 * w] = fin[1]
        if with_dlam:
            dl_acc[:, 0:w] = fin[2]
            dl_acc[:, w:2 * w] = fin[3]

            @pl.when(i == nt - 1)
            def _():
                dl_ref[...] = dl_acc[...]

    blk = (lambda s, i: (nt - 1 - i, s)) if reverse else (lambda s, i: (i, s))
    x_spec = pl.BlockSpec((tt, 2 * w), blk)
    lam_spec = pl.BlockSpec((None, 1, 2 * w), lambda s, i: (s, 0, 0))
    if not with_dlam:
        return _call(body, name, _sds(x.shape), (N_SUPER, nt), [x_spec, lam_spec], x_spec,
                     scratch=[pltpu.VMEM((1, 2 * w), f32)])(x, lam)
    hz = jnp.concatenate([jnp.zeros((8, h_for_dlam.shape[1]), f32), h_for_dlam], axis=0)
    hprev_spec = pl.BlockSpec((8, 2 * w), lambda s, i: ((nt - 1 - i) * (tt // 8), s))
    return _call(body, name, (_sds(x.shape), _sds((N_SUPER, 1, 2 * w))), (N_SUPER, nt),
                 [x_spec, lam_spec, x_spec, hprev_spec], (x_spec, lam_spec),
                 scratch=[pltpu.VMEM((1, 2 * w), f32), pltpu.VMEM((1, 2 * w), f32)])(x, lam, h_for_dlam, hz)


def _s5_disc(a_re, a_im, log_dt, br, bi):
    dt = jnp.exp(log_dt)
    mag = jnp.exp(a_re * dt)
    lr, li = mag * jnp.cos(a_im * dt), mag * jnp.sin(a_im * dt)
    den = a_re * a_re + a_im * a_im
    fr = ((lr - 1.0) * a_re + li * a_im) / den
    fi = (li * a_re - (lr - 1.0) * a_im) / den
    return lr, li, fr * br - fi * bi, fr * bi + fi * br


def s5_disc_fwd(name, a_re, a_im, log_dt, br, bi):
    g, n = SSM_GROUPS, SSM_STATE

    def body(ar, ai, ld, brr, bir, lr, li, bbr, bbi):
        o = _s5_disc(ar[...], ai[...], ld[...], brr[...], bir[...])
        lr[...], li[...], bbr[...], bbi[...] = o

    return _call(body, name, (_sds((g, 1, n)), _sds((g, 1, n)), _sds((g, SSM_GROUP, n)), _sds((g, SSM_GROUP, n))))(
        a_re, a_im, log_dt, br, bi)


def s5_disc_bwd(name, a_re, a_im, log_dt, br, bi, d_lr, d_li, d_bbr, d_bbi):
    g, n = SSM_GROUPS, SSM_STATE

    def body(ar, ai, ld, brr, bir, c1, c2, c3, c4, o1, o2, o3, o4, o5):
        _, vjp = jax.vjp(_s5_disc, ar[...], ai[...], ld[...], brr[...], bir[...])
        o1[...], o2[...], o3[...], o4[...], o5[...] = vjp((c1[...], c2[...], c3[...], c4[...]))

    return _call(body, name, (_sds((g, 1, n)), _sds((g, 1, n)), _sds((g, 1, 1)), _sds((g, SSM_GROUP, n)),
                              _sds((g, SSM_GROUP, n))))(a_re, a_im, log_dt, br, bi, d_lr, d_li, d_bbr, d_bbi)


def gelu_fwd(name, ys_lin, u, d_skip):
    t, d = u.shape
    tm = _rows(t)

    def body(y_ref, u_ref, d_ref, o_ref):
        o_ref[...] = _gelu(y_ref[...] + d_ref[...] * u_ref[...])

    return _call(body, name, _sds((t, d)), (t // tm,), [_row_spec(tm, d), _row_spec(tm, d), _acc_spec(1, d)],
                 _row_spec(tm, d))(ys_lin, u, d_skip)


def _head_norm(o, hn):
    outs, ns, rs = [], [], []
    for h in range(DN_HEADS):
        oh = o[:, h * 128:(h + 1) * 128]
        r = lax.rsqrt(jnp.mean(oh * oh, axis=-1, keepdims=True) + EPS)
        n = oh * r
        ns.append(n)
        rs.append(r)
        outs.append(n * hn)
    return outs, ns, rs


def merge_fwd(name, proj, o, yg, glu_lin, head_norm, b_glu):
    t = o.shape[0]
    tm = _rows(t)
    d = D_MODEL

    def body(za_ref, zb_ref, ra_ref, rb_ref, o_ref, yg_ref, gl_ref, hn_ref, bg_ref, m_ref):
        on, _, _ = _head_norm(o_ref[...], hn_ref[...])
        y_a = jnp.concatenate(on, axis=1) * _silu(za_ref[...])
        y_b = yg_ref[...] * _sigmoid(gl_ref[...] + bg_ref[...]) * _silu(zb_ref[...])
        m_ref[...] = _sigmoid(ra_ref[...]) * y_a + _sigmoid(rb_ref[...]) * y_b

    pc = lambda c: pl.BlockSpec((tm, d), lambda i: (i, c))
    return _call(body, name, _sds((t, d)), (t // tm,),
                 [pc(3), pc(5), pc(6), pc(7), _row_spec(tm, d), _row_spec(tm, d), _row_spec(tm, d),
                  _acc_spec(1, 128), _acc_spec(1, d)], _row_spec(tm, d))(
        proj, proj, proj, proj, o, yg, glu_lin, head_norm, b_glu)


def merge_bwd(name, proj, o, yg, glu_lin, head_norm, b_glu, dm):
    t = o.shape[0]
    tm = _rows(t)
    d = D_MODEL

    def body(za_ref, zb_ref, ra_ref, rb_ref, o_ref, yg_ref, gl_ref, hn_ref, bg_ref, dm_ref,
             dza_ref, dzb_ref, dra_ref, drb_ref, do_ref, dgl_ref, dyg_ref, dhn_ref, dbg_ref):
        hn = hn_ref[...]
        za, zb, ra, rb = za_ref[...], zb_ref[...], ra_ref[...], rb_ref[...]
        on, ns, rs = _head_norm(o_ref[...], hn)
        onc = jnp.concatenate(on, axis=1)
        sza = _silu(za)
        y_a = onc * sza
        yg = yg_ref[...]
        sgl = _sigmoid(gl_ref[...] + bg_ref[...])
        y2 = yg * sgl
        szb = _silu(zb)
        y_b = y2 * szb
        sra, srb = _sigmoid(ra), _sigmoid(rb)
        dmv = dm_ref[...]
        dra_ref[...] = dmv * y_a * sra * (1.0 - sra)
        drb_ref[...] = dmv * y_b * srb * (1.0 - srb)
        d_ya = dmv * sra
        d_yb = dmv * srb
        dza_ref[...] = d_ya * onc * _dsilu(za)
        dzb_ref[...] = d_yb * y2 * _dsilu(zb)
        d_on = d_ya * sza
        d_y2 = d_yb * szb
        dyg_ref[...] = d_y2 * sgl
        d_gl = d_y2 * yg * sgl * (1.0 - sgl)
        dgl_ref[...] = d_gl
        _accumulate(dbg_ref, jnp.sum(d_gl, axis=0, keepdims=True))
        d_hn = jnp.zeros((1, 128), f32)
        for h in range(DN_HEADS):
            hs = slice(h * 128, (h + 1) * 128)
            dh = d_on[:, hs]
            d_hn = d_hn + jnp.sum(dh * ns[h], axis=0, keepdims=True)
            dn = dh * hn
            do_ref[:, hs] = rs[h] * (dn - ns[h] * jnp.mean(dn * ns[h], axis=-1, keepdims=True))
        _accumulate(dhn_ref, d_hn)

    pc = lambda c: pl.BlockSpec((tm, d), lambda i: (i, c))
    rs_ = _row_spec(tm, d)
    return _call(body, name, (_sds((t, d)),) * 7 + (_sds((1, 128)), _sds((1, d))), (t // tm,),
                 [pc(3), pc(5), pc(6), pc(7), rs_, rs_, rs_, _acc_spec(1, 128), _acc_spec(1, d), rs_],
                 (rs_,) * 7 + (_acc_spec(1, 128), _acc_spec(1, d)))(
        proj, proj, proj, proj, o, yg, glu_lin, head_norm, b_glu, dm)


def gelu_bwd(name, ys_lin, u, d_skip, dyg_a, dyg_b):
    t, d = u.shape
    tm = _rows(t)

    def body(y_ref, u_ref, d_ref, a_ref, b_ref, dys_ref, du_ref, dd_ref):
        uv = u_ref[...]
        dys = (a_ref[...] + b_ref[...]) * _dgelu(y_ref[...] + d_ref[...] * uv)
        dys_ref[...] = dys
        du_ref[...] = dys * d_ref[...]
        _accumulate(dd_ref, jnp.sum(dys * uv, axis=0, keepdims=True))

    rs_ = _row_spec(tm, d)
    return _call(body, name, (_sds((t, d)), _sds((t, d)), _sds((1, d))), (t // tm,),
                 [rs_, rs_, _acc_spec(1, d), rs_, rs_], (rs_, rs_, _acc_spec(1, d)))(ys_lin, u, d_skip, dyg_a, dyg_b)


def assemble_dproj(name, dqkv, dza, du_a, du_b, dzb, dra, drb, dbd):
    t = dza.shape[0]
    tm = _rows(t)
    d = D_MODEL

    def body(qkv_ref, za_ref, ua_ref, ub_ref, zb_ref, ra_ref, rb_ref, bd_ref, o_ref):
        o_ref[:, 0:3 * d] = qkv_ref[...].astype(bf16)
        o_ref[:, 3 * d:4 * d] = za_ref[...].astype(bf16)
        o_ref[:, 4 * d:5 * d] = (ua_ref[...] + ub_ref[...]).astype(bf16)
        o_ref[:, 5 * d:6 * d] = zb_ref[...].astype(bf16)
        o_ref[:, 6 * d:7 * d] = ra_ref[...].astype(bf16)
        o_ref[:, 7 * d:8 * d] = rb_ref[...].astype(bf16)
        o_ref[:, 8 * d:8 * d + 128] = bd_ref[...].astype(bf16)

    rs_ = _row_spec(tm, d)
    return _call(body, name, _sds((t, W_PAD), bf16), (t // tm,),
                 [_row_spec(tm, 3 * d), rs_, rs_, rs_, rs_, rs_, rs_, _row_spec(tm, 128)], _row_spec(tm, W_PAD))(
        dqkv, dza, du_a, du_b, dzb, dra, drb, dbd)


def adamw(name, w, g, m, v):
    r, c = w.shape
    tm = _tile(r, (512, 256, 128, 64, 32, 16, 8))
    c1 = 1.0 / (1.0 - ADAM_B1 ** ADAM_STEP)
    c2 = 1.0 / (1.0 - ADAM_B2 ** ADAM_STEP)

    def body(w_ref, g_ref, m_ref, v_ref, d_ref, nm_ref, nv_ref):
        gv = g_ref[...]
        nm = ADAM_B1 * m_ref[...] + (1.0 - ADAM_B1) * gv
        nv = ADAM_B2 * v_ref[...] + (1.0 - ADAM_B2) * (gv * gv)
        d_ref[...] = -ADAM_LR * ((nm * c1) / (jnp.sqrt(nv * c2) + ADAM_EPS) + ADAM_WD * w_ref[...])
        nm_ref[...] = nm
        nv_ref[...] = nv

    sp = pl.BlockSpec((tm, c), lambda i: (i, 0))
    return _call(body, name, (_sds((r, c)),) * 3, (r // tm,), [sp] * 4, (sp,) * 3)(w, g, m, v)


def _coords():
    return lax.axis_index("x"), lax.axis_index("y"), lax.axis_index("c")


def _flip(me, k):
    x, y, c = me
    return (1 - x if k & 4 else x, 1 - y if k & 2 else y, 1 - c if k & 1 else c)


def _lin(dev):
    return 4 * dev[0] + 2 * dev[1] + dev[2]


def all_gather(name, shard):
    r, c = shard.shape

    def body(x_ref, o_ref, send_sems, recv_sems, local_sem):
        me = _coords()
        mine = pltpu.make_async_copy(x_ref, o_ref.at[_lin(me)], local_sem)
        mine.start()
        sends = []
        for k in range(1, N_DEV):
            cp = pltpu.make_async_remote_copy(src_ref=x_ref, dst_ref=o_ref.at[_lin(me)], send_sem=send_sems.at[k - 1],
                                              recv_sem=recv_sems.at[k - 1], device_id=_flip(me, k), device_id_type=MESH)
            cp.start()
            sends.append(cp)
        for k in range(1, N_DEV):
            src = _lin(_flip(me, k))
            pltpu.make_async_remote_copy(src_ref=x_ref, dst_ref=o_ref.at[src], send_sem=send_sems.at[k - 1],
                                         recv_sem=recv_sems.at[k - 1], device_id=_flip(me, k),
                                         device_id_type=MESH).wait_recv()
        for cp in sends:
            cp.wait_send()
        mine.wait()

    any_spec = pl.BlockSpec(memory_space=pl.ANY)
    return _call(body, name, _sds((N_DEV, r, c), shard.dtype), in_specs=[any_spec], out_specs=any_spec,
                 scratch=[pltpu.SemaphoreType.DMA((N_DEV - 1,)), pltpu.SemaphoreType.DMA((N_DEV - 1,)),
                          pltpu.SemaphoreType.DMA(())])(shard)


def exchange_blocks(name, blocks):
    _, r, c = blocks.shape

    def body(x_ref, o_ref, send_sems, recv_sems, local_sem):
        me = _coords()
        my = _lin(me)
        mine = pltpu.make_async_copy(x_ref.at[my], o_ref.at[my], local_sem)
        mine.start()
        sends = []
        for k in range(1, N_DEV):
            peer = _flip(me, k)
            cp = pltpu.make_async_remote_copy(src_ref=x_ref.at[_lin(peer)], dst_ref=o_ref.at[my],
                                              send_sem=send_sems.at[k - 1], recv_sem=recv_sems.at[k - 1],
                                              device_id=peer, device_id_type=MESH)
            cp.start()
            sends.append(cp)
        for k in range(1, N_DEV):
            peer = _flip(me, k)
            pltpu.make_async_remote_copy(src_ref=x_ref.at[my], dst_ref=o_ref.at[_lin(peer)],
                                         send_sem=send_sems.at[k - 1], recv_sem=recv_sems.at[k - 1],
                                         device_id=peer, device_id_type=MESH).wait_recv()
        for cp in sends:
            cp.wait_send()
        mine.wait()

    any_spec = pl.BlockSpec(memory_space=pl.ANY)
    return _call(body, name, _sds(blocks.shape, blocks.dtype), in_specs=[any_spec], out_specs=any_spec,
                 scratch=[pltpu.SemaphoreType.DMA((N_DEV - 1,)), pltpu.SemaphoreType.DMA((N_DEV - 1,)),
                          pltpu.SemaphoreType.DMA(())])(blocks)


def sum_slots(name, slots):
    _, r, c = slots.shape
    tm = _tile(r, (240, 256, 128, 64, 32, 16))

    def body(s_ref, o_ref):
        acc = s_ref[0].astype(f32)
        for d in range(1, N_DEV):
            acc = acc + s_ref[d].astype(f32)
        o_ref[...] = acc

    return _call(body, name, _sds((r, c)), (r // tm,), [pl.BlockSpec((N_DEV, tm, c), lambda i: (0, i, 0))],
                 pl.BlockSpec((tm, c), lambda i: (i, 0)))(slots)


def _rows_layout(col8, t):
    return col8.T.reshape(DN_HEADS, t // CHUNK, CHUNK)


def _blockdiag(m):
    g, a, b = m.shape
    m = m.reshape(N_SUPER, SUPER, a, b)
    out = jnp.einsum("sgab,gh->sgahb", m, jnp.eye(SUPER, dtype=m.dtype))
    return out.reshape(N_SUPER, SUPER * a, SUPER * b)


def _diag_blocks(m, a, b):
    m = m.reshape(N_SUPER, SUPER, a, SUPER, b)
    return jnp.einsum("sgahb,gh->sgab", m, jnp.eye(SUPER, dtype=m.dtype)).reshape(SSM_GROUPS, a, b)


def _s5_params(p, li):
    tag = f"l{li}"
    n = SSM_STATE
    a_re = p["ssm_a_re"].reshape(SSM_GROUPS, 1, n)
    a_im = p["ssm_a_im"].reshape(SSM_GROUPS, 1, n)
    log_dt = p["ssm_log_dt"].reshape(SSM_GROUPS, 1, 1)
    br = jnp.swapaxes(p["ssm_b_re"], 1, 2)
    bi = jnp.swapaxes(p["ssm_b_im"], 1, 2)
    lr, li_, bbr, bbi = s5_disc_fwd("s5_disc_" + tag, a_re, a_im, log_dt, br, bi)
    lam = jnp.concatenate([lr.reshape(N_SUPER, 1, SG_STATE), li_.reshape(N_SUPER, 1, SG_STATE)], axis=-1)
    bbd = jnp.concatenate([_blockdiag(bbr), _blockdiag(bbi)], axis=-1).astype(bf16)
    c_re = jnp.swapaxes(p["ssm_c_re"], 1, 2)
    c_im = jnp.swapaxes(p["ssm_c_im"], 1, 2)
    cbd = jnp.concatenate([_blockdiag(c_re), -_blockdiag(c_im)], axis=1).astype(bf16)
    return dict(a_re=a_re, a_im=a_im, log_dt=log_dt, br=br, bi=bi, lam=lam, bbd=bbd, cbd=cbd)


def layer_fwd(x, p, li):
    tag = f"l{li}"
    t = x.shape[0]
    d = D_MODEL
    h = rmsnorm_fwd("norm_pre_" + tag, x, p["norm_pre"])
    proj = mm_nn("proj_" + tag, h, p["w_all"])
    q, k, v, bg = conv_qkv_fwd("conv_" + tag, proj, p["conv_w"], p["a_log"], p["dt_bias"])
    b_rows = _rows_layout(bg[:, 0:DN_HEADS], t)
    g_rows = _rows_layout(bg[:, DN_HEADS:2 * DN_HEADS], t)
    o, s_all = delta_fwd("delta_" + tag, q, k, v, g_rows, b_rows)
    sp = _s5_params(p, li)
    u = proj[:, 4 * d:5 * d]
    xs = sg_in("s5_in_" + tag, u, sp["bbd"])
    hs = s5_scan("s5_scan_" + tag, xs, sp["lam"])
    ys_lin = sg_out("s5_out_" + tag, hs, sp["cbd"])
    yg = gelu_fwd("gelu_" + tag, ys_lin, u, p["ssm_d"])
    glu_lin = mm_nn("glu_" + tag, yg, p["w_glu"])
    merged = merge_fwd("merge_" + tag, proj, o, yg, glu_lin, p["head_norm"], p["b_glu"])
    out = mm_nn("out_" + tag, merged, p["w_out"])
    y = residual_norm_fwd("norm_post_" + tag, x, out, p["norm_post"])
    saved = dict(x=x, h=h, proj=proj, q=q, k=k, v=v, g_rows=g_rows, b_rows=b_rows, o=o, s_all=s_all, sp=sp, u=u,
                 hs=hs, ys_lin=ys_lin, yg=yg, glu_lin=glu_lin, merged=merged, out=out)
    return y, saved


def layer_bwd(dy, p, s, li):
    tag = f"l{li}"
    t = dy.shape[0]
    d = D_MODEL
    sp = s["sp"]
    gr = {}
    d_out, gr["norm_post"] = post_norm_bwd("norm_post_b_" + tag, s["out"], p["norm_post"], dy)
    d_merged = mm_nt("out_b_" + tag, d_out, p["w_out"])
    gr["w_out"] = mm_tn("out_w_" + tag, s["merged"], d_out)
    (dza, dzb, dra, drb, d_o, d_glu, dyg_a, gr["head_norm"], gr["b_glu"]) = merge_bwd(
        "merge_b_" + tag, s["proj"], s["o"], s["yg"], s["glu_lin"], p["head_norm"], p["b_glu"], d_merged)
    dyg_b = mm_nt("glu_b_" + tag, d_glu, p["w_glu"])
    gr["w_glu"] = mm_tn("glu_w_" + tag, s["yg"], d_glu)
    d_ys, du_a, gr["ssm_d"] = gelu_bwd("gelu_b_" + tag, s["ys_lin"], s["u"], p["ssm_d"], dyg_a, dyg_b)
    d_hs = sg_out_t("s5_out_b_" + tag, d_ys, sp["cbd"])
    d_cbd = sg_grad("s5_out_w_" + tag, s["hs"], d_ys, 2 * SG_STATE, 128)
    adj, d_lam = s5_scan("s5_scan_b_" + tag, d_hs, sp["lam"] * jnp.concatenate(
        [jnp.ones((1, 1, SG_STATE), f32), -jnp.ones((1, 1, SG_STATE), f32)], axis=-1), reverse=True,
        h_for_dlam=s["hs"])
    du_b = sg_in_t("s5_in_b_" + tag, adj, sp["bbd"])
    d_bbd = sg_grad("s5_in_w_" + tag, s["u"], adj, 128, 2 * SG_STATE)
    d_c_re = _diag_blocks(d_cbd[:, 0:SG_STATE, :], SSM_STATE, SSM_GROUP)
    d_c_im = -_diag_blocks(d_cbd[:, SG_STATE:, :], SSM_STATE, SSM_GROUP)
    gr["ssm_c_re"] = jnp.swapaxes(d_c_re, 1, 2)
    gr["ssm_c_im"] = jnp.swapaxes(d_c_im, 1, 2)
    d_bbr = _diag_blocks(d_bbd[:, :, 0:SG_STATE], SSM_GROUP, SSM_STATE)
    d_bbi = _diag_blocks(d_bbd[:, :, SG_STATE:], SSM_GROUP, SSM_STATE)
    d_lr = d_lam[:, :, 0:SG_STATE].reshape(SSM_GROUPS, 1, SSM_STATE)
    d_li = d_lam[:, :, SG_STATE:].reshape(SSM_GROUPS, 1, SSM_STATE)
    d_are, d_aim, d_ldt, d_br, d_bi = s5_disc_bwd("s5_disc_b_" + tag, sp["a_re"], sp["a_im"], sp["log_dt"], sp["br"],
                                                  sp["bi"], d_lr, d_li, d_bbr, d_bbi)
    gr["ssm_a_re"] = d_are.reshape(SSM_GROUPS, SSM_STATE)
    gr["ssm_a_im"] = d_aim.reshape(SSM_GROUPS, SSM_STATE)
    gr["ssm_log_dt"] = d_ldt.reshape(SSM_GROUPS)
    gr["ssm_b_re"] = jnp.swapaxes(d_br, 1, 2)
    gr["ssm_b_im"] = jnp.swapaxes(d_bi, 1, 2)
    dq, dk, dv, dg_rows, db_rows = delta_bwd("delta_b_" + tag, s["q"], s["k"], s["v"], s["g_rows"], s["b_rows"],
                                             s["s_all"], d_o)
    dbg = jnp.concatenate([db_rows.reshape(DN_HEADS, t).T, dg_rows.reshape(DN_HEADS, t).T,
                           jnp.zeros((t, 128 - 2 * DN_HEADS), f32)], axis=1)
    dc, gr["conv_w"], dbd, dab = conv_qkv_bwd("conv_b_" + tag, s["proj"], p["conv_w"], p["a_log"], p["dt_bias"],
                                              dq, dk, dv, dbg)
    gr["a_log"] = dab[0, DN_HEADS:2 * DN_HEADS]
    gr["dt_bias"] = dab[1, DN_HEADS:2 * DN_HEADS]
    dqkv = conv_transpose("conv_t_" + tag, dc, p["conv_w"])
    dproj = assemble_dproj("dproj_" + tag, dqkv, dza, du_a, du_b, dzb, dra, drb, dbd)
    d_h = mm_nt("proj_b_" + tag, dproj, p["w_all"])
    gr["w_all"] = mm_tn("proj_w_" + tag, s["h"], dproj)
    dx, gr["norm_pre"] = rmsnorm_bwd_from_h("norm_pre_b_" + tag, s["x"], p["norm_pre"], d_h, dy)
    return dx, gr


def rmsnorm_bwd_from_h(name, x, gain, d_h, dres):
    return rmsnorm_bwd(name, x, gain, d_h, dres)


REPL = ["norm_pre", "a_log", "dt_bias", "head_norm", "ssm_a_re", "ssm_a_im", "ssm_log_dt", "ssm_b_re", "ssm_b_im",
        "ssm_c_re", "ssm_c_im", "ssm_d", "b_glu", "norm_post"]
SHARDED = ["w_in", "conv_w", "w_glu", "w_out"]
ALL_W = ["norm_pre", "w_in", "conv_w", "a_log", "dt_bias", "head_norm", "ssm_a_re", "ssm_a_im", "ssm_log_dt",
         "ssm_b_re", "ssm_b_im", "ssm_c_re", "ssm_c_im", "ssm_d", "w_glu", "b_glu", "w_out", "norm_post"]
PACK_W = 1024


def _pack_rows(arrs, mult):
    flat = jnp.concatenate([a.reshape(-1) for a in arrs])
    n = flat.shape[0]
    rows = -(-n // PACK_W)
    rows = -(-rows // mult) * mult
    return jnp.pad(flat, (0, rows * PACK_W - n)).reshape(rows, PACK_W)


def _unpack(flat, shapes):
    out, off = [], 0
    for sh in shapes:
        n = math.prod(sh)
        out.append(flat[off:off + n].reshape(sh))
        off += n
    return out


def _permute_w(w):
    return jnp.concatenate([w[:, 0:4096], w[:, 4112:W_COLS], w[:, 4096:4112],
                            jnp.zeros((w.shape[0], W_PAD - W_COLS), w.dtype)], axis=1)


def _unpermute_w(g):
    return jnp.concatenate([g[:, 0:4096], g[:, 8192:8208], g[:, 4096:8192]], axis=1)


def kernel(x, norm_pre, w_in, conv_w, a_log, dt_bias, head_norm, ssm_a_re, ssm_a_im, ssm_log_dt, ssm_b_re, ssm_b_im, ssm_c_re, ssm_c_im, ssm_d, w_glu, b_glu, w_out, norm_post, loss_target, m_norm_pre, m_w_in, m_conv_w, m_a_log, m_dt_bias, m_head_norm, m_ssm_a_re, m_ssm_a_im, m_ssm_log_dt, m_ssm_b_re, m_ssm_b_im, m_ssm_c_re, m_ssm_c_im, m_ssm_d, m_w_glu, m_b_glu, m_w_out, m_norm_post, v_norm_pre, v_w_in, v_conv_w, v_a_log, v_dt_bias, v_head_norm, v_ssm_a_re, v_ssm_a_im, v_ssm_log_dt, v_ssm_b_re, v_ssm_b_im, v_ssm_c_re, v_ssm_c_im, v_ssm_d, v_w_glu, v_b_glu, v_w_out, v_norm_post):
    loc = dict(locals())
    w = {n: loc[n] for n in ALL_W}
    m = {n: loc["m_" + n] for n in ALL_W}
    v = {n: loc["v_" + n] for n in ALL_W}
    depth = w_in.shape[0]
    wc = w_in.shape[2]
    cc = conv_w.shape[2]
    wr = w_glu.shape[1]

    conv_hi = conv_w.astype(bf16)
    conv_mid = (conv_w - conv_hi.astype(f32)).astype(bf16)
    conv_lo = (conv_w - conv_hi.astype(f32) - conv_mid.astype(f32)).astype(bf16)
    shard = _pack_rows([w_in.astype(bf16), w_glu.astype(bf16), w_out.astype(bf16), conv_hi, conv_mid, conv_lo], 16)
    gathered = all_gather("gather_weights", shard).reshape(N_DEV, -1)
    parts = [_unpack(gathered[dv], [(depth, D_MODEL, wc), (depth, wr, D_MODEL), (depth, wr, D_MODEL),
                                    (3, depth, CONV_K, cc)]) for dv in range(N_DEV)]
    w_in_full = jnp.concatenate([pt[0] for pt in parts], axis=2)
    w_glu_full = jnp.concatenate([pt[1] for pt in parts], axis=1)
    w_out_full = jnp.concatenate([pt[2] for pt in parts], axis=1)
    conv_full = jnp.concatenate([pt[3][0].astype(f32) + pt[3][1].astype(f32) + pt[3][2].astype(f32)
                                 for pt in parts], axis=2)

    def layer_params(li):
        return dict(norm_pre=norm_pre[li].reshape(1, -1), w_all=_permute_w(w_in_full[li]), conv_w=conv_full[li],
                    a_log=jnp.pad(a_log[li], (DN_HEADS, 128 - 2 * DN_HEADS)).reshape(1, 128),
                    dt_bias=jnp.pad(dt_bias[li], (DN_HEADS, 128 - 2 * DN_HEADS)).reshape(1, 128),
                    head_norm=head_norm[li].reshape(1, -1), ssm_a_re=ssm_a_re[li], ssm_a_im=ssm_a_im[li],
                    ssm_log_dt=ssm_log_dt[li], ssm_b_re=ssm_b_re[li], ssm_b_im=ssm_b_im[li], ssm_c_re=ssm_c_re[li],
                    ssm_c_im=ssm_c_im[li], ssm_d=ssm_d[li].reshape(1, -1), w_glu=w_glu_full[li],
                    b_glu=b_glu[li].reshape(1, -1), w_out=w_out_full[li], norm_post=norm_post[li].reshape(1, -1))

    act = x[0]
    saved, params = [], []
    for li in range(depth):
        params.append(layer_params(li))
        act, sv = layer_fwd(act, params[li], li)
        saved.append(sv)
    loss_part, dy = loss_head("loss_head", act, loss_target[0])
    grads = [None] * depth
    for li in reversed(range(depth)):
        dy, grads[li] = layer_bwd(dy, params[li], saved[li], li)
    grad_x = dy[None]
    loss = lax.psum(loss_part[0, 0], ("x", "y", "c"))

    def stack(name):
        return jnp.stack([grads[li][name] for li in range(depth)])

    g_w_in = jnp.stack([_unpermute_w(grads[li]["w_all"]) for li in range(depth)])
    g_w_in = g_w_in.reshape(depth, D_MODEL, N_DEV, wc)
    g_conv = stack("conv_w").reshape(depth, CONV_K, N_DEV, cc)
    g_glu = stack("w_glu").reshape(depth, N_DEV, wr, D_MODEL)
    g_out = stack("w_out").reshape(depth, N_DEV, wr, D_MODEL)
    repl_shapes = [w[n].shape for n in REPL]
    g_repl = _pack_rows([stack(n).reshape(w[n].shape) for n in REPL], 8 * N_DEV)
    rr = g_repl.shape[0] // N_DEV
    blocks = jnp.stack([_pack_rows([g_w_in[:, :, dv, :], g_glu[:, dv], g_out[:, dv], g_conv[:, :, dv, :],
                                    g_repl[dv * rr:(dv + 1) * rr]], 16) for dv in range(N_DEV)]).astype(bf16)
    slots = exchange_blocks("scatter_grads", blocks)
    mine = sum_slots("sum_grads", slots)
    flat = mine.reshape(-1)
    gs_w_in, gs_glu, gs_out, gs_conv, gs_repl = _unpack(
        flat, [(depth, D_MODEL, wc), (depth, wr, D_MODEL), (depth, wr, D_MODEL), (depth, CONV_K, cc), (rr, PACK_W)])
    g_repl_full = all_gather("gather_repl_grads", gs_repl).reshape(-1)
    g = dict(zip(REPL, _unpack(g_repl_full, repl_shapes)))
    g.update(w_in=gs_w_in, w_glu=gs_glu, w_out=gs_out, conv_w=gs_conv)

    delta, new_m, new_v = {}, {}, {}
    for n in SHARDED:
        sh = w[n].shape
        two_d = (-1, sh[-1])
        delta[n], new_m[n], new_v[n] = [o.reshape(sh) for o in adamw(
            "adamw_" + n, w[n].reshape(two_d), g[n].reshape(two_d), m[n].reshape(two_d), v[n].reshape(two_d))]
    packed = [_pack_rows([src[n] for n in REPL], 8) for src in (w, g, m, v)]
    outs = adamw("adamw_repl", *packed)
    for dst, o in zip((delta, new_m, new_v), outs):
        dst.update(zip(REPL, _unpack(o.reshape(-1), repl_shapes)))
    return (loss, grad_x, *[g[n] for n in ALL_W], *[delta[n] for n in ALL_W], *[new_m[n] for n in ALL_W],
            *[new_v[n] for n in ALL_W])
```

```python
import math

import jax
import jax.numpy as jnp
from jax import lax
from jax.experimental import pallas as pl
from jax.experimental.pallas import tpu as pltpu

f32 = jnp.float32
bf16 = jnp.bfloat16

D_MODEL = 1024
DEPTH = 4
N_DEV = 8
DN_HEADS = 8
HEAD_DIM = 128
CHUNK = 64
CONV_K = 4
SSM_GROUPS = 64
SSM_GROUP = 16
SSM_STATE = 64
SUPER = 8
N_SUPER = SSM_GROUPS // SUPER
SG_STATE = SUPER * SSM_STATE
EPS = 1e-6
W_COLS = 8208
W_PAD = 8320
ADAM_LR, ADAM_B1, ADAM_B2, ADAM_EPS, ADAM_WD, ADAM_STEP = 0.001, 0.9, 0.999, 1e-08, 0.01, 10
VMEM_LIMIT = 56 * 1024 * 1024
MESH = pl.DeviceIdType.MESH
HIGH = lax.Precision.HIGH
DELTA_HB = 8


def _call(body, name, out_shape, grid=None, in_specs=None, out_specs=None, scratch=(), **kw):
    args = dict(out_shape=out_shape, name=name, scratch_shapes=list(scratch),
                compiler_params=pltpu.CompilerParams(vmem_limit_bytes=VMEM_LIMIT, **kw))
    if grid is not None:
        args.update(grid=grid, in_specs=in_specs, out_specs=out_specs)
    else:
        if in_specs is not None:
            args.update(in_specs=in_specs)
        if out_specs is not None:
            args.update(out_specs=out_specs)
    return pl.pallas_call(body, **args)


def _sds(shape, dtype=f32):
    return jax.ShapeDtypeStruct(tuple(shape), dtype)


def _sigmoid(x):
    return 1.0 / (1.0 + jnp.exp(-x))


def _silu(x):
    return x * _sigmoid(x)


def _dsilu(x):
    s = _sigmoid(x)
    return s * (1.0 + x * (1.0 - s))


_GELU_C = math.sqrt(2.0 / math.pi)


def _gelu(x):
    return 0.5 * x * (1.0 + jnp.tanh(_GELU_C * (x + 0.044715 * x * x * x)))


def _dgelu(x):
    t = jnp.tanh(_GELU_C * (x + 0.044715 * x * x * x))
    return 0.5 * (1.0 + t) + 0.5 * x * (1.0 - t * t) * _GELU_C * (1.0 + 3 * 0.044715 * x * x)


def _softplus(x):
    return jnp.maximum(x, 0.0) + jnp.log(1.0 + jnp.exp(-jnp.abs(x)))


def _bdot(a, b, dn):
    return lax.dot_general(a.astype(bf16), b.astype(bf16), (dn, ((), ())), preferred_element_type=f32)


def _matmul(name, a, b, *, dn, grid, a_spec, b_spec, o_spec, o_shape, o_dtype=f32):
    nk = grid[-1]

    def body(a_ref, b_ref, o_ref, acc_ref):
        p = _bdot(a_ref[...], b_ref[...], dn)
        if nk == 1:
            o_ref[...] = p.astype(o_dtype)
        else:
            k = pl.program_id(len(grid) - 1)

            @pl.when(k == 0)
            def _():
                acc_ref[...] = p

            @pl.when(k > 0)
            def _():
                acc_ref[...] += p

            @pl.when(k == nk - 1)
            def _():
                o_ref[...] = acc_ref[...].astype(o_dtype)

    blk = tuple(d for d in o_spec.block_shape if d is not None)
    return _call(body, name, _sds(o_shape, o_dtype), grid, [a_spec, b_spec], o_spec,
                 scratch=[pltpu.VMEM(blk if nk > 1 else (8, 128), f32)])(a, b)


def _tile(n, pref):
    for t in pref:
        if n % t == 0:
            return t
    return n


def mm_nn(name, a, b):
    m, k = a.shape
    n = b.shape[1]
    tm, tn, tk = _tile(m, (512, 256)), _tile(n, (1664, 1024, 512)), _tile(k, (1664, 1024))
    return _matmul(name, a, b, dn=((1,), (0,)), grid=(m // tm, n // tn, k // tk),
                   a_spec=pl.BlockSpec((tm, tk), lambda i, j, l: (i, l)),
                   b_spec=pl.BlockSpec((tk, tn), lambda i, j, l: (l, j)),
                   o_spec=pl.BlockSpec((tm, tn), lambda i, j, l: (i, j)), o_shape=(m, n))


def mm_nt(name, a, b):
    m, k = a.shape
    n = b.shape[0]
    tm, tn, tk = _tile(m, (512, 256)), _tile(n, (1024, 512)), _tile(k, (1664, 1024))
    return _matmul(name, a, b, dn=((1,), (1,)), grid=(m // tm, n // tn, k // tk),
                   a_spec=pl.BlockSpec((tm, tk), lambda i, j, l: (i, l)),
                   b_spec=pl.BlockSpec((tn, tk), lambda i, j, l: (j, l)),
                   o_spec=pl.BlockSpec((tm, tn), lambda i, j, l: (i, j)), o_shape=(m, n))


def mm_tn(name, a, b, o_dtype=f32):
    k, m = a.shape
    n = b.shape[1]
    tm, tn, tk = _tile(m, (512,)), _tile(n, (1664, 1024, 512)), _tile(k, (512, 256))
    return _matmul(name, a, b, dn=((0,), (0,)), grid=(m // tm, n // tn, k // tk),
                   a_spec=pl.BlockSpec((tk, tm), lambda i, j, l: (l, i)),
                   b_spec=pl.BlockSpec((tk, tn), lambda i, j, l: (l, j)),
                   o_spec=pl.BlockSpec((tm, tn), lambda i, j, l: (i, j)), o_shape=(m, n), o_dtype=o_dtype)


def sg_in(name, u, bbd, col0=0):
    t = u.shape[0]
    tm = _tile(t, (512, 256))
    return _matmul(name, u, bbd, dn=((1,), (0,)), grid=(t // tm, N_SUPER, 1),
                   a_spec=pl.BlockSpec((tm, 128), lambda i, s, l: (i, col0 + s)),
                   b_spec=pl.BlockSpec((None, 128, 1024), lambda i, s, l: (s, 0, 0)),
                   o_spec=pl.BlockSpec((tm, 1024), lambda i, s, l: (i, s)), o_shape=(t, 8192))


def sg_out(name, h, cbd):
    t = h.shape[0]
    tm = _tile(t, (512, 256))
    return _matmul(name, h, cbd, dn=((1,), (0,)), grid=(t // tm, N_SUPER, 1),
                   a_spec=pl.BlockSpec((tm, 1024), lambda i, s, l: (i, s)),
                   b_spec=pl.BlockSpec((None, 1024, 128), lambda i, s, l: (s, 0, 0)),
                   o_spec=pl.BlockSpec((tm, 128), lambda i, s, l: (i, s)), o_shape=(t, 1024))


def sg_in_t(name, a, bbd):
    t = a.shape[0]
    tm = _tile(t, (512, 256))
    return _matmul(name, a, bbd, dn=((1,), (1,)), grid=(t // tm, N_SUPER, 1),
                   a_spec=pl.BlockSpec((tm, 1024), lambda i, s, l: (i, s)),
                   b_spec=pl.BlockSpec((None, 128, 1024), lambda i, s, l: (s, 0, 0)),
                   o_spec=pl.BlockSpec((tm, 128), lambda i, s, l: (i, s)), o_shape=(t, 1024))


def sg_out_t(name, dy, cbd):
    t = dy.shape[0]
    tm = _tile(t, (512, 256))
    return _matmul(name, dy, cbd, dn=((1,), (1,)), grid=(t // tm, N_SUPER, 1),
                   a_spec=pl.BlockSpec((tm, 128), lambda i, s, l: (i, s)),
                   b_spec=pl.BlockSpec((None, 1024, 128), lambda i, s, l: (s, 0, 0)),
                   o_spec=pl.BlockSpec((tm, 1024), lambda i, s, l: (i, s)), o_shape=(t, 8192))


def sg_grad(name, a, b, wa, wb, a_col0=0):
    t = a.shape[0]
    tk = _tile(t, (512, 256))
    return _matmul(name, a, b, dn=((0,), (0,)), grid=(N_SUPER, t // tk),
                   a_spec=pl.BlockSpec((tk, wa), lambda s, l: (l, a_col0 + s)),
                   b_spec=pl.BlockSpec((tk, wb), lambda s, l: (l, s)),
                   o_spec=pl.BlockSpec((None, wa, wb), lambda s, l: (s, 0, 0)), o_shape=(N_SUPER, wa, wb))


def _rows(t):
    return _tile(t, (256,))


def _row_spec(tm, w):
    return pl.BlockSpec((tm, w), lambda i: (i, 0))


def _acc_spec(r, w):
    return pl.BlockSpec((r, w), lambda i: (0, 0))


def _accumulate(ref, val):
    @pl.when(pl.program_id(0) == 0)
    def _():
        ref[...] = val

    @pl.when(pl.program_id(0) > 0)
    def _():
        ref[...] += val


def rmsnorm_fwd(name, x, gain):
    t, d = x.shape
    tm = _rows(t)

    def body(x_ref, g_ref, o_ref):
        xv = x_ref[...]
        r = lax.rsqrt(jnp.mean(xv * xv, axis=-1, keepdims=True) + EPS)
        o_ref[...] = xv * r * g_ref[...]

    return _call(body, name, _sds((t, d)), (t // tm,), [_row_spec(tm, d), _acc_spec(1, d)], _row_spec(tm, d))(x, gain)


def rmsnorm_bwd(name, x, gain, dn, dres):
    t, d = x.shape
    tm = _rows(t)

    def body(x_ref, g_ref, dn_ref, dr_ref, dx_ref, dg_ref):
        xv = x_ref[...]
        r = lax.rsqrt(jnp.mean(xv * xv, axis=-1, keepdims=True) + EPS)
        n = xv * r
        dnv = dn_ref[...]
        _accumulate(dg_ref, jnp.sum(dnv * n, axis=0, keepdims=True))
        dng = dnv * g_ref[...]
        dx_ref[...] = dr_ref[...] + r * (dng - n * jnp.mean(dng * n, axis=-1, keepdims=True))

    return _call(body, name, (_sds((t, d)), _sds((1, d))), (t // tm,),
                 [_row_spec(tm, d), _acc_spec(1, d), _row_spec(tm, d), _row_spec(tm, d)],
                 (_row_spec(tm, d), _acc_spec(1, d)))(x, gain, dn, dres)


def residual_norm_fwd(name, x, out, gain):
    t, d = x.shape
    tm = _rows(t)

    def body(x_ref, o_ref, g_ref, y_ref):
        ov = o_ref[...]
        r = lax.rsqrt(jnp.mean(ov * ov, axis=-1, keepdims=True) + EPS)
        y_ref[...] = x_ref[...] + ov * r * g_ref[...]

    return _call(body, name, _sds((t, d)), (t // tm,), [_row_spec(tm, d), _row_spec(tm, d), _acc_spec(1, d)],
                 _row_spec(tm, d))(x, out, gain)


def post_norm_bwd(name, out, gain, dy):
    t, d = out.shape
    tm = _rows(t)

    def body(o_ref, g_ref, dy_ref, do_ref, dg_ref):
        ov = o_ref[...]
        r = lax.rsqrt(jnp.mean(ov * ov, axis=-1, keepdims=True) + EPS)
        n = ov * r
        dyv = dy_ref[...]
        _accumulate(dg_ref, jnp.sum(dyv * n, axis=0, keepdims=True))
        dng = dyv * g_ref[...]
        do_ref[...] = r * (dng - n * jnp.mean(dng * n, axis=-1, keepdims=True))

    return _call(body, name, (_sds((t, d)), _sds((1, d))), (t // tm,),
                 [_row_spec(tm, d), _acc_spec(1, d), _row_spec(tm, d)],
                 (_row_spec(tm, d), _acc_spec(1, d)))(out, gain, dy)


def loss_head(name, y, target):
    t, d = y.shape
    tm = _rows(t)

    def body(y_ref, t_ref, l_ref, dy_ref):
        e = y_ref[...] - t_ref[...]
        dy_ref[...] = e * (1.0 / d)
        s = jnp.sum(jnp.sum(e * e, axis=1, keepdims=True), axis=0, keepdims=True) * (0.5 / d)
        _accumulate(l_ref, s)

    return _call(body, name, (_sds((1, 1)), _sds((t, d))), (t // tm,),
                 [_row_spec(tm, d), _row_spec(tm, d)], (_acc_spec(1, 1), _row_spec(tm, d)))(y, target)


def _prev_spec(tm, w):
    return pl.BlockSpec((8, w), lambda i: (jnp.maximum(i * (tm // 8) - 1, 0), 0))


def _next_spec(tm, w, t):
    return pl.BlockSpec((8, w), lambda i: (jnp.minimum((i + 1) * (tm // 8), t // 8 - 1), 0))


def _fill_pad(pad_ref, prev_ref, cur_ref, tm):
    keep = (pl.program_id(0) > 0).astype(f32)
    pad_ref[0:8, :] = prev_ref[...] * keep
    pad_ref[8:8 + tm, :] = cur_ref[...]


def _conv_block(pad_ref, w_ref, cb, tm):
    cs = slice(cb * 128, (cb + 1) * 128)
    acc = pad_ref[pl.ds(8 - (CONV_K - 1), tm), cs] * w_ref[0:1, cs]
    for j in range(1, CONV_K):
        acc = acc + pad_ref[pl.ds(8 - (CONV_K - 1) + j, tm), cs] * w_ref[j:j + 1, cs]
    return acc


def conv_qkv_fwd(name, proj, conv_w, a_log, dt_bias):
    t = proj.shape[0]
    tm = _rows(t)
    scale = HEAD_DIM ** -0.5

    def body(cur_ref, prev_ref, w_ref, bd_ref, al_ref, db_ref, q_ref, k_ref, v_ref, bg_ref, pad_ref):
        _fill_pad(pad_ref, prev_ref, cur_ref, tm)
        for cb in range(3 * DN_HEADS):
            s = _silu(_conv_block(pad_ref, w_ref, cb, tm))
            hs = slice((cb % DN_HEADS) * 128, (cb % DN_HEADS + 1) * 128)
            if cb < DN_HEADS:
                q_ref[:, hs] = s * (lax.rsqrt(jnp.sum(s * s, axis=-1, keepdims=True) + EPS) * scale)
            elif cb < 2 * DN_HEADS:
                k_ref[:, hs] = s * lax.rsqrt(jnp.sum(s * s, axis=-1, keepdims=True) + EPS)
            else:
                v_ref[:, hs] = s
        bd = bd_ref[...]
        beta = _sigmoid(bd)
        g = -jnp.exp(al_ref[...]) * _softplus(bd + db_ref[...])
        lane = lax.broadcasted_iota(jnp.int32, bd.shape, 1)
        bg_ref[...] = jnp.where(lane < DN_HEADS, beta, jnp.where(lane < 2 * DN_HEADS, g, 0.0))

    w3 = 3 * D_MODEL
    return _call(body, name, (_sds((t, D_MODEL)),) * 3 + (_sds((t, 128)),), (t // tm,),
                 [pl.BlockSpec((tm, w3), lambda i: (i, 0)), _prev_spec(tm, w3), _acc_spec(CONV_K, w3),
                  pl.BlockSpec((tm, 128), lambda i: (i, 8192 // 128)), _acc_spec(1, 128), _acc_spec(1, 128)],
                 (_row_spec(tm, D_MODEL),) * 3 + (_row_spec(tm, 128),),
                 scratch=[pltpu.VMEM((tm + 8, w3), f32)])(proj, proj, conv_w, proj, a_log, dt_bias)


def conv_qkv_bwd(name, proj, conv_w, a_log, dt_bias, dq, dk, dv, dbg):
    t = proj.shape[0]
    tm = _rows(t)
    scale = HEAD_DIM ** -0.5

    def body(cur_ref, prev_ref, w_ref, bd_ref, al_ref, db_ref, dq_ref, dk_ref, dv_ref, dbg_ref,
             dc_ref, dw_ref, dbd_ref, dab_ref, pad_ref):
        _fill_pad(pad_ref, prev_ref, cur_ref, tm)

        @pl.when(pl.program_id(0) == 0)
        def _():
            dw_ref[...] = jnp.zeros_like(dw_ref)

        for cb in range(3 * DN_HEADS):
            cs = slice(cb * 128, (cb + 1) * 128)
            hs = slice((cb % DN_HEADS) * 128, (cb % DN_HEADS + 1) * 128)
            c = _conv_block(pad_ref, w_ref, cb, tm)
            s = _silu(c)
            if cb < 2 * DN_HEADS:
                dn = (dq_ref[:, hs] * scale) if cb < DN_HEADS else dk_ref[:, hs]
                r = lax.rsqrt(jnp.sum(s * s, axis=-1, keepdims=True) + EPS)
                ds = r * dn - s * (r * r * r) * jnp.sum(dn * s, axis=-1, keepdims=True)
            else:
                ds = dv_ref[:, hs]
            dc = ds * _dsilu(c)
            dc_ref[:, cs] = dc
            for j in range(CONV_K):
                dw_ref[j:j + 1, cs] += jnp.sum(dc * pad_ref[pl.ds(8 - (CONV_K - 1) + j, tm), cs], axis=0,
                                               keepdims=True)
        bd = bd_ref[...]
        dbg_v = dbg_ref[...]
        lane = lax.broadcasted_iota(jnp.int32, bd.shape, 1)
        sg = _sigmoid(bd)
        ea = jnp.exp(al_ref[...])
        z = bd + db_ref[...]
        sp = _softplus(z)
        is_b = lane < DN_HEADS
        is_g = jnp.logical_and(lane >= DN_HEADS, lane < 2 * DN_HEADS)
        d_z = jnp.where(is_g, dbg_v * (-ea) * _sigmoid(z), 0.0)
        dbd_ref[...] = jnp.where(is_b, dbg_v * sg * (1.0 - sg), d_z)
        d_al = jnp.sum(jnp.where(is_g, dbg_v * (-ea) * sp, 0.0), axis=0, keepdims=True)
        d_db = jnp.sum(d_z, axis=0, keepdims=True)
        _accumulate(dab_ref, jnp.concatenate([d_al, d_db] + [jnp.zeros_like(d_al)] * 6, axis=0))

    w3 = 3 * D_MODEL
    return _call(body, name, (_sds((t, w3)), _sds((CONV_K, w3)), _sds((t, 128)), _sds((8, 128))), (t // tm,),
                 [pl.BlockSpec((tm, w3), lambda i: (i, 0)), _prev_spec(tm, w3), _acc_spec(CONV_K, w3),
                  pl.BlockSpec((tm, 128), lambda i: (i, 8192 // 128)), _acc_spec(1, 128), _acc_spec(1, 128),
                  _row_spec(tm, D_MODEL), _row_spec(tm, D_MODEL), _row_spec(tm, D_MODEL), _row_spec(tm, 128)],
                 (_row_spec(tm, w3), _acc_spec(CONV_K, w3), _row_spec(tm, 128), _acc_spec(8, 128)),
                 scratch=[pltpu.VMEM((tm + 8, w3), f32)])(proj, proj, conv_w, proj, a_log, dt_bias, dq, dk, dv, dbg)


def conv_transpose(name, dc, conv_w):
    t, w3 = dc.shape
    tm = _rows(t)
    nt = t // tm

    def body(cur_ref, nxt_ref, w_ref, o_ref, pad_ref):
        keep = (pl.program_id(0) < nt - 1).astype(f32)
        pad_ref[0:tm, :] = cur_ref[...]
        pad_ref[tm:tm + 8, :] = nxt_ref[...] * keep
        for cb in range(w3 // 128):
            cs = slice(cb * 128, (cb + 1) * 128)
            acc = pad_ref[pl.ds(CONV_K - 1, tm), cs] * w_ref[0:1, cs]
            for j in range(1, CONV_K):
                acc = acc + pad_ref[pl.ds(CONV_K - 1 - j, tm), cs] * w_ref[j:j + 1, cs]
            o_ref[:, cs] = acc

    return _call(body, name, _sds((t, w3)), (nt,),
                 [_row_spec(tm, w3), _next_spec(tm, w3, t), _acc_spec(CONV_K, w3)], _row_spec(tm, w3),
                 scratch=[pltpu.VMEM((tm + 8, w3), f32)])(dc, dc, conv_w)


def _bdg(a, b, ca, cb, prec=None):
    if prec is None:
        a, b = a.astype(bf16), b.astype(bf16)
    return lax.dot_general(a, b, (((ca,), (cb,)), ((0,), (0,))), precision=prec, preferred_element_type=f32)


def _bnn(a, b, prec=None):
    return _bdg(a, b, 2, 1, prec)


def _bnt(a, b, prec=None):
    return _bdg(a, b, 2, 2, prec)


def _btn(a, b, prec=None):
    return _bdg(a, b, 1, 1, prec)


def _delta_local(q, k, v, g_row, b_row):
    c = CHUNK
    ii = lax.broadcasted_iota(jnp.int32, (c, c), 0)
    jj = lax.broadcasted_iota(jnp.int32, (c, c), 1)
    eye, lower, strict = ii == jj, ii >= jj, ii > jj
    shp = (q.shape[0], c, c)
    g_b = jnp.broadcast_to(g_row, shp)
    gc_col = jnp.sum(jnp.where(lower, g_b, 0.0), axis=2, keepdims=True)
    gc_row = jnp.sum(jnp.where(eye, jnp.broadcast_to(gc_col, shp), 0.0), axis=1, keepdims=True)
    b_col = jnp.sum(jnp.where(eye, jnp.broadcast_to(b_row, shp), 0.0), axis=2, keepdims=True)
    gl = jnp.sum(g_row, axis=2, keepdims=True)
    decay = jnp.exp(jnp.where(lower, gc_col - gc_row, -1e30))
    kb = k * b_col
    kk = _bnt(kb, k)
    lmat = jnp.where(strict, kk * decay, 0.0)
    tinv = eye.astype(f32) - lmat
    pw = lmat
    for _ in range(5):
        pw = _bnn(pw, pw, HIGH)
        tinv = tinv + _bnn(tinv, pw, HIGH)
    egc = jnp.exp(gc_col)
    rhs_w = kb * egc
    u = _bnn(tinv, v * b_col, HIGH)
    w = _bnn(tinv, rhs_w, HIGH)
    qk = _bnt(q, k)
    amat = jnp.where(lower, qk * decay, 0.0)
    q_dec = q * egc
    kdf = jnp.exp(gl - gc_col)
    k_dec = k * kdf
    return dict(eye=eye, lower=lower, strict=strict, gc_col=gc_col, b_col=b_col, gl=gl, decay=decay,
                kb=kb, kk=kk, tinv=tinv, egc=egc, rhs_w=rhs_w, u=u, w=w, qk=qk, amat=amat, q_dec=q_dec, kdf=kdf,
                k_dec=k_dec)


def _heads(ref, hb):
    return jnp.stack([ref[:, h * HEAD_DIM:(h + 1) * HEAD_DIM] for h in range(hb)])


def _put_heads(ref, val):
    for h in range(val.shape[0]):
        ref[:, h * HEAD_DIM:(h + 1) * HEAD_DIM] = val[h]


def delta_fwd(name, q, k, v, g_rows, b_rows):
    t = q.shape[0]
    nc = t // CHUNK
    hb = DELTA_HB

    def body(q_ref, k_ref, v_ref, g_ref, b_ref, o_ref, s_ref, state):
        n = pl.program_id(1)

        @pl.when(n == 0)
        def _():
            state[...] = jnp.zeros_like(state)

        loc = _delta_local(_heads(q_ref, hb), _heads(k_ref, hb), _heads(v_ref, hb), g_ref[:, pl.ds(n, 1), :],
                           b_ref[:, pl.ds(n, 1), :])
        s0 = state[...]
        s_ref[...] = s0
        v_new = loc["u"] - _bnn(loc["w"], s0)
        _put_heads(o_ref, _bnn(loc["q_dec"], s0) + _bnn(loc["amat"], v_new))
        state[...] = s0 * jnp.exp(loc["gl"]) + _btn(loc["k_dec"], v_new)

    tok = pl.BlockSpec((CHUNK, hb * HEAD_DIM), lambda h, n: (n, h))
    row = pl.BlockSpec((hb, nc, CHUNK), lambda h, n: (h, 0, 0))
    return _call(body, name, (_sds((t, D_MODEL)), _sds((DN_HEADS, nc, HEAD_DIM, HEAD_DIM))), (DN_HEADS // hb, nc),
                 [tok, tok, tok, row, row],
                 (tok, pl.BlockSpec((hb, None, HEAD_DIM, HEAD_DIM), lambda h, n: (h, n, 0, 0))),
                 scratch=[pltpu.VMEM((hb, HEAD_DIM, HEAD_DIM), f32)])(q, k, v, g_rows, b_rows)


def delta_bwd(name, q, k, v, g_rows, b_rows, s_all, do):
    t = q.shape[0]
    nc = t // CHUNK
    c = CHUNK
    hb = DELTA_HB

    def body(q_ref, k_ref, v_ref, g_ref, b_ref, s_ref, do_ref, dq_ref, dk_ref, dv_ref, dg_ref, db_ref, dstate):
        step = pl.program_id(1)
        n = nc - 1 - step

        @pl.when(step == 0)
        def _():
            dstate[...] = jnp.zeros_like(dstate)

        qv, kv, vv = _heads(q_ref, hb), _heads(k_ref, hb), _heads(v_ref, hb)
        L = _delta_local(qv, kv, vv, g_ref[:, pl.ds(n, 1), :], b_ref[:, pl.ds(n, 1), :])
        eye, lower, strict = L["eye"], L["lower"], L["strict"]
        shp = (hb, c, c)
        s0 = s_ref[...]
        dov = _heads(do_ref, hb)
        ds = dstate[...]
        eg = jnp.exp(L["gl"])
        v_new = L["u"] - _bnn(L["w"], s0)
        d_k_dec = _bnt(v_new, ds)
        d_v_new = _bnn(L["k_dec"], ds) + _btn(L["amat"], dov)
        d_eg = jnp.sum(jnp.sum(ds * s0, axis=2, keepdims=True), axis=1, keepdims=True)
        d_q_dec = _bnt(dov, s0)
        d_a = _bnt(dov, v_new)
        d_w = -_bnt(d_v_new, s0)
        dstate[...] = ds * eg + _btn(L["q_dec"], dov) - _btn(L["w"], d_v_new)
        d_am = jnp.where(lower, d_a * L["decay"], 0.0)
        dq = _bnn(d_am, kv) + d_q_dec * L["egc"]
        dk = _btn(d_am, qv) + d_k_dec * L["kdf"]
        e_col = jnp.sum(d_k_dec * L["k_dec"], axis=2, keepdims=True)
        d_gc_col = jnp.sum(d_q_dec * L["q_dec"], axis=2, keepdims=True) - e_col
        d_gl = jnp.sum(e_col, axis=1, keepdims=True) + d_eg * eg
        tinv = L["tinv"]
        d_rhs_u = _btn(tinv, d_v_new, HIGH)
        d_rhs_w = _btn(tinv, d_w, HIGH)
        d_l = -(_bnt(d_rhs_u, L["u"], HIGH) + _bnt(d_rhs_w, L["w"], HIGH))
        _put_heads(dv_ref, d_rhs_u * L["b_col"])
        d_b_col = jnp.sum(d_rhs_u * vv, axis=2, keepdims=True)
        d_gc_col = d_gc_col + jnp.sum(d_rhs_w * L["rhs_w"], axis=2, keepdims=True)
        d_lm = jnp.where(strict, d_l * L["decay"], 0.0)
        d_kb = d_rhs_w * L["egc"] + _bnn(d_lm, kv)
        dk = dk + _btn(d_lm, L["kb"]) + d_kb * L["b_col"]
        d_b_col = d_b_col + jnp.sum(d_kb * kv, axis=2, keepdims=True)
        m = d_am * L["qk"] + d_lm * L["kk"]
        d_gc_col = d_gc_col + jnp.sum(m, axis=2, keepdims=True)
        d_gc_row = (jnp.sum(jnp.where(eye, jnp.broadcast_to(d_gc_col, shp), 0.0), axis=1, keepdims=True)
                    - jnp.sum(m, axis=1, keepdims=True))
        lane = lax.broadcasted_iota(jnp.int32, (1, 1, c), 2)
        d_gc_row = d_gc_row + jnp.where(lane == c - 1, d_gl, 0.0)
        d_gc_tot = jnp.sum(jnp.where(eye, jnp.broadcast_to(d_gc_row, shp), 0.0), axis=2, keepdims=True)
        dg_ref[:, pl.ds(n, 1), :] = jnp.sum(jnp.where(lower, jnp.broadcast_to(d_gc_tot, shp), 0.0), axis=1,
                                            keepdims=True)
        db_ref[:, pl.ds(n, 1), :] = jnp.sum(jnp.where(eye, jnp.broadcast_to(d_b_col, shp), 0.0), axis=1,
                                            keepdims=True)
        _put_heads(dq_ref, dq)
        _put_heads(dk_ref, dk)

    tok = pl.BlockSpec((CHUNK, hb * HEAD_DIM), lambda h, s: (nc - 1 - s, h))
    row = pl.BlockSpec((hb, nc, CHUNK), lambda h, s: (h, 0, 0))
    return _call(body, name, (_sds((t, D_MODEL)),) * 3 + (_sds((DN_HEADS, nc, CHUNK)),) * 2, (DN_HEADS // hb, nc),
                 [tok, tok, tok, row, row,
                  pl.BlockSpec((hb, None, HEAD_DIM, HEAD_DIM), lambda h, s: (h, nc - 1 - s, 0, 0)), tok],
                 (tok, tok, tok, row, row),
                 scratch=[pltpu.VMEM((hb, HEAD_DIM, HEAD_DIM), f32)])(q, k, v, g_rows, b_rows, s_all, do)


def s5_scan(name, x, lam, *, reverse=False, h_for_dlam=None):
    t = x.shape[0]
    tt = _tile(t, (512, 256))
    nt = t // tt
    w = SG_STATE
    with_dlam = h_for_dlam is not None

    def body(*refs):
        if with_dlam:
            x_ref, lam_ref, h_ref, hprev_ref, o_ref, dl_ref, carry, dl_acc = refs
        else:
            x_ref, lam_ref, o_ref, carry = refs
        i = pl.program_id(1)

        @pl.when(i == 0)
        def _():
            carry[...] = jnp.zeros_like(carry)
            if with_dlam:
                dl_acc[...] = jnp.zeros_like(dl_acc)

        lr = lam_ref[:, 0:w]
        li = lam_ref[:, w:2 * w]

        def step(s, st):
            tpos = (tt - 1 - s) if reverse else s
            row = x_ref[pl.ds(tpos, 1), :]
            hr, hi = st[0], st[1]
            nr = lr * hr - li * hi + row[:, 0:w]
            ni = lr * hi + li * hr + row[:, w:2 * w]
            o_ref[pl.ds(tpos, 1), 0:w] = nr
            o_ref[pl.ds(tpos, 1), w:2 * w] = ni
            if not with_dlam:
                return nr, ni
            hp = jnp.where(tpos > 0, h_ref[pl.ds(jnp.maximum(tpos - 1, 0), 1), :], hp_tile)
            pr, pi = hp[:, 0:w], hp[:, w:2 * w]
            return nr, ni, st[2] + nr * pr + ni * pi, st[3] + ni * pr - nr * pi

        init = (carry[:, 0:w], carry[:, w:2 * w])
        if with_dlam:
            hp_tile = hprev_ref[7:8, :] * (((nt - 1 - i) if reverse else i) > 0).astype(f32)
            init = init + (dl_acc[:, 0:w], dl_acc[:, w:2 * w])
        fin = lax.fori_loop(0, tt, step, init)
        carry[:, 0:w] = fin[0]
        carry[:, w:2 * w] = fin[1]
        if with_dlam:
            dl_acc[:, 0:w] = fin[2]
            dl_acc[:, w:2 * w] = fin[3]

            @pl.when(i == nt - 1)
            def _():
                dl_ref[...] = dl_acc[...]

    blk = (lambda s, i: (nt - 1 - i, s)) if reverse else (lambda s, i: (i, s))
    x_spec = pl.BlockSpec((tt, 2 * w), blk)
    lam_spec = pl.BlockSpec((None, 1, 2 * w), lambda s, i: (s, 0, 0))
    if not with_dlam:
        return _call(body, name, _sds(x.shape), (N_SUPER, nt), [x_spec, lam_spec], x_spec,
                     scratch=[pltpu.VMEM((1, 2 * w), f32)])(x, lam)
    hprev_spec = pl.BlockSpec((8, 2 * w), lambda s, i: (jnp.maximum((nt - 1 - i) * (tt // 8) - 1, 0), s))
    return _call(body, name, (_sds(x.shape), _sds((N_SUPER, 1, 2 * w))), (N_SUPER, nt),
                 [x_spec, lam_spec, x_spec, hprev_spec], (x_spec, lam_spec),
                 scratch=[pltpu.VMEM((1, 2 * w), f32), pltpu.VMEM((1, 2 * w), f32)])(x, lam, h_for_dlam, h_for_dlam)


def _s5_disc(a_re, a_im, log_dt, br, bi):
    dt = jnp.exp(log_dt)
    mag = jnp.exp(a_re * dt)
    lr, li = mag * jnp.cos(a_im * dt), mag * jnp.sin(a_im * dt)
    den = a_re * a_re + a_im * a_im
    fr = ((lr - 1.0) * a_re + li * a_im) / den
    fi = (li * a_re - (lr - 1.0) * a_im) / den
    return lr, li, fr * br - fi * bi, fr * bi + fi * br


def s5_disc_fwd(name, a_re, a_im, log_dt, br, bi):
    g, n = SSM_GROUPS, SSM_STATE

    def body(ar, ai, ld, brr, bir, lr, li, bbr, bbi):
        o = _s5_disc(ar[...], ai[...], ld[...], brr[...], bir[...])
        lr[...], li[...], bbr[...], bbi[...] = o

    return _call(body, name, (_sds((g, 1, n)), _sds((g, 1, n)), _sds((g, SSM_GROUP, n)), _sds((g, SSM_GROUP, n))))(
        a_re, a_im, log_dt, br, bi)


def s5_disc_bwd(name, a_re, a_im, log_dt, br, bi, d_lr, d_li, d_bbr, d_bbi):
    g, n = SSM_GROUPS, SSM_STATE

    def body(ar, ai, ld, brr, bir, c1, c2, c3, c4, o1, o2, o3, o4, o5):
        _, vjp = jax.vjp(_s5_disc, ar[...], ai[...], ld[...], brr[...], bir[...])
        o1[...], o2[...], o3[...], o4[...], o5[...] = vjp((c1[...], c2[...], c3[...], c4[...]))

    return _call(body, name, (_sds((g, 1, n)), _sds((g, 1, n)), _sds((g, 1, 1)), _sds((g, SSM_GROUP, n)),
                              _sds((g, SSM_GROUP, n))))(a_re, a_im, log_dt, br, bi, d_lr, d_li, d_bbr, d_bbi)


def gelu_fwd(name, ys_lin, proj, d_skip):
    t, d = ys_lin.shape
    tm = _rows(t)

    def body(y_ref, u_ref, d_ref, o_ref):
        o_ref[...] = _gelu(y_ref[...] + d_ref[...] * u_ref[...])

    return _call(body, name, _sds((t, d)), (t // tm,),
                 [_row_spec(tm, d), pl.BlockSpec((tm, d), lambda i: (i, 4)), _acc_spec(1, d)],
                 _row_spec(tm, d))(ys_lin, proj, d_skip)


def _head_norm(o, hn):
    outs, ns, rs = [], [], []
    for h in range(DN_HEADS):
        oh = o[:, h * 128:(h + 1) * 128]
        r = lax.rsqrt(jnp.mean(oh * oh, axis=-1, keepdims=True) + EPS)
        n = oh * r
        ns.append(n)
        rs.append(r)
        outs.append(n * hn)
    return outs, ns, rs


def merge_fwd(name, proj, o, yg, glu_lin, head_norm, b_glu):
    t = o.shape[0]
    tm = _rows(t)
    d = D_MODEL

    def body(za_ref, zb_ref, ra_ref, rb_ref, o_ref, yg_ref, gl_ref, hn_ref, bg_ref, m_ref):
        on, _, _ = _head_norm(o_ref[...], hn_ref[...])
        y_a = jnp.concatenate(on, axis=1) * _silu(za_ref[...])
        y_b = yg_ref[...] * _sigmoid(gl_ref[...] + bg_ref[...]) * _silu(zb_ref[...])
        m_ref[...] = _sigmoid(ra_ref[...]) * y_a + _sigmoid(rb_ref[...]) * y_b

    pc = lambda c: pl.BlockSpec((tm, d), lambda i: (i, c))
    return _call(body, name, _sds((t, d)), (t // tm,),
                 [pc(3), pc(5), pc(6), pc(7), _row_spec(tm, d), _row_spec(tm, d), _row_spec(tm, d),
                  _acc_spec(1, 128), _acc_spec(1, d)], _row_spec(tm, d))(
        proj, proj, proj, proj, o, yg, glu_lin, head_norm, b_glu)


def merge_bwd(name, proj, o, yg, glu_lin, head_norm, b_glu, dm):
    t = o.shape[0]
    tm = _rows(t)
    d = D_MODEL

    def body(za_ref, zb_ref, ra_ref, rb_ref, o_ref, yg_ref, gl_ref, hn_ref, bg_ref, dm_ref,
             dza_ref, dzb_ref, dra_ref, drb_ref, do_ref, dgl_ref, dyg_ref, dhn_ref, dbg_ref):
        hn = hn_ref[...]
        za, zb, ra, rb = za_ref[...], zb_ref[...], ra_ref[...], rb_ref[...]
        on, ns, rs = _head_norm(o_ref[...], hn)
        onc = jnp.concatenate(on, axis=1)
        sza = _silu(za)
        y_a = onc * sza
        yg = yg_ref[...]
        sgl = _sigmoid(gl_ref[...] + bg_ref[...])
        y2 = yg * sgl
        szb = _silu(zb)
        y_b = y2 * szb
        sra, srb = _sigmoid(ra), _sigmoid(rb)
        dmv = dm_ref[...]
        dra_ref[...] = dmv * y_a * sra * (1.0 - sra)
        drb_ref[...] = dmv * y_b * srb * (1.0 - srb)
        d_ya = dmv * sra
        d_yb = dmv * srb
        dza_ref[...] = d_ya * onc * _dsilu(za)
        dzb_ref[...] = d_yb * y2 * _dsilu(zb)
        d_on = d_ya * sza
        d_y2 = d_yb * szb
        dyg_ref[...] = d_y2 * sgl
        d_gl = d_y2 * yg * sgl * (1.0 - sgl)
        dgl_ref[...] = d_gl
        _accumulate(dbg_ref, jnp.sum(d_gl, axis=0, keepdims=True))
        d_hn = jnp.zeros((1, 128), f32)
        for h in range(DN_HEADS):
            hs = slice(h * 128, (h + 1) * 128)
            dh = d_on[:, hs]
            d_hn = d_hn + jnp.sum(dh * ns[h], axis=0, keepdims=True)
            dn = dh * hn
            do_ref[:, hs] = rs[h] * (dn - ns[h] * jnp.mean(dn * ns[h], axis=-1, keepdims=True))
        _accumulate(dhn_ref, d_hn)

    pc = lambda c: pl.BlockSpec((tm, d), lambda i: (i, c))
    rs_ = _row_spec(tm, d)
    return _call(body, name, (_sds((t, d)),) * 7 + (_sds((1, 128)), _sds((1, d))), (t // tm,),
                 [pc(3), pc(5), pc(6), pc(7), rs_, rs_, rs_, _acc_spec(1, 128), _acc_spec(1, d), rs_],
                 (rs_,) * 7 + (_acc_spec(1, 128), _acc_spec(1, d)))(
        proj, proj, proj, proj, o, yg, glu_lin, head_norm, b_glu, dm)


def gelu_bwd(name, ys_lin, proj, d_skip, dyg_a, dyg_b):
    t, d = ys_lin.shape
    tm = _rows(t)

    def body(y_ref, u_ref, d_ref, a_ref, b_ref, dys_ref, du_ref, dd_ref):
        uv = u_ref[...]
        dys = (a_ref[...] + b_ref[...]) * _dgelu(y_ref[...] + d_ref[...] * uv)
        dys_ref[...] = dys
        du_ref[...] = dys * d_ref[...]
        _accumulate(dd_ref, jnp.sum(dys * uv, axis=0, keepdims=True))

    rs_ = _row_spec(tm, d)
    return _call(body, name, (_sds((t, d)), _sds((t, d)), _sds((1, d))), (t // tm,),
                 [rs_, pl.BlockSpec((tm, d), lambda i: (i, 4)), _acc_spec(1, d), rs_, rs_],
                 (rs_, rs_, _acc_spec(1, d)))(ys_lin, proj, d_skip, dyg_a, dyg_b)


def assemble_dproj(name, dqkv, dza, du_a, du_b, dzb, dra, drb, dbd):
    t = dza.shape[0]
    tm = _rows(t)
    d = D_MODEL

    def body(qkv_ref, za_ref, ua_ref, ub_ref, zb_ref, ra_ref, rb_ref, bd_ref, o_ref):
        o_ref[:, 0:3 * d] = qkv_ref[...].astype(bf16)
        o_ref[:, 3 * d:4 * d] = za_ref[...].astype(bf16)
        o_ref[:, 4 * d:5 * d] = (ua_ref[...] + ub_ref[...]).astype(bf16)
        o_ref[:, 5 * d:6 * d] = zb_ref[...].astype(bf16)
        o_ref[:, 6 * d:7 * d] = ra_ref[...].astype(bf16)
        o_ref[:, 7 * d:8 * d] = rb_ref[...].astype(bf16)
        o_ref[:, 8 * d:8 * d + 128] = bd_ref[...].astype(bf16)

    rs_ = _row_spec(tm, d)
    return _call(body, name, _sds((t, W_PAD), bf16), (t // tm,),
                 [_row_spec(tm, 3 * d), rs_, rs_, rs_, rs_, rs_, rs_, _row_spec(tm, 128)], _row_spec(tm, W_PAD))(
        dqkv, dza, du_a, du_b, dzb, dra, drb, dbd)


def adamw(name, w, g, m, v):
    r, c = w.shape
    tm = _tile(r, (512, 256, 128, 64, 32, 16, 8))
    c1 = 1.0 / (1.0 - ADAM_B1 ** ADAM_STEP)
    c2 = 1.0 / (1.0 - ADAM_B2 ** ADAM_STEP)

    def body(w_ref, g_ref, m_ref, v_ref, d_ref, nm_ref, nv_ref):
        gv = g_ref[...]
        nm = ADAM_B1 * m_ref[...] + (1.0 - ADAM_B1) * gv
        nv = ADAM_B2 * v_ref[...] + (1.0 - ADAM_B2) * (gv * gv)
        d_ref[...] = -ADAM_LR * ((nm * c1) / (jnp.sqrt(nv * c2) + ADAM_EPS) + ADAM_WD * w_ref[...])
        nm_ref[...] = nm
        nv_ref[...] = nv

    sp = pl.BlockSpec((tm, c), lambda i: (i, 0))
    return _call(body, name, (_sds((r, c)),) * 3, (r // tm,), [sp] * 4, (sp,) * 3)(w, g, m, v)


def _coords():
    return lax.axis_index("x"), lax.axis_index("y"), lax.axis_index("c")


def _flip(me, k):
    x, y, c = me
    return (1 - x if k & 4 else x, 1 - y if k & 2 else y, 1 - c if k & 1 else c)


def _lin(dev):
    return 4 * dev[0] + 2 * dev[1] + dev[2]


def all_gather(name, shard):
    r, c = shard.shape

    def body(x_ref, o_ref, send_sems, recv_sems, local_sem):
        me = _coords()
        mine = pltpu.make_async_copy(x_ref, o_ref.at[_lin(me)], local_sem)
        mine.start()
        sends = []
        for k in range(1, N_DEV):
            cp = pltpu.make_async_remote_copy(src_ref=x_ref, dst_ref=o_ref.at[_lin(me)], send_sem=send_sems.at[k - 1],
                                              recv_sem=recv_sems.at[k - 1], device_id=_flip(me, k), device_id_type=MESH)
            cp.start()
            sends.append(cp)
        for k in range(1, N_DEV):
            src = _lin(_flip(me, k))
            pltpu.make_async_remote_copy(src_ref=x_ref, dst_ref=o_ref.at[src], send_sem=send_sems.at[k - 1],
                                         recv_sem=recv_sems.at[k - 1], device_id=_flip(me, k),
                                         device_id_type=MESH).wait_recv()
        for cp in sends:
            cp.wait_send()
        mine.wait()

    any_spec = pl.BlockSpec(memory_space=pl.ANY)
    return _call(body, name, _sds((N_DEV, r, c), shard.dtype), in_specs=[any_spec], out_specs=any_spec,
                 scratch=[pltpu.SemaphoreType.DMA((N_DEV - 1,)), pltpu.SemaphoreType.DMA((N_DEV - 1,)),
                          pltpu.SemaphoreType.DMA(())])(shard)


def exchange_blocks(name, blocks):
    _, r, c = blocks.shape

    def body(x_ref, o_ref, send_sems, recv_sems, local_sem):
        me = _coords()
        my = _lin(me)
        mine = pltpu.make_async_copy(x_ref.at[my], o_ref.at[my], local_sem)
        mine.start()
        sends = []
        for k in range(1, N_DEV):
            peer = _flip(me, k)
            cp = pltpu.make_async_remote_copy(src_ref=x_ref.at[_lin(peer)], dst_ref=o_ref.at[my],
                                              send_sem=send_sems.at[k - 1], recv_sem=recv_sems.at[k - 1],
                                              device_id=peer, device_id_type=MESH)
            cp.start()
            sends.append(cp)
        for k in range(1, N_DEV):
            peer = _flip(me, k)
            pltpu.make_async_remote_copy(src_ref=x_ref.at[my], dst_ref=o_ref.at[_lin(peer)],
                                         send_sem=send_sems.at[k - 1], recv_sem=recv_sems.at[k - 1],
                                         device_id=peer, device_id_type=MESH).wait_recv()
        for cp in sends:
            cp.wait_send()
        mine.wait()

    any_spec = pl.BlockSpec(memory_space=pl.ANY)
    return _call(body, name, _sds(blocks.shape, blocks.dtype), in_specs=[any_spec], out_specs=any_spec,
                 scratch=[pltpu.SemaphoreType.DMA((N_DEV - 1,)), pltpu.SemaphoreType.DMA((N_DEV - 1,)),
                          pltpu.SemaphoreType.DMA(())])(blocks)


def sum_slots(name, slots):
    _, r, c = slots.shape
    tm = _tile(r, (240, 256, 128, 64, 32, 16))

    def body(s_ref, o_ref):
        acc = s_ref[0].astype(f32)
        for d in range(1, N_DEV):
            acc = acc + s_ref[d].astype(f32)
        o_ref[...] = acc

    return _call(body, name, _sds((r, c)), (r // tm,), [pl.BlockSpec((N_DEV, tm, c), lambda i: (0, i, 0))],
                 pl.BlockSpec((tm, c), lambda i: (i, 0)))(slots)


def _rows_layout(col8, t):
    return col8.T.reshape(DN_HEADS, t // CHUNK, CHUNK)


def _blockdiag(m):
    g, a, b = m.shape
    m = m.reshape(N_SUPER, SUPER, a, b)
    out = jnp.einsum("sgab,gh->sgahb", m, jnp.eye(SUPER, dtype=m.dtype))
    return out.reshape(N_SUPER, SUPER * a, SUPER * b)


def _diag_blocks(m, a, b):
    m = m.reshape(N_SUPER, SUPER, a, SUPER, b)
    return jnp.einsum("sgahb,gh->sgab", m, jnp.eye(SUPER, dtype=m.dtype)).reshape(SSM_GROUPS, a, b)


def _s5_params(p, li):
    tag = f"l{li}"
    n = SSM_STATE
    a_re = p["ssm_a_re"].reshape(SSM_GROUPS, 1, n)
    a_im = p["ssm_a_im"].reshape(SSM_GROUPS, 1, n)
    log_dt = p["ssm_log_dt"].reshape(SSM_GROUPS, 1, 1)
    br = jnp.swapaxes(p["ssm_b_re"], 1, 2)
    bi = jnp.swapaxes(p["ssm_b_im"], 1, 2)
    lr, li_, bbr, bbi = s5_disc_fwd("s5_disc_" + tag, a_re, a_im, log_dt, br, bi)
    lam = jnp.concatenate([lr.reshape(N_SUPER, 1, SG_STATE), li_.reshape(N_SUPER, 1, SG_STATE)], axis=-1)
    bbd = jnp.concatenate([_blockdiag(bbr), _blockdiag(bbi)], axis=-1).astype(bf16)
    c_re = jnp.swapaxes(p["ssm_c_re"], 1, 2)
    c_im = jnp.swapaxes(p["ssm_c_im"], 1, 2)
    cbd = jnp.concatenate([_blockdiag(c_re), -_blockdiag(c_im)], axis=1).astype(bf16)
    return dict(a_re=a_re, a_im=a_im, log_dt=log_dt, br=br, bi=bi, lam=lam, bbd=bbd, cbd=cbd)


def layer_fwd(x, p, li):
    tag = f"l{li}"
    t = x.shape[0]
    d = D_MODEL
    h = rmsnorm_fwd("norm_pre_" + tag, x, p["norm_pre"])
    proj = mm_nn("proj_" + tag, h, p["w_all"])
    q, k, v, bg = conv_qkv_fwd("conv_" + tag, proj, p["conv_w"], p["a_log"], p["dt_bias"])
    b_rows = _rows_layout(bg[:, 0:DN_HEADS], t)
    g_rows = _rows_layout(bg[:, DN_HEADS:2 * DN_HEADS], t)
    o, s_all = delta_fwd("delta_" + tag, q, k, v, g_rows, b_rows)
    sp = _s5_params(p, li)
    xs = sg_in("s5_in_" + tag, proj, sp["bbd"], col0=4 * d // 128)
    hs = s5_scan("s5_scan_" + tag, xs, sp["lam"])
    ys_lin = sg_out("s5_out_" + tag, hs, sp["cbd"])
    yg = gelu_fwd("gelu_" + tag, ys_lin, proj, p["ssm_d"])
    glu_lin = mm_nn("glu_" + tag, yg, p["w_glu"])
    merged = merge_fwd("merge_" + tag, proj, o, yg, glu_lin, p["head_norm"], p["b_glu"])
    out = mm_nn("out_" + tag, merged, p["w_out"])
    y = residual_norm_fwd("norm_post_" + tag, x, out, p["norm_post"])
    saved = dict(x=x, h=h, proj=proj, q=q, k=k, v=v, g_rows=g_rows, b_rows=b_rows, o=o, s_all=s_all, sp=sp,
                 hs=hs, ys_lin=ys_lin, yg=yg, glu_lin=glu_lin, merged=merged, out=out)
    return y, saved


def layer_bwd(dy, p, s, li):
    tag = f"l{li}"
    t = dy.shape[0]
    d = D_MODEL
    sp = s["sp"]
    gr = {}
    d_out, gr["norm_post"] = post_norm_bwd("norm_post_b_" + tag, s["out"], p["norm_post"], dy)
    d_merged = mm_nt("out_b_" + tag, d_out, p["w_out"])
    gr["w_out"] = mm_tn("out_w_" + tag, s["merged"], d_out, bf16)
    (dza, dzb, dra, drb, d_o, d_glu, dyg_a, gr["head_norm"], gr["b_glu"]) = merge_bwd(
        "merge_b_" + tag, s["proj"], s["o"], s["yg"], s["glu_lin"], p["head_norm"], p["b_glu"], d_merged)
    dyg_b = mm_nt("glu_b_" + tag, d_glu, p["w_glu"])
    gr["w_glu"] = mm_tn("glu_w_" + tag, s["yg"], d_glu, bf16)
    d_ys, du_a, gr["ssm_d"] = gelu_bwd("gelu_b_" + tag, s["ys_lin"], s["proj"], p["ssm_d"], dyg_a, dyg_b)
    d_hs = sg_out_t("s5_out_b_" + tag, d_ys, sp["cbd"])
    d_cbd = sg_grad("s5_out_w_" + tag, s["hs"], d_ys, 2 * SG_STATE, 128)
    adj, d_lam = s5_scan("s5_scan_b_" + tag, d_hs, sp["lam"] * jnp.concatenate(
        [jnp.ones((1, 1, SG_STATE), f32), -jnp.ones((1, 1, SG_STATE), f32)], axis=-1), reverse=True,
        h_for_dlam=s["hs"])
    du_b = sg_in_t("s5_in_b_" + tag, adj, sp["bbd"])
    d_bbd = sg_grad("s5_in_w_" + tag, s["proj"], adj, 128, 2 * SG_STATE, a_col0=4 * d // 128)
    d_c_re = _diag_blocks(d_cbd[:, 0:SG_STATE, :], SSM_STATE, SSM_GROUP)
    d_c_im = -_diag_blocks(d_cbd[:, SG_STATE:, :], SSM_STATE, SSM_GROUP)
    gr["ssm_c_re"] = jnp.swapaxes(d_c_re, 1, 2)
    gr["ssm_c_im"] = jnp.swapaxes(d_c_im, 1, 2)
    d_bbr = _diag_blocks(d_bbd[:, :, 0:SG_STATE], SSM_GROUP, SSM_STATE)
    d_bbi = _diag_blocks(d_bbd[:, :, SG_STATE:], SSM_GROUP, SSM_STATE)
    d_lr = d_lam[:, :, 0:SG_STATE].reshape(SSM_GROUPS, 1, SSM_STATE)
    d_li = d_lam[:, :, SG_STATE:].reshape(SSM_GROUPS, 1, SSM_STATE)
    d_are, d_aim, d_ldt, d_br, d_bi = s5_disc_bwd("s5_disc_b_" + tag, sp["a_re"], sp["a_im"], sp["log_dt"], sp["br"],
                                                  sp["bi"], d_lr, d_li, d_bbr, d_bbi)
    gr["ssm_a_re"] = d_are.reshape(SSM_GROUPS, SSM_STATE)
    gr["ssm_a_im"] = d_aim.reshape(SSM_GROUPS, SSM_STATE)
    gr["ssm_log_dt"] = d_ldt.reshape(SSM_GROUPS)
    gr["ssm_b_re"] = jnp.swapaxes(d_br, 1, 2)
    gr["ssm_b_im"] = jnp.swapaxes(d_bi, 1, 2)
    dq, dk, dv, dg_rows, db_rows = delta_bwd("delta_b_" + tag, s["q"], s["k"], s["v"], s["g_rows"], s["b_rows"],
                                             s["s_all"], d_o)
    dbg = jnp.concatenate([db_rows.reshape(DN_HEADS, t).T, dg_rows.reshape(DN_HEADS, t).T,
                           jnp.zeros((t, 128 - 2 * DN_HEADS), f32)], axis=1)
    dc, gr["conv_w"], dbd, dab = conv_qkv_bwd("conv_b_" + tag, s["proj"], p["conv_w"], p["a_log"], p["dt_bias"],
                                              dq, dk, dv, dbg)
    gr["a_log"] = dab[0, DN_HEADS:2 * DN_HEADS]
    gr["dt_bias"] = dab[1, DN_HEADS:2 * DN_HEADS]
    dqkv = conv_transpose("conv_t_" + tag, dc, p["conv_w"])
    dproj = assemble_dproj("dproj_" + tag, dqkv, dza, du_a, du_b, dzb, dra, drb, dbd)
    d_h = mm_nt("proj_b_" + tag, dproj, p["w_all"])
    gr["w_all"] = mm_tn("proj_w_" + tag, s["h"], dproj, bf16)
    dx, gr["norm_pre"] = rmsnorm_bwd("norm_pre_b_" + tag, s["x"], p["norm_pre"], d_h, dy)
    return dx, gr


REPL = ["norm_pre", "a_log", "dt_bias", "head_norm", "ssm_a_re", "ssm_a_im", "ssm_log_dt", "ssm_b_re", "ssm_b_im",
        "ssm_c_re", "ssm_c_im", "ssm_d", "b_glu", "norm_post"]
SHARDED = ["w_in", "conv_w", "w_glu", "w_out"]
ALL_W = ["norm_pre", "w_in", "conv_w", "a_log", "dt_bias", "head_norm", "ssm_a_re", "ssm_a_im", "ssm_log_dt",
         "ssm_b_re", "ssm_b_im", "ssm_c_re", "ssm_c_im", "ssm_d", "w_glu", "b_glu", "w_out", "norm_post"]
PACK_W = 1024


def _pack_flat(arrs, rows):
    flat = jnp.concatenate([a.reshape(-1) for a in arrs])
    return jnp.pad(flat, (0, rows * PACK_W - flat.shape[0])).reshape(rows, PACK_W)


def _flat_rows(arrs, mult=8):
    n = sum(math.prod(a.shape) for a in arrs)
    rows = -(-n // PACK_W)
    return -(-rows // mult) * mult


def _unpack(flat, shapes):
    out, off = [], 0
    for sh in shapes:
        n = math.prod(sh)
        out.append(flat[off:off + n].reshape(sh))
        off += n
    return out


def _permute_w(w):
    return jnp.concatenate([w[:, 0:4096], w[:, 4112:W_COLS], w[:, 4096:4112],
                            jnp.zeros((w.shape[0], W_PAD - W_COLS), w.dtype)], axis=1)


def _unpermute_w(g):
    return jnp.concatenate([g[:, 0:4096], g[:, 8192:8208], g[:, 4096:8192]], axis=1)


def kernel(x, norm_pre, w_in, conv_w, a_log, dt_bias, head_norm, ssm_a_re, ssm_a_im, ssm_log_dt, ssm_b_re, ssm_b_im, ssm_c_re, ssm_c_im, ssm_d, w_glu, b_glu, w_out, norm_post, loss_target, m_norm_pre, m_w_in, m_conv_w, m_a_log, m_dt_bias, m_head_norm, m_ssm_a_re, m_ssm_a_im, m_ssm_log_dt, m_ssm_b_re, m_ssm_b_im, m_ssm_c_re, m_ssm_c_im, m_ssm_d, m_w_glu, m_b_glu, m_w_out, m_norm_post, v_norm_pre, v_w_in, v_conv_w, v_a_log, v_dt_bias, v_head_norm, v_ssm_a_re, v_ssm_a_im, v_ssm_log_dt, v_ssm_b_re, v_ssm_b_im, v_ssm_c_re, v_ssm_c_im, v_ssm_d, v_w_glu, v_b_glu, v_w_out, v_norm_post):
    loc = dict(locals())
    w = {n: loc[n] for n in ALL_W}
    m = {n: loc["m_" + n] for n in ALL_W}
    v = {n: loc["v_" + n] for n in ALL_W}
    depth = w_in.shape[0]
    wc = w_in.shape[2]
    cc = conv_w.shape[2]
    wr = w_glu.shape[1]

    tail = wc - PACK_W
    conv_hi = conv_w.astype(bf16)
    conv_mid = (conv_w - conv_hi.astype(f32)).astype(bf16)
    conv_lo = (conv_w - conv_hi.astype(f32) - conv_mid.astype(f32)).astype(bf16)
    w_in_b = w_in.astype(bf16)
    misc_shapes = [(depth, D_MODEL, tail), (3, depth, CONV_K, cc)]
    misc_rows = _flat_rows([_sds(sh, bf16) for sh in misc_shapes], 16)
    n_main, n_sq = depth * D_MODEL, depth * wr
    shard = jnp.concatenate([w_in_b[:, :, :PACK_W].reshape(n_main, PACK_W), w_glu.astype(bf16).reshape(n_sq, PACK_W),
                             w_out.astype(bf16).reshape(n_sq, PACK_W),
                             _pack_flat([w_in_b[:, :, PACK_W:], jnp.stack([conv_hi, conv_mid, conv_lo])], misc_rows)])
    gathered = all_gather("gather_weights", shard)
    miscs = [_unpack(gathered[dv, n_main + 2 * n_sq:].reshape(-1), misc_shapes) for dv in range(N_DEV)]
    w_in_full = jnp.concatenate(
        [pc for dv in range(N_DEV) for pc in (gathered[dv, :n_main].reshape(depth, D_MODEL, PACK_W), miscs[dv][0])],
        axis=2)
    w_glu_full = gathered[:, n_main:n_main + n_sq].reshape(N_DEV, depth, wr, D_MODEL)
    w_glu_full = jnp.swapaxes(w_glu_full, 0, 1).reshape(depth, N_DEV * wr, D_MODEL)
    w_out_full = gathered[:, n_main + n_sq:n_main + 2 * n_sq].reshape(N_DEV, depth, wr, D_MODEL)
    w_out_full = jnp.swapaxes(w_out_full, 0, 1).reshape(depth, N_DEV * wr, D_MODEL)
    conv_full = jnp.concatenate([mi[1][0].astype(f32) + mi[1][1].astype(f32) + mi[1][2].astype(f32)
                                 for mi in miscs], axis=2)

    def layer_params(li):
        return dict(norm_pre=norm_pre[li].reshape(1, -1), w_all=_permute_w(w_in_full[li]), conv_w=conv_full[li],
                    a_log=jnp.pad(a_log[li], (DN_HEADS, 128 - 2 * DN_HEADS)).reshape(1, 128),
                    dt_bias=jnp.pad(dt_bias[li], (DN_HEADS, 128 - 2 * DN_HEADS)).reshape(1, 128),
                    head_norm=head_norm[li].reshape(1, -1), ssm_a_re=ssm_a_re[li], ssm_a_im=ssm_a_im[li],
                    ssm_log_dt=ssm_log_dt[li], ssm_b_re=ssm_b_re[li], ssm_b_im=ssm_b_im[li], ssm_c_re=ssm_c_re[li],
                    ssm_c_im=ssm_c_im[li], ssm_d=ssm_d[li].reshape(1, -1), w_glu=w_glu_full[li],
                    b_glu=b_glu[li].reshape(1, -1), w_out=w_out_full[li], norm_post=norm_post[li].reshape(1, -1))

    act = x[0]
    saved, params = [], []
    for li in range(depth):
        params.append(layer_params(li))
        act, sv = layer_fwd(act, params[li], li)
        saved.append(sv)
    loss_part, dy = loss_head("loss_head", act, loss_target[0])
    grads = [None] * depth
    for li in reversed(range(depth)):
        dy, grads[li] = layer_bwd(dy, params[li], saved[li], li)
    grad_x = dy[None]
    loss = lax.psum(loss_part[0, 0], ("x", "y", "c"))

    def stack(name):
        return jnp.stack([grads[li][name] for li in range(depth)])

    g_w_in = jnp.stack([_unpermute_w(grads[li]["w_all"]) for li in range(depth)])
    g_conv = stack("conv_w").astype(bf16)
    g_glu = stack("w_glu").reshape(depth, N_DEV, wr, D_MODEL)
    g_out = stack("w_out").reshape(depth, N_DEV, wr, D_MODEL)
    repl_shapes = [w[n].shape for n in REPL]
    repl_rows = _flat_rows([w[n] for n in REPL], 8 * N_DEV)
    g_repl = _pack_flat([stack(n).reshape(w[n].shape) for n in REPL], repl_rows).astype(bf16)
    rr = repl_rows // N_DEV
    gmisc_shapes = [(depth, D_MODEL, tail), (depth, CONV_K, cc)]
    gmisc_rows = _flat_rows([_sds(sh, bf16) for sh in gmisc_shapes], 16)
    blocks = jnp.stack([jnp.concatenate([
        g_w_in[:, :, dv * wc:dv * wc + PACK_W].reshape(n_main, PACK_W), g_glu[:, dv].reshape(n_sq, PACK_W),
        g_out[:, dv].reshape(n_sq, PACK_W),
        _pack_flat([g_w_in[:, :, dv * wc + PACK_W:(dv + 1) * wc], g_conv[:, :, dv * cc:(dv + 1) * cc]], gmisc_rows),
        g_repl[dv * rr:(dv + 1) * rr]]) for dv in range(N_DEV)])
    pad_rows = -blocks.shape[1] % 16
    if pad_rows:
        blocks = jnp.pad(blocks, ((0, 0), (0, pad_rows), (0, 0)))
    slots = exchange_blocks("scatter_grads", blocks)
    mine = sum_slots("sum_grads", slots)
    o1, o2, o3, o4 = n_main, n_main + n_sq, n_main + 2 * n_sq, n_main + 2 * n_sq + gmisc_rows
    gs_tail, gs_conv = _unpack(mine[o3:o4].reshape(-1), gmisc_shapes)
    gs_w_in = jnp.concatenate([mine[:o1].reshape(depth, D_MODEL, PACK_W), gs_tail], axis=2)
    gs_glu = mine[o1:o2].reshape(depth, wr, D_MODEL)
    gs_out = mine[o2:o3].reshape(depth, wr, D_MODEL)
    g_repl_full = all_gather("gather_repl_grads", mine[o4:o4 + rr]).reshape(-1)
    g = dict(zip(REPL, _unpack(g_repl_full, repl_shapes)))
    g.update(w_in=gs_w_in, w_glu=gs_glu, w_out=gs_out, conv_w=gs_conv)

    delta, new_m, new_v = {}, {}, {}
    for n in SHARDED:
        sh = w[n].shape
        two_d = (-1, sh[-1])
        delta[n], new_m[n], new_v[n] = [o.reshape(sh) for o in adamw(
            "adamw_" + n, w[n].reshape(two_d), g[n].reshape(two_d), m[n].reshape(two_d), v[n].reshape(two_d))]
    packed = [_pack_flat([src[n] for n in REPL], repl_rows) for src in (w, g, m, v)]
    outs = adamw("adamw_repl", *packed)
    for dst, o in zip((delta, new_m, new_v), outs):
        dst.update(zip(REPL, _unpack(o.reshape(-1), repl_shapes)))
    return (loss, grad_x, *[g[n] for n in ALL_W], *[delta[n] for n in ALL_W], *[new_m[n] for n in ALL_W],
            *[new_v[n] for n in ALL_W])
```

```python
import math

import jax
import jax.numpy as jnp
from jax import lax
from jax.experimental import pallas as pl
from jax.experimental.pallas import tpu as pltpu

f32 = jnp.float32
bf16 = jnp.bfloat16

D_MODEL = 1024
DEPTH = 4
N_DEV = 8
DN_HEADS = 8
HEAD_DIM = 128
CHUNK = 64
CONV_K = 4
SSM_GROUPS = 64
SSM_GROUP = 16
SSM_STATE = 64
SUPER = 8
N_SUPER = SSM_GROUPS // SUPER
SG_STATE = SUPER * SSM_STATE
EPS = 1e-6
W_COLS = 8208
W_PAD = 8320
ADAM_LR, ADAM_B1, ADAM_B2, ADAM_EPS, ADAM_WD, ADAM_STEP = 0.001, 0.9, 0.999, 1e-08, 0.01, 10
VMEM_LIMIT = 56 * 1024 * 1024
MESH = pl.DeviceIdType.MESH
HIGH = lax.Precision.HIGH
DELTA_HB = 8


def _call(body, name, out_shape, grid=None, in_specs=None, out_specs=None, scratch=(), **kw):
    args = dict(out_shape=out_shape, name=name, scratch_shapes=list(scratch),
                compiler_params=pltpu.CompilerParams(vmem_limit_bytes=VMEM_LIMIT, **kw))
    if grid is not None:
        args.update(grid=grid, in_specs=in_specs, out_specs=out_specs)
    else:
        if in_specs is not None:
            args.update(in_specs=in_specs)
        if out_specs is not None:
            args.update(out_specs=out_specs)
    return pl.pallas_call(body, **args)


def _sds(shape, dtype=f32):
    return jax.ShapeDtypeStruct(tuple(shape), dtype)


def _sigmoid(x):
    return 1.0 / (1.0 + jnp.exp(-x))


def _silu(x):
    return x * _sigmoid(x)


def _dsilu(x):
    s = _sigmoid(x)
    return s * (1.0 + x * (1.0 - s))


_GELU_C = math.sqrt(2.0 / math.pi)


def _gelu(x):
    return 0.5 * x * (1.0 + jnp.tanh(_GELU_C * (x + 0.044715 * x * x * x)))


def _dgelu(x):
    t = jnp.tanh(_GELU_C * (x + 0.044715 * x * x * x))
    return 0.5 * (1.0 + t) + 0.5 * x * (1.0 - t * t) * _GELU_C * (1.0 + 3 * 0.044715 * x * x)


def _softplus(x):
    return jnp.maximum(x, 0.0) + jnp.log(1.0 + jnp.exp(-jnp.abs(x)))


def _bdot(a, b, dn):
    return lax.dot_general(a.astype(bf16), b.astype(bf16), (dn, ((), ())), preferred_element_type=f32)


def _matmul(name, a, b, *, dn, grid, a_spec, b_spec, o_spec, o_shape, o_dtype=f32):
    nk = grid[-1]

    def body(a_ref, b_ref, o_ref, acc_ref):
        p = _bdot(a_ref[...], b_ref[...], dn)
        if nk == 1:
            o_ref[...] = p.astype(o_dtype)
        else:
            k = pl.program_id(len(grid) - 1)

            @pl.when(k == 0)
            def _():
                acc_ref[...] = p

            @pl.when(k > 0)
            def _():
                acc_ref[...] += p

            @pl.when(k == nk - 1)
            def _():
                o_ref[...] = acc_ref[...].astype(o_dtype)

    blk = tuple(d for d in o_spec.block_shape if d is not None)
    return _call(body, name, _sds(o_shape, o_dtype), grid, [a_spec, b_spec], o_spec,
                 scratch=[pltpu.VMEM(blk if nk > 1 else (8, 128), f32)])(a, b)


def _tile(n, pref):
    for t in pref:
        if n % t == 0:
            return t
    return n


def mm_nn(name, a, b):
    m, k = a.shape
    n = b.shape[1]
    tm, tn, tk = _tile(m, (512, 256)), _tile(n, (1664, 1024, 512)), _tile(k, (1664, 1024))
    return _matmul(name, a, b, dn=((1,), (0,)), grid=(m // tm, n // tn, k // tk),
                   a_spec=pl.BlockSpec((tm, tk), lambda i, j, l: (i, l)),
                   b_spec=pl.BlockSpec((tk, tn), lambda i, j, l: (l, j)),
                   o_spec=pl.BlockSpec((tm, tn), lambda i, j, l: (i, j)), o_shape=(m, n))


def mm_nt(name, a, b):
    m, k = a.shape
    n = b.shape[0]
    tm, tn, tk = _tile(m, (512, 256)), _tile(n, (1024, 512)), _tile(k, (1664, 1024))
    return _matmul(name, a, b, dn=((1,), (1,)), grid=(m // tm, n // tn, k // tk),
                   a_spec=pl.BlockSpec((tm, tk), lambda i, j, l: (i, l)),
                   b_spec=pl.BlockSpec((tn, tk), lambda i, j, l: (j, l)),
                   o_spec=pl.BlockSpec((tm, tn), lambda i, j, l: (i, j)), o_shape=(m, n))


def mm_tn(name, a, b, o_dtype=f32):
    k, m = a.shape
    n = b.shape[1]
    tm, tn, tk = _tile(m, (512,)), _tile(n, (1664, 1024, 512)), _tile(k, (512, 256))
    return _matmul(name, a, b, dn=((0,), (0,)), grid=(m // tm, n // tn, k // tk),
                   a_spec=pl.BlockSpec((tk, tm), lambda i, j, l: (l, i)),
                   b_spec=pl.BlockSpec((tk, tn), lambda i, j, l: (l, j)),
                   o_spec=pl.BlockSpec((tm, tn), lambda i, j, l: (i, j)), o_shape=(m, n), o_dtype=o_dtype)


def _rows(t):
    return _tile(t, (256,))


def _row_spec(tm, w):
    return pl.BlockSpec((tm, w), lambda i: (i, 0))


def _acc_spec(r, w):
    return pl.BlockSpec((r, w), lambda i: (0, 0))


def _accumulate(ref, val):
    @pl.when(pl.program_id(0) == 0)
    def _():
        ref[...] = val

    @pl.when(pl.program_id(0) > 0)
    def _():
        ref[...] += val


def rmsnorm_fwd(name, x, gain):
    t, d = x.shape
    tm = _rows(t)

    def body(x_ref, g_ref, o_ref):
        xv = x_ref[...]
        r = lax.rsqrt(jnp.mean(xv * xv, axis=-1, keepdims=True) + EPS)
        o_ref[...] = xv * r * g_ref[...]

    return _call(body, name, _sds((t, d)), (t // tm,), [_row_spec(tm, d), _acc_spec(1, d)], _row_spec(tm, d))(x, gain)


def rmsnorm_bwd(name, x, gain, dn, dres):
    t, d = x.shape
    tm = _rows(t)

    def body(x_ref, g_ref, dn_ref, dr_ref, dx_ref, dg_ref):
        xv = x_ref[...]
        r = lax.rsqrt(jnp.mean(xv * xv, axis=-1, keepdims=True) + EPS)
        n = xv * r
        dnv = dn_ref[...]
        _accumulate(dg_ref, jnp.sum(dnv * n, axis=0, keepdims=True))
        dng = dnv * g_ref[...]
        dx_ref[...] = dr_ref[...] + r * (dng - n * jnp.mean(dng * n, axis=-1, keepdims=True))

    return _call(body, name, (_sds((t, d)), _sds((1, d))), (t // tm,),
                 [_row_spec(tm, d), _acc_spec(1, d), _row_spec(tm, d), _row_spec(tm, d)],
                 (_row_spec(tm, d), _acc_spec(1, d)))(x, gain, dn, dres)


def residual_norm_fwd(name, x, out, gain):
    t, d = x.shape
    tm = _rows(t)

    def body(x_ref, o_ref, g_ref, y_ref):
        ov = o_ref[...]
        r = lax.rsqrt(jnp.mean(ov * ov, axis=-1, keepdims=True) + EPS)
        y_ref[...] = x_ref[...] + ov * r * g_ref[...]

    return _call(body, name, _sds((t, d)), (t // tm,), [_row_spec(tm, d), _row_spec(tm, d), _acc_spec(1, d)],
                 _row_spec(tm, d))(x, out, gain)


def post_norm_bwd(name, out, gain, dy):
    t, d = out.shape
    tm = _rows(t)

    def body(o_ref, g_ref, dy_ref, do_ref, dg_ref):
        ov = o_ref[...]
        r = lax.rsqrt(jnp.mean(ov * ov, axis=-1, keepdims=True) + EPS)
        n = ov * r
        dyv = dy_ref[...]
        _accumulate(dg_ref, jnp.sum(dyv * n, axis=0, keepdims=True))
        dng = dyv * g_ref[...]
        do_ref[...] = r * (dng - n * jnp.mean(dng * n, axis=-1, keepdims=True))

    return _call(body, name, (_sds((t, d)), _sds((1, d))), (t // tm,),
                 [_row_spec(tm, d), _acc_spec(1, d), _row_spec(tm, d)],
                 (_row_spec(tm, d), _acc_spec(1, d)))(out, gain, dy)


def loss_head(name, y, target):
    t, d = y.shape
    tm = _rows(t)

    def body(y_ref, t_ref, l_ref, dy_ref):
        e = y_ref[...] - t_ref[...]
        dy_ref[...] = e * (1.0 / d)
        s = jnp.sum(jnp.sum(e * e, axis=1, keepdims=True), axis=0, keepdims=True) * (0.5 / d)
        _accumulate(l_ref, s)

    return _call(body, name, (_sds((1, 1)), _sds((t, d))), (t // tm,),
                 [_row_spec(tm, d), _row_spec(tm, d)], (_acc_spec(1, 1), _row_spec(tm, d)))(y, target)


def _prev_spec(tm, w):
    return pl.BlockSpec((8, w), lambda i: (jnp.maximum(i * (tm // 8) - 1, 0), 0))


def _next_spec(tm, w, t):
    return pl.BlockSpec((8, w), lambda i: (jnp.minimum((i + 1) * (tm // 8), t // 8 - 1), 0))


def _fill_pad(pad_ref, prev_ref, cur_ref, tm):
    keep = (pl.program_id(0) > 0).astype(f32)
    pad_ref[0:8, :] = prev_ref[...] * keep
    pad_ref[8:8 + tm, :] = cur_ref[...]


def _conv_block(pad_ref, w_ref, cb, tm):
    cs = slice(cb * 128, (cb + 1) * 128)
    acc = pad_ref[pl.ds(8 - (CONV_K - 1), tm), cs] * w_ref[0:1, cs]
    for j in range(1, CONV_K):
        acc = acc + pad_ref[pl.ds(8 - (CONV_K - 1) + j, tm), cs] * w_ref[j:j + 1, cs]
    return acc


def conv_qkv_fwd(name, proj, conv_w, a_log, dt_bias):
    t = proj.shape[0]
    tm = _rows(t)
    scale = HEAD_DIM ** -0.5

    def body(cur_ref, prev_ref, w_ref, bd_ref, al_ref, db_ref, q_ref, k_ref, v_ref, bg_ref, pad_ref):
        _fill_pad(pad_ref, prev_ref, cur_ref, tm)
        for cb in range(3 * DN_HEADS):
            s = _silu(_conv_block(pad_ref, w_ref, cb, tm))
            hs = slice((cb % DN_HEADS) * 128, (cb % DN_HEADS + 1) * 128)
            if cb < DN_HEADS:
                q_ref[:, hs] = s * (lax.rsqrt(jnp.sum(s * s, axis=-1, keepdims=True) + EPS) * scale)
            elif cb < 2 * DN_HEADS:
                k_ref[:, hs] = s * lax.rsqrt(jnp.sum(s * s, axis=-1, keepdims=True) + EPS)
            else:
                v_ref[:, hs] = s
        bd = bd_ref[...]
        beta = _sigmoid(bd)
        g = -jnp.exp(al_ref[...]) * _softplus(bd + db_ref[...])
        lane = lax.broadcasted_iota(jnp.int32, bd.shape, 1)
        bg_ref[...] = jnp.where(lane < DN_HEADS, beta, jnp.where(lane < 2 * DN_HEADS, g, 0.0))

    w3 = 3 * D_MODEL
    return _call(body, name, (_sds((t, D_MODEL)),) * 3 + (_sds((t, 128)),), (t // tm,),
                 [pl.BlockSpec((tm, w3), lambda i: (i, 0)), _prev_spec(tm, w3), _acc_spec(CONV_K, w3),
                  pl.BlockSpec((tm, 128), lambda i: (i, 8192 // 128)), _acc_spec(1, 128), _acc_spec(1, 128)],
                 (_row_spec(tm, D_MODEL),) * 3 + (_row_spec(tm, 128),),
                 scratch=[pltpu.VMEM((tm + 8, w3), f32)])(proj, proj, conv_w, proj, a_log, dt_bias)


def conv_qkv_bwd(name, proj, conv_w, a_log, dt_bias, dq, dk, dv, dbg):
    t = proj.shape[0]
    tm = _rows(t)
    scale = HEAD_DIM ** -0.5

    def body(cur_ref, prev_ref, w_ref, bd_ref, al_ref, db_ref, dq_ref, dk_ref, dv_ref, dbg_ref,
             dc_ref, dw_ref, dbd_ref, dab_ref, pad_ref):
        _fill_pad(pad_ref, prev_ref, cur_ref, tm)

        @pl.when(pl.program_id(0) == 0)
        def _():
            dw_ref[...] = jnp.zeros_like(dw_ref)

        for cb in range(3 * DN_HEADS):
            cs = slice(cb * 128, (cb + 1) * 128)
            hs = slice((cb % DN_HEADS) * 128, (cb % DN_HEADS + 1) * 128)
            c = _conv_block(pad_ref, w_ref, cb, tm)
            s = _silu(c)
            if cb < 2 * DN_HEADS:
                dn = (dq_ref[:, hs] * scale) if cb < DN_HEADS else dk_ref[:, hs]
                r = lax.rsqrt(jnp.sum(s * s, axis=-1, keepdims=True) + EPS)
                ds = r * dn - s * (r * r * r) * jnp.sum(dn * s, axis=-1, keepdims=True)
            else:
                ds = dv_ref[:, hs]
            dc = ds * _dsilu(c)
            dc_ref[:, cs] = dc
            for j in range(CONV_K):
                dw_ref[j:j + 1, cs] += jnp.sum(dc * pad_ref[pl.ds(8 - (CONV_K - 1) + j, tm), cs], axis=0,
                                               keepdims=True)
        bd = bd_ref[...]
        dbg_v = dbg_ref[...]
        lane = lax.broadcasted_iota(jnp.int32, bd.shape, 1)
        sg = _sigmoid(bd)
        ea = jnp.exp(al_ref[...])
        z = bd + db_ref[...]
        sp = _softplus(z)
        is_b = lane < DN_HEADS
        is_g = jnp.logical_and(lane >= DN_HEADS, lane < 2 * DN_HEADS)
        d_z = jnp.where(is_g, dbg_v * (-ea) * _sigmoid(z), 0.0)
        dbd_ref[...] = jnp.where(is_b, dbg_v * sg * (1.0 - sg), d_z)
        d_al = jnp.sum(jnp.where(is_g, dbg_v * (-ea) * sp, 0.0), axis=0, keepdims=True)
        d_db = jnp.sum(d_z, axis=0, keepdims=True)
        _accumulate(dab_ref, jnp.concatenate([d_al, d_db] + [jnp.zeros_like(d_al)] * 6, axis=0))

    w3 = 3 * D_MODEL
    return _call(body, name, (_sds((t, w3)), _sds((CONV_K, w3)), _sds((t, 128)), _sds((8, 128))), (t // tm,),
                 [pl.BlockSpec((tm, w3), lambda i: (i, 0)), _prev_spec(tm, w3), _acc_spec(CONV_K, w3),
                  pl.BlockSpec((tm, 128), lambda i: (i, 8192 // 128)), _acc_spec(1, 128), _acc_spec(1, 128),
                  _row_spec(tm, D_MODEL), _row_spec(tm, D_MODEL), _row_spec(tm, D_MODEL), _row_spec(tm, 128)],
                 (_row_spec(tm, w3), _acc_spec(CONV_K, w3), _row_spec(tm, 128), _acc_spec(8, 128)),
                 scratch=[pltpu.VMEM((tm + 8, w3), f32)])(proj, proj, conv_w, proj, a_log, dt_bias, dq, dk, dv, dbg)


def conv_transpose(name, dc, conv_w):
    t, w3 = dc.shape
    tm = _rows(t)
    nt = t // tm

    def body(cur_ref, nxt_ref, w_ref, o_ref, pad_ref):
        keep = (pl.program_id(0) < nt - 1).astype(f32)
        pad_ref[0:tm, :] = cur_ref[...]
        pad_ref[tm:tm + 8, :] = nxt_ref[...] * keep
        for cb in range(w3 // 128):
            cs = slice(cb * 128, (cb + 1) * 128)
            acc = pad_ref[pl.ds(CONV_K - 1, tm), cs] * w_ref[0:1, cs]
            for j in range(1, CONV_K):
                acc = acc + pad_ref[pl.ds(CONV_K - 1 - j, tm), cs] * w_ref[j:j + 1, cs]
            o_ref[:, cs] = acc

    return _call(body, name, _sds((t, w3)), (nt,),
                 [_row_spec(tm, w3), _next_spec(tm, w3, t), _acc_spec(CONV_K, w3)], _row_spec(tm, w3),
                 scratch=[pltpu.VMEM((tm + 8, w3), f32)])(dc, dc, conv_w)


def _bdg(a, b, ca, cb, prec=None):
    if prec is None:
        a, b = a.astype(bf16), b.astype(bf16)
    return lax.dot_general(a, b, (((ca,), (cb,)), ((0,), (0,))), precision=prec, preferred_element_type=f32)


def _bnn(a, b, prec=None):
    return _bdg(a, b, 2, 1, prec)


def _bnt(a, b, prec=None):
    return _bdg(a, b, 2, 2, prec)


def _btn(a, b, prec=None):
    return _bdg(a, b, 1, 1, prec)


def _delta_local(q, k, v, g_row, b_row):
    c = CHUNK
    ii = lax.broadcasted_iota(jnp.int32, (c, c), 0)
    jj = lax.broadcasted_iota(jnp.int32, (c, c), 1)
    eye, lower, strict = ii == jj, ii >= jj, ii > jj
    shp = (q.shape[0], c, c)
    g_b = jnp.broadcast_to(g_row, shp)
    gc_col = jnp.sum(jnp.where(lower, g_b, 0.0), axis=2, keepdims=True)
    gc_row = jnp.sum(jnp.where(eye, jnp.broadcast_to(gc_col, shp), 0.0), axis=1, keepdims=True)
    b_col = jnp.sum(jnp.where(eye, jnp.broadcast_to(b_row, shp), 0.0), axis=2, keepdims=True)
    gl = jnp.sum(g_row, axis=2, keepdims=True)
    decay = jnp.exp(jnp.where(lower, gc_col - gc_row, -1e30))
    kb = k * b_col
    kk = _bnt(kb, k)
    lmat = jnp.where(strict, kk * decay, 0.0)
    tinv = eye.astype(f32) - lmat
    pw = lmat
    for _ in range(5):
        pw = _bnn(pw, pw, HIGH)
        tinv = tinv + _bnn(tinv, pw, HIGH)
    egc = jnp.exp(gc_col)
    rhs_w = kb * egc
    u = _bnn(tinv, v * b_col, HIGH)
    w = _bnn(tinv, rhs_w, HIGH)
    qk = _bnt(q, k)
    amat = jnp.where(lower, qk * decay, 0.0)
    q_dec = q * egc
    kdf = jnp.exp(gl - gc_col)
    k_dec = k * kdf
    return dict(eye=eye, lower=lower, strict=strict, gc_col=gc_col, b_col=b_col, gl=gl, decay=decay,
                kb=kb, kk=kk, tinv=tinv, egc=egc, rhs_w=rhs_w, u=u, w=w, qk=qk, amat=amat, q_dec=q_dec, kdf=kdf,
                k_dec=k_dec)


def _heads(ref, hb):
    return jnp.stack([ref[:, h * HEAD_DIM:(h + 1) * HEAD_DIM] for h in range(hb)])


def _put_heads(ref, val):
    for h in range(val.shape[0]):
        ref[:, h * HEAD_DIM:(h + 1) * HEAD_DIM] = val[h]


def delta_fwd(name, q, k, v, g_rows, b_rows):
    t = q.shape[0]
    nc = t // CHUNK
    hb = DELTA_HB

    def body(q_ref, k_ref, v_ref, g_ref, b_ref, o_ref, s_ref, state):
        n = pl.program_id(1)

        @pl.when(n == 0)
        def _():
            state[...] = jnp.zeros_like(state)

        loc = _delta_local(_heads(q_ref, hb), _heads(k_ref, hb), _heads(v_ref, hb), g_ref[:, pl.ds(n, 1), :],
                           b_ref[:, pl.ds(n, 1), :])
        s0 = state[...]
        s_ref[...] = s0
        v_new = loc["u"] - _bnn(loc["w"], s0)
        _put_heads(o_ref, _bnn(loc["q_dec"], s0) + _bnn(loc["amat"], v_new))
        state[...] = s0 * jnp.exp(loc["gl"]) + _btn(loc["k_dec"], v_new)

    tok = pl.BlockSpec((CHUNK, hb * HEAD_DIM), lambda h, n: (n, h))
    row = pl.BlockSpec((hb, nc, CHUNK), lambda h, n: (h, 0, 0))
    return _call(body, name, (_sds((t, D_MODEL)), _sds((DN_HEADS, nc, HEAD_DIM, HEAD_DIM))), (DN_HEADS // hb, nc),
                 [tok, tok, tok, row, row],
                 (tok, pl.BlockSpec((hb, None, HEAD_DIM, HEAD_DIM), lambda h, n: (h, n, 0, 0))),
                 scratch=[pltpu.VMEM((hb, HEAD_DIM, HEAD_DIM), f32)])(q, k, v, g_rows, b_rows)


def delta_bwd(name, q, k, v, g_rows, b_rows, s_all, do):
    t = q.shape[0]
    nc = t // CHUNK
    c = CHUNK
    hb = DELTA_HB

    def body(q_ref, k_ref, v_ref, g_ref, b_ref, s_ref, do_ref, dq_ref, dk_ref, dv_ref, dg_ref, db_ref, dstate):
        step = pl.program_id(1)
        n = nc - 1 - step

        @pl.when(step == 0)
        def _():
            dstate[...] = jnp.zeros_like(dstate)

        qv, kv, vv = _heads(q_ref, hb), _heads(k_ref, hb), _heads(v_ref, hb)
        L = _delta_local(qv, kv, vv, g_ref[:, pl.ds(n, 1), :], b_ref[:, pl.ds(n, 1), :])
        eye, lower, strict = L["eye"], L["lower"], L["strict"]
        shp = (hb, c, c)
        s0 = s_ref[...]
        dov = _heads(do_ref, hb)
        ds = dstate[...]
        eg = jnp.exp(L["gl"])
        v_new = L["u"] - _bnn(L["w"], s0)
        d_k_dec = _bnt(v_new, ds)
        d_v_new = _bnn(L["k_dec"], ds) + _btn(L["amat"], dov)
        d_eg = jnp.sum(jnp.sum(ds * s0, axis=2, keepdims=True), axis=1, keepdims=True)
        d_q_dec = _bnt(dov, s0)
        d_a = _bnt(dov, v_new)
        d_w = -_bnt(d_v_new, s0)
        dstate[...] = ds * eg + _btn(L["q_dec"], dov) - _btn(L["w"], d_v_new)
        d_am = jnp.where(lower, d_a * L["decay"], 0.0)
        dq = _bnn(d_am, kv) + d_q_dec * L["egc"]
        dk = _btn(d_am, qv) + d_k_dec * L["kdf"]
        e_col = jnp.sum(d_k_dec * L["k_dec"], axis=2, keepdims=True)
        d_gc_col = jnp.sum(d_q_dec * L["q_dec"], axis=2, keepdims=True) - e_col
        d_gl = jnp.sum(e_col, axis=1, keepdims=True) + d_eg * eg
        tinv = L["tinv"]
        d_rhs_u = _btn(tinv, d_v_new, HIGH)
        d_rhs_w = _btn(tinv, d_w, HIGH)
        d_l = -(_bnt(d_rhs_u, L["u"], HIGH) + _bnt(d_rhs_w, L["w"], HIGH))
        _put_heads(dv_ref, d_rhs_u * L["b_col"])
        d_b_col = jnp.sum(d_rhs_u * vv, axis=2, keepdims=True)
        d_gc_col = d_gc_col + jnp.sum(d_rhs_w * L["rhs_w"], axis=2, keepdims=True)
        d_lm = jnp.where(strict, d_l * L["decay"], 0.0)
        d_kb = d_rhs_w * L["egc"] + _bnn(d_lm, kv)
        dk = dk + _btn(d_lm, L["kb"]) + d_kb * L["b_col"]
        d_b_col = d_b_col + jnp.sum(d_kb * kv, axis=2, keepdims=True)
        m = d_am * L["qk"] + d_lm * L["kk"]
        d_gc_col = d_gc_col + jnp.sum(m, axis=2, keepdims=True)
        d_gc_row = (jnp.sum(jnp.where(eye, jnp.broadcast_to(d_gc_col, shp), 0.0), axis=1, keepdims=True)
                    - jnp.sum(m, axis=1, keepdims=True))
        lane = lax.broadcasted_iota(jnp.int32, (1, 1, c), 2)
        d_gc_row = d_gc_row + jnp.where(lane == c - 1, d_gl, 0.0)
        d_gc_tot = jnp.sum(jnp.where(eye, jnp.broadcast_to(d_gc_row, shp), 0.0), axis=2, keepdims=True)
        dg_ref[:, pl.ds(n, 1), :] = jnp.sum(jnp.where(lower, jnp.broadcast_to(d_gc_tot, shp), 0.0), axis=1,
                                            keepdims=True)
        db_ref[:, pl.ds(n, 1), :] = jnp.sum(jnp.where(eye, jnp.broadcast_to(d_b_col, shp), 0.0), axis=1,
                                            keepdims=True)
        _put_heads(dq_ref, dq)
        _put_heads(dk_ref, dk)

    tok = pl.BlockSpec((CHUNK, hb * HEAD_DIM), lambda h, s: (nc - 1 - s, h))
    row = pl.BlockSpec((hb, nc, CHUNK), lambda h, s: (h, 0, 0))
    return _call(body, name, (_sds((t, D_MODEL)),) * 3 + (_sds((DN_HEADS, nc, CHUNK)),) * 2, (DN_HEADS // hb, nc),
                 [tok, tok, tok, row, row,
                  pl.BlockSpec((hb, None, HEAD_DIM, HEAD_DIM), lambda h, s: (h, nc - 1 - s, 0, 0)), tok],
                 (tok, tok, tok, row, row),
                 scratch=[pltpu.VMEM((hb, HEAD_DIM, HEAD_DIM), f32)])(q, k, v, g_rows, b_rows, s_all, do)


SEG = 8


def _perm_rows(a):
    t, c = a.shape
    return a.reshape(SEG, t // SEG, c).transpose(1, 0, 2).reshape(t, c)


def _unperm_rows(a):
    t, c = a.shape
    return a.reshape(t // SEG, SEG, c).transpose(1, 0, 2).reshape(t, c)


def _cmul(ar, ai, br, bi):
    return ar * br - ai * bi, ar * bi + ai * br


def _segment_init(er, ei, lr, li, seg_len, reverse):
    w = er.shape[1]
    sub = lax.broadcasted_iota(jnp.int32, (SEG, w), 0)

    def shift(x, k):
        if reverse:
            return jnp.where(sub < SEG - k, pltpu.roll(x, SEG - k, 0), 0.0)
        return jnp.where(sub >= k, pltpu.roll(x, k, 0), 0.0)

    pr, pi = lr, li
    for _ in range(seg_len.bit_length() - 1):
        pr, pi = _cmul(pr, pi, pr, pi)
    fr, fi = shift(er, 1), shift(ei, 1)
    for k in (1, 2, 4):
        sr, si = shift(fr, k), shift(fi, k)
        mr, mi = _cmul(pr, pi, sr, si)
        fr, fi = fr + mr, fi + mi
        pr, pi = _cmul(pr, pi, pr, pi)
    return fr, fi


def s5_fwd(name, u_perm, bbd, cbd, lam):
    t = u_perm.shape[0]
    tt = _tile(t, (512, 256, 128))
    nt, ng, w = t // tt, tt // SEG, SG_STATE
    seg_len = t // SEG
    assert seg_len & (seg_len - 1) == 0 and tt % SEG == 0

    def body(u_ref, b_ref, c_ref, lam_ref, y_ref, h_ref, x_scr, state):
        p, i = pl.program_id(1), pl.program_id(2)
        lr1, li1 = lam_ref[:, 0:w], lam_ref[:, w:2 * w]
        lr, li = jnp.broadcast_to(lr1, (SEG, w)), jnp.broadcast_to(li1, (SEG, w))
        x_scr[...] = _bdot(u_ref[...], b_ref[...], ((1,), (0,)))

        @pl.when(jnp.logical_and(p == 0, i == 0))
        def _():
            state[...] = jnp.zeros_like(state)

        @pl.when(jnp.logical_and(p == 1, i == 0))
        def _():
            sr, si = _segment_init(state[:, 0:w], state[:, w:2 * w], lr1, li1, seg_len, False)
            state[:, 0:w] = sr
            state[:, w:2 * w] = si

        def run(store):
            def step(g, st):
                row = pl.multiple_of(g * SEG, SEG)
                xg = x_scr[pl.ds(row, SEG), :]
                nr = lr * st[0] - li * st[1] + xg[:, 0:w]
                ni = lr * st[1] + li * st[0] + xg[:, w:2 * w]
                if store:
                    h_ref[pl.ds(row, SEG), 0:w] = nr
                    h_ref[pl.ds(row, SEG), w:2 * w] = ni
                return nr, ni

            fin = lax.fori_loop(0, ng, step, (state[:, 0:w], state[:, w:2 * w]))
            state[:, 0:w] = fin[0]
            state[:, w:2 * w] = fin[1]

        @pl.when(p == 0)
        def _():
            run(False)

        @pl.when(p == 1)
        def _():
            run(True)
            y_ref[...] = _bdot(h_ref[...], c_ref[...], ((1,), (0,)))

    return _call(body, name, (_sds((t, D_MODEL)), _sds((t, N_SUPER * 2 * w))), (N_SUPER, 2, nt),
                 [pl.BlockSpec((tt, 128), lambda s, p, i: (i, s)),
                  pl.BlockSpec((None, 128, 2 * w), lambda s, p, i: (s, 0, 0)),
                  pl.BlockSpec((None, 2 * w, 128), lambda s, p, i: (s, 0, 0)),
                  pl.BlockSpec((None, 1, 2 * w), lambda s, p, i: (s, 0, 0))],
                 (pl.BlockSpec((tt, 128), lambda s, p, i: (i * p, s)),
                  pl.BlockSpec((tt, 2 * w), lambda s, p, i: (i * p, s))),
                 scratch=[pltpu.VMEM((tt, 2 * w), f32), pltpu.VMEM((SEG, 2 * w), f32)])(u_perm, bbd, cbd, lam)


def s5_bwd(name, dy_perm, u_perm, h_perm, hprev0, bbd, cbd, lam):
    t = u_perm.shape[0]
    tt = _tile(t, (512, 256, 128))
    nt, ng, w = t // tt, tt // SEG, SG_STATE
    seg_len = t // SEG

    def body(dy_ref, u_ref, h_ref, hp_ref, hp0_ref, b_ref, c_ref, lam_ref, du_ref, db_ref, dc_ref, dl_ref,
             g_scr, state, dl_acc):
        p, i = pl.program_id(1), pl.program_id(2)
        first_tile = jnp.logical_or(p == 0, i == nt - 1)
        lr1, li1 = lam_ref[:, 0:w], -lam_ref[:, w:2 * w]
        lr, li = jnp.broadcast_to(lr1, (SEG, w)), jnp.broadcast_to(li1, (SEG, w))
        g_scr[...] = _bdot(dy_ref[...], c_ref[...], ((1,), (1,)))

        @pl.when(jnp.logical_and(p == 0, i == 0))
        def _():
            state[...] = jnp.zeros_like(state)

        @pl.when(jnp.logical_and(p == 1, i == 0))
        def _():
            sr, si = _segment_init(state[:, 0:w], state[:, w:2 * w], lr1, li1, seg_len, True)
            state[:, 0:w] = sr
            state[:, w:2 * w] = si
            dl_acc[...] = jnp.zeros_like(dl_acc)

        def adj(g, st):
            row = pl.multiple_of(g * SEG, SEG)
            gg = g_scr[pl.ds(row, SEG), :]
            nr = lr * st[0] - li * st[1] + gg[:, 0:w]
            ni = lr * st[1] + li * st[0] + gg[:, w:2 * w]
            return row, nr, ni

        @pl.when(p == 0)
        def _():
            def step(k, st):
                _, nr, ni = adj(ng - 1 - k, st)
                return nr, ni

            fin = lax.fori_loop(0, ng, step, (state[:, 0:w], state[:, w:2 * w]))
            state[:, 0:w] = fin[0]
            state[:, w:2 * w] = fin[1]

        @pl.when(p == 1)
        def _():
            above = jnp.where(first_tile, hp0_ref[...], hp_ref[...])

            def step(k, st):
                g = ng - 1 - k
                row, nr, ni = adj(g, st)
                g_scr[pl.ds(row, SEG), 0:w] = nr
                g_scr[pl.ds(row, SEG), w:2 * w] = ni
                prow = pl.multiple_of(jnp.maximum(g - 1, 0) * SEG, SEG)
                hp = jnp.where(g > 0, h_ref[pl.ds(prow, SEG), :], above)
                pr, pi = hp[:, 0:w], hp[:, w:2 * w]
                return nr, ni, st[2] + nr * pr + ni * pi, st[3] + ni * pr - nr * pi

            fin = lax.fori_loop(0, ng, step, (state[:, 0:w], state[:, w:2 * w], dl_acc[:, 0:w], dl_acc[:, w:2 * w]))
            state[:, 0:w] = fin[0]
            state[:, w:2 * w] = fin[1]
            dl_acc[:, 0:w] = fin[2]
            dl_acc[:, w:2 * w] = fin[3]
            a = g_scr[...]
            du_ref[...] = _bdot(a, b_ref[...], ((1,), (1,)))
            d_b = _bdot(u_ref[...], a, ((0,), (0,)))
            d_c = _bdot(h_ref[...], dy_ref[...], ((0,), (0,)))

            @pl.when(i == 0)
            def _():
                db_ref[...] = d_b
                dc_ref[...] = d_c

            @pl.when(i > 0)
            def _():
                db_ref[...] += d_b
                dc_ref[...] += d_c

            @pl.when(i == nt - 1)
            def _():
                dl_ref[...] = jnp.sum(dl_acc[...], axis=0, keepdims=True)

    tile = lambda s, p, i: (nt - 1 - i, s)
    tile1 = lambda s, p, i: (nt - 1 - i * p, s)
    above = lambda s, p, i: (jnp.maximum((nt - 1 - i * p) * (tt // SEG) - 1, 0), s)
    per_s = lambda s, p, i: (s, 0, 0)
    return _call(body, name, (_sds((t, D_MODEL)), _sds((N_SUPER, 128, 2 * w)), _sds((N_SUPER, 2 * w, 128)),
                              _sds((N_SUPER, 1, 2 * w))), (N_SUPER, 2, nt),
                 [pl.BlockSpec((tt, 128), tile), pl.BlockSpec((tt, 128), tile1), pl.BlockSpec((tt, 2 * w), tile1),
                  pl.BlockSpec((SEG, 2 * w), above), pl.BlockSpec((SEG, 2 * w), lambda s, p, i: (0, s)),
                  pl.BlockSpec((None, 128, 2 * w), per_s), pl.BlockSpec((None, 2 * w, 128), per_s),
                  pl.BlockSpec((None, 1, 2 * w), per_s)],
                 (pl.BlockSpec((tt, 128), tile1), pl.BlockSpec((None, 128, 2 * w), per_s),
                  pl.BlockSpec((None, 2 * w, 128), per_s), pl.BlockSpec((None, 1, 2 * w), per_s)),
                 scratch=[pltpu.VMEM((tt, 2 * w), f32), pltpu.VMEM((SEG, 2 * w), f32),
                          pltpu.VMEM((SEG, 2 * w), f32)])(dy_perm, u_perm, h_perm, h_perm, hprev0, bbd, cbd, lam)


def _s5_disc(a_re, a_im, log_dt, br, bi):
    dt = jnp.exp(log_dt)
    mag = jnp.exp(a_re * dt)
    lr, li = mag * jnp.cos(a_im * dt), mag * jnp.sin(a_im * dt)
    den = a_re * a_re + a_im * a_im
    fr = ((lr - 1.0) * a_re + li * a_im) / den
    fi = (li * a_re - (lr - 1.0) * a_im) / den
    return lr, li, fr * br - fi * bi, fr * bi + fi * br


def s5_disc_fwd(name, a_re, a_im, log_dt, br, bi):
    g, n = SSM_GROUPS, SSM_STATE

    def body(ar, ai, ld, brr, bir, lr, li, bbr, bbi):
        o = _s5_disc(ar[...], ai[...], ld[...], brr[...], bir[...])
        lr[...], li[...], bbr[...], bbi[...] = o

    return _call(body, name, (_sds((g, 1, n)), _sds((g, 1, n)), _sds((g, SSM_GROUP, n)), _sds((g, SSM_GROUP, n))))(
        a_re, a_im, log_dt, br, bi)


def s5_disc_bwd(name, a_re, a_im, log_dt, br, bi, d_lr, d_li, d_bbr, d_bbi):
    g, n = SSM_GROUPS, SSM_STATE

    def body(ar, ai, ld, brr, bir, c1, c2, c3, c4, o1, o2, o3, o4, o5):
        _, vjp = jax.vjp(_s5_disc, ar[...], ai[...], ld[...], brr[...], bir[...])
        o1[...], o2[...], o3[...], o4[...], o5[...] = vjp((c1[...], c2[...], c3[...], c4[...]))

    return _call(body, name, (_sds((g, 1, n)), _sds((g, 1, n)), _sds((g, 1, 1)), _sds((g, SSM_GROUP, n)),
                              _sds((g, SSM_GROUP, n))))(a_re, a_im, log_dt, br, bi, d_lr, d_li, d_bbr, d_bbi)


def gelu_fwd(name, ys_lin, proj, d_skip):
    t, d = ys_lin.shape
    tm = _rows(t)

    def body(y_ref, u_ref, d_ref, o_ref):
        o_ref[...] = _gelu(y_ref[...] + d_ref[...] * u_ref[...])

    return _call(body, name, _sds((t, d)), (t // tm,),
                 [_row_spec(tm, d), pl.BlockSpec((tm, d), lambda i: (i, 4)), _acc_spec(1, d)],
                 _row_spec(tm, d))(ys_lin, proj, d_skip)


def _head_norm(o, hn):
    outs, ns, rs = [], [], []
    for h in range(DN_HEADS):
        oh = o[:, h * 128:(h + 1) * 128]
        r = lax.rsqrt(jnp.mean(oh * oh, axis=-1, keepdims=True) + EPS)
        n = oh * r
        ns.append(n)
        rs.append(r)
        outs.append(n * hn)
    return outs, ns, rs


def merge_fwd(name, proj, o, yg, glu_lin, head_norm, b_glu):
    t = o.shape[0]
    tm = _rows(t)
    d = D_MODEL

    def body(za_ref, zb_ref, ra_ref, rb_ref, o_ref, yg_ref, gl_ref, hn_ref, bg_ref, m_ref):
        on, _, _ = _head_norm(o_ref[...], hn_ref[...])
        y_a = jnp.concatenate(on, axis=1) * _silu(za_ref[...])
        y_b = yg_ref[...] * _sigmoid(gl_ref[...] + bg_ref[...]) * _silu(zb_ref[...])
        m_ref[...] = _sigmoid(ra_ref[...]) * y_a + _sigmoid(rb_ref[...]) * y_b

    pc = lambda c: pl.BlockSpec((tm, d), lambda i: (i, c))
    return _call(body, name, _sds((t, d)), (t // tm,),
                 [pc(3), pc(5), pc(6), pc(7), _row_spec(tm, d), _row_spec(tm, d), _row_spec(tm, d),
                  _acc_spec(1, 128), _acc_spec(1, d)], _row_spec(tm, d))(
        proj, proj, proj, proj, o, yg, glu_lin, head_norm, b_glu)


def merge_bwd(name, proj, o, yg, glu_lin, head_norm, b_glu, dm):
    t = o.shape[0]
    tm = _rows(t)
    d = D_MODEL

    def body(za_ref, zb_ref, ra_ref, rb_ref, o_ref, yg_ref, gl_ref, hn_ref, bg_ref, dm_ref,
             dza_ref, dzb_ref, dra_ref, drb_ref, do_ref, dgl_ref, dyg_ref, dhn_ref, dbg_ref):
        hn = hn_ref[...]
        za, zb, ra, rb = za_ref[...], zb_ref[...], ra_ref[...], rb_ref[...]
        on, ns, rs = _head_norm(o_ref[...], hn)
        onc = jnp.concatenate(on, axis=1)
        sza = _silu(za)
        y_a = onc * sza
        yg = yg_ref[...]
        sgl = _sigmoid(gl_ref[...] + bg_ref[...])
        y2 = yg * sgl
        szb = _silu(zb)
        y_b = y2 * szb
        sra, srb = _sigmoid(ra), _sigmoid(rb)
        dmv = dm_ref[...]
        dra_ref[...] = dmv * y_a * sra * (1.0 - sra)
        drb_ref[...] = dmv * y_b * srb * (1.0 - srb)
        d_ya = dmv * sra
        d_yb = dmv * srb
        dza_ref[...] = d_ya * onc * _dsilu(za)
        dzb_ref[...] = d_yb * y2 * _dsilu(zb)
        d_on = d_ya * sza
        d_y2 = d_yb * szb
        dyg_ref[...] = d_y2 * sgl
        d_gl = d_y2 * yg * sgl * (1.0 - sgl)
        dgl_ref[...] = d_gl
        _accumulate(dbg_ref, jnp.sum(d_gl, axis=0, keepdims=True))
        d_hn = jnp.zeros((1, 128), f32)
        for h in range(DN_HEADS):
            hs = slice(h * 128, (h + 1) * 128)
            dh = d_on[:, hs]
            d_hn = d_hn + jnp.sum(dh * ns[h], axis=0, keepdims=True)
            dn = dh * hn
            do_ref[:, hs] = rs[h] * (dn - ns[h] * jnp.mean(dn * ns[h], axis=-1, keepdims=True))
        _accumulate(dhn_ref, d_hn)

    pc = lambda c: pl.BlockSpec((tm, d), lambda i: (i, c))
    rs_ = _row_spec(tm, d)
    return _call(body, name, (_sds((t, d)),) * 7 + (_sds((1, 128)), _sds((1, d))), (t // tm,),
                 [pc(3), pc(5), pc(6), pc(7), rs_, rs_, rs_, _acc_spec(1, 128), _acc_spec(1, d), rs_],
                 (rs_,) * 7 + (_acc_spec(1, 128), _acc_spec(1, d)))(
        proj, proj, proj, proj, o, yg, glu_lin, head_norm, b_glu, dm)


def gelu_bwd(name, ys_lin, proj, d_skip, dyg_a, dyg_b):
    t, d = ys_lin.shape
    tm = _rows(t)

    def body(y_ref, u_ref, d_ref, a_ref, b_ref, dys_ref, du_ref, dd_ref):
        uv = u_ref[...]
        dys = (a_ref[...] + b_ref[...]) * _dgelu(y_ref[...] + d_ref[...] * uv)
        dys_ref[...] = dys
        du_ref[...] = dys * d_ref[...]
        _accumulate(dd_ref, jnp.sum(dys * uv, axis=0, keepdims=True))

    rs_ = _row_spec(tm, d)
    return _call(body, name, (_sds((t, d)), _sds((t, d)), _sds((1, d))), (t // tm,),
                 [rs_, pl.BlockSpec((tm, d), lambda i: (i, 4)), _acc_spec(1, d), rs_, rs_],
                 (rs_, rs_, _acc_spec(1, d)))(ys_lin, proj, d_skip, dyg_a, dyg_b)


def assemble_dproj(name, dqkv, dza, du_a, du_b, dzb, dra, drb, dbd):
    t = dza.shape[0]
    tm = _rows(t)
    d = D_MODEL

    def body(qkv_ref, za_ref, ua_ref, ub_ref, zb_ref, ra_ref, rb_ref, bd_ref, o_ref):
        o_ref[:, 0:3 * d] = qkv_ref[...].astype(bf16)
        o_ref[:, 3 * d:4 * d] = za_ref[...].astype(bf16)
        o_ref[:, 4 * d:5 * d] = (ua_ref[...] + ub_ref[...]).astype(bf16)
        o_ref[:, 5 * d:6 * d] = zb_ref[...].astype(bf16)
        o_ref[:, 6 * d:7 * d] = ra_ref[...].astype(bf16)
        o_ref[:, 7 * d:8 * d] = rb_ref[...].astype(bf16)
        o_ref[:, 8 * d:8 * d + 128] = bd_ref[...].astype(bf16)

    rs_ = _row_spec(tm, d)
    return _call(body, name, _sds((t, W_PAD), bf16), (t // tm,),
                 [_row_spec(tm, 3 * d), rs_, rs_, rs_, rs_, rs_, rs_, _row_spec(tm, 128)], _row_spec(tm, W_PAD))(
        dqkv, dza, du_a, du_b, dzb, dra, drb, dbd)


def adamw(name, w, g, m, v):
    r, c = w.shape
    tm = _tile(r, (512, 256, 128, 64, 32, 16, 8))
    c1 = 1.0 / (1.0 - ADAM_B1 ** ADAM_STEP)
    c2 = 1.0 / (1.0 - ADAM_B2 ** ADAM_STEP)

    def body(w_ref, g_ref, m_ref, v_ref, d_ref, nm_ref, nv_ref):
        gv = g_ref[...]
        nm = ADAM_B1 * m_ref[...] + (1.0 - ADAM_B1) * gv
        nv = ADAM_B2 * v_ref[...] + (1.0 - ADAM_B2) * (gv * gv)
        d_ref[...] = -ADAM_LR * ((nm * c1) / (jnp.sqrt(nv * c2) + ADAM_EPS) + ADAM_WD * w_ref[...])
        nm_ref[...] = nm
        nv_ref[...] = nv

    sp = pl.BlockSpec((tm, c), lambda i: (i, 0))
    return _call(body, name, (_sds((r, c)),) * 3, (r // tm,), [sp] * 4, (sp,) * 3)(w, g, m, v)


def _coords():
    return lax.axis_index("x"), lax.axis_index("y"), lax.axis_index("c")


def _flip(me, k):
    x, y, c = me
    return (1 - x if k & 4 else x, 1 - y if k & 2 else y, 1 - c if k & 1 else c)


def _lin(dev):
    return 4 * dev[0] + 2 * dev[1] + dev[2]


def all_gather(name, shard):
    r, c = shard.shape

    def body(x_ref, o_ref, send_sems, recv_sems, local_sem):
        me = _coords()
        mine = pltpu.make_async_copy(x_ref, o_ref.at[_lin(me)], local_sem)
        mine.start()
        sends = []
        for k in range(1, N_DEV):
            cp = pltpu.make_async_remote_copy(src_ref=x_ref, dst_ref=o_ref.at[_lin(me)], send_sem=send_sems.at[k - 1],
                                              recv_sem=recv_sems.at[k - 1], device_id=_flip(me, k), device_id_type=MESH)
            cp.start()
            sends.append(cp)
        for k in range(1, N_DEV):
            src = _lin(_flip(me, k))
            pltpu.make_async_remote_copy(src_ref=x_ref, dst_ref=o_ref.at[src], send_sem=send_sems.at[k - 1],
                                         recv_sem=recv_sems.at[k - 1], device_id=_flip(me, k),
                                         device_id_type=MESH).wait_recv()
        for cp in sends:
            cp.wait_send()
        mine.wait()

    any_spec = pl.BlockSpec(memory_space=pl.ANY)
    return _call(body, name, _sds((N_DEV, r, c), shard.dtype), in_specs=[any_spec], out_specs=any_spec,
                 scratch=[pltpu.SemaphoreType.DMA((N_DEV - 1,)), pltpu.SemaphoreType.DMA((N_DEV - 1,)),
                          pltpu.SemaphoreType.DMA(())])(shard)


def exchange_blocks(name, blocks):
    _, r, c = blocks.shape

    def body(x_ref, o_ref, send_sems, recv_sems, local_sem):
        me = _coords()
        my = _lin(me)
        mine = pltpu.make_async_copy(x_ref.at[my], o_ref.at[my], local_sem)
        mine.start()
        sends = []
        for k in range(1, N_DEV):
            peer = _flip(me, k)
            cp = pltpu.make_async_remote_copy(src_ref=x_ref.at[_lin(peer)], dst_ref=o_ref.at[my],
                                              send_sem=send_sems.at[k - 1], recv_sem=recv_sems.at[k - 1],
                                              device_id=peer, device_id_type=MESH)
            cp.start()
            sends.append(cp)
        for k in range(1, N_DEV):
            peer = _flip(me, k)
            pltpu.make_async_remote_copy(src_ref=x_ref.at[my], dst_ref=o_ref.at[_lin(peer)],
                                         send_sem=send_sems.at[k - 1], recv_sem=recv_sems.at[k - 1],
                                         device_id=peer, device_id_type=MESH).wait_recv()
        for cp in sends:
            cp.wait_send()
        mine.wait()

    any_spec = pl.BlockSpec(memory_space=pl.ANY)
    return _call(body, name, _sds(blocks.shape, blocks.dtype), in_specs=[any_spec], out_specs=any_spec,
                 scratch=[pltpu.SemaphoreType.DMA((N_DEV - 1,)), pltpu.SemaphoreType.DMA((N_DEV - 1,)),
                          pltpu.SemaphoreType.DMA(())])(blocks)


def sum_slots(name, slots):
    _, r, c = slots.shape
    tm = _tile(r, (240, 256, 128, 64, 32, 16))

    def body(s_ref, o_ref):
        acc = s_ref[0].astype(f32)
        for d in range(1, N_DEV):
            acc = acc + s_ref[d].astype(f32)
        o_ref[...] = acc

    return _call(body, name, _sds((r, c)), (r // tm,), [pl.BlockSpec((N_DEV, tm, c), lambda i: (0, i, 0))],
                 pl.BlockSpec((tm, c), lambda i: (i, 0)))(slots)


def _rows_layout(col8, t):
    return col8.T.reshape(DN_HEADS, t // CHUNK, CHUNK)


def _blockdiag(m):
    g, a, b = m.shape
    m = m.reshape(N_SUPER, SUPER, a, b)
    out = jnp.einsum("sgab,gh->sgahb", m, jnp.eye(SUPER, dtype=m.dtype))
    return out.reshape(N_SUPER, SUPER * a, SUPER * b)


def _diag_blocks(m, a, b):
    m = m.reshape(N_SUPER, SUPER, a, SUPER, b)
    return jnp.einsum("sgahb,gh->sgab", m, jnp.eye(SUPER, dtype=m.dtype)).reshape(SSM_GROUPS, a, b)


def _s5_params(p, li):
    tag = f"l{li}"
    n = SSM_STATE
    a_re = p["ssm_a_re"].reshape(SSM_GROUPS, 1, n)
    a_im = p["ssm_a_im"].reshape(SSM_GROUPS, 1, n)
    log_dt = p["ssm_log_dt"].reshape(SSM_GROUPS, 1, 1)
    br = jnp.swapaxes(p["ssm_b_re"], 1, 2)
    bi = jnp.swapaxes(p["ssm_b_im"], 1, 2)
    lr, li_, bbr, bbi = s5_disc_fwd("s5_disc_" + tag, a_re, a_im, log_dt, br, bi)
    lam = jnp.concatenate([lr.reshape(N_SUPER, 1, SG_STATE), li_.reshape(N_SUPER, 1, SG_STATE)], axis=-1)
    bbd = jnp.concatenate([_blockdiag(bbr), _blockdiag(bbi)], axis=-1).astype(bf16)
    c_re = jnp.swapaxes(p["ssm_c_re"], 1, 2)
    c_im = jnp.swapaxes(p["ssm_c_im"], 1, 2)
    cbd = jnp.concatenate([_blockdiag(c_re), -_blockdiag(c_im)], axis=1).astype(bf16)
    return dict(a_re=a_re, a_im=a_im, log_dt=log_dt, br=br, bi=bi, lam=lam, bbd=bbd, cbd=cbd)


def layer_fwd(x, p, li):
    tag = f"l{li}"
    t = x.shape[0]
    d = D_MODEL
    h = rmsnorm_fwd("norm_pre_" + tag, x, p["norm_pre"])
    proj = mm_nn("proj_" + tag, h, p["w_all"])
    q, k, v, bg = conv_qkv_fwd("conv_" + tag, proj, p["conv_w"], p["a_log"], p["dt_bias"])
    b_rows = _rows_layout(bg[:, 0:DN_HEADS], t)
    g_rows = _rows_layout(bg[:, DN_HEADS:2 * DN_HEADS], t)
    o, s_all = delta_fwd("delta_" + tag, q, k, v, g_rows, b_rows)
    sp = _s5_params(p, li)
    u_perm = _perm_rows(proj[:, 4 * d:5 * d])
    ys_perm, hs = s5_fwd("s5_" + tag, u_perm, sp["bbd"], sp["cbd"], sp["lam"])
    ys_lin = _unperm_rows(ys_perm)
    yg = gelu_fwd("gelu_" + tag, ys_lin, proj, p["ssm_d"])
    glu_lin = mm_nn("glu_" + tag, yg, p["w_glu"])
    merged = merge_fwd("merge_" + tag, proj, o, yg, glu_lin, p["head_norm"], p["b_glu"])
    out = mm_nn("out_" + tag, merged, p["w_out"])
    y = residual_norm_fwd("norm_post_" + tag, x, out, p["norm_post"])
    saved = dict(x=x, h=h, proj=proj, q=q, k=k, v=v, g_rows=g_rows, b_rows=b_rows, o=o, s_all=s_all, sp=sp, u_perm=u_perm,
                 hs=hs, ys_lin=ys_lin, yg=yg, glu_lin=glu_lin, merged=merged, out=out)
    return y, saved


def layer_bwd(dy, p, s, li):
    tag = f"l{li}"
    t = dy.shape[0]
    d = D_MODEL
    sp = s["sp"]
    gr = {}
    d_out, gr["norm_post"] = post_norm_bwd("norm_post_b_" + tag, s["out"], p["norm_post"], dy)
    d_merged = mm_nt("out_b_" + tag, d_out, p["w_out"])
    gr["w_out"] = mm_tn("out_w_" + tag, s["merged"], d_out, bf16)
    (dza, dzb, dra, drb, d_o, d_glu, dyg_a, gr["head_norm"], gr["b_glu"]) = merge_bwd(
        "merge_b_" + tag, s["proj"], s["o"], s["yg"], s["glu_lin"], p["head_norm"], p["b_glu"], d_merged)
    dyg_b = mm_nt("glu_b_" + tag, d_glu, p["w_glu"])
    gr["w_glu"] = mm_tn("glu_w_" + tag, s["yg"], d_glu, bf16)
    d_ys, du_a, gr["ssm_d"] = gelu_bwd("gelu_b_" + tag, s["ys_lin"], s["proj"], p["ssm_d"], dyg_a, dyg_b)
    hprev0 = jnp.concatenate([jnp.zeros((1, s["hs"].shape[1]), f32), s["hs"][-SEG:-1]], axis=0)
    du_perm, d_bbd, d_cbd, d_lam = s5_bwd("s5_b_" + tag, _perm_rows(d_ys), s["u_perm"], s["hs"], hprev0, sp["bbd"],
                                          sp["cbd"], sp["lam"])
    du_b = _unperm_rows(du_perm)
    d_c_re = _diag_blocks(d_cbd[:, 0:SG_STATE, :], SSM_STATE, SSM_GROUP)
    d_c_im = -_diag_blocks(d_cbd[:, SG_STATE:, :], SSM_STATE, SSM_GROUP)
    gr["ssm_c_re"] = jnp.swapaxes(d_c_re, 1, 2)
    gr["ssm_c_im"] = jnp.swapaxes(d_c_im, 1, 2)
    d_bbr = _diag_blocks(d_bbd[:, :, 0:SG_STATE], SSM_GROUP, SSM_STATE)
    d_bbi = _diag_blocks(d_bbd[:, :, SG_STATE:], SSM_GROUP, SSM_STATE)
    d_lr = d_lam[:, :, 0:SG_STATE].reshape(SSM_GROUPS, 1, SSM_STATE)
    d_li = d_lam[:, :, SG_STATE:].reshape(SSM_GROUPS, 1, SSM_STATE)
    d_are, d_aim, d_ldt, d_br, d_bi = s5_disc_bwd("s5_disc_b_" + tag, sp["a_re"], sp["a_im"], sp["log_dt"], sp["br"],
                                                  sp["bi"], d_lr, d_li, d_bbr, d_bbi)
    gr["ssm_a_re"] = d_are.reshape(SSM_GROUPS, SSM_STATE)
    gr["ssm_a_im"] = d_aim.reshape(SSM_GROUPS, SSM_STATE)
    gr["ssm_log_dt"] = d_ldt.reshape(SSM_GROUPS)
    gr["ssm_b_re"] = jnp.swapaxes(d_br, 1, 2)
    gr["ssm_b_im"] = jnp.swapaxes(d_bi, 1, 2)
    dq, dk, dv, dg_rows, db_rows = delta_bwd("delta_b_" + tag, s["q"], s["k"], s["v"], s["g_rows"], s["b_rows"],
                                             s["s_all"], d_o)
    dbg = jnp.concatenate([db_rows.reshape(DN_HEADS, t).T, dg_rows.reshape(DN_HEADS, t).T,
                           jnp.zeros((t, 128 - 2 * DN_HEADS), f32)], axis=1)
    dc, gr["conv_w"], dbd, dab = conv_qkv_bwd("conv_b_" + tag, s["proj"], p["conv_w"], p["a_log"], p["dt_bias"],
                                              dq, dk, dv, dbg)
    gr["a_log"] = dab[0, DN_HEADS:2 * DN_HEADS]
    gr["dt_bias"] = dab[1, DN_HEADS:2 * DN_HEADS]
    dqkv = conv_transpose("conv_t_" + tag, dc, p["conv_w"])
    dproj = assemble_dproj("dproj_" + tag, dqkv, dza, du_a, du_b, dzb, dra, drb, dbd)
    d_h = mm_nt("proj_b_" + tag, dproj, p["w_all"])
    gr["w_all"] = mm_tn("proj_w_" + tag, s["h"], dproj, bf16)
    dx, gr["norm_pre"] = rmsnorm_bwd("norm_pre_b_" + tag, s["x"], p["norm_pre"], d_h, dy)
    return dx, gr


REPL = ["norm_pre", "a_log", "dt_bias", "head_norm", "ssm_a_re", "ssm_a_im", "ssm_log_dt", "ssm_b_re", "ssm_b_im",
        "ssm_c_re", "ssm_c_im", "ssm_d", "b_glu", "norm_post"]
SHARDED = ["w_in", "conv_w", "w_glu", "w_out"]
ALL_W = ["norm_pre", "w_in", "conv_w", "a_log", "dt_bias", "head_norm", "ssm_a_re", "ssm_a_im", "ssm_log_dt",
         "ssm_b_re", "ssm_b_im", "ssm_c_re", "ssm_c_im", "ssm_d", "w_glu", "b_glu", "w_out", "norm_post"]
PACK_W = 1024


def _pack_flat(arrs, rows):
    flat = jnp.concatenate([a.reshape(-1) for a in arrs])
    return jnp.pad(flat, (0, rows * PACK_W - flat.shape[0])).reshape(rows, PACK_W)


def _flat_rows(arrs, mult=8):
    n = sum(math.prod(a.shape) for a in arrs)
    rows = -(-n // PACK_W)
    return -(-rows // mult) * mult


def _unpack(flat, shapes):
    out, off = [], 0
    for sh in shapes:
        n = math.prod(sh)
        out.append(flat[off:off + n].reshape(sh))
        off += n
    return out


def _permute_w(w):
    return jnp.concatenate([w[:, 0:4096], w[:, 4112:W_COLS], w[:, 4096:4112],
                            jnp.zeros((w.shape[0], W_PAD - W_COLS), w.dtype)], axis=1)


def _unpermute_w(g):
    return jnp.concatenate([g[:, 0:4096], g[:, 8192:8208], g[:, 4096:8192]], axis=1)


def kernel(x, norm_pre, w_in, conv_w, a_log, dt_bias, head_norm, ssm_a_re, ssm_a_im, ssm_log_dt, ssm_b_re, ssm_b_im, ssm_c_re, ssm_c_im, ssm_d, w_glu, b_glu, w_out, norm_post, loss_target, m_norm_pre, m_w_in, m_conv_w, m_a_log, m_dt_bias, m_head_norm, m_ssm_a_re, m_ssm_a_im, m_ssm_log_dt, m_ssm_b_re, m_ssm_b_im, m_ssm_c_re, m_ssm_c_im, m_ssm_d, m_w_glu, m_b_glu, m_w_out, m_norm_post, v_norm_pre, v_w_in, v_conv_w, v_a_log, v_dt_bias, v_head_norm, v_ssm_a_re, v_ssm_a_im, v_ssm_log_dt, v_ssm_b_re, v_ssm_b_im, v_ssm_c_re, v_ssm_c_im, v_ssm_d, v_w_glu, v_b_glu, v_w_out, v_norm_post):
    loc = dict(locals())
    w = {n: loc[n] for n in ALL_W}
    m = {n: loc["m_" + n] for n in ALL_W}
    v = {n: loc["v_" + n] for n in ALL_W}
    depth = w_in.shape[0]
    wc = w_in.shape[2]
    cc = conv_w.shape[2]
    wr = w_glu.shape[1]

    tail = wc - PACK_W
    conv_hi = conv_w.astype(bf16)
    conv_mid = (conv_w - conv_hi.astype(f32)).astype(bf16)
    conv_lo = (conv_w - conv_hi.astype(f32) - conv_mid.astype(f32)).astype(bf16)
    w_in_b = w_in.astype(bf16)
    misc_shapes = [(depth, D_MODEL, tail), (3, depth, CONV_K, cc)]
    misc_rows = _flat_rows([_sds(sh, bf16) for sh in misc_shapes], 16)
    n_main, n_sq = depth * D_MODEL, depth * wr
    shard = jnp.concatenate([w_in_b[:, :, :PACK_W].reshape(n_main, PACK_W), w_glu.astype(bf16).reshape(n_sq, PACK_W),
                             w_out.astype(bf16).reshape(n_sq, PACK_W),
                             _pack_flat([w_in_b[:, :, PACK_W:], jnp.stack([conv_hi, conv_mid, conv_lo])], misc_rows)])
    gathered = all_gather("gather_weights", shard)
    miscs = [_unpack(gathered[dv, n_main + 2 * n_sq:].reshape(-1), misc_shapes) for dv in range(N_DEV)]
    w_in_full = jnp.concatenate(
        [pc for dv in range(N_DEV) for pc in (gathered[dv, :n_main].reshape(depth, D_MODEL, PACK_W), miscs[dv][0])],
        axis=2)
    w_glu_full = gathered[:, n_main:n_main + n_sq].reshape(N_DEV, depth, wr, D_MODEL)
    w_glu_full = jnp.swapaxes(w_glu_full, 0, 1).reshape(depth, N_DEV * wr, D_MODEL)
    w_out_full = gathered[:, n_main + n_sq:n_main + 2 * n_sq].reshape(N_DEV, depth, wr, D_MODEL)
    w_out_full = jnp.swapaxes(w_out_full, 0, 1).reshape(depth, N_DEV * wr, D_MODEL)
    conv_full = jnp.concatenate([mi[1][0].astype(f32) + mi[1][1].astype(f32) + mi[1][2].astype(f32)
                                 for mi in miscs], axis=2)

    def layer_params(li):
        return dict(norm_pre=norm_pre[li].reshape(1, -1), w_all=_permute_w(w_in_full[li]), conv_w=conv_full[li],
                    a_log=jnp.pad(a_log[li], (DN_HEADS, 128 - 2 * DN_HEADS)).reshape(1, 128),
                    dt_bias=jnp.pad(dt_bias[li], (DN_HEADS, 128 - 2 * DN_HEADS)).reshape(1, 128),
                    head_norm=head_norm[li].reshape(1, -1), ssm_a_re=ssm_a_re[li], ssm_a_im=ssm_a_im[li],
                    ssm_log_dt=ssm_log_dt[li], ssm_b_re=ssm_b_re[li], ssm_b_im=ssm_b_im[li], ssm_c_re=ssm_c_re[li],
                    ssm_c_im=ssm_c_im[li], ssm_d=ssm_d[li].reshape(1, -1), w_glu=w_glu_full[li],
                    b_glu=b_glu[li].reshape(1, -1), w_out=w_out_full[li], norm_post=norm_post[li].reshape(1, -1))

    act = x[0]
    saved, params = [], []
    for li in range(depth):
        params.append(layer_params(li))
        act, sv = layer_fwd(act, params[li], li)
        saved.append(sv)
    loss_part, dy = loss_head("loss_head", act, loss_target[0])
    grads = [None] * depth
    for li in reversed(range(depth)):
        dy, grads[li] = layer_bwd(dy, params[li], saved[li], li)
    grad_x = dy[None]
    loss = lax.psum(loss_part[0, 0], ("x", "y", "c"))

    def stack(name):
        return jnp.stack([grads[li][name] for li in range(depth)])

    g_w_in = jnp.stack([_unpermute_w(grads[li]["w_all"]) for li in range(depth)])
    g_conv = stack("conv_w").astype(bf16)
    g_glu = stack("w_glu").reshape(depth, N_DEV, wr, D_MODEL)
    g_out = stack("w_out").reshape(depth, N_DEV, wr, D_MODEL)
    repl_shapes = [w[n].shape for n in REPL]
    repl_rows = _flat_rows([w[n] for n in REPL], 8 * N_DEV)
    g_repl = _pack_flat([stack(n).reshape(w[n].shape) for n in REPL], repl_rows).astype(bf16)
    rr = repl_rows // N_DEV
    gmisc_shapes = [(depth, D_MODEL, tail), (depth, CONV_K, cc)]
    gmisc_rows = _flat_rows([_sds(sh, bf16) for sh in gmisc_shapes], 16)
    blocks = jnp.stack([jnp.concatenate([
        g_w_in[:, :, dv * wc:dv * wc + PACK_W].reshape(n_main, PACK_W), g_glu[:, dv].reshape(n_sq, PACK_W),
        g_out[:, dv].reshape(n_sq, PACK_W),
        _pack_flat([g_w_in[:, :, dv * wc + PACK_W:(dv + 1) * wc], g_conv[:, :, dv * cc:(dv + 1) * cc]], gmisc_rows),
        g_repl[dv * rr:(dv + 1) * rr]]) for dv in range(N_DEV)])
    pad_rows = -blocks.shape[1] % 16
    if pad_rows:
        blocks = jnp.pad(blocks, ((0, 0), (0, pad_rows), (0, 0)))
    slots = exchange_blocks("scatter_grads", blocks)
    mine = sum_slots("sum_grads", slots)
    o1, o2, o3, o4 = n_main, n_main + n_sq, n_main + 2 * n_sq, n_main + 2 * n_sq + gmisc_rows
    gs_tail, gs_conv = _unpack(mine[o3:o4].reshape(-1), gmisc_shapes)
    gs_w_in = jnp.concatenate([mine[:o1].reshape(depth, D_MODEL, PACK_W), gs_tail], axis=2)
    gs_glu = mine[o1:o2].reshape(depth, wr, D_MODEL)
    gs_out = mine[o2:o3].reshape(depth, wr, D_MODEL)
    g_repl_full = all_gather("gather_repl_grads", mine[o4:o4 + rr]).reshape(-1)
    g = dict(zip(REPL, _unpack(g_repl_full, repl_shapes)))
    g.update(w_in=gs_w_in, w_glu=gs_glu, w_out=gs_out, conv_w=gs_conv)

    delta, new_m, new_v = {}, {}, {}
    for n in SHARDED:
        sh = w[n].shape
        two_d = (-1, sh[-1])
        delta[n], new_m[n], new_v[n] = [o.reshape(sh) for o in adamw(
            "adamw_" + n, w[n].reshape(two_d), g[n].reshape(two_d), m[n].reshape(two_d), v[n].reshape(two_d))]
    packed = [_pack_flat([src[n] for n in REPL], repl_rows) for src in (w, g, m, v)]
    outs = adamw("adamw_repl", *packed)
    for dst, o in zip((delta, new_m, new_v), outs):
        dst.update(zip(REPL, _unpack(o.reshape(-1), repl_shapes)))
    return (loss, grad_x, *[g[n] for n in ALL_W], *[delta[n] for n in ALL_W], *[new_m[n] for n in ALL_W],
            *[new_v[n] for n in ALL_W])
```

```python
import math

import jax
import jax.numpy as jnp
from jax import lax
from jax.experimental import pallas as pl
from jax.experimental.pallas import tpu as pltpu

f32 = jnp.float32
bf16 = jnp.bfloat16

D_MODEL = 1024
DEPTH = 4
N_DEV = 8
DN_HEADS = 8
HEAD_DIM = 128
CHUNK = 64
CONV_K = 4
SSM_GROUPS = 64
SSM_GROUP = 16
SSM_STATE = 64
SUPER = 8
N_SUPER = SSM_GROUPS // SUPER
SG_STATE = SUPER * SSM_STATE
EPS = 1e-6
W_COLS = 8208
W_PAD = 8320
ADAM_LR, ADAM_B1, ADAM_B2, ADAM_EPS, ADAM_WD, ADAM_STEP = 0.001, 0.9, 0.999, 1e-08, 0.01, 10
VMEM_LIMIT = 56 * 1024 * 1024
MESH = pl.DeviceIdType.MESH
HIGH = lax.Precision.HIGH
DELTA_HB = 8


def _call(body, name, out_shape, grid=None, in_specs=None, out_specs=None, scratch=(), **kw):
    args = dict(out_shape=out_shape, name=name, scratch_shapes=list(scratch),
                compiler_params=pltpu.CompilerParams(vmem_limit_bytes=VMEM_LIMIT, **kw))
    if grid is not None:
        args.update(grid=grid, in_specs=in_specs, out_specs=out_specs)
    else:
        if in_specs is not None:
            args.update(in_specs=in_specs)
        if out_specs is not None:
            args.update(out_specs=out_specs)
    return pl.pallas_call(body, **args)


def _sds(shape, dtype=f32):
    return jax.ShapeDtypeStruct(tuple(shape), dtype)


def _sigmoid(x):
    return 1.0 / (1.0 + jnp.exp(-x))


def _silu(x):
    return x * _sigmoid(x)


def _dsilu(x):
    s = _sigmoid(x)
    return s * (1.0 + x * (1.0 - s))


_GELU_C = math.sqrt(2.0 / math.pi)


def _gelu(x):
    return 0.5 * x * (1.0 + jnp.tanh(_GELU_C * (x + 0.044715 * x * x * x)))


def _dgelu(x):
    t = jnp.tanh(_GELU_C * (x + 0.044715 * x * x * x))
    return 0.5 * (1.0 + t) + 0.5 * x * (1.0 - t * t) * _GELU_C * (1.0 + 3 * 0.044715 * x * x)


def _softplus(x):
    return jnp.maximum(x, 0.0) + jnp.log(1.0 + jnp.exp(-jnp.abs(x)))


def _bdot(a, b, dn):
    return lax.dot_general(a.astype(bf16), b.astype(bf16), (dn, ((), ())), preferred_element_type=f32)


def _matmul(name, a, b, *, dn, grid, a_spec, b_spec, o_spec, o_shape, o_dtype=f32):
    nk = grid[-1]

    def body(a_ref, b_ref, o_ref, acc_ref):
        p = _bdot(a_ref[...], b_ref[...], dn)
        if nk == 1:
            o_ref[...] = p.astype(o_dtype)
        else:
            k = pl.program_id(len(grid) - 1)

            @pl.when(k == 0)
            def _():
                acc_ref[...] = p

            @pl.when(k > 0)
            def _():
                acc_ref[...] += p

            @pl.when(k == nk - 1)
            def _():
                o_ref[...] = acc_ref[...].astype(o_dtype)

    blk = tuple(d for d in o_spec.block_shape if d is not None)
    return _call(body, name, _sds(o_shape, o_dtype), grid, [a_spec, b_spec], o_spec,
                 scratch=[pltpu.VMEM(blk if nk > 1 else (8, 128), f32)])(a, b)


def _tile(n, pref):
    for t in pref:
        if n % t == 0:
            return t
    return n


def mm_nn(name, a, b):
    m, k = a.shape
    n = b.shape[1]
    tm, tn, tk = _tile(m, (512, 256)), _tile(n, (1664, 1024, 512)), _tile(k, (1664, 1024))
    return _matmul(name, a, b, dn=((1,), (0,)), grid=(m // tm, n // tn, k // tk),
                   a_spec=pl.BlockSpec((tm, tk), lambda i, j, l: (i, l)),
                   b_spec=pl.BlockSpec((tk, tn), lambda i, j, l: (l, j)),
                   o_spec=pl.BlockSpec((tm, tn), lambda i, j, l: (i, j)), o_shape=(m, n))


def mm_nt(name, a, b):
    m, k = a.shape
    n = b.shape[0]
    tm, tn, tk = _tile(m, (512, 256)), _tile(n, (1024, 512)), _tile(k, (1664, 1024))
    return _matmul(name, a, b, dn=((1,), (1,)), grid=(m // tm, n // tn, k // tk),
                   a_spec=pl.BlockSpec((tm, tk), lambda i, j, l: (i, l)),
                   b_spec=pl.BlockSpec((tn, tk), lambda i, j, l: (j, l)),
                   o_spec=pl.BlockSpec((tm, tn), lambda i, j, l: (i, j)), o_shape=(m, n))


def mm_tn(name, a, b, o_dtype=f32):
    k, m = a.shape
    n = b.shape[1]
    tm, tn, tk = _tile(m, (512,)), _tile(n, (1664, 1024, 512)), _tile(k, (512, 256))
    return _matmul(name, a, b, dn=((0,), (0,)), grid=(m // tm, n // tn, k // tk),
                   a_spec=pl.BlockSpec((tk, tm), lambda i, j, l: (l, i)),
                   b_spec=pl.BlockSpec((tk, tn), lambda i, j, l: (l, j)),
                   o_spec=pl.BlockSpec((tm, tn), lambda i, j, l: (i, j)), o_shape=(m, n), o_dtype=o_dtype)


def _rows(t):
    return _tile(t, (256,))


def _row_spec(tm, w):
    return pl.BlockSpec((tm, w), lambda i: (i, 0))


def _acc_spec(r, w):
    return pl.BlockSpec((r, w), lambda i: (0, 0))


def _accumulate(ref, val):
    @pl.when(pl.program_id(0) == 0)
    def _():
        ref[...] = val

    @pl.when(pl.program_id(0) > 0)
    def _():
        ref[...] += val


def rmsnorm_fwd(name, x, gain):
    t, d = x.shape
    tm = _rows(t)

    def body(x_ref, g_ref, o_ref):
        xv = x_ref[...]
        r = lax.rsqrt(jnp.mean(xv * xv, axis=-1, keepdims=True) + EPS)
        o_ref[...] = xv * r * g_ref[...]

    return _call(body, name, _sds((t, d)), (t // tm,), [_row_spec(tm, d), _acc_spec(1, d)], _row_spec(tm, d))(x, gain)


def rmsnorm_bwd(name, x, gain, dn, dres):
    t, d = x.shape
    tm = _rows(t)

    def body(x_ref, g_ref, dn_ref, dr_ref, dx_ref, dg_ref):
        xv = x_ref[...]
        r = lax.rsqrt(jnp.mean(xv * xv, axis=-1, keepdims=True) + EPS)
        n = xv * r
        dnv = dn_ref[...]
        _accumulate(dg_ref, jnp.sum(dnv * n, axis=0, keepdims=True))
        dng = dnv * g_ref[...]
        dx_ref[...] = dr_ref[...] + r * (dng - n * jnp.mean(dng * n, axis=-1, keepdims=True))

    return _call(body, name, (_sds((t, d)), _sds((1, d))), (t // tm,),
                 [_row_spec(tm, d), _acc_spec(1, d), _row_spec(tm, d), _row_spec(tm, d)],
                 (_row_spec(tm, d), _acc_spec(1, d)))(x, gain, dn, dres)


def residual_norm_fwd(name, x, out, gain):
    t, d = x.shape
    tm = _rows(t)

    def body(x_ref, o_ref, g_ref, y_ref):
        ov = o_ref[...]
        r = lax.rsqrt(jnp.mean(ov * ov, axis=-1, keepdims=True) + EPS)
        y_ref[...] = x_ref[...] + ov * r * g_ref[...]

    return _call(body, name, _sds((t, d)), (t // tm,), [_row_spec(tm, d), _row_spec(tm, d), _acc_spec(1, d)],
                 _row_spec(tm, d))(x, out, gain)


def post_norm_bwd(name, out, gain, dy):
    t, d = out.shape
    tm = _rows(t)

    def body(o_ref, g_ref, dy_ref, do_ref, dg_ref):
        ov = o_ref[...]
        r = lax.rsqrt(jnp.mean(ov * ov, axis=-1, keepdims=True) + EPS)
        n = ov * r
        dyv = dy_ref[...]
        _accumulate(dg_ref, jnp.sum(dyv * n, axis=0, keepdims=True))
        dng = dyv * g_ref[...]
        do_ref[...] = r * (dng - n * jnp.mean(dng * n, axis=-1, keepdims=True))

    return _call(body, name, (_sds((t, d)), _sds((1, d))), (t // tm,),
                 [_row_spec(tm, d), _acc_spec(1, d), _row_spec(tm, d)],
                 (_row_spec(tm, d), _acc_spec(1, d)))(out, gain, dy)


def loss_head(name, y, target):
    t, d = y.shape
    tm = _rows(t)

    def body(y_ref, t_ref, l_ref, dy_ref):
        e = y_ref[...] - t_ref[...]
        dy_ref[...] = e * (1.0 / d)
        s = jnp.sum(jnp.sum(e * e, axis=1, keepdims=True), axis=0, keepdims=True) * (0.5 / d)
        _accumulate(l_ref, s)

    return _call(body, name, (_sds((1, 1)), _sds((t, d))), (t // tm,),
                 [_row_spec(tm, d), _row_spec(tm, d)], (_acc_spec(1, 1), _row_spec(tm, d)))(y, target)


def _prev_spec(tm, w):
    return pl.BlockSpec((8, w), lambda i: (jnp.maximum(i * (tm // 8) - 1, 0), 0))


def _next_spec(tm, w, t):
    return pl.BlockSpec((8, w), lambda i: (jnp.minimum((i + 1) * (tm // 8), t // 8 - 1), 0))


def _fill_pad(pad_ref, prev_ref, cur_ref, tm):
    keep = (pl.program_id(0) > 0).astype(f32)
    pad_ref[0:8, :] = prev_ref[...] * keep
    pad_ref[8:8 + tm, :] = cur_ref[...]


def _conv_block(pad_ref, w_ref, cb, tm):
    cs = slice(cb * 128, (cb + 1) * 128)
    acc = pad_ref[pl.ds(8 - (CONV_K - 1), tm), cs] * w_ref[0:1, cs]
    for j in range(1, CONV_K):
        acc = acc + pad_ref[pl.ds(8 - (CONV_K - 1) + j, tm), cs] * w_ref[j:j + 1, cs]
    return acc


def conv_qkv_fwd(name, proj, conv_w, a_log, dt_bias):
    t = proj.shape[0]
    tm = _rows(t)
    scale = HEAD_DIM ** -0.5

    def body(cur_ref, prev_ref, w_ref, bd_ref, al_ref, db_ref, q_ref, k_ref, v_ref, bg_ref, pad_ref):
        _fill_pad(pad_ref, prev_ref, cur_ref, tm)
        for cb in range(3 * DN_HEADS):
            s = _silu(_conv_block(pad_ref, w_ref, cb, tm))
            hs = slice((cb % DN_HEADS) * 128, (cb % DN_HEADS + 1) * 128)
            if cb < DN_HEADS:
                q_ref[:, hs] = s * (lax.rsqrt(jnp.sum(s * s, axis=-1, keepdims=True) + EPS) * scale)
            elif cb < 2 * DN_HEADS:
                k_ref[:, hs] = s * lax.rsqrt(jnp.sum(s * s, axis=-1, keepdims=True) + EPS)
            else:
                v_ref[:, hs] = s
        bd = bd_ref[...]
        beta = _sigmoid(bd)
        g = -jnp.exp(al_ref[...]) * _softplus(bd + db_ref[...])
        lane = lax.broadcasted_iota(jnp.int32, bd.shape, 1)
        bg_ref[...] = jnp.where(lane < DN_HEADS, beta, jnp.where(lane < 2 * DN_HEADS, g, 0.0))

    w3 = 3 * D_MODEL
    return _call(body, name, (_sds((t, D_MODEL)),) * 3 + (_sds((t, 128)),), (t // tm,),
                 [pl.BlockSpec((tm, w3), lambda i: (i, 0)), _prev_spec(tm, w3), _acc_spec(CONV_K, w3),
                  pl.BlockSpec((tm, 128), lambda i: (i, 8192 // 128)), _acc_spec(1, 128), _acc_spec(1, 128)],
                 (_row_spec(tm, D_MODEL),) * 3 + (_row_spec(tm, 128),),
                 scratch=[pltpu.VMEM((tm + 8, w3), f32)])(proj, proj, conv_w, proj, a_log, dt_bias)


def conv_qkv_bwd(name, proj, conv_w, a_log, dt_bias, dq, dk, dv, dbg):
    t = proj.shape[0]
    tm = _rows(t)
    scale = HEAD_DIM ** -0.5

    def body(cur_ref, prev_ref, w_ref, bd_ref, al_ref, db_ref, dq_ref, dk_ref, dv_ref, dbg_ref,
             dc_ref, dw_ref, dbd_ref, dab_ref, pad_ref):
        _fill_pad(pad_ref, prev_ref, cur_ref, tm)

        @pl.when(pl.program_id(0) == 0)
        def _():
            dw_ref[...] = jnp.zeros_like(dw_ref)

        for cb in range(3 * DN_HEADS):
            cs = slice(cb * 128, (cb + 1) * 128)
            hs = slice((cb % DN_HEADS) * 128, (cb % DN_HEADS + 1) * 128)
            c = _conv_block(pad_ref, w_ref, cb, tm)
            s = _silu(c)
            if cb < 2 * DN_HEADS:
                dn = (dq_ref[:, hs] * scale) if cb < DN_HEADS else dk_ref[:, hs]
                r = lax.rsqrt(jnp.sum(s * s, axis=-1, keepdims=True) + EPS)
                ds = r * dn - s * (r * r * r) * jnp.sum(dn * s, axis=-1, keepdims=True)
            else:
                ds = dv_ref[:, hs]
            dc = ds * _dsilu(c)
            dc_ref[:, cs] = dc
            for j in range(CONV_K):
                dw_ref[j:j + 1, cs] += jnp.sum(dc * pad_ref[pl.ds(8 - (CONV_K - 1) + j, tm), cs], axis=0,
                                               keepdims=True)
        bd = bd_ref[...]
        dbg_v = dbg_ref[...]
        lane = lax.broadcasted_iota(jnp.int32, bd.shape, 1)
        sg = _sigmoid(bd)
        ea = jnp.exp(al_ref[...])
        z = bd + db_ref[...]
        sp = _softplus(z)
        is_b = lane < DN_HEADS
        is_g = jnp.logical_and(lane >= DN_HEADS, lane < 2 * DN_HEADS)
        d_z = jnp.where(is_g, dbg_v * (-ea) * _sigmoid(z), 0.0)
        dbd_ref[...] = jnp.where(is_b, dbg_v * sg * (1.0 - sg), d_z)
        d_al = jnp.sum(jnp.where(is_g, dbg_v * (-ea) * sp, 0.0), axis=0, keepdims=True)
        d_db = jnp.sum(d_z, axis=0, keepdims=True)
        _accumulate(dab_ref, jnp.concatenate([d_al, d_db] + [jnp.zeros_like(d_al)] * 6, axis=0))

    w3 = 3 * D_MODEL
    return _call(body, name, (_sds((t, w3)), _sds((CONV_K, w3)), _sds((t, 128)), _sds((8, 128))), (t // tm,),
                 [pl.BlockSpec((tm, w3), lambda i: (i, 0)), _prev_spec(tm, w3), _acc_spec(CONV_K, w3),
                  pl.BlockSpec((tm, 128), lambda i: (i, 8192 // 128)), _acc_spec(1, 128), _acc_spec(1, 128),
                  _row_spec(tm, D_MODEL), _row_spec(tm, D_MODEL), _row_spec(tm, D_MODEL), _row_spec(tm, 128)],
                 (_row_spec(tm, w3), _acc_spec(CONV_K, w3), _row_spec(tm, 128), _acc_spec(8, 128)),
                 scratch=[pltpu.VMEM((tm + 8, w3), f32)])(proj, proj, conv_w, proj, a_log, dt_bias, dq, dk, dv, dbg)


def conv_transpose(name, dc, conv_w):
    t, w3 = dc.shape
    tm = _rows(t)
    nt = t // tm

    def body(cur_ref, nxt_ref, w_ref, o_ref, pad_ref):
        keep = (pl.program_id(0) < nt - 1).astype(f32)
        pad_ref[0:tm, :] = cur_ref[...]
        pad_ref[tm:tm + 8, :] = nxt_ref[...] * keep
        for cb in range(w3 // 128):
            cs = slice(cb * 128, (cb + 1) * 128)
            acc = pad_ref[pl.ds(CONV_K - 1, tm), cs] * w_ref[0:1, cs]
            for j in range(1, CONV_K):
                acc = acc + pad_ref[pl.ds(CONV_K - 1 - j, tm), cs] * w_ref[j:j + 1, cs]
            o_ref[:, cs] = acc

    return _call(body, name, _sds((t, w3)), (nt,),
                 [_row_spec(tm, w3), _next_spec(tm, w3, t), _acc_spec(CONV_K, w3)], _row_spec(tm, w3),
                 scratch=[pltpu.VMEM((tm + 8, w3), f32)])(dc, dc, conv_w)


def _bdg(a, b, ca, cb, prec=None):
    if prec is None:
        a, b = a.astype(bf16), b.astype(bf16)
    return lax.dot_general(a, b, (((ca,), (cb,)), ((0,), (0,))), precision=prec, preferred_element_type=f32)


def _bnn(a, b, prec=None):
    return _bdg(a, b, 2, 1, prec)


def _bnt(a, b, prec=None):
    return _bdg(a, b, 2, 2, prec)


def _btn(a, b, prec=None):
    return _bdg(a, b, 1, 1, prec)


def _delta_local(q, k, v, g_row, b_row):
    c = CHUNK
    ii = lax.broadcasted_iota(jnp.int32, (c, c), 0)
    jj = lax.broadcasted_iota(jnp.int32, (c, c), 1)
    eye, lower, strict = ii == jj, ii >= jj, ii > jj
    shp = (q.shape[0], c, c)
    g_b = jnp.broadcast_to(g_row, shp)
    gc_col = jnp.sum(jnp.where(lower, g_b, 0.0), axis=2, keepdims=True)
    gc_row = jnp.sum(jnp.where(eye, jnp.broadcast_to(gc_col, shp), 0.0), axis=1, keepdims=True)
    b_col = jnp.sum(jnp.where(eye, jnp.broadcast_to(b_row, shp), 0.0), axis=2, keepdims=True)
    gl = jnp.sum(g_row, axis=2, keepdims=True)
    decay = jnp.exp(jnp.where(lower, gc_col - gc_row, -1e30))
    kb = k * b_col
    kk = _bnt(kb, k)
    lmat = jnp.where(strict, kk * decay, 0.0)
    tinv = eye.astype(f32) - lmat
    pw = lmat
    for _ in range(5):
        pw = _bnn(pw, pw, HIGH)
        tinv = tinv + _bnn(tinv, pw, HIGH)
    egc = jnp.exp(gc_col)
    rhs_w = kb * egc
    u = _bnn(tinv, v * b_col, HIGH)
    w = _bnn(tinv, rhs_w, HIGH)
    qk = _bnt(q, k)
    amat = jnp.where(lower, qk * decay, 0.0)
    q_dec = q * egc
    kdf = jnp.exp(gl - gc_col)
    k_dec = k * kdf
    return dict(eye=eye, lower=lower, strict=strict, gc_col=gc_col, b_col=b_col, gl=gl, decay=decay,
                kb=kb, kk=kk, tinv=tinv, egc=egc, rhs_w=rhs_w, u=u, w=w, qk=qk, amat=amat, q_dec=q_dec, kdf=kdf,
                k_dec=k_dec)


def _heads(ref, hb):
    return jnp.stack([ref[:, h * HEAD_DIM:(h + 1) * HEAD_DIM] for h in range(hb)])


def _put_heads(ref, val):
    for h in range(val.shape[0]):
        ref[:, h * HEAD_DIM:(h + 1) * HEAD_DIM] = val[h]


def delta_fwd(name, q, k, v, g_rows, b_rows):
    t = q.shape[0]
    nc = t // CHUNK
    hb = DELTA_HB

    def body(q_ref, k_ref, v_ref, g_ref, b_ref, o_ref, s_ref, state):
        n = pl.program_id(1)

        @pl.when(n == 0)
        def _():
            state[...] = jnp.zeros_like(state)

        loc = _delta_local(_heads(q_ref, hb), _heads(k_ref, hb), _heads(v_ref, hb), g_ref[:, pl.ds(n, 1), :],
                           b_ref[:, pl.ds(n, 1), :])
        s0 = state[...]
        s_ref[...] = s0
        v_new = loc["u"] - _bnn(loc["w"], s0)
        _put_heads(o_ref, _bnn(loc["q_dec"], s0) + _bnn(loc["amat"], v_new))
        state[...] = s0 * jnp.exp(loc["gl"]) + _btn(loc["k_dec"], v_new)

    tok = pl.BlockSpec((CHUNK, hb * HEAD_DIM), lambda h, n: (n, h))
    row = pl.BlockSpec((hb, nc, CHUNK), lambda h, n: (h, 0, 0))
    return _call(body, name, (_sds((t, D_MODEL)), _sds((DN_HEADS, nc, HEAD_DIM, HEAD_DIM))), (DN_HEADS // hb, nc),
                 [tok, tok, tok, row, row],
                 (tok, pl.BlockSpec((hb, None, HEAD_DIM, HEAD_DIM), lambda h, n: (h, n, 0, 0))),
                 scratch=[pltpu.VMEM((hb, HEAD_DIM, HEAD_DIM), f32)])(q, k, v, g_rows, b_rows)


def delta_bwd(name, q, k, v, g_rows, b_rows, s_all, do):
    t = q.shape[0]
    nc = t // CHUNK
    c = CHUNK
    hb = DELTA_HB

    def body(q_ref, k_ref, v_ref, g_ref, b_ref, s_ref, do_ref, dq_ref, dk_ref, dv_ref, dg_ref, db_ref, dstate):
        step = pl.program_id(1)
        n = nc - 1 - step

        @pl.when(step == 0)
        def _():
            dstate[...] = jnp.zeros_like(dstate)

        qv, kv, vv = _heads(q_ref, hb), _heads(k_ref, hb), _heads(v_ref, hb)
        L = _delta_local(qv, kv, vv, g_ref[:, pl.ds(n, 1), :], b_ref[:, pl.ds(n, 1), :])
        eye, lower, strict = L["eye"], L["lower"], L["strict"]
        shp = (hb, c, c)
        s0 = s_ref[...]
        dov = _heads(do_ref, hb)
        ds = dstate[...]
        eg = jnp.exp(L["gl"])
        v_new = L["u"] - _bnn(L["w"], s0)
        d_k_dec = _bnt(v_new, ds)
        d_v_new = _bnn(L["k_dec"], ds) + _btn(L["amat"], dov)
        d_eg = jnp.sum(jnp.sum(ds * s0, axis=2, keepdims=True), axis=1, keepdims=True)
        d_q_dec = _bnt(dov, s0)
        d_a = _bnt(dov, v_new)
        d_w = -_bnt(d_v_new, s0)
        dstate[...] = ds * eg + _btn(L["q_dec"], dov) - _btn(L["w"], d_v_new)
        d_am = jnp.where(lower, d_a * L["decay"], 0.0)
        dq = _bnn(d_am, kv) + d_q_dec * L["egc"]
        dk = _btn(d_am, qv) + d_k_dec * L["kdf"]
        e_col = jnp.sum(d_k_dec * L["k_dec"], axis=2, keepdims=True)
        d_gc_col = jnp.sum(d_q_dec * L["q_dec"], axis=2, keepdims=True) - e_col
        d_gl = jnp.sum(e_col, axis=1, keepdims=True) + d_eg * eg
        tinv = L["tinv"]
        d_rhs_u = _btn(tinv, d_v_new, HIGH)
        d_rhs_w = _btn(tinv, d_w, HIGH)
        d_l = -(_bnt(d_rhs_u, L["u"], HIGH) + _bnt(d_rhs_w, L["w"], HIGH))
        _put_heads(dv_ref, d_rhs_u * L["b_col"])
        d_b_col = jnp.sum(d_rhs_u * vv, axis=2, keepdims=True)
        d_gc_col = d_gc_col + jnp.sum(d_rhs_w * L["rhs_w"], axis=2, keepdims=True)
        d_lm = jnp.where(strict, d_l * L["decay"], 0.0)
        d_kb = d_rhs_w * L["egc"] + _bnn(d_lm, kv)
        dk = dk + _btn(d_lm, L["kb"]) + d_kb * L["b_col"]
        d_b_col = d_b_col + jnp.sum(d_kb * kv, axis=2, keepdims=True)
        m = d_am * L["qk"] + d_lm * L["kk"]
        d_gc_col = d_gc_col + jnp.sum(m, axis=2, keepdims=True)
        d_gc_row = (jnp.sum(jnp.where(eye, jnp.broadcast_to(d_gc_col, shp), 0.0), axis=1, keepdims=True)
                    - jnp.sum(m, axis=1, keepdims=True))
        lane = lax.broadcasted_iota(jnp.int32, (1, 1, c), 2)
        d_gc_row = d_gc_row + jnp.where(lane == c - 1, d_gl, 0.0)
        d_gc_tot = jnp.sum(jnp.where(eye, jnp.broadcast_to(d_gc_row, shp), 0.0), axis=2, keepdims=True)
        dg_ref[:, pl.ds(n, 1), :] = jnp.sum(jnp.where(lower, jnp.broadcast_to(d_gc_tot, shp), 0.0), axis=1,
                                            keepdims=True)
        db_ref[:, pl.ds(n, 1), :] = jnp.sum(jnp.where(eye, jnp.broadcast_to(d_b_col, shp), 0.0), axis=1,
                                            keepdims=True)
        _put_heads(dq_ref, dq)
        _put_heads(dk_ref, dk)

    tok = pl.BlockSpec((CHUNK, hb * HEAD_DIM), lambda h, s: (nc - 1 - s, h))
    row = pl.BlockSpec((hb, nc, CHUNK), lambda h, s: (h, 0, 0))
    return _call(body, name, (_sds((t, D_MODEL)),) * 3 + (_sds((DN_HEADS, nc, CHUNK)),) * 2, (DN_HEADS // hb, nc),
                 [tok, tok, tok, row, row,
                  pl.BlockSpec((hb, None, HEAD_DIM, HEAD_DIM), lambda h, s: (h, nc - 1 - s, 0, 0)), tok],
                 (tok, tok, tok, row, row),
                 scratch=[pltpu.VMEM((hb, HEAD_DIM, HEAD_DIM), f32)])(q, k, v, g_rows, b_rows, s_all, do)


SEG = 8


def _perm_rows(a):
    t, c = a.shape
    return a.reshape(SEG, t // SEG, c).transpose(1, 0, 2).reshape(t, c)


def _unperm_rows(a):
    t, c = a.shape
    return a.reshape(t // SEG, SEG, c).transpose(1, 0, 2).reshape(t, c)


def _cmul(ar, ai, br, bi):
    return ar * br - ai * bi, ar * bi + ai * br


def _segment_init(er, ei, lr, li, seg_len, reverse):
    w = er.shape[1]
    sub = lax.broadcasted_iota(jnp.int32, (SEG, w), 0)

    def shift(x, k):
        if reverse:
            return jnp.where(sub < SEG - k, pltpu.roll(x, SEG - k, 0), 0.0)
        return jnp.where(sub >= k, pltpu.roll(x, k, 0), 0.0)

    pr, pi = lr, li
    for _ in range(seg_len.bit_length() - 1):
        pr, pi = _cmul(pr, pi, pr, pi)
    fr, fi = shift(er, 1), shift(ei, 1)
    for k in (1, 2, 4):
        sr, si = shift(fr, k), shift(fi, k)
        mr, mi = _cmul(pr, pi, sr, si)
        fr, fi = fr + mr, fi + mi
        pr, pi = _cmul(pr, pi, pr, pi)
    return fr, fi


def s5_fwd(name, u_perm, bbd, cbd, lam):
    t = u_perm.shape[0]
    tt = _tile(t, (512, 256, 128))
    nt, ng, w = t // tt, tt // SEG, SG_STATE
    seg_len = t // SEG
    assert seg_len & (seg_len - 1) == 0 and tt % SEG == 0

    def body(u_ref, b_ref, c_ref, lam_ref, y_ref, h_ref, x_scr, state):
        p, i = pl.program_id(1), pl.program_id(2)
        lr1, li1 = lam_ref[:, 0:w], lam_ref[:, w:2 * w]
        lr, li = jnp.broadcast_to(lr1, (SEG, w)), jnp.broadcast_to(li1, (SEG, w))
        x_scr[...] = _bdot(u_ref[...], b_ref[...], ((1,), (0,)))

        @pl.when(jnp.logical_and(p == 0, i == 0))
        def _():
            state[...] = jnp.zeros_like(state)

        @pl.when(jnp.logical_and(p == 1, i == 0))
        def _():
            sr, si = _segment_init(state[:, 0:w], state[:, w:2 * w], lr1, li1, seg_len, False)
            state[:, 0:w] = sr
            state[:, w:2 * w] = si

        def run(store):
            def step(g, st):
                row = pl.multiple_of(g * SEG, SEG)
                xg = x_scr[pl.ds(row, SEG), :]
                nr = lr * st[0] - li * st[1] + xg[:, 0:w]
                ni = lr * st[1] + li * st[0] + xg[:, w:2 * w]
                if store:
                    h_ref[pl.ds(row, SEG), 0:w] = nr
                    h_ref[pl.ds(row, SEG), w:2 * w] = ni
                return nr, ni

            fin = lax.fori_loop(0, ng, step, (state[:, 0:w], state[:, w:2 * w]))
            state[:, 0:w] = fin[0]
            state[:, w:2 * w] = fin[1]

        @pl.when(p == 0)
        def _():
            run(False)

        @pl.when(p == 1)
        def _():
            run(True)
            y_ref[...] = _bdot(h_ref[...], c_ref[...], ((1,), (0,)))

    return _call(body, name, (_sds((t, D_MODEL)), _sds((t, N_SUPER * 2 * w))), (N_SUPER, 2, nt),
                 [pl.BlockSpec((tt, 128), lambda s, p, i: (i, s)),
                  pl.BlockSpec((None, 128, 2 * w), lambda s, p, i: (s, 0, 0)),
                  pl.BlockSpec((None, 2 * w, 128), lambda s, p, i: (s, 0, 0)),
                  pl.BlockSpec((None, 1, 2 * w), lambda s, p, i: (s, 0, 0))],
                 (pl.BlockSpec((tt, 128), lambda s, p, i: (i * p, s)),
                  pl.BlockSpec((tt, 2 * w), lambda s, p, i: (i * p, s))),
                 scratch=[pltpu.VMEM((tt, 2 * w), f32), pltpu.VMEM((SEG, 2 * w), f32)])(u_perm, bbd, cbd, lam)


def s5_bwd(name, dy_perm, u_perm, h_perm, hprev0, bbd, cbd, lam):
    t = u_perm.shape[0]
    tt = _tile(t, (512, 256, 128))
    nt, ng, w = t // tt, tt // SEG, SG_STATE
    seg_len = t // SEG

    def body(dy_ref, u_ref, h_ref, hp_ref, hp0_ref, b_ref, c_ref, lam_ref, du_ref, db_ref, dc_ref, dl_ref,
             g_scr, state, dl_acc):
        p, i = pl.program_id(1), pl.program_id(2)
        first_tile = jnp.logical_or(p == 0, i == nt - 1)
        lr1, li1 = lam_ref[:, 0:w], -lam_ref[:, w:2 * w]
        lr, li = jnp.broadcast_to(lr1, (SEG, w)), jnp.broadcast_to(li1, (SEG, w))
        g_scr[...] = _bdot(dy_ref[...], c_ref[...], ((1,), (1,)))

        @pl.when(jnp.logical_and(p == 0, i == 0))
        def _():
            state[...] = jnp.zeros_like(state)

        @pl.when(jnp.logical_and(p == 1, i == 0))
        def _():
            sr, si = _segment_init(state[:, 0:w], state[:, w:2 * w], lr1, li1, seg_len, True)
            state[:, 0:w] = sr
            state[:, w:2 * w] = si
            dl_acc[...] = jnp.zeros_like(dl_acc)

        def adj(g, st):
            row = pl.multiple_of(g * SEG, SEG)
            gg = g_scr[pl.ds(row, SEG), :]
            nr = lr * st[0] - li * st[1] + gg[:, 0:w]
            ni = lr * st[1] + li * st[0] + gg[:, w:2 * w]
            return row, nr, ni

        @pl.when(p == 0)
        def _():
            def step(k, st):
                _, nr, ni = adj(ng - 1 - k, st)
                return nr, ni

            fin = lax.fori_loop(0, ng, step, (state[:, 0:w], state[:, w:2 * w]))
            state[:, 0:w] = fin[0]
            state[:, w:2 * w] = fin[1]

        @pl.when(p == 1)
        def _():
            above = jnp.where(first_tile, hp0_ref[...], hp_ref[...])

            def step(k, st):
                g = ng - 1 - k
                row, nr, ni = adj(g, st)
                g_scr[pl.ds(row, SEG), 0:w] = nr
                g_scr[pl.ds(row, SEG), w:2 * w] = ni
                prow = pl.multiple_of(jnp.maximum(g - 1, 0) * SEG, SEG)
                hp = jnp.where(g > 0, h_ref[pl.ds(prow, SEG), :], above)
                pr, pi = hp[:, 0:w], hp[:, w:2 * w]
                return nr, ni, st[2] + nr * pr + ni * pi, st[3] + ni * pr - nr * pi

            fin = lax.fori_loop(0, ng, step, (state[:, 0:w], state[:, w:2 * w], dl_acc[:, 0:w], dl_acc[:, w:2 * w]))
            state[:, 0:w] = fin[0]
            state[:, w:2 * w] = fin[1]
            dl_acc[:, 0:w] = fin[2]
            dl_acc[:, w:2 * w] = fin[3]
            a = g_scr[...]
            du_ref[...] = _bdot(a, b_ref[...], ((1,), (1,)))
            d_b = _bdot(u_ref[...], a, ((0,), (0,)))
            d_c = _bdot(h_ref[...], dy_ref[...], ((0,), (0,)))

            @pl.when(i == 0)
            def _():
                db_ref[...] = d_b
                dc_ref[...] = d_c

            @pl.when(i > 0)
            def _():
                db_ref[...] += d_b
                dc_ref[...] += d_c

            @pl.when(i == nt - 1)
            def _():
                dl_ref[...] = jnp.sum(dl_acc[...], axis=0, keepdims=True)

    tile = lambda s, p, i: (nt - 1 - i, s)
    tile1 = lambda s, p, i: (nt - 1 - i * p, s)
    above = lambda s, p, i: (jnp.maximum((nt - 1 - i * p) * (tt // SEG) - 1, 0), s)
    per_s = lambda s, p, i: (s, 0, 0)
    return _call(body, name, (_sds((t, D_MODEL)), _sds((N_SUPER, 128, 2 * w)), _sds((N_SUPER, 2 * w, 128)),
                              _sds((N_SUPER, 1, 2 * w))), (N_SUPER, 2, nt),
                 [pl.BlockSpec((tt, 128), tile), pl.BlockSpec((tt, 128), tile1), pl.BlockSpec((tt, 2 * w), tile1),
                  pl.BlockSpec((SEG, 2 * w), above), pl.BlockSpec((SEG, 2 * w), lambda s, p, i: (0, s)),
                  pl.BlockSpec((None, 128, 2 * w), per_s), pl.BlockSpec((None, 2 * w, 128), per_s),
                  pl.BlockSpec((None, 1, 2 * w), per_s)],
                 (pl.BlockSpec((tt, 128), tile1), pl.BlockSpec((None, 128, 2 * w), per_s),
                  pl.BlockSpec((None, 2 * w, 128), per_s), pl.BlockSpec((None, 1, 2 * w), per_s)),
                 scratch=[pltpu.VMEM((tt, 2 * w), f32), pltpu.VMEM((SEG, 2 * w), f32),
                          pltpu.VMEM((SEG, 2 * w), f32)])(dy_perm, u_perm, h_perm, h_perm, hprev0, bbd, cbd, lam)


def _s5_disc(a_re, a_im, log_dt, br, bi):
    dt = jnp.exp(log_dt)
    mag = jnp.exp(a_re * dt)
    lr, li = mag * jnp.cos(a_im * dt), mag * jnp.sin(a_im * dt)
    den = a_re * a_re + a_im * a_im
    fr = ((lr - 1.0) * a_re + li * a_im) / den
    fi = (li * a_re - (lr - 1.0) * a_im) / den
    return lr, li, fr * br - fi * bi, fr * bi + fi * br


def s5_disc_fwd(name, a_re, a_im, log_dt, br, bi):
    g, n = SSM_GROUPS, SSM_STATE

    def body(ar, ai, ld, brr, bir, lr, li, bbr, bbi):
        o = _s5_disc(ar[...], ai[...], ld[...], brr[...], bir[...])
        lr[...], li[...], bbr[...], bbi[...] = o

    return _call(body, name, (_sds((g, 1, n)), _sds((g, 1, n)), _sds((g, SSM_GROUP, n)), _sds((g, SSM_GROUP, n))))(
        a_re, a_im, log_dt, br, bi)


def s5_disc_bwd(name, a_re, a_im, log_dt, br, bi, d_lr, d_li, d_bbr, d_bbi):
    g, n = SSM_GROUPS, SSM_STATE

    def body(ar, ai, ld, brr, bir, c1, c2, c3, c4, o1, o2, o3, o4, o5):
        _, vjp = jax.vjp(_s5_disc, ar[...], ai[...], ld[...], brr[...], bir[...])
        o1[...], o2[...], o3[...], o4[...], o5[...] = vjp((c1[...], c2[...], c3[...], c4[...]))

    return _call(body, name, (_sds((g, 1, n)), _sds((g, 1, n)), _sds((g, 1, 1)), _sds((g, SSM_GROUP, n)),
                              _sds((g, SSM_GROUP, n))))(a_re, a_im, log_dt, br, bi, d_lr, d_li, d_bbr, d_bbi)


def gelu_fwd(name, ys_lin, proj, d_skip):
    t, d = ys_lin.shape
    tm = _rows(t)

    def body(y_ref, u_ref, d_ref, o_ref):
        o_ref[...] = _gelu(y_ref[...] + d_ref[...] * u_ref[...])

    return _call(body, name, _sds((t, d)), (t // tm,),
                 [_row_spec(tm, d), pl.BlockSpec((tm, d), lambda i: (i, 4)), _acc_spec(1, d)],
                 _row_spec(tm, d))(ys_lin, proj, d_skip)


def _head_norm(o, hn):
    outs, ns, rs = [], [], []
    for h in range(DN_HEADS):
        oh = o[:, h * 128:(h + 1) * 128]
        r = lax.rsqrt(jnp.mean(oh * oh, axis=-1, keepdims=True) + EPS)
        n = oh * r
        ns.append(n)
        rs.append(r)
        outs.append(n * hn)
    return outs, ns, rs


def merge_fwd(name, proj, o, yg, glu_lin, head_norm, b_glu):
    t = o.shape[0]
    tm = _rows(t)
    d = D_MODEL

    def body(za_ref, zb_ref, ra_ref, rb_ref, o_ref, yg_ref, gl_ref, hn_ref, bg_ref, m_ref):
        on, _, _ = _head_norm(o_ref[...], hn_ref[...])
        y_a = jnp.concatenate(on, axis=1) * _silu(za_ref[...])
        y_b = yg_ref[...] * _sigmoid(gl_ref[...] + bg_ref[...]) * _silu(zb_ref[...])
        m_ref[...] = _sigmoid(ra_ref[...]) * y_a + _sigmoid(rb_ref[...]) * y_b

    pc = lambda c: pl.BlockSpec((tm, d), lambda i: (i, c))
    return _call(body, name, _sds((t, d)), (t // tm,),
                 [pc(3), pc(5), pc(6), pc(7), _row_spec(tm, d), _row_spec(tm, d), _row_spec(tm, d),
                  _acc_spec(1, 128), _acc_spec(1, d)], _row_spec(tm, d))(
        proj, proj, proj, proj, o, yg, glu_lin, head_norm, b_glu)


def merge_bwd(name, proj, o, yg, glu_lin, head_norm, b_glu, dm):
    t = o.shape[0]
    tm = _rows(t)
    d = D_MODEL

    def body(za_ref, zb_ref, ra_ref, rb_ref, o_ref, yg_ref, gl_ref, hn_ref, bg_ref, dm_ref,
             dza_ref, dzb_ref, dra_ref, drb_ref, do_ref, dgl_ref, dyg_ref, dhn_ref, dbg_ref):
        hn = hn_ref[...]
        za, zb, ra, rb = za_ref[...], zb_ref[...], ra_ref[...], rb_ref[...]
        on, ns, rs = _head_norm(o_ref[...], hn)
        onc = jnp.concatenate(on, axis=1)
        sza = _silu(za)
        y_a = onc * sza
        yg = yg_ref[...]
        sgl = _sigmoid(gl_ref[...] + bg_ref[...])
        y2 = yg * sgl
        szb = _silu(zb)
        y_b = y2 * szb
        sra, srb = _sigmoid(ra), _sigmoid(rb)
        dmv = dm_ref[...]
        dra_ref[...] = dmv * y_a * sra * (1.0 - sra)
        drb_ref[...] = dmv * y_b * srb * (1.0 - srb)
        d_ya = dmv * sra
        d_yb = dmv * srb
        dza_ref[...] = d_ya * onc * _dsilu(za)
        dzb_ref[...] = d_yb * y2 * _dsilu(zb)
        d_on = d_ya * sza
        d_y2 = d_yb * szb
        dyg_ref[...] = d_y2 * sgl
        d_gl = d_y2 * yg * sgl * (1.0 - sgl)
        dgl_ref[...] = d_gl
        _accumulate(dbg_ref, jnp.sum(d_gl, axis=0, keepdims=True))
        d_hn = jnp.zeros((1, 128), f32)
        for h in range(DN_HEADS):
            hs = slice(h * 128, (h + 1) * 128)
            dh = d_on[:, hs]
            d_hn = d_hn + jnp.sum(dh * ns[h], axis=0, keepdims=True)
            dn = dh * hn
            do_ref[:, hs] = rs[h] * (dn - ns[h] * jnp.mean(dn * ns[h], axis=-1, keepdims=True))
        _accumulate(dhn_ref, d_hn)

    pc = lambda c: pl.BlockSpec((tm, d), lambda i: (i, c))
    rs_ = _row_spec(tm, d)
    return _call(body, name, (_sds((t, d)),) * 7 + (_sds((1, 128)), _sds((1, d))), (t // tm,),
                 [pc(3), pc(5), pc(6), pc(7), rs_, rs_, rs_, _acc_spec(1, 128), _acc_spec(1, d), rs_],
                 (rs_,) * 7 + (_acc_spec(1, 128), _acc_spec(1, d)))(
        proj, proj, proj, proj, o, yg, glu_lin, head_norm, b_glu, dm)


def gelu_bwd(name, ys_lin, proj, d_skip, dyg_a, dyg_b):
    t, d = ys_lin.shape
    tm = _rows(t)

    def body(y_ref, u_ref, d_ref, a_ref, b_ref, dys_ref, du_ref, dd_ref):
        uv = u_ref[...]
        dys = (a_ref[...] + b_ref[...]) * _dgelu(y_ref[...] + d_ref[...] * uv)
        dys_ref[...] = dys
        du_ref[...] = dys * d_ref[...]
        _accumulate(dd_ref, jnp.sum(dys * uv, axis=0, keepdims=True))

    rs_ = _row_spec(tm, d)
    return _call(body, name, (_sds((t, d)), _sds((t, d)), _sds((1, d))), (t // tm,),
                 [rs_, pl.BlockSpec((tm, d), lambda i: (i, 4)), _acc_spec(1, d), rs_, rs_],
                 (rs_, rs_, _acc_spec(1, d)))(ys_lin, proj, d_skip, dyg_a, dyg_b)


def assemble_dproj(name, dqkv, dza, du_a, du_b, dzb, dra, drb, dbd):
    t = dza.shape[0]
    tm = _rows(t)
    d = D_MODEL

    def body(qkv_ref, za_ref, ua_ref, ub_ref, zb_ref, ra_ref, rb_ref, bd_ref, o_ref):
        o_ref[:, 0:3 * d] = qkv_ref[...].astype(bf16)
        o_ref[:, 3 * d:4 * d] = za_ref[...].astype(bf16)
        o_ref[:, 4 * d:5 * d] = (ua_ref[...] + ub_ref[...]).astype(bf16)
        o_ref[:, 5 * d:6 * d] = zb_ref[...].astype(bf16)
        o_ref[:, 6 * d:7 * d] = ra_ref[...].astype(bf16)
        o_ref[:, 7 * d:8 * d] = rb_ref[...].astype(bf16)
        o_ref[:, 8 * d:8 * d + 128] = bd_ref[...].astype(bf16)

    rs_ = _row_spec(tm, d)
    return _call(body, name, _sds((t, W_PAD), bf16), (t // tm,),
                 [_row_spec(tm, 3 * d), rs_, rs_, rs_, rs_, rs_, rs_, _row_spec(tm, 128)], _row_spec(tm, W_PAD))(
        dqkv, dza, du_a, du_b, dzb, dra, drb, dbd)


def adamw(name, w, g, m, v):
    r, c = w.shape
    tm = _tile(r, (512, 256, 128, 64, 32, 16, 8))
    c1 = 1.0 / (1.0 - ADAM_B1 ** ADAM_STEP)
    c2 = 1.0 / (1.0 - ADAM_B2 ** ADAM_STEP)

    def body(w_ref, g_ref, m_ref, v_ref, d_ref, nm_ref, nv_ref):
        gv = g_ref[...]
        nm = ADAM_B1 * m_ref[...] + (1.0 - ADAM_B1) * gv
        nv = ADAM_B2 * v_ref[...] + (1.0 - ADAM_B2) * (gv * gv)
        d_ref[...] = -ADAM_LR * ((nm * c1) / (jnp.sqrt(nv * c2) + ADAM_EPS) + ADAM_WD * w_ref[...])
        nm_ref[...] = nm
        nv_ref[...] = nv

    sp = pl.BlockSpec((tm, c), lambda i: (i, 0))
    return _call(body, name, (_sds((r, c)),) * 3, (r // tm,), [sp] * 4, (sp,) * 3)(w, g, m, v)


def _coords():
    return lax.axis_index("x"), lax.axis_index("y"), lax.axis_index("c")


def _lin(dev):
    return 4 * dev[0] + 2 * dev[1] + dev[2]


def _chips(me):
    x, y, _ = me
    return [(1 - x, y), (x, 1 - y), (1 - x, 1 - y)]


def all_gather(name, shard):
    r, c = shard.shape

    def body(x_ref, o_ref, send_sems, recv_sems, local_sem):
        me = _coords()
        x, y, cc = me
        sibling = (x, y, 1 - cc)
        chips = _chips(me)

        def copy(k, block, to, src=None):
            return pltpu.make_async_remote_copy(
                src_ref=o_ref.at[_lin(block)] if src is None else src, dst_ref=o_ref.at[_lin(block)],
                send_sem=send_sems.at[k], recv_sem=recv_sems.at[k], device_id=to, device_id_type=MESH)

        mine = pltpu.make_async_copy(x_ref, o_ref.at[_lin(me)], local_sem)
        mine.start()
        first = [copy(0, me, sibling, src=x_ref)]
        first += [copy(1 + j, me, (*chip, cc), src=x_ref) for j, chip in enumerate(chips)]
        for cp in first:
            cp.start()
        passed = [copy(4 + j, (*chip, cc), sibling) for j, chip in enumerate(chips)]
        for j, chip in enumerate(chips):
            copy(1 + j, (*chip, cc), me).wait_recv()
            passed[j].start()
        copy(0, sibling, me).wait_recv()
        for j, chip in enumerate(chips):
            copy(4 + j, (*chip, 1 - cc), me).wait_recv()
        for cp in first + passed:
            cp.wait_send()
        mine.wait()

    any_spec = pl.BlockSpec(memory_space=pl.ANY)
    return _call(body, name, _sds((N_DEV, r, c), shard.dtype), in_specs=[any_spec], out_specs=any_spec,
                 scratch=[pltpu.SemaphoreType.DMA((N_DEV - 1,)), pltpu.SemaphoreType.DMA((N_DEV - 1,)),
                          pltpu.SemaphoreType.DMA(())])(shard)


def pair_exchange(name, blocks):
    _, _, r, c = blocks.shape

    def body(x_ref, o_ref, send_sems, recv_sems):
        x, y, cc = _coords()
        sibling = (x, y, 1 - cc)
        cps = [pltpu.make_async_remote_copy(src_ref=x_ref.at[ch, 1 - cc], dst_ref=o_ref.at[ch], send_sem=send_sems.at[ch],
                                            recv_sem=recv_sems.at[ch], device_id=sibling, device_id_type=MESH)
               for ch in range(4)]
        for cp in cps:
            cp.start()
        for cp in cps:
            cp.wait()

    any_spec = pl.BlockSpec(memory_space=pl.ANY)
    return _call(body, name, _sds((4, r, c), blocks.dtype), in_specs=[any_spec], out_specs=any_spec,
                 scratch=[pltpu.SemaphoreType.DMA((4,)), pltpu.SemaphoreType.DMA((4,))])(blocks)


def chip_exchange(name, blocks):
    _, r, c = blocks.shape

    def body(x_ref, o_ref, send_sems, recv_sems, local_sem):
        me = _coords()
        x, y, cc = me
        my_chip = 2 * x + y
        mine = pltpu.make_async_copy(x_ref.at[my_chip], o_ref.at[my_chip], local_sem)
        mine.start()
        cps = []
        for j, (px, py) in enumerate(_chips(me)):
            cp = pltpu.make_async_remote_copy(src_ref=x_ref.at[2 * px + py], dst_ref=o_ref.at[my_chip],
                                              send_sem=send_sems.at[j], recv_sem=recv_sems.at[j],
                                              device_id=(px, py, cc), device_id_type=MESH)
            cp.start()
            cps.append(cp)
        for j, (px, py) in enumerate(_chips(me)):
            pltpu.make_async_remote_copy(src_ref=x_ref.at[my_chip], dst_ref=o_ref.at[2 * px + py],
                                         send_sem=send_sems.at[j], recv_sem=recv_sems.at[j],
                                         device_id=(px, py, cc), device_id_type=MESH).wait_recv()
        for cp in cps:
            cp.wait_send()
        mine.wait()

    any_spec = pl.BlockSpec(memory_space=pl.ANY)
    return _call(body, name, _sds(blocks.shape, blocks.dtype), in_specs=[any_spec], out_specs=any_spec,
                 scratch=[pltpu.SemaphoreType.DMA((3,)), pltpu.SemaphoreType.DMA((3,)),
                          pltpu.SemaphoreType.DMA(())])(blocks)


def pair_sum(name, mine, theirs):
    _, r, c = mine.shape
    tm = _tile(r, (240, 256, 128, 64, 32, 16))

    def body(a_ref, b_ref, o_ref):
        o_ref[...] = (a_ref[...].astype(f32) + b_ref[...].astype(f32)).astype(o_ref.dtype)

    sp = pl.BlockSpec((None, tm, c), lambda ch, i: (ch, i, 0))
    return _call(body, name, _sds(mine.shape, mine.dtype), (4, r // tm), [sp, sp], sp)(mine, theirs)


def sum_slots(name, slots):
    n, r, c = slots.shape
    tm = _tile(r, (240, 256, 128, 64, 32, 16))

    def body(s_ref, o_ref):
        acc = s_ref[0].astype(f32)
        for d in range(1, n):
            acc = acc + s_ref[d].astype(f32)
        o_ref[...] = acc

    return _call(body, name, _sds((r, c)), (r // tm,), [pl.BlockSpec((n, tm, c), lambda i: (0, i, 0))],
                 pl.BlockSpec((tm, c), lambda i: (i, 0)))(slots)


def _rows_layout(col8, t):
    return col8.T.reshape(DN_HEADS, t // CHUNK, CHUNK)


def _blockdiag(m):
    g, a, b = m.shape
    m = m.reshape(N_SUPER, SUPER, a, b)
    out = jnp.einsum("sgab,gh->sgahb", m, jnp.eye(SUPER, dtype=m.dtype))
    return out.reshape(N_SUPER, SUPER * a, SUPER * b)


def _diag_blocks(m, a, b):
    m = m.reshape(N_SUPER, SUPER, a, SUPER, b)
    return jnp.einsum("sgahb,gh->sgab", m, jnp.eye(SUPER, dtype=m.dtype)).reshape(SSM_GROUPS, a, b)


def _s5_params(p, li):
    tag = f"l{li}"
    n = SSM_STATE
    a_re = p["ssm_a_re"].reshape(SSM_GROUPS, 1, n)
    a_im = p["ssm_a_im"].reshape(SSM_GROUPS, 1, n)
    log_dt = p["ssm_log_dt"].reshape(SSM_GROUPS, 1, 1)
    br = jnp.swapaxes(p["ssm_b_re"], 1, 2)
    bi = jnp.swapaxes(p["ssm_b_im"], 1, 2)
    lr, li_, bbr, bbi = s5_disc_fwd("s5_disc_" + tag, a_re, a_im, log_dt, br, bi)
    lam = jnp.concatenate([lr.reshape(N_SUPER, 1, SG_STATE), li_.reshape(N_SUPER, 1, SG_STATE)], axis=-1)
    bbd = jnp.concatenate([_blockdiag(bbr), _blockdiag(bbi)], axis=-1).astype(bf16)
    c_re = jnp.swapaxes(p["ssm_c_re"], 1, 2)
    c_im = jnp.swapaxes(p["ssm_c_im"], 1, 2)
    cbd = jnp.concatenate([_blockdiag(c_re), -_blockdiag(c_im)], axis=1).astype(bf16)
    return dict(a_re=a_re, a_im=a_im, log_dt=log_dt, br=br, bi=bi, lam=lam, bbd=bbd, cbd=cbd)


def layer_fwd(x, p, li):
    tag = f"l{li}"
    t = x.shape[0]
    d = D_MODEL
    h = rmsnorm_fwd("norm_pre_" + tag, x, p["norm_pre"])
    proj = mm_nn("proj_" + tag, h, p["w_all"])
    q, k, v, bg = conv_qkv_fwd("conv_" + tag, proj, p["conv_w"], p["a_log"], p["dt_bias"])
    b_rows = _rows_layout(bg[:, 0:DN_HEADS], t)
    g_rows = _rows_layout(bg[:, DN_HEADS:2 * DN_HEADS], t)
    o, s_all = delta_fwd("delta_" + tag, q, k, v, g_rows, b_rows)
    sp = _s5_params(p, li)
    u_perm = _perm_rows(proj[:, 4 * d:5 * d])
    ys_perm, hs = s5_fwd("s5_" + tag, u_perm, sp["bbd"], sp["cbd"], sp["lam"])
    ys_lin = _unperm_rows(ys_perm)
    yg = gelu_fwd("gelu_" + tag, ys_lin, proj, p["ssm_d"])
    glu_lin = mm_nn("glu_" + tag, yg, p["w_glu"])
    merged = merge_fwd("merge_" + tag, proj, o, yg, glu_lin, p["head_norm"], p["b_glu"])
    out = mm_nn("out_" + tag, merged, p["w_out"])
    y = residual_norm_fwd("norm_post_" + tag, x, out, p["norm_post"])
    saved = dict(x=x, h=h, proj=proj, q=q, k=k, v=v, g_rows=g_rows, b_rows=b_rows, o=o, s_all=s_all, sp=sp, u_perm=u_perm,
                 hs=hs, ys_lin=ys_lin, yg=yg, glu_lin=glu_lin, merged=merged, out=out)
    return y, saved


def layer_bwd(dy, p, s, li):
    tag = f"l{li}"
    t = dy.shape[0]
    d = D_MODEL
    sp = s["sp"]
    gr = {}
    d_out, gr["norm_post"] = post_norm_bwd("norm_post_b_" + tag, s["out"], p["norm_post"], dy)
    d_merged = mm_nt("out_b_" + tag, d_out, p["w_out"])
    gr["w_out"] = mm_tn("out_w_" + tag, s["merged"], d_out, bf16)
    (dza, dzb, dra, drb, d_o, d_glu, dyg_a, gr["head_norm"], gr["b_glu"]) = merge_bwd(
        "merge_b_" + tag, s["proj"], s["o"], s["yg"], s["glu_lin"], p["head_norm"], p["b_glu"], d_merged)
    dyg_b = mm_nt("glu_b_" + tag, d_glu, p["w_glu"])
    gr["w_glu"] = mm_tn("glu_w_" + tag, s["yg"], d_glu, bf16)
    d_ys, du_a, gr["ssm_d"] = gelu_bwd("gelu_b_" + tag, s["ys_lin"], s["proj"], p["ssm_d"], dyg_a, dyg_b)
    hprev0 = jnp.concatenate([jnp.zeros((1, s["hs"].shape[1]), f32), s["hs"][-SEG:-1]], axis=0)
    du_perm, d_bbd, d_cbd, d_lam = s5_bwd("s5_b_" + tag, _perm_rows(d_ys), s["u_perm"], s["hs"], hprev0, sp["bbd"],
                                          sp["cbd"], sp["lam"])
    du_b = _unperm_rows(du_perm)
    d_c_re = _diag_blocks(d_cbd[:, 0:SG_STATE, :], SSM_STATE, SSM_GROUP)
    d_c_im = -_diag_blocks(d_cbd[:, SG_STATE:, :], SSM_STATE, SSM_GROUP)
    gr["ssm_c_re"] = jnp.swapaxes(d_c_re, 1, 2)
    gr["ssm_c_im"] = jnp.swapaxes(d_c_im, 1, 2)
    d_bbr = _diag_blocks(d_bbd[:, :, 0:SG_STATE], SSM_GROUP, SSM_STATE)
    d_bbi = _diag_blocks(d_bbd[:, :, SG_STATE:], SSM_GROUP, SSM_STATE)
    d_lr = d_lam[:, :, 0:SG_STATE].reshape(SSM_GROUPS, 1, SSM_STATE)
    d_li = d_lam[:, :, SG_STATE:].reshape(SSM_GROUPS, 1, SSM_STATE)
    d_are, d_aim, d_ldt, d_br, d_bi = s5_disc_bwd("s5_disc_b_" + tag, sp["a_re"], sp["a_im"], sp["log_dt"], sp["br"],
                                                  sp["bi"], d_lr, d_li, d_bbr, d_bbi)
    gr["ssm_a_re"] = d_are.reshape(SSM_GROUPS, SSM_STATE)
    gr["ssm_a_im"] = d_aim.reshape(SSM_GROUPS, SSM_STATE)
    gr["ssm_log_dt"] = d_ldt.reshape(SSM_GROUPS)
    gr["ssm_b_re"] = jnp.swapaxes(d_br, 1, 2)
    gr["ssm_b_im"] = jnp.swapaxes(d_bi, 1, 2)
    dq, dk, dv, dg_rows, db_rows = delta_bwd("delta_b_" + tag, s["q"], s["k"], s["v"], s["g_rows"], s["b_rows"],
                                             s["s_all"], d_o)
    dbg = jnp.concatenate([db_rows.reshape(DN_HEADS, t).T, dg_rows.reshape(DN_HEADS, t).T,
                           jnp.zeros((t, 128 - 2 * DN_HEADS), f32)], axis=1)
    dc, gr["conv_w"], dbd, dab = conv_qkv_bwd("conv_b_" + tag, s["proj"], p["conv_w"], p["a_log"], p["dt_bias"],
                                              dq, dk, dv, dbg)
    gr["a_log"] = dab[0, DN_HEADS:2 * DN_HEADS]
    gr["dt_bias"] = dab[1, DN_HEADS:2 * DN_HEADS]
    dqkv = conv_transpose("conv_t_" + tag, dc, p["conv_w"])
    dproj = assemble_dproj("dproj_" + tag, dqkv, dza, du_a, du_b, dzb, dra, drb, dbd)
    d_h = mm_nt("proj_b_" + tag, dproj, p["w_all"])
    gr["w_all"] = mm_tn("proj_w_" + tag, s["h"], dproj, bf16)
    dx, gr["norm_pre"] = rmsnorm_bwd("norm_pre_b_" + tag, s["x"], p["norm_pre"], d_h, dy)
    return dx, gr


REPL = ["norm_pre", "a_log", "dt_bias", "head_norm", "ssm_a_re", "ssm_a_im", "ssm_log_dt", "ssm_b_re", "ssm_b_im",
        "ssm_c_re", "ssm_c_im", "ssm_d", "b_glu", "norm_post"]
SHARDED = ["w_in", "conv_w", "w_glu", "w_out"]
ALL_W = ["norm_pre", "w_in", "conv_w", "a_log", "dt_bias", "head_norm", "ssm_a_re", "ssm_a_im", "ssm_log_dt",
         "ssm_b_re", "ssm_b_im", "ssm_c_re", "ssm_c_im", "ssm_d", "w_glu", "b_glu", "w_out", "norm_post"]
PACK_W = 1024


def _pack_flat(arrs, rows):
    flat = jnp.concatenate([a.reshape(-1) for a in arrs])
    return jnp.pad(flat, (0, rows * PACK_W - flat.shape[0])).reshape(rows, PACK_W)


def _flat_rows(arrs, mult=8):
    n = sum(math.prod(a.shape) for a in arrs)
    rows = -(-n // PACK_W)
    return -(-rows // mult) * mult


def _unpack(flat, shapes):
    out, off = [], 0
    for sh in shapes:
        n = math.prod(sh)
        out.append(flat[off:off + n].reshape(sh))
        off += n
    return out


def _permute_w(w):
    return jnp.concatenate([w[:, 0:4096], w[:, 4112:W_COLS], w[:, 4096:4112],
                            jnp.zeros((w.shape[0], W_PAD - W_COLS), w.dtype)], axis=1)


def _unpermute_w(g):
    return jnp.concatenate([g[:, 0:4096], g[:, 8192:8208], g[:, 4096:8192]], axis=1)


def kernel(x, norm_pre, w_in, conv_w, a_log, dt_bias, head_norm, ssm_a_re, ssm_a_im, ssm_log_dt, ssm_b_re, ssm_b_im, ssm_c_re, ssm_c_im, ssm_d, w_glu, b_glu, w_out, norm_post, loss_target, m_norm_pre, m_w_in, m_conv_w, m_a_log, m_dt_bias, m_head_norm, m_ssm_a_re, m_ssm_a_im, m_ssm_log_dt, m_ssm_b_re, m_ssm_b_im, m_ssm_c_re, m_ssm_c_im, m_ssm_d, m_w_glu, m_b_glu, m_w_out, m_norm_post, v_norm_pre, v_w_in, v_conv_w, v_a_log, v_dt_bias, v_head_norm, v_ssm_a_re, v_ssm_a_im, v_ssm_log_dt, v_ssm_b_re, v_ssm_b_im, v_ssm_c_re, v_ssm_c_im, v_ssm_d, v_w_glu, v_b_glu, v_w_out, v_norm_post):
    loc = dict(locals())
    w = {n: loc[n] for n in ALL_W}
    m = {n: loc["m_" + n] for n in ALL_W}
    v = {n: loc["v_" + n] for n in ALL_W}
    depth = w_in.shape[0]
    wc = w_in.shape[2]
    cc = conv_w.shape[2]
    wr = w_glu.shape[1]

    tail = wc - PACK_W
    conv_hi = conv_w.astype(bf16)
    conv_mid = (conv_w - conv_hi.astype(f32)).astype(bf16)
    conv_lo = (conv_w - conv_hi.astype(f32) - conv_mid.astype(f32)).astype(bf16)
    w_in_b = w_in.astype(bf16)
    misc_shapes = [(depth, D_MODEL, tail), (3, depth, CONV_K, cc)]
    misc_rows = _flat_rows([_sds(sh, bf16) for sh in misc_shapes], 16)
    n_main, n_sq = depth * D_MODEL, depth * wr
    shard = jnp.concatenate([w_in_b[:, :, :PACK_W].reshape(n_main, PACK_W), w_glu.astype(bf16).reshape(n_sq, PACK_W),
                             w_out.astype(bf16).reshape(n_sq, PACK_W),
                             _pack_flat([w_in_b[:, :, PACK_W:], jnp.stack([conv_hi, conv_mid, conv_lo])], misc_rows)])
    gathered = all_gather("gather_weights", shard)
    miscs = [_unpack(gathered[dv, n_main + 2 * n_sq:].reshape(-1), misc_shapes) for dv in range(N_DEV)]
    w_in_full = jnp.concatenate(
        [pc for dv in range(N_DEV) for pc in (gathered[dv, :n_main].reshape(depth, D_MODEL, PACK_W), miscs[dv][0])],
        axis=2)
    w_glu_full = gathered[:, n_main:n_main + n_sq].reshape(N_DEV, depth, wr, D_MODEL)
    w_glu_full = jnp.swapaxes(w_glu_full, 0, 1).reshape(depth, N_DEV * wr, D_MODEL)
    w_out_full = gathered[:, n_main + n_sq:n_main + 2 * n_sq].reshape(N_DEV, depth, wr, D_MODEL)
    w_out_full = jnp.swapaxes(w_out_full, 0, 1).reshape(depth, N_DEV * wr, D_MODEL)
    conv_full = jnp.concatenate([mi[1][0].astype(f32) + mi[1][1].astype(f32) + mi[1][2].astype(f32)
                                 for mi in miscs], axis=2)

    def layer_params(li):
        return dict(norm_pre=norm_pre[li].reshape(1, -1), w_all=_permute_w(w_in_full[li]), conv_w=conv_full[li],
                    a_log=jnp.pad(a_log[li], (DN_HEADS, 128 - 2 * DN_HEADS)).reshape(1, 128),
                    dt_bias=jnp.pad(dt_bias[li], (DN_HEADS, 128 - 2 * DN_HEADS)).reshape(1, 128),
                    head_norm=head_norm[li].reshape(1, -1), ssm_a_re=ssm_a_re[li], ssm_a_im=ssm_a_im[li],
                    ssm_log_dt=ssm_log_dt[li], ssm_b_re=ssm_b_re[li], ssm_b_im=ssm_b_im[li], ssm_c_re=ssm_c_re[li],
                    ssm_c_im=ssm_c_im[li], ssm_d=ssm_d[li].reshape(1, -1), w_glu=w_glu_full[li],
                    b_glu=b_glu[li].reshape(1, -1), w_out=w_out_full[li], norm_post=norm_post[li].reshape(1, -1))

    act = x[0]
    saved, params = [], []
    for li in range(depth):
        params.append(layer_params(li))
        act, sv = layer_fwd(act, params[li], li)
        saved.append(sv)
    loss_part, dy = loss_head("loss_head", act, loss_target[0])
    grads = [None] * depth
    for li in reversed(range(depth)):
        dy, grads[li] = layer_bwd(dy, params[li], saved[li], li)
    grad_x = dy[None]
    loss = lax.psum(loss_part[0, 0], ("x", "y", "c"))

    def stack(name):
        return jnp.stack([grads[li][name] for li in range(depth)])

    g_w_in = jnp.stack([_unpermute_w(grads[li]["w_all"]) for li in range(depth)])
    g_conv = stack("conv_w").astype(bf16)
    g_glu = stack("w_glu").reshape(depth, N_DEV, wr, D_MODEL)
    g_out = stack("w_out").reshape(depth, N_DEV, wr, D_MODEL)
    repl_shapes = [w[n].shape for n in REPL]
    repl_rows = _flat_rows([w[n] for n in REPL], 8 * N_DEV)
    g_repl = _pack_flat([stack(n).reshape(w[n].shape) for n in REPL], repl_rows).astype(bf16)
    rr = repl_rows // N_DEV
    gmisc_shapes = [(depth, D_MODEL, tail), (depth, CONV_K, cc)]
    gmisc_rows = _flat_rows([_sds(sh, bf16) for sh in gmisc_shapes], 16)
    blocks = jnp.stack([jnp.concatenate([
        g_w_in[:, :, dv * wc:dv * wc + PACK_W].reshape(n_main, PACK_W), g_glu[:, dv].reshape(n_sq, PACK_W),
        g_out[:, dv].reshape(n_sq, PACK_W),
        _pack_flat([g_w_in[:, :, dv * wc + PACK_W:(dv + 1) * wc], g_conv[:, :, dv * cc:(dv + 1) * cc]], gmisc_rows),
        g_repl[dv * rr:(dv + 1) * rr]]) for dv in range(N_DEV)])
    pad_rows = -blocks.shape[1] % 16
    if pad_rows:
        blocks = jnp.pad(blocks, ((0, 0), (0, pad_rows), (0, 0)))
    blocks = blocks.reshape(4, 2, blocks.shape[1], PACK_W)
    from_sibling = pair_exchange("pair_grads", blocks)
    own = lax.dynamic_index_in_dim(blocks, lax.axis_index("c"), axis=1, keepdims=False)
    chip_part = pair_sum("pair_sum_grads", own, from_sibling)
    slots = chip_exchange("scatter_grads", chip_part)
    mine = sum_slots("sum_grads", slots)
    o1, o2, o3, o4 = n_main, n_main + n_sq, n_main + 2 * n_sq, n_main + 2 * n_sq + gmisc_rows
    gs_tail, gs_conv = _unpack(mine[o3:o4].reshape(-1), gmisc_shapes)
    gs_w_in = jnp.concatenate([mine[:o1].reshape(depth, D_MODEL, PACK_W), gs_tail], axis=2)
    gs_glu = mine[o1:o2].reshape(depth, wr, D_MODEL)
    gs_out = mine[o2:o3].reshape(depth, wr, D_MODEL)
    g_repl_full = all_gather("gather_repl_grads", mine[o4:o4 + rr]).reshape(-1)
    g = dict(zip(REPL, _unpack(g_repl_full, repl_shapes)))
    g.update(w_in=gs_w_in, w_glu=gs_glu, w_out=gs_out, conv_w=gs_conv)

    delta, new_m, new_v = {}, {}, {}
    for n in SHARDED:
        sh = w[n].shape
        two_d = (-1, sh[-1])
        delta[n], new_m[n], new_v[n] = [o.reshape(sh) for o in adamw(
            "adamw_" + n, w[n].reshape(two_d), g[n].reshape(two_d), m[n].reshape(two_d), v[n].reshape(two_d))]
    packed = [_pack_flat([src[n] for n in REPL], repl_rows) for src in (w, g, m, v)]
    outs = adamw("adamw_repl", *packed)
    for dst, o in zip((delta, new_m, new_v), outs):
        dst.update(zip(REPL, _unpack(o.reshape(-1), repl_shapes)))
    return (loss, grad_x, *[g[n] for n in ALL_W], *[delta[n] for n in ALL_W], *[new_m[n] for n in ALL_W],
            *[new_v[n] for n in ALL_W])
```

```python
import math

import jax
import jax.numpy as jnp
from jax import lax
from jax.experimental import pallas as pl
from jax.experimental.pallas import tpu as pltpu

f32 = jnp.float32
bf16 = jnp.bfloat16

D_MODEL = 1024
DEPTH = 4
N_DEV = 8
DN_HEADS = 8
HEAD_DIM = 128
CHUNK = 64
CONV_K = 4
SSM_GROUPS = 64
SSM_GROUP = 16
SSM_STATE = 64
SUPER = 8
N_SUPER = SSM_GROUPS // SUPER
SG_STATE = SUPER * SSM_STATE
EPS = 1e-6
W_COLS = 8208
W_PAD = 8448
ADAM_LR, ADAM_B1, ADAM_B2, ADAM_EPS, ADAM_WD, ADAM_STEP = 0.001, 0.9, 0.999, 1e-08, 0.01, 10
VMEM_LIMIT = 56 * 1024 * 1024
MESH = pl.DeviceIdType.MESH
HIGH = lax.Precision.HIGH
DELTA_HB = 8


def _call(body, name, out_shape, grid=None, in_specs=None, out_specs=None, scratch=(), **kw):
    args = dict(out_shape=out_shape, name=name, scratch_shapes=list(scratch),
                compiler_params=pltpu.CompilerParams(vmem_limit_bytes=VMEM_LIMIT, **kw))
    if grid is not None:
        args.update(grid=grid, in_specs=in_specs, out_specs=out_specs)
    else:
        if in_specs is not None:
            args.update(in_specs=in_specs)
        if out_specs is not None:
            args.update(out_specs=out_specs)
    return pl.pallas_call(body, **args)


def _sds(shape, dtype=f32):
    return jax.ShapeDtypeStruct(tuple(shape), dtype)


def _sigmoid(x):
    return 1.0 / (1.0 + jnp.exp(-x))


def _silu(x):
    return x * _sigmoid(x)


def _dsilu(x):
    s = _sigmoid(x)
    return s * (1.0 + x * (1.0 - s))


_GELU_C = math.sqrt(2.0 / math.pi)


def _gelu(x):
    return 0.5 * x * (1.0 + jnp.tanh(_GELU_C * (x + 0.044715 * x * x * x)))


def _dgelu(x):
    t = jnp.tanh(_GELU_C * (x + 0.044715 * x * x * x))
    return 0.5 * (1.0 + t) + 0.5 * x * (1.0 - t * t) * _GELU_C * (1.0 + 3 * 0.044715 * x * x)


def _softplus(x):
    return jnp.maximum(x, 0.0) + jnp.log(1.0 + jnp.exp(-jnp.abs(x)))


def _bdot(a, b, dn):
    return lax.dot_general(a.astype(bf16), b.astype(bf16), (dn, ((), ())), preferred_element_type=f32)


def _matmul(name, a, b, *, dn, grid, a_spec, b_spec, o_spec, o_shape, o_dtype=f32):
    nk = grid[-1]

    def body(a_ref, b_ref, o_ref, acc_ref):
        p = _bdot(a_ref[...], b_ref[...], dn)
        if nk == 1:
            o_ref[...] = p.astype(o_dtype)
        else:
            k = pl.program_id(len(grid) - 1)

            @pl.when(k == 0)
            def _():
                acc_ref[...] = p

            @pl.when(k > 0)
            def _():
                acc_ref[...] += p

            @pl.when(k == nk - 1)
            def _():
                o_ref[...] = acc_ref[...].astype(o_dtype)

    blk = tuple(d for d in o_spec.block_shape if d is not None)
    return _call(body, name, _sds(o_shape, o_dtype), grid, [a_spec, b_spec], o_spec,
                 scratch=[pltpu.VMEM(blk if nk > 1 else (8, 128), f32)])(a, b)


def _tile(n, pref):
    for t in pref:
        if n % t == 0:
            return t
    return n


def mm_nn(name, a, b):
    m, k = a.shape
    n = b.shape[1]
    tm, tn, tk = _tile(m, (512, 256)), _tile(n, (2816, 1024, 512)), _tile(k, (2816, 1024))
    return _matmul(name, a, b, dn=((1,), (0,)), grid=(m // tm, n // tn, k // tk),
                   a_spec=pl.BlockSpec((tm, tk), lambda i, j, l: (i, l)),
                   b_spec=pl.BlockSpec((tk, tn), lambda i, j, l: (l, j)),
                   o_spec=pl.BlockSpec((tm, tn), lambda i, j, l: (i, j)), o_shape=(m, n))


def mm_nt(name, a, b):
    m, k = a.shape
    n = b.shape[0]
    tm, tn, tk = _tile(m, (512, 256)), _tile(n, (1024, 512)), _tile(k, (2816, 1024))
    return _matmul(name, a, b, dn=((1,), (1,)), grid=(m // tm, n // tn, k // tk),
                   a_spec=pl.BlockSpec((tm, tk), lambda i, j, l: (i, l)),
                   b_spec=pl.BlockSpec((tn, tk), lambda i, j, l: (j, l)),
                   o_spec=pl.BlockSpec((tm, tn), lambda i, j, l: (i, j)), o_shape=(m, n))


def mm_tn(name, a, b, o_dtype=f32):
    k, m = a.shape
    n = b.shape[1]
    tm, tn, tk = _tile(m, (512,)), _tile(n, (2816, 1024, 512)), _tile(k, (512, 256))
    return _matmul(name, a, b, dn=((0,), (0,)), grid=(m // tm, n // tn, k // tk),
                   a_spec=pl.BlockSpec((tk, tm), lambda i, j, l: (l, i)),
                   b_spec=pl.BlockSpec((tk, tn), lambda i, j, l: (l, j)),
                   o_spec=pl.BlockSpec((tm, tn), lambda i, j, l: (i, j)), o_shape=(m, n), o_dtype=o_dtype)


def _rows(t):
    return _tile(t, (256,))


def _row_spec(tm, w):
    return pl.BlockSpec((tm, w), lambda i: (i, 0))


def _acc_spec(r, w):
    return pl.BlockSpec((r, w), lambda i: (0, 0))


def _accumulate(ref, val):
    @pl.when(pl.program_id(0) == 0)
    def _():
        ref[...] = val

    @pl.when(pl.program_id(0) > 0)
    def _():
        ref[...] += val


def rmsnorm_fwd(name, x, gain):
    t, d = x.shape
    tm = _rows(t)

    def body(x_ref, g_ref, o_ref):
        xv = x_ref[...]
        r = lax.rsqrt(jnp.mean(xv * xv, axis=-1, keepdims=True) + EPS)
        o_ref[...] = (xv * r * g_ref[...]).astype(bf16)

    return _call(body, name, _sds((t, d), bf16), (t // tm,), [_row_spec(tm, d), _acc_spec(1, d)],
                 _row_spec(tm, d))(x, gain)


def rmsnorm_bwd(name, x, gain, dn, dres):
    t, d = x.shape
    tm = _rows(t)

    def body(x_ref, g_ref, dn_ref, dr_ref, dx_ref, dg_ref):
        xv = x_ref[...]
        r = lax.rsqrt(jnp.mean(xv * xv, axis=-1, keepdims=True) + EPS)
        n = xv * r
        dnv = dn_ref[...]
        _accumulate(dg_ref, jnp.sum(dnv * n, axis=0, keepdims=True))
        dng = dnv * g_ref[...]
        dx_ref[...] = dr_ref[...] + r * (dng - n * jnp.mean(dng * n, axis=-1, keepdims=True))

    return _call(body, name, (_sds((t, d)), _sds((1, d))), (t // tm,),
                 [_row_spec(tm, d), _acc_spec(1, d), _row_spec(tm, d), _row_spec(tm, d)],
                 (_row_spec(tm, d), _acc_spec(1, d)))(x, gain, dn, dres)


def residual_norm_fwd(name, x, out, gain):
    t, d = x.shape
    tm = _rows(t)

    def body(x_ref, o_ref, g_ref, y_ref):
        ov = o_ref[...]
        r = lax.rsqrt(jnp.mean(ov * ov, axis=-1, keepdims=True) + EPS)
        y_ref[...] = x_ref[...] + ov * r * g_ref[...]

    return _call(body, name, _sds((t, d)), (t // tm,), [_row_spec(tm, d), _row_spec(tm, d), _acc_spec(1, d)],
                 _row_spec(tm, d))(x, out, gain)


def post_norm_bwd(name, out, gain, dy):
    t, d = out.shape
    tm = _rows(t)

    def body(o_ref, g_ref, dy_ref, do_ref, dg_ref):
        ov = o_ref[...]
        r = lax.rsqrt(jnp.mean(ov * ov, axis=-1, keepdims=True) + EPS)
        n = ov * r
        dyv = dy_ref[...]
        _accumulate(dg_ref, jnp.sum(dyv * n, axis=0, keepdims=True))
        dng = dyv * g_ref[...]
        do_ref[...] = (r * (dng - n * jnp.mean(dng * n, axis=-1, keepdims=True))).astype(bf16)

    return _call(body, name, (_sds((t, d), bf16), _sds((1, d))), (t // tm,),
                 [_row_spec(tm, d), _acc_spec(1, d), _row_spec(tm, d)],
                 (_row_spec(tm, d), _acc_spec(1, d)))(out, gain, dy)


def loss_head(name, y, target):
    t, d = y.shape
    tm = _rows(t)

    def body(y_ref, t_ref, l_ref, dy_ref):
        e = y_ref[...] - t_ref[...]
        dy_ref[...] = e * (1.0 / d)
        s = jnp.sum(jnp.sum(e * e, axis=1, keepdims=True), axis=0, keepdims=True) * (0.5 / d)
        _accumulate(l_ref, s)

    return _call(body, name, (_sds((1, 1)), _sds((t, d))), (t // tm,),
                 [_row_spec(tm, d), _row_spec(tm, d)], (_acc_spec(1, 1), _row_spec(tm, d)))(y, target)


def _prev_spec(tm, w):
    return pl.BlockSpec((8, w), lambda i: (jnp.maximum(i * (tm // 8) - 1, 0), 0))


def _next_spec(tm, w, t):
    return pl.BlockSpec((8, w), lambda i: (jnp.minimum((i + 1) * (tm // 8), t // 8 - 1), 0))


def _fill_pad(pad_ref, prev_ref, cur_ref, tm):
    keep = (pl.program_id(0) > 0).astype(f32)
    pad_ref[0:8, :] = prev_ref[...] * keep
    pad_ref[8:8 + tm, :] = cur_ref[...]


def _conv_block(pad_ref, w_ref, cb, tm):
    cs = slice(cb * 128, (cb + 1) * 128)
    acc = pad_ref[pl.ds(8 - (CONV_K - 1), tm), cs] * w_ref[0:1, cs]
    for j in range(1, CONV_K):
        acc = acc + pad_ref[pl.ds(8 - (CONV_K - 1) + j, tm), cs] * w_ref[j:j + 1, cs]
    return acc


def conv_qkv_fwd(name, proj, conv_w, a_log, dt_bias):
    t = proj.shape[0]
    tm = _rows(t)
    scale = HEAD_DIM ** -0.5

    def body(cur_ref, prev_ref, w_ref, bd_ref, al_ref, db_ref, q_ref, k_ref, v_ref, bg_ref, pad_ref):
        _fill_pad(pad_ref, prev_ref, cur_ref, tm)
        for cb in range(3 * DN_HEADS):
            s = _silu(_conv_block(pad_ref, w_ref, cb, tm))
            hs = slice((cb % DN_HEADS) * 128, (cb % DN_HEADS + 1) * 128)
            if cb < DN_HEADS:
                q_ref[:, hs] = s * (lax.rsqrt(jnp.sum(s * s, axis=-1, keepdims=True) + EPS) * scale)
            elif cb < 2 * DN_HEADS:
                k_ref[:, hs] = s * lax.rsqrt(jnp.sum(s * s, axis=-1, keepdims=True) + EPS)
            else:
                v_ref[:, hs] = s
        bd = bd_ref[...]
        beta = _sigmoid(bd)
        g = -jnp.exp(al_ref[...]) * _softplus(bd + db_ref[...])
        lane = lax.broadcasted_iota(jnp.int32, bd.shape, 1)
        bg_ref[...] = jnp.where(lane < DN_HEADS, beta, jnp.where(lane < 2 * DN_HEADS, g, 0.0))

    w3 = 3 * D_MODEL
    return _call(body, name, (_sds((t, D_MODEL)),) * 3 + (_sds((t, 128)),), (t // tm,),
                 [pl.BlockSpec((tm, w3), lambda i: (i, 0)), _prev_spec(tm, w3), _acc_spec(CONV_K, w3),
                  pl.BlockSpec((tm, 128), lambda i: (i, 8192 // 128)), _acc_spec(1, 128), _acc_spec(1, 128)],
                 (_row_spec(tm, D_MODEL),) * 3 + (_row_spec(tm, 128),),
                 scratch=[pltpu.VMEM((tm + 8, w3), f32)])(proj, proj, conv_w, proj, a_log, dt_bias)


def conv_qkv_bwd(name, proj, conv_w, a_log, dt_bias, dq, dk, dv, dbg):
    t = proj.shape[0]
    tm = _rows(t)
    scale = HEAD_DIM ** -0.5

    def body(cur_ref, prev_ref, w_ref, bd_ref, al_ref, db_ref, dq_ref, dk_ref, dv_ref, dbg_ref,
             dc_ref, dw_ref, dbd_ref, dab_ref, pad_ref):
        _fill_pad(pad_ref, prev_ref, cur_ref, tm)

        @pl.when(pl.program_id(0) == 0)
        def _():
            dw_ref[...] = jnp.zeros_like(dw_ref)

        for cb in range(3 * DN_HEADS):
            cs = slice(cb * 128, (cb + 1) * 128)
            hs = slice((cb % DN_HEADS) * 128, (cb % DN_HEADS + 1) * 128)
            c = _conv_block(pad_ref, w_ref, cb, tm)
            s = _silu(c)
            if cb < 2 * DN_HEADS:
                dn = (dq_ref[:, hs] * scale) if cb < DN_HEADS else dk_ref[:, hs]
                r = lax.rsqrt(jnp.sum(s * s, axis=-1, keepdims=True) + EPS)
                ds = r * dn - s * (r * r * r) * jnp.sum(dn * s, axis=-1, keepdims=True)
            else:
                ds = dv_ref[:, hs]
            dc = ds * _dsilu(c)
            dc_ref[:, cs] = dc
            for j in range(CONV_K):
                dw_ref[j:j + 1, cs] += jnp.sum(dc * pad_ref[pl.ds(8 - (CONV_K - 1) + j, tm), cs], axis=0,
                                               keepdims=True)
        bd = bd_ref[...]
        dbg_v = dbg_ref[...]
        lane = lax.broadcasted_iota(jnp.int32, bd.shape, 1)
        sg = _sigmoid(bd)
        ea = jnp.exp(al_ref[...])
        z = bd + db_ref[...]
        sp = _softplus(z)
        is_b = lane < DN_HEADS
        is_g = jnp.logical_and(lane >= DN_HEADS, lane < 2 * DN_HEADS)
        d_z = jnp.where(is_g, dbg_v * (-ea) * _sigmoid(z), 0.0)
        dbd_ref[...] = jnp.where(is_b, dbg_v * sg * (1.0 - sg), d_z)
        d_al = jnp.sum(jnp.where(is_g, dbg_v * (-ea) * sp, 0.0), axis=0, keepdims=True)
        d_db = jnp.sum(d_z, axis=0, keepdims=True)
        _accumulate(dab_ref, jnp.concatenate([d_al, d_db] + [jnp.zeros_like(d_al)] * 6, axis=0))

    w3 = 3 * D_MODEL
    return _call(body, name, (_sds((t, w3)), _sds((CONV_K, w3)), _sds((t, 128)), _sds((8, 128))), (t // tm,),
                 [pl.BlockSpec((tm, w3), lambda i: (i, 0)), _prev_spec(tm, w3), _acc_spec(CONV_K, w3),
                  pl.BlockSpec((tm, 128), lambda i: (i, 8192 // 128)), _acc_spec(1, 128), _acc_spec(1, 128),
                  _row_spec(tm, D_MODEL), _row_spec(tm, D_MODEL), _row_spec(tm, D_MODEL), _row_spec(tm, 128)],
                 (_row_spec(tm, w3), _acc_spec(CONV_K, w3), _row_spec(tm, 128), _acc_spec(8, 128)),
                 scratch=[pltpu.VMEM((tm + 8, w3), f32)])(proj, proj, conv_w, proj, a_log, dt_bias, dq, dk, dv, dbg)


def conv_transpose(name, dc, conv_w):
    t, w3 = dc.shape
    tm = _rows(t)
    nt = t // tm

    def body(cur_ref, nxt_ref, w_ref, o_ref, pad_ref):
        keep = (pl.program_id(0) < nt - 1).astype(f32)
        pad_ref[0:tm, :] = cur_ref[...]
        pad_ref[tm:tm + 8, :] = nxt_ref[...] * keep
        for cb in range(w3 // 128):
            cs = slice(cb * 128, (cb + 1) * 128)
            acc = pad_ref[pl.ds(CONV_K - 1, tm), cs] * w_ref[0:1, cs]
            for j in range(1, CONV_K):
                acc = acc + pad_ref[pl.ds(CONV_K - 1 - j, tm), cs] * w_ref[j:j + 1, cs]
            o_ref[:, cs] = acc

    return _call(body, name, _sds((t, w3)), (nt,),
                 [_row_spec(tm, w3), _next_spec(tm, w3, t), _acc_spec(CONV_K, w3)], _row_spec(tm, w3),
                 scratch=[pltpu.VMEM((tm + 8, w3), f32)])(dc, dc, conv_w)


def _bdg(a, b, ca, cb, prec=None):
    if prec is None:
        a, b = a.astype(bf16), b.astype(bf16)
    return lax.dot_general(a, b, (((ca,), (cb,)), ((0,), (0,))), precision=prec, preferred_element_type=f32)


def _bnn(a, b, prec=None):
    return _bdg(a, b, 2, 1, prec)


def _bnt(a, b, prec=None):
    return _bdg(a, b, 2, 2, prec)


def _btn(a, b, prec=None):
    return _bdg(a, b, 1, 1, prec)


def _delta_local(q, k, v, g_row, b_row):
    c = CHUNK
    ii = lax.broadcasted_iota(jnp.int32, (c, c), 0)
    jj = lax.broadcasted_iota(jnp.int32, (c, c), 1)
    eye, lower, strict = ii == jj, ii >= jj, ii > jj
    shp = (q.shape[0], c, c)
    g_b = jnp.broadcast_to(g_row, shp)
    gc_col = jnp.sum(jnp.where(lower, g_b, 0.0), axis=2, keepdims=True)
    gc_row = jnp.sum(jnp.where(eye, jnp.broadcast_to(gc_col, shp), 0.0), axis=1, keepdims=True)
    b_col = jnp.sum(jnp.where(eye, jnp.broadcast_to(b_row, shp), 0.0), axis=2, keepdims=True)
    gl = jnp.sum(g_row, axis=2, keepdims=True)
    decay = jnp.exp(jnp.where(lower, gc_col - gc_row, -1e30))
    kb = k * b_col
    kk = _bnt(kb, k)
    lmat = jnp.where(strict, kk * decay, 0.0)
    tinv = eye.astype(f32) - lmat
    pw = lmat
    for _ in range(5):
        pw = _bnn(pw, pw, HIGH)
        tinv = tinv + _bnn(tinv, pw, HIGH)
    egc = jnp.exp(gc_col)
    rhs_w = kb * egc
    u = _bnn(tinv, v * b_col, HIGH)
    w = _bnn(tinv, rhs_w, HIGH)
    qk = _bnt(q, k)
    amat = jnp.where(lower, qk * decay, 0.0)
    q_dec = q * egc
    kdf = jnp.exp(gl - gc_col)
    k_dec = k * kdf
    return dict(eye=eye, lower=lower, strict=strict, gc_col=gc_col, b_col=b_col, gl=gl, decay=decay,
                kb=kb, kk=kk, tinv=tinv, egc=egc, rhs_w=rhs_w, u=u, w=w, qk=qk, amat=amat, q_dec=q_dec, kdf=kdf,
                k_dec=k_dec)


def _heads(ref, hb):
    return jnp.stack([ref[:, h * HEAD_DIM:(h + 1) * HEAD_DIM] for h in range(hb)])


def _put_heads(ref, val):
    for h in range(val.shape[0]):
        ref[:, h * HEAD_DIM:(h + 1) * HEAD_DIM] = val[h]


def delta_fwd(name, q, k, v, g_rows, b_rows):
    t = q.shape[0]
    nc = t // CHUNK
    hb = DELTA_HB

    def body(q_ref, k_ref, v_ref, g_ref, b_ref, o_ref, s_ref, state):
        n = pl.program_id(1)

        @pl.when(n == 0)
        def _():
            state[...] = jnp.zeros_like(state)

        loc = _delta_local(_heads(q_ref, hb), _heads(k_ref, hb), _heads(v_ref, hb), g_ref[:, pl.ds(n, 1), :],
                           b_ref[:, pl.ds(n, 1), :])
        s0 = state[...]
        s_ref[...] = s0
        v_new = loc["u"] - _bnn(loc["w"], s0)
        _put_heads(o_ref, _bnn(loc["q_dec"], s0) + _bnn(loc["amat"], v_new))
        state[...] = s0 * jnp.exp(loc["gl"]) + _btn(loc["k_dec"], v_new)

    tok = pl.BlockSpec((CHUNK, hb * HEAD_DIM), lambda h, n: (n, h))
    row = pl.BlockSpec((hb, nc, CHUNK), lambda h, n: (h, 0, 0))
    return _call(body, name, (_sds((t, D_MODEL)), _sds((DN_HEADS, nc, HEAD_DIM, HEAD_DIM))), (DN_HEADS // hb, nc),
                 [tok, tok, tok, row, row],
                 (tok, pl.BlockSpec((hb, None, HEAD_DIM, HEAD_DIM), lambda h, n: (h, n, 0, 0))),
                 scratch=[pltpu.VMEM((hb, HEAD_DIM, HEAD_DIM), f32)])(q, k, v, g_rows, b_rows)


def delta_bwd(name, q, k, v, g_rows, b_rows, s_all, do):
    t = q.shape[0]
    nc = t // CHUNK
    c = CHUNK
    hb = DELTA_HB

    def body(q_ref, k_ref, v_ref, g_ref, b_ref, s_ref, do_ref, dq_ref, dk_ref, dv_ref, dg_ref, db_ref, dstate):
        step = pl.program_id(1)
        n = nc - 1 - step

        @pl.when(step == 0)
        def _():
            dstate[...] = jnp.zeros_like(dstate)

        qv, kv, vv = _heads(q_ref, hb), _heads(k_ref, hb), _heads(v_ref, hb)
        L = _delta_local(qv, kv, vv, g_ref[:, pl.ds(n, 1), :], b_ref[:, pl.ds(n, 1), :])
        eye, lower, strict = L["eye"], L["lower"], L["strict"]
        shp = (hb, c, c)
        s0 = s_ref[...]
        dov = _heads(do_ref, hb)
        ds = dstate[...]
        eg = jnp.exp(L["gl"])
        v_new = L["u"] - _bnn(L["w"], s0)
        d_k_dec = _bnt(v_new, ds)
        d_v_new = _bnn(L["k_dec"], ds) + _btn(L["amat"], dov)
        d_eg = jnp.sum(jnp.sum(ds * s0, axis=2, keepdims=True), axis=1, keepdims=True)
        d_q_dec = _bnt(dov, s0)
        d_a = _bnt(dov, v_new)
        d_w = -_bnt(d_v_new, s0)
        dstate[...] = ds * eg + _btn(L["q_dec"], dov) - _btn(L["w"], d_v_new)
        d_am = jnp.where(lower, d_a * L["decay"], 0.0)
        dq = _bnn(d_am, kv) + d_q_dec * L["egc"]
        dk = _btn(d_am, qv) + d_k_dec * L["kdf"]
        e_col = jnp.sum(d_k_dec * L["k_dec"], axis=2, keepdims=True)
        d_gc_col = jnp.sum(d_q_dec * L["q_dec"], axis=2, keepdims=True) - e_col
        d_gl = jnp.sum(e_col, axis=1, keepdims=True) + d_eg * eg
        tinv = L["tinv"]
        d_rhs_u = _btn(tinv, d_v_new, HIGH)
        d_rhs_w = _btn(tinv, d_w, HIGH)
        d_l = -(_bnt(d_rhs_u, L["u"], HIGH) + _bnt(d_rhs_w, L["w"], HIGH))
        _put_heads(dv_ref, d_rhs_u * L["b_col"])
        d_b_col = jnp.sum(d_rhs_u * vv, axis=2, keepdims=True)
        d_gc_col = d_gc_col + jnp.sum(d_rhs_w * L["rhs_w"], axis=2, keepdims=True)
        d_lm = jnp.where(strict, d_l * L["decay"], 0.0)
        d_kb = d_rhs_w * L["egc"] + _bnn(d_lm, kv)
        dk = dk + _btn(d_lm, L["kb"]) + d_kb * L["b_col"]
        d_b_col = d_b_col + jnp.sum(d_kb * kv, axis=2, keepdims=True)
        m = d_am * L["qk"] + d_lm * L["kk"]
        d_gc_col = d_gc_col + jnp.sum(m, axis=2, keepdims=True)
        d_gc_row = (jnp.sum(jnp.where(eye, jnp.broadcast_to(d_gc_col, shp), 0.0), axis=1, keepdims=True)
                    - jnp.sum(m, axis=1, keepdims=True))
        lane = lax.broadcasted_iota(jnp.int32, (1, 1, c), 2)
        d_gc_row = d_gc_row + jnp.where(lane == c - 1, d_gl, 0.0)
        d_gc_tot = jnp.sum(jnp.where(eye, jnp.broadcast_to(d_gc_row, shp), 0.0), axis=2, keepdims=True)
        dg_ref[:, pl.ds(n, 1), :] = jnp.sum(jnp.where(lower, jnp.broadcast_to(d_gc_tot, shp), 0.0), axis=1,
                                            keepdims=True)
        db_ref[:, pl.ds(n, 1), :] = jnp.sum(jnp.where(eye, jnp.broadcast_to(d_b_col, shp), 0.0), axis=1,
                                            keepdims=True)
        _put_heads(dq_ref, dq)
        _put_heads(dk_ref, dk)

    tok = pl.BlockSpec((CHUNK, hb * HEAD_DIM), lambda h, s: (nc - 1 - s, h))
    row = pl.BlockSpec((hb, nc, CHUNK), lambda h, s: (h, 0, 0))
    return _call(body, name, (_sds((t, D_MODEL)),) * 3 + (_sds((DN_HEADS, nc, CHUNK)),) * 2, (DN_HEADS // hb, nc),
                 [tok, tok, tok, row, row,
                  pl.BlockSpec((hb, None, HEAD_DIM, HEAD_DIM), lambda h, s: (h, nc - 1 - s, 0, 0)), tok],
                 (tok, tok, tok, row, row),
                 scratch=[pltpu.VMEM((hb, HEAD_DIM, HEAD_DIM), f32)])(q, k, v, g_rows, b_rows, s_all, do)


SEG = 8


def _perm_rows(a):
    t, c = a.shape
    return a.reshape(SEG, t // SEG, c).transpose(1, 0, 2).reshape(t, c)


def _unperm_rows(a):
    t, c = a.shape
    return a.reshape(t // SEG, SEG, c).transpose(1, 0, 2).reshape(t, c)


def _cmul(ar, ai, br, bi):
    return ar * br - ai * bi, ar * bi + ai * br


def _segment_init(er, ei, lr, li, seg_len, reverse):
    w = er.shape[1]
    sub = lax.broadcasted_iota(jnp.int32, (SEG, w), 0)

    def shift(x, k):
        if reverse:
            return jnp.where(sub < SEG - k, pltpu.roll(x, SEG - k, 0), 0.0)
        return jnp.where(sub >= k, pltpu.roll(x, k, 0), 0.0)

    pr, pi = lr, li
    for _ in range(seg_len.bit_length() - 1):
        pr, pi = _cmul(pr, pi, pr, pi)
    fr, fi = shift(er, 1), shift(ei, 1)
    for k in (1, 2, 4):
        sr, si = shift(fr, k), shift(fi, k)
        mr, mi = _cmul(pr, pi, sr, si)
        fr, fi = fr + mr, fi + mi
        pr, pi = _cmul(pr, pi, pr, pi)
    return fr, fi


def s5_fwd(name, u_perm, bbd, cbd, lam):
    t = u_perm.shape[0]
    tt = _tile(t, (512, 256, 128))
    nt, ng, w = t // tt, tt // SEG, SG_STATE
    seg_len = t // SEG
    assert seg_len & (seg_len - 1) == 0 and tt % SEG == 0

    def body(u_ref, b_ref, c_ref, lam_ref, y_ref, h_ref, x_scr, state):
        p, i = pl.program_id(1), pl.program_id(2)
        lr1, li1 = lam_ref[:, 0:w], lam_ref[:, w:2 * w]
        lr, li = jnp.broadcast_to(lr1, (SEG, w)), jnp.broadcast_to(li1, (SEG, w))
        x_scr[...] = _bdot(u_ref[...], b_ref[...], ((1,), (0,)))

        @pl.when(jnp.logical_and(p == 0, i == 0))
        def _():
            state[...] = jnp.zeros_like(state)

        @pl.when(jnp.logical_and(p == 1, i == 0))
        def _():
            sr, si = _segment_init(state[:, 0:w], state[:, w:2 * w], lr1, li1, seg_len, False)
            state[:, 0:w] = sr
            state[:, w:2 * w] = si

        def run(store):
            def step(g, st):
                row = pl.multiple_of(g * SEG, SEG)
                xg = x_scr[pl.ds(row, SEG), :]
                nr = lr * st[0] - li * st[1] + xg[:, 0:w]
                ni = lr * st[1] + li * st[0] + xg[:, w:2 * w]
                if store:
                    h_ref[pl.ds(row, SEG), 0:w] = nr
                    h_ref[pl.ds(row, SEG), w:2 * w] = ni
                return nr, ni

            fin = lax.fori_loop(0, ng, step, (state[:, 0:w], state[:, w:2 * w]))
            state[:, 0:w] = fin[0]
            state[:, w:2 * w] = fin[1]

        @pl.when(p == 0)
        def _():
            run(False)

        @pl.when(p == 1)
        def _():
            run(True)
            y_ref[...] = _bdot(h_ref[...], c_ref[...], ((1,), (0,)))

    return _call(body, name, (_sds((t, D_MODEL)), _sds((t, N_SUPER * 2 * w))), (N_SUPER, 2, nt),
                 [pl.BlockSpec((tt, 128), lambda s, p, i: (i, s)),
                  pl.BlockSpec((None, 128, 2 * w), lambda s, p, i: (s, 0, 0)),
                  pl.BlockSpec((None, 2 * w, 128), lambda s, p, i: (s, 0, 0)),
                  pl.BlockSpec((None, 1, 2 * w), lambda s, p, i: (s, 0, 0))],
                 (pl.BlockSpec((tt, 128), lambda s, p, i: (i * p, s)),
                  pl.BlockSpec((tt, 2 * w), lambda s, p, i: (i * p, s))),
                 scratch=[pltpu.VMEM((tt, 2 * w), f32), pltpu.VMEM((SEG, 2 * w), f32)])(u_perm, bbd, cbd, lam)


def s5_bwd(name, dy_perm, u_perm, h_perm, hprev0, bbd, cbd, lam):
    t = u_perm.shape[0]
    tt = _tile(t, (512, 256, 128))
    nt, ng, w = t // tt, tt // SEG, SG_STATE
    seg_len = t // SEG

    def body(dy_ref, u_ref, h_ref, hp_ref, hp0_ref, b_ref, c_ref, lam_ref, du_ref, db_ref, dc_ref, dl_ref,
             g_scr, state, dl_acc):
        p, i = pl.program_id(1), pl.program_id(2)
        first_tile = jnp.logical_or(p == 0, i == nt - 1)
        lr1, li1 = lam_ref[:, 0:w], -lam_ref[:, w:2 * w]
        lr, li = jnp.broadcast_to(lr1, (SEG, w)), jnp.broadcast_to(li1, (SEG, w))
        g_scr[...] = _bdot(dy_ref[...], c_ref[...], ((1,), (1,)))

        @pl.when(jnp.logical_and(p == 0, i == 0))
        def _():
            state[...] = jnp.zeros_like(state)

        @pl.when(jnp.logical_and(p == 1, i == 0))
        def _():
            sr, si = _segment_init(state[:, 0:w], state[:, w:2 * w], lr1, li1, seg_len, True)
            state[:, 0:w] = sr
            state[:, w:2 * w] = si
            dl_acc[...] = jnp.zeros_like(dl_acc)

        def adj(g, st):
            row = pl.multiple_of(g * SEG, SEG)
            gg = g_scr[pl.ds(row, SEG), :]
            nr = lr * st[0] - li * st[1] + gg[:, 0:w]
            ni = lr * st[1] + li * st[0] + gg[:, w:2 * w]
            return row, nr, ni

        @pl.when(p == 0)
        def _():
            def step(k, st):
                _, nr, ni = adj(ng - 1 - k, st)
                return nr, ni

            fin = lax.fori_loop(0, ng, step, (state[:, 0:w], state[:, w:2 * w]))
            state[:, 0:w] = fin[0]
            state[:, w:2 * w] = fin[1]

        @pl.when(p == 1)
        def _():
            above = jnp.where(first_tile, hp0_ref[...], hp_ref[...])

            def step(k, st):
                g = ng - 1 - k
                row, nr, ni = adj(g, st)
                g_scr[pl.ds(row, SEG), 0:w] = nr
                g_scr[pl.ds(row, SEG), w:2 * w] = ni
                prow = pl.multiple_of(jnp.maximum(g - 1, 0) * SEG, SEG)
                hp = jnp.where(g > 0, h_ref[pl.ds(prow, SEG), :], above)
                pr, pi = hp[:, 0:w], hp[:, w:2 * w]
                return nr, ni, st[2] + nr * pr + ni * pi, st[3] + ni * pr - nr * pi

            fin = lax.fori_loop(0, ng, step, (state[:, 0:w], state[:, w:2 * w], dl_acc[:, 0:w], dl_acc[:, w:2 * w]))
            state[:, 0:w] = fin[0]
            state[:, w:2 * w] = fin[1]
            dl_acc[:, 0:w] = fin[2]
            dl_acc[:, w:2 * w] = fin[3]
            a = g_scr[...]
            du_ref[...] = _bdot(a, b_ref[...], ((1,), (1,)))
            d_b = _bdot(u_ref[...], a, ((0,), (0,)))
            d_c = _bdot(h_ref[...], dy_ref[...], ((0,), (0,)))

            @pl.when(i == 0)
            def _():
                db_ref[...] = d_b
                dc_ref[...] = d_c

            @pl.when(i > 0)
            def _():
                db_ref[...] += d_b
                dc_ref[...] += d_c

            @pl.when(i == nt - 1)
            def _():
                dl_ref[...] = jnp.sum(dl_acc[...], axis=0, keepdims=True)

    tile = lambda s, p, i: (nt - 1 - i, s)
    tile1 = lambda s, p, i: (nt - 1 - i * p, s)
    above = lambda s, p, i: (jnp.maximum((nt - 1 - i * p) * (tt // SEG) - 1, 0), s)
    per_s = lambda s, p, i: (s, 0, 0)
    return _call(body, name, (_sds((t, D_MODEL)), _sds((N_SUPER, 128, 2 * w)), _sds((N_SUPER, 2 * w, 128)),
                              _sds((N_SUPER, 1, 2 * w))), (N_SUPER, 2, nt),
                 [pl.BlockSpec((tt, 128), tile), pl.BlockSpec((tt, 128), tile1), pl.BlockSpec((tt, 2 * w), tile1),
                  pl.BlockSpec((SEG, 2 * w), above), pl.BlockSpec((SEG, 2 * w), lambda s, p, i: (0, s)),
                  pl.BlockSpec((None, 128, 2 * w), per_s), pl.BlockSpec((None, 2 * w, 128), per_s),
                  pl.BlockSpec((None, 1, 2 * w), per_s)],
                 (pl.BlockSpec((tt, 128), tile1), pl.BlockSpec((None, 128, 2 * w), per_s),
                  pl.BlockSpec((None, 2 * w, 128), per_s), pl.BlockSpec((None, 1, 2 * w), per_s)),
                 scratch=[pltpu.VMEM((tt, 2 * w), f32), pltpu.VMEM((SEG, 2 * w), f32),
                          pltpu.VMEM((SEG, 2 * w), f32)])(dy_perm, u_perm, h_perm, h_perm, hprev0, bbd, cbd, lam)


def _s5_disc(a_re, a_im, log_dt, br, bi):
    dt = jnp.exp(log_dt)
    mag = jnp.exp(a_re * dt)
    lr, li = mag * jnp.cos(a_im * dt), mag * jnp.sin(a_im * dt)
    den = a_re * a_re + a_im * a_im
    fr = ((lr - 1.0) * a_re + li * a_im) / den
    fi = (li * a_re - (lr - 1.0) * a_im) / den
    return lr, li, fr * br - fi * bi, fr * bi + fi * br


def s5_disc_fwd(name, a_re, a_im, log_dt, br, bi):
    g, n = SSM_GROUPS, SSM_STATE

    def body(ar, ai, ld, brr, bir, lr, li, bbr, bbi):
        o = _s5_disc(ar[...], ai[...], ld[...], brr[...], bir[...])
        lr[...], li[...], bbr[...], bbi[...] = o

    return _call(body, name, (_sds((g, 1, n)), _sds((g, 1, n)), _sds((g, SSM_GROUP, n)), _sds((g, SSM_GROUP, n))))(
        a_re, a_im, log_dt, br, bi)


def s5_disc_bwd(name, a_re, a_im, log_dt, br, bi, d_lr, d_li, d_bbr, d_bbi):
    g, n = SSM_GROUPS, SSM_STATE

    def body(ar, ai, ld, brr, bir, c1, c2, c3, c4, o1, o2, o3, o4, o5):
        _, vjp = jax.vjp(_s5_disc, ar[...], ai[...], ld[...], brr[...], bir[...])
        o1[...], o2[...], o3[...], o4[...], o5[...] = vjp((c1[...], c2[...], c3[...], c4[...]))

    return _call(body, name, (_sds((g, 1, n)), _sds((g, 1, n)), _sds((g, 1, 1)), _sds((g, SSM_GROUP, n)),
                              _sds((g, SSM_GROUP, n))))(a_re, a_im, log_dt, br, bi, d_lr, d_li, d_bbr, d_bbi)


def gelu_fwd(name, ys_lin, proj, d_skip):
    t, d = ys_lin.shape
    tm = _rows(t)

    def body(y_ref, u_ref, d_ref, o_ref):
        o_ref[...] = _gelu(y_ref[...] + d_ref[...] * u_ref[...])

    return _call(body, name, _sds((t, d)), (t // tm,),
                 [_row_spec(tm, d), pl.BlockSpec((tm, d), lambda i: (i, 4)), _acc_spec(1, d)],
                 _row_spec(tm, d))(ys_lin, proj, d_skip)


def _head_norm(o, hn):
    outs, ns, rs = [], [], []
    for h in range(DN_HEADS):
        oh = o[:, h * 128:(h + 1) * 128]
        r = lax.rsqrt(jnp.mean(oh * oh, axis=-1, keepdims=True) + EPS)
        n = oh * r
        ns.append(n)
        rs.append(r)
        outs.append(n * hn)
    return outs, ns, rs


def merge_fwd(name, proj, o, yg, glu_lin, head_norm, b_glu):
    t = o.shape[0]
    tm = _rows(t)
    d = D_MODEL

    def body(za_ref, zb_ref, ra_ref, rb_ref, o_ref, yg_ref, gl_ref, hn_ref, bg_ref, m_ref):
        on, _, _ = _head_norm(o_ref[...], hn_ref[...])
        y_a = jnp.concatenate(on, axis=1) * _silu(za_ref[...])
        y_b = yg_ref[...] * _sigmoid(gl_ref[...] + bg_ref[...]) * _silu(zb_ref[...])
        m_ref[...] = (_sigmoid(ra_ref[...]) * y_a + _sigmoid(rb_ref[...]) * y_b).astype(bf16)

    pc = lambda c: pl.BlockSpec((tm, d), lambda i: (i, c))
    return _call(body, name, _sds((t, d), bf16), (t // tm,),
                 [pc(3), pc(5), pc(6), pc(7), _row_spec(tm, d), _row_spec(tm, d), _row_spec(tm, d),
                  _acc_spec(1, 128), _acc_spec(1, d)], _row_spec(tm, d))(
        proj, proj, proj, proj, o, yg, glu_lin, head_norm, b_glu)


def merge_bwd(name, proj, o, yg, glu_lin, head_norm, b_glu, dm):
    t = o.shape[0]
    tm = _rows(t)
    d = D_MODEL

    def body(za_ref, zb_ref, ra_ref, rb_ref, o_ref, yg_ref, gl_ref, hn_ref, bg_ref, dm_ref,
             dza_ref, dzb_ref, dra_ref, drb_ref, do_ref, dgl_ref, dyg_ref, dhn_ref, dbg_ref):
        hn = hn_ref[...]
        za, zb, ra, rb = za_ref[...], zb_ref[...], ra_ref[...], rb_ref[...]
        on, ns, rs = _head_norm(o_ref[...], hn)
        onc = jnp.concatenate(on, axis=1)
        sza = _silu(za)
        y_a = onc * sza
        yg = yg_ref[...]
        sgl = _sigmoid(gl_ref[...] + bg_ref[...])
        y2 = yg * sgl
        szb = _silu(zb)
        y_b = y2 * szb
        sra, srb = _sigmoid(ra), _sigmoid(rb)
        dmv = dm_ref[...]
        dra_ref[...] = dmv * y_a * sra * (1.0 - sra)
        drb_ref[...] = dmv * y_b * srb * (1.0 - srb)
        d_ya = dmv * sra
        d_yb = dmv * srb
        dza_ref[...] = d_ya * onc * _dsilu(za)
        dzb_ref[...] = d_yb * y2 * _dsilu(zb)
        d_on = d_ya * sza
        d_y2 = d_yb * szb
        dyg_ref[...] = d_y2 * sgl
        d_gl = d_y2 * yg * sgl * (1.0 - sgl)
        dgl_ref[...] = d_gl.astype(bf16)
        _accumulate(dbg_ref, jnp.sum(d_gl, axis=0, keepdims=True))
        d_hn = jnp.zeros((1, 128), f32)
        for h in range(DN_HEADS):
            hs = slice(h * 128, (h + 1) * 128)
            dh = d_on[:, hs]
            d_hn = d_hn + jnp.sum(dh * ns[h], axis=0, keepdims=True)
            dn = dh * hn
            do_ref[:, hs] = rs[h] * (dn - ns[h] * jnp.mean(dn * ns[h], axis=-1, keepdims=True))
        _accumulate(dhn_ref, d_hn)

    pc = lambda c: pl.BlockSpec((tm, d), lambda i: (i, c))
    rs_ = _row_spec(tm, d)
    return _call(body, name, (_sds((t, d)),) * 5 + (_sds((t, d), bf16), _sds((t, d)), _sds((1, 128)), _sds((1, d))),
                 (t // tm,), [pc(3), pc(5), pc(6), pc(7), rs_, rs_, rs_, _acc_spec(1, 128), _acc_spec(1, d), rs_],
                 (rs_,) * 7 + (_acc_spec(1, 128), _acc_spec(1, d)))(
        proj, proj, proj, proj, o, yg, glu_lin, head_norm, b_glu, dm)


def gelu_bwd(name, ys_lin, proj, d_skip, dyg_a, dyg_b):
    t, d = ys_lin.shape
    tm = _rows(t)

    def body(y_ref, u_ref, d_ref, a_ref, b_ref, dys_ref, du_ref, dd_ref):
        uv = u_ref[...]
        dys = (a_ref[...] + b_ref[...]) * _dgelu(y_ref[...] + d_ref[...] * uv)
        dys_ref[...] = dys
        du_ref[...] = dys * d_ref[...]
        _accumulate(dd_ref, jnp.sum(dys * uv, axis=0, keepdims=True))

    rs_ = _row_spec(tm, d)
    return _call(body, name, (_sds((t, d)), _sds((t, d)), _sds((1, d))), (t // tm,),
                 [rs_, pl.BlockSpec((tm, d), lambda i: (i, 4)), _acc_spec(1, d), rs_, rs_],
                 (rs_, rs_, _acc_spec(1, d)))(ys_lin, proj, d_skip, dyg_a, dyg_b)


def assemble_dproj(name, dqkv, dza, du_a, du_b, dzb, dra, drb, dbd):
    t = dza.shape[0]
    tm = _rows(t)
    d = D_MODEL

    def body(qkv_ref, za_ref, ua_ref, ub_ref, zb_ref, ra_ref, rb_ref, bd_ref, o_ref):
        o_ref[:, 0:3 * d] = qkv_ref[...].astype(bf16)
        o_ref[:, 3 * d:4 * d] = za_ref[...].astype(bf16)
        o_ref[:, 4 * d:5 * d] = (ua_ref[...] + ub_ref[...]).astype(bf16)
        o_ref[:, 5 * d:6 * d] = zb_ref[...].astype(bf16)
        o_ref[:, 6 * d:7 * d] = ra_ref[...].astype(bf16)
        o_ref[:, 7 * d:8 * d] = rb_ref[...].astype(bf16)
        o_ref[:, 8 * d:8 * d + 128] = bd_ref[...].astype(bf16)
        o_ref[:, 8 * d + 128:W_PAD] = jnp.zeros((tm, W_PAD - 8 * d - 128), bf16)

    rs_ = _row_spec(tm, d)
    return _call(body, name, _sds((t, W_PAD), bf16), (t // tm,),
                 [_row_spec(tm, 3 * d), rs_, rs_, rs_, rs_, rs_, rs_, _row_spec(tm, 128)], _row_spec(tm, W_PAD))(
        dqkv, dza, du_a, du_b, dzb, dra, drb, dbd)


def adamw(name, w, g, m, v):
    lead, (r, c) = w.shape[:-2], w.shape[-2:]
    tm = _tile(r, (512, 256, 128, 64, 32, 16, 8))
    c1 = 1.0 / (1.0 - ADAM_B1 ** ADAM_STEP)
    c2 = 1.0 / (1.0 - ADAM_B2 ** ADAM_STEP)

    def body(w_ref, g_ref, m_ref, v_ref, d_ref, nm_ref, nv_ref):
        gv = g_ref[...]
        nm = ADAM_B1 * m_ref[...] + (1.0 - ADAM_B1) * gv
        nv = ADAM_B2 * v_ref[...] + (1.0 - ADAM_B2) * (gv * gv)
        d_ref[...] = -ADAM_LR * ((nm * c1) / (jnp.sqrt(nv * c2) + ADAM_EPS) + ADAM_WD * w_ref[...])
        nm_ref[...] = nm
        nv_ref[...] = nv

    if lead:
        sp = pl.BlockSpec((None, tm, c), lambda l, i: (l, i, 0))
        grid = (lead[0], r // tm)
    else:
        sp = pl.BlockSpec((tm, c), lambda i: (i, 0))
        grid = (r // tm,)
    return _call(body, name, (_sds(w.shape),) * 3, grid, [sp] * 4, (sp,) * 3)(w, g, m, v)


def _coords():
    return lax.axis_index("x"), lax.axis_index("y"), lax.axis_index("c")


def _lin(dev):
    return 4 * dev[0] + 2 * dev[1] + dev[2]


def _chips(me):
    x, y, _ = me
    return [(1 - x, y), (x, 1 - y), (1 - x, 1 - y)]


def all_gather(name, shard):
    r, c = shard.shape

    def body(x_ref, o_ref, send_sems, recv_sems, local_sem):
        me = _coords()
        x, y, cc = me
        sibling = (x, y, 1 - cc)
        chips = _chips(me)

        def copy(k, block, to, src=None):
            return pltpu.make_async_remote_copy(
                src_ref=o_ref.at[_lin(block)] if src is None else src, dst_ref=o_ref.at[_lin(block)],
                send_sem=send_sems.at[k], recv_sem=recv_sems.at[k], device_id=to, device_id_type=MESH)

        mine = pltpu.make_async_copy(x_ref, o_ref.at[_lin(me)], local_sem)
        mine.start()
        first = [copy(0, me, sibling, src=x_ref)]
        first += [copy(1 + j, me, (*chip, cc), src=x_ref) for j, chip in enumerate(chips)]
        for cp in first:
            cp.start()
        passed = [copy(4 + j, (*chip, cc), sibling) for j, chip in enumerate(chips)]
        for j, chip in enumerate(chips):
            copy(1 + j, (*chip, cc), me).wait_recv()
            passed[j].start()
        copy(0, sibling, me).wait_recv()
        for j, chip in enumerate(chips):
            copy(4 + j, (*chip, 1 - cc), me).wait_recv()
        for cp in first + passed:
            cp.wait_send()
        mine.wait()

    any_spec = pl.BlockSpec(memory_space=pl.ANY)
    return _call(body, name, _sds((N_DEV, r, c), shard.dtype), in_specs=[any_spec], out_specs=any_spec,
                 scratch=[pltpu.SemaphoreType.DMA((N_DEV - 1,)), pltpu.SemaphoreType.DMA((N_DEV - 1,)),
                          pltpu.SemaphoreType.DMA(())])(shard)


def pair_exchange(name, blocks):
    _, _, r, c = blocks.shape

    def body(x_ref, o_ref, send_sems, recv_sems):
        x, y, cc = _coords()
        sibling = (x, y, 1 - cc)
        cps = [pltpu.make_async_remote_copy(src_ref=x_ref.at[ch, 1 - cc], dst_ref=o_ref.at[ch], send_sem=send_sems.at[ch],
                                            recv_sem=recv_sems.at[ch], device_id=sibling, device_id_type=MESH)
               for ch in range(4)]
        for cp in cps:
            cp.start()
        for cp in cps:
            cp.wait()

    any_spec = pl.BlockSpec(memory_space=pl.ANY)
    return _call(body, name, _sds((4, r, c), blocks.dtype), in_specs=[any_spec], out_specs=any_spec,
                 scratch=[pltpu.SemaphoreType.DMA((4,)), pltpu.SemaphoreType.DMA((4,))])(blocks)


def chip_exchange(name, blocks):
    _, r, c = blocks.shape

    def body(x_ref, o_ref, send_sems, recv_sems, local_sem):
        me = _coords()
        x, y, cc = me
        my_chip = 2 * x + y
        mine = pltpu.make_async_copy(x_ref.at[my_chip], o_ref.at[my_chip], local_sem)
        mine.start()
        cps = []
        for j, (px, py) in enumerate(_chips(me)):
            cp = pltpu.make_async_remote_copy(src_ref=x_ref.at[2 * px + py], dst_ref=o_ref.at[my_chip],
                                              send_sem=send_sems.at[j], recv_sem=recv_sems.at[j],
                                              device_id=(px, py, cc), device_id_type=MESH)
            cp.start()
            cps.append(cp)
        for j, (px, py) in enumerate(_chips(me)):
            pltpu.make_async_remote_copy(src_ref=x_ref.at[my_chip], dst_ref=o_ref.at[2 * px + py],
                                         send_sem=send_sems.at[j], recv_sem=recv_sems.at[j],
                                         device_id=(px, py, cc), device_id_type=MESH).wait_recv()
        for cp in cps:
            cp.wait_send()
        mine.wait()

    any_spec = pl.BlockSpec(memory_space=pl.ANY)
    return _call(body, name, _sds(blocks.shape, blocks.dtype), in_specs=[any_spec], out_specs=any_spec,
                 scratch=[pltpu.SemaphoreType.DMA((3,)), pltpu.SemaphoreType.DMA((3,)),
                          pltpu.SemaphoreType.DMA(())])(blocks)


def pair_sum(name, mine, theirs):
    _, r, c = mine.shape
    tm = _tile(r, (240, 256, 128, 64, 32, 16))

    def body(a_ref, b_ref, o_ref):
        o_ref[...] = (a_ref[...].astype(f32) + b_ref[...].astype(f32)).astype(o_ref.dtype)

    sp = pl.BlockSpec((None, tm, c), lambda ch, i: (ch, i, 0))
    return _call(body, name, _sds(mine.shape, mine.dtype), (4, r // tm), [sp, sp], sp)(mine, theirs)


def sum_slots(name, slots):
    n, r, c = slots.shape
    tm = _tile(r, (240, 256, 128, 64, 32, 16))

    def body(s_ref, o_ref):
        acc = s_ref[0].astype(f32)
        for d in range(1, n):
            acc = acc + s_ref[d].astype(f32)
        o_ref[...] = acc

    return _call(body, name, _sds((r, c)), (r // tm,), [pl.BlockSpec((n, tm, c), lambda i: (0, i, 0))],
                 pl.BlockSpec((tm, c), lambda i: (i, 0)))(slots)


def _rows_layout(col8, t):
    return col8.T.reshape(DN_HEADS, t // CHUNK, CHUNK)


def _blockdiag(m):
    g, a, b = m.shape
    m = m.reshape(N_SUPER, SUPER, a, b)
    out = jnp.einsum("sgab,gh->sgahb", m, jnp.eye(SUPER, dtype=m.dtype))
    return out.reshape(N_SUPER, SUPER * a, SUPER * b)


def _diag_blocks(m, a, b):
    m = m.reshape(N_SUPER, SUPER, a, SUPER, b)
    return jnp.einsum("sgahb,gh->sgab", m, jnp.eye(SUPER, dtype=m.dtype)).reshape(SSM_GROUPS, a, b)


def _s5_params(p, li):
    tag = f"l{li}"
    n = SSM_STATE
    a_re = p["ssm_a_re"].reshape(SSM_GROUPS, 1, n)
    a_im = p["ssm_a_im"].reshape(SSM_GROUPS, 1, n)
    log_dt = p["ssm_log_dt"].reshape(SSM_GROUPS, 1, 1)
    br = jnp.swapaxes(p["ssm_b_re"], 1, 2)
    bi = jnp.swapaxes(p["ssm_b_im"], 1, 2)
    lr, li_, bbr, bbi = s5_disc_fwd("s5_disc_" + tag, a_re, a_im, log_dt, br, bi)
    lam = jnp.concatenate([lr.reshape(N_SUPER, 1, SG_STATE), li_.reshape(N_SUPER, 1, SG_STATE)], axis=-1)
    bbd = jnp.concatenate([_blockdiag(bbr), _blockdiag(bbi)], axis=-1).astype(bf16)
    c_re = jnp.swapaxes(p["ssm_c_re"], 1, 2)
    c_im = jnp.swapaxes(p["ssm_c_im"], 1, 2)
    cbd = jnp.concatenate([_blockdiag(c_re), -_blockdiag(c_im)], axis=1).astype(bf16)
    return dict(a_re=a_re, a_im=a_im, log_dt=log_dt, br=br, bi=bi, lam=lam, bbd=bbd, cbd=cbd)


def layer_fwd(x, p, li):
    tag = f"l{li}"
    t = x.shape[0]
    d = D_MODEL
    h = rmsnorm_fwd("norm_pre_" + tag, x, p["norm_pre"])
    proj = mm_nn("proj_" + tag, h, p["w_all"])
    q, k, v, bg = conv_qkv_fwd("conv_" + tag, proj, p["conv_w"], p["a_log"], p["dt_bias"])
    b_rows = _rows_layout(bg[:, 0:DN_HEADS], t)
    g_rows = _rows_layout(bg[:, DN_HEADS:2 * DN_HEADS], t)
    o, s_all = delta_fwd("delta_" + tag, q, k, v, g_rows, b_rows)
    sp = _s5_params(p, li)
    u_perm = _perm_rows(proj[:, 4 * d:5 * d])
    ys_perm, hs = s5_fwd("s5_" + tag, u_perm, sp["bbd"], sp["cbd"], sp["lam"])
    ys_lin = _unperm_rows(ys_perm)
    yg = gelu_fwd("gelu_" + tag, ys_lin, proj, p["ssm_d"])
    glu_lin = mm_nn("glu_" + tag, yg, p["w_glu"])
    merged = merge_fwd("merge_" + tag, proj, o, yg, glu_lin, p["head_norm"], p["b_glu"])
    out = mm_nn("out_" + tag, merged, p["w_out"])
    y = residual_norm_fwd("norm_post_" + tag, x, out, p["norm_post"])
    saved = dict(x=x, h=h, proj=proj, q=q, k=k, v=v, g_rows=g_rows, b_rows=b_rows, o=o, s_all=s_all, sp=sp, u_perm=u_perm,
                 hs=hs, ys_lin=ys_lin, yg=yg, glu_lin=glu_lin, merged=merged, out=out)
    return y, saved


def layer_bwd(dy, p, s, li):
    tag = f"l{li}"
    t = dy.shape[0]
    d = D_MODEL
    sp = s["sp"]
    gr = {}
    d_out, gr["norm_post"] = post_norm_bwd("norm_post_b_" + tag, s["out"], p["norm_post"], dy)
    d_merged = mm_nt("out_b_" + tag, d_out, p["w_out"])
    gr["w_out"] = mm_tn("out_w_" + tag, s["merged"], d_out, bf16)
    (dza, dzb, dra, drb, d_o, d_glu, dyg_a, gr["head_norm"], gr["b_glu"]) = merge_bwd(
        "merge_b_" + tag, s["proj"], s["o"], s["yg"], s["glu_lin"], p["head_norm"], p["b_glu"], d_merged)
    dyg_b = mm_nt("glu_b_" + tag, d_glu, p["w_glu"])
    gr["w_glu"] = mm_tn("glu_w_" + tag, s["yg"], d_glu, bf16)
    d_ys, du_a, gr["ssm_d"] = gelu_bwd("gelu_b_" + tag, s["ys_lin"], s["proj"], p["ssm_d"], dyg_a, dyg_b)
    hprev0 = jnp.concatenate([jnp.zeros((1, s["hs"].shape[1]), f32), s["hs"][-SEG:-1]], axis=0)
    du_perm, d_bbd, d_cbd, d_lam = s5_bwd("s5_b_" + tag, _perm_rows(d_ys), s["u_perm"], s["hs"], hprev0, sp["bbd"],
                                          sp["cbd"], sp["lam"])
    du_b = _unperm_rows(du_perm)
    d_c_re = _diag_blocks(d_cbd[:, 0:SG_STATE, :], SSM_STATE, SSM_GROUP)
    d_c_im = -_diag_blocks(d_cbd[:, SG_STATE:, :], SSM_STATE, SSM_GROUP)
    gr["ssm_c_re"] = jnp.swapaxes(d_c_re, 1, 2)
    gr["ssm_c_im"] = jnp.swapaxes(d_c_im, 1, 2)
    d_bbr = _diag_blocks(d_bbd[:, :, 0:SG_STATE], SSM_GROUP, SSM_STATE)
    d_bbi = _diag_blocks(d_bbd[:, :, SG_STATE:], SSM_GROUP, SSM_STATE)
    d_lr = d_lam[:, :, 0:SG_STATE].reshape(SSM_GROUPS, 1, SSM_STATE)
    d_li = d_lam[:, :, SG_STATE:].reshape(SSM_GROUPS, 1, SSM_STATE)
    d_are, d_aim, d_ldt, d_br, d_bi = s5_disc_bwd("s5_disc_b_" + tag, sp["a_re"], sp["a_im"], sp["log_dt"], sp["br"],
                                                  sp["bi"], d_lr, d_li, d_bbr, d_bbi)
    gr["ssm_a_re"] = d_are.reshape(SSM_GROUPS, SSM_STATE)
    gr["ssm_a_im"] = d_aim.reshape(SSM_GROUPS, SSM_STATE)
    gr["ssm_log_dt"] = d_ldt.reshape(SSM_GROUPS)
    gr["ssm_b_re"] = jnp.swapaxes(d_br, 1, 2)
    gr["ssm_b_im"] = jnp.swapaxes(d_bi, 1, 2)
    dq, dk, dv, dg_rows, db_rows = delta_bwd("delta_b_" + tag, s["q"], s["k"], s["v"], s["g_rows"], s["b_rows"],
                                             s["s_all"], d_o)
    dbg = jnp.concatenate([db_rows.reshape(DN_HEADS, t).T, dg_rows.reshape(DN_HEADS, t).T,
                           jnp.zeros((t, 128 - 2 * DN_HEADS), f32)], axis=1)
    dc, gr["conv_w"], dbd, dab = conv_qkv_bwd("conv_b_" + tag, s["proj"], p["conv_w"], p["a_log"], p["dt_bias"],
                                              dq, dk, dv, dbg)
    gr["a_log"] = dab[0, DN_HEADS:2 * DN_HEADS]
    gr["dt_bias"] = dab[1, DN_HEADS:2 * DN_HEADS]
    dqkv = conv_transpose("conv_t_" + tag, dc, p["conv_w"])
    dproj = assemble_dproj("dproj_" + tag, dqkv, dza, du_a, du_b, dzb, dra, drb, dbd)
    d_h = mm_nt("proj_b_" + tag, dproj, p["w_all"])
    gr["w_all"] = mm_tn("proj_w_" + tag, s["h"], dproj, bf16)
    dx, gr["norm_pre"] = rmsnorm_bwd("norm_pre_b_" + tag, s["x"], p["norm_pre"], d_h, dy)
    return dx, gr


REPL = ["norm_pre", "a_log", "dt_bias", "head_norm", "ssm_a_re", "ssm_a_im", "ssm_log_dt", "ssm_b_re", "ssm_b_im",
        "ssm_c_re", "ssm_c_im", "ssm_d", "b_glu", "norm_post"]
SHARDED = ["w_in", "conv_w", "w_glu", "w_out"]
ALL_W = ["norm_pre", "w_in", "conv_w", "a_log", "dt_bias", "head_norm", "ssm_a_re", "ssm_a_im", "ssm_log_dt",
         "ssm_b_re", "ssm_b_im", "ssm_c_re", "ssm_c_im", "ssm_d", "w_glu", "b_glu", "w_out", "norm_post"]
PACK_W = 1024


def _pack_flat(arrs, rows):
    flat = jnp.concatenate([a.reshape(-1) for a in arrs])
    return jnp.pad(flat, (0, rows * PACK_W - flat.shape[0])).reshape(rows, PACK_W)


def _flat_rows(arrs, mult=8):
    n = sum(math.prod(a.shape) for a in arrs)
    rows = -(-n // PACK_W)
    return -(-rows // mult) * mult


def _unpack(flat, shapes):
    out, off = [], 0
    for sh in shapes:
        n = math.prod(sh)
        out.append(flat[off:off + n].reshape(sh))
        off += n
    return out


def _repl_split(shapes):
    big = [n for n in REPL if math.prod(shapes[n]) % PACK_W == 0]
    small = [n for n in REPL if n not in big]
    return big, small


def _rows8(n):
    return -(-n // (8 * PACK_W)) * 8


def _repl_rows(shapes):
    big, small = _repl_split(shapes)
    rows = sum(_rows8(math.prod(shapes[n])) for n in big) + _flat_rows([_sds(shapes[n]) for n in small], 8)
    return -(-rows // (8 * N_DEV)) * (8 * N_DEV)


def _repl_pack(arrs, shapes):
    big, small = _repl_split(shapes)
    parts = []
    for n in big:
        a = arrs[n].reshape(-1, PACK_W)
        parts.append(jnp.pad(a, ((0, _rows8(a.size) - a.shape[0]), (0, 0))))
    parts.append(_pack_flat([arrs[n] for n in small], _flat_rows([_sds(shapes[n]) for n in small], 8)))
    used = sum(p.shape[0] for p in parts)
    parts.append(jnp.zeros((_repl_rows(shapes) - used, PACK_W), parts[0].dtype))
    return jnp.concatenate(parts, axis=0)


def _repl_unpack(packed, shapes):
    big, small = _repl_split(shapes)
    out, off = {}, 0
    for n in big:
        size = math.prod(shapes[n])
        out[n] = packed[off:off + size // PACK_W].reshape(shapes[n])
        off += _rows8(size)
    srows = _flat_rows([_sds(shapes[n]) for n in small], 8)
    out.update(zip(small, _unpack(packed[off:off + srows].reshape(-1), [shapes[n] for n in small])))
    return out


_COL_RUNS = ((0, 4096), (4112, W_COLS), (4096, 4112))


def _w_all(main, tails, wc):
    pieces = []
    for lo, hi in _COL_RUNS:
        for dv in range(N_DEV):
            a, b = max(lo, dv * wc) - dv * wc, min(hi, (dv + 1) * wc) - dv * wc
            if a < min(b, PACK_W):
                pieces.append(main[dv][:, :, a:min(b, PACK_W)])
            if b > max(a, PACK_W):
                pieces.append(tails[dv][:, :, max(a, PACK_W) - PACK_W:b - PACK_W])
    ll, rows = main[0].shape[0], main[0].shape[1]
    pieces.append(jnp.zeros((ll, rows, W_PAD - W_COLS), main[0].dtype))
    return jnp.concatenate(pieces, axis=2)


def _ref_cols(g, lo, hi):
    pieces, off = [], 0
    for a, b in _COL_RUNS:
        s, e = max(lo, a), min(hi, b)
        if s < e:
            pieces.append((s, g[..., off + s - a:off + e - a]))
        off += b - a
    pieces.sort(key=lambda t: t[0])
    return jnp.concatenate([p for _, p in pieces], axis=-1) if len(pieces) > 1 else pieces[0][1]


def kernel(x, norm_pre, w_in, conv_w, a_log, dt_bias, head_norm, ssm_a_re, ssm_a_im, ssm_log_dt, ssm_b_re, ssm_b_im, ssm_c_re, ssm_c_im, ssm_d, w_glu, b_glu, w_out, norm_post, loss_target, m_norm_pre, m_w_in, m_conv_w, m_a_log, m_dt_bias, m_head_norm, m_ssm_a_re, m_ssm_a_im, m_ssm_log_dt, m_ssm_b_re, m_ssm_b_im, m_ssm_c_re, m_ssm_c_im, m_ssm_d, m_w_glu, m_b_glu, m_w_out, m_norm_post, v_norm_pre, v_w_in, v_conv_w, v_a_log, v_dt_bias, v_head_norm, v_ssm_a_re, v_ssm_a_im, v_ssm_log_dt, v_ssm_b_re, v_ssm_b_im, v_ssm_c_re, v_ssm_c_im, v_ssm_d, v_w_glu, v_b_glu, v_w_out, v_norm_post):
    loc = dict(locals())
    w = {n: loc[n] for n in ALL_W}
    m = {n: loc["m_" + n] for n in ALL_W}
    v = {n: loc["v_" + n] for n in ALL_W}
    depth = w_in.shape[0]
    wc = w_in.shape[2]
    cc = conv_w.shape[2]
    wr = w_glu.shape[1]

    tail = wc - PACK_W
    conv_hi = conv_w.astype(bf16)
    conv_mid = (conv_w - conv_hi.astype(f32)).astype(bf16)
    conv_lo = (conv_w - conv_hi.astype(f32) - conv_mid.astype(f32)).astype(bf16)
    w_in_b = w_in.astype(bf16)
    misc_shapes = [(depth, D_MODEL, tail), (3, depth, CONV_K, cc)]
    misc_rows = _flat_rows([_sds(sh, bf16) for sh in misc_shapes], 16)
    n_main, n_sq = depth * D_MODEL, depth * wr
    shard = jnp.concatenate([w_in_b[:, :, :PACK_W].reshape(n_main, PACK_W), w_glu.astype(bf16).reshape(n_sq, PACK_W),
                             w_out.astype(bf16).reshape(n_sq, PACK_W),
                             _pack_flat([w_in_b[:, :, PACK_W:], jnp.stack([conv_hi, conv_mid, conv_lo])], misc_rows)])
    gathered = all_gather("gather_weights", shard)
    miscs = [_unpack(gathered[dv, n_main + 2 * n_sq:].reshape(-1), misc_shapes) for dv in range(N_DEV)]
    w_all_full = _w_all([gathered[dv, :n_main].reshape(depth, D_MODEL, PACK_W) for dv in range(N_DEV)],
                        [mi[0] for mi in miscs], wc)
    w_glu_full = gathered[:, n_main:n_main + n_sq].reshape(N_DEV, depth, wr, D_MODEL)
    w_glu_full = jnp.swapaxes(w_glu_full, 0, 1).reshape(depth, N_DEV * wr, D_MODEL)
    w_out_full = gathered[:, n_main + n_sq:n_main + 2 * n_sq].reshape(N_DEV, depth, wr, D_MODEL)
    w_out_full = jnp.swapaxes(w_out_full, 0, 1).reshape(depth, N_DEV * wr, D_MODEL)
    conv_full = jnp.concatenate([mi[1][0].astype(f32) + mi[1][1].astype(f32) + mi[1][2].astype(f32)
                                 for mi in miscs], axis=2)

    def layer_params(li):
        return dict(norm_pre=norm_pre[li].reshape(1, -1), w_all=w_all_full[li], conv_w=conv_full[li],
                    a_log=jnp.pad(a_log[li], (DN_HEADS, 128 - 2 * DN_HEADS)).reshape(1, 128),
                    dt_bias=jnp.pad(dt_bias[li], (DN_HEADS, 128 - 2 * DN_HEADS)).reshape(1, 128),
                    head_norm=head_norm[li].reshape(1, -1), ssm_a_re=ssm_a_re[li], ssm_a_im=ssm_a_im[li],
                    ssm_log_dt=ssm_log_dt[li], ssm_b_re=ssm_b_re[li], ssm_b_im=ssm_b_im[li], ssm_c_re=ssm_c_re[li],
                    ssm_c_im=ssm_c_im[li], ssm_d=ssm_d[li].reshape(1, -1), w_glu=w_glu_full[li],
                    b_glu=b_glu[li].reshape(1, -1), w_out=w_out_full[li], norm_post=norm_post[li].reshape(1, -1))

    act = x[0]
    saved, params = [], []
    for li in range(depth):
        params.append(layer_params(li))
        act, sv = layer_fwd(act, params[li], li)
        saved.append(sv)
    loss_part, dy = loss_head("loss_head", act, loss_target[0])
    grads = [None] * depth
    for li in reversed(range(depth)):
        dy, grads[li] = layer_bwd(dy, params[li], saved[li], li)
    grad_x = dy[None]
    loss = lax.psum(loss_part[0, 0], ("x", "y", "c"))

    def stack(name):
        return jnp.stack([grads[li][name] for li in range(depth)])

    g_w_all = stack("w_all")
    g_conv = stack("conv_w").astype(bf16)
    g_glu = stack("w_glu").reshape(depth, N_DEV, wr, D_MODEL)
    g_out = stack("w_out").reshape(depth, N_DEV, wr, D_MODEL)
    repl_shapes = {n: w[n].shape for n in REPL}
    repl_rows = _repl_rows(repl_shapes)
    g_repl = _repl_pack({n: stack(n).reshape(w[n].shape) for n in REPL}, repl_shapes).astype(bf16)
    rr = repl_rows // N_DEV
    gmisc_shapes = [(depth, D_MODEL, tail), (depth, CONV_K, cc)]
    gmisc_rows = _flat_rows([_sds(sh, bf16) for sh in gmisc_shapes], 16)
    used = n_main + 2 * n_sq + gmisc_rows + rr
    blocks = jnp.stack([jnp.concatenate([
        _ref_cols(g_w_all, dv * wc, dv * wc + PACK_W).reshape(n_main, PACK_W), g_glu[:, dv].reshape(n_sq, PACK_W),
        g_out[:, dv].reshape(n_sq, PACK_W),
        _pack_flat([_ref_cols(g_w_all, dv * wc + PACK_W, (dv + 1) * wc), g_conv[:, :, dv * cc:(dv + 1) * cc]],
                   gmisc_rows),
        g_repl[dv * rr:(dv + 1) * rr], jnp.zeros((-used % 16, PACK_W), bf16)]) for dv in range(N_DEV)])
    blocks = blocks.reshape(4, 2, blocks.shape[1], PACK_W)
    from_sibling = pair_exchange("pair_grads", blocks)
    own = lax.dynamic_index_in_dim(blocks, lax.axis_index("c"), axis=1, keepdims=False)
    chip_part = pair_sum("pair_sum_grads", own, from_sibling)
    slots = chip_exchange("scatter_grads", chip_part)
    mine = sum_slots("sum_grads", slots)
    o1, o2, o3, o4 = n_main, n_main + n_sq, n_main + 2 * n_sq, n_main + 2 * n_sq + gmisc_rows
    gs_tail, gs_conv = _unpack(mine[o3:o4].reshape(-1), gmisc_shapes)
    gs_w_in = jnp.concatenate([mine[:o1].reshape(depth, D_MODEL, PACK_W), gs_tail], axis=2)
    gs_glu = mine[o1:o2].reshape(depth, wr, D_MODEL)
    gs_out = mine[o2:o3].reshape(depth, wr, D_MODEL)
    g_repl_full = all_gather("gather_repl_grads", mine[o4:o4 + rr]).reshape(repl_rows, PACK_W)
    g = _repl_unpack(g_repl_full, repl_shapes)
    g.update(w_in=gs_w_in, w_glu=gs_glu, w_out=gs_out, conv_w=gs_conv)

    delta, new_m, new_v = {}, {}, {}
    for n in SHARDED:
        delta[n], new_m[n], new_v[n] = adamw("adamw_" + n, w[n], g[n], m[n], v[n])
    outs = adamw("adamw_repl", *[_repl_pack({n: src[n] for n in REPL}, repl_shapes) for src in (w, g, m, v)])
    for dst, o in zip((delta, new_m, new_v), outs):
        dst.update(_repl_unpack(o, repl_shapes))
    return (loss, grad_x, *[g[n] for n in ALL_W], *[delta[n] for n in ALL_W], *[new_m[n] for n in ALL_W],
            *[new_v[n] for n in ALL_W])
```

```python
import math

import jax
import jax.numpy as jnp
from jax import lax
from jax.experimental import pallas as pl
from jax.experimental.pallas import tpu as pltpu

f32 = jnp.float32
bf16 = jnp.bfloat16

D_MODEL = 1024
DEPTH = 4
N_DEV = 8
DN_HEADS = 8
HEAD_DIM = 128
CHUNK = 64
CONV_K = 4
SSM_GROUPS = 64
SSM_GROUP = 16
SSM_STATE = 64
SUPER = 8
N_SUPER = SSM_GROUPS // SUPER
SG_STATE = SUPER * SSM_STATE
EPS = 1e-6
W_COLS = 8208
W_PAD = 8448
ADAM_LR, ADAM_B1, ADAM_B2, ADAM_EPS, ADAM_WD, ADAM_STEP = 0.001, 0.9, 0.999, 1e-08, 0.01, 10
VMEM_LIMIT = 56 * 1024 * 1024
MESH = pl.DeviceIdType.MESH
HIGH = lax.Precision.HIGH
DELTA_HB = 8


def _call(body, name, out_shape, grid=None, in_specs=None, out_specs=None, scratch=(), **kw):
    args = dict(out_shape=out_shape, name=name, scratch_shapes=list(scratch),
                compiler_params=pltpu.CompilerParams(vmem_limit_bytes=VMEM_LIMIT, **kw))
    if grid is not None:
        args.update(grid=grid, in_specs=in_specs, out_specs=out_specs)
    else:
        if in_specs is not None:
            args.update(in_specs=in_specs)
        if out_specs is not None:
            args.update(out_specs=out_specs)
    return pl.pallas_call(body, **args)


def _sds(shape, dtype=f32):
    return jax.ShapeDtypeStruct(tuple(shape), dtype)


def _sigmoid(x):
    return 1.0 / (1.0 + jnp.exp(-x))


def _silu(x):
    return x * _sigmoid(x)


def _dsilu(x):
    s = _sigmoid(x)
    return s * (1.0 + x * (1.0 - s))


_GELU_C = math.sqrt(2.0 / math.pi)


def _gelu(x):
    return 0.5 * x * (1.0 + jnp.tanh(_GELU_C * (x + 0.044715 * x * x * x)))


def _dgelu(x):
    t = jnp.tanh(_GELU_C * (x + 0.044715 * x * x * x))
    return 0.5 * (1.0 + t) + 0.5 * x * (1.0 - t * t) * _GELU_C * (1.0 + 3 * 0.044715 * x * x)


def _softplus(x):
    return jnp.maximum(x, 0.0) + jnp.log(1.0 + jnp.exp(-jnp.abs(x)))


def _bdot(a, b, dn):
    return lax.dot_general(a.astype(bf16), b.astype(bf16), (dn, ((), ())), preferred_element_type=f32)


def _matmul(name, a, b, *, dn, grid, a_spec, b_spec, o_spec, o_shape, o_dtype=f32):
    nk = grid[-1]

    def body(a_ref, b_ref, o_ref, acc_ref):
        p = _bdot(a_ref[...], b_ref[...], dn)
        if nk == 1:
            o_ref[...] = p.astype(o_dtype)
        else:
            k = pl.program_id(len(grid) - 1)

            @pl.when(k == 0)
            def _():
                acc_ref[...] = p

            @pl.when(k > 0)
            def _():
                acc_ref[...] += p

            @pl.when(k == nk - 1)
            def _():
                o_ref[...] = acc_ref[...].astype(o_dtype)

    blk = tuple(d for d in o_spec.block_shape if d is not None)
    return _call(body, name, _sds(o_shape, o_dtype), grid, [a_spec, b_spec], o_spec,
                 scratch=[pltpu.VMEM(blk if nk > 1 else (8, 128), f32)])(a, b)


def _tile(n, pref):
    for t in pref:
        if n % t == 0:
            return t
    return n


def mm_nn(name, a, b):
    m, k = a.shape
    n = b.shape[1]
    tm, tn, tk = _tile(m, (512, 256)), _tile(n, (2816, 1024, 512)), _tile(k, (2816, 1024))
    return _matmul(name, a, b, dn=((1,), (0,)), grid=(m // tm, n // tn, k // tk),
                   a_spec=pl.BlockSpec((tm, tk), lambda i, j, l: (i, l)),
                   b_spec=pl.BlockSpec((tk, tn), lambda i, j, l: (l, j)),
                   o_spec=pl.BlockSpec((tm, tn), lambda i, j, l: (i, j)), o_shape=(m, n))


def mm_nt(name, a, b):
    m, k = a.shape
    n = b.shape[0]
    tm, tn, tk = _tile(m, (512, 256)), _tile(n, (1024, 512)), _tile(k, (2816, 1024))
    return _matmul(name, a, b, dn=((1,), (1,)), grid=(m // tm, n // tn, k // tk),
                   a_spec=pl.BlockSpec((tm, tk), lambda i, j, l: (i, l)),
                   b_spec=pl.BlockSpec((tn, tk), lambda i, j, l: (j, l)),
                   o_spec=pl.BlockSpec((tm, tn), lambda i, j, l: (i, j)), o_shape=(m, n))


def mm_tn(name, a, b, o_dtype=f32):
    k, m = a.shape
    n = b.shape[1]
    tm, tn, tk = _tile(m, (512,)), _tile(n, (2816, 1024, 512)), _tile(k, (512, 256))
    return _matmul(name, a, b, dn=((0,), (0,)), grid=(m // tm, n // tn, k // tk),
                   a_spec=pl.BlockSpec((tk, tm), lambda i, j, l: (l, i)),
                   b_spec=pl.BlockSpec((tk, tn), lambda i, j, l: (l, j)),
                   o_spec=pl.BlockSpec((tm, tn), lambda i, j, l: (i, j)), o_shape=(m, n), o_dtype=o_dtype)


def _rows(t):
    return _tile(t, (256,))


def _row_spec(tm, w):
    return pl.BlockSpec((tm, w), lambda i: (i, 0))


def _acc_spec(r, w):
    return pl.BlockSpec((r, w), lambda i: (0, 0))


def _accumulate(ref, val):
    @pl.when(pl.program_id(0) == 0)
    def _():
        ref[...] = val

    @pl.when(pl.program_id(0) > 0)
    def _():
        ref[...] += val


def rmsnorm_fwd(name, x, gain):
    t, d = x.shape
    tm = _rows(t)

    def body(x_ref, g_ref, o_ref):
        xv = x_ref[...]
        r = lax.rsqrt(jnp.mean(xv * xv, axis=-1, keepdims=True) + EPS)
        o_ref[...] = (xv * r * g_ref[...]).astype(bf16)

    return _call(body, name, _sds((t, d), bf16), (t // tm,), [_row_spec(tm, d), _acc_spec(1, d)],
                 _row_spec(tm, d))(x, gain)


def rmsnorm_bwd(name, x, gain, dn, dres):
    t, d = x.shape
    tm = _rows(t)

    def body(x_ref, g_ref, dn_ref, dr_ref, dx_ref, dg_ref):
        xv = x_ref[...]
        r = lax.rsqrt(jnp.mean(xv * xv, axis=-1, keepdims=True) + EPS)
        n = xv * r
        dnv = dn_ref[...]
        _accumulate(dg_ref, jnp.sum(dnv * n, axis=0, keepdims=True))
        dng = dnv * g_ref[...]
        dx_ref[...] = dr_ref[...] + r * (dng - n * jnp.mean(dng * n, axis=-1, keepdims=True))

    return _call(body, name, (_sds((t, d)), _sds((1, d))), (t // tm,),
                 [_row_spec(tm, d), _acc_spec(1, d), _row_spec(tm, d), _row_spec(tm, d)],
                 (_row_spec(tm, d), _acc_spec(1, d)))(x, gain, dn, dres)


def residual_norm_fwd(name, x, out, gain):
    t, d = x.shape
    tm = _rows(t)

    def body(x_ref, o_ref, g_ref, y_ref):
        ov = o_ref[...]
        r = lax.rsqrt(jnp.mean(ov * ov, axis=-1, keepdims=True) + EPS)
        y_ref[...] = x_ref[...] + ov * r * g_ref[...]

    return _call(body, name, _sds((t, d)), (t // tm,), [_row_spec(tm, d), _row_spec(tm, d), _acc_spec(1, d)],
                 _row_spec(tm, d))(x, out, gain)


def post_norm_bwd(name, out, gain, dy):
    t, d = out.shape
    tm = _rows(t)

    def body(o_ref, g_ref, dy_ref, do_ref, dg_ref):
        ov = o_ref[...]
        r = lax.rsqrt(jnp.mean(ov * ov, axis=-1, keepdims=True) + EPS)
        n = ov * r
        dyv = dy_ref[...]
        _accumulate(dg_ref, jnp.sum(dyv * n, axis=0, keepdims=True))
        dng = dyv * g_ref[...]
        do_ref[...] = (r * (dng - n * jnp.mean(dng * n, axis=-1, keepdims=True))).astype(bf16)

    return _call(body, name, (_sds((t, d), bf16), _sds((1, d))), (t // tm,),
                 [_row_spec(tm, d), _acc_spec(1, d), _row_spec(tm, d)],
                 (_row_spec(tm, d), _acc_spec(1, d)))(out, gain, dy)


def loss_head(name, y, target):
    t, d = y.shape
    tm = _rows(t)

    def body(y_ref, t_ref, l_ref, dy_ref):
        e = y_ref[...] - t_ref[...]
        dy_ref[...] = e * (1.0 / d)
        s = jnp.sum(jnp.sum(e * e, axis=1, keepdims=True), axis=0, keepdims=True) * (0.5 / d)
        _accumulate(l_ref, s)

    return _call(body, name, (_sds((1, 1)), _sds((t, d))), (t // tm,),
                 [_row_spec(tm, d), _row_spec(tm, d)], (_acc_spec(1, 1), _row_spec(tm, d)))(y, target)


def _prev_spec(tm, w):
    return pl.BlockSpec((8, w), lambda i: (jnp.maximum(i * (tm // 8) - 1, 0), 0))


def _next_spec(tm, w, t):
    return pl.BlockSpec((8, w), lambda i: (jnp.minimum((i + 1) * (tm // 8), t // 8 - 1), 0))


def _fill_pad(pad_ref, prev_ref, cur_ref, tm):
    keep = (pl.program_id(0) > 0).astype(f32)
    pad_ref[0:8, :] = prev_ref[...] * keep
    pad_ref[8:8 + tm, :] = cur_ref[...]


def _conv_block(pad_ref, w_ref, cb, tm):
    cs = slice(cb * 128, (cb + 1) * 128)
    acc = pad_ref[pl.ds(8 - (CONV_K - 1), tm), cs] * w_ref[0:1, cs]
    for j in range(1, CONV_K):
        acc = acc + pad_ref[pl.ds(8 - (CONV_K - 1) + j, tm), cs] * w_ref[j:j + 1, cs]
    return acc


def conv_qkv_fwd(name, proj, conv_w, a_log, dt_bias):
    t = proj.shape[0]
    tm = _rows(t)
    scale = HEAD_DIM ** -0.5

    def body(cur_ref, prev_ref, w_ref, bd_ref, al_ref, db_ref, q_ref, k_ref, v_ref, bg_ref, pad_ref):
        _fill_pad(pad_ref, prev_ref, cur_ref, tm)
        for cb in range(3 * DN_HEADS):
            s = _silu(_conv_block(pad_ref, w_ref, cb, tm))
            hs = slice((cb % DN_HEADS) * 128, (cb % DN_HEADS + 1) * 128)
            if cb < DN_HEADS:
                q_ref[:, hs] = s * (lax.rsqrt(jnp.sum(s * s, axis=-1, keepdims=True) + EPS) * scale)
            elif cb < 2 * DN_HEADS:
                k_ref[:, hs] = s * lax.rsqrt(jnp.sum(s * s, axis=-1, keepdims=True) + EPS)
            else:
                v_ref[:, hs] = s
        bd = bd_ref[...]
        beta = _sigmoid(bd)
        g = -jnp.exp(al_ref[...]) * _softplus(bd + db_ref[...])
        lane = lax.broadcasted_iota(jnp.int32, bd.shape, 1)
        bg_ref[...] = jnp.where(lane < DN_HEADS, beta, jnp.where(lane < 2 * DN_HEADS, g, 0.0))

    w3 = 3 * D_MODEL
    return _call(body, name, (_sds((t, D_MODEL)),) * 3 + (_sds((t, 128)),), (t // tm,),
                 [pl.BlockSpec((tm, w3), lambda i: (i, 0)), _prev_spec(tm, w3), _acc_spec(CONV_K, w3),
                  pl.BlockSpec((tm, 128), lambda i: (i, 8192 // 128)), _acc_spec(1, 128), _acc_spec(1, 128)],
                 (_row_spec(tm, D_MODEL),) * 3 + (_row_spec(tm, 128),),
                 scratch=[pltpu.VMEM((tm + 8, w3), f32)])(proj, proj, conv_w, proj, a_log, dt_bias)


def conv_qkv_bwd(name, proj, conv_w, a_log, dt_bias, dq, dk, dv, dbg):
    t = proj.shape[0]
    tm = _rows(t)
    scale = HEAD_DIM ** -0.5

    def body(cur_ref, prev_ref, w_ref, bd_ref, al_ref, db_ref, dq_ref, dk_ref, dv_ref, dbg_ref,
             dc_ref, dw_ref, dbd_ref, dab_ref, pad_ref):
        _fill_pad(pad_ref, prev_ref, cur_ref, tm)

        @pl.when(pl.program_id(0) == 0)
        def _():
            dw_ref[...] = jnp.zeros_like(dw_ref)

        for cb in range(3 * DN_HEADS):
            cs = slice(cb * 128, (cb + 1) * 128)
            hs = slice((cb % DN_HEADS) * 128, (cb % DN_HEADS + 1) * 128)
            c = _conv_block(pad_ref, w_ref, cb, tm)
            s = _silu(c)
            if cb < 2 * DN_HEADS:
                dn = (dq_ref[:, hs] * scale) if cb < DN_HEADS else dk_ref[:, hs]
                r = lax.rsqrt(jnp.sum(s * s, axis=-1, keepdims=True) + EPS)
                ds = r * dn - s * (r * r * r) * jnp.sum(dn * s, axis=-1, keepdims=True)
            else:
                ds = dv_ref[:, hs]
            dc = ds * _dsilu(c)
            dc_ref[:, cs] = dc
            for j in range(CONV_K):
                dw_ref[j:j + 1, cs] += jnp.sum(dc * pad_ref[pl.ds(8 - (CONV_K - 1) + j, tm), cs], axis=0,
                                               keepdims=True)
        bd = bd_ref[...]
        dbg_v = dbg_ref[...]
        lane = lax.broadcasted_iota(jnp.int32, bd.shape, 1)
        sg = _sigmoid(bd)
        ea = jnp.exp(al_ref[...])
        z = bd + db_ref[...]
        sp = _softplus(z)
        is_b = lane < DN_HEADS
        is_g = jnp.logical_and(lane >= DN_HEADS, lane < 2 * DN_HEADS)
        d_z = jnp.where(is_g, dbg_v * (-ea) * _sigmoid(z), 0.0)
        dbd_ref[...] = jnp.where(is_b, dbg_v * sg * (1.0 - sg), d_z)
        d_al = jnp.sum(jnp.where(is_g, dbg_v * (-ea) * sp, 0.0), axis=0, keepdims=True)
        d_db = jnp.sum(d_z, axis=0, keepdims=True)
        _accumulate(dab_ref, jnp.concatenate([d_al, d_db] + [jnp.zeros_like(d_al)] * 6, axis=0))

    w3 = 3 * D_MODEL
    return _call(body, name, (_sds((t, w3)), _sds((CONV_K, w3)), _sds((t, 128)), _sds((8, 128))), (t // tm,),
                 [pl.BlockSpec((tm, w3), lambda i: (i, 0)), _prev_spec(tm, w3), _acc_spec(CONV_K, w3),
                  pl.BlockSpec((tm, 128), lambda i: (i, 8192 // 128)), _acc_spec(1, 128), _acc_spec(1, 128),
                  _row_spec(tm, D_MODEL), _row_spec(tm, D_MODEL), _row_spec(tm, D_MODEL), _row_spec(tm, 128)],
                 (_row_spec(tm, w3), _acc_spec(CONV_K, w3), _row_spec(tm, 128), _acc_spec(8, 128)),
                 scratch=[pltpu.VMEM((tm + 8, w3), f32)])(proj, proj, conv_w, proj, a_log, dt_bias, dq, dk, dv, dbg)


def conv_transpose(name, dc, conv_w):
    t, w3 = dc.shape
    tm = _rows(t)
    nt = t // tm

    def body(cur_ref, nxt_ref, w_ref, o_ref, pad_ref):
        keep = (pl.program_id(0) < nt - 1).astype(f32)
        pad_ref[0:tm, :] = cur_ref[...]
        pad_ref[tm:tm + 8, :] = nxt_ref[...] * keep
        for cb in range(w3 // 128):
            cs = slice(cb * 128, (cb + 1) * 128)
            acc = pad_ref[pl.ds(CONV_K - 1, tm), cs] * w_ref[0:1, cs]
            for j in range(1, CONV_K):
                acc = acc + pad_ref[pl.ds(CONV_K - 1 - j, tm), cs] * w_ref[j:j + 1, cs]
            o_ref[:, cs] = acc

    return _call(body, name, _sds((t, w3)), (nt,),
                 [_row_spec(tm, w3), _next_spec(tm, w3, t), _acc_spec(CONV_K, w3)], _row_spec(tm, w3),
                 scratch=[pltpu.VMEM((tm + 8, w3), f32)])(dc, dc, conv_w)


def _bdg(a, b, ca, cb, prec=None):
    if prec is None:
        a, b = a.astype(bf16), b.astype(bf16)
    return lax.dot_general(a, b, (((ca,), (cb,)), ((0,), (0,))), precision=prec, preferred_element_type=f32)


def _bnn(a, b, prec=None):
    return _bdg(a, b, 2, 1, prec)


def _bnt(a, b, prec=None):
    return _bdg(a, b, 2, 2, prec)


def _btn(a, b, prec=None):
    return _bdg(a, b, 1, 1, prec)


def _delta_local(q, k, v, g_row, b_row, solved=None):
    c = CHUNK
    ii = lax.broadcasted_iota(jnp.int32, (c, c), 0)
    jj = lax.broadcasted_iota(jnp.int32, (c, c), 1)
    eye, lower, strict = ii == jj, ii >= jj, ii > jj
    shp = (q.shape[0], c, c)
    g_b = jnp.broadcast_to(g_row, shp)
    gc_col = jnp.sum(jnp.where(lower, g_b, 0.0), axis=2, keepdims=True)
    gc_row = jnp.sum(jnp.where(eye, jnp.broadcast_to(gc_col, shp), 0.0), axis=1, keepdims=True)
    b_col = jnp.sum(jnp.where(eye, jnp.broadcast_to(b_row, shp), 0.0), axis=2, keepdims=True)
    gl = jnp.sum(g_row, axis=2, keepdims=True)
    decay = jnp.exp(jnp.where(lower, gc_col - gc_row, -1e30))
    kb = k * b_col
    kk = _bnt(kb, k)
    lmat = jnp.where(strict, kk * decay, 0.0)
    egc = jnp.exp(gc_col)
    rhs_w = kb * egc
    if solved is None:
        tinv = eye.astype(f32) - lmat
        pw = lmat
        for _ in range(5):
            pw = _bnn(pw, pw, HIGH)
            tinv = tinv + _bnn(tinv, pw, HIGH)
        u = _bnn(tinv, v * b_col, HIGH)
        w = _bnn(tinv, rhs_w, HIGH)
    else:
        tinv, u, w = solved
    qk = _bnt(q, k)
    amat = jnp.where(lower, qk * decay, 0.0)
    q_dec = q * egc
    kdf = jnp.exp(gl - gc_col)
    k_dec = k * kdf
    return dict(eye=eye, lower=lower, strict=strict, gc_col=gc_col, b_col=b_col, gl=gl, decay=decay,
                kb=kb, kk=kk, tinv=tinv, egc=egc, rhs_w=rhs_w, u=u, w=w, qk=qk, amat=amat, q_dec=q_dec, kdf=kdf,
                k_dec=k_dec)


def _heads(ref, hb):
    return jnp.stack([ref[:, h * HEAD_DIM:(h + 1) * HEAD_DIM] for h in range(hb)])


def _put_heads(ref, val):
    for h in range(val.shape[0]):
        ref[:, h * HEAD_DIM:(h + 1) * HEAD_DIM] = val[h]


def delta_fwd(name, q, k, v, g_rows, b_rows):
    t = q.shape[0]
    nc = t // CHUNK
    hb = DELTA_HB

    def body(q_ref, k_ref, v_ref, g_ref, b_ref, o_ref, s_ref, t_ref, u_ref, w_ref, state):
        n = pl.program_id(1)

        @pl.when(n == 0)
        def _():
            state[...] = jnp.zeros_like(state)

        loc = _delta_local(_heads(q_ref, hb), _heads(k_ref, hb), _heads(v_ref, hb), g_ref[:, pl.ds(n, 1), :],
                           b_ref[:, pl.ds(n, 1), :])
        s0 = state[...]
        s_ref[...] = s0
        t_ref[...] = loc["tinv"]
        _put_heads(u_ref, loc["u"])
        _put_heads(w_ref, loc["w"])
        v_new = loc["u"] - _bnn(loc["w"], s0)
        _put_heads(o_ref, _bnn(loc["q_dec"], s0) + _bnn(loc["amat"], v_new))
        state[...] = s0 * jnp.exp(loc["gl"]) + _btn(loc["k_dec"], v_new)

    tok = pl.BlockSpec((CHUNK, hb * HEAD_DIM), lambda h, n: (n, h))
    row = pl.BlockSpec((hb, nc, CHUNK), lambda h, n: (h, 0, 0))
    return _call(body, name, (_sds((t, D_MODEL)), _sds((DN_HEADS, nc, HEAD_DIM, HEAD_DIM)),
                              _sds((DN_HEADS, nc, CHUNK, CHUNK)), _sds((t, D_MODEL)), _sds((t, D_MODEL))),
                 (DN_HEADS // hb, nc), [tok, tok, tok, row, row],
                 (tok, pl.BlockSpec((hb, None, HEAD_DIM, HEAD_DIM), lambda h, n: (h, n, 0, 0)),
                  pl.BlockSpec((hb, None, CHUNK, CHUNK), lambda h, n: (h, n, 0, 0)), tok, tok),
                 scratch=[pltpu.VMEM((hb, HEAD_DIM, HEAD_DIM), f32)])(q, k, v, g_rows, b_rows)


def delta_bwd(name, q, k, v, g_rows, b_rows, s_all, t_all, u_all, w_all, do):
    t = q.shape[0]
    nc = t // CHUNK
    c = CHUNK
    hb = DELTA_HB

    def body(q_ref, k_ref, v_ref, g_ref, b_ref, s_ref, t_ref, u_ref, w_ref, do_ref, dq_ref, dk_ref, dv_ref, dg_ref,
             db_ref, dstate):
        step = pl.program_id(1)
        n = nc - 1 - step

        @pl.when(step == 0)
        def _():
            dstate[...] = jnp.zeros_like(dstate)

        qv, kv, vv = _heads(q_ref, hb), _heads(k_ref, hb), _heads(v_ref, hb)
        L = _delta_local(qv, kv, vv, g_ref[:, pl.ds(n, 1), :], b_ref[:, pl.ds(n, 1), :],
                         solved=(t_ref[...], _heads(u_ref, hb), _heads(w_ref, hb)))
        eye, lower, strict = L["eye"], L["lower"], L["strict"]
        shp = (hb, c, c)
        s0 = s_ref[...]
        dov = _heads(do_ref, hb)
        ds = dstate[...]
        eg = jnp.exp(L["gl"])
        v_new = L["u"] - _bnn(L["w"], s0)
        d_k_dec = _bnt(v_new, ds)
        d_v_new = _bnn(L["k_dec"], ds) + _btn(L["amat"], dov)
        d_eg = jnp.sum(jnp.sum(ds * s0, axis=2, keepdims=True), axis=1, keepdims=True)
        d_q_dec = _bnt(dov, s0)
        d_a = _bnt(dov, v_new)
        d_w = -_bnt(d_v_new, s0)
        dstate[...] = ds * eg + _btn(L["q_dec"], dov) - _btn(L["w"], d_v_new)
        d_am = jnp.where(lower, d_a * L["decay"], 0.0)
        dq = _bnn(d_am, kv) + d_q_dec * L["egc"]
        dk = _btn(d_am, qv) + d_k_dec * L["kdf"]
        e_col = jnp.sum(d_k_dec * L["k_dec"], axis=2, keepdims=True)
        d_gc_col = jnp.sum(d_q_dec * L["q_dec"], axis=2, keepdims=True) - e_col
        d_gl = jnp.sum(e_col, axis=1, keepdims=True) + d_eg * eg
        tinv = L["tinv"]
        d_rhs_u = _btn(tinv, d_v_new, HIGH)
        d_rhs_w = _btn(tinv, d_w, HIGH)
        d_l = -(_bnt(d_rhs_u, L["u"], HIGH) + _bnt(d_rhs_w, L["w"], HIGH))
        _put_heads(dv_ref, d_rhs_u * L["b_col"])
        d_b_col = jnp.sum(d_rhs_u * vv, axis=2, keepdims=True)
        d_gc_col = d_gc_col + jnp.sum(d_rhs_w * L["rhs_w"], axis=2, keepdims=True)
        d_lm = jnp.where(strict, d_l * L["decay"], 0.0)
        d_kb = d_rhs_w * L["egc"] + _bnn(d_lm, kv)
        dk = dk + _btn(d_lm, L["kb"]) + d_kb * L["b_col"]
        d_b_col = d_b_col + jnp.sum(d_kb * kv, axis=2, keepdims=True)
        m = d_am * L["qk"] + d_lm * L["kk"]
        d_gc_col = d_gc_col + jnp.sum(m, axis=2, keepdims=True)
        d_gc_row = (jnp.sum(jnp.where(eye, jnp.broadcast_to(d_gc_col, shp), 0.0), axis=1, keepdims=True)
                    - jnp.sum(m, axis=1, keepdims=True))
        lane = lax.broadcasted_iota(jnp.int32, (1, 1, c), 2)
        d_gc_row = d_gc_row + jnp.where(lane == c - 1, d_gl, 0.0)
        d_gc_tot = jnp.sum(jnp.where(eye, jnp.broadcast_to(d_gc_row, shp), 0.0), axis=2, keepdims=True)
        dg_ref[:, pl.ds(n, 1), :] = jnp.sum(jnp.where(lower, jnp.broadcast_to(d_gc_tot, shp), 0.0), axis=1,
                                            keepdims=True)
        db_ref[:, pl.ds(n, 1), :] = jnp.sum(jnp.where(eye, jnp.broadcast_to(d_b_col, shp), 0.0), axis=1,
                                            keepdims=True)
        _put_heads(dq_ref, dq)
        _put_heads(dk_ref, dk)

    tok = pl.BlockSpec((CHUNK, hb * HEAD_DIM), lambda h, s: (nc - 1 - s, h))
    row = pl.BlockSpec((hb, nc, CHUNK), lambda h, s: (h, 0, 0))
    return _call(body, name, (_sds((t, D_MODEL)),) * 3 + (_sds((DN_HEADS, nc, CHUNK)),) * 2, (DN_HEADS // hb, nc),
                 [tok, tok, tok, row, row,
                  pl.BlockSpec((hb, None, HEAD_DIM, HEAD_DIM), lambda h, s: (h, nc - 1 - s, 0, 0)),
                  pl.BlockSpec((hb, None, CHUNK, CHUNK), lambda h, s: (h, nc - 1 - s, 0, 0)), tok, tok, tok],
                 (tok, tok, tok, row, row),
                 scratch=[pltpu.VMEM((hb, HEAD_DIM, HEAD_DIM), f32)])(q, k, v, g_rows, b_rows, s_all, t_all, u_all, w_all,
                                                                      do)


SEG = 8


def _perm_rows(a):
    t, c = a.shape
    return a.reshape(SEG, t // SEG, c).transpose(1, 0, 2).reshape(t, c)


def _unperm_rows(a):
    t, c = a.shape
    return a.reshape(t // SEG, SEG, c).transpose(1, 0, 2).reshape(t, c)


def _cmul(ar, ai, br, bi):
    return ar * br - ai * bi, ar * bi + ai * br


def _segment_init(er, ei, lr, li, seg_len, reverse):
    w = er.shape[1]
    sub = lax.broadcasted_iota(jnp.int32, (SEG, w), 0)

    def shift(x, k):
        if reverse:
            return jnp.where(sub < SEG - k, pltpu.roll(x, SEG - k, 0), 0.0)
        return jnp.where(sub >= k, pltpu.roll(x, k, 0), 0.0)

    pr, pi = lr, li
    for _ in range(seg_len.bit_length() - 1):
        pr, pi = _cmul(pr, pi, pr, pi)
    fr, fi = shift(er, 1), shift(ei, 1)
    for k in (1, 2, 4):
        sr, si = shift(fr, k), shift(fi, k)
        mr, mi = _cmul(pr, pi, sr, si)
        fr, fi = fr + mr, fi + mi
        pr, pi = _cmul(pr, pi, pr, pi)
    return fr, fi


def s5_fwd(name, u_perm, bbd, cbd, lam):
    t = u_perm.shape[0]
    tt = _tile(t, (1024, 512, 256, 128))
    nt, ng, w = t // tt, tt // SEG, SG_STATE
    seg_len = t // SEG
    assert seg_len & (seg_len - 1) == 0 and tt % SEG == 0

    def body(u_ref, b_ref, c_ref, lam_ref, y_ref, h_ref, x_scr, state):
        p, i = pl.program_id(1), pl.program_id(2)
        lr1, li1 = lam_ref[:, 0:w], lam_ref[:, w:2 * w]
        lr, li = jnp.broadcast_to(lr1, (SEG, w)), jnp.broadcast_to(li1, (SEG, w))
        x_scr[...] = _bdot(u_ref[...], b_ref[...], ((1,), (0,)))

        @pl.when(jnp.logical_and(p == 0, i == 0))
        def _():
            state[...] = jnp.zeros_like(state)

        @pl.when(jnp.logical_and(p == 1, i == 0))
        def _():
            sr, si = _segment_init(state[:, 0:w], state[:, w:2 * w], lr1, li1, seg_len, False)
            state[:, 0:w] = sr
            state[:, w:2 * w] = si

        def run(store):
            def step(g, st):
                row = pl.multiple_of(g * SEG, SEG)
                xg = x_scr[pl.ds(row, SEG), :]
                nr = lr * st[0] - li * st[1] + xg[:, 0:w]
                ni = lr * st[1] + li * st[0] + xg[:, w:2 * w]
                if store:
                    h_ref[pl.ds(row, SEG), 0:w] = nr
                    h_ref[pl.ds(row, SEG), w:2 * w] = ni
                return nr, ni

            fin = lax.fori_loop(0, ng, step, (state[:, 0:w], state[:, w:2 * w]))
            state[:, 0:w] = fin[0]
            state[:, w:2 * w] = fin[1]

        @pl.when(p == 0)
        def _():
            run(False)

        @pl.when(p == 1)
        def _():
            run(True)
            y_ref[...] = _bdot(h_ref[...], c_ref[...], ((1,), (0,)))

    return _call(body, name, (_sds((t, D_MODEL)), _sds((t, N_SUPER * 2 * w))), (N_SUPER, 2, nt),
                 [pl.BlockSpec((tt, 128), lambda s, p, i: (i, s)),
                  pl.BlockSpec((None, 128, 2 * w), lambda s, p, i: (s, 0, 0)),
                  pl.BlockSpec((None, 2 * w, 128), lambda s, p, i: (s, 0, 0)),
                  pl.BlockSpec((None, 1, 2 * w), lambda s, p, i: (s, 0, 0))],
                 (pl.BlockSpec((tt, 128), lambda s, p, i: (i * p, s)),
                  pl.BlockSpec((tt, 2 * w), lambda s, p, i: (i * p, s))),
                 scratch=[pltpu.VMEM((tt, 2 * w), f32), pltpu.VMEM((SEG, 2 * w), f32)])(u_perm, bbd, cbd, lam)


def s5_bwd(name, dy_perm, u_perm, h_perm, hprev0, bbd, cbd, lam):
    t = u_perm.shape[0]
    tt = _tile(t, (1024, 512, 256, 128))
    nt, ng, w = t // tt, tt // SEG, SG_STATE
    seg_len = t // SEG

    def body(dy_ref, u_ref, h_ref, hp_ref, hp0_ref, b_ref, c_ref, lam_ref, du_ref, db_ref, dc_ref, dl_ref,
             g_scr, state, dl_acc):
        p, i = pl.program_id(1), pl.program_id(2)
        first_tile = jnp.logical_or(p == 0, i == nt - 1)
        lr1, li1 = lam_ref[:, 0:w], -lam_ref[:, w:2 * w]
        lr, li = jnp.broadcast_to(lr1, (SEG, w)), jnp.broadcast_to(li1, (SEG, w))
        g_scr[...] = _bdot(dy_ref[...], c_ref[...], ((1,), (1,)))

        @pl.when(jnp.logical_and(p == 0, i == 0))
        def _():
            state[...] = jnp.zeros_like(state)

        @pl.when(jnp.logical_and(p == 1, i == 0))
        def _():
            sr, si = _segment_init(state[:, 0:w], state[:, w:2 * w], lr1, li1, seg_len, True)
            state[:, 0:w] = sr
            state[:, w:2 * w] = si
            dl_acc[...] = jnp.zeros_like(dl_acc)

        def adj(g, st):
            row = pl.multiple_of(g * SEG, SEG)
            gg = g_scr[pl.ds(row, SEG), :]
            nr = lr * st[0] - li * st[1] + gg[:, 0:w]
            ni = lr * st[1] + li * st[0] + gg[:, w:2 * w]
            return row, nr, ni

        @pl.when(p == 0)
        def _():
            def step(k, st):
                _, nr, ni = adj(ng - 1 - k, st)
                return nr, ni

            fin = lax.fori_loop(0, ng, step, (state[:, 0:w], state[:, w:2 * w]))
            state[:, 0:w] = fin[0]
            state[:, w:2 * w] = fin[1]

        @pl.when(p == 1)
        def _():
            above = jnp.where(first_tile, hp0_ref[...], hp_ref[...])

            def step(k, st):
                g = ng - 1 - k
                row, nr, ni = adj(g, st)
                g_scr[pl.ds(row, SEG), 0:w] = nr
                g_scr[pl.ds(row, SEG), w:2 * w] = ni
                prow = pl.multiple_of(jnp.maximum(g - 1, 0) * SEG, SEG)
                hp = jnp.where(g > 0, h_ref[pl.ds(prow, SEG), :], above)
                pr, pi = hp[:, 0:w], hp[:, w:2 * w]
                return nr, ni, st[2] + nr * pr + ni * pi, st[3] + ni * pr - nr * pi

            fin = lax.fori_loop(0, ng, step, (state[:, 0:w], state[:, w:2 * w], dl_acc[:, 0:w], dl_acc[:, w:2 * w]))
            state[:, 0:w] = fin[0]
            state[:, w:2 * w] = fin[1]
            dl_acc[:, 0:w] = fin[2]
            dl_acc[:, w:2 * w] = fin[3]
            a = g_scr[...]
            du_ref[...] = _bdot(a, b_ref[...], ((1,), (1,)))
            d_b = _bdot(u_ref[...], a, ((0,), (0,)))
            d_c = _bdot(h_ref[...], dy_ref[...], ((0,), (0,)))

            @pl.when(i == 0)
            def _():
                db_ref[...] = d_b
                dc_ref[...] = d_c

            @pl.when(i > 0)
            def _():
                db_ref[...] += d_b
                dc_ref[...] += d_c

            @pl.when(i == nt - 1)
            def _():
                dl_ref[...] = jnp.sum(dl_acc[...], axis=0, keepdims=True)

    tile = lambda s, p, i: (nt - 1 - i, s)
    tile1 = lambda s, p, i: (nt - 1 - i * p, s)
    above = lambda s, p, i: (jnp.maximum((nt - 1 - i * p) * (tt // SEG) - 1, 0), s)
    per_s = lambda s, p, i: (s, 0, 0)
    return _call(body, name, (_sds((t, D_MODEL)), _sds((N_SUPER, 128, 2 * w)), _sds((N_SUPER, 2 * w, 128)),
                              _sds((N_SUPER, 1, 2 * w))), (N_SUPER, 2, nt),
                 [pl.BlockSpec((tt, 128), tile), pl.BlockSpec((tt, 128), tile1), pl.BlockSpec((tt, 2 * w), tile1),
                  pl.BlockSpec((SEG, 2 * w), above), pl.BlockSpec((SEG, 2 * w), lambda s, p, i: (0, s)),
                  pl.BlockSpec((None, 128, 2 * w), per_s), pl.BlockSpec((None, 2 * w, 128), per_s),
                  pl.BlockSpec((None, 1, 2 * w), per_s)],
                 (pl.BlockSpec((tt, 128), tile1), pl.BlockSpec((None, 128, 2 * w), per_s),
                  pl.BlockSpec((None, 2 * w, 128), per_s), pl.BlockSpec((None, 1, 2 * w), per_s)),
                 scratch=[pltpu.VMEM((tt, 2 * w), f32), pltpu.VMEM((SEG, 2 * w), f32),
                          pltpu.VMEM((SEG, 2 * w), f32)])(dy_perm, u_perm, h_perm, h_perm, hprev0, bbd, cbd, lam)


def _s5_disc(a_re, a_im, log_dt, br, bi):
    dt = jnp.exp(log_dt)
    mag = jnp.exp(a_re * dt)
    lr, li = mag * jnp.cos(a_im * dt), mag * jnp.sin(a_im * dt)
    den = a_re * a_re + a_im * a_im
    fr = ((lr - 1.0) * a_re + li * a_im) / den
    fi = (li * a_re - (lr - 1.0) * a_im) / den
    return lr, li, fr * br - fi * bi, fr * bi + fi * br


def s5_disc_fwd(name, a_re, a_im, log_dt, br, bi):
    g, n = SSM_GROUPS, SSM_STATE

    def body(ar, ai, ld, brr, bir, lr, li, bbr, bbi):
        o = _s5_disc(ar[...], ai[...], ld[...], brr[...], bir[...])
        lr[...], li[...], bbr[...], bbi[...] = o

    return _call(body, name, (_sds((g, 1, n)), _sds((g, 1, n)), _sds((g, SSM_GROUP, n)), _sds((g, SSM_GROUP, n))))(
        a_re, a_im, log_dt, br, bi)


def s5_disc_bwd(name, a_re, a_im, log_dt, br, bi, d_lr, d_li, d_bbr, d_bbi):
    g, n = SSM_GROUPS, SSM_STATE

    def body(ar, ai, ld, brr, bir, c1, c2, c3, c4, o1, o2, o3, o4, o5):
        _, vjp = jax.vjp(_s5_disc, ar[...], ai[...], ld[...], brr[...], bir[...])
        o1[...], o2[...], o3[...], o4[...], o5[...] = vjp((c1[...], c2[...], c3[...], c4[...]))

    return _call(body, name, (_sds((g, 1, n)), _sds((g, 1, n)), _sds((g, 1, 1)), _sds((g, SSM_GROUP, n)),
                              _sds((g, SSM_GROUP, n))))(a_re, a_im, log_dt, br, bi, d_lr, d_li, d_bbr, d_bbi)


def gelu_fwd(name, ys_lin, proj, d_skip):
    t, d = ys_lin.shape
    tm = _rows(t)

    def body(y_ref, u_ref, d_ref, o_ref):
        o_ref[...] = _gelu(y_ref[...] + d_ref[...] * u_ref[...])

    return _call(body, name, _sds((t, d)), (t // tm,),
                 [_row_spec(tm, d), pl.BlockSpec((tm, d), lambda i: (i, 4)), _acc_spec(1, d)],
                 _row_spec(tm, d))(ys_lin, proj, d_skip)


def _head_norm(o, hn):
    outs, ns, rs = [], [], []
    for h in range(DN_HEADS):
        oh = o[:, h * 128:(h + 1) * 128]
        r = lax.rsqrt(jnp.mean(oh * oh, axis=-1, keepdims=True) + EPS)
        n = oh * r
        ns.append(n)
        rs.append(r)
        outs.append(n * hn)
    return outs, ns, rs


def merge_fwd(name, proj, o, yg, glu_lin, head_norm, b_glu):
    t = o.shape[0]
    tm = _rows(t)
    d = D_MODEL

    def body(za_ref, zb_ref, ra_ref, rb_ref, o_ref, yg_ref, gl_ref, hn_ref, bg_ref, m_ref):
        on, _, _ = _head_norm(o_ref[...], hn_ref[...])
        y_a = jnp.concatenate(on, axis=1) * _silu(za_ref[...])
        y_b = yg_ref[...] * _sigmoid(gl_ref[...] + bg_ref[...]) * _silu(zb_ref[...])
        m_ref[...] = (_sigmoid(ra_ref[...]) * y_a + _sigmoid(rb_ref[...]) * y_b).astype(bf16)

    pc = lambda c: pl.BlockSpec((tm, d), lambda i: (i, c))
    return _call(body, name, _sds((t, d), bf16), (t // tm,),
                 [pc(3), pc(5), pc(6), pc(7), _row_spec(tm, d), _row_spec(tm, d), _row_spec(tm, d),
                  _acc_spec(1, 128), _acc_spec(1, d)], _row_spec(tm, d))(
        proj, proj, proj, proj, o, yg, glu_lin, head_norm, b_glu)


def merge_bwd(name, proj, o, yg, glu_lin, head_norm, b_glu, dm):
    t = o.shape[0]
    tm = _rows(t)
    d = D_MODEL

    def body(za_ref, zb_ref, ra_ref, rb_ref, o_ref, yg_ref, gl_ref, hn_ref, bg_ref, dm_ref,
             dza_ref, dzb_ref, dra_ref, drb_ref, do_ref, dgl_ref, dyg_ref, dhn_ref, dbg_ref):
        hn = hn_ref[...]
        za, zb, ra, rb = za_ref[...], zb_ref[...], ra_ref[...], rb_ref[...]
        on, ns, rs = _head_norm(o_ref[...], hn)
        onc = jnp.concatenate(on, axis=1)
        sza = _silu(za)
        y_a = onc * sza
        yg = yg_ref[...]
        sgl = _sigmoid(gl_ref[...] + bg_ref[...])
        y2 = yg * sgl
        szb = _silu(zb)
        y_b = y2 * szb
        sra, srb = _sigmoid(ra), _sigmoid(rb)
        dmv = dm_ref[...]
        dra_ref[...] = dmv * y_a * sra * (1.0 - sra)
        drb_ref[...] = dmv * y_b * srb * (1.0 - srb)
        d_ya = dmv * sra
        d_yb = dmv * srb
        dza_ref[...] = d_ya * onc * _dsilu(za)
        dzb_ref[...] = d_yb * y2 * _dsilu(zb)
        d_on = d_ya * sza
        d_y2 = d_yb * szb
        dyg_ref[...] = d_y2 * sgl
        d_gl = d_y2 * yg * sgl * (1.0 - sgl)
        dgl_ref[...] = d_gl.astype(bf16)
        _accumulate(dbg_ref, jnp.sum(d_gl, axis=0, keepdims=True))
        d_hn = jnp.zeros((1, 128), f32)
        for h in range(DN_HEADS):
            hs = slice(h * 128, (h + 1) * 128)
            dh = d_on[:, hs]
            d_hn = d_hn + jnp.sum(dh * ns[h], axis=0, keepdims=True)
            dn = dh * hn
            do_ref[:, hs] = rs[h] * (dn - ns[h] * jnp.mean(dn * ns[h], axis=-1, keepdims=True))
        _accumulate(dhn_ref, d_hn)

    pc = lambda c: pl.BlockSpec((tm, d), lambda i: (i, c))
    rs_ = _row_spec(tm, d)
    return _call(body, name, (_sds((t, d)),) * 5 + (_sds((t, d), bf16), _sds((t, d)), _sds((1, 128)), _sds((1, d))),
                 (t // tm,), [pc(3), pc(5), pc(6), pc(7), rs_, rs_, rs_, _acc_spec(1, 128), _acc_spec(1, d), rs_],
                 (rs_,) * 7 + (_acc_spec(1, 128), _acc_spec(1, d)))(
        proj, proj, proj, proj, o, yg, glu_lin, head_norm, b_glu, dm)


def gelu_bwd(name, ys_lin, proj, d_skip, dyg_a, dyg_b):
    t, d = ys_lin.shape
    tm = _rows(t)

    def body(y_ref, u_ref, d_ref, a_ref, b_ref, dys_ref, du_ref, dd_ref):
        uv = u_ref[...]
        dys = (a_ref[...] + b_ref[...]) * _dgelu(y_ref[...] + d_ref[...] * uv)
        dys_ref[...] = dys
        du_ref[...] = dys * d_ref[...]
        _accumulate(dd_ref, jnp.sum(dys * uv, axis=0, keepdims=True))

    rs_ = _row_spec(tm, d)
    return _call(body, name, (_sds((t, d)), _sds((t, d)), _sds((1, d))), (t // tm,),
                 [rs_, pl.BlockSpec((tm, d), lambda i: (i, 4)), _acc_spec(1, d), rs_, rs_],
                 (rs_, rs_, _acc_spec(1, d)))(ys_lin, proj, d_skip, dyg_a, dyg_b)


def assemble_dproj(name, dqkv, dza, du_a, du_b, dzb, dra, drb, dbd):
    t = dza.shape[0]
    tm = _rows(t)
    d = D_MODEL

    def body(qkv_ref, za_ref, ua_ref, ub_ref, zb_ref, ra_ref, rb_ref, bd_ref, o_ref):
        o_ref[:, 0:3 * d] = qkv_ref[...].astype(bf16)
        o_ref[:, 3 * d:4 * d] = za_ref[...].astype(bf16)
        o_ref[:, 4 * d:5 * d] = (ua_ref[...] + ub_ref[...]).astype(bf16)
        o_ref[:, 5 * d:6 * d] = zb_ref[...].astype(bf16)
        o_ref[:, 6 * d:7 * d] = ra_ref[...].astype(bf16)
        o_ref[:, 7 * d:8 * d] = rb_ref[...].astype(bf16)
        o_ref[:, 8 * d:8 * d + 128] = bd_ref[...].astype(bf16)
        o_ref[:, 8 * d + 128:W_PAD] = jnp.zeros((tm, W_PAD - 8 * d - 128), bf16)

    rs_ = _row_spec(tm, d)
    return _call(body, name, _sds((t, W_PAD), bf16), (t // tm,),
                 [_row_spec(tm, 3 * d), rs_, rs_, rs_, rs_, rs_, rs_, _row_spec(tm, 128)], _row_spec(tm, W_PAD))(
        dqkv, dza, du_a, du_b, dzb, dra, drb, dbd)


def adamw(name, w, g, m, v):
    lead, (r, c) = w.shape[:-2], w.shape[-2:]
    tm = _tile(r, (512, 256, 128, 64, 32, 16, 8))
    c1 = 1.0 / (1.0 - ADAM_B1 ** ADAM_STEP)
    c2 = 1.0 / (1.0 - ADAM_B2 ** ADAM_STEP)

    def body(w_ref, g_ref, m_ref, v_ref, d_ref, nm_ref, nv_ref):
        gv = g_ref[...]
        nm = ADAM_B1 * m_ref[...] + (1.0 - ADAM_B1) * gv
        nv = ADAM_B2 * v_ref[...] + (1.0 - ADAM_B2) * (gv * gv)
        d_ref[...] = -ADAM_LR * ((nm * c1) / (jnp.sqrt(nv * c2) + ADAM_EPS) + ADAM_WD * w_ref[...])
        nm_ref[...] = nm
        nv_ref[...] = nv

    if lead:
        sp = pl.BlockSpec((None, tm, c), lambda l, i: (l, i, 0))
        grid = (lead[0], r // tm)
    else:
        sp = pl.BlockSpec((tm, c), lambda i: (i, 0))
        grid = (r // tm,)
    return _call(body, name, (_sds(w.shape),) * 3, grid, [sp] * 4, (sp,) * 3)(w, g, m, v)


def _coords():
    return lax.axis_index("x"), lax.axis_index("y"), lax.axis_index("c")


def _lin(dev):
    return 4 * dev[0] + 2 * dev[1] + dev[2]


def _chips(me):
    x, y, _ = me
    return [(1 - x, y), (x, 1 - y), (1 - x, 1 - y)]


def all_gather(name, shard):
    r, c = shard.shape

    def body(x_ref, o_ref, send_sems, recv_sems, local_sem):
        me = _coords()
        x, y, cc = me
        sibling = (x, y, 1 - cc)
        chips = _chips(me)

        def copy(k, block, to, src=None):
            return pltpu.make_async_remote_copy(
                src_ref=o_ref.at[_lin(block)] if src is None else src, dst_ref=o_ref.at[_lin(block)],
                send_sem=send_sems.at[k], recv_sem=recv_sems.at[k], device_id=to, device_id_type=MESH)

        mine = pltpu.make_async_copy(x_ref, o_ref.at[_lin(me)], local_sem)
        mine.start()
        first = [copy(0, me, sibling, src=x_ref)]
        first += [copy(1 + j, me, (*chip, cc), src=x_ref) for j, chip in enumerate(chips)]
        for cp in first:
            cp.start()
        passed = [copy(4 + j, (*chip, cc), sibling) for j, chip in enumerate(chips)]
        for j, chip in enumerate(chips):
            copy(1 + j, (*chip, cc), me).wait_recv()
            passed[j].start()
        copy(0, sibling, me).wait_recv()
        for j, chip in enumerate(chips):
            copy(4 + j, (*chip, 1 - cc), me).wait_recv()
        for cp in first + passed:
            cp.wait_send()
        mine.wait()

    any_spec = pl.BlockSpec(memory_space=pl.ANY)
    return _call(body, name, _sds((N_DEV, r, c), shard.dtype), in_specs=[any_spec], out_specs=any_spec,
                 scratch=[pltpu.SemaphoreType.DMA((N_DEV - 1,)), pltpu.SemaphoreType.DMA((N_DEV - 1,)),
                          pltpu.SemaphoreType.DMA(())])(shard)


def pair_exchange(name, blocks):
    _, _, r, c = blocks.shape

    def body(x_ref, o_ref, send_sems, recv_sems):
        x, y, cc = _coords()
        sibling = (x, y, 1 - cc)
        cps = [pltpu.make_async_remote_copy(src_ref=x_ref.at[ch, 1 - cc], dst_ref=o_ref.at[ch], send_sem=send_sems.at[ch],
                                            recv_sem=recv_sems.at[ch], device_id=sibling, device_id_type=MESH)
               for ch in range(4)]
        for cp in cps:
            cp.start()
        for cp in cps:
            cp.wait()

    any_spec = pl.BlockSpec(memory_space=pl.ANY)
    return _call(body, name, _sds((4, r, c), blocks.dtype), in_specs=[any_spec], out_specs=any_spec,
                 scratch=[pltpu.SemaphoreType.DMA((4,)), pltpu.SemaphoreType.DMA((4,))])(blocks)


def chip_exchange(name, blocks):
    _, r, c = blocks.shape

    def body(x_ref, o_ref, send_sems, recv_sems, local_sem):
        me = _coords()
        x, y, cc = me
        my_chip = 2 * x + y
        mine = pltpu.make_async_copy(x_ref.at[my_chip], o_ref.at[my_chip], local_sem)
        mine.start()
        cps = []
        for j, (px, py) in enumerate(_chips(me)):
            cp = pltpu.make_async_remote_copy(src_ref=x_ref.at[2 * px + py], dst_ref=o_ref.at[my_chip],
                                              send_sem=send_sems.at[j], recv_sem=recv_sems.at[j],
                                              device_id=(px, py, cc), device_id_type=MESH)
            cp.start()
            cps.append(cp)
        for j, (px, py) in enumerate(_chips(me)):
            pltpu.make_async_remote_copy(src_ref=x_ref.at[my_chip], dst_ref=o_ref.at[2 * px + py],
                                         send_sem=send_sems.at[j], recv_sem=recv_sems.at[j],
                                         device_id=(px, py, cc), device_id_type=MESH).wait_recv()
        for cp in cps:
            cp.wait_send()
        mine.wait()

    any_spec = pl.BlockSpec(memory_space=pl.ANY)
    return _call(body, name, _sds(blocks.shape, blocks.dtype), in_specs=[any_spec], out_specs=any_spec,
                 scratch=[pltpu.SemaphoreType.DMA((3,)), pltpu.SemaphoreType.DMA((3,)),
                          pltpu.SemaphoreType.DMA(())])(blocks)


def pair_sum(name, mine, theirs):
    _, r, c = mine.shape
    tm = _tile(r, (240, 256, 128, 64, 32, 16))

    def body(a_ref, b_ref, o_ref):
        o_ref[...] = (a_ref[...].astype(f32) + b_ref[...].astype(f32)).astype(o_ref.dtype)

    sp = pl.BlockSpec((None, tm, c), lambda ch, i: (ch, i, 0))
    return _call(body, name, _sds(mine.shape, mine.dtype), (4, r // tm), [sp, sp], sp)(mine, theirs)


def sum_slots(name, slots):
    n, r, c = slots.shape
    tm = _tile(r, (240, 256, 128, 64, 32, 16))

    def body(s_ref, o_ref):
        acc = s_ref[0].astype(f32)
        for d in range(1, n):
            acc = acc + s_ref[d].astype(f32)
        o_ref[...] = acc

    return _call(body, name, _sds((r, c)), (r // tm,), [pl.BlockSpec((n, tm, c), lambda i: (0, i, 0))],
                 pl.BlockSpec((tm, c), lambda i: (i, 0)))(slots)


def _rows_layout(col8, t):
    return col8.T.reshape(DN_HEADS, t // CHUNK, CHUNK)


def _blockdiag(m):
    g, a, b = m.shape
    m = m.reshape(N_SUPER, SUPER, a, b)
    out = jnp.einsum("sgab,gh->sgahb", m, jnp.eye(SUPER, dtype=m.dtype))
    return out.reshape(N_SUPER, SUPER * a, SUPER * b)


def _diag_blocks(m, a, b):
    m = m.reshape(N_SUPER, SUPER, a, SUPER, b)
    return jnp.einsum("sgahb,gh->sgab", m, jnp.eye(SUPER, dtype=m.dtype)).reshape(SSM_GROUPS, a, b)


def _s5_params(p, li):
    tag = f"l{li}"
    n = SSM_STATE
    a_re = p["ssm_a_re"].reshape(SSM_GROUPS, 1, n)
    a_im = p["ssm_a_im"].reshape(SSM_GROUPS, 1, n)
    log_dt = p["ssm_log_dt"].reshape(SSM_GROUPS, 1, 1)
    br = jnp.swapaxes(p["ssm_b_re"], 1, 2)
    bi = jnp.swapaxes(p["ssm_b_im"], 1, 2)
    lr, li_, bbr, bbi = s5_disc_fwd("s5_disc_" + tag, a_re, a_im, log_dt, br, bi)
    lam = jnp.concatenate([lr.reshape(N_SUPER, 1, SG_STATE), li_.reshape(N_SUPER, 1, SG_STATE)], axis=-1)
    bbd = jnp.concatenate([_blockdiag(bbr), _blockdiag(bbi)], axis=-1).astype(bf16)
    c_re = jnp.swapaxes(p["ssm_c_re"], 1, 2)
    c_im = jnp.swapaxes(p["ssm_c_im"], 1, 2)
    cbd = jnp.concatenate([_blockdiag(c_re), -_blockdiag(c_im)], axis=1).astype(bf16)
    return dict(a_re=a_re, a_im=a_im, log_dt=log_dt, br=br, bi=bi, lam=lam, bbd=bbd, cbd=cbd)


def layer_fwd(x, p, li):
    tag = f"l{li}"
    t = x.shape[0]
    d = D_MODEL
    h = rmsnorm_fwd("norm_pre_" + tag, x, p["norm_pre"])
    proj = mm_nn("proj_" + tag, h, p["w_all"])
    q, k, v, bg = conv_qkv_fwd("conv_" + tag, proj, p["conv_w"], p["a_log"], p["dt_bias"])
    b_rows = _rows_layout(bg[:, 0:DN_HEADS], t)
    g_rows = _rows_layout(bg[:, DN_HEADS:2 * DN_HEADS], t)
    o, s_all, t_all, u_all, w_sol = delta_fwd("delta_" + tag, q, k, v, g_rows, b_rows)
    sp = _s5_params(p, li)
    u_perm = _perm_rows(proj[:, 4 * d:5 * d])
    ys_perm, hs = s5_fwd("s5_" + tag, u_perm, sp["bbd"], sp["cbd"], sp["lam"])
    ys_lin = _unperm_rows(ys_perm)
    yg = gelu_fwd("gelu_" + tag, ys_lin, proj, p["ssm_d"])
    glu_lin = mm_nn("glu_" + tag, yg, p["w_glu"])
    merged = merge_fwd("merge_" + tag, proj, o, yg, glu_lin, p["head_norm"], p["b_glu"])
    out = mm_nn("out_" + tag, merged, p["w_out"])
    y = residual_norm_fwd("norm_post_" + tag, x, out, p["norm_post"])
    saved = dict(x=x, h=h, proj=proj, q=q, k=k, v=v, g_rows=g_rows, b_rows=b_rows, o=o, s_all=s_all, t_all=t_all, u_all=u_all, w_sol=w_sol, sp=sp, u_perm=u_perm,
                 hs=hs, ys_lin=ys_lin, yg=yg, glu_lin=glu_lin, merged=merged, out=out)
    return y, saved


def layer_bwd(dy, p, s, li):
    tag = f"l{li}"
    t = dy.shape[0]
    d = D_MODEL
    sp = s["sp"]
    gr = {}
    d_out, gr["norm_post"] = post_norm_bwd("norm_post_b_" + tag, s["out"], p["norm_post"], dy)
    d_merged = mm_nt("out_b_" + tag, d_out, p["w_out"])
    gr["w_out"] = mm_tn("out_w_" + tag, s["merged"], d_out, bf16)
    (dza, dzb, dra, drb, d_o, d_glu, dyg_a, gr["head_norm"], gr["b_glu"]) = merge_bwd(
        "merge_b_" + tag, s["proj"], s["o"], s["yg"], s["glu_lin"], p["head_norm"], p["b_glu"], d_merged)
    dyg_b = mm_nt("glu_b_" + tag, d_glu, p["w_glu"])
    gr["w_glu"] = mm_tn("glu_w_" + tag, s["yg"], d_glu, bf16)
    d_ys, du_a, gr["ssm_d"] = gelu_bwd("gelu_b_" + tag, s["ys_lin"], s["proj"], p["ssm_d"], dyg_a, dyg_b)
    hprev0 = jnp.concatenate([jnp.zeros((1, s["hs"].shape[1]), f32), s["hs"][-SEG:-1]], axis=0)
    du_perm, d_bbd, d_cbd, d_lam = s5_bwd("s5_b_" + tag, _perm_rows(d_ys), s["u_perm"], s["hs"], hprev0, sp["bbd"],
                                          sp["cbd"], sp["lam"])
    du_b = _unperm_rows(du_perm)
    d_c_re = _diag_blocks(d_cbd[:, 0:SG_STATE, :], SSM_STATE, SSM_GROUP)
    d_c_im = -_diag_blocks(d_cbd[:, SG_STATE:, :], SSM_STATE, SSM_GROUP)
    gr["ssm_c_re"] = jnp.swapaxes(d_c_re, 1, 2)
    gr["ssm_c_im"] = jnp.swapaxes(d_c_im, 1, 2)
    d_bbr = _diag_blocks(d_bbd[:, :, 0:SG_STATE], SSM_GROUP, SSM_STATE)
    d_bbi = _diag_blocks(d_bbd[:, :, SG_STATE:], SSM_GROUP, SSM_STATE)
    d_lr = d_lam[:, :, 0:SG_STATE].reshape(SSM_GROUPS, 1, SSM_STATE)
    d_li = d_lam[:, :, SG_STATE:].reshape(SSM_GROUPS, 1, SSM_STATE)
    d_are, d_aim, d_ldt, d_br, d_bi = s5_disc_bwd("s5_disc_b_" + tag, sp["a_re"], sp["a_im"], sp["log_dt"], sp["br"],
                                                  sp["bi"], d_lr, d_li, d_bbr, d_bbi)
    gr["ssm_a_re"] = d_are.reshape(SSM_GROUPS, SSM_STATE)
    gr["ssm_a_im"] = d_aim.reshape(SSM_GROUPS, SSM_STATE)
    gr["ssm_log_dt"] = d_ldt.reshape(SSM_GROUPS)
    gr["ssm_b_re"] = jnp.swapaxes(d_br, 1, 2)
    gr["ssm_b_im"] = jnp.swapaxes(d_bi, 1, 2)
    dq, dk, dv, dg_rows, db_rows = delta_bwd("delta_b_" + tag, s["q"], s["k"], s["v"], s["g_rows"], s["b_rows"],
                                             s["s_all"], s["t_all"], s["u_all"], s["w_sol"], d_o)
    dbg = jnp.concatenate([db_rows.reshape(DN_HEADS, t).T, dg_rows.reshape(DN_HEADS, t).T,
                           jnp.zeros((t, 128 - 2 * DN_HEADS), f32)], axis=1)
    dc, gr["conv_w"], dbd, dab = conv_qkv_bwd("conv_b_" + tag, s["proj"], p["conv_w"], p["a_log"], p["dt_bias"],
                                              dq, dk, dv, dbg)
    gr["a_log"] = dab[0, DN_HEADS:2 * DN_HEADS]
    gr["dt_bias"] = dab[1, DN_HEADS:2 * DN_HEADS]
    dqkv = conv_transpose("conv_t_" + tag, dc, p["conv_w"])
    dproj = assemble_dproj("dproj_" + tag, dqkv, dza, du_a, du_b, dzb, dra, drb, dbd)
    d_h = mm_nt("proj_b_" + tag, dproj, p["w_all"])
    gr["w_all"] = mm_tn("proj_w_" + tag, s["h"], dproj, bf16)
    dx, gr["norm_pre"] = rmsnorm_bwd("norm_pre_b_" + tag, s["x"], p["norm_pre"], d_h, dy)
    return dx, gr


REPL = ["norm_pre", "a_log", "dt_bias", "head_norm", "ssm_a_re", "ssm_a_im", "ssm_log_dt", "ssm_b_re", "ssm_b_im",
        "ssm_c_re", "ssm_c_im", "ssm_d", "b_glu", "norm_post"]
SHARDED = ["w_in", "conv_w", "w_glu", "w_out"]
ALL_W = ["norm_pre", "w_in", "conv_w", "a_log", "dt_bias", "head_norm", "ssm_a_re", "ssm_a_im", "ssm_log_dt",
         "ssm_b_re", "ssm_b_im", "ssm_c_re", "ssm_c_im", "ssm_d", "w_glu", "b_glu", "w_out", "norm_post"]
PACK_W = 1024


def _pack_flat(arrs, rows):
    flat = jnp.concatenate([a.reshape(-1) for a in arrs])
    return jnp.pad(flat, (0, rows * PACK_W - flat.shape[0])).reshape(rows, PACK_W)


def _flat_rows(arrs, mult=8):
    n = sum(math.prod(a.shape) for a in arrs)
    rows = -(-n // PACK_W)
    return -(-rows // mult) * mult


def _unpack(flat, shapes):
    out, off = [], 0
    for sh in shapes:
        n = math.prod(sh)
        out.append(flat[off:off + n].reshape(sh))
        off += n
    return out


def _repl_split(shapes):
    big = [n for n in REPL if math.prod(shapes[n]) % PACK_W == 0]
    small = [n for n in REPL if n not in big]
    return big, small


def _rows8(n):
    return -(-n // (8 * PACK_W)) * 8


def _repl_rows(shapes):
    big, small = _repl_split(shapes)
    rows = sum(_rows8(math.prod(shapes[n])) for n in big) + _flat_rows([_sds(shapes[n]) for n in small], 8)
    return -(-rows // (8 * N_DEV)) * (8 * N_DEV)


def _repl_pack(arrs, shapes):
    big, small = _repl_split(shapes)
    parts = []
    for n in big:
        a = arrs[n].reshape(-1, PACK_W)
        parts.append(jnp.pad(a, ((0, _rows8(a.size) - a.shape[0]), (0, 0))))
    parts.append(_pack_flat([arrs[n] for n in small], _flat_rows([_sds(shapes[n]) for n in small], 8)))
    used = sum(p.shape[0] for p in parts)
    parts.append(jnp.zeros((_repl_rows(shapes) - used, PACK_W), parts[0].dtype))
    return jnp.concatenate(parts, axis=0)


def _repl_unpack(packed, shapes):
    big, small = _repl_split(shapes)
    out, off = {}, 0
    for n in big:
        size = math.prod(shapes[n])
        out[n] = packed[off:off + size // PACK_W].reshape(shapes[n])
        off += _rows8(size)
    srows = _flat_rows([_sds(shapes[n]) for n in small], 8)
    out.update(zip(small, _unpack(packed[off:off + srows].reshape(-1), [shapes[n] for n in small])))
    return out


_COL_RUNS = ((0, 4096), (4112, W_COLS), (4096, 4112))


def _w_all(main, tails, wc):
    pieces = []
    for lo, hi in _COL_RUNS:
        for dv in range(N_DEV):
            a, b = max(lo, dv * wc) - dv * wc, min(hi, (dv + 1) * wc) - dv * wc
            if a < min(b, PACK_W):
                pieces.append(main[dv][:, :, a:min(b, PACK_W)])
            if b > max(a, PACK_W):
                pieces.append(tails[dv][:, :, max(a, PACK_W) - PACK_W:b - PACK_W])
    ll, rows = main[0].shape[0], main[0].shape[1]
    pieces.append(jnp.zeros((ll, rows, W_PAD - W_COLS), main[0].dtype))
    return jnp.concatenate(pieces, axis=2)


def _ref_cols(g, lo, hi):
    pieces, off = [], 0
    for a, b in _COL_RUNS:
        s, e = max(lo, a), min(hi, b)
        if s < e:
            pieces.append((s, g[..., off + s - a:off + e - a]))
        off += b - a
    pieces.sort(key=lambda t: t[0])
    return jnp.concatenate([p for _, p in pieces], axis=-1) if len(pieces) > 1 else pieces[0][1]


def kernel(x, norm_pre, w_in, conv_w, a_log, dt_bias, head_norm, ssm_a_re, ssm_a_im, ssm_log_dt, ssm_b_re, ssm_b_im, ssm_c_re, ssm_c_im, ssm_d, w_glu, b_glu, w_out, norm_post, loss_target, m_norm_pre, m_w_in, m_conv_w, m_a_log, m_dt_bias, m_head_norm, m_ssm_a_re, m_ssm_a_im, m_ssm_log_dt, m_ssm_b_re, m_ssm_b_im, m_ssm_c_re, m_ssm_c_im, m_ssm_d, m_w_glu, m_b_glu, m_w_out, m_norm_post, v_norm_pre, v_w_in, v_conv_w, v_a_log, v_dt_bias, v_head_norm, v_ssm_a_re, v_ssm_a_im, v_ssm_log_dt, v_ssm_b_re, v_ssm_b_im, v_ssm_c_re, v_ssm_c_im, v_ssm_d, v_w_glu, v_b_glu, v_w_out, v_norm_post):
    loc = dict(locals())
    w = {n: loc[n] for n in ALL_W}
    m = {n: loc["m_" + n] for n in ALL_W}
    v = {n: loc["v_" + n] for n in ALL_W}
    depth = w_in.shape[0]
    wc = w_in.shape[2]
    cc = conv_w.shape[2]
    wr = w_glu.shape[1]

    tail = wc - PACK_W
    conv_hi = conv_w.astype(bf16)
    conv_mid = (conv_w - conv_hi.astype(f32)).astype(bf16)
    conv_lo = (conv_w - conv_hi.astype(f32) - conv_mid.astype(f32)).astype(bf16)
    w_in_b = w_in.astype(bf16)
    misc_shapes = [(depth, D_MODEL, tail), (3, depth, CONV_K, cc)]
    misc_rows = _flat_rows([_sds(sh, bf16) for sh in misc_shapes], 16)
    n_main, n_sq = depth * D_MODEL, depth * wr
    shard = jnp.concatenate([w_in_b[:, :, :PACK_W].reshape(n_main, PACK_W), w_glu.astype(bf16).reshape(n_sq, PACK_W),
                             w_out.astype(bf16).reshape(n_sq, PACK_W),
                             _pack_flat([w_in_b[:, :, PACK_W:], jnp.stack([conv_hi, conv_mid, conv_lo])], misc_rows)])
    gathered = all_gather("gather_weights", shard)
    miscs = [_unpack(gathered[dv, n_main + 2 * n_sq:].reshape(-1), misc_shapes) for dv in range(N_DEV)]
    w_all_full = _w_all([gathered[dv, :n_main].reshape(depth, D_MODEL, PACK_W) for dv in range(N_DEV)],
                        [mi[0] for mi in miscs], wc)
    w_glu_full = gathered[:, n_main:n_main + n_sq].reshape(N_DEV, depth, wr, D_MODEL)
    w_glu_full = jnp.swapaxes(w_glu_full, 0, 1).reshape(depth, N_DEV * wr, D_MODEL)
    w_out_full = gathered[:, n_main + n_sq:n_main + 2 * n_sq].reshape(N_DEV, depth, wr, D_MODEL)
    w_out_full = jnp.swapaxes(w_out_full, 0, 1).reshape(depth, N_DEV * wr, D_MODEL)
    conv_full = jnp.concatenate([mi[1][0].astype(f32) + mi[1][1].astype(f32) + mi[1][2].astype(f32)
                                 for mi in miscs], axis=2)

    def layer_params(li):
        return dict(norm_pre=norm_pre[li].reshape(1, -1), w_all=w_all_full[li], conv_w=conv_full[li],
                    a_log=jnp.pad(a_log[li], (DN_HEADS, 128 - 2 * DN_HEADS)).reshape(1, 128),
                    dt_bias=jnp.pad(dt_bias[li], (DN_HEADS, 128 - 2 * DN_HEADS)).reshape(1, 128),
                    head_norm=head_norm[li].reshape(1, -1), ssm_a_re=ssm_a_re[li], ssm_a_im=ssm_a_im[li],
                    ssm_log_dt=ssm_log_dt[li], ssm_b_re=ssm_b_re[li], ssm_b_im=ssm_b_im[li], ssm_c_re=ssm_c_re[li],
                    ssm_c_im=ssm_c_im[li], ssm_d=ssm_d[li].reshape(1, -1), w_glu=w_glu_full[li],
                    b_glu=b_glu[li].reshape(1, -1), w_out=w_out_full[li], norm_post=norm_post[li].reshape(1, -1))

    act = x[0]
    saved, params = [], []
    for li in range(depth):
        params.append(layer_params(li))
        act, sv = layer_fwd(act, params[li], li)
        saved.append(sv)
    loss_part, dy = loss_head("loss_head", act, loss_target[0])
    grads = [None] * depth
    for li in reversed(range(depth)):
        dy, grads[li] = layer_bwd(dy, params[li], saved[li], li)
    grad_x = dy[None]
    loss = lax.psum(loss_part[0, 0], ("x", "y", "c"))

    def stack(name):
        return jnp.stack([grads[li][name] for li in range(depth)])

    g_w_all = stack("w_all")
    g_conv = stack("conv_w").astype(bf16)
    g_glu = stack("w_glu").reshape(depth, N_DEV, wr, D_MODEL)
    g_out = stack("w_out").reshape(depth, N_DEV, wr, D_MODEL)
    repl_shapes = {n: w[n].shape for n in REPL}
    repl_rows = _repl_rows(repl_shapes)
    g_repl = _repl_pack({n: stack(n).reshape(w[n].shape) for n in REPL}, repl_shapes).astype(bf16)
    rr = repl_rows // N_DEV
    gmisc_shapes = [(depth, D_MODEL, tail), (depth, CONV_K, cc)]
    gmisc_rows = _flat_rows([_sds(sh, bf16) for sh in gmisc_shapes], 16)
    used = n_main + 2 * n_sq + gmisc_rows + rr
    blocks = jnp.stack([jnp.concatenate([
        _ref_cols(g_w_all, dv * wc, dv * wc + PACK_W).reshape(n_main, PACK_W), g_glu[:, dv].reshape(n_sq, PACK_W),
        g_out[:, dv].reshape(n_sq, PACK_W),
        _pack_flat([_ref_cols(g_w_all, dv * wc + PACK_W, (dv + 1) * wc), g_conv[:, :, dv * cc:(dv + 1) * cc]],
                   gmisc_rows),
        g_repl[dv * rr:(dv + 1) * rr], jnp.zeros((-used % 16, PACK_W), bf16)]) for dv in range(N_DEV)])
    blocks = blocks.reshape(4, 2, blocks.shape[1], PACK_W)
    from_sibling = pair_exchange("pair_grads", blocks)
    own = lax.dynamic_index_in_dim(blocks, lax.axis_index("c"), axis=1, keepdims=False)
    chip_part = pair_sum("pair_sum_grads", own, from_sibling)
    slots = chip_exchange("scatter_grads", chip_part)
    mine = sum_slots("sum_grads", slots)
    o1, o2, o3, o4 = n_main, n_main + n_sq, n_main + 2 * n_sq, n_main + 2 * n_sq + gmisc_rows
    gs_tail, gs_conv = _unpack(mine[o3:o4].reshape(-1), gmisc_shapes)
    gs_w_in = jnp.concatenate([mine[:o1].reshape(depth, D_MODEL, PACK_W), gs_tail], axis=2)
    gs_glu = mine[o1:o2].reshape(depth, wr, D_MODEL)
    gs_out = mine[o2:o3].reshape(depth, wr, D_MODEL)
    g_repl_full = all_gather("gather_repl_grads", mine[o4:o4 + rr]).reshape(repl_rows, PACK_W)
    g = _repl_unpack(g_repl_full, repl_shapes)
    g.update(w_in=gs_w_in, w_glu=gs_glu, w_out=gs_out, conv_w=gs_conv)

    delta, new_m, new_v = {}, {}, {}
    for n in SHARDED:
        delta[n], new_m[n], new_v[n] = adamw("adamw_" + n, w[n], g[n], m[n], v[n])
    outs = adamw("adamw_repl", *[_repl_pack({n: src[n] for n in REPL}, repl_shapes) for src in (w, g, m, v)])
    for dst, o in zip((delta, new_m, new_v), outs):
        dst.update(_repl_unpack(o, repl_shapes))
    return (loss, grad_x, *[g[n] for n in ALL_W], *[delta[n] for n in ALL_W], *[new_m[n] for n in ALL_W],
            *[new_v[n] for n in ALL_W])
```

```python
import math

import jax
import jax.numpy as jnp
from jax import lax
from jax.experimental import pallas as pl
from jax.experimental.pallas import tpu as pltpu

f32 = jnp.float32
bf16 = jnp.bfloat16

D_MODEL = 1024
DEPTH = 4
N_DEV = 8
DN_HEADS = 8
HEAD_DIM = 128
CHUNK = 64
CONV_K = 4
SSM_GROUPS = 64
SSM_GROUP = 16
SSM_STATE = 64
SUPER = 8
N_SUPER = SSM_GROUPS // SUPER
SG_STATE = SUPER * SSM_STATE
EPS = 1e-6
W_COLS = 8208
W_PAD = 8448
ADAM_LR, ADAM_B1, ADAM_B2, ADAM_EPS, ADAM_WD, ADAM_STEP = 0.001, 0.9, 0.999, 1e-08, 0.01, 10
VMEM_LIMIT = 56 * 1024 * 1024
MESH = pl.DeviceIdType.MESH
HIGH = lax.Precision.HIGH
DELTA_HB = 8


def _call(body, name, out_shape, grid=None, in_specs=None, out_specs=None, scratch=(), **kw):
    args = dict(out_shape=out_shape, name=name, scratch_shapes=list(scratch),
                compiler_params=pltpu.CompilerParams(vmem_limit_bytes=VMEM_LIMIT, **kw))
    if grid is not None:
        args.update(grid=grid, in_specs=in_specs, out_specs=out_specs)
    else:
        if in_specs is not None:
            args.update(in_specs=in_specs)
        if out_specs is not None:
            args.update(out_specs=out_specs)
    return pl.pallas_call(body, **args)


def _sds(shape, dtype=f32):
    return jax.ShapeDtypeStruct(tuple(shape), dtype)


def _sigmoid(x):
    return 1.0 / (1.0 + jnp.exp(-x))


def _silu(x):
    return x * _sigmoid(x)


def _dsilu(x):
    s = _sigmoid(x)
    return s * (1.0 + x * (1.0 - s))


_GELU_C = math.sqrt(2.0 / math.pi)


def _gelu(x):
    return 0.5 * x * (1.0 + jnp.tanh(_GELU_C * (x + 0.044715 * x * x * x)))


def _dgelu(x):
    t = jnp.tanh(_GELU_C * (x + 0.044715 * x * x * x))
    return 0.5 * (1.0 + t) + 0.5 * x * (1.0 - t * t) * _GELU_C * (1.0 + 3 * 0.044715 * x * x)


def _softplus(x):
    return jnp.maximum(x, 0.0) + jnp.log(1.0 + jnp.exp(-jnp.abs(x)))


def _bdot(a, b, dn):
    return lax.dot_general(a.astype(bf16), b.astype(bf16), (dn, ((), ())), preferred_element_type=f32)


def _matmul(name, a, b, *, dn, grid, a_spec, b_spec, o_spec, o_shape, o_dtype=f32):
    nk = grid[-1]

    def body(a_ref, b_ref, o_ref, acc_ref):
        p = _bdot(a_ref[...], b_ref[...], dn)
        if nk == 1:
            o_ref[...] = p.astype(o_dtype)
        else:
            k = pl.program_id(len(grid) - 1)

            @pl.when(k == 0)
            def _():
                acc_ref[...] = p

            @pl.when(k > 0)
            def _():
                acc_ref[...] += p

            @pl.when(k == nk - 1)
            def _():
                o_ref[...] = acc_ref[...].astype(o_dtype)

    blk = tuple(d for d in o_spec.block_shape if d is not None)
    return _call(body, name, _sds(o_shape, o_dtype), grid, [a_spec, b_spec], o_spec,
                 scratch=[pltpu.VMEM(blk if nk > 1 else (8, 128), f32)])(a, b)


def _tile(n, pref):
    for t in pref:
        if n % t == 0:
            return t
    return n


def mm_nn(name, a, b):
    m, k = a.shape
    n = b.shape[1]
    tm, tn, tk = _tile(m, (512, 256)), _tile(n, (2816, 1024, 512)), _tile(k, (2816, 1024))
    return _matmul(name, a, b, dn=((1,), (0,)), grid=(m // tm, n // tn, k // tk),
                   a_spec=pl.BlockSpec((tm, tk), lambda i, j, l: (i, l)),
                   b_spec=pl.BlockSpec((tk, tn), lambda i, j, l: (l, j)),
                   o_spec=pl.BlockSpec((tm, tn), lambda i, j, l: (i, j)), o_shape=(m, n))


def mm_nt(name, a, b):
    m, k = a.shape
    n = b.shape[0]
    tm, tn, tk = _tile(m, (512, 256)), _tile(n, (1024, 512)), _tile(k, (2816, 1024))
    return _matmul(name, a, b, dn=((1,), (1,)), grid=(m // tm, n // tn, k // tk),
                   a_spec=pl.BlockSpec((tm, tk), lambda i, j, l: (i, l)),
                   b_spec=pl.BlockSpec((tn, tk), lambda i, j, l: (j, l)),
                   o_spec=pl.BlockSpec((tm, tn), lambda i, j, l: (i, j)), o_shape=(m, n))


def mm_tn(name, a, b, o_dtype=f32):
    k, m = a.shape
    n = b.shape[1]
    tm, tn, tk = _tile(m, (512,)), _tile(n, (2816, 1024, 512)), _tile(k, (512, 256))
    return _matmul(name, a, b, dn=((0,), (0,)), grid=(m // tm, n // tn, k // tk),
                   a_spec=pl.BlockSpec((tk, tm), lambda i, j, l: (l, i)),
                   b_spec=pl.BlockSpec((tk, tn), lambda i, j, l: (l, j)),
                   o_spec=pl.BlockSpec((tm, tn), lambda i, j, l: (i, j)), o_shape=(m, n), o_dtype=o_dtype)


def _rows(t):
    return _tile(t, (256,))


def _row_spec(tm, w):
    return pl.BlockSpec((tm, w), lambda i: (i, 0))


def _acc_spec(r, w):
    return pl.BlockSpec((r, w), lambda i: (0, 0))


def _accumulate(ref, val):
    @pl.when(pl.program_id(0) == 0)
    def _():
        ref[...] = val

    @pl.when(pl.program_id(0) > 0)
    def _():
        ref[...] += val


def rmsnorm_fwd(name, x, gain):
    t, d = x.shape
    tm = _rows(t)

    def body(x_ref, g_ref, o_ref):
        xv = x_ref[...]
        r = lax.rsqrt(jnp.mean(xv * xv, axis=-1, keepdims=True) + EPS)
        o_ref[...] = (xv * r * g_ref[...]).astype(bf16)

    return _call(body, name, _sds((t, d), bf16), (t // tm,), [_row_spec(tm, d), _acc_spec(1, d)],
                 _row_spec(tm, d))(x, gain)


def rmsnorm_bwd(name, x, gain, dn, dres):
    t, d = x.shape
    tm = _rows(t)

    def body(x_ref, g_ref, dn_ref, dr_ref, dx_ref, dg_ref):
        xv = x_ref[...]
        r = lax.rsqrt(jnp.mean(xv * xv, axis=-1, keepdims=True) + EPS)
        n = xv * r
        dnv = dn_ref[...]
        _accumulate(dg_ref, jnp.sum(dnv * n, axis=0, keepdims=True))
        dng = dnv * g_ref[...]
        dx_ref[...] = dr_ref[...] + r * (dng - n * jnp.mean(dng * n, axis=-1, keepdims=True))

    return _call(body, name, (_sds((t, d)), _sds((1, d))), (t // tm,),
                 [_row_spec(tm, d), _acc_spec(1, d), _row_spec(tm, d), _row_spec(tm, d)],
                 (_row_spec(tm, d), _acc_spec(1, d)))(x, gain, dn, dres)


def residual_norm_fwd(name, x, out, gain):
    t, d = x.shape
    tm = _rows(t)

    def body(x_ref, o_ref, g_ref, y_ref):
        ov = o_ref[...]
        r = lax.rsqrt(jnp.mean(ov * ov, axis=-1, keepdims=True) + EPS)
        y_ref[...] = x_ref[...] + ov * r * g_ref[...]

    return _call(body, name, _sds((t, d)), (t // tm,), [_row_spec(tm, d), _row_spec(tm, d), _acc_spec(1, d)],
                 _row_spec(tm, d))(x, out, gain)


def post_norm_bwd(name, out, gain, dy):
    t, d = out.shape
    tm = _rows(t)

    def body(o_ref, g_ref, dy_ref, do_ref, dg_ref):
        ov = o_ref[...]
        r = lax.rsqrt(jnp.mean(ov * ov, axis=-1, keepdims=True) + EPS)
        n = ov * r
        dyv = dy_ref[...]
        _accumulate(dg_ref, jnp.sum(dyv * n, axis=0, keepdims=True))
        dng = dyv * g_ref[...]
        do_ref[...] = (r * (dng - n * jnp.mean(dng * n, axis=-1, keepdims=True))).astype(bf16)

    return _call(body, name, (_sds((t, d), bf16), _sds((1, d))), (t // tm,),
                 [_row_spec(tm, d), _acc_spec(1, d), _row_spec(tm, d)],
                 (_row_spec(tm, d), _acc_spec(1, d)))(out, gain, dy)


def loss_head(name, y, target):
    t, d = y.shape
    tm = _rows(t)

    def body(y_ref, t_ref, l_ref, dy_ref):
        e = y_ref[...] - t_ref[...]
        dy_ref[...] = e * (1.0 / d)
        s = jnp.sum(jnp.sum(e * e, axis=1, keepdims=True), axis=0, keepdims=True) * (0.5 / d)
        _accumulate(l_ref, s)

    return _call(body, name, (_sds((1, 1)), _sds((t, d))), (t // tm,),
                 [_row_spec(tm, d), _row_spec(tm, d)], (_acc_spec(1, 1), _row_spec(tm, d)))(y, target)


def _prev_spec(tm, w):
    return pl.BlockSpec((8, w), lambda i: (jnp.maximum(i * (tm // 8) - 1, 0), 0))


def _next_spec(tm, w, t):
    return pl.BlockSpec((8, w), lambda i: (jnp.minimum((i + 1) * (tm // 8), t // 8 - 1), 0))


def _fill_pad(pad_ref, prev_ref, cur_ref, tm):
    keep = (pl.program_id(0) > 0).astype(f32)
    pad_ref[0:8, :] = prev_ref[...] * keep
    pad_ref[8:8 + tm, :] = cur_ref[...]


def _conv_block(pad_ref, w_ref, cb, tm):
    cs = slice(cb * 128, (cb + 1) * 128)
    acc = pad_ref[pl.ds(8 - (CONV_K - 1), tm), cs] * w_ref[0:1, cs]
    for j in range(1, CONV_K):
        acc = acc + pad_ref[pl.ds(8 - (CONV_K - 1) + j, tm), cs] * w_ref[j:j + 1, cs]
    return acc


def conv_qkv_fwd(name, proj, conv_w, a_log, dt_bias):
    t = proj.shape[0]
    tm = _rows(t)
    scale = HEAD_DIM ** -0.5

    def body(cur_ref, prev_ref, w_ref, bd_ref, al_ref, db_ref, q_ref, k_ref, v_ref, bg_ref, pad_ref):
        _fill_pad(pad_ref, prev_ref, cur_ref, tm)
        for cb in range(3 * DN_HEADS):
            s = _silu(_conv_block(pad_ref, w_ref, cb, tm))
            hs = slice((cb % DN_HEADS) * 128, (cb % DN_HEADS + 1) * 128)
            if cb < DN_HEADS:
                q_ref[:, hs] = s * (lax.rsqrt(jnp.sum(s * s, axis=-1, keepdims=True) + EPS) * scale)
            elif cb < 2 * DN_HEADS:
                k_ref[:, hs] = s * lax.rsqrt(jnp.sum(s * s, axis=-1, keepdims=True) + EPS)
            else:
                v_ref[:, hs] = s
        bd = bd_ref[...]
        beta = _sigmoid(bd)
        g = -jnp.exp(al_ref[...]) * _softplus(bd + db_ref[...])
        lane = lax.broadcasted_iota(jnp.int32, bd.shape, 1)
        bg_ref[...] = jnp.where(lane < DN_HEADS, beta, jnp.where(lane < 2 * DN_HEADS, g, 0.0))

    w3 = 3 * D_MODEL
    return _call(body, name, (_sds((t, D_MODEL)),) * 3 + (_sds((t, 128)),), (t // tm,),
                 [pl.BlockSpec((tm, w3), lambda i: (i, 0)), _prev_spec(tm, w3), _acc_spec(CONV_K, w3),
                  pl.BlockSpec((tm, 128), lambda i: (i, 8192 // 128)), _acc_spec(1, 128), _acc_spec(1, 128)],
                 (_row_spec(tm, D_MODEL),) * 3 + (_row_spec(tm, 128),),
                 scratch=[pltpu.VMEM((tm + 8, w3), f32)])(proj, proj, conv_w, proj, a_log, dt_bias)


def conv_qkv_bwd(name, proj, conv_w, a_log, dt_bias, dq, dk, dv, dbg):
    t = proj.shape[0]
    tm = _rows(t)
    scale = HEAD_DIM ** -0.5

    def body(cur_ref, prev_ref, w_ref, bd_ref, al_ref, db_ref, dq_ref, dk_ref, dv_ref, dbg_ref,
             dc_ref, dw_ref, dbd_ref, dab_ref, pad_ref):
        _fill_pad(pad_ref, prev_ref, cur_ref, tm)

        @pl.when(pl.program_id(0) == 0)
        def _():
            dw_ref[...] = jnp.zeros_like(dw_ref)

        for cb in range(3 * DN_HEADS):
            cs = slice(cb * 128, (cb + 1) * 128)
            hs = slice((cb % DN_HEADS) * 128, (cb % DN_HEADS + 1) * 128)
            c = _conv_block(pad_ref, w_ref, cb, tm)
            s = _silu(c)
            if cb < 2 * DN_HEADS:
                dn = (dq_ref[:, hs] * scale) if cb < DN_HEADS else dk_ref[:, hs]
                r = lax.rsqrt(jnp.sum(s * s, axis=-1, keepdims=True) + EPS)
                ds = r * dn - s * (r * r * r) * jnp.sum(dn * s, axis=-1, keepdims=True)
            else:
                ds = dv_ref[:, hs]
            dc = ds * _dsilu(c)
            dc_ref[:, cs] = dc
            for j in range(CONV_K):
                dw_ref[j:j + 1, cs] += jnp.sum(dc * pad_ref[pl.ds(8 - (CONV_K - 1) + j, tm), cs], axis=0,
                                               keepdims=True)
        bd = bd_ref[...]
        dbg_v = dbg_ref[...]
        lane = lax.broadcasted_iota(jnp.int32, bd.shape, 1)
        sg = _sigmoid(bd)
        ea = jnp.exp(al_ref[...])
        z = bd + db_ref[...]
        sp = _softplus(z)
        is_b = lane < DN_HEADS
        is_g = jnp.logical_and(lane >= DN_HEADS, lane < 2 * DN_HEADS)
        d_z = jnp.where(is_g, dbg_v * (-ea) * _sigmoid(z), 0.0)
        dbd_ref[...] = jnp.where(is_b, dbg_v * sg * (1.0 - sg), d_z)
        d_al = jnp.sum(jnp.where(is_g, dbg_v * (-ea) * sp, 0.0), axis=0, keepdims=True)
        d_db = jnp.sum(d_z, axis=0, keepdims=True)
        _accumulate(dab_ref, jnp.concatenate([d_al, d_db] + [jnp.zeros_like(d_al)] * 6, axis=0))

    w3 = 3 * D_MODEL
    return _call(body, name, (_sds((t, w3)), _sds((CONV_K, w3)), _sds((t, 128)), _sds((8, 128))), (t // tm,),
                 [pl.BlockSpec((tm, w3), lambda i: (i, 0)), _prev_spec(tm, w3), _acc_spec(CONV_K, w3),
                  pl.BlockSpec((tm, 128), lambda i: (i, 8192 // 128)), _acc_spec(1, 128), _acc_spec(1, 128),
                  _row_spec(tm, D_MODEL), _row_spec(tm, D_MODEL), _row_spec(tm, D_MODEL), _row_spec(tm, 128)],
                 (_row_spec(tm, w3), _acc_spec(CONV_K, w3), _row_spec(tm, 128), _acc_spec(8, 128)),
                 scratch=[pltpu.VMEM((tm + 8, w3), f32)])(proj, proj, conv_w, proj, a_log, dt_bias, dq, dk, dv, dbg)


def conv_transpose(name, dc, conv_w):
    t, w3 = dc.shape
    tm = _rows(t)
    nt = t // tm

    def body(cur_ref, nxt_ref, w_ref, o_ref, pad_ref):
        keep = (pl.program_id(0) < nt - 1).astype(f32)
        pad_ref[0:tm, :] = cur_ref[...]
        pad_ref[tm:tm + 8, :] = nxt_ref[...] * keep
        for cb in range(w3 // 128):
            cs = slice(cb * 128, (cb + 1) * 128)
            acc = pad_ref[pl.ds(CONV_K - 1, tm), cs] * w_ref[0:1, cs]
            for j in range(1, CONV_K):
                acc = acc + pad_ref[pl.ds(CONV_K - 1 - j, tm), cs] * w_ref[j:j + 1, cs]
            o_ref[:, cs] = acc

    return _call(body, name, _sds((t, w3)), (nt,),
                 [_row_spec(tm, w3), _next_spec(tm, w3, t), _acc_spec(CONV_K, w3)], _row_spec(tm, w3),
                 scratch=[pltpu.VMEM((tm + 8, w3), f32)])(dc, dc, conv_w)


def _bdg(a, b, ca, cb, prec=None):
    if prec is None:
        a, b = a.astype(bf16), b.astype(bf16)
    return lax.dot_general(a, b, (((ca,), (cb,)), ((0,), (0,))), precision=prec, preferred_element_type=f32)


def _bnn(a, b, prec=None):
    return _bdg(a, b, 2, 1, prec)


def _bnt(a, b, prec=None):
    return _bdg(a, b, 2, 2, prec)


def _btn(a, b, prec=None):
    return _bdg(a, b, 1, 1, prec)


def _delta_local(q, k, v, g_row, b_row, solved=None):
    c = CHUNK
    ii = lax.broadcasted_iota(jnp.int32, (c, c), 0)
    jj = lax.broadcasted_iota(jnp.int32, (c, c), 1)
    eye, lower, strict = ii == jj, ii >= jj, ii > jj
    shp = (q.shape[0], c, c)
    g_b = jnp.broadcast_to(g_row, shp)
    gc_col = jnp.sum(jnp.where(lower, g_b, 0.0), axis=2, keepdims=True)
    gc_row = jnp.sum(jnp.where(eye, jnp.broadcast_to(gc_col, shp), 0.0), axis=1, keepdims=True)
    b_col = jnp.sum(jnp.where(eye, jnp.broadcast_to(b_row, shp), 0.0), axis=2, keepdims=True)
    gl = jnp.sum(g_row, axis=2, keepdims=True)
    decay = jnp.exp(jnp.where(lower, gc_col - gc_row, -1e30))
    kb = k * b_col
    kk = _bnt(kb, k)
    lmat = jnp.where(strict, kk * decay, 0.0)
    egc = jnp.exp(gc_col)
    rhs_w = kb * egc
    if solved is None:
        tinv = eye.astype(f32) - lmat
        pw = lmat
        for _ in range(5):
            pw = _bnn(pw, pw, HIGH)
            tinv = tinv + _bnn(tinv, pw, HIGH)
        u = _bnn(tinv, v * b_col, HIGH)
        w = _bnn(tinv, rhs_w, HIGH)
    else:
        tinv, u, w = solved
    qk = _bnt(q, k)
    amat = jnp.where(lower, qk * decay, 0.0)
    q_dec = q * egc
    kdf = jnp.exp(gl - gc_col)
    k_dec = k * kdf
    return dict(eye=eye, lower=lower, strict=strict, gc_col=gc_col, b_col=b_col, gl=gl, decay=decay,
                kb=kb, kk=kk, tinv=tinv, egc=egc, rhs_w=rhs_w, u=u, w=w, qk=qk, amat=amat, q_dec=q_dec, kdf=kdf,
                k_dec=k_dec)


def _heads(ref, hb):
    return jnp.stack([ref[:, h * HEAD_DIM:(h + 1) * HEAD_DIM] for h in range(hb)])


def _put_heads(ref, val):
    for h in range(val.shape[0]):
        ref[:, h * HEAD_DIM:(h + 1) * HEAD_DIM] = val[h]


def delta_fwd(name, q, k, v, g_rows, b_rows, next_shard=None):
    t = q.shape[0]
    nc = t // CHUNK
    hb = DELTA_HB
    fused = next_shard is not None
    assert not fused or (hb == DN_HEADS and nc >= 2)

    def body(*refs):
        if fused:
            (q_ref, k_ref, v_ref, g_ref, b_ref, x_ref, o_ref, s_ref, t_ref, u_ref, w_ref, gath_ref, state, send_sems,
             recv_sems, local_sem) = refs
        else:
            q_ref, k_ref, v_ref, g_ref, b_ref, o_ref, s_ref, t_ref, u_ref, w_ref, state = refs
        n = pl.program_id(1)
        if fused:
            start, forward, finish = _gather_steps(x_ref, gath_ref, send_sems, recv_sems, local_sem)
            pl.when(n == 0)(start)

        @pl.when(n == 0)
        def _():
            state[...] = jnp.zeros_like(state)

        loc = _delta_local(_heads(q_ref, hb), _heads(k_ref, hb), _heads(v_ref, hb), g_ref[:, pl.ds(n, 1), :],
                           b_ref[:, pl.ds(n, 1), :])
        s0 = state[...]
        s_ref[...] = s0
        t_ref[...] = loc["tinv"]
        _put_heads(u_ref, loc["u"])
        _put_heads(w_ref, loc["w"])
        v_new = loc["u"] - _bnn(loc["w"], s0)
        _put_heads(o_ref, _bnn(loc["q_dec"], s0) + _bnn(loc["amat"], v_new))
        state[...] = s0 * jnp.exp(loc["gl"]) + _btn(loc["k_dec"], v_new)
        if fused:
            pl.when(n == nc // 2)(forward)
            pl.when(n == nc - 1)(finish)

    tok = pl.BlockSpec((CHUNK, hb * HEAD_DIM), lambda h, n: (n, h))
    row = pl.BlockSpec((hb, nc, CHUNK), lambda h, n: (h, 0, 0))
    any_spec = pl.BlockSpec(memory_space=pl.ANY)
    outs = (_sds((t, D_MODEL)), _sds((DN_HEADS, nc, HEAD_DIM, HEAD_DIM)), _sds((DN_HEADS, nc, CHUNK, CHUNK)),
            _sds((t, D_MODEL)), _sds((t, D_MODEL)))
    out_specs = (tok, pl.BlockSpec((hb, None, HEAD_DIM, HEAD_DIM), lambda h, n: (h, n, 0, 0)),
                 pl.BlockSpec((hb, None, CHUNK, CHUNK), lambda h, n: (h, n, 0, 0)), tok, tok)
    in_specs, args = [tok, tok, tok, row, row], (q, k, v, g_rows, b_rows)
    scratch = [pltpu.VMEM((hb, HEAD_DIM, HEAD_DIM), f32)]
    if fused:
        outs += (_sds((N_DEV,) + next_shard.shape, next_shard.dtype),)
        out_specs += (any_spec,)
        in_specs, args = in_specs + [any_spec], args + (next_shard,)
        scratch = scratch + list(_GATHER_SEMS)
    return _call(body, name, outs, (DN_HEADS // hb, nc), in_specs, out_specs, scratch=scratch)(*args)


def delta_bwd(name, q, k, v, g_rows, b_rows, s_all, t_all, u_all, w_all, do):
    t = q.shape[0]
    nc = t // CHUNK
    c = CHUNK
    hb = DELTA_HB

    def body(q_ref, k_ref, v_ref, g_ref, b_ref, s_ref, t_ref, u_ref, w_ref, do_ref, dq_ref, dk_ref, dv_ref, dg_ref,
             db_ref, dstate):
        step = pl.program_id(1)
        n = nc - 1 - step

        @pl.when(step == 0)
        def _():
            dstate[...] = jnp.zeros_like(dstate)

        qv, kv, vv = _heads(q_ref, hb), _heads(k_ref, hb), _heads(v_ref, hb)
        L = _delta_local(qv, kv, vv, g_ref[:, pl.ds(n, 1), :], b_ref[:, pl.ds(n, 1), :],
                         solved=(t_ref[...], _heads(u_ref, hb), _heads(w_ref, hb)))
        eye, lower, strict = L["eye"], L["lower"], L["strict"]
        shp = (hb, c, c)
        s0 = s_ref[...]
        dov = _heads(do_ref, hb)
        ds = dstate[...]
        eg = jnp.exp(L["gl"])
        v_new = L["u"] - _bnn(L["w"], s0)
        d_k_dec = _bnt(v_new, ds)
        d_v_new = _bnn(L["k_dec"], ds) + _btn(L["amat"], dov)
        d_eg = jnp.sum(jnp.sum(ds * s0, axis=2, keepdims=True), axis=1, keepdims=True)
        d_q_dec = _bnt(dov, s0)
        d_a = _bnt(dov, v_new)
        d_w = -_bnt(d_v_new, s0)
        dstate[...] = ds * eg + _btn(L["q_dec"], dov) - _btn(L["w"], d_v_new)
        d_am = jnp.where(lower, d_a * L["decay"], 0.0)
        dq = _bnn(d_am, kv) + d_q_dec * L["egc"]
        dk = _btn(d_am, qv) + d_k_dec * L["kdf"]
        e_col = jnp.sum(d_k_dec * L["k_dec"], axis=2, keepdims=True)
        d_gc_col = jnp.sum(d_q_dec * L["q_dec"], axis=2, keepdims=True) - e_col
        d_gl = jnp.sum(e_col, axis=1, keepdims=True) + d_eg * eg
        tinv = L["tinv"]
        d_rhs_u = _btn(tinv, d_v_new, HIGH)
        d_rhs_w = _btn(tinv, d_w, HIGH)
        d_l = -(_bnt(d_rhs_u, L["u"], HIGH) + _bnt(d_rhs_w, L["w"], HIGH))
        _put_heads(dv_ref, d_rhs_u * L["b_col"])
        d_b_col = jnp.sum(d_rhs_u * vv, axis=2, keepdims=True)
        d_gc_col = d_gc_col + jnp.sum(d_rhs_w * L["rhs_w"], axis=2, keepdims=True)
        d_lm = jnp.where(strict, d_l * L["decay"], 0.0)
        d_kb = d_rhs_w * L["egc"] + _bnn(d_lm, kv)
        dk = dk + _btn(d_lm, L["kb"]) + d_kb * L["b_col"]
        d_b_col = d_b_col + jnp.sum(d_kb * kv, axis=2, keepdims=True)
        m = d_am * L["qk"] + d_lm * L["kk"]
        d_gc_col = d_gc_col + jnp.sum(m, axis=2, keepdims=True)
        d_gc_row = (jnp.sum(jnp.where(eye, jnp.broadcast_to(d_gc_col, shp), 0.0), axis=1, keepdims=True)
                    - jnp.sum(m, axis=1, keepdims=True))
        lane = lax.broadcasted_iota(jnp.int32, (1, 1, c), 2)
        d_gc_row = d_gc_row + jnp.where(lane == c - 1, d_gl, 0.0)
        d_gc_tot = jnp.sum(jnp.where(eye, jnp.broadcast_to(d_gc_row, shp), 0.0), axis=2, keepdims=True)
        dg_ref[:, pl.ds(n, 1), :] = jnp.sum(jnp.where(lower, jnp.broadcast_to(d_gc_tot, shp), 0.0), axis=1,
                                            keepdims=True)
        db_ref[:, pl.ds(n, 1), :] = jnp.sum(jnp.where(eye, jnp.broadcast_to(d_b_col, shp), 0.0), axis=1,
                                            keepdims=True)
        _put_heads(dq_ref, dq)
        _put_heads(dk_ref, dk)

    tok = pl.BlockSpec((CHUNK, hb * HEAD_DIM), lambda h, s: (nc - 1 - s, h))
    row = pl.BlockSpec((hb, nc, CHUNK), lambda h, s: (h, 0, 0))
    return _call(body, name, (_sds((t, D_MODEL)),) * 3 + (_sds((DN_HEADS, nc, CHUNK)),) * 2, (DN_HEADS // hb, nc),
                 [tok, tok, tok, row, row,
                  pl.BlockSpec((hb, None, HEAD_DIM, HEAD_DIM), lambda h, s: (h, nc - 1 - s, 0, 0)),
                  pl.BlockSpec((hb, None, CHUNK, CHUNK), lambda h, s: (h, nc - 1 - s, 0, 0)), tok, tok, tok],
                 (tok, tok, tok, row, row),
                 scratch=[pltpu.VMEM((hb, HEAD_DIM, HEAD_DIM), f32)])(q, k, v, g_rows, b_rows, s_all, t_all, u_all, w_all,
                                                                      do)


SEG = 8


def _perm_rows(a):
    t, c = a.shape
    return a.reshape(SEG, t // SEG, c).transpose(1, 0, 2).reshape(t, c)


def _unperm_rows(a):
    t, c = a.shape
    return a.reshape(t // SEG, SEG, c).transpose(1, 0, 2).reshape(t, c)


def _cmul(ar, ai, br, bi):
    return ar * br - ai * bi, ar * bi + ai * br


def _segment_init(er, ei, lr, li, seg_len, reverse):
    w = er.shape[1]
    sub = lax.broadcasted_iota(jnp.int32, (SEG, w), 0)

    def shift(x, k):
        if reverse:
            return jnp.where(sub < SEG - k, pltpu.roll(x, SEG - k, 0), 0.0)
        return jnp.where(sub >= k, pltpu.roll(x, k, 0), 0.0)

    pr, pi = lr, li
    for _ in range(seg_len.bit_length() - 1):
        pr, pi = _cmul(pr, pi, pr, pi)
    fr, fi = shift(er, 1), shift(ei, 1)
    for k in (1, 2, 4):
        sr, si = shift(fr, k), shift(fi, k)
        mr, mi = _cmul(pr, pi, sr, si)
        fr, fi = fr + mr, fi + mi
        pr, pi = _cmul(pr, pi, pr, pi)
    return fr, fi


def s5_fwd(name, u_perm, bbd, cbd, lam):
    t = u_perm.shape[0]
    tt = _tile(t, (1024, 512, 256, 128))
    nt, ng, w = t // tt, tt // SEG, SG_STATE
    seg_len = t // SEG
    assert seg_len & (seg_len - 1) == 0 and tt % SEG == 0

    def body(u_ref, b_ref, c_ref, lam_ref, y_ref, h_ref, x_scr, state):
        p, i = pl.program_id(1), pl.program_id(2)
        lr1, li1 = lam_ref[:, 0:w], lam_ref[:, w:2 * w]
        lr, li = jnp.broadcast_to(lr1, (SEG, w)), jnp.broadcast_to(li1, (SEG, w))
        x_scr[...] = _bdot(u_ref[...], b_ref[...], ((1,), (0,)))

        @pl.when(jnp.logical_and(p == 0, i == 0))
        def _():
            state[...] = jnp.zeros_like(state)

        @pl.when(jnp.logical_and(p == 1, i == 0))
        def _():
            sr, si = _segment_init(state[:, 0:w], state[:, w:2 * w], lr1, li1, seg_len, False)
            state[:, 0:w] = sr
            state[:, w:2 * w] = si

        def run(store):
            def step(g, st):
                row = pl.multiple_of(g * SEG, SEG)
                xg = x_scr[pl.ds(row, SEG), :]
                nr = lr * st[0] - li * st[1] + xg[:, 0:w]
                ni = lr * st[1] + li * st[0] + xg[:, w:2 * w]
                if store:
                    h_ref[pl.ds(row, SEG), 0:w] = nr
                    h_ref[pl.ds(row, SEG), w:2 * w] = ni
                return nr, ni

            fin = lax.fori_loop(0, ng, step, (state[:, 0:w], state[:, w:2 * w]))
            state[:, 0:w] = fin[0]
            state[:, w:2 * w] = fin[1]

        @pl.when(p == 0)
        def _():
            run(False)

        @pl.when(p == 1)
        def _():
            run(True)
            y_ref[...] = _bdot(h_ref[...], c_ref[...], ((1,), (0,)))

    return _call(body, name, (_sds((t, D_MODEL)), _sds((t, N_SUPER * 2 * w))), (N_SUPER, 2, nt),
                 [pl.BlockSpec((tt, 128), lambda s, p, i: (i, s)),
                  pl.BlockSpec((None, 128, 2 * w), lambda s, p, i: (s, 0, 0)),
                  pl.BlockSpec((None, 2 * w, 128), lambda s, p, i: (s, 0, 0)),
                  pl.BlockSpec((None, 1, 2 * w), lambda s, p, i: (s, 0, 0))],
                 (pl.BlockSpec((tt, 128), lambda s, p, i: (i * p, s)),
                  pl.BlockSpec((tt, 2 * w), lambda s, p, i: (i * p, s))),
                 scratch=[pltpu.VMEM((tt, 2 * w), f32), pltpu.VMEM((SEG, 2 * w), f32)])(u_perm, bbd, cbd, lam)


def s5_bwd(name, dy_perm, u_perm, h_perm, hprev0, bbd, cbd, lam):
    t = u_perm.shape[0]
    tt = _tile(t, (1024, 512, 256, 128))
    nt, ng, w = t // tt, tt // SEG, SG_STATE
    seg_len = t // SEG

    def body(dy_ref, u_ref, h_ref, hp_ref, hp0_ref, b_ref, c_ref, lam_ref, du_ref, db_ref, dc_ref, dl_ref,
             g_scr, state, dl_acc):
        p, i = pl.program_id(1), pl.program_id(2)
        first_tile = jnp.logical_or(p == 0, i == nt - 1)
        lr1, li1 = lam_ref[:, 0:w], -lam_ref[:, w:2 * w]
        lr, li = jnp.broadcast_to(lr1, (SEG, w)), jnp.broadcast_to(li1, (SEG, w))
        g_scr[...] = _bdot(dy_ref[...], c_ref[...], ((1,), (1,)))

        @pl.when(jnp.logical_and(p == 0, i == 0))
        def _():
            state[...] = jnp.zeros_like(state)

        @pl.when(jnp.logical_and(p == 1, i == 0))
        def _():
            sr, si = _segment_init(state[:, 0:w], state[:, w:2 * w], lr1, li1, seg_len, True)
            state[:, 0:w] = sr
            state[:, w:2 * w] = si
            dl_acc[...] = jnp.zeros_like(dl_acc)

        def adj(g, st):
            row = pl.multiple_of(g * SEG, SEG)
            gg = g_scr[pl.ds(row, SEG), :]
            nr = lr * st[0] - li * st[1] + gg[:, 0:w]
            ni = lr * st[1] + li * st[0] + gg[:, w:2 * w]
            return row, nr, ni

        @pl.when(p == 0)
        def _():
            def step(k, st):
                _, nr, ni = adj(ng - 1 - k, st)
                return nr, ni

            fin = lax.fori_loop(0, ng, step, (state[:, 0:w], state[:, w:2 * w]))
            state[:, 0:w] = fin[0]
            state[:, w:2 * w] = fin[1]

        @pl.when(p == 1)
        def _():
            above = jnp.where(first_tile, hp0_ref[...], hp_ref[...])

            def step(k, st):
                g = ng - 1 - k
                row, nr, ni = adj(g, st)
                g_scr[pl.ds(row, SEG), 0:w] = nr
                g_scr[pl.ds(row, SEG), w:2 * w] = ni
                prow = pl.multiple_of(jnp.maximum(g - 1, 0) * SEG, SEG)
                hp = jnp.where(g > 0, h_ref[pl.ds(prow, SEG), :], above)
                pr, pi = hp[:, 0:w], hp[:, w:2 * w]
                return nr, ni, st[2] + nr * pr + ni * pi, st[3] + ni * pr - nr * pi

            fin = lax.fori_loop(0, ng, step, (state[:, 0:w], state[:, w:2 * w], dl_acc[:, 0:w], dl_acc[:, w:2 * w]))
            state[:, 0:w] = fin[0]
            state[:, w:2 * w] = fin[1]
            dl_acc[:, 0:w] = fin[2]
            dl_acc[:, w:2 * w] = fin[3]
            a = g_scr[...]
            du_ref[...] = _bdot(a, b_ref[...], ((1,), (1,)))
            d_b = _bdot(u_ref[...], a, ((0,), (0,)))
            d_c = _bdot(h_ref[...], dy_ref[...], ((0,), (0,)))

            @pl.when(i == 0)
            def _():
                db_ref[...] = d_b
                dc_ref[...] = d_c

            @pl.when(i > 0)
            def _():
                db_ref[...] += d_b
                dc_ref[...] += d_c

            @pl.when(i == nt - 1)
            def _():
                dl_ref[...] = jnp.sum(dl_acc[...], axis=0, keepdims=True)

    tile = lambda s, p, i: (nt - 1 - i, s)
    tile1 = lambda s, p, i: (nt - 1 - i * p, s)
    above = lambda s, p, i: (jnp.maximum((nt - 1 - i * p) * (tt // SEG) - 1, 0), s)
    per_s = lambda s, p, i: (s, 0, 0)
    return _call(body, name, (_sds((t, D_MODEL)), _sds((N_SUPER, 128, 2 * w)), _sds((N_SUPER, 2 * w, 128)),
                              _sds((N_SUPER, 1, 2 * w))), (N_SUPER, 2, nt),
                 [pl.BlockSpec((tt, 128), tile), pl.BlockSpec((tt, 128), tile1), pl.BlockSpec((tt, 2 * w), tile1),
                  pl.BlockSpec((SEG, 2 * w), above), pl.BlockSpec((SEG, 2 * w), lambda s, p, i: (0, s)),
                  pl.BlockSpec((None, 128, 2 * w), per_s), pl.BlockSpec((None, 2 * w, 128), per_s),
                  pl.BlockSpec((None, 1, 2 * w), per_s)],
                 (pl.BlockSpec((tt, 128), tile1), pl.BlockSpec((None, 128, 2 * w), per_s),
                  pl.BlockSpec((None, 2 * w, 128), per_s), pl.BlockSpec((None, 1, 2 * w), per_s)),
                 scratch=[pltpu.VMEM((tt, 2 * w), f32), pltpu.VMEM((SEG, 2 * w), f32),
                          pltpu.VMEM((SEG, 2 * w), f32)])(dy_perm, u_perm, h_perm, h_perm, hprev0, bbd, cbd, lam)


def _s5_disc(a_re, a_im, log_dt, br, bi):
    dt = jnp.exp(log_dt)
    mag = jnp.exp(a_re * dt)
    lr, li = mag * jnp.cos(a_im * dt), mag * jnp.sin(a_im * dt)
    den = a_re * a_re + a_im * a_im
    fr = ((lr - 1.0) * a_re + li * a_im) / den
    fi = (li * a_re - (lr - 1.0) * a_im) / den
    return lr, li, fr * br - fi * bi, fr * bi + fi * br


def s5_disc_fwd(name, a_re, a_im, log_dt, br, bi):
    g, n = SSM_GROUPS, SSM_STATE

    def body(ar, ai, ld, brr, bir, lr, li, bbr, bbi):
        o = _s5_disc(ar[...], ai[...], ld[...], brr[...], bir[...])
        lr[...], li[...], bbr[...], bbi[...] = o

    return _call(body, name, (_sds((g, 1, n)), _sds((g, 1, n)), _sds((g, SSM_GROUP, n)), _sds((g, SSM_GROUP, n))))(
        a_re, a_im, log_dt, br, bi)


def s5_disc_bwd(name, a_re, a_im, log_dt, br, bi, d_lr, d_li, d_bbr, d_bbi):
    g, n = SSM_GROUPS, SSM_STATE

    def body(ar, ai, ld, brr, bir, c1, c2, c3, c4, o1, o2, o3, o4, o5):
        _, vjp = jax.vjp(_s5_disc, ar[...], ai[...], ld[...], brr[...], bir[...])
        o1[...], o2[...], o3[...], o4[...], o5[...] = vjp((c1[...], c2[...], c3[...], c4[...]))

    return _call(body, name, (_sds((g, 1, n)), _sds((g, 1, n)), _sds((g, 1, 1)), _sds((g, SSM_GROUP, n)),
                              _sds((g, SSM_GROUP, n))))(a_re, a_im, log_dt, br, bi, d_lr, d_li, d_bbr, d_bbi)


def gelu_fwd(name, ys_lin, proj, d_skip):
    t, d = ys_lin.shape
    tm = _rows(t)

    def body(y_ref, u_ref, d_ref, o_ref):
        o_ref[...] = _gelu(y_ref[...] + d_ref[...] * u_ref[...])

    return _call(body, name, _sds((t, d)), (t // tm,),
                 [_row_spec(tm, d), pl.BlockSpec((tm, d), lambda i: (i, 4)), _acc_spec(1, d)],
                 _row_spec(tm, d))(ys_lin, proj, d_skip)


def _head_norm(o, hn):
    outs, ns, rs = [], [], []
    for h in range(DN_HEADS):
        oh = o[:, h * 128:(h + 1) * 128]
        r = lax.rsqrt(jnp.mean(oh * oh, axis=-1, keepdims=True) + EPS)
        n = oh * r
        ns.append(n)
        rs.append(r)
        outs.append(n * hn)
    return outs, ns, rs


def merge_fwd(name, proj, o, yg, glu_lin, head_norm, b_glu):
    t = o.shape[0]
    tm = _rows(t)
    d = D_MODEL

    def body(za_ref, zb_ref, ra_ref, rb_ref, o_ref, yg_ref, gl_ref, hn_ref, bg_ref, m_ref):
        on, _, _ = _head_norm(o_ref[...], hn_ref[...])
        y_a = jnp.concatenate(on, axis=1) * _silu(za_ref[...])
        y_b = yg_ref[...] * _sigmoid(gl_ref[...] + bg_ref[...]) * _silu(zb_ref[...])
        m_ref[...] = (_sigmoid(ra_ref[...]) * y_a + _sigmoid(rb_ref[...]) * y_b).astype(bf16)

    pc = lambda c: pl.BlockSpec((tm, d), lambda i: (i, c))
    return _call(body, name, _sds((t, d), bf16), (t // tm,),
                 [pc(3), pc(5), pc(6), pc(7), _row_spec(tm, d), _row_spec(tm, d), _row_spec(tm, d),
                  _acc_spec(1, 128), _acc_spec(1, d)], _row_spec(tm, d))(
        proj, proj, proj, proj, o, yg, glu_lin, head_norm, b_glu)


def merge_bwd(name, proj, o, yg, glu_lin, head_norm, b_glu, dm):
    t = o.shape[0]
    tm = _rows(t)
    d = D_MODEL

    def body(za_ref, zb_ref, ra_ref, rb_ref, o_ref, yg_ref, gl_ref, hn_ref, bg_ref, dm_ref,
             dza_ref, dzb_ref, dra_ref, drb_ref, do_ref, dgl_ref, dyg_ref, dhn_ref, dbg_ref):
        hn = hn_ref[...]
        za, zb, ra, rb = za_ref[...], zb_ref[...], ra_ref[...], rb_ref[...]
        on, ns, rs = _head_norm(o_ref[...], hn)
        onc = jnp.concatenate(on, axis=1)
        sza = _silu(za)
        y_a = onc * sza
        yg = yg_ref[...]
        sgl = _sigmoid(gl_ref[...] + bg_ref[...])
        y2 = yg * sgl
        szb = _silu(zb)
        y_b = y2 * szb
        sra, srb = _sigmoid(ra), _sigmoid(rb)
        dmv = dm_ref[...]
        dra_ref[...] = dmv * y_a * sra * (1.0 - sra)
        drb_ref[...] = dmv * y_b * srb * (1.0 - srb)
        d_ya = dmv * sra
        d_yb = dmv * srb
        dza_ref[...] = d_ya * onc * _dsilu(za)
        dzb_ref[...] = d_yb * y2 * _dsilu(zb)
        d_on = d_ya * sza
        d_y2 = d_yb * szb
        dyg_ref[...] = d_y2 * sgl
        d_gl = d_y2 * yg * sgl * (1.0 - sgl)
        dgl_ref[...] = d_gl.astype(bf16)
        _accumulate(dbg_ref, jnp.sum(d_gl, axis=0, keepdims=True))
        d_hn = jnp.zeros((1, 128), f32)
        for h in range(DN_HEADS):
            hs = slice(h * 128, (h + 1) * 128)
            dh = d_on[:, hs]
            d_hn = d_hn + jnp.sum(dh * ns[h], axis=0, keepdims=True)
            dn = dh * hn
            do_ref[:, hs] = rs[h] * (dn - ns[h] * jnp.mean(dn * ns[h], axis=-1, keepdims=True))
        _accumulate(dhn_ref, d_hn)

    pc = lambda c: pl.BlockSpec((tm, d), lambda i: (i, c))
    rs_ = _row_spec(tm, d)
    return _call(body, name, (_sds((t, d)),) * 5 + (_sds((t, d), bf16), _sds((t, d)), _sds((1, 128)), _sds((1, d))),
                 (t // tm,), [pc(3), pc(5), pc(6), pc(7), rs_, rs_, rs_, _acc_spec(1, 128), _acc_spec(1, d), rs_],
                 (rs_,) * 7 + (_acc_spec(1, 128), _acc_spec(1, d)))(
        proj, proj, proj, proj, o, yg, glu_lin, head_norm, b_glu, dm)


def gelu_bwd(name, ys_lin, proj, d_skip, dyg_a, dyg_b):
    t, d = ys_lin.shape
    tm = _rows(t)

    def body(y_ref, u_ref, d_ref, a_ref, b_ref, dys_ref, du_ref, dd_ref):
        uv = u_ref[...]
        dys = (a_ref[...] + b_ref[...]) * _dgelu(y_ref[...] + d_ref[...] * uv)
        dys_ref[...] = dys
        du_ref[...] = dys * d_ref[...]
        _accumulate(dd_ref, jnp.sum(dys * uv, axis=0, keepdims=True))

    rs_ = _row_spec(tm, d)
    return _call(body, name, (_sds((t, d)), _sds((t, d)), _sds((1, d))), (t // tm,),
                 [rs_, pl.BlockSpec((tm, d), lambda i: (i, 4)), _acc_spec(1, d), rs_, rs_],
                 (rs_, rs_, _acc_spec(1, d)))(ys_lin, proj, d_skip, dyg_a, dyg_b)


def assemble_dproj(name, dqkv, dza, du_a, du_b, dzb, dra, drb, dbd):
    t = dza.shape[0]
    tm = _rows(t)
    d = D_MODEL

    def body(qkv_ref, za_ref, ua_ref, ub_ref, zb_ref, ra_ref, rb_ref, bd_ref, o_ref):
        o_ref[:, 0:3 * d] = qkv_ref[...].astype(bf16)
        o_ref[:, 3 * d:4 * d] = za_ref[...].astype(bf16)
        o_ref[:, 4 * d:5 * d] = (ua_ref[...] + ub_ref[...]).astype(bf16)
        o_ref[:, 5 * d:6 * d] = zb_ref[...].astype(bf16)
        o_ref[:, 6 * d:7 * d] = ra_ref[...].astype(bf16)
        o_ref[:, 7 * d:8 * d] = rb_ref[...].astype(bf16)
        o_ref[:, 8 * d:8 * d + 128] = bd_ref[...].astype(bf16)
        o_ref[:, 8 * d + 128:W_PAD] = jnp.zeros((tm, W_PAD - 8 * d - 128), bf16)

    rs_ = _row_spec(tm, d)
    return _call(body, name, _sds((t, W_PAD), bf16), (t // tm,),
                 [_row_spec(tm, 3 * d), rs_, rs_, rs_, rs_, rs_, rs_, _row_spec(tm, 128)], _row_spec(tm, W_PAD))(
        dqkv, dza, du_a, du_b, dzb, dra, drb, dbd)


def adamw(name, w, g, m, v):
    lead, (r, c) = w.shape[:-2], w.shape[-2:]
    tm = _tile(r, (512, 256, 128, 64, 32, 16, 8))
    c1 = 1.0 / (1.0 - ADAM_B1 ** ADAM_STEP)
    c2 = 1.0 / (1.0 - ADAM_B2 ** ADAM_STEP)

    def body(w_ref, g_ref, m_ref, v_ref, d_ref, nm_ref, nv_ref):
        gv = g_ref[...]
        nm = ADAM_B1 * m_ref[...] + (1.0 - ADAM_B1) * gv
        nv = ADAM_B2 * v_ref[...] + (1.0 - ADAM_B2) * (gv * gv)
        d_ref[...] = -ADAM_LR * ((nm * c1) / (jnp.sqrt(nv * c2) + ADAM_EPS) + ADAM_WD * w_ref[...])
        nm_ref[...] = nm
        nv_ref[...] = nv

    if lead:
        sp = pl.BlockSpec((None, tm, c), lambda l, i: (l, i, 0))
        grid = (lead[0], r // tm)
    else:
        sp = pl.BlockSpec((tm, c), lambda i: (i, 0))
        grid = (r // tm,)
    return _call(body, name, (_sds(w.shape),) * 3, grid, [sp] * 4, (sp,) * 3)(w, g, m, v)


def _coords():
    return lax.axis_index("x"), lax.axis_index("y"), lax.axis_index("c")


def _lin(dev):
    return 4 * dev[0] + 2 * dev[1] + dev[2]


def _chips(me):
    x, y, _ = me
    return [(1 - x, y), (x, 1 - y), (1 - x, 1 - y)]


def _gather_steps(x_ref, o_ref, send_sems, recv_sems, local_sem):
    me = _coords()
    x, y, cc = me
    sibling = (x, y, 1 - cc)
    chips = _chips(me)

    def copy(k, block, to, src=None):
        return pltpu.make_async_remote_copy(
            src_ref=o_ref.at[_lin(block)] if src is None else src, dst_ref=o_ref.at[_lin(block)],
            send_sem=send_sems.at[k], recv_sem=recv_sems.at[k], device_id=to, device_id_type=MESH)

    mine = pltpu.make_async_copy(x_ref, o_ref.at[_lin(me)], local_sem)
    first = [copy(0, me, sibling, src=x_ref)] + [copy(1 + j, me, (*chip, cc), src=x_ref)
                                                 for j, chip in enumerate(chips)]
    passed = [copy(4 + j, (*chip, cc), sibling) for j, chip in enumerate(chips)]

    def start():
        mine.start()
        for cp in first:
            cp.start()

    def forward():
        for j, chip in enumerate(chips):
            copy(1 + j, (*chip, cc), me).wait_recv()
            passed[j].start()

    def finish():
        copy(0, sibling, me).wait_recv()
        for j, chip in enumerate(chips):
            copy(4 + j, (*chip, 1 - cc), me).wait_recv()
        for cp in first + passed:
            cp.wait_send()
        mine.wait()

    return start, forward, finish


_GATHER_SEMS = [pltpu.SemaphoreType.DMA((N_DEV - 1,)), pltpu.SemaphoreType.DMA((N_DEV - 1,)),
                pltpu.SemaphoreType.DMA(())]


def all_gather(name, shard):
    r, c = shard.shape

    def body(x_ref, o_ref, send_sems, recv_sems, local_sem):
        for step in _gather_steps(x_ref, o_ref, send_sems, recv_sems, local_sem):
            step()

    any_spec = pl.BlockSpec(memory_space=pl.ANY)
    return _call(body, name, _sds((N_DEV, r, c), shard.dtype), in_specs=[any_spec], out_specs=any_spec,
                 scratch=list(_GATHER_SEMS))(shard)


def pair_exchange(name, blocks):
    _, _, r, c = blocks.shape

    def body(x_ref, o_ref, send_sems, recv_sems):
        x, y, cc = _coords()
        sibling = (x, y, 1 - cc)
        cps = [pltpu.make_async_remote_copy(src_ref=x_ref.at[ch, 1 - cc], dst_ref=o_ref.at[ch], send_sem=send_sems.at[ch],
                                            recv_sem=recv_sems.at[ch], device_id=sibling, device_id_type=MESH)
               for ch in range(4)]
        for cp in cps:
            cp.start()
        for cp in cps:
            cp.wait()

    any_spec = pl.BlockSpec(memory_space=pl.ANY)
    return _call(body, name, _sds((4, r, c), blocks.dtype), in_specs=[any_spec], out_specs=any_spec,
                 scratch=[pltpu.SemaphoreType.DMA((4,)), pltpu.SemaphoreType.DMA((4,))])(blocks)


def chip_exchange(name, blocks):
    _, r, c = blocks.shape

    def body(x_ref, o_ref, send_sems, recv_sems, local_sem):
        me = _coords()
        x, y, cc = me
        my_chip = 2 * x + y
        mine = pltpu.make_async_copy(x_ref.at[my_chip], o_ref.at[my_chip], local_sem)
        mine.start()
        cps = []
        for j, (px, py) in enumerate(_chips(me)):
            cp = pltpu.make_async_remote_copy(src_ref=x_ref.at[2 * px + py], dst_ref=o_ref.at[my_chip],
                                              send_sem=send_sems.at[j], recv_sem=recv_sems.at[j],
                                              device_id=(px, py, cc), device_id_type=MESH)
            cp.start()
            cps.append(cp)
        for j, (px, py) in enumerate(_chips(me)):
            pltpu.make_async_remote_copy(src_ref=x_ref.at[my_chip], dst_ref=o_ref.at[2 * px + py],
                                         send_sem=send_sems.at[j], recv_sem=recv_sems.at[j],
                                         device_id=(px, py, cc), device_id_type=MESH).wait_recv()
        for cp in cps:
            cp.wait_send()
        mine.wait()

    any_spec = pl.BlockSpec(memory_space=pl.ANY)
    return _call(body, name, _sds(blocks.shape, blocks.dtype), in_specs=[any_spec], out_specs=any_spec,
                 scratch=[pltpu.SemaphoreType.DMA((3,)), pltpu.SemaphoreType.DMA((3,)),
                          pltpu.SemaphoreType.DMA(())])(blocks)


def pair_sum(name, mine, theirs):
    _, r, c = mine.shape
    tm = _tile(r, (240, 256, 128, 64, 32, 16))

    def body(a_ref, b_ref, o_ref):
        o_ref[...] = (a_ref[...].astype(f32) + b_ref[...].astype(f32)).astype(o_ref.dtype)

    sp = pl.BlockSpec((None, tm, c), lambda ch, i: (ch, i, 0))
    return _call(body, name, _sds(mine.shape, mine.dtype), (4, r // tm), [sp, sp], sp)(mine, theirs)


def sum_slots(name, slots):
    n, r, c = slots.shape
    tm = _tile(r, (240, 256, 128, 64, 32, 16))

    def body(s_ref, o_ref):
        acc = s_ref[0].astype(f32)
        for d in range(1, n):
            acc = acc + s_ref[d].astype(f32)
        o_ref[...] = acc

    return _call(body, name, _sds((r, c)), (r // tm,), [pl.BlockSpec((n, tm, c), lambda i: (0, i, 0))],
                 pl.BlockSpec((tm, c), lambda i: (i, 0)))(slots)


def _rows_layout(col8, t):
    return col8.T.reshape(DN_HEADS, t // CHUNK, CHUNK)


def _blockdiag(m):
    g, a, b = m.shape
    m = m.reshape(N_SUPER, SUPER, a, b)
    out = jnp.einsum("sgab,gh->sgahb", m, jnp.eye(SUPER, dtype=m.dtype))
    return out.reshape(N_SUPER, SUPER * a, SUPER * b)


def _diag_blocks(m, a, b):
    m = m.reshape(N_SUPER, SUPER, a, SUPER, b)
    return jnp.einsum("sgahb,gh->sgab", m, jnp.eye(SUPER, dtype=m.dtype)).reshape(SSM_GROUPS, a, b)


def _s5_params(p, li):
    tag = f"l{li}"
    n = SSM_STATE
    a_re = p["ssm_a_re"].reshape(SSM_GROUPS, 1, n)
    a_im = p["ssm_a_im"].reshape(SSM_GROUPS, 1, n)
    log_dt = p["ssm_log_dt"].reshape(SSM_GROUPS, 1, 1)
    br = jnp.swapaxes(p["ssm_b_re"], 1, 2)
    bi = jnp.swapaxes(p["ssm_b_im"], 1, 2)
    lr, li_, bbr, bbi = s5_disc_fwd("s5_disc_" + tag, a_re, a_im, log_dt, br, bi)
    lam = jnp.concatenate([lr.reshape(N_SUPER, 1, SG_STATE), li_.reshape(N_SUPER, 1, SG_STATE)], axis=-1)
    bbd = jnp.concatenate([_blockdiag(bbr), _blockdiag(bbi)], axis=-1).astype(bf16)
    c_re = jnp.swapaxes(p["ssm_c_re"], 1, 2)
    c_im = jnp.swapaxes(p["ssm_c_im"], 1, 2)
    cbd = jnp.concatenate([_blockdiag(c_re), -_blockdiag(c_im)], axis=1).astype(bf16)
    return dict(a_re=a_re, a_im=a_im, log_dt=log_dt, br=br, bi=bi, lam=lam, bbd=bbd, cbd=cbd)


def layer_fwd(x, p, li, next_shard=None):
    tag = f"l{li}"
    t = x.shape[0]
    d = D_MODEL
    h = rmsnorm_fwd("norm_pre_" + tag, x, p["norm_pre"])
    proj = mm_nn("proj_" + tag, h, p["w_all"])
    q, k, v, bg = conv_qkv_fwd("conv_" + tag, proj, p["conv_w"], p["a_log"], p["dt_bias"])
    b_rows = _rows_layout(bg[:, 0:DN_HEADS], t)
    g_rows = _rows_layout(bg[:, DN_HEADS:2 * DN_HEADS], t)
    o, s_all, t_all, u_all, w_sol, *gathered = delta_fwd("delta_" + tag, q, k, v, g_rows, b_rows, next_shard)
    sp = _s5_params(p, li)
    u_perm = _perm_rows(proj[:, 4 * d:5 * d])
    ys_perm, hs = s5_fwd("s5_" + tag, u_perm, sp["bbd"], sp["cbd"], sp["lam"])
    ys_lin = _unperm_rows(ys_perm)
    yg = gelu_fwd("gelu_" + tag, ys_lin, proj, p["ssm_d"])
    glu_lin = mm_nn("glu_" + tag, yg, p["w_glu"])
    merged = merge_fwd("merge_" + tag, proj, o, yg, glu_lin, p["head_norm"], p["b_glu"])
    out = mm_nn("out_" + tag, merged, p["w_out"])
    y = residual_norm_fwd("norm_post_" + tag, x, out, p["norm_post"])
    saved = dict(x=x, h=h, proj=proj, q=q, k=k, v=v, g_rows=g_rows, b_rows=b_rows, o=o, s_all=s_all, t_all=t_all, u_all=u_all, w_sol=w_sol, sp=sp, u_perm=u_perm,
                 hs=hs, ys_lin=ys_lin, yg=yg, glu_lin=glu_lin, merged=merged, out=out)
    return y, saved, (gathered[0] if gathered else None)


def layer_bwd(dy, p, s, li):
    tag = f"l{li}"
    t = dy.shape[0]
    d = D_MODEL
    sp = s["sp"]
    gr = {}
    d_out, gr["norm_post"] = post_norm_bwd("norm_post_b_" + tag, s["out"], p["norm_post"], dy)
    d_merged = mm_nt("out_b_" + tag, d_out, p["w_out"])
    gr["w_out"] = mm_tn("out_w_" + tag, s["merged"], d_out, bf16)
    (dza, dzb, dra, drb, d_o, d_glu, dyg_a, gr["head_norm"], gr["b_glu"]) = merge_bwd(
        "merge_b_" + tag, s["proj"], s["o"], s["yg"], s["glu_lin"], p["head_norm"], p["b_glu"], d_merged)
    dyg_b = mm_nt("glu_b_" + tag, d_glu, p["w_glu"])
    gr["w_glu"] = mm_tn("glu_w_" + tag, s["yg"], d_glu, bf16)
    d_ys, du_a, gr["ssm_d"] = gelu_bwd("gelu_b_" + tag, s["ys_lin"], s["proj"], p["ssm_d"], dyg_a, dyg_b)
    hprev0 = jnp.concatenate([jnp.zeros((1, s["hs"].shape[1]), f32), s["hs"][-SEG:-1]], axis=0)
    du_perm, d_bbd, d_cbd, d_lam = s5_bwd("s5_b_" + tag, _perm_rows(d_ys), s["u_perm"], s["hs"], hprev0, sp["bbd"],
                                          sp["cbd"], sp["lam"])
    du_b = _unperm_rows(du_perm)
    d_c_re = _diag_blocks(d_cbd[:, 0:SG_STATE, :], SSM_STATE, SSM_GROUP)
    d_c_im = -_diag_blocks(d_cbd[:, SG_STATE:, :], SSM_STATE, SSM_GROUP)
    gr["ssm_c_re"] = jnp.swapaxes(d_c_re, 1, 2)
    gr["ssm_c_im"] = jnp.swapaxes(d_c_im, 1, 2)
    d_bbr = _diag_blocks(d_bbd[:, :, 0:SG_STATE], SSM_GROUP, SSM_STATE)
    d_bbi = _diag_blocks(d_bbd[:, :, SG_STATE:], SSM_GROUP, SSM_STATE)
    d_lr = d_lam[:, :, 0:SG_STATE].reshape(SSM_GROUPS, 1, SSM_STATE)
    d_li = d_lam[:, :, SG_STATE:].reshape(SSM_GROUPS, 1, SSM_STATE)
    d_are, d_aim, d_ldt, d_br, d_bi = s5_disc_bwd("s5_disc_b_" + tag, sp["a_re"], sp["a_im"], sp["log_dt"], sp["br"],
                                                  sp["bi"], d_lr, d_li, d_bbr, d_bbi)
    gr["ssm_a_re"] = d_are.reshape(SSM_GROUPS, SSM_STATE)
    gr["ssm_a_im"] = d_aim.reshape(SSM_GROUPS, SSM_STATE)
    gr["ssm_log_dt"] = d_ldt.reshape(SSM_GROUPS)
    gr["ssm_b_re"] = jnp.swapaxes(d_br, 1, 2)
    gr["ssm_b_im"] = jnp.swapaxes(d_bi, 1, 2)
    dq, dk, dv, dg_rows, db_rows = delta_bwd("delta_b_" + tag, s["q"], s["k"], s["v"], s["g_rows"], s["b_rows"],
                                             s["s_all"], s["t_all"], s["u_all"], s["w_sol"], d_o)
    dbg = jnp.concatenate([db_rows.reshape(DN_HEADS, t).T, dg_rows.reshape(DN_HEADS, t).T,
                           jnp.zeros((t, 128 - 2 * DN_HEADS), f32)], axis=1)
    dc, gr["conv_w"], dbd, dab = conv_qkv_bwd("conv_b_" + tag, s["proj"], p["conv_w"], p["a_log"], p["dt_bias"],
                                              dq, dk, dv, dbg)
    gr["a_log"] = dab[0, DN_HEADS:2 * DN_HEADS]
    gr["dt_bias"] = dab[1, DN_HEADS:2 * DN_HEADS]
    dqkv = conv_transpose("conv_t_" + tag, dc, p["conv_w"])
    dproj = assemble_dproj("dproj_" + tag, dqkv, dza, du_a, du_b, dzb, dra, drb, dbd)
    d_h = mm_nt("proj_b_" + tag, dproj, p["w_all"])
    gr["w_all"] = mm_tn("proj_w_" + tag, s["h"], dproj, bf16)
    dx, gr["norm_pre"] = rmsnorm_bwd("norm_pre_b_" + tag, s["x"], p["norm_pre"], d_h, dy)
    return dx, gr


REPL = ["norm_pre", "a_log", "dt_bias", "head_norm", "ssm_a_re", "ssm_a_im", "ssm_log_dt", "ssm_b_re", "ssm_b_im",
        "ssm_c_re", "ssm_c_im", "ssm_d", "b_glu", "norm_post"]
SHARDED = ["w_in", "conv_w", "w_glu", "w_out"]
ALL_W = ["norm_pre", "w_in", "conv_w", "a_log", "dt_bias", "head_norm", "ssm_a_re", "ssm_a_im", "ssm_log_dt",
         "ssm_b_re", "ssm_b_im", "ssm_c_re", "ssm_c_im", "ssm_d", "w_glu", "b_glu", "w_out", "norm_post"]
PACK_W = 1024


def _pack_flat(arrs, rows):
    flat = jnp.concatenate([a.reshape(-1) for a in arrs])
    return jnp.pad(flat, (0, rows * PACK_W - flat.shape[0])).reshape(rows, PACK_W)


def _flat_rows(arrs, mult=8):
    n = sum(math.prod(a.shape) for a in arrs)
    rows = -(-n // PACK_W)
    return -(-rows // mult) * mult


def _unpack(flat, shapes):
    out, off = [], 0
    for sh in shapes:
        n = math.prod(sh)
        out.append(flat[off:off + n].reshape(sh))
        off += n
    return out


def _repl_split(shapes):
    big = [n for n in REPL if math.prod(shapes[n]) % PACK_W == 0]
    small = [n for n in REPL if n not in big]
    return big, small


def _rows8(n):
    return -(-n // (8 * PACK_W)) * 8


def _repl_rows(shapes):
    big, small = _repl_split(shapes)
    rows = sum(_rows8(math.prod(shapes[n])) for n in big) + _flat_rows([_sds(shapes[n]) for n in small], 8)
    return -(-rows // (8 * N_DEV)) * (8 * N_DEV)


def _repl_pack(arrs, shapes):
    big, small = _repl_split(shapes)
    parts = []
    for n in big:
        a = arrs[n].reshape(-1, PACK_W)
        parts.append(jnp.pad(a, ((0, _rows8(a.size) - a.shape[0]), (0, 0))))
    parts.append(_pack_flat([arrs[n] for n in small], _flat_rows([_sds(shapes[n]) for n in small], 8)))
    used = sum(p.shape[0] for p in parts)
    parts.append(jnp.zeros((_repl_rows(shapes) - used, PACK_W), parts[0].dtype))
    return jnp.concatenate(parts, axis=0)


def _repl_unpack(packed, shapes):
    big, small = _repl_split(shapes)
    out, off = {}, 0
    for n in big:
        size = math.prod(shapes[n])
        out[n] = packed[off:off + size // PACK_W].reshape(shapes[n])
        off += _rows8(size)
    srows = _flat_rows([_sds(shapes[n]) for n in small], 8)
    out.update(zip(small, _unpack(packed[off:off + srows].reshape(-1), [shapes[n] for n in small])))
    return out


_COL_RUNS = ((0, 4096), (4112, W_COLS), (4096, 4112))


def _w_all(main, tails, wc):
    pieces = []
    for lo, hi in _COL_RUNS:
        for dv in range(N_DEV):
            a, b = max(lo, dv * wc) - dv * wc, min(hi, (dv + 1) * wc) - dv * wc
            if a < min(b, PACK_W):
                pieces.append(main[dv][:, :, a:min(b, PACK_W)])
            if b > max(a, PACK_W):
                pieces.append(tails[dv][:, :, max(a, PACK_W) - PACK_W:b - PACK_W])
    ll, rows = main[0].shape[0], main[0].shape[1]
    pieces.append(jnp.zeros((ll, rows, W_PAD - W_COLS), main[0].dtype))
    return jnp.concatenate(pieces, axis=2)


def _ref_cols(g, lo, hi):
    pieces, off = [], 0
    for a, b in _COL_RUNS:
        s, e = max(lo, a), min(hi, b)
        if s < e:
            pieces.append((s, g[..., off + s - a:off + e - a]))
        off += b - a
    pieces.sort(key=lambda t: t[0])
    return jnp.concatenate([p for _, p in pieces], axis=-1) if len(pieces) > 1 else pieces[0][1]


def kernel(x, norm_pre, w_in, conv_w, a_log, dt_bias, head_norm, ssm_a_re, ssm_a_im, ssm_log_dt, ssm_b_re, ssm_b_im, ssm_c_re, ssm_c_im, ssm_d, w_glu, b_glu, w_out, norm_post, loss_target, m_norm_pre, m_w_in, m_conv_w, m_a_log, m_dt_bias, m_head_norm, m_ssm_a_re, m_ssm_a_im, m_ssm_log_dt, m_ssm_b_re, m_ssm_b_im, m_ssm_c_re, m_ssm_c_im, m_ssm_d, m_w_glu, m_b_glu, m_w_out, m_norm_post, v_norm_pre, v_w_in, v_conv_w, v_a_log, v_dt_bias, v_head_norm, v_ssm_a_re, v_ssm_a_im, v_ssm_log_dt, v_ssm_b_re, v_ssm_b_im, v_ssm_c_re, v_ssm_c_im, v_ssm_d, v_w_glu, v_b_glu, v_w_out, v_norm_post):
    loc = dict(locals())
    w = {n: loc[n] for n in ALL_W}
    m = {n: loc["m_" + n] for n in ALL_W}
    v = {n: loc["v_" + n] for n in ALL_W}
    depth = w_in.shape[0]
    wc = w_in.shape[2]
    cc = conv_w.shape[2]
    wr = w_glu.shape[1]

    tail = wc - PACK_W
    conv_hi = conv_w.astype(bf16)
    conv_mid = (conv_w - conv_hi.astype(f32)).astype(bf16)
    conv_lo = (conv_w - conv_hi.astype(f32) - conv_mid.astype(f32)).astype(bf16)
    conv3 = jnp.stack([conv_hi, conv_mid, conv_lo], axis=1)
    w_in_b = w_in.astype(bf16)
    misc_shapes = [(D_MODEL, tail), (3, CONV_K, cc)]
    misc_rows = _flat_rows([_sds(sh, bf16) for sh in misc_shapes], 16)

    def layer_shard(li):
        return jnp.concatenate([w_in_b[li, :, :PACK_W], w_glu[li].astype(bf16), w_out[li].astype(bf16),
                                _pack_flat([w_in_b[li, :, PACK_W:], conv3[li]], misc_rows)])

    def layer_weights(gathered):
        miscs = [_unpack(gathered[dv, D_MODEL + 2 * wr:].reshape(-1), misc_shapes) for dv in range(N_DEV)]
        w_all = _w_all([gathered[dv, :D_MODEL][None] for dv in range(N_DEV)], [mi[0][None] for mi in miscs], wc)[0]
        glu = gathered[:, D_MODEL:D_MODEL + wr].reshape(N_DEV * wr, D_MODEL)
        out = gathered[:, D_MODEL + wr:D_MODEL + 2 * wr].reshape(N_DEV * wr, D_MODEL)
        conv = jnp.concatenate([mi[1][0].astype(f32) + mi[1][1].astype(f32) + mi[1][2].astype(f32) for mi in miscs],
                               axis=1)
        return dict(w_all=w_all, w_glu=glu, w_out=out, conv_w=conv)

    def layer_params(li, gathered):
        return dict(layer_weights(gathered), norm_pre=norm_pre[li].reshape(1, -1),
                    a_log=jnp.pad(a_log[li], (DN_HEADS, 128 - 2 * DN_HEADS)).reshape(1, 128),
                    dt_bias=jnp.pad(dt_bias[li], (DN_HEADS, 128 - 2 * DN_HEADS)).reshape(1, 128),
                    head_norm=head_norm[li].reshape(1, -1), ssm_a_re=ssm_a_re[li], ssm_a_im=ssm_a_im[li],
                    ssm_log_dt=ssm_log_dt[li], ssm_b_re=ssm_b_re[li], ssm_b_im=ssm_b_im[li], ssm_c_re=ssm_c_re[li],
                    ssm_c_im=ssm_c_im[li], ssm_d=ssm_d[li].reshape(1, -1),
                    b_glu=b_glu[li].reshape(1, -1), norm_post=norm_post[li].reshape(1, -1))

    act = x[0]
    saved, params = [], []
    gathered = all_gather("gather_weights", layer_shard(0))
    for li in range(depth):
        params.append(layer_params(li, gathered))
        act, sv, gathered = layer_fwd(act, params[li], li, layer_shard(li + 1) if li + 1 < depth else None)
        saved.append(sv)
    loss_part, dy = loss_head("loss_head", act, loss_target[0])
    grads = [None] * depth
    for li in reversed(range(depth)):
        dy, grads[li] = layer_bwd(dy, params[li], saved[li], li)
    grad_x = dy[None]
    loss = lax.psum(loss_part[0, 0], ("x", "y", "c"))

    def stack(name):
        return jnp.stack([grads[li][name] for li in range(depth)])

    g_w_all = stack("w_all")
    g_conv = stack("conv_w").astype(bf16)
    g_glu = stack("w_glu").reshape(depth, N_DEV, wr, D_MODEL)
    g_out = stack("w_out").reshape(depth, N_DEV, wr, D_MODEL)
    repl_shapes = {n: w[n].shape for n in REPL}
    repl_rows = _repl_rows(repl_shapes)
    g_repl = _repl_pack({n: stack(n).reshape(w[n].shape) for n in REPL}, repl_shapes).astype(bf16)
    rr = repl_rows // N_DEV
    n_main, n_sq = depth * D_MODEL, depth * wr
    gmisc_shapes = [(depth, D_MODEL, tail), (depth, CONV_K, cc)]
    gmisc_rows = _flat_rows([_sds(sh, bf16) for sh in gmisc_shapes], 16)
    used = n_main + 2 * n_sq + gmisc_rows + rr
    blocks = jnp.stack([jnp.concatenate([
        _ref_cols(g_w_all, dv * wc, dv * wc + PACK_W).reshape(n_main, PACK_W), g_glu[:, dv].reshape(n_sq, PACK_W),
        g_out[:, dv].reshape(n_sq, PACK_W),
        _pack_flat([_ref_cols(g_w_all, dv * wc + PACK_W, (dv + 1) * wc), g_conv[:, :, dv * cc:(dv + 1) * cc]],
                   gmisc_rows),
        g_repl[dv * rr:(dv + 1) * rr], jnp.zeros((-used % 16, PACK_W), bf16)]) for dv in range(N_DEV)])
    blocks = blocks.reshape(4, 2, blocks.shape[1], PACK_W)
    from_sibling = pair_exchange("pair_grads", blocks)
    own = lax.dynamic_index_in_dim(blocks, lax.axis_index("c"), axis=1, keepdims=False)
    chip_part = pair_sum("pair_sum_grads", own, from_sibling)
    slots = chip_exchange("scatter_grads", chip_part)
    mine = sum_slots("sum_grads", slots)
    o1, o2, o3, o4 = n_main, n_main + n_sq, n_main + 2 * n_sq, n_main + 2 * n_sq + gmisc_rows
    gs_tail, gs_conv = _unpack(mine[o3:o4].reshape(-1), gmisc_shapes)
    gs_w_in = jnp.concatenate([mine[:o1].reshape(depth, D_MODEL, PACK_W), gs_tail], axis=2)
    gs_glu = mine[o1:o2].reshape(depth, wr, D_MODEL)
    gs_out = mine[o2:o3].reshape(depth, wr, D_MODEL)
    g_repl_full = all_gather("gather_repl_grads", mine[o4:o4 + rr]).reshape(repl_rows, PACK_W)
    g = _repl_unpack(g_repl_full, repl_shapes)
    g.update(w_in=gs_w_in, w_glu=gs_glu, w_out=gs_out, conv_w=gs_conv)

    delta, new_m, new_v = {}, {}, {}
    for n in SHARDED:
        delta[n], new_m[n], new_v[n] = adamw("adamw_" + n, w[n], g[n], m[n], v[n])
    outs = adamw("adamw_repl", *[_repl_pack({n: src[n] for n in REPL}, repl_shapes) for src in (w, g, m, v)])
    for dst, o in zip((delta, new_m, new_v), outs):
        dst.update(_repl_unpack(o, repl_shapes))
    return (loss, grad_x, *[g[n] for n in ALL_W], *[delta[n] for n in ALL_W], *[new_m[n] for n in ALL_W],
            *[new_v[n] for n in ALL_W])
```

```python
import math

import jax
import jax.numpy as jnp
from jax import lax
from jax.experimental import pallas as pl
from jax.experimental.pallas import tpu as pltpu

f32 = jnp.float32
bf16 = jnp.bfloat16

D_MODEL = 1024
DEPTH = 4
N_DEV = 8
DN_HEADS = 8
HEAD_DIM = 128
CHUNK = 64
CONV_K = 4
SSM_GROUPS = 64
SSM_GROUP = 16
SSM_STATE = 64
SUPER = 8
N_SUPER = SSM_GROUPS // SUPER
SG_STATE = SUPER * SSM_STATE
EPS = 1e-6
W_COLS = 8208
W_PAD = 8448
ADAM_LR, ADAM_B1, ADAM_B2, ADAM_EPS, ADAM_WD, ADAM_STEP = 0.001, 0.9, 0.999, 1e-08, 0.01, 10
VMEM_LIMIT = 56 * 1024 * 1024
MESH = pl.DeviceIdType.MESH
HIGH = lax.Precision.HIGH
DELTA_HB = 8


def _call(body, name, out_shape, grid=None, in_specs=None, out_specs=None, scratch=(), **kw):
    args = dict(out_shape=out_shape, name=name, scratch_shapes=list(scratch),
                compiler_params=pltpu.CompilerParams(vmem_limit_bytes=VMEM_LIMIT, **kw))
    if grid is not None:
        args.update(grid=grid, in_specs=in_specs, out_specs=out_specs)
    else:
        if in_specs is not None:
            args.update(in_specs=in_specs)
        if out_specs is not None:
            args.update(out_specs=out_specs)
    return pl.pallas_call(body, **args)


def _sds(shape, dtype=f32):
    return jax.ShapeDtypeStruct(tuple(shape), dtype)


def _sigmoid(x):
    return 1.0 / (1.0 + jnp.exp(-x))


def _silu(x):
    return x * _sigmoid(x)


def _dsilu(x):
    s = _sigmoid(x)
    return s * (1.0 + x * (1.0 - s))


_GELU_C = math.sqrt(2.0 / math.pi)


def _gelu(x):
    return 0.5 * x * (1.0 + jnp.tanh(_GELU_C * (x + 0.044715 * x * x * x)))


def _dgelu(x):
    t = jnp.tanh(_GELU_C * (x + 0.044715 * x * x * x))
    return 0.5 * (1.0 + t) + 0.5 * x * (1.0 - t * t) * _GELU_C * (1.0 + 3 * 0.044715 * x * x)


def _softplus(x):
    return jnp.maximum(x, 0.0) + jnp.log(1.0 + jnp.exp(-jnp.abs(x)))


def _bdot(a, b, dn):
    return lax.dot_general(a.astype(bf16), b.astype(bf16), (dn, ((), ())), preferred_element_type=f32)


def _matmul(name, a, b, *, dn, grid, a_spec, b_spec, o_spec, o_shape, o_dtype=f32):
    nk = grid[-1]

    def body(a_ref, b_ref, o_ref, acc_ref):
        p = _bdot(a_ref[...], b_ref[...], dn)
        if nk == 1:
            o_ref[...] = p.astype(o_dtype)
        else:
            k = pl.program_id(len(grid) - 1)

            @pl.when(k == 0)
            def _():
                acc_ref[...] = p

            @pl.when(k > 0)
            def _():
                acc_ref[...] += p

            @pl.when(k == nk - 1)
            def _():
                o_ref[...] = acc_ref[...].astype(o_dtype)

    blk = tuple(d for d in o_spec.block_shape if d is not None)
    return _call(body, name, _sds(o_shape, o_dtype), grid, [a_spec, b_spec], o_spec,
                 scratch=[pltpu.VMEM(blk if nk > 1 else (8, 128), f32)])(a, b)


def _tile(n, pref):
    for t in pref:
        if n % t == 0:
            return t
    return n


def mm_nn(name, a, b):
    m, k = a.shape
    n = b.shape[1]
    tm, tn, tk = _tile(m, (512, 256)), _tile(n, (2816, 1024, 512)), _tile(k, (2816, 1024))
    return _matmul(name, a, b, dn=((1,), (0,)), grid=(m // tm, n // tn, k // tk),
                   a_spec=pl.BlockSpec((tm, tk), lambda i, j, l: (i, l)),
                   b_spec=pl.BlockSpec((tk, tn), lambda i, j, l: (l, j)),
                   o_spec=pl.BlockSpec((tm, tn), lambda i, j, l: (i, j)), o_shape=(m, n))


def mm_nt(name, a, b):
    m, k = a.shape
    n = b.shape[0]
    tm, tn, tk = _tile(m, (512, 256)), _tile(n, (1024, 512)), _tile(k, (2816, 1024))
    return _matmul(name, a, b, dn=((1,), (1,)), grid=(m // tm, n // tn, k // tk),
                   a_spec=pl.BlockSpec((tm, tk), lambda i, j, l: (i, l)),
                   b_spec=pl.BlockSpec((tn, tk), lambda i, j, l: (j, l)),
                   o_spec=pl.BlockSpec((tm, tn), lambda i, j, l: (i, j)), o_shape=(m, n))


def mm_tn(name, a, b, o_dtype=f32):
    k, m = a.shape
    n = b.shape[1]
    tm, tn, tk = _tile(m, (512,)), _tile(n, (2816, 1024, 512)), _tile(k, (512, 256))
    return _matmul(name, a, b, dn=((0,), (0,)), grid=(m // tm, n // tn, k // tk),
                   a_spec=pl.BlockSpec((tk, tm), lambda i, j, l: (l, i)),
                   b_spec=pl.BlockSpec((tk, tn), lambda i, j, l: (l, j)),
                   o_spec=pl.BlockSpec((tm, tn), lambda i, j, l: (i, j)), o_shape=(m, n), o_dtype=o_dtype)


def _rows(t, light=False):
    return _tile(t, (512, 256) if light else (256,))


def _row_spec(tm, w):
    return pl.BlockSpec((tm, w), lambda i: (i, 0))


def _acc_spec(r, w):
    return pl.BlockSpec((r, w), lambda i: (0, 0))


def _accumulate(ref, val):
    @pl.when(pl.program_id(0) == 0)
    def _():
        ref[...] = val

    @pl.when(pl.program_id(0) > 0)
    def _():
        ref[...] += val


def rmsnorm_fwd(name, x, gain):
    t, d = x.shape
    tm = _rows(t, light=True)

    def body(x_ref, g_ref, o_ref):
        xv = x_ref[...]
        r = lax.rsqrt(jnp.mean(xv * xv, axis=-1, keepdims=True) + EPS)
        o_ref[...] = (xv * r * g_ref[...]).astype(bf16)

    return _call(body, name, _sds((t, d), bf16), (t // tm,), [_row_spec(tm, d), _acc_spec(1, d)],
                 _row_spec(tm, d))(x, gain)


def rmsnorm_bwd(name, x, gain, dn, dres):
    t, d = x.shape
    tm = _rows(t, light=True)

    def body(x_ref, g_ref, dn_ref, dr_ref, dx_ref, dg_ref):
        xv = x_ref[...]
        r = lax.rsqrt(jnp.mean(xv * xv, axis=-1, keepdims=True) + EPS)
        n = xv * r
        dnv = dn_ref[...]
        _accumulate(dg_ref, jnp.sum(dnv * n, axis=0, keepdims=True))
        dng = dnv * g_ref[...]
        dx_ref[...] = dr_ref[...] + r * (dng - n * jnp.mean(dng * n, axis=-1, keepdims=True))

    return _call(body, name, (_sds((t, d)), _sds((1, d))), (t // tm,),
                 [_row_spec(tm, d), _acc_spec(1, d), _row_spec(tm, d), _row_spec(tm, d)],
                 (_row_spec(tm, d), _acc_spec(1, d)))(x, gain, dn, dres)


def residual_norm_fwd(name, x, out, gain):
    t, d = x.shape
    tm = _rows(t, light=True)

    def body(x_ref, o_ref, g_ref, y_ref):
        ov = o_ref[...]
        r = lax.rsqrt(jnp.mean(ov * ov, axis=-1, keepdims=True) + EPS)
        y_ref[...] = x_ref[...] + ov * r * g_ref[...]

    return _call(body, name, _sds((t, d)), (t // tm,), [_row_spec(tm, d), _row_spec(tm, d), _acc_spec(1, d)],
                 _row_spec(tm, d))(x, out, gain)


def post_norm_bwd(name, out, gain, dy):
    t, d = out.shape
    tm = _rows(t, light=True)

    def body(o_ref, g_ref, dy_ref, do_ref, dg_ref):
        ov = o_ref[...]
        r = lax.rsqrt(jnp.mean(ov * ov, axis=-1, keepdims=True) + EPS)
        n = ov * r
        dyv = dy_ref[...]
        _accumulate(dg_ref, jnp.sum(dyv * n, axis=0, keepdims=True))
        dng = dyv * g_ref[...]
        do_ref[...] = (r * (dng - n * jnp.mean(dng * n, axis=-1, keepdims=True))).astype(bf16)

    return _call(body, name, (_sds((t, d), bf16), _sds((1, d))), (t // tm,),
                 [_row_spec(tm, d), _acc_spec(1, d), _row_spec(tm, d)],
                 (_row_spec(tm, d), _acc_spec(1, d)))(out, gain, dy)


def loss_head(name, y, target):
    t, d = y.shape
    tm = _rows(t, light=True)

    def body(y_ref, t_ref, l_ref, dy_ref):
        e = y_ref[...] - t_ref[...]
        dy_ref[...] = e * (1.0 / d)
        s = jnp.sum(jnp.sum(e * e, axis=1, keepdims=True), axis=0, keepdims=True) * (0.5 / d)
        _accumulate(l_ref, s)

    return _call(body, name, (_sds((1, 1)), _sds((t, d))), (t // tm,),
                 [_row_spec(tm, d), _row_spec(tm, d)], (_acc_spec(1, 1), _row_spec(tm, d)))(y, target)


def _prev_spec(tm, w):
    return pl.BlockSpec((8, w), lambda i: (jnp.maximum(i * (tm // 8) - 1, 0), 0))


def _next_spec(tm, w, t):
    return pl.BlockSpec((8, w), lambda i: (jnp.minimum((i + 1) * (tm // 8), t // 8 - 1), 0))


def _fill_pad(pad_ref, prev_ref, cur_ref, tm):
    keep = (pl.program_id(0) > 0).astype(f32)
    pad_ref[0:8, :] = prev_ref[...] * keep
    pad_ref[8:8 + tm, :] = cur_ref[...]


def _conv_block(pad_ref, w_ref, cb, tm):
    cs = slice(cb * 128, (cb + 1) * 128)
    acc = pad_ref[pl.ds(8 - (CONV_K - 1), tm), cs] * w_ref[0:1, cs]
    for j in range(1, CONV_K):
        acc = acc + pad_ref[pl.ds(8 - (CONV_K - 1) + j, tm), cs] * w_ref[j:j + 1, cs]
    return acc


def conv_qkv_fwd(name, proj, conv_w, a_log, dt_bias):
    t = proj.shape[0]
    tm = _rows(t)
    scale = HEAD_DIM ** -0.5

    def body(cur_ref, prev_ref, w_ref, bd_ref, al_ref, db_ref, q_ref, k_ref, v_ref, bg_ref, pad_ref):
        _fill_pad(pad_ref, prev_ref, cur_ref, tm)
        for cb in range(3 * DN_HEADS):
            s = _silu(_conv_block(pad_ref, w_ref, cb, tm))
            hs = slice((cb % DN_HEADS) * 128, (cb % DN_HEADS + 1) * 128)
            if cb < DN_HEADS:
                q_ref[:, hs] = s * (lax.rsqrt(jnp.sum(s * s, axis=-1, keepdims=True) + EPS) * scale)
            elif cb < 2 * DN_HEADS:
                k_ref[:, hs] = s * lax.rsqrt(jnp.sum(s * s, axis=-1, keepdims=True) + EPS)
            else:
                v_ref[:, hs] = s
        bd = bd_ref[...]
        beta = _sigmoid(bd)
        g = -jnp.exp(al_ref[...]) * _softplus(bd + db_ref[...])
        lane = lax.broadcasted_iota(jnp.int32, bd.shape, 1)
        bg_ref[...] = jnp.where(lane < DN_HEADS, beta, jnp.where(lane < 2 * DN_HEADS, g, 0.0))

    w3 = 3 * D_MODEL
    return _call(body, name, (_sds((t, D_MODEL)),) * 3 + (_sds((t, 128)),), (t // tm,),
                 [pl.BlockSpec((tm, w3), lambda i: (i, 0)), _prev_spec(tm, w3), _acc_spec(CONV_K, w3),
                  pl.BlockSpec((tm, 128), lambda i: (i, 8192 // 128)), _acc_spec(1, 128), _acc_spec(1, 128)],
                 (_row_spec(tm, D_MODEL),) * 3 + (_row_spec(tm, 128),),
                 scratch=[pltpu.VMEM((tm + 8, w3), f32)])(proj, proj, conv_w, proj, a_log, dt_bias)


def conv_qkv_bwd(name, proj, conv_w, a_log, dt_bias, dq, dk, dv, dbg):
    t = proj.shape[0]
    tm = _rows(t)
    scale = HEAD_DIM ** -0.5

    def body(cur_ref, prev_ref, w_ref, bd_ref, al_ref, db_ref, dq_ref, dk_ref, dv_ref, dbg_ref,
             dc_ref, dw_ref, dbd_ref, dab_ref, pad_ref):
        _fill_pad(pad_ref, prev_ref, cur_ref, tm)

        @pl.when(pl.program_id(0) == 0)
        def _():
            dw_ref[...] = jnp.zeros_like(dw_ref)

        for cb in range(3 * DN_HEADS):
            cs = slice(cb * 128, (cb + 1) * 128)
            hs = slice((cb % DN_HEADS) * 128, (cb % DN_HEADS + 1) * 128)
            c = _conv_block(pad_ref, w_ref, cb, tm)
            s = _silu(c)
            if cb < 2 * DN_HEADS:
                dn = (dq_ref[:, hs] * scale) if cb < DN_HEADS else dk_ref[:, hs]
                r = lax.rsqrt(jnp.sum(s * s, axis=-1, keepdims=True) + EPS)
                ds = r * dn - s * (r * r * r) * jnp.sum(dn * s, axis=-1, keepdims=True)
            else:
                ds = dv_ref[:, hs]
            dc = ds * _dsilu(c)
            dc_ref[:, cs] = dc
            for j in range(CONV_K):
                dw_ref[j:j + 1, cs] += jnp.sum(dc * pad_ref[pl.ds(8 - (CONV_K - 1) + j, tm), cs], axis=0,
                                               keepdims=True)
        bd = bd_ref[...]
        dbg_v = dbg_ref[...]
        lane = lax.broadcasted_iota(jnp.int32, bd.shape, 1)
        sg = _sigmoid(bd)
        ea = jnp.exp(al_ref[...])
        z = bd + db_ref[...]
        sp = _softplus(z)
        is_b = lane < DN_HEADS
        is_g = jnp.logical_and(lane >= DN_HEADS, lane < 2 * DN_HEADS)
        d_z = jnp.where(is_g, dbg_v * (-ea) * _sigmoid(z), 0.0)
        dbd_ref[...] = jnp.where(is_b, dbg_v * sg * (1.0 - sg), d_z)
        d_al = jnp.sum(jnp.where(is_g, dbg_v * (-ea) * sp, 0.0), axis=0, keepdims=True)
        d_db = jnp.sum(d_z, axis=0, keepdims=True)
        _accumulate(dab_ref, jnp.concatenate([d_al, d_db] + [jnp.zeros_like(d_al)] * 6, axis=0))

    w3 = 3 * D_MODEL
    return _call(body, name, (_sds((t, w3)), _sds((CONV_K, w3)), _sds((t, 128)), _sds((8, 128))), (t // tm,),
                 [pl.BlockSpec((tm, w3), lambda i: (i, 0)), _prev_spec(tm, w3), _acc_spec(CONV_K, w3),
                  pl.BlockSpec((tm, 128), lambda i: (i, 8192 // 128)), _acc_spec(1, 128), _acc_spec(1, 128),
                  _row_spec(tm, D_MODEL), _row_spec(tm, D_MODEL), _row_spec(tm, D_MODEL), _row_spec(tm, 128)],
                 (_row_spec(tm, w3), _acc_spec(CONV_K, w3), _row_spec(tm, 128), _acc_spec(8, 128)),
                 scratch=[pltpu.VMEM((tm + 8, w3), f32)])(proj, proj, conv_w, proj, a_log, dt_bias, dq, dk, dv, dbg)


def conv_transpose(name, dc, conv_w):
    t, w3 = dc.shape
    tm = _rows(t)
    nt = t // tm

    def body(cur_ref, nxt_ref, w_ref, o_ref, pad_ref):
        keep = (pl.program_id(0) < nt - 1).astype(f32)
        pad_ref[0:tm, :] = cur_ref[...]
        pad_ref[tm:tm + 8, :] = nxt_ref[...] * keep
        for cb in range(w3 // 128):
            cs = slice(cb * 128, (cb + 1) * 128)
            acc = pad_ref[pl.ds(CONV_K - 1, tm), cs] * w_ref[0:1, cs]
            for j in range(1, CONV_K):
                acc = acc + pad_ref[pl.ds(CONV_K - 1 - j, tm), cs] * w_ref[j:j + 1, cs]
            o_ref[:, cs] = acc.astype(bf16)

    return _call(body, name, _sds((t, w3), bf16), (nt,),
                 [_row_spec(tm, w3), _next_spec(tm, w3, t), _acc_spec(CONV_K, w3)], _row_spec(tm, w3),
                 scratch=[pltpu.VMEM((tm + 8, w3), f32)])(dc, dc, conv_w)


def _bdg(a, b, ca, cb, prec=None):
    if prec is None:
        a, b = a.astype(bf16), b.astype(bf16)
    return lax.dot_general(a, b, (((ca,), (cb,)), ((0,), (0,))), precision=prec, preferred_element_type=f32)


def _bnn(a, b, prec=None):
    return _bdg(a, b, 2, 1, prec)


def _bnt(a, b, prec=None):
    return _bdg(a, b, 2, 2, prec)


def _btn(a, b, prec=None):
    return _bdg(a, b, 1, 1, prec)


def _delta_local(q, k, v, g_row, b_row, solved=None):
    c = CHUNK
    ii = lax.broadcasted_iota(jnp.int32, (c, c), 0)
    jj = lax.broadcasted_iota(jnp.int32, (c, c), 1)
    eye, lower, strict = ii == jj, ii >= jj, ii > jj
    shp = (q.shape[0], c, c)
    g_b = jnp.broadcast_to(g_row, shp)
    gc_col = jnp.sum(jnp.where(lower, g_b, 0.0), axis=2, keepdims=True)
    gc_row = jnp.sum(jnp.where(eye, jnp.broadcast_to(gc_col, shp), 0.0), axis=1, keepdims=True)
    b_col = jnp.sum(jnp.where(eye, jnp.broadcast_to(b_row, shp), 0.0), axis=2, keepdims=True)
    gl = jnp.sum(g_row, axis=2, keepdims=True)
    decay = jnp.exp(jnp.where(lower, gc_col - gc_row, -1e30))
    kb = k * b_col
    kk = _bnt(kb, k)
    lmat = jnp.where(strict, kk * decay, 0.0)
    egc = jnp.exp(gc_col)
    rhs_w = kb * egc
    if solved is None:
        tinv = eye.astype(f32) - lmat
        pw = lmat
        for _ in range(5):
            pw = _bnn(pw, pw, HIGH)
            tinv = tinv + _bnn(tinv, pw, HIGH)
        u = _bnn(tinv, v * b_col, HIGH)
        w = _bnn(tinv, rhs_w, HIGH)
    else:
        tinv, u, w = solved
    qk = _bnt(q, k)
    amat = jnp.where(lower, qk * decay, 0.0)
    q_dec = q * egc
    kdf = jnp.exp(gl - gc_col)
    k_dec = k * kdf
    return dict(eye=eye, lower=lower, strict=strict, gc_col=gc_col, b_col=b_col, gl=gl, decay=decay,
                kb=kb, kk=kk, tinv=tinv, egc=egc, rhs_w=rhs_w, u=u, w=w, qk=qk, amat=amat, q_dec=q_dec, kdf=kdf,
                k_dec=k_dec)


def _heads(ref, hb):
    return jnp.stack([ref[:, h * HEAD_DIM:(h + 1) * HEAD_DIM] for h in range(hb)])


def _put_heads(ref, val):
    for h in range(val.shape[0]):
        ref[:, h * HEAD_DIM:(h + 1) * HEAD_DIM] = val[h]


def delta_fwd(name, q, k, v, g_rows, b_rows, next_shard=None):
    t = q.shape[0]
    nc = t // CHUNK
    hb = DELTA_HB
    fused = next_shard is not None
    assert not fused or (hb == DN_HEADS and nc >= 2)

    def body(*refs):
        if fused:
            (q_ref, k_ref, v_ref, g_ref, b_ref, x_ref, o_ref, s_ref, t_ref, u_ref, w_ref, gath_ref, state, send_sems,
             recv_sems, local_sem) = refs
        else:
            q_ref, k_ref, v_ref, g_ref, b_ref, o_ref, s_ref, t_ref, u_ref, w_ref, state = refs
        n = pl.program_id(1)
        if fused:
            start, forward, finish = _gather_steps(x_ref, gath_ref, send_sems, recv_sems, local_sem)
            pl.when(n == 0)(start)

        @pl.when(n == 0)
        def _():
            state[...] = jnp.zeros_like(state)

        loc = _delta_local(_heads(q_ref, hb), _heads(k_ref, hb), _heads(v_ref, hb), g_ref[:, pl.ds(n, 1), :],
                           b_ref[:, pl.ds(n, 1), :])
        s0 = state[...]
        s_ref[...] = s0
        t_ref[...] = loc["tinv"]
        _put_heads(u_ref, loc["u"])
        _put_heads(w_ref, loc["w"])
        v_new = loc["u"] - _bnn(loc["w"], s0)
        _put_heads(o_ref, _bnn(loc["q_dec"], s0) + _bnn(loc["amat"], v_new))
        state[...] = s0 * jnp.exp(loc["gl"]) + _btn(loc["k_dec"], v_new)
        if fused:
            pl.when(n == (3 * nc) // 4)(forward)
            pl.when(n == nc - 1)(finish)

    tok = pl.BlockSpec((CHUNK, hb * HEAD_DIM), lambda h, n: (n, h))
    row = pl.BlockSpec((hb, nc, CHUNK), lambda h, n: (h, 0, 0))
    any_spec = pl.BlockSpec(memory_space=pl.ANY)
    outs = (_sds((t, D_MODEL)), _sds((DN_HEADS, nc, HEAD_DIM, HEAD_DIM)), _sds((DN_HEADS, nc, CHUNK, CHUNK)),
            _sds((t, D_MODEL)), _sds((t, D_MODEL)))
    out_specs = (tok, pl.BlockSpec((hb, None, HEAD_DIM, HEAD_DIM), lambda h, n: (h, n, 0, 0)),
                 pl.BlockSpec((hb, None, CHUNK, CHUNK), lambda h, n: (h, n, 0, 0)), tok, tok)
    in_specs, args = [tok, tok, tok, row, row], (q, k, v, g_rows, b_rows)
    scratch = [pltpu.VMEM((hb, HEAD_DIM, HEAD_DIM), f32)]
    if fused:
        outs += (_sds((N_DEV,) + next_shard.shape, next_shard.dtype),)
        out_specs += (any_spec,)
        in_specs, args = in_specs + [any_spec], args + (next_shard,)
        scratch = scratch + list(_GATHER_SEMS)
    return _call(body, name, outs, (DN_HEADS // hb, nc), in_specs, out_specs, scratch=scratch)(*args)


def delta_bwd(name, q, k, v, g_rows, b_rows, s_all, t_all, u_all, w_all, do):
    t = q.shape[0]
    nc = t // CHUNK
    c = CHUNK
    hb = DELTA_HB

    def body(q_ref, k_ref, v_ref, g_ref, b_ref, s_ref, t_ref, u_ref, w_ref, do_ref, dq_ref, dk_ref, dv_ref, dg_ref,
             db_ref, dstate):
        step = pl.program_id(1)
        n = nc - 1 - step

        @pl.when(step == 0)
        def _():
            dstate[...] = jnp.zeros_like(dstate)

        qv, kv, vv = _heads(q_ref, hb), _heads(k_ref, hb), _heads(v_ref, hb)
        L = _delta_local(qv, kv, vv, g_ref[:, pl.ds(n, 1), :], b_ref[:, pl.ds(n, 1), :],
                         solved=(t_ref[...], _heads(u_ref, hb), _heads(w_ref, hb)))
        eye, lower, strict = L["eye"], L["lower"], L["strict"]
        shp = (hb, c, c)
        s0 = s_ref[...]
        dov = _heads(do_ref, hb)
        ds = dstate[...]
        eg = jnp.exp(L["gl"])
        v_new = L["u"] - _bnn(L["w"], s0)
        d_k_dec = _bnt(v_new, ds)
        d_v_new = _bnn(L["k_dec"], ds) + _btn(L["amat"], dov)
        d_eg = jnp.sum(jnp.sum(ds * s0, axis=2, keepdims=True), axis=1, keepdims=True)
        d_q_dec = _bnt(dov, s0)
        d_a = _bnt(dov, v_new)
        d_w = -_bnt(d_v_new, s0)
        dstate[...] = ds * eg + _btn(L["q_dec"], dov) - _btn(L["w"], d_v_new)
        d_am = jnp.where(lower, d_a * L["decay"], 0.0)
        dq = _bnn(d_am, kv) + d_q_dec * L["egc"]
        dk = _btn(d_am, qv) + d_k_dec * L["kdf"]
        e_col = jnp.sum(d_k_dec * L["k_dec"], axis=2, keepdims=True)
        d_gc_col = jnp.sum(d_q_dec * L["q_dec"], axis=2, keepdims=True) - e_col
        d_gl = jnp.sum(e_col, axis=1, keepdims=True) + d_eg * eg
        tinv = L["tinv"]
        d_rhs_u = _btn(tinv, d_v_new, HIGH)
        d_rhs_w = _btn(tinv, d_w, HIGH)
        d_l = -(_bnt(d_rhs_u, L["u"], HIGH) + _bnt(d_rhs_w, L["w"], HIGH))
        _put_heads(dv_ref, d_rhs_u * L["b_col"])
        d_b_col = jnp.sum(d_rhs_u * vv, axis=2, keepdims=True)
        d_gc_col = d_gc_col + jnp.sum(d_rhs_w * L["rhs_w"], axis=2, keepdims=True)
        d_lm = jnp.where(strict, d_l * L["decay"], 0.0)
        d_kb = d_rhs_w * L["egc"] + _bnn(d_lm, kv)
        dk = dk + _btn(d_lm, L["kb"]) + d_kb * L["b_col"]
        d_b_col = d_b_col + jnp.sum(d_kb * kv, axis=2, keepdims=True)
        m = d_am * L["qk"] + d_lm * L["kk"]
        d_gc_col = d_gc_col + jnp.sum(m, axis=2, keepdims=True)
        d_gc_row = (jnp.sum(jnp.where(eye, jnp.broadcast_to(d_gc_col, shp), 0.0), axis=1, keepdims=True)
                    - jnp.sum(m, axis=1, keepdims=True))
        lane = lax.broadcasted_iota(jnp.int32, (1, 1, c), 2)
        d_gc_row = d_gc_row + jnp.where(lane == c - 1, d_gl, 0.0)
        d_gc_tot = jnp.sum(jnp.where(eye, jnp.broadcast_to(d_gc_row, shp), 0.0), axis=2, keepdims=True)
        dg_ref[:, pl.ds(n, 1), :] = jnp.sum(jnp.where(lower, jnp.broadcast_to(d_gc_tot, shp), 0.0), axis=1,
                                            keepdims=True)
        db_ref[:, pl.ds(n, 1), :] = jnp.sum(jnp.where(eye, jnp.broadcast_to(d_b_col, shp), 0.0), axis=1,
                                            keepdims=True)
        _put_heads(dq_ref, dq)
        _put_heads(dk_ref, dk)

    tok = pl.BlockSpec((CHUNK, hb * HEAD_DIM), lambda h, s: (nc - 1 - s, h))
    row = pl.BlockSpec((hb, nc, CHUNK), lambda h, s: (h, 0, 0))
    return _call(body, name, (_sds((t, D_MODEL)),) * 3 + (_sds((DN_HEADS, nc, CHUNK)),) * 2, (DN_HEADS // hb, nc),
                 [tok, tok, tok, row, row,
                  pl.BlockSpec((hb, None, HEAD_DIM, HEAD_DIM), lambda h, s: (h, nc - 1 - s, 0, 0)),
                  pl.BlockSpec((hb, None, CHUNK, CHUNK), lambda h, s: (h, nc - 1 - s, 0, 0)), tok, tok, tok],
                 (tok, tok, tok, row, row),
                 scratch=[pltpu.VMEM((hb, HEAD_DIM, HEAD_DIM), f32)])(q, k, v, g_rows, b_rows, s_all, t_all, u_all, w_all,
                                                                      do)


SEG = 8


def _perm_rows(a):
    t, c = a.shape
    return a.reshape(SEG, t // SEG, c).transpose(1, 0, 2).reshape(t, c)


def _unperm_rows(a):
    t, c = a.shape
    return a.reshape(t // SEG, SEG, c).transpose(1, 0, 2).reshape(t, c)


def _cmul(ar, ai, br, bi):
    return ar * br - ai * bi, ar * bi + ai * br


def _segment_init(er, ei, lr, li, seg_len, reverse):
    w = er.shape[1]
    sub = lax.broadcasted_iota(jnp.int32, (SEG, w), 0)

    def shift(x, k):
        if reverse:
            return jnp.where(sub < SEG - k, pltpu.roll(x, SEG - k, 0), 0.0)
        return jnp.where(sub >= k, pltpu.roll(x, k, 0), 0.0)

    pr, pi = lr, li
    for _ in range(seg_len.bit_length() - 1):
        pr, pi = _cmul(pr, pi, pr, pi)
    fr, fi = shift(er, 1), shift(ei, 1)
    for k in (1, 2, 4):
        sr, si = shift(fr, k), shift(fi, k)
        mr, mi = _cmul(pr, pi, sr, si)
        fr, fi = fr + mr, fi + mi
        pr, pi = _cmul(pr, pi, pr, pi)
    return fr, fi


def s5_fwd(name, u_perm, bbd, cbd, lam):
    t = u_perm.shape[0]
    tt = _tile(t, (1024, 512, 256, 128))
    nt, ng, w = t // tt, tt // SEG, SG_STATE
    seg_len = t // SEG
    assert seg_len & (seg_len - 1) == 0 and tt % SEG == 0

    def body(u_ref, b_ref, c_ref, lam_ref, y_ref, h_ref, x_scr, state):
        p, i = pl.program_id(1), pl.program_id(2)
        lr1, li1 = lam_ref[:, 0:w], lam_ref[:, w:2 * w]
        lr, li = jnp.broadcast_to(lr1, (SEG, w)), jnp.broadcast_to(li1, (SEG, w))
        x_scr[...] = _bdot(u_ref[...], b_ref[...], ((1,), (0,)))

        @pl.when(jnp.logical_and(p == 0, i == 0))
        def _():
            state[...] = jnp.zeros_like(state)

        @pl.when(jnp.logical_and(p == 1, i == 0))
        def _():
            sr, si = _segment_init(state[:, 0:w], state[:, w:2 * w], lr1, li1, seg_len, False)
            state[:, 0:w] = sr
            state[:, w:2 * w] = si

        def run(store):
            def step(g, st):
                row = pl.multiple_of(g * SEG, SEG)
                xg = x_scr[pl.ds(row, SEG), :]
                nr = lr * st[0] - li * st[1] + xg[:, 0:w]
                ni = lr * st[1] + li * st[0] + xg[:, w:2 * w]
                if store:
                    h_ref[pl.ds(row, SEG), 0:w] = nr
                    h_ref[pl.ds(row, SEG), w:2 * w] = ni
                return nr, ni

            fin = lax.fori_loop(0, ng, step, (state[:, 0:w], state[:, w:2 * w]))
            state[:, 0:w] = fin[0]
            state[:, w:2 * w] = fin[1]

        @pl.when(p == 0)
        def _():
            run(False)

        @pl.when(p == 1)
        def _():
            run(True)
            y_ref[...] = _bdot(h_ref[...], c_ref[...], ((1,), (0,)))

    return _call(body, name, (_sds((t, D_MODEL)), _sds((t, N_SUPER * 2 * w))), (N_SUPER, 2, nt),
                 [pl.BlockSpec((tt, 128), lambda s, p, i: (i, s)),
                  pl.BlockSpec((None, 128, 2 * w), lambda s, p, i: (s, 0, 0)),
                  pl.BlockSpec((None, 2 * w, 128), lambda s, p, i: (s, 0, 0)),
                  pl.BlockSpec((None, 1, 2 * w), lambda s, p, i: (s, 0, 0))],
                 (pl.BlockSpec((tt, 128), lambda s, p, i: (i * p, s)),
                  pl.BlockSpec((tt, 2 * w), lambda s, p, i: (i * p, s))),
                 scratch=[pltpu.VMEM((tt, 2 * w), f32), pltpu.VMEM((SEG, 2 * w), f32)])(u_perm, bbd, cbd, lam)


def s5_bwd(name, dy_perm, u_perm, h_perm, hprev0, bbd, cbd, lam):
    t = u_perm.shape[0]
    tt = _tile(t, (1024, 512, 256, 128))
    nt, ng, w = t // tt, tt // SEG, SG_STATE
    seg_len = t // SEG

    def body(dy_ref, u_ref, h_ref, hp_ref, hp0_ref, b_ref, c_ref, lam_ref, du_ref, db_ref, dc_ref, dl_ref,
             g_scr, state, dl_acc):
        p, i = pl.program_id(1), pl.program_id(2)
        first_tile = jnp.logical_or(p == 0, i == nt - 1)
        lr1, li1 = lam_ref[:, 0:w], -lam_ref[:, w:2 * w]
        lr, li = jnp.broadcast_to(lr1, (SEG, w)), jnp.broadcast_to(li1, (SEG, w))
        g_scr[...] = _bdot(dy_ref[...], c_ref[...], ((1,), (1,)))

        @pl.when(jnp.logical_and(p == 0, i == 0))
        def _():
            state[...] = jnp.zeros_like(state)

        @pl.when(jnp.logical_and(p == 1, i == 0))
        def _():
            sr, si = _segment_init(state[:, 0:w], state[:, w:2 * w], lr1, li1, seg_len, True)
            state[:, 0:w] = sr
            state[:, w:2 * w] = si
            dl_acc[...] = jnp.zeros_like(dl_acc)

        def adj(g, st):
            row = pl.multiple_of(g * SEG, SEG)
            gg = g_scr[pl.ds(row, SEG), :]
            nr = lr * st[0] - li * st[1] + gg[:, 0:w]
            ni = lr * st[1] + li * st[0] + gg[:, w:2 * w]
            return row, nr, ni

        @pl.when(p == 0)
        def _():
            def step(k, st):
                _, nr, ni = adj(ng - 1 - k, st)
                return nr, ni

            fin = lax.fori_loop(0, ng, step, (state[:, 0:w], state[:, w:2 * w]))
            state[:, 0:w] = fin[0]
            state[:, w:2 * w] = fin[1]

        @pl.when(p == 1)
        def _():
            above = jnp.where(first_tile, hp0_ref[...], hp_ref[...])

            def step(k, st):
                g = ng - 1 - k
                row, nr, ni = adj(g, st)
                g_scr[pl.ds(row, SEG), 0:w] = nr
                g_scr[pl.ds(row, SEG), w:2 * w] = ni
                prow = pl.multiple_of(jnp.maximum(g - 1, 0) * SEG, SEG)
                hp = jnp.where(g > 0, h_ref[pl.ds(prow, SEG), :], above)
                pr, pi = hp[:, 0:w], hp[:, w:2 * w]
                return nr, ni, st[2] + nr * pr + ni * pi, st[3] + ni * pr - nr * pi

            fin = lax.fori_loop(0, ng, step, (state[:, 0:w], state[:, w:2 * w], dl_acc[:, 0:w], dl_acc[:, w:2 * w]))
            state[:, 0:w] = fin[0]
            state[:, w:2 * w] = fin[1]
            dl_acc[:, 0:w] = fin[2]
            dl_acc[:, w:2 * w] = fin[3]
            a = g_scr[...]
            du_ref[...] = _bdot(a, b_ref[...], ((1,), (1,)))
            d_b = _bdot(u_ref[...], a, ((0,), (0,)))
            d_c = _bdot(dy_ref[...], h_ref[...], ((0,), (0,)))

            @pl.when(i == 0)
            def _():
                db_ref[...] = d_b
                dc_ref[...] = d_c

            @pl.when(i > 0)
            def _():
                db_ref[...] += d_b
                dc_ref[...] += d_c

            @pl.when(i == nt - 1)
            def _():
                dl_ref[...] = jnp.sum(dl_acc[...], axis=0, keepdims=True)

    tile = lambda s, p, i: (nt - 1 - i, s)
    tile1 = lambda s, p, i: (nt - 1 - i * p, s)
    above = lambda s, p, i: (jnp.maximum((nt - 1 - i * p) * (tt // SEG) - 1, 0), s)
    per_s = lambda s, p, i: (s, 0, 0)
    return _call(body, name, (_sds((t, D_MODEL)), _sds((N_SUPER, 128, 2 * w)), _sds((N_SUPER, 128, 2 * w)),
                              _sds((N_SUPER, 1, 2 * w))), (N_SUPER, 2, nt),
                 [pl.BlockSpec((tt, 128), tile), pl.BlockSpec((tt, 128), tile1), pl.BlockSpec((tt, 2 * w), tile1),
                  pl.BlockSpec((SEG, 2 * w), above), pl.BlockSpec((SEG, 2 * w), lambda s, p, i: (0, s)),
                  pl.BlockSpec((None, 128, 2 * w), per_s), pl.BlockSpec((None, 2 * w, 128), per_s),
                  pl.BlockSpec((None, 1, 2 * w), per_s)],
                 (pl.BlockSpec((tt, 128), tile1), pl.BlockSpec((None, 128, 2 * w), per_s),
                  pl.BlockSpec((None, 128, 2 * w), per_s), pl.BlockSpec((None, 1, 2 * w), per_s)),
                 scratch=[pltpu.VMEM((tt, 2 * w), f32), pltpu.VMEM((SEG, 2 * w), f32),
                          pltpu.VMEM((SEG, 2 * w), f32)])(dy_perm, u_perm, h_perm, h_perm, hprev0, bbd, cbd, lam)


def _s5_disc(a_re, a_im, log_dt, br, bi):
    dt = jnp.exp(log_dt)
    mag = jnp.exp(a_re * dt)
    lr, li = mag * jnp.cos(a_im * dt), mag * jnp.sin(a_im * dt)
    den = a_re * a_re + a_im * a_im
    fr = ((lr - 1.0) * a_re + li * a_im) / den
    fi = (li * a_re - (lr - 1.0) * a_im) / den
    return lr, li, fr * br - fi * bi, fr * bi + fi * br


def s5_disc_fwd(name, a_re, a_im, log_dt, br, bi):
    g, n = SSM_GROUPS, SSM_STATE

    def body(ar, ai, ld, brr, bir, lr, li, bbr, bbi):
        o = _s5_disc(ar[...], ai[...], ld[...], brr[...], bir[...])
        lr[...], li[...], bbr[...], bbi[...] = o

    return _call(body, name, (_sds((g, 1, n)), _sds((g, 1, n)), _sds((g, SSM_GROUP, n)), _sds((g, SSM_GROUP, n))))(
        a_re, a_im, log_dt, br, bi)


def s5_disc_bwd(name, a_re, a_im, log_dt, br, bi, d_lr, d_li, d_bbr, d_bbi):
    g, n = SSM_GROUPS, SSM_STATE

    def body(ar, ai, ld, brr, bir, c1, c2, c3, c4, o1, o2, o3, o4, o5):
        _, vjp = jax.vjp(_s5_disc, ar[...], ai[...], ld[...], brr[...], bir[...])
        o1[...], o2[...], o3[...], o4[...], o5[...] = vjp((c1[...], c2[...], c3[...], c4[...]))

    return _call(body, name, (_sds((g, 1, n)), _sds((g, 1, n)), _sds((g, 1, 1)), _sds((g, SSM_GROUP, n)),
                              _sds((g, SSM_GROUP, n))))(a_re, a_im, log_dt, br, bi, d_lr, d_li, d_bbr, d_bbi)


def gelu_fwd(name, ys_lin, proj, d_skip):
    t, d = ys_lin.shape
    tm = _rows(t, light=True)

    def body(y_ref, u_ref, d_ref, o_ref):
        o_ref[...] = _gelu(y_ref[...] + d_ref[...] * u_ref[...])

    return _call(body, name, _sds((t, d)), (t // tm,),
                 [_row_spec(tm, d), pl.BlockSpec((tm, d), lambda i: (i, 4)), _acc_spec(1, d)],
                 _row_spec(tm, d))(ys_lin, proj, d_skip)


def _head_norm(o, hn):
    outs, ns, rs = [], [], []
    for h in range(DN_HEADS):
        oh = o[:, h * 128:(h + 1) * 128]
        r = lax.rsqrt(jnp.mean(oh * oh, axis=-1, keepdims=True) + EPS)
        n = oh * r
        ns.append(n)
        rs.append(r)
        outs.append(n * hn)
    return outs, ns, rs


def merge_fwd(name, proj, o, yg, glu_lin, head_norm, b_glu):
    t = o.shape[0]
    tm = _rows(t)
    d = D_MODEL

    def body(za_ref, zb_ref, ra_ref, rb_ref, o_ref, yg_ref, gl_ref, hn_ref, bg_ref, m_ref):
        on, _, _ = _head_norm(o_ref[...], hn_ref[...])
        y_a = jnp.concatenate(on, axis=1) * _silu(za_ref[...])
        y_b = yg_ref[...] * _sigmoid(gl_ref[...] + bg_ref[...]) * _silu(zb_ref[...])
        m_ref[...] = (_sigmoid(ra_ref[...]) * y_a + _sigmoid(rb_ref[...]) * y_b).astype(bf16)

    pc = lambda c: pl.BlockSpec((tm, d), lambda i: (i, c))
    return _call(body, name, _sds((t, d), bf16), (t // tm,),
                 [pc(3), pc(5), pc(6), pc(7), _row_spec(tm, d), _row_spec(tm, d), _row_spec(tm, d),
                  _acc_spec(1, 128), _acc_spec(1, d)], _row_spec(tm, d))(
        proj, proj, proj, proj, o, yg, glu_lin, head_norm, b_glu)


def merge_bwd(name, proj, o, yg, glu_lin, head_norm, b_glu, dm):
    t = o.shape[0]
    tm = _rows(t)
    d = D_MODEL

    def body(za_ref, zb_ref, ra_ref, rb_ref, o_ref, yg_ref, gl_ref, hn_ref, bg_ref, dm_ref,
             dza_ref, dzb_ref, dra_ref, drb_ref, do_ref, dgl_ref, dyg_ref, dhn_ref, dbg_ref):
        hn = hn_ref[...]
        za, zb, ra, rb = za_ref[...], zb_ref[...], ra_ref[...], rb_ref[...]
        on, ns, rs = _head_norm(o_ref[...], hn)
        onc = jnp.concatenate(on, axis=1)
        sza = _silu(za)
        y_a = onc * sza
        yg = yg_ref[...]
        sgl = _sigmoid(gl_ref[...] + bg_ref[...])
        y2 = yg * sgl
        szb = _silu(zb)
        y_b = y2 * szb
        sra, srb = _sigmoid(ra), _sigmoid(rb)
        dmv = dm_ref[...]
        dra_ref[...] = (dmv * y_a * sra * (1.0 - sra)).astype(bf16)
        drb_ref[...] = (dmv * y_b * srb * (1.0 - srb)).astype(bf16)
        d_ya = dmv * sra
        d_yb = dmv * srb
        dza_ref[...] = (d_ya * onc * _dsilu(za)).astype(bf16)
        dzb_ref[...] = (d_yb * y2 * _dsilu(zb)).astype(bf16)
        d_on = d_ya * sza
        d_y2 = d_yb * szb
        dyg_ref[...] = d_y2 * sgl
        d_gl = d_y2 * yg * sgl * (1.0 - sgl)
        dgl_ref[...] = d_gl.astype(bf16)
        _accumulate(dbg_ref, jnp.sum(d_gl, axis=0, keepdims=True))
        d_hn = jnp.zeros((1, 128), f32)
        for h in range(DN_HEADS):
            hs = slice(h * 128, (h + 1) * 128)
            dh = d_on[:, hs]
            d_hn = d_hn + jnp.sum(dh * ns[h], axis=0, keepdims=True)
            dn = dh * hn
            do_ref[:, hs] = rs[h] * (dn - ns[h] * jnp.mean(dn * ns[h], axis=-1, keepdims=True))
        _accumulate(dhn_ref, d_hn)

    pc = lambda c: pl.BlockSpec((tm, d), lambda i: (i, c))
    rs_ = _row_spec(tm, d)
    return _call(body, name, (_sds((t, d), bf16),) * 4 + (_sds((t, d)), _sds((t, d), bf16), _sds((t, d)),
                              _sds((1, 128)), _sds((1, d))),
                 (t // tm,), [pc(3), pc(5), pc(6), pc(7), rs_, rs_, rs_, _acc_spec(1, 128), _acc_spec(1, d), rs_],
                 (rs_,) * 7 + (_acc_spec(1, 128), _acc_spec(1, d)))(
        proj, proj, proj, proj, o, yg, glu_lin, head_norm, b_glu, dm)


def gelu_bwd(name, ys_lin, proj, d_skip, dyg_a, dyg_b):
    t, d = ys_lin.shape
    tm = _rows(t, light=True)

    def body(y_ref, u_ref, d_ref, a_ref, b_ref, dys_ref, du_ref, dd_ref):
        uv = u_ref[...]
        dys = (a_ref[...] + b_ref[...]) * _dgelu(y_ref[...] + d_ref[...] * uv)
        dys_ref[...] = dys
        du_ref[...] = dys * d_ref[...]
        _accumulate(dd_ref, jnp.sum(dys * uv, axis=0, keepdims=True))

    rs_ = _row_spec(tm, d)
    return _call(body, name, (_sds((t, d)), _sds((t, d)), _sds((1, d))), (t // tm,),
                 [rs_, pl.BlockSpec((tm, d), lambda i: (i, 4)), _acc_spec(1, d), rs_, rs_],
                 (rs_, rs_, _acc_spec(1, d)))(ys_lin, proj, d_skip, dyg_a, dyg_b)


def assemble_dproj(name, dqkv, dza, du_a, du_b, dzb, dra, drb, dbd):
    t = dza.shape[0]
    tm = _rows(t)
    d = D_MODEL

    def body(qkv_ref, za_ref, ua_ref, ub_ref, zb_ref, ra_ref, rb_ref, bd_ref, o_ref):
        o_ref[:, 0:3 * d] = qkv_ref[...].astype(bf16)
        o_ref[:, 3 * d:4 * d] = za_ref[...].astype(bf16)
        o_ref[:, 4 * d:5 * d] = (ua_ref[...] + ub_ref[...]).astype(bf16)
        o_ref[:, 5 * d:6 * d] = zb_ref[...].astype(bf16)
        o_ref[:, 6 * d:7 * d] = ra_ref[...].astype(bf16)
        o_ref[:, 7 * d:8 * d] = rb_ref[...].astype(bf16)
        o_ref[:, 8 * d:8 * d + 128] = bd_ref[...].astype(bf16)
        o_ref[:, 8 * d + 128:W_PAD] = jnp.zeros((tm, W_PAD - 8 * d - 128), bf16)

    rs_ = _row_spec(tm, d)
    return _call(body, name, _sds((t, W_PAD), bf16), (t // tm,),
                 [_row_spec(tm, 3 * d), rs_, rs_, rs_, rs_, rs_, rs_, _row_spec(tm, 128)], _row_spec(tm, W_PAD))(
        dqkv, dza, du_a, du_b, dzb, dra, drb, dbd)


def adamw(name, w, g, m, v):
    lead, (r, c) = w.shape[:-2], w.shape[-2:]
    tm = _tile(r, (512, 256, 128, 64, 32, 16, 8))
    c1 = 1.0 / (1.0 - ADAM_B1 ** ADAM_STEP)
    c2 = 1.0 / (1.0 - ADAM_B2 ** ADAM_STEP)

    def body(w_ref, g_ref, m_ref, v_ref, d_ref, nm_ref, nv_ref):
        gv = g_ref[...]
        nm = ADAM_B1 * m_ref[...] + (1.0 - ADAM_B1) * gv
        nv = ADAM_B2 * v_ref[...] + (1.0 - ADAM_B2) * (gv * gv)
        d_ref[...] = -ADAM_LR * ((nm * c1) / (jnp.sqrt(nv * c2) + ADAM_EPS) + ADAM_WD * w_ref[...])
        nm_ref[...] = nm
        nv_ref[...] = nv

    if lead:
        sp = pl.BlockSpec((None, tm, c), lambda l, i: (l, i, 0))
        grid = (lead[0], r // tm)
    else:
        sp = pl.BlockSpec((tm, c), lambda i: (i, 0))
        grid = (r // tm,)
    return _call(body, name, (_sds(w.shape),) * 3, grid, [sp] * 4, (sp,) * 3)(w, g, m, v)


def _coords():
    return lax.axis_index("x"), lax.axis_index("y"), lax.axis_index("c")


def _lin(dev):
    return 4 * dev[0] + 2 * dev[1] + dev[2]


def _chips(me):
    x, y, _ = me
    return [(1 - x, y), (x, 1 - y), (1 - x, 1 - y)]


def _gather_steps(x_ref, o_ref, send_sems, recv_sems, local_sem):
    me = _coords()
    x, y, cc = me
    sibling = (x, y, 1 - cc)
    chips = _chips(me)

    def copy(k, block, to, src=None):
        return pltpu.make_async_remote_copy(
            src_ref=o_ref.at[_lin(block)] if src is None else src, dst_ref=o_ref.at[_lin(block)],
            send_sem=send_sems.at[k], recv_sem=recv_sems.at[k], device_id=to, device_id_type=MESH)

    mine = pltpu.make_async_copy(x_ref, o_ref.at[_lin(me)], local_sem)
    first = [copy(0, me, sibling, src=x_ref)] + [copy(1 + j, me, (*chip, cc), src=x_ref)
                                                 for j, chip in enumerate(chips)]
    passed = [copy(4 + j, (*chip, cc), sibling) for j, chip in enumerate(chips)]

    def start():
        mine.start()
        for cp in first:
            cp.start()

    def forward():
        for j, chip in enumerate(chips):
            copy(1 + j, (*chip, cc), me).wait_recv()
            passed[j].start()

    def finish():
        copy(0, sibling, me).wait_recv()
        for j, chip in enumerate(chips):
            copy(4 + j, (*chip, 1 - cc), me).wait_recv()
        for cp in first + passed:
            cp.wait_send()
        mine.wait()

    return start, forward, finish


_GATHER_SEMS = [pltpu.SemaphoreType.DMA((N_DEV - 1,)), pltpu.SemaphoreType.DMA((N_DEV - 1,)),
                pltpu.SemaphoreType.DMA(())]


def all_gather(name, shard):
    r, c = shard.shape

    def body(x_ref, o_ref, send_sems, recv_sems, local_sem):
        for step in _gather_steps(x_ref, o_ref, send_sems, recv_sems, local_sem):
            step()

    any_spec = pl.BlockSpec(memory_space=pl.ANY)
    return _call(body, name, _sds((N_DEV, r, c), shard.dtype), in_specs=[any_spec], out_specs=any_spec,
                 scratch=list(_GATHER_SEMS))(shard)


def pair_exchange(name, blocks):
    _, _, r, c = blocks.shape

    def body(x_ref, o_ref, send_sems, recv_sems):
        x, y, cc = _coords()
        sibling = (x, y, 1 - cc)
        cps = [pltpu.make_async_remote_copy(src_ref=x_ref.at[ch, 1 - cc], dst_ref=o_ref.at[ch], send_sem=send_sems.at[ch],
                                            recv_sem=recv_sems.at[ch], device_id=sibling, device_id_type=MESH)
               for ch in range(4)]
        for cp in cps:
            cp.start()
        for cp in cps:
            cp.wait()

    any_spec = pl.BlockSpec(memory_space=pl.ANY)
    return _call(body, name, _sds((4, r, c), blocks.dtype), in_specs=[any_spec], out_specs=any_spec,
                 scratch=[pltpu.SemaphoreType.DMA((4,)), pltpu.SemaphoreType.DMA((4,))])(blocks)


def chip_exchange(name, blocks):
    _, r, c = blocks.shape

    def body(x_ref, o_ref, send_sems, recv_sems, local_sem):
        me = _coords()
        x, y, cc = me
        my_chip = 2 * x + y
        mine = pltpu.make_async_copy(x_ref.at[my_chip], o_ref.at[my_chip], local_sem)
        mine.start()
        cps = []
        for j, (px, py) in enumerate(_chips(me)):
            cp = pltpu.make_async_remote_copy(src_ref=x_ref.at[2 * px + py], dst_ref=o_ref.at[my_chip],
                                              send_sem=send_sems.at[j], recv_sem=recv_sems.at[j],
                                              device_id=(px, py, cc), device_id_type=MESH)
            cp.start()
            cps.append(cp)
        for j, (px, py) in enumerate(_chips(me)):
            pltpu.make_async_remote_copy(src_ref=x_ref.at[my_chip], dst_ref=o_ref.at[2 * px + py],
                                         send_sem=send_sems.at[j], recv_sem=recv_sems.at[j],
                                         device_id=(px, py, cc), device_id_type=MESH).wait_recv()
        for cp in cps:
            cp.wait_send()
        mine.wait()

    any_spec = pl.BlockSpec(memory_space=pl.ANY)
    return _call(body, name, _sds(blocks.shape, blocks.dtype), in_specs=[any_spec], out_specs=any_spec,
                 scratch=[pltpu.SemaphoreType.DMA((3,)), pltpu.SemaphoreType.DMA((3,)),
                          pltpu.SemaphoreType.DMA(())])(blocks)


def pair_sum(name, mine, theirs):
    _, r, c = mine.shape
    tm = _tile(r, (240, 256, 128, 64, 32, 16))

    def body(a_ref, b_ref, o_ref):
        o_ref[...] = (a_ref[...].astype(f32) + b_ref[...].astype(f32)).astype(o_ref.dtype)

    sp = pl.BlockSpec((None, tm, c), lambda ch, i: (ch, i, 0))
    return _call(body, name, _sds(mine.shape, mine.dtype), (4, r // tm), [sp, sp], sp)(mine, theirs)


def sum_slots(name, slots):
    n, r, c = slots.shape
    tm = _tile(r, (240, 256, 128, 64, 32, 16))

    def body(s_ref, o_ref):
        acc = s_ref[0].astype(f32)
        for d in range(1, n):
            acc = acc + s_ref[d].astype(f32)
        o_ref[...] = acc

    return _call(body, name, _sds((r, c)), (r // tm,), [pl.BlockSpec((n, tm, c), lambda i: (0, i, 0))],
                 pl.BlockSpec((tm, c), lambda i: (i, 0)))(slots)


def _rows_layout(col8, t):
    return col8.T.reshape(DN_HEADS, t // CHUNK, CHUNK)


def _blockdiag(m):
    g, a, b = m.shape
    m = m.reshape(N_SUPER, SUPER, a, b)
    out = jnp.einsum("sgab,gh->sgahb", m, jnp.eye(SUPER, dtype=m.dtype))
    return out.reshape(N_SUPER, SUPER * a, SUPER * b)


def _diag_blocks(m, a, b):
    m = m.reshape(N_SUPER, SUPER, a, SUPER, b)
    return jnp.einsum("sgahb,gh->sgab", m, jnp.eye(SUPER, dtype=m.dtype)).reshape(SSM_GROUPS, a, b)


def _s5_params(p, li):
    tag = f"l{li}"
    n = SSM_STATE
    a_re = p["ssm_a_re"].reshape(SSM_GROUPS, 1, n)
    a_im = p["ssm_a_im"].reshape(SSM_GROUPS, 1, n)
    log_dt = p["ssm_log_dt"].reshape(SSM_GROUPS, 1, 1)
    br = jnp.swapaxes(p["ssm_b_re"], 1, 2)
    bi = jnp.swapaxes(p["ssm_b_im"], 1, 2)
    lr, li_, bbr, bbi = s5_disc_fwd("s5_disc_" + tag, a_re, a_im, log_dt, br, bi)
    lam = jnp.concatenate([lr.reshape(N_SUPER, 1, SG_STATE), li_.reshape(N_SUPER, 1, SG_STATE)], axis=-1)
    bbd = jnp.concatenate([_blockdiag(bbr), _blockdiag(bbi)], axis=-1).astype(bf16)
    c_re = jnp.swapaxes(p["ssm_c_re"], 1, 2)
    c_im = jnp.swapaxes(p["ssm_c_im"], 1, 2)
    cbd = jnp.concatenate([_blockdiag(c_re), -_blockdiag(c_im)], axis=1).astype(bf16)
    return dict(a_re=a_re, a_im=a_im, log_dt=log_dt, br=br, bi=bi, lam=lam, bbd=bbd, cbd=cbd)


def layer_fwd(x, p, li, next_shard=None):
    tag = f"l{li}"
    t = x.shape[0]
    d = D_MODEL
    h = rmsnorm_fwd("norm_pre_" + tag, x, p["norm_pre"])
    proj = mm_nn("proj_" + tag, h, p["w_all"])
    q, k, v, bg = conv_qkv_fwd("conv_" + tag, proj, p["conv_w"], p["a_log"], p["dt_bias"])
    b_rows = _rows_layout(bg[:, 0:DN_HEADS], t)
    g_rows = _rows_layout(bg[:, DN_HEADS:2 * DN_HEADS], t)
    o, s_all, t_all, u_all, w_sol, *gathered = delta_fwd("delta_" + tag, q, k, v, g_rows, b_rows, next_shard)
    sp = _s5_params(p, li)
    u_perm = _perm_rows(proj[:, 4 * d:5 * d])
    ys_perm, hs = s5_fwd("s5_" + tag, u_perm, sp["bbd"], sp["cbd"], sp["lam"])
    ys_lin = _unperm_rows(ys_perm)
    yg = gelu_fwd("gelu_" + tag, ys_lin, proj, p["ssm_d"])
    glu_lin = mm_nn("glu_" + tag, yg, p["w_glu"])
    merged = merge_fwd("merge_" + tag, proj, o, yg, glu_lin, p["head_norm"], p["b_glu"])
    out = mm_nn("out_" + tag, merged, p["w_out"])
    y = residual_norm_fwd("norm_post_" + tag, x, out, p["norm_post"])
    saved = dict(x=x, h=h, proj=proj, q=q, k=k, v=v, g_rows=g_rows, b_rows=b_rows, o=o, s_all=s_all, t_all=t_all, u_all=u_all, w_sol=w_sol, sp=sp, u_perm=u_perm,
                 hs=hs, ys_lin=ys_lin, yg=yg, glu_lin=glu_lin, merged=merged, out=out)
    return y, saved, (gathered[0] if gathered else None)


def layer_bwd(dy, p, s, li):
    tag = f"l{li}"
    t = dy.shape[0]
    d = D_MODEL
    sp = s["sp"]
    gr = {}
    d_out, gr["norm_post"] = post_norm_bwd("norm_post_b_" + tag, s["out"], p["norm_post"], dy)
    d_merged = mm_nt("out_b_" + tag, d_out, p["w_out"])
    gr["w_out"] = mm_tn("out_w_" + tag, s["merged"], d_out, bf16)
    (dza, dzb, dra, drb, d_o, d_glu, dyg_a, gr["head_norm"], gr["b_glu"]) = merge_bwd(
        "merge_b_" + tag, s["proj"], s["o"], s["yg"], s["glu_lin"], p["head_norm"], p["b_glu"], d_merged)
    dyg_b = mm_nt("glu_b_" + tag, d_glu, p["w_glu"])
    gr["w_glu"] = mm_tn("glu_w_" + tag, s["yg"], d_glu, bf16)
    d_ys, du_a, gr["ssm_d"] = gelu_bwd("gelu_b_" + tag, s["ys_lin"], s["proj"], p["ssm_d"], dyg_a, dyg_b)
    hprev0 = jnp.concatenate([jnp.zeros((1, s["hs"].shape[1]), f32), s["hs"][-SEG:-1]], axis=0)
    du_perm, d_bbd, d_cbd, d_lam = s5_bwd("s5_b_" + tag, _perm_rows(d_ys), s["u_perm"], s["hs"], hprev0, sp["bbd"],
                                          sp["cbd"], sp["lam"])
    du_b = _unperm_rows(du_perm)
    gr["ssm_c_re"] = _diag_blocks(d_cbd[:, :, 0:SG_STATE], SSM_GROUP, SSM_STATE)
    gr["ssm_c_im"] = -_diag_blocks(d_cbd[:, :, SG_STATE:], SSM_GROUP, SSM_STATE)
    d_bbr = _diag_blocks(d_bbd[:, :, 0:SG_STATE], SSM_GROUP, SSM_STATE)
    d_bbi = _diag_blocks(d_bbd[:, :, SG_STATE:], SSM_GROUP, SSM_STATE)
    d_lr = d_lam[:, :, 0:SG_STATE].reshape(SSM_GROUPS, 1, SSM_STATE)
    d_li = d_lam[:, :, SG_STATE:].reshape(SSM_GROUPS, 1, SSM_STATE)
    d_are, d_aim, d_ldt, d_br, d_bi = s5_disc_bwd("s5_disc_b_" + tag, sp["a_re"], sp["a_im"], sp["log_dt"], sp["br"],
                                                  sp["bi"], d_lr, d_li, d_bbr, d_bbi)
    gr["ssm_a_re"] = d_are.reshape(SSM_GROUPS, SSM_STATE)
    gr["ssm_a_im"] = d_aim.reshape(SSM_GROUPS, SSM_STATE)
    gr["ssm_log_dt"] = d_ldt.reshape(SSM_GROUPS)
    gr["ssm_b_re"] = jnp.swapaxes(d_br, 1, 2)
    gr["ssm_b_im"] = jnp.swapaxes(d_bi, 1, 2)
    dq, dk, dv, dg_rows, db_rows = delta_bwd("delta_b_" + tag, s["q"], s["k"], s["v"], s["g_rows"], s["b_rows"],
                                             s["s_all"], s["t_all"], s["u_all"], s["w_sol"], d_o)
    dbg = jnp.concatenate([db_rows.reshape(DN_HEADS, t).T, dg_rows.reshape(DN_HEADS, t).T,
                           jnp.zeros((t, 128 - 2 * DN_HEADS), f32)], axis=1)
    dc, gr["conv_w"], dbd, dab = conv_qkv_bwd("conv_b_" + tag, s["proj"], p["conv_w"], p["a_log"], p["dt_bias"],
                                              dq, dk, dv, dbg)
    gr["a_log"] = dab[0, DN_HEADS:2 * DN_HEADS]
    gr["dt_bias"] = dab[1, DN_HEADS:2 * DN_HEADS]
    dqkv = conv_transpose("conv_t_" + tag, dc, p["conv_w"])
    dproj = assemble_dproj("dproj_" + tag, dqkv, dza, du_a, du_b, dzb, dra, drb, dbd)
    d_h = mm_nt("proj_b_" + tag, dproj, p["w_all"])
    gr["w_all"] = mm_tn("proj_w_" + tag, s["h"], dproj, bf16)
    dx, gr["norm_pre"] = rmsnorm_bwd("norm_pre_b_" + tag, s["x"], p["norm_pre"], d_h, dy)
    return dx, gr


REPL = ["norm_pre", "a_log", "dt_bias", "head_norm", "ssm_a_re", "ssm_a_im", "ssm_log_dt", "ssm_b_re", "ssm_b_im",
        "ssm_c_re", "ssm_c_im", "ssm_d", "b_glu", "norm_post"]
SHARDED = ["w_in", "conv_w", "w_glu", "w_out"]
ALL_W = ["norm_pre", "w_in", "conv_w", "a_log", "dt_bias", "head_norm", "ssm_a_re", "ssm_a_im", "ssm_log_dt",
         "ssm_b_re", "ssm_b_im", "ssm_c_re", "ssm_c_im", "ssm_d", "w_glu", "b_glu", "w_out", "norm_post"]
PACK_W = 1024


def _pack_flat(arrs, rows):
    flat = jnp.concatenate([a.reshape(-1) for a in arrs])
    return jnp.pad(flat, (0, rows * PACK_W - flat.shape[0])).reshape(rows, PACK_W)


def _flat_rows(arrs, mult=8):
    n = sum(math.prod(a.shape) for a in arrs)
    rows = -(-n // PACK_W)
    return -(-rows // mult) * mult


def _unpack(flat, shapes):
    out, off = [], 0
    for sh in shapes:
        n = math.prod(sh)
        out.append(flat[off:off + n].reshape(sh))
        off += n
    return out


def _repl_split(shapes):
    big = [n for n in REPL if math.prod(shapes[n]) % PACK_W == 0]
    small = [n for n in REPL if n not in big]
    return big, small


def _rows8(n):
    return -(-n // (8 * PACK_W)) * 8


def _repl_rows(shapes):
    big, small = _repl_split(shapes)
    rows = sum(_rows8(math.prod(shapes[n])) for n in big) + _flat_rows([_sds(shapes[n]) for n in small], 8)
    return -(-rows // (8 * N_DEV)) * (8 * N_DEV)


def _repl_pack(arrs, shapes):
    big, small = _repl_split(shapes)
    parts = []
    for n in big:
        a = arrs[n].reshape(-1, PACK_W)
        parts.append(jnp.pad(a, ((0, _rows8(a.size) - a.shape[0]), (0, 0))))
    parts.append(_pack_flat([arrs[n] for n in small], _flat_rows([_sds(shapes[n]) for n in small], 8)))
    used = sum(p.shape[0] for p in parts)
    parts.append(jnp.zeros((_repl_rows(shapes) - used, PACK_W), parts[0].dtype))
    return jnp.concatenate(parts, axis=0)


def _repl_unpack(packed, shapes):
    big, small = _repl_split(shapes)
    out, off = {}, 0
    for n in big:
        size = math.prod(shapes[n])
        out[n] = packed[off:off + size // PACK_W].reshape(shapes[n])
        off += _rows8(size)
    srows = _flat_rows([_sds(shapes[n]) for n in small], 8)
    out.update(zip(small, _unpack(packed[off:off + srows].reshape(-1), [shapes[n] for n in small])))
    return out


_COL_RUNS = ((0, 4096), (4112, W_COLS), (4096, 4112))


def _w_all(main, tails, wc):
    pieces = []
    for lo, hi in _COL_RUNS:
        for dv in range(N_DEV):
            a, b = max(lo, dv * wc) - dv * wc, min(hi, (dv + 1) * wc) - dv * wc
            if a < min(b, PACK_W):
                pieces.append(main[dv][:, :, a:min(b, PACK_W)])
            if b > max(a, PACK_W):
                pieces.append(tails[dv][:, :, max(a, PACK_W) - PACK_W:b - PACK_W])
    ll, rows = main[0].shape[0], main[0].shape[1]
    pieces.append(jnp.zeros((ll, rows, W_PAD - W_COLS), main[0].dtype))
    return jnp.concatenate(pieces, axis=2)


def _ref_cols(g, lo, hi):
    pieces, off = [], 0
    for a, b in _COL_RUNS:
        s, e = max(lo, a), min(hi, b)
        if s < e:
            pieces.append((s, g[..., off + s - a:off + e - a]))
        off += b - a
    pieces.sort(key=lambda t: t[0])
    return jnp.concatenate([p for _, p in pieces], axis=-1) if len(pieces) > 1 else pieces[0][1]


def kernel(x, norm_pre, w_in, conv_w, a_log, dt_bias, head_norm, ssm_a_re, ssm_a_im, ssm_log_dt, ssm_b_re, ssm_b_im, ssm_c_re, ssm_c_im, ssm_d, w_glu, b_glu, w_out, norm_post, loss_target, m_norm_pre, m_w_in, m_conv_w, m_a_log, m_dt_bias, m_head_norm, m_ssm_a_re, m_ssm_a_im, m_ssm_log_dt, m_ssm_b_re, m_ssm_b_im, m_ssm_c_re, m_ssm_c_im, m_ssm_d, m_w_glu, m_b_glu, m_w_out, m_norm_post, v_norm_pre, v_w_in, v_conv_w, v_a_log, v_dt_bias, v_head_norm, v_ssm_a_re, v_ssm_a_im, v_ssm_log_dt, v_ssm_b_re, v_ssm_b_im, v_ssm_c_re, v_ssm_c_im, v_ssm_d, v_w_glu, v_b_glu, v_w_out, v_norm_post):
    loc = dict(locals())
    w = {n: loc[n] for n in ALL_W}
    m = {n: loc["m_" + n] for n in ALL_W}
    v = {n: loc["v_" + n] for n in ALL_W}
    depth = w_in.shape[0]
    wc = w_in.shape[2]
    cc = conv_w.shape[2]
    wr = w_glu.shape[1]

    tail = wc - PACK_W
    conv_hi = conv_w.astype(bf16)
    conv_mid = (conv_w - conv_hi.astype(f32)).astype(bf16)
    conv_lo = (conv_w - conv_hi.astype(f32) - conv_mid.astype(f32)).astype(bf16)
    conv3 = jnp.stack([conv_hi, conv_mid, conv_lo], axis=1)
    w_in_b = w_in.astype(bf16)
    misc_shapes = [(D_MODEL, tail), (3, CONV_K, cc)]
    misc_rows = _flat_rows([_sds(sh, bf16) for sh in misc_shapes], 16)

    def layer_shard(li):
        return jnp.concatenate([w_in_b[li, :, :PACK_W], w_glu[li].astype(bf16), w_out[li].astype(bf16),
                                _pack_flat([w_in_b[li, :, PACK_W:], conv3[li]], misc_rows)])

    def layer_weights(gathered):
        miscs = [_unpack(gathered[dv, D_MODEL + 2 * wr:].reshape(-1), misc_shapes) for dv in range(N_DEV)]
        w_all = _w_all([gathered[dv, :D_MODEL][None] for dv in range(N_DEV)], [mi[0][None] for mi in miscs], wc)[0]
        glu = gathered[:, D_MODEL:D_MODEL + wr].reshape(N_DEV * wr, D_MODEL)
        out = gathered[:, D_MODEL + wr:D_MODEL + 2 * wr].reshape(N_DEV * wr, D_MODEL)
        conv = jnp.concatenate([mi[1][0].astype(f32) + mi[1][1].astype(f32) + mi[1][2].astype(f32) for mi in miscs],
                               axis=1)
        return dict(w_all=w_all, w_glu=glu, w_out=out, conv_w=conv)

    def layer_params(li, gathered):
        return dict(layer_weights(gathered), norm_pre=norm_pre[li].reshape(1, -1),
                    a_log=jnp.pad(a_log[li], (DN_HEADS, 128 - 2 * DN_HEADS)).reshape(1, 128),
                    dt_bias=jnp.pad(dt_bias[li], (DN_HEADS, 128 - 2 * DN_HEADS)).reshape(1, 128),
                    head_norm=head_norm[li].reshape(1, -1), ssm_a_re=ssm_a_re[li], ssm_a_im=ssm_a_im[li],
                    ssm_log_dt=ssm_log_dt[li], ssm_b_re=ssm_b_re[li], ssm_b_im=ssm_b_im[li], ssm_c_re=ssm_c_re[li],
                    ssm_c_im=ssm_c_im[li], ssm_d=ssm_d[li].reshape(1, -1),
                    b_glu=b_glu[li].reshape(1, -1), norm_post=norm_post[li].reshape(1, -1))

    act = x[0]
    saved, params = [], []
    gathered = all_gather("gather_weights", layer_shard(0))
    for li in range(depth):
        params.append(layer_params(li, gathered))
        act, sv, gathered = layer_fwd(act, params[li], li, layer_shard(li + 1) if li + 1 < depth else None)
        saved.append(sv)
    loss_part, dy = loss_head("loss_head", act, loss_target[0])
    grads = [None] * depth
    for li in reversed(range(depth)):
        dy, grads[li] = layer_bwd(dy, params[li], saved[li], li)
    grad_x = dy[None]
    loss = lax.psum(loss_part[0, 0], ("x", "y", "c"))

    def stack(name):
        return jnp.stack([grads[li][name] for li in range(depth)])

    g_conv = stack("conv_w").astype(bf16)

    def per_layer(fn):
        return [fn(grads[li]) for li in range(depth)]
    repl_shapes = {n: w[n].shape for n in REPL}
    repl_rows = _repl_rows(repl_shapes)
    g_repl = _repl_pack({n: stack(n).reshape(w[n].shape) for n in REPL}, repl_shapes).astype(bf16)
    rr = repl_rows // N_DEV
    n_main, n_sq = depth * D_MODEL, depth * wr
    gmisc_shapes = [(depth, D_MODEL, tail), (depth, CONV_K, cc)]
    gmisc_rows = _flat_rows([_sds(sh, bf16) for sh in gmisc_shapes], 16)
    used = n_main + 2 * n_sq + gmisc_rows + rr
    blocks = jnp.stack([jnp.concatenate([
        *per_layer(lambda gl: _ref_cols(gl["w_all"], dv * wc, dv * wc + PACK_W)),
        *per_layer(lambda gl: gl["w_glu"][dv * wr:(dv + 1) * wr]), *per_layer(lambda gl: gl["w_out"][dv * wr:(dv + 1) * wr]),
        _pack_flat([jnp.stack(per_layer(lambda gl: _ref_cols(gl["w_all"], dv * wc + PACK_W, (dv + 1) * wc))),
                    g_conv[:, :, dv * cc:(dv + 1) * cc]], gmisc_rows),
        g_repl[dv * rr:(dv + 1) * rr], jnp.zeros((-used % 16, PACK_W), bf16)]) for dv in range(N_DEV)])
    blocks = blocks.reshape(4, 2, blocks.shape[1], PACK_W)
    from_sibling = pair_exchange("pair_grads", blocks)
    own = lax.dynamic_index_in_dim(blocks, lax.axis_index("c"), axis=1, keepdims=False)
    chip_part = pair_sum("pair_sum_grads", own, from_sibling)
    slots = chip_exchange("scatter_grads", chip_part)
    mine = sum_slots("sum_grads", slots)
    o1, o2, o3, o4 = n_main, n_main + n_sq, n_main + 2 * n_sq, n_main + 2 * n_sq + gmisc_rows
    gs_tail, gs_conv = _unpack(mine[o3:o4].reshape(-1), gmisc_shapes)
    gs_w_in = jnp.concatenate([mine[:o1].reshape(depth, D_MODEL, PACK_W), gs_tail], axis=2)
    gs_glu = mine[o1:o2].reshape(depth, wr, D_MODEL)
    gs_out = mine[o2:o3].reshape(depth, wr, D_MODEL)
    g_repl_full = all_gather("gather_repl_grads", mine[o4:o4 + rr]).reshape(repl_rows, PACK_W)
    g = _repl_unpack(g_repl_full, repl_shapes)
    g.update(w_in=gs_w_in, w_glu=gs_glu, w_out=gs_out, conv_w=gs_conv)

    delta, new_m, new_v = {}, {}, {}
    for n in SHARDED:
        delta[n], new_m[n], new_v[n] = adamw("adamw_" + n, w[n], g[n], m[n], v[n])
    outs = adamw("adamw_repl", *[_repl_pack({n: src[n] for n in REPL}, repl_shapes) for src in (w, g, m, v)])
    for dst, o in zip((delta, new_m, new_v), outs):
        dst.update(_repl_unpack(o, repl_shapes))
    return (loss, grad_x, *[g[n] for n in ALL_W], *[delta[n] for n in ALL_W], *[new_m[n] for n in ALL_W],
            *[new_v[n] for n in ALL_W])
```

```python
import math

import jax
import jax.numpy as jnp
from jax import lax
from jax.experimental import pallas as pl
from jax.experimental.pallas import tpu as pltpu

f32 = jnp.float32
bf16 = jnp.bfloat16

D_MODEL = 1024
DEPTH = 4
N_DEV = 8
DN_HEADS = 8
HEAD_DIM = 128
CHUNK = 64
CONV_K = 4
SSM_GROUPS = 64
SSM_GROUP = 16
SSM_STATE = 64
SUPER = 8
N_SUPER = SSM_GROUPS // SUPER
SG_STATE = SUPER * SSM_STATE
EPS = 1e-6
W_COLS = 8208
W_PAD = 8448
ADAM_LR, ADAM_B1, ADAM_B2, ADAM_EPS, ADAM_WD, ADAM_STEP = 0.001, 0.9, 0.999, 1e-08, 0.01, 10
VMEM_LIMIT = 56 * 1024 * 1024
MESH = pl.DeviceIdType.MESH
HIGH = lax.Precision.HIGH
DELTA_HB = 8


def _call(body, name, out_shape, grid=None, in_specs=None, out_specs=None, scratch=(), **kw):
    args = dict(out_shape=out_shape, name=name, scratch_shapes=list(scratch),
                compiler_params=pltpu.CompilerParams(vmem_limit_bytes=VMEM_LIMIT, **kw))
    if grid is not None:
        args.update(grid=grid, in_specs=in_specs, out_specs=out_specs)
    else:
        if in_specs is not None:
            args.update(in_specs=in_specs)
        if out_specs is not None:
            args.update(out_specs=out_specs)
    return pl.pallas_call(body, **args)


def _sds(shape, dtype=f32):
    return jax.ShapeDtypeStruct(tuple(shape), dtype)


def _sigmoid(x):
    return 1.0 / (1.0 + jnp.exp(-x))


def _silu(x):
    return x * _sigmoid(x)


def _dsilu(x):
    s = _sigmoid(x)
    return s * (1.0 + x * (1.0 - s))


_GELU_C = math.sqrt(2.0 / math.pi)


def _gelu(x):
    return 0.5 * x * (1.0 + jnp.tanh(_GELU_C * (x + 0.044715 * x * x * x)))


def _dgelu(x):
    t = jnp.tanh(_GELU_C * (x + 0.044715 * x * x * x))
    return 0.5 * (1.0 + t) + 0.5 * x * (1.0 - t * t) * _GELU_C * (1.0 + 3 * 0.044715 * x * x)


def _softplus(x):
    return jnp.maximum(x, 0.0) + jnp.log(1.0 + jnp.exp(-jnp.abs(x)))


def _bdot(a, b, dn):
    return lax.dot_general(a.astype(bf16), b.astype(bf16), (dn, ((), ())), preferred_element_type=f32)


def _matmul(name, a, b, *, dn, grid, a_spec, b_spec, o_spec, o_shape, o_dtype=f32):
    nk = grid[-1]

    def body(a_ref, b_ref, o_ref, acc_ref):
        p = _bdot(a_ref[...], b_ref[...], dn)
        if nk == 1:
            o_ref[...] = p.astype(o_dtype)
        else:
            k = pl.program_id(len(grid) - 1)

            @pl.when(k == 0)
            def _():
                acc_ref[...] = p

            @pl.when(k > 0)
            def _():
                acc_ref[...] += p

            @pl.when(k == nk - 1)
            def _():
                o_ref[...] = acc_ref[...].astype(o_dtype)

    blk = tuple(d for d in o_spec.block_shape if d is not None)
    return _call(body, name, _sds(o_shape, o_dtype), grid, [a_spec, b_spec], o_spec,
                 scratch=[pltpu.VMEM(blk if nk > 1 else (8, 128), f32)])(a, b)


def _tile(n, pref):
    for t in pref:
        if n % t == 0:
            return t
    return n


def mm_nn(name, a, b):
    m, k = a.shape
    n = b.shape[1]
    tm, tn, tk = _tile(m, (512, 256)), _tile(n, (2816, 1024, 512)), _tile(k, (2816, 1024))
    return _matmul(name, a, b, dn=((1,), (0,)), grid=(m // tm, n // tn, k // tk),
                   a_spec=pl.BlockSpec((tm, tk), lambda i, j, l: (i, l)),
                   b_spec=pl.BlockSpec((tk, tn), lambda i, j, l: (l, j)),
                   o_spec=pl.BlockSpec((tm, tn), lambda i, j, l: (i, j)), o_shape=(m, n))


def mm_nt(name, a, b):
    m, k = a.shape
    n = b.shape[0]
    tm, tn, tk = _tile(m, (512, 256)), _tile(n, (1024, 512)), _tile(k, (2816, 1024))
    return _matmul(name, a, b, dn=((1,), (1,)), grid=(m // tm, n // tn, k // tk),
                   a_spec=pl.BlockSpec((tm, tk), lambda i, j, l: (i, l)),
                   b_spec=pl.BlockSpec((tn, tk), lambda i, j, l: (j, l)),
                   o_spec=pl.BlockSpec((tm, tn), lambda i, j, l: (i, j)), o_shape=(m, n))


def mm_tn(name, a, b, o_dtype=f32):
    k, m = a.shape
    n = b.shape[1]
    tm, tn, tk = _tile(m, (512,)), _tile(n, (2816, 1024, 512)), _tile(k, (512, 256))
    return _matmul(name, a, b, dn=((0,), (0,)), grid=(m // tm, n // tn, k // tk),
                   a_spec=pl.BlockSpec((tk, tm), lambda i, j, l: (l, i)),
                   b_spec=pl.BlockSpec((tk, tn), lambda i, j, l: (l, j)),
                   o_spec=pl.BlockSpec((tm, tn), lambda i, j, l: (i, j)), o_shape=(m, n), o_dtype=o_dtype)


def _rows(t, light=False):
    return _tile(t, (512, 256) if light else (256,))


def _row_spec(tm, w):
    return pl.BlockSpec((tm, w), lambda i: (i, 0))


def _acc_spec(r, w):
    return pl.BlockSpec((r, w), lambda i: (0, 0))


def _accumulate(ref, val):
    @pl.when(pl.program_id(0) == 0)
    def _():
        ref[...] = val

    @pl.when(pl.program_id(0) > 0)
    def _():
        ref[...] += val


def rmsnorm_fwd(name, x, gain):
    t, d = x.shape
    tm = _rows(t, light=True)

    def body(x_ref, g_ref, o_ref):
        xv = x_ref[...]
        r = lax.rsqrt(jnp.mean(xv * xv, axis=-1, keepdims=True) + EPS)
        o_ref[...] = (xv * r * g_ref[...]).astype(bf16)

    return _call(body, name, _sds((t, d), bf16), (t // tm,), [_row_spec(tm, d), _acc_spec(1, d)],
                 _row_spec(tm, d))(x, gain)


def rmsnorm_bwd(name, x, gain, dn, dres):
    t, d = x.shape
    tm = _rows(t, light=True)

    def body(x_ref, g_ref, dn_ref, dr_ref, dx_ref, dg_ref):
        xv = x_ref[...]
        r = lax.rsqrt(jnp.mean(xv * xv, axis=-1, keepdims=True) + EPS)
        n = xv * r
        dnv = dn_ref[...]
        _accumulate(dg_ref, jnp.sum(dnv * n, axis=0, keepdims=True))
        dng = dnv * g_ref[...]
        dx_ref[...] = dr_ref[...] + r * (dng - n * jnp.mean(dng * n, axis=-1, keepdims=True))

    return _call(body, name, (_sds((t, d)), _sds((1, d))), (t // tm,),
                 [_row_spec(tm, d), _acc_spec(1, d), _row_spec(tm, d), _row_spec(tm, d)],
                 (_row_spec(tm, d), _acc_spec(1, d)))(x, gain, dn, dres)


def residual_norm_fwd(name, x, out, gain):
    t, d = x.shape
    tm = _rows(t, light=True)

    def body(x_ref, o_ref, g_ref, y_ref):
        ov = o_ref[...]
        r = lax.rsqrt(jnp.mean(ov * ov, axis=-1, keepdims=True) + EPS)
        y_ref[...] = x_ref[...] + ov * r * g_ref[...]

    return _call(body, name, _sds((t, d)), (t // tm,), [_row_spec(tm, d), _row_spec(tm, d), _acc_spec(1, d)],
                 _row_spec(tm, d))(x, out, gain)


def post_norm_bwd(name, out, gain, dy):
    t, d = out.shape
    tm = _rows(t, light=True)

    def body(o_ref, g_ref, dy_ref, do_ref, dg_ref):
        ov = o_ref[...]
        r = lax.rsqrt(jnp.mean(ov * ov, axis=-1, keepdims=True) + EPS)
        n = ov * r
        dyv = dy_ref[...]
        _accumulate(dg_ref, jnp.sum(dyv * n, axis=0, keepdims=True))
        dng = dyv * g_ref[...]
        do_ref[...] = (r * (dng - n * jnp.mean(dng * n, axis=-1, keepdims=True))).astype(bf16)

    return _call(body, name, (_sds((t, d), bf16), _sds((1, d))), (t // tm,),
                 [_row_spec(tm, d), _acc_spec(1, d), _row_spec(tm, d)],
                 (_row_spec(tm, d), _acc_spec(1, d)))(out, gain, dy)


def loss_head(name, y, target):
    t, d = y.shape
    tm = _rows(t, light=True)

    def body(y_ref, t_ref, l_ref, dy_ref):
        e = y_ref[...] - t_ref[...]
        dy_ref[...] = e * (1.0 / d)
        s = jnp.sum(jnp.sum(e * e, axis=1, keepdims=True), axis=0, keepdims=True) * (0.5 / d)
        _accumulate(l_ref, s)

    return _call(body, name, (_sds((1, 1)), _sds((t, d))), (t // tm,),
                 [_row_spec(tm, d), _row_spec(tm, d)], (_acc_spec(1, 1), _row_spec(tm, d)))(y, target)


def _prev_spec(tm, w):
    return pl.BlockSpec((8, w), lambda i: (jnp.maximum(i * (tm // 8) - 1, 0), 0))


def _next_spec(tm, w, t):
    return pl.BlockSpec((8, w), lambda i: (jnp.minimum((i + 1) * (tm // 8), t // 8 - 1), 0))


def _fill_pad(pad_ref, prev_ref, cur_ref, tm):
    keep = (pl.program_id(0) > 0).astype(f32)
    pad_ref[0:8, :] = prev_ref[...] * keep
    pad_ref[8:8 + tm, :] = cur_ref[...]


def _conv_block(pad_ref, w_ref, cb, tm):
    cs = slice(cb * 128, (cb + 1) * 128)
    acc = pad_ref[pl.ds(8 - (CONV_K - 1), tm), cs] * w_ref[0:1, cs]
    for j in range(1, CONV_K):
        acc = acc + pad_ref[pl.ds(8 - (CONV_K - 1) + j, tm), cs] * w_ref[j:j + 1, cs]
    return acc


def conv_qkv_fwd(name, proj, conv_w, a_log, dt_bias):
    t = proj.shape[0]
    tm = _rows(t)
    scale = HEAD_DIM ** -0.5

    def body(cur_ref, prev_ref, w_ref, bd_ref, al_ref, db_ref, q_ref, k_ref, v_ref, bg_ref, pad_ref):
        _fill_pad(pad_ref, prev_ref, cur_ref, tm)
        for cb in range(3 * DN_HEADS):
            s = _silu(_conv_block(pad_ref, w_ref, cb, tm))
            hs = slice((cb % DN_HEADS) * 128, (cb % DN_HEADS + 1) * 128)
            if cb < DN_HEADS:
                q_ref[:, hs] = s * (lax.rsqrt(jnp.sum(s * s, axis=-1, keepdims=True) + EPS) * scale)
            elif cb < 2 * DN_HEADS:
                k_ref[:, hs] = s * lax.rsqrt(jnp.sum(s * s, axis=-1, keepdims=True) + EPS)
            else:
                v_ref[:, hs] = s
        bd = bd_ref[...]
        beta = _sigmoid(bd)
        g = -jnp.exp(al_ref[...]) * _softplus(bd + db_ref[...])
        lane = lax.broadcasted_iota(jnp.int32, bd.shape, 1)
        bg_ref[...] = jnp.where(lane < DN_HEADS, beta, jnp.where(lane < 2 * DN_HEADS, g, 0.0))

    w3 = 3 * D_MODEL
    return _call(body, name, (_sds((t, D_MODEL)),) * 3 + (_sds((t, 128)),), (t // tm,),
                 [pl.BlockSpec((tm, w3), lambda i: (i, 0)), _prev_spec(tm, w3), _acc_spec(CONV_K, w3),
                  pl.BlockSpec((tm, 128), lambda i: (i, 8192 // 128)), _acc_spec(1, 128), _acc_spec(1, 128)],
                 (_row_spec(tm, D_MODEL),) * 3 + (_row_spec(tm, 128),),
                 scratch=[pltpu.VMEM((tm + 8, w3), f32)])(proj, proj, conv_w, proj, a_log, dt_bias)


def conv_qkv_bwd(name, proj, conv_w, a_log, dt_bias, dq, dk, dv, dbg):
    t = proj.shape[0]
    tm = _rows(t)
    scale = HEAD_DIM ** -0.5

    def body(cur_ref, prev_ref, w_ref, bd_ref, al_ref, db_ref, dq_ref, dk_ref, dv_ref, dbg_ref,
             dc_ref, dw_ref, dbd_ref, dab_ref, pad_ref):
        _fill_pad(pad_ref, prev_ref, cur_ref, tm)

        @pl.when(pl.program_id(0) == 0)
        def _():
            dw_ref[...] = jnp.zeros_like(dw_ref)

        for cb in range(3 * DN_HEADS):
            cs = slice(cb * 128, (cb + 1) * 128)
            hs = slice((cb % DN_HEADS) * 128, (cb % DN_HEADS + 1) * 128)
            taps = [pad_ref[pl.ds(8 - (CONV_K - 1) + j, tm), cs] for j in range(CONV_K)]
            c = taps[0] * w_ref[0:1, cs]
            for j in range(1, CONV_K):
                c = c + taps[j] * w_ref[j:j + 1, cs]
            sg = _sigmoid(c)
            s = c * sg
            if cb < 2 * DN_HEADS:
                dn = (dq_ref[:, hs] * scale) if cb < DN_HEADS else dk_ref[:, hs]
                r = lax.rsqrt(jnp.sum(s * s, axis=-1, keepdims=True) + EPS)
                ds = r * dn - s * (r * r * r) * jnp.sum(dn * s, axis=-1, keepdims=True)
            else:
                ds = dv_ref[:, hs]
            dc = ds * (sg * (1.0 + c * (1.0 - sg)))
            dc_ref[:, cs] = dc
            for j in range(CONV_K):
                dw_ref[j:j + 1, cs] += jnp.sum(dc * taps[j], axis=0, keepdims=True)
        bd = bd_ref[...]
        dbg_v = dbg_ref[...]
        lane = lax.broadcasted_iota(jnp.int32, bd.shape, 1)
        sg = _sigmoid(bd)
        ea = jnp.exp(al_ref[...])
        z = bd + db_ref[...]
        sp = _softplus(z)
        is_b = lane < DN_HEADS
        is_g = jnp.logical_and(lane >= DN_HEADS, lane < 2 * DN_HEADS)
        d_z = jnp.where(is_g, dbg_v * (-ea) * _sigmoid(z), 0.0)
        dbd_ref[...] = jnp.where(is_b, dbg_v * sg * (1.0 - sg), d_z)
        d_al = jnp.sum(jnp.where(is_g, dbg_v * (-ea) * sp, 0.0), axis=0, keepdims=True)
        d_db = jnp.sum(d_z, axis=0, keepdims=True)
        _accumulate(dab_ref, jnp.concatenate([d_al, d_db] + [jnp.zeros_like(d_al)] * 6, axis=0))

    w3 = 3 * D_MODEL
    return _call(body, name, (_sds((t, w3)), _sds((CONV_K, w3)), _sds((t, 128)), _sds((8, 128))), (t // tm,),
                 [pl.BlockSpec((tm, w3), lambda i: (i, 0)), _prev_spec(tm, w3), _acc_spec(CONV_K, w3),
                  pl.BlockSpec((tm, 128), lambda i: (i, 8192 // 128)), _acc_spec(1, 128), _acc_spec(1, 128),
                  _row_spec(tm, D_MODEL), _row_spec(tm, D_MODEL), _row_spec(tm, D_MODEL), _row_spec(tm, 128)],
                 (_row_spec(tm, w3), _acc_spec(CONV_K, w3), _row_spec(tm, 128), _acc_spec(8, 128)),
                 scratch=[pltpu.VMEM((tm + 8, w3), f32)])(proj, proj, conv_w, proj, a_log, dt_bias, dq, dk, dv, dbg)


def conv_transpose(name, dc, conv_w):
    t, w3 = dc.shape
    tm = _rows(t)
    nt = t // tm

    def body(cur_ref, nxt_ref, w_ref, o_ref, pad_ref):
        keep = (pl.program_id(0) < nt - 1).astype(f32)
        pad_ref[0:tm, :] = cur_ref[...]
        pad_ref[tm:tm + 8, :] = nxt_ref[...] * keep
        for cb in range(w3 // 128):
            cs = slice(cb * 128, (cb + 1) * 128)
            acc = pad_ref[pl.ds(CONV_K - 1, tm), cs] * w_ref[0:1, cs]
            for j in range(1, CONV_K):
                acc = acc + pad_ref[pl.ds(CONV_K - 1 - j, tm), cs] * w_ref[j:j + 1, cs]
            o_ref[:, cs] = acc.astype(bf16)

    return _call(body, name, _sds((t, w3), bf16), (nt,),
                 [_row_spec(tm, w3), _next_spec(tm, w3, t), _acc_spec(CONV_K, w3)], _row_spec(tm, w3),
                 scratch=[pltpu.VMEM((tm + 8, w3), f32)])(dc, dc, conv_w)


def _bdg(a, b, ca, cb, prec=None):
    if prec is None:
        a, b = a.astype(bf16), b.astype(bf16)
    return lax.dot_general(a, b, (((ca,), (cb,)), ((0,), (0,))), precision=prec, preferred_element_type=f32)


def _bnn(a, b, prec=None):
    return _bdg(a, b, 2, 1, prec)


def _bnt(a, b, prec=None):
    return _bdg(a, b, 2, 2, prec)


def _btn(a, b, prec=None):
    return _bdg(a, b, 1, 1, prec)


def _delta_local(q, k, v, g_row, b_row, solved=None):
    c = CHUNK
    ii = lax.broadcasted_iota(jnp.int32, (c, c), 0)
    jj = lax.broadcasted_iota(jnp.int32, (c, c), 1)
    eye, lower, strict = ii == jj, ii >= jj, ii > jj
    shp = (q.shape[0], c, c)
    g_b = jnp.broadcast_to(g_row, shp)
    gc_col = jnp.sum(jnp.where(lower, g_b, 0.0), axis=2, keepdims=True)
    gc_row = jnp.sum(jnp.where(eye, jnp.broadcast_to(gc_col, shp), 0.0), axis=1, keepdims=True)
    b_col = jnp.sum(jnp.where(eye, jnp.broadcast_to(b_row, shp), 0.0), axis=2, keepdims=True)
    gl = jnp.sum(g_row, axis=2, keepdims=True)
    decay = jnp.exp(jnp.where(lower, gc_col - gc_row, -1e30))
    kb = k * b_col
    kk = _bnt(kb, k)
    lmat = jnp.where(strict, kk * decay, 0.0)
    egc = jnp.exp(gc_col)
    rhs_w = kb * egc
    if solved is None:
        tinv = eye.astype(f32) - lmat
        pw = lmat
        for _ in range(5):
            pw = _bnn(pw, pw, HIGH)
            tinv = tinv + _bnn(tinv, pw, HIGH)
        u = _bnn(tinv, v * b_col, HIGH)
        w = _bnn(tinv, rhs_w, HIGH)
    else:
        tinv, u, w = solved
    qk = _bnt(q, k)
    amat = jnp.where(lower, qk * decay, 0.0)
    q_dec = q * egc
    kdf = jnp.exp(gl - gc_col)
    k_dec = k * kdf
    return dict(eye=eye, lower=lower, strict=strict, gc_col=gc_col, b_col=b_col, gl=gl, decay=decay,
                kb=kb, kk=kk, tinv=tinv, egc=egc, rhs_w=rhs_w, u=u, w=w, qk=qk, amat=amat, q_dec=q_dec, kdf=kdf,
                k_dec=k_dec)


def _heads(ref, hb):
    return jnp.stack([ref[:, h * HEAD_DIM:(h + 1) * HEAD_DIM] for h in range(hb)])


def _put_heads(ref, val):
    for h in range(val.shape[0]):
        ref[:, h * HEAD_DIM:(h + 1) * HEAD_DIM] = val[h]


def delta_fwd(name, q, k, v, g_rows, b_rows, next_shard=None):
    t = q.shape[0]
    nc = t // CHUNK
    hb = DELTA_HB
    fused = next_shard is not None
    assert not fused or (hb == DN_HEADS and nc >= 2)

    def body(*refs):
        if fused:
            (q_ref, k_ref, v_ref, g_ref, b_ref, x_ref, o_ref, s_ref, t_ref, u_ref, w_ref, gath_ref, state, send_sems,
             recv_sems, local_sem) = refs
        else:
            q_ref, k_ref, v_ref, g_ref, b_ref, o_ref, s_ref, t_ref, u_ref, w_ref, state = refs
        n = pl.program_id(1)
        if fused:
            start, forward, finish = _gather_steps(x_ref, gath_ref, send_sems, recv_sems, local_sem)
            pl.when(n == 0)(start)

        @pl.when(n == 0)
        def _():
            state[...] = jnp.zeros_like(state)

        loc = _delta_local(_heads(q_ref, hb), _heads(k_ref, hb), _heads(v_ref, hb), g_ref[:, pl.ds(n, 1), :],
                           b_ref[:, pl.ds(n, 1), :])
        s0 = state[...]
        s_ref[...] = s0
        t_ref[...] = loc["tinv"]
        _put_heads(u_ref, loc["u"])
        _put_heads(w_ref, loc["w"])
        v_new = loc["u"] - _bnn(loc["w"], s0)
        _put_heads(o_ref, _bnn(loc["q_dec"], s0) + _bnn(loc["amat"], v_new))
        state[...] = s0 * jnp.exp(loc["gl"]) + _btn(loc["k_dec"], v_new)
        if fused:
            pl.when(n == (3 * nc) // 4)(forward)
            pl.when(n == nc - 1)(finish)

    tok = pl.BlockSpec((CHUNK, hb * HEAD_DIM), lambda h, n: (n, h))
    row = pl.BlockSpec((hb, nc, CHUNK), lambda h, n: (h, 0, 0))
    any_spec = pl.BlockSpec(memory_space=pl.ANY)
    outs = (_sds((t, D_MODEL)), _sds((DN_HEADS, nc, HEAD_DIM, HEAD_DIM)), _sds((DN_HEADS, nc, CHUNK, CHUNK)),
            _sds((t, D_MODEL)), _sds((t, D_MODEL)))
    out_specs = (tok, pl.BlockSpec((hb, None, HEAD_DIM, HEAD_DIM), lambda h, n: (h, n, 0, 0)),
                 pl.BlockSpec((hb, None, CHUNK, CHUNK), lambda h, n: (h, n, 0, 0)), tok, tok)
    in_specs, args = [tok, tok, tok, row, row], (q, k, v, g_rows, b_rows)
    scratch = [pltpu.VMEM((hb, HEAD_DIM, HEAD_DIM), f32)]
    if fused:
        outs += (_sds((N_DEV,) + next_shard.shape, next_shard.dtype),)
        out_specs += (any_spec,)
        in_specs, args = in_specs + [any_spec], args + (next_shard,)
        scratch = scratch + list(_GATHER_SEMS)
    return _call(body, name, outs, (DN_HEADS // hb, nc), in_specs, out_specs, scratch=scratch)(*args)


def delta_bwd(name, q, k, v, g_rows, b_rows, s_all, t_all, u_all, w_all, do, chip_part=None):
    t = q.shape[0]
    nc = t // CHUNK
    c = CHUNK
    hb = DELTA_HB

    fused = chip_part is not None
    assert not fused or hb == DN_HEADS

    def body(*refs):
        if fused:
            (q_ref, k_ref, v_ref, g_ref, b_ref, s_ref, t_ref, u_ref, w_ref, do_ref, x_ref, dq_ref, dk_ref, dv_ref,
             dg_ref, db_ref, slot_ref, dstate, send_sems, recv_sems, local_sem) = refs
        else:
            (q_ref, k_ref, v_ref, g_ref, b_ref, s_ref, t_ref, u_ref, w_ref, do_ref, dq_ref, dk_ref, dv_ref, dg_ref,
             db_ref, dstate) = refs
        step = pl.program_id(1)
        n = nc - 1 - step
        if fused:
            start, finish = _chip_exchange_steps(x_ref, slot_ref, send_sems, recv_sems, local_sem)
            pl.when(step == 0)(start)

        @pl.when(step == 0)
        def _():
            dstate[...] = jnp.zeros_like(dstate)

        qv, kv, vv = _heads(q_ref, hb), _heads(k_ref, hb), _heads(v_ref, hb)
        L = _delta_local(qv, kv, vv, g_ref[:, pl.ds(n, 1), :], b_ref[:, pl.ds(n, 1), :],
                         solved=(t_ref[...], _heads(u_ref, hb), _heads(w_ref, hb)))
        eye, lower, strict = L["eye"], L["lower"], L["strict"]
        shp = (hb, c, c)
        s0 = s_ref[...]
        dov = _heads(do_ref, hb)
        ds = dstate[...]
        eg = jnp.exp(L["gl"])
        v_new = L["u"] - _bnn(L["w"], s0)
        d_k_dec = _bnt(v_new, ds)
        d_v_new = _bnn(L["k_dec"], ds) + _btn(L["amat"], dov)
        d_eg = jnp.sum(jnp.sum(ds * s0, axis=2, keepdims=True), axis=1, keepdims=True)
        d_q_dec = _bnt(dov, s0)
        d_a = _bnt(dov, v_new)
        d_w = -_bnt(d_v_new, s0)
        dstate[...] = ds * eg + _btn(L["q_dec"], dov) - _btn(L["w"], d_v_new)
        d_am = jnp.where(lower, d_a * L["decay"], 0.0)
        dq = _bnn(d_am, kv) + d_q_dec * L["egc"]
        dk = _btn(d_am, qv) + d_k_dec * L["kdf"]
        e_col = jnp.sum(d_k_dec * L["k_dec"], axis=2, keepdims=True)
        d_gc_col = jnp.sum(d_q_dec * L["q_dec"], axis=2, keepdims=True) - e_col
        d_gl = jnp.sum(e_col, axis=1, keepdims=True) + d_eg * eg
        tinv = L["tinv"]
        d_rhs_u = _btn(tinv, d_v_new, HIGH)
        d_rhs_w = _btn(tinv, d_w, HIGH)
        d_l = -(_bnt(d_rhs_u, L["u"], HIGH) + _bnt(d_rhs_w, L["w"], HIGH))
        _put_heads(dv_ref, d_rhs_u * L["b_col"])
        d_b_col = jnp.sum(d_rhs_u * vv, axis=2, keepdims=True)
        d_gc_col = d_gc_col + jnp.sum(d_rhs_w * L["rhs_w"], axis=2, keepdims=True)
        d_lm = jnp.where(strict, d_l * L["decay"], 0.0)
        d_kb = d_rhs_w * L["egc"] + _bnn(d_lm, kv)
        dk = dk + _btn(d_lm, L["kb"]) + d_kb * L["b_col"]
        d_b_col = d_b_col + jnp.sum(d_kb * kv, axis=2, keepdims=True)
        m = d_am * L["qk"] + d_lm * L["kk"]
        d_gc_col = d_gc_col + jnp.sum(m, axis=2, keepdims=True)
        d_gc_row = (jnp.sum(jnp.where(eye, jnp.broadcast_to(d_gc_col, shp), 0.0), axis=1, keepdims=True)
                    - jnp.sum(m, axis=1, keepdims=True))
        lane = lax.broadcasted_iota(jnp.int32, (1, 1, c), 2)
        d_gc_row = d_gc_row + jnp.where(lane == c - 1, d_gl, 0.0)
        d_gc_tot = jnp.sum(jnp.where(eye, jnp.broadcast_to(d_gc_row, shp), 0.0), axis=2, keepdims=True)
        dg_ref[:, pl.ds(n, 1), :] = jnp.sum(jnp.where(lower, jnp.broadcast_to(d_gc_tot, shp), 0.0), axis=1,
                                            keepdims=True)
        db_ref[:, pl.ds(n, 1), :] = jnp.sum(jnp.where(eye, jnp.broadcast_to(d_b_col, shp), 0.0), axis=1,
                                            keepdims=True)
        _put_heads(dq_ref, dq)
        _put_heads(dk_ref, dk)
        if fused:
            pl.when(step == nc - 1)(finish)

    tok = pl.BlockSpec((CHUNK, hb * HEAD_DIM), lambda h, s: (nc - 1 - s, h))
    row = pl.BlockSpec((hb, nc, CHUNK), lambda h, s: (h, 0, 0))
    any_spec = pl.BlockSpec(memory_space=pl.ANY)
    in_specs = [tok, tok, tok, row, row,
                pl.BlockSpec((hb, None, HEAD_DIM, HEAD_DIM), lambda h, s: (h, nc - 1 - s, 0, 0)),
                pl.BlockSpec((hb, None, CHUNK, CHUNK), lambda h, s: (h, nc - 1 - s, 0, 0)), tok, tok, tok]
    args = (q, k, v, g_rows, b_rows, s_all, t_all, u_all, w_all, do)
    outs = (_sds((t, D_MODEL)),) * 3 + (_sds((DN_HEADS, nc, CHUNK)),) * 2
    out_specs = (tok, tok, tok, row, row)
    scratch = [pltpu.VMEM((hb, HEAD_DIM, HEAD_DIM), f32)]
    if fused:
        in_specs, args = in_specs + [any_spec], args + (chip_part,)
        outs += (_sds(chip_part.shape, chip_part.dtype),)
        out_specs += (any_spec,)
        scratch = scratch + list(_CHIP_SEMS)
    return _call(body, name, outs, (DN_HEADS // hb, nc), in_specs, out_specs, scratch=scratch)(*args)


SEG = 8


def _perm_rows(a):
    t, c = a.shape
    return a.reshape(SEG, t // SEG, c).transpose(1, 0, 2).reshape(t, c)


def _unperm_rows(a):
    t, c = a.shape
    return a.reshape(t // SEG, SEG, c).transpose(1, 0, 2).reshape(t, c)


def _cmul(ar, ai, br, bi):
    return ar * br - ai * bi, ar * bi + ai * br


def _segment_init(er, ei, lr, li, seg_len, reverse):
    w = er.shape[1]
    sub = lax.broadcasted_iota(jnp.int32, (SEG, w), 0)

    def shift(x, k):
        if reverse:
            return jnp.where(sub < SEG - k, pltpu.roll(x, SEG - k, 0), 0.0)
        return jnp.where(sub >= k, pltpu.roll(x, k, 0), 0.0)

    pr, pi = lr, li
    for _ in range(seg_len.bit_length() - 1):
        pr, pi = _cmul(pr, pi, pr, pi)
    fr, fi = shift(er, 1), shift(ei, 1)
    for k in (1, 2, 4):
        sr, si = shift(fr, k), shift(fi, k)
        mr, mi = _cmul(pr, pi, sr, si)
        fr, fi = fr + mr, fi + mi
        pr, pi = _cmul(pr, pi, pr, pi)
    return fr, fi


def s5_fwd(name, u_perm, bbd, cbd, lam):
    t = u_perm.shape[0]
    tt = _tile(t, (1024, 512, 256, 128))
    nt, ng, w = t // tt, tt // SEG, SG_STATE
    seg_len = t // SEG
    assert seg_len & (seg_len - 1) == 0 and tt % SEG == 0

    def body(u_ref, b_ref, c_ref, lam_ref, y_ref, h_ref, x_scr, state):
        p, i = pl.program_id(1), pl.program_id(2)
        lr1, li1 = lam_ref[:, 0:w], lam_ref[:, w:2 * w]
        lr, li = jnp.broadcast_to(lr1, (SEG, w)), jnp.broadcast_to(li1, (SEG, w))
        x_scr[...] = _bdot(u_ref[...], b_ref[...], ((1,), (0,)))

        @pl.when(jnp.logical_and(p == 0, i == 0))
        def _():
            state[...] = jnp.zeros_like(state)

        @pl.when(jnp.logical_and(p == 1, i == 0))
        def _():
            sr, si = _segment_init(state[:, 0:w], state[:, w:2 * w], lr1, li1, seg_len, False)
            state[:, 0:w] = sr
            state[:, w:2 * w] = si

        def run(store):
            def step(g, st):
                row = pl.multiple_of(g * SEG, SEG)
                xg = x_scr[pl.ds(row, SEG), :]
                nr = lr * st[0] - li * st[1] + xg[:, 0:w]
                ni = lr * st[1] + li * st[0] + xg[:, w:2 * w]
                if store:
                    h_ref[pl.ds(row, SEG), 0:w] = nr
                    h_ref[pl.ds(row, SEG), w:2 * w] = ni
                return nr, ni

            fin = lax.fori_loop(0, ng, step, (state[:, 0:w], state[:, w:2 * w]))
            state[:, 0:w] = fin[0]
            state[:, w:2 * w] = fin[1]

        @pl.when(p == 0)
        def _():
            run(False)

        @pl.when(p == 1)
        def _():
            run(True)
            y_ref[...] = _bdot(h_ref[...], c_ref[...], ((1,), (0,)))

    return _call(body, name, (_sds((t, D_MODEL)), _sds((t, N_SUPER * 2 * w))), (N_SUPER, 2, nt),
                 [pl.BlockSpec((tt, 128), lambda s, p, i: (i, s)),
                  pl.BlockSpec((None, 128, 2 * w), lambda s, p, i: (s, 0, 0)),
                  pl.BlockSpec((None, 2 * w, 128), lambda s, p, i: (s, 0, 0)),
                  pl.BlockSpec((None, 1, 2 * w), lambda s, p, i: (s, 0, 0))],
                 (pl.BlockSpec((tt, 128), lambda s, p, i: (i * p, s)),
                  pl.BlockSpec((tt, 2 * w), lambda s, p, i: (i * p, s))),
                 scratch=[pltpu.VMEM((tt, 2 * w), f32), pltpu.VMEM((SEG, 2 * w), f32)])(u_perm, bbd, cbd, lam)


def s5_bwd(name, dy_perm, u_perm, h_perm, hprev0, bbd, cbd, lam):
    t = u_perm.shape[0]
    tt = _tile(t, (1024, 512, 256, 128))
    nt, ng, w = t // tt, tt // SEG, SG_STATE
    seg_len = t // SEG

    def body(dy_ref, u_ref, h_ref, hp_ref, hp0_ref, b_ref, c_ref, lam_ref, du_ref, db_ref, dc_ref, dl_ref,
             g_scr, state, dl_acc):
        p, i = pl.program_id(1), pl.program_id(2)
        first_tile = jnp.logical_or(p == 0, i == nt - 1)
        lr1, li1 = lam_ref[:, 0:w], -lam_ref[:, w:2 * w]
        lr, li = jnp.broadcast_to(lr1, (SEG, w)), jnp.broadcast_to(li1, (SEG, w))
        g_scr[...] = _bdot(dy_ref[...], c_ref[...], ((1,), (1,)))

        @pl.when(jnp.logical_and(p == 0, i == 0))
        def _():
            state[...] = jnp.zeros_like(state)

        @pl.when(jnp.logical_and(p == 1, i == 0))
        def _():
            sr, si = _segment_init(state[:, 0:w], state[:, w:2 * w], lr1, li1, seg_len, True)
            state[:, 0:w] = sr
            state[:, w:2 * w] = si
            dl_acc[...] = jnp.zeros_like(dl_acc)

        def adj(g, st):
            row = pl.multiple_of(g * SEG, SEG)
            gg = g_scr[pl.ds(row, SEG), :]
            nr = lr * st[0] - li * st[1] + gg[:, 0:w]
            ni = lr * st[1] + li * st[0] + gg[:, w:2 * w]
            return row, nr, ni

        @pl.when(p == 0)
        def _():
            def step(k, st):
                _, nr, ni = adj(ng - 1 - k, st)
                return nr, ni

            fin = lax.fori_loop(0, ng, step, (state[:, 0:w], state[:, w:2 * w]))
            state[:, 0:w] = fin[0]
            state[:, w:2 * w] = fin[1]

        @pl.when(p == 1)
        def _():
            above = jnp.where(first_tile, hp0_ref[...], hp_ref[...])

            def step(k, st):
                g = ng - 1 - k
                row, nr, ni = adj(g, st)
                g_scr[pl.ds(row, SEG), 0:w] = nr
                g_scr[pl.ds(row, SEG), w:2 * w] = ni
                prow = pl.multiple_of(jnp.maximum(g - 1, 0) * SEG, SEG)
                hp = jnp.where(g > 0, h_ref[pl.ds(prow, SEG), :], above)
                pr, pi = hp[:, 0:w], hp[:, w:2 * w]
                return nr, ni, st[2] + nr * pr + ni * pi, st[3] + ni * pr - nr * pi

            fin = lax.fori_loop(0, ng, step, (state[:, 0:w], state[:, w:2 * w], dl_acc[:, 0:w], dl_acc[:, w:2 * w]))
            state[:, 0:w] = fin[0]
            state[:, w:2 * w] = fin[1]
            dl_acc[:, 0:w] = fin[2]
            dl_acc[:, w:2 * w] = fin[3]
            a = g_scr[...]
            du_ref[...] = _bdot(a, b_ref[...], ((1,), (1,)))
            d_b = _bdot(u_ref[...], a, ((0,), (0,)))
            d_c = _bdot(dy_ref[...], h_ref[...], ((0,), (0,)))

            @pl.when(i == 0)
            def _():
                db_ref[...] = d_b
                dc_ref[...] = d_c

            @pl.when(i > 0)
            def _():
                db_ref[...] += d_b
                dc_ref[...] += d_c

            @pl.when(i == nt - 1)
            def _():
                dl_ref[...] = jnp.sum(dl_acc[...], axis=0, keepdims=True)

    tile = lambda s, p, i: (nt - 1 - i, s)
    tile1 = lambda s, p, i: (nt - 1 - i * p, s)
    above = lambda s, p, i: (jnp.maximum((nt - 1 - i * p) * (tt // SEG) - 1, 0), s)
    per_s = lambda s, p, i: (s, 0, 0)
    return _call(body, name, (_sds((t, D_MODEL)), _sds((N_SUPER, 128, 2 * w)), _sds((N_SUPER, 128, 2 * w)),
                              _sds((N_SUPER, 1, 2 * w))), (N_SUPER, 2, nt),
                 [pl.BlockSpec((tt, 128), tile), pl.BlockSpec((tt, 128), tile1), pl.BlockSpec((tt, 2 * w), tile1),
                  pl.BlockSpec((SEG, 2 * w), above), pl.BlockSpec((SEG, 2 * w), lambda s, p, i: (0, s)),
                  pl.BlockSpec((None, 128, 2 * w), per_s), pl.BlockSpec((None, 2 * w, 128), per_s),
                  pl.BlockSpec((None, 1, 2 * w), per_s)],
                 (pl.BlockSpec((tt, 128), tile1), pl.BlockSpec((None, 128, 2 * w), per_s),
                  pl.BlockSpec((None, 128, 2 * w), per_s), pl.BlockSpec((None, 1, 2 * w), per_s)),
                 scratch=[pltpu.VMEM((tt, 2 * w), f32), pltpu.VMEM((SEG, 2 * w), f32),
                          pltpu.VMEM((SEG, 2 * w), f32)])(dy_perm, u_perm, h_perm, h_perm, hprev0, bbd, cbd, lam)


def _s5_disc(a_re, a_im, log_dt, br, bi):
    dt = jnp.exp(log_dt)
    mag = jnp.exp(a_re * dt)
    lr, li = mag * jnp.cos(a_im * dt), mag * jnp.sin(a_im * dt)
    den = a_re * a_re + a_im * a_im
    fr = ((lr - 1.0) * a_re + li * a_im) / den
    fi = (li * a_re - (lr - 1.0) * a_im) / den
    return lr, li, fr * br - fi * bi, fr * bi + fi * br


def s5_disc_fwd(name, a_re, a_im, log_dt, br, bi):
    g, n = SSM_GROUPS, SSM_STATE

    def body(ar, ai, ld, brr, bir, lr, li, bbr, bbi):
        o = _s5_disc(ar[...], ai[...], ld[...], brr[...], bir[...])
        lr[...], li[...], bbr[...], bbi[...] = o

    return _call(body, name, (_sds((g, 1, n)), _sds((g, 1, n)), _sds((g, SSM_GROUP, n)), _sds((g, SSM_GROUP, n))))(
        a_re, a_im, log_dt, br, bi)


def s5_disc_bwd(name, a_re, a_im, log_dt, br, bi, d_lr, d_li, d_bbr, d_bbi):
    g, n = SSM_GROUPS, SSM_STATE

    def body(ar, ai, ld, brr, bir, c1, c2, c3, c4, o1, o2, o3, o4, o5):
        _, vjp = jax.vjp(_s5_disc, ar[...], ai[...], ld[...], brr[...], bir[...])
        o1[...], o2[...], o3[...], o4[...], o5[...] = vjp((c1[...], c2[...], c3[...], c4[...]))

    return _call(body, name, (_sds((g, 1, n)), _sds((g, 1, n)), _sds((g, 1, 1)), _sds((g, SSM_GROUP, n)),
                              _sds((g, SSM_GROUP, n))))(a_re, a_im, log_dt, br, bi, d_lr, d_li, d_bbr, d_bbi)


def gelu_fwd(name, ys_lin, proj, d_skip):
    t, d = ys_lin.shape
    tm = _rows(t, light=True)

    def body(y_ref, u_ref, d_ref, o_ref):
        o_ref[...] = _gelu(y_ref[...] + d_ref[...] * u_ref[...])

    return _call(body, name, _sds((t, d)), (t // tm,),
                 [_row_spec(tm, d), pl.BlockSpec((tm, d), lambda i: (i, 4)), _acc_spec(1, d)],
                 _row_spec(tm, d))(ys_lin, proj, d_skip)


def _head_norm(o, hn):
    outs, ns, rs = [], [], []
    for h in range(DN_HEADS):
        oh = o[:, h * 128:(h + 1) * 128]
        r = lax.rsqrt(jnp.mean(oh * oh, axis=-1, keepdims=True) + EPS)
        n = oh * r
        ns.append(n)
        rs.append(r)
        outs.append(n * hn)
    return outs, ns, rs


def merge_fwd(name, proj, o, yg, glu_lin, head_norm, b_glu):
    t = o.shape[0]
    tm = _rows(t)
    d = D_MODEL

    def body(za_ref, zb_ref, ra_ref, rb_ref, o_ref, yg_ref, gl_ref, hn_ref, bg_ref, m_ref):
        on, _, _ = _head_norm(o_ref[...], hn_ref[...])
        y_a = jnp.concatenate(on, axis=1) * _silu(za_ref[...])
        y_b = yg_ref[...] * _sigmoid(gl_ref[...] + bg_ref[...]) * _silu(zb_ref[...])
        m_ref[...] = (_sigmoid(ra_ref[...]) * y_a + _sigmoid(rb_ref[...]) * y_b).astype(bf16)

    pc = lambda c: pl.BlockSpec((tm, d), lambda i: (i, c))
    return _call(body, name, _sds((t, d), bf16), (t // tm,),
                 [pc(3), pc(5), pc(6), pc(7), _row_spec(tm, d), _row_spec(tm, d), _row_spec(tm, d),
                  _acc_spec(1, 128), _acc_spec(1, d)], _row_spec(tm, d))(
        proj, proj, proj, proj, o, yg, glu_lin, head_norm, b_glu)


def merge_bwd(name, proj, o, yg, glu_lin, head_norm, b_glu, dm):
    t = o.shape[0]
    tm = _rows(t)
    d = D_MODEL

    def body(za_ref, zb_ref, ra_ref, rb_ref, o_ref, yg_ref, gl_ref, hn_ref, bg_ref, dm_ref,
             dza_ref, dzb_ref, dra_ref, drb_ref, do_ref, dgl_ref, dyg_ref, dhn_ref, dbg_ref):
        hn = hn_ref[...]
        za, zb, ra, rb = za_ref[...], zb_ref[...], ra_ref[...], rb_ref[...]
        on, ns, rs = _head_norm(o_ref[...], hn)
        onc = jnp.concatenate(on, axis=1)
        sza = _silu(za)
        y_a = onc * sza
        yg = yg_ref[...]
        sgl = _sigmoid(gl_ref[...] + bg_ref[...])
        y2 = yg * sgl
        szb = _silu(zb)
        y_b = y2 * szb
        sra, srb = _sigmoid(ra), _sigmoid(rb)
        dmv = dm_ref[...]
        dra_ref[...] = (dmv * y_a * sra * (1.0 - sra)).astype(bf16)
        drb_ref[...] = (dmv * y_b * srb * (1.0 - srb)).astype(bf16)
        d_ya = dmv * sra
        d_yb = dmv * srb
        dza_ref[...] = (d_ya * onc * _dsilu(za)).astype(bf16)
        dzb_ref[...] = (d_yb * y2 * _dsilu(zb)).astype(bf16)
        d_on = d_ya * sza
        d_y2 = d_yb * szb
        dyg_ref[...] = d_y2 * sgl
        d_gl = d_y2 * yg * sgl * (1.0 - sgl)
        dgl_ref[...] = d_gl.astype(bf16)
        _accumulate(dbg_ref, jnp.sum(d_gl, axis=0, keepdims=True))
        d_hn = jnp.zeros((1, 128), f32)
        for h in range(DN_HEADS):
            hs = slice(h * 128, (h + 1) * 128)
            dh = d_on[:, hs]
            d_hn = d_hn + jnp.sum(dh * ns[h], axis=0, keepdims=True)
            dn = dh * hn
            do_ref[:, hs] = rs[h] * (dn - ns[h] * jnp.mean(dn * ns[h], axis=-1, keepdims=True))
        _accumulate(dhn_ref, d_hn)

    pc = lambda c: pl.BlockSpec((tm, d), lambda i: (i, c))
    rs_ = _row_spec(tm, d)
    return _call(body, name, (_sds((t, d), bf16),) * 4 + (_sds((t, d)), _sds((t, d), bf16), _sds((t, d)),
                              _sds((1, 128)), _sds((1, d))),
                 (t // tm,), [pc(3), pc(5), pc(6), pc(7), rs_, rs_, rs_, _acc_spec(1, 128), _acc_spec(1, d), rs_],
                 (rs_,) * 7 + (_acc_spec(1, 128), _acc_spec(1, d)))(
        proj, proj, proj, proj, o, yg, glu_lin, head_norm, b_glu, dm)


def gelu_bwd(name, ys_lin, proj, d_skip, dyg_a, dyg_b):
    t, d = ys_lin.shape
    tm = _rows(t, light=True)

    def body(y_ref, u_ref, d_ref, a_ref, b_ref, dys_ref, du_ref, dd_ref):
        uv = u_ref[...]
        dys = (a_ref[...] + b_ref[...]) * _dgelu(y_ref[...] + d_ref[...] * uv)
        dys_ref[...] = dys
        du_ref[...] = dys * d_ref[...]
        _accumulate(dd_ref, jnp.sum(dys * uv, axis=0, keepdims=True))

    rs_ = _row_spec(tm, d)
    return _call(body, name, (_sds((t, d)), _sds((t, d)), _sds((1, d))), (t // tm,),
                 [rs_, pl.BlockSpec((tm, d), lambda i: (i, 4)), _acc_spec(1, d), rs_, rs_],
                 (rs_, rs_, _acc_spec(1, d)))(ys_lin, proj, d_skip, dyg_a, dyg_b)


def assemble_dproj(name, dqkv, dza, du_a, du_b, dzb, dra, drb, dbd):
    t = dza.shape[0]
    tm = _rows(t)
    d = D_MODEL

    def body(qkv_ref, za_ref, ua_ref, ub_ref, zb_ref, ra_ref, rb_ref, bd_ref, o_ref):
        o_ref[:, 0:3 * d] = qkv_ref[...].astype(bf16)
        o_ref[:, 3 * d:4 * d] = za_ref[...].astype(bf16)
        o_ref[:, 4 * d:5 * d] = (ua_ref[...] + ub_ref[...]).astype(bf16)
        o_ref[:, 5 * d:6 * d] = zb_ref[...].astype(bf16)
        o_ref[:, 6 * d:7 * d] = ra_ref[...].astype(bf16)
        o_ref[:, 7 * d:8 * d] = rb_ref[...].astype(bf16)
        o_ref[:, 8 * d:8 * d + 128] = bd_ref[...].astype(bf16)
        o_ref[:, 8 * d + 128:W_PAD] = jnp.zeros((tm, W_PAD - 8 * d - 128), bf16)

    rs_ = _row_spec(tm, d)
    return _call(body, name, _sds((t, W_PAD), bf16), (t // tm,),
                 [_row_spec(tm, 3 * d), rs_, rs_, rs_, rs_, rs_, rs_, _row_spec(tm, 128)], _row_spec(tm, W_PAD))(
        dqkv, dza, du_a, du_b, dzb, dra, drb, dbd)


def adamw(name, w, g, m, v):
    lead, (r, c) = w.shape[:-2], w.shape[-2:]
    tm = _tile(r, (512, 256, 128, 64, 32, 16, 8))
    c1 = 1.0 / (1.0 - ADAM_B1 ** ADAM_STEP)
    c2 = 1.0 / (1.0 - ADAM_B2 ** ADAM_STEP)

    def body(w_ref, g_ref, m_ref, v_ref, d_ref, nm_ref, nv_ref):
        gv = g_ref[...]
        nm = ADAM_B1 * m_ref[...] + (1.0 - ADAM_B1) * gv
        nv = ADAM_B2 * v_ref[...] + (1.0 - ADAM_B2) * (gv * gv)
        d_ref[...] = -ADAM_LR * ((nm * c1) / (jnp.sqrt(nv * c2) + ADAM_EPS) + ADAM_WD * w_ref[...])
        nm_ref[...] = nm
        nv_ref[...] = nv

    if lead:
        sp = pl.BlockSpec((None, tm, c), lambda l, i: (l, i, 0))
        grid = (lead[0], r // tm)
    else:
        sp = pl.BlockSpec((tm, c), lambda i: (i, 0))
        grid = (r // tm,)
    return _call(body, name, (_sds(w.shape),) * 3, grid, [sp] * 4, (sp,) * 3)(w, g, m, v)


def _coords():
    return lax.axis_index("x"), lax.axis_index("y"), lax.axis_index("c")


def _lin(dev):
    return 4 * dev[0] + 2 * dev[1] + dev[2]


def _chips(me):
    x, y, _ = me
    return [(1 - x, y), (x, 1 - y), (1 - x, 1 - y)]


def _gather_steps(x_ref, o_ref, send_sems, recv_sems, local_sem):
    me = _coords()
    x, y, cc = me
    sibling = (x, y, 1 - cc)
    chips = _chips(me)

    def copy(k, block, to, src=None):
        return pltpu.make_async_remote_copy(
            src_ref=o_ref.at[_lin(block)] if src is None else src, dst_ref=o_ref.at[_lin(block)],
            send_sem=send_sems.at[k], recv_sem=recv_sems.at[k], device_id=to, device_id_type=MESH)

    mine = pltpu.make_async_copy(x_ref, o_ref.at[_lin(me)], local_sem)
    first = [copy(0, me, sibling, src=x_ref)] + [copy(1 + j, me, (*chip, cc), src=x_ref)
                                                 for j, chip in enumerate(chips)]
    passed = [copy(4 + j, (*chip, cc), sibling) for j, chip in enumerate(chips)]

    def start():
        mine.start()
        for cp in first:
            cp.start()

    def forward():
        for j, chip in enumerate(chips):
            copy(1 + j, (*chip, cc), me).wait_recv()
            passed[j].start()

    def finish():
        copy(0, sibling, me).wait_recv()
        for j, chip in enumerate(chips):
            copy(4 + j, (*chip, 1 - cc), me).wait_recv()
        for cp in first + passed:
            cp.wait_send()
        mine.wait()

    return start, forward, finish


_GATHER_SEMS = [pltpu.SemaphoreType.DMA((N_DEV - 1,)), pltpu.SemaphoreType.DMA((N_DEV - 1,)),
                pltpu.SemaphoreType.DMA(())]


def all_gather(name, shard):
    r, c = shard.shape

    def body(x_ref, o_ref, send_sems, recv_sems, local_sem):
        for step in _gather_steps(x_ref, o_ref, send_sems, recv_sems, local_sem):
            step()

    any_spec = pl.BlockSpec(memory_space=pl.ANY)
    return _call(body, name, _sds((N_DEV, r, c), shard.dtype), in_specs=[any_spec], out_specs=any_spec,
                 scratch=list(_GATHER_SEMS))(shard)


def pair_exchange(name, blocks):
    _, _, r, c = blocks.shape

    def body(x_ref, o_ref, send_sems, recv_sems):
        x, y, cc = _coords()
        sibling = (x, y, 1 - cc)
        cps = [pltpu.make_async_remote_copy(src_ref=x_ref.at[ch, 1 - cc], dst_ref=o_ref.at[ch], send_sem=send_sems.at[ch],
                                            recv_sem=recv_sems.at[ch], device_id=sibling, device_id_type=MESH)
               for ch in range(4)]
        for cp in cps:
            cp.start()
        for cp in cps:
            cp.wait()

    any_spec = pl.BlockSpec(memory_space=pl.ANY)
    return _call(body, name, _sds((4, r, c), blocks.dtype), in_specs=[any_spec], out_specs=any_spec,
                 scratch=[pltpu.SemaphoreType.DMA((4,)), pltpu.SemaphoreType.DMA((4,))])(blocks)


def _chip_exchange_steps(x_ref, o_ref, send_sems, recv_sems, local_sem):
    me = _coords()
    x, y, cc = me
    my_chip = 2 * x + y
    mine = pltpu.make_async_copy(x_ref.at[my_chip], o_ref.at[my_chip], local_sem)
    peers = [(px, py, cc) for px, py in _chips(me)]
    sends = [pltpu.make_async_remote_copy(src_ref=x_ref.at[2 * px + py], dst_ref=o_ref.at[my_chip],
                                          send_sem=send_sems.at[j], recv_sem=recv_sems.at[j], device_id=(px, py, pc),
                                          device_id_type=MESH) for j, (px, py, pc) in enumerate(peers)]

    def start():
        mine.start()
        for cp in sends:
            cp.start()

    def finish():
        for j, (px, py, pc) in enumerate(peers):
            pltpu.make_async_remote_copy(src_ref=x_ref.at[my_chip], dst_ref=o_ref.at[2 * px + py],
                                         send_sem=send_sems.at[j], recv_sem=recv_sems.at[j], device_id=(px, py, pc),
                                         device_id_type=MESH).wait_recv()
        for cp in sends:
            cp.wait_send()
        mine.wait()

    return start, finish


_CHIP_SEMS = [pltpu.SemaphoreType.DMA((3,)), pltpu.SemaphoreType.DMA((3,)), pltpu.SemaphoreType.DMA(())]


def chip_exchange(name, blocks):
    def body(x_ref, o_ref, send_sems, recv_sems, local_sem):
        for step in _chip_exchange_steps(x_ref, o_ref, send_sems, recv_sems, local_sem):
            step()

    any_spec = pl.BlockSpec(memory_space=pl.ANY)
    return _call(body, name, _sds(blocks.shape, blocks.dtype), in_specs=[any_spec], out_specs=any_spec,
                 scratch=list(_CHIP_SEMS))(blocks)


def pair_sum(name, mine, theirs):
    _, r, c = mine.shape
    tm = _tile(r, (480, 432, 240, 256, 128, 64, 32, 16))

    def body(a_ref, b_ref, o_ref):
        o_ref[...] = (a_ref[...].astype(f32) + b_ref[...].astype(f32)).astype(o_ref.dtype)

    sp = pl.BlockSpec((None, tm, c), lambda ch, i: (ch, i, 0))
    return _call(body, name, _sds(mine.shape, mine.dtype), (4, r // tm), [sp, sp], sp)(mine, theirs)


def sum_slots(name, slots):
    n, r, c = slots.shape
    tm = _tile(r, (480, 432, 240, 256, 128, 64, 32, 16))

    def body(s_ref, o_ref):
        acc = s_ref[0].astype(f32)
        for d in range(1, n):
            acc = acc + s_ref[d].astype(f32)
        o_ref[...] = acc

    return _call(body, name, _sds((r, c)), (r // tm,), [pl.BlockSpec((n, tm, c), lambda i: (0, i, 0))],
                 pl.BlockSpec((tm, c), lambda i: (i, 0)))(slots)


def _rows_layout(col8, t):
    return col8.T.reshape(DN_HEADS, t // CHUNK, CHUNK)


def _blockdiag(m):
    g, a, b = m.shape
    m = m.reshape(N_SUPER, SUPER, a, b)
    out = jnp.einsum("sgab,gh->sgahb", m, jnp.eye(SUPER, dtype=m.dtype))
    return out.reshape(N_SUPER, SUPER * a, SUPER * b)


def _diag_blocks(m, a, b):
    m = m.reshape(N_SUPER, SUPER, a, SUPER, b)
    return jnp.einsum("sgahb,gh->sgab", m, jnp.eye(SUPER, dtype=m.dtype)).reshape(SSM_GROUPS, a, b)


def _s5_params(p, li):
    tag = f"l{li}"
    n = SSM_STATE
    a_re = p["ssm_a_re"].reshape(SSM_GROUPS, 1, n)
    a_im = p["ssm_a_im"].reshape(SSM_GROUPS, 1, n)
    log_dt = p["ssm_log_dt"].reshape(SSM_GROUPS, 1, 1)
    br = jnp.swapaxes(p["ssm_b_re"], 1, 2)
    bi = jnp.swapaxes(p["ssm_b_im"], 1, 2)
    lr, li_, bbr, bbi = s5_disc_fwd("s5_disc_" + tag, a_re, a_im, log_dt, br, bi)
    lam = jnp.concatenate([lr.reshape(N_SUPER, 1, SG_STATE), li_.reshape(N_SUPER, 1, SG_STATE)], axis=-1)
    bbd = jnp.concatenate([_blockdiag(bbr), _blockdiag(bbi)], axis=-1).astype(bf16)
    c_re = jnp.swapaxes(p["ssm_c_re"], 1, 2)
    c_im = jnp.swapaxes(p["ssm_c_im"], 1, 2)
    cbd = jnp.concatenate([_blockdiag(c_re), -_blockdiag(c_im)], axis=1).astype(bf16)
    return dict(a_re=a_re, a_im=a_im, log_dt=log_dt, br=br, bi=bi, lam=lam, bbd=bbd, cbd=cbd)


def layer_fwd(x, p, li, next_shard=None):
    tag = f"l{li}"
    t = x.shape[0]
    d = D_MODEL
    h = rmsnorm_fwd("norm_pre_" + tag, x, p["norm_pre"])
    proj = mm_nn("proj_" + tag, h, p["w_all"])
    q, k, v, bg = conv_qkv_fwd("conv_" + tag, proj, p["conv_w"], p["a_log"], p["dt_bias"])
    b_rows = _rows_layout(bg[:, 0:DN_HEADS], t)
    g_rows = _rows_layout(bg[:, DN_HEADS:2 * DN_HEADS], t)
    o, s_all, t_all, u_all, w_sol, *gathered = delta_fwd("delta_" + tag, q, k, v, g_rows, b_rows, next_shard)
    sp = _s5_params(p, li)
    u_perm = _perm_rows(proj[:, 4 * d:5 * d])
    ys_perm, hs = s5_fwd("s5_" + tag, u_perm, sp["bbd"], sp["cbd"], sp["lam"])
    ys_lin = _unperm_rows(ys_perm)
    yg = gelu_fwd("gelu_" + tag, ys_lin, proj, p["ssm_d"])
    glu_lin = mm_nn("glu_" + tag, yg, p["w_glu"])
    merged = merge_fwd("merge_" + tag, proj, o, yg, glu_lin, p["head_norm"], p["b_glu"])
    out = mm_nn("out_" + tag, merged, p["w_out"])
    y = residual_norm_fwd("norm_post_" + tag, x, out, p["norm_post"])
    saved = dict(x=x, h=h, proj=proj, q=q, k=k, v=v, g_rows=g_rows, b_rows=b_rows, o=o, s_all=s_all, t_all=t_all, u_all=u_all, w_sol=w_sol, sp=sp, u_perm=u_perm,
                 hs=hs, ys_lin=ys_lin, yg=yg, glu_lin=glu_lin, merged=merged, out=out)
    return y, saved, (gathered[0] if gathered else None)


def layer_bwd(dy, p, s, li, chip_part=None):
    tag = f"l{li}"
    t = dy.shape[0]
    d = D_MODEL
    sp = s["sp"]
    gr = {}
    d_out, gr["norm_post"] = post_norm_bwd("norm_post_b_" + tag, s["out"], p["norm_post"], dy)
    d_merged = mm_nt("out_b_" + tag, d_out, p["w_out"])
    gr["w_out"] = mm_tn("out_w_" + tag, s["merged"], d_out, bf16)
    (dza, dzb, dra, drb, d_o, d_glu, dyg_a, gr["head_norm"], gr["b_glu"]) = merge_bwd(
        "merge_b_" + tag, s["proj"], s["o"], s["yg"], s["glu_lin"], p["head_norm"], p["b_glu"], d_merged)
    dyg_b = mm_nt("glu_b_" + tag, d_glu, p["w_glu"])
    gr["w_glu"] = mm_tn("glu_w_" + tag, s["yg"], d_glu, bf16)
    d_ys, du_a, gr["ssm_d"] = gelu_bwd("gelu_b_" + tag, s["ys_lin"], s["proj"], p["ssm_d"], dyg_a, dyg_b)
    hprev0 = jnp.concatenate([jnp.zeros((1, s["hs"].shape[1]), f32), s["hs"][-SEG:-1]], axis=0)
    du_perm, d_bbd, d_cbd, d_lam = s5_bwd("s5_b_" + tag, _perm_rows(d_ys), s["u_perm"], s["hs"], hprev0, sp["bbd"],
                                          sp["cbd"], sp["lam"])
    du_b = _unperm_rows(du_perm)
    gr["ssm_c_re"] = _diag_blocks(d_cbd[:, :, 0:SG_STATE], SSM_GROUP, SSM_STATE)
    gr["ssm_c_im"] = -_diag_blocks(d_cbd[:, :, SG_STATE:], SSM_GROUP, SSM_STATE)
    d_bbr = _diag_blocks(d_bbd[:, :, 0:SG_STATE], SSM_GROUP, SSM_STATE)
    d_bbi = _diag_blocks(d_bbd[:, :, SG_STATE:], SSM_GROUP, SSM_STATE)
    d_lr = d_lam[:, :, 0:SG_STATE].reshape(SSM_GROUPS, 1, SSM_STATE)
    d_li = d_lam[:, :, SG_STATE:].reshape(SSM_GROUPS, 1, SSM_STATE)
    d_are, d_aim, d_ldt, d_br, d_bi = s5_disc_bwd("s5_disc_b_" + tag, sp["a_re"], sp["a_im"], sp["log_dt"], sp["br"],
                                                  sp["bi"], d_lr, d_li, d_bbr, d_bbi)
    gr["ssm_a_re"] = d_are.reshape(SSM_GROUPS, SSM_STATE)
    gr["ssm_a_im"] = d_aim.reshape(SSM_GROUPS, SSM_STATE)
    gr["ssm_log_dt"] = d_ldt.reshape(SSM_GROUPS)
    gr["ssm_b_re"] = jnp.swapaxes(d_br, 1, 2)
    gr["ssm_b_im"] = jnp.swapaxes(d_bi, 1, 2)
    dq, dk, dv, dg_rows, db_rows, *slots = delta_bwd("delta_b_" + tag, s["q"], s["k"], s["v"], s["g_rows"],
                                                     s["b_rows"], s["s_all"], s["t_all"], s["u_all"], s["w_sol"], d_o,
                                                     chip_part)
    dbg = jnp.concatenate([db_rows.reshape(DN_HEADS, t).T, dg_rows.reshape(DN_HEADS, t).T,
                           jnp.zeros((t, 128 - 2 * DN_HEADS), f32)], axis=1)
    dc, gr["conv_w"], dbd, dab = conv_qkv_bwd("conv_b_" + tag, s["proj"], p["conv_w"], p["a_log"], p["dt_bias"],
                                              dq, dk, dv, dbg)
    gr["a_log"] = dab[0, DN_HEADS:2 * DN_HEADS]
    gr["dt_bias"] = dab[1, DN_HEADS:2 * DN_HEADS]
    dqkv = conv_transpose("conv_t_" + tag, dc, p["conv_w"])
    dproj = assemble_dproj("dproj_" + tag, dqkv, dza, du_a, du_b, dzb, dra, drb, dbd)
    d_h = mm_nt("proj_b_" + tag, dproj, p["w_all"])
    gr["w_all"] = mm_tn("proj_w_" + tag, s["h"], dproj, bf16)
    dx, gr["norm_pre"] = rmsnorm_bwd("norm_pre_b_" + tag, s["x"], p["norm_pre"], d_h, dy)
    return dx, gr, (slots[0] if slots else None)


REPL = ["norm_pre", "a_log", "dt_bias", "head_norm", "ssm_a_re", "ssm_a_im", "ssm_log_dt", "ssm_b_re", "ssm_b_im",
        "ssm_c_re", "ssm_c_im", "ssm_d", "b_glu", "norm_post"]
SHARDED = ["w_in", "conv_w", "w_glu", "w_out"]
ALL_W = ["norm_pre", "w_in", "conv_w", "a_log", "dt_bias", "head_norm", "ssm_a_re", "ssm_a_im", "ssm_log_dt",
         "ssm_b_re", "ssm_b_im", "ssm_c_re", "ssm_c_im", "ssm_d", "w_glu", "b_glu", "w_out", "norm_post"]
PACK_W = 1024


def _pack_flat(arrs, rows):
    flat = jnp.concatenate([a.reshape(-1) for a in arrs])
    return jnp.pad(flat, (0, rows * PACK_W - flat.shape[0])).reshape(rows, PACK_W)


def _flat_rows(arrs, mult=8):
    n = sum(math.prod(a.shape) for a in arrs)
    rows = -(-n // PACK_W)
    return -(-rows // mult) * mult


def _unpack(flat, shapes):
    out, off = [], 0
    for sh in shapes:
        n = math.prod(sh)
        out.append(flat[off:off + n].reshape(sh))
        off += n
    return out


def _repl_split(shapes):
    big = [n for n in REPL if math.prod(shapes[n]) % PACK_W == 0]
    small = [n for n in REPL if n not in big]
    return big, small


def _rows8(n):
    return -(-n // (8 * PACK_W)) * 8


def _repl_rows(shapes):
    big, small = _repl_split(shapes)
    rows = sum(_rows8(math.prod(shapes[n])) for n in big) + _flat_rows([_sds(shapes[n]) for n in small], 8)
    return -(-rows // (8 * N_DEV)) * (8 * N_DEV)


def _repl_pack(arrs, shapes):
    big, small = _repl_split(shapes)
    parts = []
    for n in big:
        a = arrs[n].reshape(-1, PACK_W)
        parts.append(jnp.pad(a, ((0, _rows8(a.size) - a.shape[0]), (0, 0))))
    parts.append(_pack_flat([arrs[n] for n in small], _flat_rows([_sds(shapes[n]) for n in small], 8)))
    used = sum(p.shape[0] for p in parts)
    parts.append(jnp.zeros((_repl_rows(shapes) - used, PACK_W), parts[0].dtype))
    return jnp.concatenate(parts, axis=0)


def _repl_unpack(packed, shapes):
    big, small = _repl_split(shapes)
    out, off = {}, 0
    for n in big:
        size = math.prod(shapes[n])
        out[n] = packed[off:off + size // PACK_W].reshape(shapes[n])
        off += _rows8(size)
    srows = _flat_rows([_sds(shapes[n]) for n in small], 8)
    out.update(zip(small, _unpack(packed[off:off + srows].reshape(-1), [shapes[n] for n in small])))
    return out


_COL_RUNS = ((0, 4096), (4112, W_COLS), (4096, 4112))


def _w_all(main, tails, wc):
    pieces = []
    for lo, hi in _COL_RUNS:
        for dv in range(N_DEV):
            a, b = max(lo, dv * wc) - dv * wc, min(hi, (dv + 1) * wc) - dv * wc
            if a < min(b, PACK_W):
                pieces.append(main[dv][:, :, a:min(b, PACK_W)])
            if b > max(a, PACK_W):
                pieces.append(tails[dv][:, :, max(a, PACK_W) - PACK_W:b - PACK_W])
    ll, rows = main[0].shape[0], main[0].shape[1]
    pieces.append(jnp.zeros((ll, rows, W_PAD - W_COLS), main[0].dtype))
    return jnp.concatenate(pieces, axis=2)


def _ref_cols(g, lo, hi):
    pieces, off = [], 0
    for a, b in _COL_RUNS:
        s, e = max(lo, a), min(hi, b)
        if s < e:
            pieces.append((s, g[..., off + s - a:off + e - a]))
        off += b - a
    pieces.sort(key=lambda t: t[0])
    return jnp.concatenate([p for _, p in pieces], axis=-1) if len(pieces) > 1 else pieces[0][1]


def kernel(x, norm_pre, w_in, conv_w, a_log, dt_bias, head_norm, ssm_a_re, ssm_a_im, ssm_log_dt, ssm_b_re, ssm_b_im, ssm_c_re, ssm_c_im, ssm_d, w_glu, b_glu, w_out, norm_post, loss_target, m_norm_pre, m_w_in, m_conv_w, m_a_log, m_dt_bias, m_head_norm, m_ssm_a_re, m_ssm_a_im, m_ssm_log_dt, m_ssm_b_re, m_ssm_b_im, m_ssm_c_re, m_ssm_c_im, m_ssm_d, m_w_glu, m_b_glu, m_w_out, m_norm_post, v_norm_pre, v_w_in, v_conv_w, v_a_log, v_dt_bias, v_head_norm, v_ssm_a_re, v_ssm_a_im, v_ssm_log_dt, v_ssm_b_re, v_ssm_b_im, v_ssm_c_re, v_ssm_c_im, v_ssm_d, v_w_glu, v_b_glu, v_w_out, v_norm_post):
    loc = dict(locals())
    w = {n: loc[n] for n in ALL_W}
    m = {n: loc["m_" + n] for n in ALL_W}
    v = {n: loc["v_" + n] for n in ALL_W}
    depth = w_in.shape[0]
    wc = w_in.shape[2]
    cc = conv_w.shape[2]
    wr = w_glu.shape[1]

    tail = wc - PACK_W
    conv_hi = conv_w.astype(bf16)
    conv_mid = (conv_w - conv_hi.astype(f32)).astype(bf16)
    conv_lo = (conv_w - conv_hi.astype(f32) - conv_mid.astype(f32)).astype(bf16)
    conv3 = jnp.stack([conv_hi, conv_mid, conv_lo], axis=1)
    w_in_b = w_in.astype(bf16)
    misc_shapes = [(D_MODEL, tail), (3, CONV_K, cc)]
    misc_rows = _flat_rows([_sds(sh, bf16) for sh in misc_shapes], 16)

    def layer_shard(li):
        return jnp.concatenate([w_in_b[li, :, :PACK_W], w_glu[li].astype(bf16), w_out[li].astype(bf16),
                                _pack_flat([w_in_b[li, :, PACK_W:], conv3[li]], misc_rows)])

    def layer_weights(gathered):
        miscs = [_unpack(gathered[dv, D_MODEL + 2 * wr:].reshape(-1), misc_shapes) for dv in range(N_DEV)]
        w_all = _w_all([gathered[dv, :D_MODEL][None] for dv in range(N_DEV)], [mi[0][None] for mi in miscs], wc)[0]
        glu = gathered[:, D_MODEL:D_MODEL + wr].reshape(N_DEV * wr, D_MODEL)
        out = gathered[:, D_MODEL + wr:D_MODEL + 2 * wr].reshape(N_DEV * wr, D_MODEL)
        conv = jnp.concatenate([mi[1][0].astype(f32) + mi[1][1].astype(f32) + mi[1][2].astype(f32) for mi in miscs],
                               axis=1)
        return dict(w_all=w_all, w_glu=glu, w_out=out, conv_w=conv)

    def layer_params(li, gathered):
        return dict(layer_weights(gathered), norm_pre=norm_pre[li].reshape(1, -1),
                    a_log=jnp.pad(a_log[li], (DN_HEADS, 128 - 2 * DN_HEADS)).reshape(1, 128),
                    dt_bias=jnp.pad(dt_bias[li], (DN_HEADS, 128 - 2 * DN_HEADS)).reshape(1, 128),
                    head_norm=head_norm[li].reshape(1, -1), ssm_a_re=ssm_a_re[li], ssm_a_im=ssm_a_im[li],
                    ssm_log_dt=ssm_log_dt[li], ssm_b_re=ssm_b_re[li], ssm_b_im=ssm_b_im[li], ssm_c_re=ssm_c_re[li],
                    ssm_c_im=ssm_c_im[li], ssm_d=ssm_d[li].reshape(1, -1),
                    b_glu=b_glu[li].reshape(1, -1), norm_post=norm_post[li].reshape(1, -1))

    act = x[0]
    saved, params = [], []
    gathered = all_gather("gather_weights", layer_shard(0))
    for li in range(depth):
        params.append(layer_params(li, gathered))
        act, sv, gathered = layer_fwd(act, params[li], li, layer_shard(li + 1) if li + 1 < depth else None)
        saved.append(sv)
    loss_part, dy = loss_head("loss_head", act, loss_target[0])
    repl_shapes = {n: w[n].shape for n in REPL}
    repl_rows = _repl_rows(repl_shapes)
    rr = repl_rows // N_DEV
    gmisc_shapes = [(D_MODEL, tail), (CONV_K, cc)]
    gmisc_rows = _flat_rows([_sds(sh, bf16) for sh in gmisc_shapes], 16)
    my_c = lax.axis_index("c")

    def chip_sums(li, gl, extra=None):
        g_conv = gl["conv_w"].astype(bf16)
        blocks = jnp.stack([jnp.concatenate(
            [_ref_cols(gl["w_all"], dv * wc, dv * wc + PACK_W), gl["w_glu"][dv * wr:(dv + 1) * wr],
             gl["w_out"][dv * wr:(dv + 1) * wr],
             _pack_flat([_ref_cols(gl["w_all"], dv * wc + PACK_W, (dv + 1) * wc), g_conv[:, dv * cc:(dv + 1) * cc]],
                        gmisc_rows)] + ([extra[dv * rr:(dv + 1) * rr]] if extra is not None else []))
            for dv in range(N_DEV)])
        blocks = blocks.reshape(4, 2, blocks.shape[1], PACK_W)
        from_sibling = pair_exchange(f"pair_grads_l{li}", blocks)
        own = lax.dynamic_index_in_dim(blocks, my_c, axis=1, keepdims=False)
        return pair_sum(f"pair_sum_grads_l{li}", own, from_sibling)

    grads, slots, pending = [None] * depth, [None] * depth, None
    for li in reversed(range(depth)):
        dy, grads[li], got = layer_bwd(dy, params[li], saved[li], li, pending)
        if pending is not None:
            slots[li + 1] = got
        g_repl = None
        if li == 0:
            g_repl = _repl_pack({n: jnp.stack([grads[l][n] for l in range(depth)]).reshape(w[n].shape) for n in REPL},
                                repl_shapes).astype(bf16)
        pending = chip_sums(li, grads[li], g_repl)
    slots[0] = chip_exchange("scatter_grads", pending)
    grad_x = dy[None]
    loss = lax.psum(loss_part[0, 0], ("x", "y", "c"))
    mine = [sum_slots(f"sum_grads_l{li}", slots[li]) for li in range(depth)]
    o1, o2, o3, o4 = D_MODEL, D_MODEL + wr, D_MODEL + 2 * wr, D_MODEL + 2 * wr + gmisc_rows
    unpacked = [_unpack(mi[o3:o4].reshape(-1), gmisc_shapes) for mi in mine]
    gs_w_in = jnp.stack([jnp.concatenate([mi[:o1], up[0]], axis=1) for mi, up in zip(mine, unpacked)])
    gs_glu = jnp.stack([mi[o1:o2] for mi in mine])
    gs_out = jnp.stack([mi[o2:o3] for mi in mine])
    gs_conv = jnp.stack([up[1] for up in unpacked])
    g_repl_full = all_gather("gather_repl_grads", mine[0][o4:o4 + rr]).reshape(repl_rows, PACK_W)
    g = _repl_unpack(g_repl_full, repl_shapes)
    g.update(w_in=gs_w_in, w_glu=gs_glu, w_out=gs_out, conv_w=gs_conv)

    delta, new_m, new_v = {}, {}, {}
    for n in SHARDED:
        delta[n], new_m[n], new_v[n] = adamw("adamw_" + n, w[n], g[n], m[n], v[n])
    outs = adamw("adamw_repl", *[_repl_pack({n: src[n] for n in REPL}, repl_shapes) for src in (w, g, m, v)])
    for dst, o in zip((delta, new_m, new_v), outs):
        dst.update(_repl_unpack(o, repl_shapes))
    return (loss, grad_x, *[g[n] for n in ALL_W], *[delta[n] for n in ALL_W], *[new_m[n] for n in ALL_W],
            *[new_v[n] for n in ALL_W])
```

```python
import math

import jax
import jax.numpy as jnp
from jax import lax
from jax.experimental import pallas as pl
from jax.experimental.pallas import tpu as pltpu

f32 = jnp.float32
bf16 = jnp.bfloat16

D_MODEL = 1024
DEPTH = 4
N_DEV = 8
DN_HEADS = 8
HEAD_DIM = 128
CHUNK = 64
CONV_K = 4
SSM_GROUPS = 64
SSM_GROUP = 16
SSM_STATE = 64
SUPER = 8
N_SUPER = SSM_GROUPS // SUPER
SG_STATE = SUPER * SSM_STATE
EPS = 1e-6
W_COLS = 8208
W_PAD = 8448
ADAM_LR, ADAM_B1, ADAM_B2, ADAM_EPS, ADAM_WD, ADAM_STEP = 0.001, 0.9, 0.999, 1e-08, 0.01, 10
VMEM_LIMIT = 56 * 1024 * 1024
MESH = pl.DeviceIdType.MESH
HIGH = lax.Precision.HIGH
DELTA_HB = 8


def _call(body, name, out_shape, grid=None, in_specs=None, out_specs=None, scratch=(), **kw):
    args = dict(out_shape=out_shape, name=name, scratch_shapes=list(scratch),
                compiler_params=pltpu.CompilerParams(vmem_limit_bytes=VMEM_LIMIT, **kw))
    if grid is not None:
        args.update(grid=grid, in_specs=in_specs, out_specs=out_specs)
    else:
        if in_specs is not None:
            args.update(in_specs=in_specs)
        if out_specs is not None:
            args.update(out_specs=out_specs)
    return pl.pallas_call(body, **args)


def _sds(shape, dtype=f32):
    return jax.ShapeDtypeStruct(tuple(shape), dtype)


def _sigmoid(x):
    return 1.0 / (1.0 + jnp.exp(-x))


def _silu(x):
    return x * _sigmoid(x)


def _dsilu(x):
    s = _sigmoid(x)
    return s * (1.0 + x * (1.0 - s))


_GELU_C = math.sqrt(2.0 / math.pi)


def _gelu(x):
    return 0.5 * x * (1.0 + jnp.tanh(_GELU_C * (x + 0.044715 * x * x * x)))


def _dgelu(x):
    t = jnp.tanh(_GELU_C * (x + 0.044715 * x * x * x))
    return 0.5 * (1.0 + t) + 0.5 * x * (1.0 - t * t) * _GELU_C * (1.0 + 3 * 0.044715 * x * x)


def _softplus(x):
    return jnp.maximum(x, 0.0) + jnp.log(1.0 + jnp.exp(-jnp.abs(x)))


def _bdot(a, b, dn):
    return lax.dot_general(a.astype(bf16), b.astype(bf16), (dn, ((), ())), preferred_element_type=f32)


def _matmul(name, a, b, *, dn, grid, a_spec, b_spec, o_spec, o_shape, o_dtype=f32):
    nk = grid[-1]

    def body(a_ref, b_ref, o_ref, acc_ref):
        p = _bdot(a_ref[...], b_ref[...], dn)
        if nk == 1:
            o_ref[...] = p.astype(o_dtype)
        else:
            k = pl.program_id(len(grid) - 1)

            @pl.when(k == 0)
            def _():
                acc_ref[...] = p

            @pl.when(k > 0)
            def _():
                acc_ref[...] += p

            @pl.when(k == nk - 1)
            def _():
                o_ref[...] = acc_ref[...].astype(o_dtype)

    blk = tuple(d for d in o_spec.block_shape if d is not None)
    return _call(body, name, _sds(o_shape, o_dtype), grid, [a_spec, b_spec], o_spec,
                 scratch=[pltpu.VMEM(blk if nk > 1 else (8, 128), f32)])(a, b)


def _tile(n, pref):
    for t in pref:
        if n % t == 0:
            return t
    return n


def mm_nn(name, a, b):
    m, k = a.shape
    n = b.shape[1]
    tm, tn, tk = _tile(m, (1024, 512, 256)), _tile(n, (2816, 1024, 512)), _tile(k, (2816, 1024))
    return _matmul(name, a, b, dn=((1,), (0,)), grid=(m // tm, n // tn, k // tk),
                   a_spec=pl.BlockSpec((tm, tk), lambda i, j, l: (i, l)),
                   b_spec=pl.BlockSpec((tk, tn), lambda i, j, l: (l, j)),
                   o_spec=pl.BlockSpec((tm, tn), lambda i, j, l: (i, j)), o_shape=(m, n))


def mm_nt(name, a, b):
    m, k = a.shape
    n = b.shape[0]
    tm, tn, tk = _tile(m, (1024, 512, 256)), _tile(n, (1024, 512)), _tile(k, (2816, 1024))
    return _matmul(name, a, b, dn=((1,), (1,)), grid=(m // tm, n // tn, k // tk),
                   a_spec=pl.BlockSpec((tm, tk), lambda i, j, l: (i, l)),
                   b_spec=pl.BlockSpec((tn, tk), lambda i, j, l: (j, l)),
                   o_spec=pl.BlockSpec((tm, tn), lambda i, j, l: (i, j)), o_shape=(m, n))


def mm_tn(name, a, b, o_dtype=f32):
    k, m = a.shape
    n = b.shape[1]
    tm, tn, tk = _tile(m, (512,)), _tile(n, (2816, 1024, 512)), _tile(k, (512, 256))
    return _matmul(name, a, b, dn=((0,), (0,)), grid=(m // tm, n // tn, k // tk),
                   a_spec=pl.BlockSpec((tk, tm), lambda i, j, l: (l, i)),
                   b_spec=pl.BlockSpec((tk, tn), lambda i, j, l: (l, j)),
                   o_spec=pl.BlockSpec((tm, tn), lambda i, j, l: (i, j)), o_shape=(m, n), o_dtype=o_dtype)


def _rows(t, light=False):
    return _tile(t, (512, 256) if light else (256,))


def _row_spec(tm, w):
    return pl.BlockSpec((tm, w), lambda i: (i, 0))


def _acc_spec(r, w):
    return pl.BlockSpec((r, w), lambda i: (0, 0))


def _accumulate(ref, val):
    @pl.when(pl.program_id(0) == 0)
    def _():
        ref[...] = val

    @pl.when(pl.program_id(0) > 0)
    def _():
        ref[...] += val


def rmsnorm_fwd(name, x, gain):
    t, d = x.shape
    tm = _rows(t, light=True)

    def body(x_ref, g_ref, o_ref):
        xv = x_ref[...]
        r = lax.rsqrt(jnp.mean(xv * xv, axis=-1, keepdims=True) + EPS)
        o_ref[...] = (xv * r * g_ref[...]).astype(bf16)

    return _call(body, name, _sds((t, d), bf16), (t // tm,), [_row_spec(tm, d), _acc_spec(1, d)],
                 _row_spec(tm, d))(x, gain)


def rmsnorm_bwd(name, x, gain, dn, dres):
    t, d = x.shape
    tm = _rows(t, light=True)

    def body(x_ref, g_ref, dn_ref, dr_ref, dx_ref, dg_ref):
        xv = x_ref[...]
        r = lax.rsqrt(jnp.mean(xv * xv, axis=-1, keepdims=True) + EPS)
        n = xv * r
        dnv = dn_ref[...]
        _accumulate(dg_ref, jnp.sum(dnv * n, axis=0, keepdims=True))
        dng = dnv * g_ref[...]
        dx_ref[...] = dr_ref[...] + r * (dng - n * jnp.mean(dng * n, axis=-1, keepdims=True))

    return _call(body, name, (_sds((t, d)), _sds((1, d))), (t // tm,),
                 [_row_spec(tm, d), _acc_spec(1, d), _row_spec(tm, d), _row_spec(tm, d)],
                 (_row_spec(tm, d), _acc_spec(1, d)))(x, gain, dn, dres)


def residual_norm_fwd(name, x, out, gain):
    t, d = x.shape
    tm = _rows(t, light=True)

    def body(x_ref, o_ref, g_ref, y_ref):
        ov = o_ref[...]
        r = lax.rsqrt(jnp.mean(ov * ov, axis=-1, keepdims=True) + EPS)
        y_ref[...] = x_ref[...] + ov * r * g_ref[...]

    return _call(body, name, _sds((t, d)), (t // tm,), [_row_spec(tm, d), _row_spec(tm, d), _acc_spec(1, d)],
                 _row_spec(tm, d))(x, out, gain)


def post_norm_bwd(name, out, gain, dy):
    t, d = out.shape
    tm = _rows(t, light=True)

    def body(o_ref, g_ref, dy_ref, do_ref, dg_ref):
        ov = o_ref[...]
        r = lax.rsqrt(jnp.mean(ov * ov, axis=-1, keepdims=True) + EPS)
        n = ov * r
        dyv = dy_ref[...]
        _accumulate(dg_ref, jnp.sum(dyv * n, axis=0, keepdims=True))
        dng = dyv * g_ref[...]
        do_ref[...] = (r * (dng - n * jnp.mean(dng * n, axis=-1, keepdims=True))).astype(bf16)

    return _call(body, name, (_sds((t, d), bf16), _sds((1, d))), (t // tm,),
                 [_row_spec(tm, d), _acc_spec(1, d), _row_spec(tm, d)],
                 (_row_spec(tm, d), _acc_spec(1, d)))(out, gain, dy)


def loss_head(name, y, target):
    t, d = y.shape
    tm = _rows(t, light=True)

    def body(y_ref, t_ref, l_ref, dy_ref):
        e = y_ref[...] - t_ref[...]
        dy_ref[...] = e * (1.0 / d)
        s = jnp.sum(jnp.sum(e * e, axis=1, keepdims=True), axis=0, keepdims=True) * (0.5 / d)
        _accumulate(l_ref, s)

    return _call(body, name, (_sds((1, 1)), _sds((t, d))), (t // tm,),
                 [_row_spec(tm, d), _row_spec(tm, d)], (_acc_spec(1, 1), _row_spec(tm, d)))(y, target)


def _prev_spec(tm, w):
    return pl.BlockSpec((8, w), lambda i: (jnp.maximum(i * (tm // 8) - 1, 0), 0))


def _next_spec(tm, w, t):
    return pl.BlockSpec((8, w), lambda i: (jnp.minimum((i + 1) * (tm // 8), t // 8 - 1), 0))


def _fill_pad(pad_ref, prev_ref, cur_ref, tm):
    keep = (pl.program_id(0) > 0).astype(f32)
    pad_ref[0:8, :] = prev_ref[...] * keep
    pad_ref[8:8 + tm, :] = cur_ref[...]


def _conv_block(pad_ref, w_ref, cb, tm):
    cs = slice(cb * 128, (cb + 1) * 128)
    acc = pad_ref[pl.ds(8 - (CONV_K - 1), tm), cs] * w_ref[0:1, cs]
    for j in range(1, CONV_K):
        acc = acc + pad_ref[pl.ds(8 - (CONV_K - 1) + j, tm), cs] * w_ref[j:j + 1, cs]
    return acc


def conv_qkv_fwd(name, proj, conv_w, a_log, dt_bias):
    t = proj.shape[0]
    tm = _rows(t)
    scale = HEAD_DIM ** -0.5

    def body(cur_ref, prev_ref, w_ref, bd_ref, al_ref, db_ref, q_ref, k_ref, v_ref, bg_ref, pad_ref):
        _fill_pad(pad_ref, prev_ref, cur_ref, tm)
        for cb in range(3 * DN_HEADS):
            s = _silu(_conv_block(pad_ref, w_ref, cb, tm))
            hs = slice((cb % DN_HEADS) * 128, (cb % DN_HEADS + 1) * 128)
            if cb < DN_HEADS:
                q_ref[:, hs] = s * (lax.rsqrt(jnp.sum(s * s, axis=-1, keepdims=True) + EPS) * scale)
            elif cb < 2 * DN_HEADS:
                k_ref[:, hs] = s * lax.rsqrt(jnp.sum(s * s, axis=-1, keepdims=True) + EPS)
            else:
                v_ref[:, hs] = s
        bd = bd_ref[...]
        beta = _sigmoid(bd)
        g = -jnp.exp(al_ref[...]) * _softplus(bd + db_ref[...])
        lane = lax.broadcasted_iota(jnp.int32, bd.shape, 1)
        bg_ref[...] = jnp.where(lane < DN_HEADS, beta, jnp.where(lane < 2 * DN_HEADS, g, 0.0))

    w3 = 3 * D_MODEL
    return _call(body, name, (_sds((t, D_MODEL)),) * 3 + (_sds((t, 128)),), (t // tm,),
                 [pl.BlockSpec((tm, w3), lambda i: (i, 0)), _prev_spec(tm, w3), _acc_spec(CONV_K, w3),
                  pl.BlockSpec((tm, 128), lambda i: (i, 8192 // 128)), _acc_spec(1, 128), _acc_spec(1, 128)],
                 (_row_spec(tm, D_MODEL),) * 3 + (_row_spec(tm, 128),),
                 scratch=[pltpu.VMEM((tm + 8, w3), f32)])(proj, proj, conv_w, proj, a_log, dt_bias)


def conv_qkv_bwd(name, proj, conv_w, a_log, dt_bias, dq, dk, dv, dbg):
    t = proj.shape[0]
    tm = _rows(t)
    scale = HEAD_DIM ** -0.5

    def body(cur_ref, prev_ref, w_ref, bd_ref, al_ref, db_ref, dq_ref, dk_ref, dv_ref, dbg_ref,
             dc_ref, dw_ref, dbd_ref, dab_ref, pad_ref):
        _fill_pad(pad_ref, prev_ref, cur_ref, tm)

        @pl.when(pl.program_id(0) == 0)
        def _():
            dw_ref[...] = jnp.zeros_like(dw_ref)

        for cb in range(3 * DN_HEADS):
            cs = slice(cb * 128, (cb + 1) * 128)
            hs = slice((cb % DN_HEADS) * 128, (cb % DN_HEADS + 1) * 128)
            taps = [pad_ref[pl.ds(8 - (CONV_K - 1) + j, tm), cs] for j in range(CONV_K)]
            c = taps[0] * w_ref[0:1, cs]
            for j in range(1, CONV_K):
                c = c + taps[j] * w_ref[j:j + 1, cs]
            sg = _sigmoid(c)
            s = c * sg
            if cb < 2 * DN_HEADS:
                dn = (dq_ref[:, hs] * scale) if cb < DN_HEADS else dk_ref[:, hs]
                r = lax.rsqrt(jnp.sum(s * s, axis=-1, keepdims=True) + EPS)
                ds = r * dn - s * (r * r * r) * jnp.sum(dn * s, axis=-1, keepdims=True)
            else:
                ds = dv_ref[:, hs]
            dc = ds * (sg * (1.0 + c * (1.0 - sg)))
            dc_ref[:, cs] = dc
            for j in range(CONV_K):
                dw_ref[j:j + 1, cs] += jnp.sum(dc * taps[j], axis=0, keepdims=True)
        bd = bd_ref[...]
        dbg_v = dbg_ref[...]
        lane = lax.broadcasted_iota(jnp.int32, bd.shape, 1)
        sg = _sigmoid(bd)
        ea = jnp.exp(al_ref[...])
        z = bd + db_ref[...]
        sp = _softplus(z)
        is_b = lane < DN_HEADS
        is_g = jnp.logical_and(lane >= DN_HEADS, lane < 2 * DN_HEADS)
        d_z = jnp.where(is_g, dbg_v * (-ea) * _sigmoid(z), 0.0)
        dbd_ref[...] = jnp.where(is_b, dbg_v * sg * (1.0 - sg), d_z)
        d_al = jnp.sum(jnp.where(is_g, dbg_v * (-ea) * sp, 0.0), axis=0, keepdims=True)
        d_db = jnp.sum(d_z, axis=0, keepdims=True)
        _accumulate(dab_ref, jnp.concatenate([d_al, d_db] + [jnp.zeros_like(d_al)] * 6, axis=0))

    w3 = 3 * D_MODEL
    return _call(body, name, (_sds((t, w3)), _sds((CONV_K, w3)), _sds((t, 128)), _sds((8, 128))), (t // tm,),
                 [pl.BlockSpec((tm, w3), lambda i: (i, 0)), _prev_spec(tm, w3), _acc_spec(CONV_K, w3),
                  pl.BlockSpec((tm, 128), lambda i: (i, 8192 // 128)), _acc_spec(1, 128), _acc_spec(1, 128),
                  _row_spec(tm, D_MODEL), _row_spec(tm, D_MODEL), _row_spec(tm, D_MODEL), _row_spec(tm, 128)],
                 (_row_spec(tm, w3), _acc_spec(CONV_K, w3), _row_spec(tm, 128), _acc_spec(8, 128)),
                 scratch=[pltpu.VMEM((tm + 8, w3), f32)])(proj, proj, conv_w, proj, a_log, dt_bias, dq, dk, dv, dbg)


def _bdg(a, b, ca, cb, prec=None):
    if prec is None:
        a, b = a.astype(bf16), b.astype(bf16)
    return lax.dot_general(a, b, (((ca,), (cb,)), ((0,), (0,))), precision=prec, preferred_element_type=f32)


def _bnn(a, b, prec=None):
    return _bdg(a, b, 2, 1, prec)


def _bnt(a, b, prec=None):
    return _bdg(a, b, 2, 2, prec)


def _btn(a, b, prec=None):
    return _bdg(a, b, 1, 1, prec)


def _delta_local(q, k, v, g_row, b_row, solved=None):
    c = CHUNK
    ii = lax.broadcasted_iota(jnp.int32, (c, c), 0)
    jj = lax.broadcasted_iota(jnp.int32, (c, c), 1)
    eye, lower, strict = ii == jj, ii >= jj, ii > jj
    shp = (q.shape[0], c, c)
    g_b = jnp.broadcast_to(g_row, shp)
    gc_col = jnp.sum(jnp.where(lower, g_b, 0.0), axis=2, keepdims=True)
    gc_row = jnp.sum(jnp.where(eye, jnp.broadcast_to(gc_col, shp), 0.0), axis=1, keepdims=True)
    b_col = jnp.sum(jnp.where(eye, jnp.broadcast_to(b_row, shp), 0.0), axis=2, keepdims=True)
    gl = jnp.sum(g_row, axis=2, keepdims=True)
    decay = jnp.exp(jnp.where(lower, gc_col - gc_row, -1e30))
    kb = k * b_col
    kk = _bnt(kb, k)
    lmat = jnp.where(strict, kk * decay, 0.0)
    egc = jnp.exp(gc_col)
    rhs_w = kb * egc
    if solved is None:
        tinv = eye.astype(f32) - lmat
        pw = lmat
        for _ in range(5):
            pw = _bnn(pw, pw, HIGH)
            tinv = tinv + _bnn(tinv, pw, HIGH)
        u = _bnn(tinv, v * b_col, HIGH)
        w = _bnn(tinv, rhs_w, HIGH)
    else:
        tinv, u, w = solved
    qk = _bnt(q, k)
    amat = jnp.where(lower, qk * decay, 0.0)
    q_dec = q * egc
    kdf = jnp.exp(gl - gc_col)
    k_dec = k * kdf
    return dict(eye=eye, lower=lower, strict=strict, gc_col=gc_col, b_col=b_col, gl=gl, decay=decay,
                kb=kb, kk=kk, tinv=tinv, egc=egc, rhs_w=rhs_w, u=u, w=w, qk=qk, amat=amat, q_dec=q_dec, kdf=kdf,
                k_dec=k_dec)


def _heads(ref, hb):
    return jnp.stack([ref[:, h * HEAD_DIM:(h + 1) * HEAD_DIM] for h in range(hb)])


def _put_heads(ref, val):
    for h in range(val.shape[0]):
        ref[:, h * HEAD_DIM:(h + 1) * HEAD_DIM] = val[h]


def delta_fwd(name, q, k, v, g_rows, b_rows, next_shard=None):
    t = q.shape[0]
    nc = t // CHUNK
    hb = DELTA_HB
    fused = next_shard is not None
    assert not fused or (hb == DN_HEADS and nc >= 2)

    def body(*refs):
        if fused:
            (q_ref, k_ref, v_ref, g_ref, b_ref, x_ref, o_ref, s_ref, t_ref, u_ref, w_ref, gath_ref, state, send_sems,
             recv_sems, local_sem) = refs
        else:
            q_ref, k_ref, v_ref, g_ref, b_ref, o_ref, s_ref, t_ref, u_ref, w_ref, state = refs
        n = pl.program_id(1)
        if fused:
            start, forward, finish = _gather_steps(x_ref, gath_ref, send_sems, recv_sems, local_sem)
            pl.when(n == 0)(start)

        @pl.when(n == 0)
        def _():
            state[...] = jnp.zeros_like(state)

        loc = _delta_local(_heads(q_ref, hb), _heads(k_ref, hb), _heads(v_ref, hb), g_ref[:, pl.ds(n, 1), :],
                           b_ref[:, pl.ds(n, 1), :])
        s0 = state[...]
        s_ref[...] = s0
        t_ref[...] = loc["tinv"]
        _put_heads(u_ref, loc["u"])
        _put_heads(w_ref, loc["w"])
        v_new = loc["u"] - _bnn(loc["w"], s0)
        _put_heads(o_ref, _bnn(loc["q_dec"], s0) + _bnn(loc["amat"], v_new))
        state[...] = s0 * jnp.exp(loc["gl"]) + _btn(loc["k_dec"], v_new)
        if fused:
            pl.when(n == (3 * nc) // 4)(forward)
            pl.when(n == nc - 1)(finish)

    tok = pl.BlockSpec((CHUNK, hb * HEAD_DIM), lambda h, n: (n, h))
    row = pl.BlockSpec((hb, nc, CHUNK), lambda h, n: (h, 0, 0))
    any_spec = pl.BlockSpec(memory_space=pl.ANY)
    outs = (_sds((t, D_MODEL)), _sds((DN_HEADS, nc, HEAD_DIM, HEAD_DIM)), _sds((DN_HEADS, nc, CHUNK, CHUNK)),
            _sds((t, D_MODEL)), _sds((t, D_MODEL)))
    out_specs = (tok, pl.BlockSpec((hb, None, HEAD_DIM, HEAD_DIM), lambda h, n: (h, n, 0, 0)),
                 pl.BlockSpec((hb, None, CHUNK, CHUNK), lambda h, n: (h, n, 0, 0)), tok, tok)
    in_specs, args = [tok, tok, tok, row, row], (q, k, v, g_rows, b_rows)
    scratch = [pltpu.VMEM((hb, HEAD_DIM, HEAD_DIM), f32)]
    if fused:
        outs += (_sds((N_DEV,) + next_shard.shape, next_shard.dtype),)
        out_specs += (any_spec,)
        in_specs, args = in_specs + [any_spec], args + (next_shard,)
        scratch = scratch + list(_GATHER_SEMS)
    return _call(body, name, outs, (DN_HEADS // hb, nc), in_specs, out_specs, scratch=scratch)(*args)


def delta_bwd(name, q, k, v, g_rows, b_rows, s_all, t_all, u_all, w_all, do, chip_part=None):
    t = q.shape[0]
    nc = t // CHUNK
    c = CHUNK
    hb = DELTA_HB

    fused = chip_part is not None
    assert not fused or hb == DN_HEADS

    def body(*refs):
        if fused:
            (q_ref, k_ref, v_ref, g_ref, b_ref, s_ref, t_ref, u_ref, w_ref, do_ref, x_ref, dq_ref, dk_ref, dv_ref,
             dg_ref, db_ref, slot_ref, dstate, send_sems, recv_sems, local_sem) = refs
        else:
            (q_ref, k_ref, v_ref, g_ref, b_ref, s_ref, t_ref, u_ref, w_ref, do_ref, dq_ref, dk_ref, dv_ref, dg_ref,
             db_ref, dstate) = refs
        step = pl.program_id(1)
        n = nc - 1 - step
        if fused:
            start, finish = _chip_exchange_steps(x_ref, slot_ref, send_sems, recv_sems, local_sem)
            pl.when(step == 0)(start)

        @pl.when(step == 0)
        def _():
            dstate[...] = jnp.zeros_like(dstate)

        qv, kv, vv = _heads(q_ref, hb), _heads(k_ref, hb), _heads(v_ref, hb)
        L = _delta_local(qv, kv, vv, g_ref[:, pl.ds(n, 1), :], b_ref[:, pl.ds(n, 1), :],
                         solved=(t_ref[...], _heads(u_ref, hb), _heads(w_ref, hb)))
        eye, lower, strict = L["eye"], L["lower"], L["strict"]
        shp = (hb, c, c)
        s0 = s_ref[...]
        dov = _heads(do_ref, hb)
        ds = dstate[...]
        eg = jnp.exp(L["gl"])
        v_new = L["u"] - _bnn(L["w"], s0)
        d_k_dec = _bnt(v_new, ds)
        d_v_new = _bnn(L["k_dec"], ds) + _btn(L["amat"], dov)
        d_eg = jnp.sum(jnp.sum(ds * s0, axis=2, keepdims=True), axis=1, keepdims=True)
        d_q_dec = _bnt(dov, s0)
        d_a = _bnt(dov, v_new)
        d_w = -_bnt(d_v_new, s0)
        dstate[...] = ds * eg + _btn(L["q_dec"], dov) - _btn(L["w"], d_v_new)
        d_am = jnp.where(lower, d_a * L["decay"], 0.0)
        dq = _bnn(d_am, kv) + d_q_dec * L["egc"]
        dk = _btn(d_am, qv) + d_k_dec * L["kdf"]
        e_col = jnp.sum(d_k_dec * L["k_dec"], axis=2, keepdims=True)
        d_gc_col = jnp.sum(d_q_dec * L["q_dec"], axis=2, keepdims=True) - e_col
        d_gl = jnp.sum(e_col, axis=1, keepdims=True) + d_eg * eg
        tinv = L["tinv"]
        d_rhs_u = _btn(tinv, d_v_new, HIGH)
        d_rhs_w = _btn(tinv, d_w, HIGH)
        d_l = -(_bnt(d_rhs_u, L["u"], HIGH) + _bnt(d_rhs_w, L["w"], HIGH))
        _put_heads(dv_ref, d_rhs_u * L["b_col"])
        d_b_col = jnp.sum(d_rhs_u * vv, axis=2, keepdims=True)
        d_gc_col = d_gc_col + jnp.sum(d_rhs_w * L["rhs_w"], axis=2, keepdims=True)
        d_lm = jnp.where(strict, d_l * L["decay"], 0.0)
        d_kb = d_rhs_w * L["egc"] + _bnn(d_lm, kv)
        dk = dk + _btn(d_lm, L["kb"]) + d_kb * L["b_col"]
        d_b_col = d_b_col + jnp.sum(d_kb * kv, axis=2, keepdims=True)
        m = d_am * L["qk"] + d_lm * L["kk"]
        d_gc_col = d_gc_col + jnp.sum(m, axis=2, keepdims=True)
        d_gc_row = (jnp.sum(jnp.where(eye, jnp.broadcast_to(d_gc_col, shp), 0.0), axis=1, keepdims=True)
                    - jnp.sum(m, axis=1, keepdims=True))
        lane = lax.broadcasted_iota(jnp.int32, (1, 1, c), 2)
        d_gc_row = d_gc_row + jnp.where(lane == c - 1, d_gl, 0.0)
        d_gc_tot = jnp.sum(jnp.where(eye, jnp.broadcast_to(d_gc_row, shp), 0.0), axis=2, keepdims=True)
        dg_ref[:, pl.ds(n, 1), :] = jnp.sum(jnp.where(lower, jnp.broadcast_to(d_gc_tot, shp), 0.0), axis=1,
                                            keepdims=True)
        db_ref[:, pl.ds(n, 1), :] = jnp.sum(jnp.where(eye, jnp.broadcast_to(d_b_col, shp), 0.0), axis=1,
                                            keepdims=True)
        _put_heads(dq_ref, dq)
        _put_heads(dk_ref, dk)
        if fused:
            pl.when(step == nc - 1)(finish)

    tok = pl.BlockSpec((CHUNK, hb * HEAD_DIM), lambda h, s: (nc - 1 - s, h))
    row = pl.BlockSpec((hb, nc, CHUNK), lambda h, s: (h, 0, 0))
    any_spec = pl.BlockSpec(memory_space=pl.ANY)
    in_specs = [tok, tok, tok, row, row,
                pl.BlockSpec((hb, None, HEAD_DIM, HEAD_DIM), lambda h, s: (h, nc - 1 - s, 0, 0)),
                pl.BlockSpec((hb, None, CHUNK, CHUNK), lambda h, s: (h, nc - 1 - s, 0, 0)), tok, tok, tok]
    args = (q, k, v, g_rows, b_rows, s_all, t_all, u_all, w_all, do)
    outs = (_sds((t, D_MODEL)),) * 3 + (_sds((DN_HEADS, nc, CHUNK)),) * 2
    out_specs = (tok, tok, tok, row, row)
    scratch = [pltpu.VMEM((hb, HEAD_DIM, HEAD_DIM), f32)]
    if fused:
        in_specs, args = in_specs + [any_spec], args + (chip_part,)
        outs += (_sds(chip_part.shape, chip_part.dtype),)
        out_specs += (any_spec,)
        scratch = scratch + list(_CHIP_SEMS)
    return _call(body, name, outs, (DN_HEADS // hb, nc), in_specs, out_specs, scratch=scratch)(*args)


SEG = 8


def _perm_rows(a):
    t, c = a.shape
    return a.reshape(SEG, t // SEG, c).transpose(1, 0, 2).reshape(t, c)


def _unperm_rows(a):
    t, c = a.shape
    return a.reshape(t // SEG, SEG, c).transpose(1, 0, 2).reshape(t, c)


def _cmul(ar, ai, br, bi):
    return ar * br - ai * bi, ar * bi + ai * br


def _segment_init(er, ei, lr, li, seg_len, reverse):
    w = er.shape[1]
    sub = lax.broadcasted_iota(jnp.int32, (SEG, w), 0)

    def shift(x, k):
        if reverse:
            return jnp.where(sub < SEG - k, pltpu.roll(x, SEG - k, 0), 0.0)
        return jnp.where(sub >= k, pltpu.roll(x, k, 0), 0.0)

    pr, pi = lr, li
    for _ in range(seg_len.bit_length() - 1):
        pr, pi = _cmul(pr, pi, pr, pi)
    fr, fi = shift(er, 1), shift(ei, 1)
    for k in (1, 2, 4):
        sr, si = shift(fr, k), shift(fi, k)
        mr, mi = _cmul(pr, pi, sr, si)
        fr, fi = fr + mr, fi + mi
        pr, pi = _cmul(pr, pi, pr, pi)
    return fr, fi


def s5_fwd(name, u_perm, bbd, cbd, lam):
    t = u_perm.shape[0]
    tt = _tile(t, (1024, 512, 256, 128))
    nt, ng, w = t // tt, tt // SEG, SG_STATE
    seg_len = t // SEG
    assert seg_len & (seg_len - 1) == 0 and tt % SEG == 0

    def body(u_ref, b_ref, c_ref, lam_ref, y_ref, h_ref, x_scr, state):
        p, i = pl.program_id(1), pl.program_id(2)
        lr1, li1 = lam_ref[:, 0:w], lam_ref[:, w:2 * w]
        lr, li = jnp.broadcast_to(lr1, (SEG, w)), jnp.broadcast_to(li1, (SEG, w))
        x_scr[...] = _bdot(u_ref[...], b_ref[...], ((1,), (0,)))

        @pl.when(jnp.logical_and(p == 0, i == 0))
        def _():
            state[...] = jnp.zeros_like(state)

        @pl.when(jnp.logical_and(p == 1, i == 0))
        def _():
            sr, si = _segment_init(state[:, 0:w], state[:, w:2 * w], lr1, li1, seg_len, False)
            state[:, 0:w] = sr
            state[:, w:2 * w] = si

        def run(store):
            def step(g, st):
                row = pl.multiple_of(g * SEG, SEG)
                xg = x_scr[pl.ds(row, SEG), :]
                nr = lr * st[0] - li * st[1] + xg[:, 0:w]
                ni = lr * st[1] + li * st[0] + xg[:, w:2 * w]
                if store:
                    h_ref[pl.ds(row, SEG), 0:w] = nr
                    h_ref[pl.ds(row, SEG), w:2 * w] = ni
                return nr, ni

            fin = lax.fori_loop(0, ng, step, (state[:, 0:w], state[:, w:2 * w]))
            state[:, 0:w] = fin[0]
            state[:, w:2 * w] = fin[1]

        @pl.when(p == 0)
        def _():
            run(False)

        @pl.when(p == 1)
        def _():
            run(True)
            y_ref[...] = _bdot(h_ref[...], c_ref[...], ((1,), (0,)))

    return _call(body, name, (_sds((t, D_MODEL)), _sds((t, N_SUPER * 2 * w))), (N_SUPER, 2, nt),
                 [pl.BlockSpec((tt, 128), lambda s, p, i: (i, s)),
                  pl.BlockSpec((None, 128, 2 * w), lambda s, p, i: (s, 0, 0)),
                  pl.BlockSpec((None, 2 * w, 128), lambda s, p, i: (s, 0, 0)),
                  pl.BlockSpec((None, 1, 2 * w), lambda s, p, i: (s, 0, 0))],
                 (pl.BlockSpec((tt, 128), lambda s, p, i: (i * p, s)),
                  pl.BlockSpec((tt, 2 * w), lambda s, p, i: (i * p, s))),
                 scratch=[pltpu.VMEM((tt, 2 * w), f32), pltpu.VMEM((SEG, 2 * w), f32)])(u_perm, bbd, cbd, lam)


def s5_bwd(name, dy_perm, u_perm, h_perm, hprev0, bbd, cbd, lam):
    t = u_perm.shape[0]
    tt = _tile(t, (1024, 512, 256, 128))
    nt, ng, w = t // tt, tt // SEG, SG_STATE
    seg_len = t // SEG

    def body(dy_ref, u_ref, h_ref, hp_ref, hp0_ref, b_ref, c_ref, lam_ref, du_ref, db_ref, dc_ref, dl_ref,
             g_scr, state, dl_acc):
        p, i = pl.program_id(1), pl.program_id(2)
        first_tile = jnp.logical_or(p == 0, i == nt - 1)
        lr1, li1 = lam_ref[:, 0:w], -lam_ref[:, w:2 * w]
        lr, li = jnp.broadcast_to(lr1, (SEG, w)), jnp.broadcast_to(li1, (SEG, w))
        g_scr[...] = _bdot(dy_ref[...], c_ref[...], ((1,), (1,)))

        @pl.when(jnp.logical_and(p == 0, i == 0))
        def _():
            state[...] = jnp.zeros_like(state)

        @pl.when(jnp.logical_and(p == 1, i == 0))
        def _():
            sr, si = _segment_init(state[:, 0:w], state[:, w:2 * w], lr1, li1, seg_len, True)
            state[:, 0:w] = sr
            state[:, w:2 * w] = si
            dl_acc[...] = jnp.zeros_like(dl_acc)

        def adj(g, st):
            row = pl.multiple_of(g * SEG, SEG)
            gg = g_scr[pl.ds(row, SEG), :]
            nr = lr * st[0] - li * st[1] + gg[:, 0:w]
            ni = lr * st[1] + li * st[0] + gg[:, w:2 * w]
            return row, nr, ni

        @pl.when(p == 0)
        def _():
            def step(k, st):
                _, nr, ni = adj(ng - 1 - k, st)
                return nr, ni

            fin = lax.fori_loop(0, ng, step, (state[:, 0:w], state[:, w:2 * w]))
            state[:, 0:w] = fin[0]
            state[:, w:2 * w] = fin[1]

        @pl.when(p == 1)
        def _():
            above = jnp.where(first_tile, hp0_ref[...], hp_ref[...])

            def step(k, st):
                g = ng - 1 - k
                row, nr, ni = adj(g, st)
                g_scr[pl.ds(row, SEG), 0:w] = nr
                g_scr[pl.ds(row, SEG), w:2 * w] = ni
                prow = pl.multiple_of(jnp.maximum(g - 1, 0) * SEG, SEG)
                hp = jnp.where(g > 0, h_ref[pl.ds(prow, SEG), :], above)
                pr, pi = hp[:, 0:w], hp[:, w:2 * w]
                return nr, ni, st[2] + nr * pr + ni * pi, st[3] + ni * pr - nr * pi

            fin = lax.fori_loop(0, ng, step, (state[:, 0:w], state[:, w:2 * w], dl_acc[:, 0:w], dl_acc[:, w:2 * w]))
            state[:, 0:w] = fin[0]
            state[:, w:2 * w] = fin[1]
            dl_acc[:, 0:w] = fin[2]
            dl_acc[:, w:2 * w] = fin[3]
            a = g_scr[...]
            du_ref[...] = _bdot(a, b_ref[...], ((1,), (1,)))
            d_b = _bdot(u_ref[...], a, ((0,), (0,)))
            d_c = _bdot(dy_ref[...], h_ref[...], ((0,), (0,)))

            @pl.when(i == 0)
            def _():
                db_ref[...] = d_b
                dc_ref[...] = d_c

            @pl.when(i > 0)
            def _():
                db_ref[...] += d_b
                dc_ref[...] += d_c

            @pl.when(i == nt - 1)
            def _():
                dl_ref[...] = jnp.sum(dl_acc[...], axis=0, keepdims=True)

    tile = lambda s, p, i: (nt - 1 - i, s)
    tile1 = lambda s, p, i: (nt - 1 - i * p, s)
    above = lambda s, p, i: (jnp.maximum((nt - 1 - i * p) * (tt // SEG) - 1, 0), s)
    per_s = lambda s, p, i: (s, 0, 0)
    return _call(body, name, (_sds((t, D_MODEL)), _sds((N_SUPER, 128, 2 * w)), _sds((N_SUPER, 128, 2 * w)),
                              _sds((N_SUPER, 1, 2 * w))), (N_SUPER, 2, nt),
                 [pl.BlockSpec((tt, 128), tile), pl.BlockSpec((tt, 128), tile1), pl.BlockSpec((tt, 2 * w), tile1),
                  pl.BlockSpec((SEG, 2 * w), above), pl.BlockSpec((SEG, 2 * w), lambda s, p, i: (0, s)),
                  pl.BlockSpec((None, 128, 2 * w), per_s), pl.BlockSpec((None, 2 * w, 128), per_s),
                  pl.BlockSpec((None, 1, 2 * w), per_s)],
                 (pl.BlockSpec((tt, 128), tile1), pl.BlockSpec((None, 128, 2 * w), per_s),
                  pl.BlockSpec((None, 128, 2 * w), per_s), pl.BlockSpec((None, 1, 2 * w), per_s)),
                 scratch=[pltpu.VMEM((tt, 2 * w), f32), pltpu.VMEM((SEG, 2 * w), f32),
                          pltpu.VMEM((SEG, 2 * w), f32)])(dy_perm, u_perm, h_perm, h_perm, hprev0, bbd, cbd, lam)


def _s5_disc(a_re, a_im, log_dt, br, bi):
    dt = jnp.exp(log_dt)
    mag = jnp.exp(a_re * dt)
    lr, li = mag * jnp.cos(a_im * dt), mag * jnp.sin(a_im * dt)
    den = a_re * a_re + a_im * a_im
    fr = ((lr - 1.0) * a_re + li * a_im) / den
    fi = (li * a_re - (lr - 1.0) * a_im) / den
    return lr, li, fr * br - fi * bi, fr * bi + fi * br


def s5_disc_fwd(name, a_re, a_im, log_dt, br, bi):
    g, n = SSM_GROUPS, SSM_STATE

    def body(ar, ai, ld, brr, bir, lr, li, bbr, bbi):
        o = _s5_disc(ar[...], ai[...], ld[...], brr[...], bir[...])
        lr[...], li[...], bbr[...], bbi[...] = o

    return _call(body, name, (_sds((g, 1, n)), _sds((g, 1, n)), _sds((g, SSM_GROUP, n)), _sds((g, SSM_GROUP, n))))(
        a_re, a_im, log_dt, br, bi)


def s5_disc_bwd(name, a_re, a_im, log_dt, br, bi, d_lr, d_li, d_bbr, d_bbi):
    g, n = SSM_GROUPS, SSM_STATE

    def body(ar, ai, ld, brr, bir, c1, c2, c3, c4, o1, o2, o3, o4, o5):
        _, vjp = jax.vjp(_s5_disc, ar[...], ai[...], ld[...], brr[...], bir[...])
        o1[...], o2[...], o3[...], o4[...], o5[...] = vjp((c1[...], c2[...], c3[...], c4[...]))

    return _call(body, name, (_sds((g, 1, n)), _sds((g, 1, n)), _sds((g, 1, 1)), _sds((g, SSM_GROUP, n)),
                              _sds((g, SSM_GROUP, n))))(a_re, a_im, log_dt, br, bi, d_lr, d_li, d_bbr, d_bbi)


def gelu_fwd(name, ys_lin, proj, d_skip):
    t, d = ys_lin.shape
    tm = _rows(t, light=True)

    def body(y_ref, u_ref, d_ref, o_ref):
        o_ref[...] = _gelu(y_ref[...] + d_ref[...] * u_ref[...])

    return _call(body, name, _sds((t, d)), (t // tm,),
                 [_row_spec(tm, d), pl.BlockSpec((tm, d), lambda i: (i, 4)), _acc_spec(1, d)],
                 _row_spec(tm, d))(ys_lin, proj, d_skip)


def _head_norm(o, hn):
    outs, ns, rs = [], [], []
    for h in range(DN_HEADS):
        oh = o[:, h * 128:(h + 1) * 128]
        r = lax.rsqrt(jnp.mean(oh * oh, axis=-1, keepdims=True) + EPS)
        n = oh * r
        ns.append(n)
        rs.append(r)
        outs.append(n * hn)
    return outs, ns, rs


def merge_fwd(name, proj, o, yg, glu_lin, head_norm, b_glu):
    t = o.shape[0]
    tm = _rows(t)
    d = D_MODEL

    def body(za_ref, zb_ref, ra_ref, rb_ref, o_ref, yg_ref, gl_ref, hn_ref, bg_ref, m_ref):
        on, _, _ = _head_norm(o_ref[...], hn_ref[...])
        y_a = jnp.concatenate(on, axis=1) * _silu(za_ref[...])
        y_b = yg_ref[...] * _sigmoid(gl_ref[...] + bg_ref[...]) * _silu(zb_ref[...])
        m_ref[...] = (_sigmoid(ra_ref[...]) * y_a + _sigmoid(rb_ref[...]) * y_b).astype(bf16)

    pc = lambda c: pl.BlockSpec((tm, d), lambda i: (i, c))
    return _call(body, name, _sds((t, d), bf16), (t // tm,),
                 [pc(3), pc(5), pc(6), pc(7), _row_spec(tm, d), _row_spec(tm, d), _row_spec(tm, d),
                  _acc_spec(1, 128), _acc_spec(1, d)], _row_spec(tm, d))(
        proj, proj, proj, proj, o, yg, glu_lin, head_norm, b_glu)


def merge_bwd(name, proj, o, yg, glu_lin, head_norm, b_glu, dm):
    t = o.shape[0]
    tm = _rows(t)
    d = D_MODEL

    def body(za_ref, zb_ref, ra_ref, rb_ref, o_ref, yg_ref, gl_ref, hn_ref, bg_ref, dm_ref,
             dza_ref, dzb_ref, dra_ref, drb_ref, do_ref, dgl_ref, dyg_ref, dhn_ref, dbg_ref):
        hn = hn_ref[...]
        za, zb, ra, rb = za_ref[...], zb_ref[...], ra_ref[...], rb_ref[...]
        on, ns, rs = _head_norm(o_ref[...], hn)
        onc = jnp.concatenate(on, axis=1)
        sza = _silu(za)
        y_a = onc * sza
        yg = yg_ref[...]
        sgl = _sigmoid(gl_ref[...] + bg_ref[...])
        y2 = yg * sgl
        szb = _silu(zb)
        y_b = y2 * szb
        sra, srb = _sigmoid(ra), _sigmoid(rb)
        dmv = dm_ref[...]
        dra_ref[...] = (dmv * y_a * sra * (1.0 - sra)).astype(bf16)
        drb_ref[...] = (dmv * y_b * srb * (1.0 - srb)).astype(bf16)
        d_ya = dmv * sra
        d_yb = dmv * srb
        dza_ref[...] = (d_ya * onc * _dsilu(za)).astype(bf16)
        dzb_ref[...] = (d_yb * y2 * _dsilu(zb)).astype(bf16)
        d_on = d_ya * sza
        d_y2 = d_yb * szb
        dyg_ref[...] = d_y2 * sgl
        d_gl = d_y2 * yg * sgl * (1.0 - sgl)
        dgl_ref[...] = d_gl.astype(bf16)
        _accumulate(dbg_ref, jnp.sum(d_gl, axis=0, keepdims=True))
        d_hn = jnp.zeros((1, 128), f32)
        for h in range(DN_HEADS):
            hs = slice(h * 128, (h + 1) * 128)
            dh = d_on[:, hs]
            d_hn = d_hn + jnp.sum(dh * ns[h], axis=0, keepdims=True)
            dn = dh * hn
            do_ref[:, hs] = rs[h] * (dn - ns[h] * jnp.mean(dn * ns[h], axis=-1, keepdims=True))
        _accumulate(dhn_ref, d_hn)

    pc = lambda c: pl.BlockSpec((tm, d), lambda i: (i, c))
    rs_ = _row_spec(tm, d)
    return _call(body, name, (_sds((t, d), bf16),) * 4 + (_sds((t, d)), _sds((t, d), bf16), _sds((t, d)),
                              _sds((1, 128)), _sds((1, d))),
                 (t // tm,), [pc(3), pc(5), pc(6), pc(7), rs_, rs_, rs_, _acc_spec(1, 128), _acc_spec(1, d), rs_],
                 (rs_,) * 7 + (_acc_spec(1, 128), _acc_spec(1, d)))(
        proj, proj, proj, proj, o, yg, glu_lin, head_norm, b_glu, dm)


def gelu_bwd(name, ys_lin, proj, d_skip, dyg_a, dyg_b):
    t, d = ys_lin.shape
    tm = _rows(t, light=True)

    def body(y_ref, u_ref, d_ref, a_ref, b_ref, dys_ref, du_ref, dd_ref):
        uv = u_ref[...]
        dys = (a_ref[...] + b_ref[...]) * _dgelu(y_ref[...] + d_ref[...] * uv)
        dys_ref[...] = dys
        du_ref[...] = dys * d_ref[...]
        _accumulate(dd_ref, jnp.sum(dys * uv, axis=0, keepdims=True))

    rs_ = _row_spec(tm, d)
    return _call(body, name, (_sds((t, d)), _sds((t, d)), _sds((1, d))), (t // tm,),
                 [rs_, pl.BlockSpec((tm, d), lambda i: (i, 4)), _acc_spec(1, d), rs_, rs_],
                 (rs_, rs_, _acc_spec(1, d)))(ys_lin, proj, d_skip, dyg_a, dyg_b)


def assemble_dproj(name, dc, conv_w, dza, du_a, du_b, dzb, dra, drb, dbd):
    t = dza.shape[0]
    tm = _rows(t)
    nt = t // tm
    d = D_MODEL
    w3 = 3 * d

    def body(cur_ref, nxt_ref, w_ref, za_ref, ua_ref, ub_ref, zb_ref, ra_ref, rb_ref, bd_ref, o_ref, pad_ref):
        keep = (pl.program_id(0) < nt - 1).astype(f32)
        pad_ref[0:tm, :] = cur_ref[...]
        pad_ref[tm:tm + 8, :] = nxt_ref[...] * keep
        for cb in range(w3 // 128):
            cs = slice(cb * 128, (cb + 1) * 128)
            acc = pad_ref[pl.ds(CONV_K - 1, tm), cs] * w_ref[0:1, cs]
            for j in range(1, CONV_K):
                acc = acc + pad_ref[pl.ds(CONV_K - 1 - j, tm), cs] * w_ref[j:j + 1, cs]
            o_ref[:, cs] = acc.astype(bf16)
        o_ref[:, 3 * d:4 * d] = za_ref[...].astype(bf16)
        o_ref[:, 4 * d:5 * d] = (ua_ref[...] + ub_ref[...]).astype(bf16)
        o_ref[:, 5 * d:6 * d] = zb_ref[...].astype(bf16)
        o_ref[:, 6 * d:7 * d] = ra_ref[...].astype(bf16)
        o_ref[:, 7 * d:8 * d] = rb_ref[...].astype(bf16)
        o_ref[:, 8 * d:8 * d + 128] = bd_ref[...].astype(bf16)
        o_ref[:, 8 * d + 128:W_PAD] = jnp.zeros((tm, W_PAD - 8 * d - 128), bf16)

    rs_ = _row_spec(tm, d)
    return _call(body, name, _sds((t, W_PAD), bf16), (nt,),
                 [_row_spec(tm, w3), _next_spec(tm, w3, t), _acc_spec(CONV_K, w3), rs_, rs_, rs_, rs_, rs_, rs_,
                  _row_spec(tm, 128)], _row_spec(tm, W_PAD),
                 scratch=[pltpu.VMEM((tm + 8, w3), f32)])(dc, dc, conv_w, dza, du_a, du_b, dzb, dra, drb, dbd)


def adamw(name, w, g, m, v):
    lead, (r, c) = w.shape[:-2], w.shape[-2:]
    tm = _tile(r, (512, 256, 128, 64, 32, 16, 8))
    c1 = 1.0 / (1.0 - ADAM_B1 ** ADAM_STEP)
    c2 = 1.0 / (1.0 - ADAM_B2 ** ADAM_STEP)

    def body(w_ref, g_ref, m_ref, v_ref, d_ref, nm_ref, nv_ref):
        gv = g_ref[...]
        nm = ADAM_B1 * m_ref[...] + (1.0 - ADAM_B1) * gv
        nv = ADAM_B2 * v_ref[...] + (1.0 - ADAM_B2) * (gv * gv)
        d_ref[...] = -ADAM_LR * ((nm * c1) / (jnp.sqrt(nv * c2) + ADAM_EPS) + ADAM_WD * w_ref[...])
        nm_ref[...] = nm
        nv_ref[...] = nv

    if lead:
        sp = pl.BlockSpec((None, tm, c), lambda l, i: (l, i, 0))
        grid = (lead[0], r // tm)
    else:
        sp = pl.BlockSpec((tm, c), lambda i: (i, 0))
        grid = (r // tm,)
    return _call(body, name, (_sds(w.shape),) * 3, grid, [sp] * 4, (sp,) * 3)(w, g, m, v)


def _coords():
    return lax.axis_index("x"), lax.axis_index("y"), lax.axis_index("c")


def _lin(dev):
    return 4 * dev[0] + 2 * dev[1] + dev[2]


def _chips(me):
    x, y, _ = me
    return [(1 - x, y), (x, 1 - y), (1 - x, 1 - y)]


def _gather_steps(x_ref, o_ref, send_sems, recv_sems, local_sem):
    me = _coords()
    x, y, cc = me
    sibling = (x, y, 1 - cc)
    chips = _chips(me)

    def copy(k, block, to, src=None):
        return pltpu.make_async_remote_copy(
            src_ref=o_ref.at[_lin(block)] if src is None else src, dst_ref=o_ref.at[_lin(block)],
            send_sem=send_sems.at[k], recv_sem=recv_sems.at[k], device_id=to, device_id_type=MESH)

    mine = pltpu.make_async_copy(x_ref, o_ref.at[_lin(me)], local_sem)
    first = [copy(0, me, sibling, src=x_ref)] + [copy(1 + j, me, (*chip, cc), src=x_ref)
                                                 for j, chip in enumerate(chips)]
    passed = [copy(4 + j, (*chip, cc), sibling) for j, chip in enumerate(chips)]

    def start():
        mine.start()
        for cp in first:
            cp.start()

    def forward():
        for j, chip in enumerate(chips):
            copy(1 + j, (*chip, cc), me).wait_recv()
            passed[j].start()

    def finish():
        copy(0, sibling, me).wait_recv()
        for j, chip in enumerate(chips):
            copy(4 + j, (*chip, 1 - cc), me).wait_recv()
        for cp in first + passed:
            cp.wait_send()
        mine.wait()

    return start, forward, finish


_GATHER_SEMS = [pltpu.SemaphoreType.DMA((N_DEV - 1,)), pltpu.SemaphoreType.DMA((N_DEV - 1,)),
                pltpu.SemaphoreType.DMA(())]


def all_gather(name, shard):
    r, c = shard.shape

    def body(x_ref, o_ref, send_sems, recv_sems, local_sem):
        for step in _gather_steps(x_ref, o_ref, send_sems, recv_sems, local_sem):
            step()

    any_spec = pl.BlockSpec(memory_space=pl.ANY)
    return _call(body, name, _sds((N_DEV, r, c), shard.dtype), in_specs=[any_spec], out_specs=any_spec,
                 scratch=list(_GATHER_SEMS))(shard)


def pair_exchange(name, blocks):
    _, _, r, c = blocks.shape

    def body(x_ref, o_ref, send_sems, recv_sems):
        x, y, cc = _coords()
        sibling = (x, y, 1 - cc)
        cps = [pltpu.make_async_remote_copy(src_ref=x_ref.at[ch, 1 - cc], dst_ref=o_ref.at[ch], send_sem=send_sems.at[ch],
                                            recv_sem=recv_sems.at[ch], device_id=sibling, device_id_type=MESH)
               for ch in range(4)]
        for cp in cps:
            cp.start()
        for cp in cps:
            cp.wait()

    any_spec = pl.BlockSpec(memory_space=pl.ANY)
    return _call(body, name, _sds((4, r, c), blocks.dtype), in_specs=[any_spec], out_specs=any_spec,
                 scratch=[pltpu.SemaphoreType.DMA((4,)), pltpu.SemaphoreType.DMA((4,))])(blocks)


def _chip_exchange_steps(x_ref, o_ref, send_sems, recv_sems, local_sem):
    me = _coords()
    x, y, cc = me
    my_chip = 2 * x + y
    mine = pltpu.make_async_copy(x_ref.at[my_chip], o_ref.at[my_chip], local_sem)
    peers = [(px, py, cc) for px, py in _chips(me)]
    sends = [pltpu.make_async_remote_copy(src_ref=x_ref.at[2 * px + py], dst_ref=o_ref.at[my_chip],
                                          send_sem=send_sems.at[j], recv_sem=recv_sems.at[j], device_id=(px, py, pc),
                                          device_id_type=MESH) for j, (px, py, pc) in enumerate(peers)]

    def start():
        mine.start()
        for cp in sends:
            cp.start()

    def finish():
        for j, (px, py, pc) in enumerate(peers):
            pltpu.make_async_remote_copy(src_ref=x_ref.at[my_chip], dst_ref=o_ref.at[2 * px + py],
                                         send_sem=send_sems.at[j], recv_sem=recv_sems.at[j], device_id=(px, py, pc),
                                         device_id_type=MESH).wait_recv()
        for cp in sends:
            cp.wait_send()
        mine.wait()

    return start, finish


_CHIP_SEMS = [pltpu.SemaphoreType.DMA((3,)), pltpu.SemaphoreType.DMA((3,)), pltpu.SemaphoreType.DMA(())]


def chip_exchange(name, blocks):
    def body(x_ref, o_ref, send_sems, recv_sems, local_sem):
        for step in _chip_exchange_steps(x_ref, o_ref, send_sems, recv_sems, local_sem):
            step()

    any_spec = pl.BlockSpec(memory_space=pl.ANY)
    return _call(body, name, _sds(blocks.shape, blocks.dtype), in_specs=[any_spec], out_specs=any_spec,
                 scratch=list(_CHIP_SEMS))(blocks)


def pair_sum(name, mine, theirs):
    _, r, c = mine.shape
    tm = _tile(r, (480, 432, 240, 256, 128, 64, 32, 16))

    def body(a_ref, b_ref, o_ref):
        o_ref[...] = (a_ref[...].astype(f32) + b_ref[...].astype(f32)).astype(o_ref.dtype)

    sp = pl.BlockSpec((None, tm, c), lambda ch, i: (ch, i, 0))
    return _call(body, name, _sds(mine.shape, mine.dtype), (4, r // tm), [sp, sp], sp)(mine, theirs)


def sum_slots(name, slots):
    n, r, c = slots.shape
    tm = _tile(r, (480, 432, 240, 256, 128, 64, 32, 16))

    def body(s_ref, o_ref):
        acc = s_ref[0].astype(f32)
        for d in range(1, n):
            acc = acc + s_ref[d].astype(f32)
        o_ref[...] = acc

    return _call(body, name, _sds((r, c)), (r // tm,), [pl.BlockSpec((n, tm, c), lambda i: (0, i, 0))],
                 pl.BlockSpec((tm, c), lambda i: (i, 0)))(slots)


def _rows_layout(col8, t):
    return col8.T.reshape(DN_HEADS, t // CHUNK, CHUNK)


def _blockdiag(m):
    g, a, b = m.shape
    m = m.reshape(N_SUPER, SUPER, a, b)
    out = jnp.einsum("sgab,gh->sgahb", m, jnp.eye(SUPER, dtype=m.dtype))
    return out.reshape(N_SUPER, SUPER * a, SUPER * b)


def _diag_blocks(m, a, b):
    m = m.reshape(N_SUPER, SUPER, a, SUPER, b)
    return jnp.einsum("sgahb,gh->sgab", m, jnp.eye(SUPER, dtype=m.dtype)).reshape(SSM_GROUPS, a, b)


def _s5_params(p, li):
    tag = f"l{li}"
    n = SSM_STATE
    a_re = p["ssm_a_re"].reshape(SSM_GROUPS, 1, n)
    a_im = p["ssm_a_im"].reshape(SSM_GROUPS, 1, n)
    log_dt = p["ssm_log_dt"].reshape(SSM_GROUPS, 1, 1)
    br = jnp.swapaxes(p["ssm_b_re"], 1, 2)
    bi = jnp.swapaxes(p["ssm_b_im"], 1, 2)
    lr, li_, bbr, bbi = s5_disc_fwd("s5_disc_" + tag, a_re, a_im, log_dt, br, bi)
    lam = jnp.concatenate([lr.reshape(N_SUPER, 1, SG_STATE), li_.reshape(N_SUPER, 1, SG_STATE)], axis=-1)
    bbd = jnp.concatenate([_blockdiag(bbr), _blockdiag(bbi)], axis=-1).astype(bf16)
    c_re = jnp.swapaxes(p["ssm_c_re"], 1, 2)
    c_im = jnp.swapaxes(p["ssm_c_im"], 1, 2)
    cbd = jnp.concatenate([_blockdiag(c_re), -_blockdiag(c_im)], axis=1).astype(bf16)
    return dict(a_re=a_re, a_im=a_im, log_dt=log_dt, br=br, bi=bi, lam=lam, bbd=bbd, cbd=cbd)


def layer_fwd(x, p, li, next_shard=None):
    tag = f"l{li}"
    t = x.shape[0]
    d = D_MODEL
    h = rmsnorm_fwd("norm_pre_" + tag, x, p["norm_pre"])
    proj = mm_nn("proj_" + tag, h, p["w_all"])
    q, k, v, bg = conv_qkv_fwd("conv_" + tag, proj, p["conv_w"], p["a_log"], p["dt_bias"])
    b_rows = _rows_layout(bg[:, 0:DN_HEADS], t)
    g_rows = _rows_layout(bg[:, DN_HEADS:2 * DN_HEADS], t)
    o, s_all, t_all, u_all, w_sol, *gathered = delta_fwd("delta_" + tag, q, k, v, g_rows, b_rows, next_shard)
    sp = _s5_params(p, li)
    u_perm = _perm_rows(proj[:, 4 * d:5 * d])
    ys_perm, hs = s5_fwd("s5_" + tag, u_perm, sp["bbd"], sp["cbd"], sp["lam"])
    ys_lin = _unperm_rows(ys_perm)
    yg = gelu_fwd("gelu_" + tag, ys_lin, proj, p["ssm_d"])
    glu_lin = mm_nn("glu_" + tag, yg, p["w_glu"])
    merged = merge_fwd("merge_" + tag, proj, o, yg, glu_lin, p["head_norm"], p["b_glu"])
    out = mm_nn("out_" + tag, merged, p["w_out"])
    y = residual_norm_fwd("norm_post_" + tag, x, out, p["norm_post"])
    saved = dict(x=x, h=h, proj=proj, q=q, k=k, v=v, g_rows=g_rows, b_rows=b_rows, o=o, s_all=s_all, t_all=t_all, u_all=u_all, w_sol=w_sol, sp=sp, u_perm=u_perm,
                 hs=hs, ys_lin=ys_lin, yg=yg, glu_lin=glu_lin, merged=merged, out=out)
    return y, saved, (gathered[0] if gathered else None)


def layer_bwd(dy, p, s, li, chip_part=None):
    tag = f"l{li}"
    t = dy.shape[0]
    d = D_MODEL
    sp = s["sp"]
    gr = {}
    d_out, gr["norm_post"] = post_norm_bwd("norm_post_b_" + tag, s["out"], p["norm_post"], dy)
    d_merged = mm_nt("out_b_" + tag, d_out, p["w_out"])
    gr["w_out"] = mm_tn("out_w_" + tag, s["merged"], d_out, bf16)
    (dza, dzb, dra, drb, d_o, d_glu, dyg_a, gr["head_norm"], gr["b_glu"]) = merge_bwd(
        "merge_b_" + tag, s["proj"], s["o"], s["yg"], s["glu_lin"], p["head_norm"], p["b_glu"], d_merged)
    dyg_b = mm_nt("glu_b_" + tag, d_glu, p["w_glu"])
    gr["w_glu"] = mm_tn("glu_w_" + tag, s["yg"], d_glu, bf16)
    d_ys, du_a, gr["ssm_d"] = gelu_bwd("gelu_b_" + tag, s["ys_lin"], s["proj"], p["ssm_d"], dyg_a, dyg_b)
    hprev0 = jnp.concatenate([jnp.zeros((1, s["hs"].shape[1]), f32), s["hs"][-SEG:-1]], axis=0)
    du_perm, d_bbd, d_cbd, d_lam = s5_bwd("s5_b_" + tag, _perm_rows(d_ys), s["u_perm"], s["hs"], hprev0, sp["bbd"],
                                          sp["cbd"], sp["lam"])
    du_b = _unperm_rows(du_perm)
    gr["ssm_c_re"] = _diag_blocks(d_cbd[:, :, 0:SG_STATE], SSM_GROUP, SSM_STATE)
    gr["ssm_c_im"] = -_diag_blocks(d_cbd[:, :, SG_STATE:], SSM_GROUP, SSM_STATE)
    d_bbr = _diag_blocks(d_bbd[:, :, 0:SG_STATE], SSM_GROUP, SSM_STATE)
    d_bbi = _diag_blocks(d_bbd[:, :, SG_STATE:], SSM_GROUP, SSM_STATE)
    d_lr = d_lam[:, :, 0:SG_STATE].reshape(SSM_GROUPS, 1, SSM_STATE)
    d_li = d_lam[:, :, SG_STATE:].reshape(SSM_GROUPS, 1, SSM_STATE)
    d_are, d_aim, d_ldt, d_br, d_bi = s5_disc_bwd("s5_disc_b_" + tag, sp["a_re"], sp["a_im"], sp["log_dt"], sp["br"],
                                                  sp["bi"], d_lr, d_li, d_bbr, d_bbi)
    gr["ssm_a_re"] = d_are.reshape(SSM_GROUPS, SSM_STATE)
    gr["ssm_a_im"] = d_aim.reshape(SSM_GROUPS, SSM_STATE)
    gr["ssm_log_dt"] = d_ldt.reshape(SSM_GROUPS)
    gr["ssm_b_re"] = jnp.swapaxes(d_br, 1, 2)
    gr["ssm_b_im"] = jnp.swapaxes(d_bi, 1, 2)
    dq, dk, dv, dg_rows, db_rows, *slots = delta_bwd("delta_b_" + tag, s["q"], s["k"], s["v"], s["g_rows"],
                                                     s["b_rows"], s["s_all"], s["t_all"], s["u_all"], s["w_sol"], d_o,
                                                     chip_part)
    dbg = jnp.concatenate([db_rows.reshape(DN_HEADS, t).T, dg_rows.reshape(DN_HEADS, t).T,
                           jnp.zeros((t, 128 - 2 * DN_HEADS), f32)], axis=1)
    dc, gr["conv_w"], dbd, dab = conv_qkv_bwd("conv_b_" + tag, s["proj"], p["conv_w"], p["a_log"], p["dt_bias"],
                                              dq, dk, dv, dbg)
    gr["a_log"] = dab[0, DN_HEADS:2 * DN_HEADS]
    gr["dt_bias"] = dab[1, DN_HEADS:2 * DN_HEADS]
    dproj = assemble_dproj("dproj_" + tag, dc, p["conv_w"], dza, du_a, du_b, dzb, dra, drb, dbd)
    d_h = mm_nt("proj_b_" + tag, dproj, p["w_all"])
    gr["w_all"] = mm_tn("proj_w_" + tag, s["h"], dproj, bf16)
    dx, gr["norm_pre"] = rmsnorm_bwd("norm_pre_b_" + tag, s["x"], p["norm_pre"], d_h, dy)
    return dx, gr, (slots[0] if slots else None)


REPL = ["norm_pre", "a_log", "dt_bias", "head_norm", "ssm_a_re", "ssm_a_im", "ssm_log_dt", "ssm_b_re", "ssm_b_im",
        "ssm_c_re", "ssm_c_im", "ssm_d", "b_glu", "norm_post"]
SHARDED = ["w_in", "conv_w", "w_glu", "w_out"]
ALL_W = ["norm_pre", "w_in", "conv_w", "a_log", "dt_bias", "head_norm", "ssm_a_re", "ssm_a_im", "ssm_log_dt",
         "ssm_b_re", "ssm_b_im", "ssm_c_re", "ssm_c_im", "ssm_d", "w_glu", "b_glu", "w_out", "norm_post"]
PACK_W = 1024


def _pack_flat(arrs, rows):
    flat = jnp.concatenate([a.reshape(-1) for a in arrs])
    return jnp.pad(flat, (0, rows * PACK_W - flat.shape[0])).reshape(rows, PACK_W)


def _flat_rows(arrs, mult=8):
    n = sum(math.prod(a.shape) for a in arrs)
    rows = -(-n // PACK_W)
    return -(-rows // mult) * mult


def _unpack(flat, shapes):
    out, off = [], 0
    for sh in shapes:
        n = math.prod(sh)
        out.append(flat[off:off + n].reshape(sh))
        off += n
    return out


def _repl_split(shapes):
    big = [n for n in REPL if math.prod(shapes[n]) % PACK_W == 0]
    small = [n for n in REPL if n not in big]
    return big, small


def _rows8(n):
    return -(-n // (8 * PACK_W)) * 8


def _repl_rows(shapes):
    big, small = _repl_split(shapes)
    rows = sum(_rows8(math.prod(shapes[n])) for n in big) + _flat_rows([_sds(shapes[n]) for n in small], 8)
    return -(-rows // (8 * N_DEV)) * (8 * N_DEV)


def _repl_pack(arrs, shapes):
    big, small = _repl_split(shapes)
    parts = []
    for n in big:
        a = arrs[n].reshape(-1, PACK_W)
        parts.append(jnp.pad(a, ((0, _rows8(a.size) - a.shape[0]), (0, 0))))
    parts.append(_pack_flat([arrs[n] for n in small], _flat_rows([_sds(shapes[n]) for n in small], 8)))
    used = sum(p.shape[0] for p in parts)
    parts.append(jnp.zeros((_repl_rows(shapes) - used, PACK_W), parts[0].dtype))
    return jnp.concatenate(parts, axis=0)


def _repl_unpack(packed, shapes):
    big, small = _repl_split(shapes)
    out, off = {}, 0
    for n in big:
        size = math.prod(shapes[n])
        out[n] = packed[off:off + size // PACK_W].reshape(shapes[n])
        off += _rows8(size)
    srows = _flat_rows([_sds(shapes[n]) for n in small], 8)
    out.update(zip(small, _unpack(packed[off:off + srows].reshape(-1), [shapes[n] for n in small])))
    return out


_COL_RUNS = ((0, 4096), (4112, W_COLS), (4096, 4112))


def _w_all(main, tails, wc):
    pieces = []
    for lo, hi in _COL_RUNS:
        for dv in range(N_DEV):
            a, b = max(lo, dv * wc) - dv * wc, min(hi, (dv + 1) * wc) - dv * wc
            if a < min(b, PACK_W):
                pieces.append(main[dv][:, :, a:min(b, PACK_W)])
            if b > max(a, PACK_W):
                pieces.append(tails[dv][:, :, max(a, PACK_W) - PACK_W:b - PACK_W])
    ll, rows = main[0].shape[0], main[0].shape[1]
    pieces.append(jnp.zeros((ll, rows, W_PAD - W_COLS), main[0].dtype))
    return jnp.concatenate(pieces, axis=2)


def _ref_cols(g, lo, hi):
    pieces, off = [], 0
    for a, b in _COL_RUNS:
        s, e = max(lo, a), min(hi, b)
        if s < e:
            pieces.append((s, g[..., off + s - a:off + e - a]))
        off += b - a
    pieces.sort(key=lambda t: t[0])
    return jnp.concatenate([p for _, p in pieces], axis=-1) if len(pieces) > 1 else pieces[0][1]


def kernel(x, norm_pre, w_in, conv_w, a_log, dt_bias, head_norm, ssm_a_re, ssm_a_im, ssm_log_dt, ssm_b_re, ssm_b_im, ssm_c_re, ssm_c_im, ssm_d, w_glu, b_glu, w_out, norm_post, loss_target, m_norm_pre, m_w_in, m_conv_w, m_a_log, m_dt_bias, m_head_norm, m_ssm_a_re, m_ssm_a_im, m_ssm_log_dt, m_ssm_b_re, m_ssm_b_im, m_ssm_c_re, m_ssm_c_im, m_ssm_d, m_w_glu, m_b_glu, m_w_out, m_norm_post, v_norm_pre, v_w_in, v_conv_w, v_a_log, v_dt_bias, v_head_norm, v_ssm_a_re, v_ssm_a_im, v_ssm_log_dt, v_ssm_b_re, v_ssm_b_im, v_ssm_c_re, v_ssm_c_im, v_ssm_d, v_w_glu, v_b_glu, v_w_out, v_norm_post):
    loc = dict(locals())
    w = {n: loc[n] for n in ALL_W}
    m = {n: loc["m_" + n] for n in ALL_W}
    v = {n: loc["v_" + n] for n in ALL_W}
    depth = w_in.shape[0]
    wc = w_in.shape[2]
    cc = conv_w.shape[2]
    wr = w_glu.shape[1]

    tail = wc - PACK_W
    conv_hi = conv_w.astype(bf16)
    conv_mid = (conv_w - conv_hi.astype(f32)).astype(bf16)
    conv_lo = (conv_w - conv_hi.astype(f32) - conv_mid.astype(f32)).astype(bf16)
    conv3 = jnp.stack([conv_hi, conv_mid, conv_lo], axis=1)
    w_in_b = w_in.astype(bf16)
    misc_shapes = [(D_MODEL, tail), (3, CONV_K, cc)]
    misc_rows = _flat_rows([_sds(sh, bf16) for sh in misc_shapes], 16)

    def layer_shard(li):
        return jnp.concatenate([w_in_b[li, :, :PACK_W], w_glu[li].astype(bf16), w_out[li].astype(bf16),
                                _pack_flat([w_in_b[li, :, PACK_W:], conv3[li]], misc_rows)])

    def layer_weights(gathered):
        miscs = [_unpack(gathered[dv, D_MODEL + 2 * wr:].reshape(-1), misc_shapes) for dv in range(N_DEV)]
        w_all = _w_all([gathered[dv, :D_MODEL][None] for dv in range(N_DEV)], [mi[0][None] for mi in miscs], wc)[0]
        glu = gathered[:, D_MODEL:D_MODEL + wr].reshape(N_DEV * wr, D_MODEL)
        out = gathered[:, D_MODEL + wr:D_MODEL + 2 * wr].reshape(N_DEV * wr, D_MODEL)
        conv = jnp.concatenate([mi[1][0].astype(f32) + mi[1][1].astype(f32) + mi[1][2].astype(f32) for mi in miscs],
                               axis=1)
        return dict(w_all=w_all, w_glu=glu, w_out=out, conv_w=conv)

    def layer_params(li, gathered):
        return dict(layer_weights(gathered), norm_pre=norm_pre[li].reshape(1, -1),
                    a_log=jnp.pad(a_log[li], (DN_HEADS, 128 - 2 * DN_HEADS)).reshape(1, 128),
                    dt_bias=jnp.pad(dt_bias[li], (DN_HEADS, 128 - 2 * DN_HEADS)).reshape(1, 128),
                    head_norm=head_norm[li].reshape(1, -1), ssm_a_re=ssm_a_re[li], ssm_a_im=ssm_a_im[li],
                    ssm_log_dt=ssm_log_dt[li], ssm_b_re=ssm_b_re[li], ssm_b_im=ssm_b_im[li], ssm_c_re=ssm_c_re[li],
                    ssm_c_im=ssm_c_im[li], ssm_d=ssm_d[li].reshape(1, -1),
                    b_glu=b_glu[li].reshape(1, -1), norm_post=norm_post[li].reshape(1, -1))

    act = x[0]
    saved, params = [], []
    gathered = all_gather("gather_weights", layer_shard(0))
    for li in range(depth):
        params.append(layer_params(li, gathered))
        act, sv, gathered = layer_fwd(act, params[li], li, layer_shard(li + 1) if li + 1 < depth else None)
        saved.append(sv)
    loss_part, dy = loss_head("loss_head", act, loss_target[0])
    repl_shapes = {n: w[n].shape for n in REPL}
    repl_rows = _repl_rows(repl_shapes)
    rr = repl_rows // N_DEV
    gmisc_shapes = [(D_MODEL, tail), (CONV_K, cc)]
    gmisc_rows = _flat_rows([_sds(sh, bf16) for sh in gmisc_shapes], 16)
    my_c = lax.axis_index("c")

    def chip_sums(li, gl, extra=None):
        g_conv = gl["conv_w"].astype(bf16)
        blocks = jnp.stack([jnp.concatenate(
            [_ref_cols(gl["w_all"], dv * wc, dv * wc + PACK_W), gl["w_glu"][dv * wr:(dv + 1) * wr],
             gl["w_out"][dv * wr:(dv + 1) * wr],
             _pack_flat([_ref_cols(gl["w_all"], dv * wc + PACK_W, (dv + 1) * wc), g_conv[:, dv * cc:(dv + 1) * cc]],
                        gmisc_rows)] + ([extra[dv * rr:(dv + 1) * rr]] if extra is not None else []))
            for dv in range(N_DEV)])
        blocks = blocks.reshape(4, 2, blocks.shape[1], PACK_W)
        from_sibling = pair_exchange(f"pair_grads_l{li}", blocks)
        own = lax.dynamic_index_in_dim(blocks, my_c, axis=1, keepdims=False)
        return pair_sum(f"pair_sum_grads_l{li}", own, from_sibling)

    grads, slots, pending = [None] * depth, [None] * depth, None
    for li in reversed(range(depth)):
        dy, grads[li], got = layer_bwd(dy, params[li], saved[li], li, pending)
        if pending is not None:
            slots[li + 1] = got
        g_repl = None
        if li == 0:
            g_repl = _repl_pack({n: jnp.stack([grads[l][n] for l in range(depth)]).reshape(w[n].shape) for n in REPL},
                                repl_shapes).astype(bf16)
        pending = chip_sums(li, grads[li], g_repl)
    slots[0] = chip_exchange("scatter_grads", pending)
    grad_x = dy[None]
    loss = lax.psum(loss_part[0, 0], ("x", "y", "c"))
    mine = [sum_slots(f"sum_grads_l{li}", slots[li]) for li in range(depth)]
    o1, o2, o3, o4 = D_MODEL, D_MODEL + wr, D_MODEL + 2 * wr, D_MODEL + 2 * wr + gmisc_rows
    unpacked = [_unpack(mi[o3:o4].reshape(-1), gmisc_shapes) for mi in mine]
    gs_w_in = jnp.stack([jnp.concatenate([mi[:o1], up[0]], axis=1) for mi, up in zip(mine, unpacked)])
    gs_glu = jnp.stack([mi[o1:o2] for mi in mine])
    gs_out = jnp.stack([mi[o2:o3] for mi in mine])
    gs_conv = jnp.stack([up[1] for up in unpacked])
    g_repl_full = all_gather("gather_repl_grads", mine[0][o4:o4 + rr]).reshape(repl_rows, PACK_W)
    g = _repl_unpack(g_repl_full, repl_shapes)
    g.update(w_in=gs_w_in, w_glu=gs_glu, w_out=gs_out, conv_w=gs_conv)

    delta, new_m, new_v = {}, {}, {}
    for n in SHARDED:
        delta[n], new_m[n], new_v[n] = adamw("adamw_" + n, w[n], g[n], m[n], v[n])
    outs = adamw("adamw_repl", *[_repl_pack({n: src[n] for n in REPL}, repl_shapes) for src in (w, g, m, v)])
    for dst, o in zip((delta, new_m, new_v), outs):
        dst.update(_repl_unpack(o, repl_shapes))
    return (loss, grad_x, *[g[n] for n in ALL_W], *[delta[n] for n in ALL_W], *[new_m[n] for n in ALL_W],
            *[new_v[n] for n in ALL_W])
```

```python
import math

import jax
import jax.numpy as jnp
from jax import lax
from jax.experimental import pallas as pl
from jax.experimental.pallas import tpu as pltpu

f32 = jnp.float32
bf16 = jnp.bfloat16

D_MODEL = 1024
DEPTH = 4
N_DEV = 8
DN_HEADS = 8
HEAD_DIM = 128
CHUNK = 64
CONV_K = 4
SSM_GROUPS = 64
SSM_GROUP = 16
SSM_STATE = 64
SUPER = 8
N_SUPER = SSM_GROUPS // SUPER
SG_STATE = SUPER * SSM_STATE
EPS = 1e-6
W_COLS = 8208
W_PAD = 8448
ADAM_LR, ADAM_B1, ADAM_B2, ADAM_EPS, ADAM_WD, ADAM_STEP = 0.001, 0.9, 0.999, 1e-08, 0.01, 10
VMEM_LIMIT = 56 * 1024 * 1024
MESH = pl.DeviceIdType.MESH
HIGH = lax.Precision.HIGH
DELTA_HB = 8


def _call(body, name, out_shape, grid=None, in_specs=None, out_specs=None, scratch=(), **kw):
    args = dict(out_shape=out_shape, name=name, scratch_shapes=list(scratch),
                compiler_params=pltpu.CompilerParams(vmem_limit_bytes=VMEM_LIMIT, **kw))
    if grid is not None:
        args.update(grid=grid, in_specs=in_specs, out_specs=out_specs)
    else:
        if in_specs is not None:
            args.update(in_specs=in_specs)
        if out_specs is not None:
            args.update(out_specs=out_specs)
    return pl.pallas_call(body, **args)


def _sds(shape, dtype=f32):
    return jax.ShapeDtypeStruct(tuple(shape), dtype)


def _sigmoid(x):
    return 1.0 / (1.0 + jnp.exp(-x))


def _silu(x):
    return x * _sigmoid(x)


def _dsilu(x):
    s = _sigmoid(x)
    return s * (1.0 + x * (1.0 - s))


_GELU_C = math.sqrt(2.0 / math.pi)


def _gelu(x):
    return 0.5 * x * (1.0 + jnp.tanh(_GELU_C * (x + 0.044715 * x * x * x)))


def _dgelu(x):
    t = jnp.tanh(_GELU_C * (x + 0.044715 * x * x * x))
    return 0.5 * (1.0 + t) + 0.5 * x * (1.0 - t * t) * _GELU_C * (1.0 + 3 * 0.044715 * x * x)


def _softplus(x):
    return jnp.maximum(x, 0.0) + jnp.log(1.0 + jnp.exp(-jnp.abs(x)))


def _bdot(a, b, dn):
    return lax.dot_general(a.astype(bf16), b.astype(bf16), (dn, ((), ())), preferred_element_type=f32)


def _matmul(name, a, b, *, dn, grid, a_spec, b_spec, o_spec, o_shape, o_dtype=f32):
    nk = grid[-1]

    def body(a_ref, b_ref, o_ref, acc_ref):
        p = _bdot(a_ref[...], b_ref[...], dn)
        if nk == 1:
            o_ref[...] = p.astype(o_dtype)
        else:
            k = pl.program_id(len(grid) - 1)

            @pl.when(k == 0)
            def _():
                acc_ref[...] = p

            @pl.when(k > 0)
            def _():
                acc_ref[...] += p

            @pl.when(k == nk - 1)
            def _():
                o_ref[...] = acc_ref[...].astype(o_dtype)

    blk = tuple(d for d in o_spec.block_shape if d is not None)
    return _call(body, name, _sds(o_shape, o_dtype), grid, [a_spec, b_spec], o_spec,
                 scratch=[pltpu.VMEM(blk if nk > 1 else (8, 128), f32)])(a, b)


def _tile(n, pref):
    for t in pref:
        if n % t == 0:
            return t
    return n


def mm_nn(name, a, b):
    m, k = a.shape
    n = b.shape[1]
    tm, tn, tk = _tile(m, (1024, 512, 256)), _tile(n, (2816, 1024, 512)), _tile(k, (2816, 1024))
    return _matmul(name, a, b, dn=((1,), (0,)), grid=(m // tm, n // tn, k // tk),
                   a_spec=pl.BlockSpec((tm, tk), lambda i, j, l: (i, l)),
                   b_spec=pl.BlockSpec((tk, tn), lambda i, j, l: (l, j)),
                   o_spec=pl.BlockSpec((tm, tn), lambda i, j, l: (i, j)), o_shape=(m, n))


def mm_nt(name, a, b):
    m, k = a.shape
    n = b.shape[0]
    tm, tn, tk = _tile(m, (1024, 512, 256)), _tile(n, (1024, 512)), _tile(k, (2816, 1024))
    return _matmul(name, a, b, dn=((1,), (1,)), grid=(m // tm, n // tn, k // tk),
                   a_spec=pl.BlockSpec((tm, tk), lambda i, j, l: (i, l)),
                   b_spec=pl.BlockSpec((tn, tk), lambda i, j, l: (j, l)),
                   o_spec=pl.BlockSpec((tm, tn), lambda i, j, l: (i, j)), o_shape=(m, n))


def mm_tn(name, a, b, o_dtype=f32):
    k, m = a.shape
    n = b.shape[1]
    tm, tn, tk = _tile(m, (512,)), _tile(n, (2816, 1024, 512)), _tile(k, (512, 256))
    return _matmul(name, a, b, dn=((0,), (0,)), grid=(m // tm, n // tn, k // tk),
                   a_spec=pl.BlockSpec((tk, tm), lambda i, j, l: (l, i)),
                   b_spec=pl.BlockSpec((tk, tn), lambda i, j, l: (l, j)),
                   o_spec=pl.BlockSpec((tm, tn), lambda i, j, l: (i, j)), o_shape=(m, n), o_dtype=o_dtype)


def _rows(t, light=False):
    return _tile(t, (512, 256) if light else (256,))


def _row_spec(tm, w):
    return pl.BlockSpec((tm, w), lambda i: (i, 0))


def _acc_spec(r, w):
    return pl.BlockSpec((r, w), lambda i: (0, 0))


def _accumulate(ref, val):
    @pl.when(pl.program_id(0) == 0)
    def _():
        ref[...] = val

    @pl.when(pl.program_id(0) > 0)
    def _():
        ref[...] += val


def rmsnorm_fwd(name, x, gain):
    t, d = x.shape
    tm = _rows(t, light=True)

    def body(x_ref, g_ref, o_ref):
        xv = x_ref[...]
        r = lax.rsqrt(jnp.mean(xv * xv, axis=-1, keepdims=True) + EPS)
        o_ref[...] = (xv * r * g_ref[...]).astype(bf16)

    return _call(body, name, _sds((t, d), bf16), (t // tm,), [_row_spec(tm, d), _acc_spec(1, d)],
                 _row_spec(tm, d))(x, gain)


def rmsnorm_bwd(name, x, gain, dn, dres):
    t, d = x.shape
    tm = _rows(t, light=True)

    def body(x_ref, g_ref, dn_ref, dr_ref, dx_ref, dg_ref):
        xv = x_ref[...]
        r = lax.rsqrt(jnp.mean(xv * xv, axis=-1, keepdims=True) + EPS)
        n = xv * r
        dnv = dn_ref[...]
        _accumulate(dg_ref, jnp.sum(dnv * n, axis=0, keepdims=True))
        dng = dnv * g_ref[...]
        dx_ref[...] = dr_ref[...] + r * (dng - n * jnp.mean(dng * n, axis=-1, keepdims=True))

    return _call(body, name, (_sds((t, d)), _sds((1, d))), (t // tm,),
                 [_row_spec(tm, d), _acc_spec(1, d), _row_spec(tm, d), _row_spec(tm, d)],
                 (_row_spec(tm, d), _acc_spec(1, d)))(x, gain, dn, dres)


def residual_norm_fwd(name, x, out, gain):
    t, d = x.shape
    tm = _rows(t, light=True)

    def body(x_ref, o_ref, g_ref, y_ref):
        ov = o_ref[...]
        r = lax.rsqrt(jnp.mean(ov * ov, axis=-1, keepdims=True) + EPS)
        y_ref[...] = x_ref[...] + ov * r * g_ref[...]

    return _call(body, name, _sds((t, d)), (t // tm,), [_row_spec(tm, d), _row_spec(tm, d), _acc_spec(1, d)],
                 _row_spec(tm, d))(x, out, gain)


def post_norm_bwd(name, out, gain, dy):
    t, d = out.shape
    tm = _rows(t, light=True)

    def body(o_ref, g_ref, dy_ref, do_ref, dg_ref):
        ov = o_ref[...]
        r = lax.rsqrt(jnp.mean(ov * ov, axis=-1, keepdims=True) + EPS)
        n = ov * r
        dyv = dy_ref[...]
        _accumulate(dg_ref, jnp.sum(dyv * n, axis=0, keepdims=True))
        dng = dyv * g_ref[...]
        do_ref[...] = (r * (dng - n * jnp.mean(dng * n, axis=-1, keepdims=True))).astype(bf16)

    return _call(body, name, (_sds((t, d), bf16), _sds((1, d))), (t // tm,),
                 [_row_spec(tm, d), _acc_spec(1, d), _row_spec(tm, d)],
                 (_row_spec(tm, d), _acc_spec(1, d)))(out, gain, dy)


def loss_head(name, y, target):
    t, d = y.shape
    tm = _rows(t, light=True)

    def body(y_ref, t_ref, l_ref, dy_ref):
        e = y_ref[...] - t_ref[...]
        dy_ref[...] = e * (1.0 / d)
        s = jnp.sum(jnp.sum(e * e, axis=1, keepdims=True), axis=0, keepdims=True) * (0.5 / d)
        _accumulate(l_ref, s)

    return _call(body, name, (_sds((1, 1)), _sds((t, d))), (t // tm,),
                 [_row_spec(tm, d), _row_spec(tm, d)], (_acc_spec(1, 1), _row_spec(tm, d)))(y, target)


def _prev_spec(tm, w):
    return pl.BlockSpec((8, w), lambda i: (jnp.maximum(i * (tm // 8) - 1, 0), 0))


def _next_spec(tm, w, t):
    return pl.BlockSpec((8, w), lambda i: (jnp.minimum((i + 1) * (tm // 8), t // 8 - 1), 0))


def _fill_pad(pad_ref, prev_ref, cur_ref, tm):
    keep = (pl.program_id(0) > 0).astype(f32)
    pad_ref[0:8, :] = prev_ref[...] * keep
    pad_ref[8:8 + tm, :] = cur_ref[...]


def _conv_block(pad_ref, w_ref, cb, tm):
    cs = slice(cb * 128, (cb + 1) * 128)
    acc = pad_ref[pl.ds(8 - (CONV_K - 1), tm), cs] * w_ref[0:1, cs]
    for j in range(1, CONV_K):
        acc = acc + pad_ref[pl.ds(8 - (CONV_K - 1) + j, tm), cs] * w_ref[j:j + 1, cs]
    return acc


def conv_qkv_fwd(name, proj, conv_w, a_log, dt_bias):
    t = proj.shape[0]
    tm = _rows(t)
    scale = HEAD_DIM ** -0.5

    def body(cur_ref, prev_ref, w_ref, bd_ref, al_ref, db_ref, q_ref, k_ref, v_ref, bg_ref, pad_ref):
        _fill_pad(pad_ref, prev_ref, cur_ref, tm)
        for cb in range(3 * DN_HEADS):
            s = _silu(_conv_block(pad_ref, w_ref, cb, tm))
            hs = slice((cb % DN_HEADS) * 128, (cb % DN_HEADS + 1) * 128)
            if cb < DN_HEADS:
                q_ref[:, hs] = s * (lax.rsqrt(jnp.sum(s * s, axis=-1, keepdims=True) + EPS) * scale)
            elif cb < 2 * DN_HEADS:
                k_ref[:, hs] = s * lax.rsqrt(jnp.sum(s * s, axis=-1, keepdims=True) + EPS)
            else:
                v_ref[:, hs] = s
        bd = bd_ref[...]
        beta = _sigmoid(bd)
        g = -jnp.exp(al_ref[...]) * _softplus(bd + db_ref[...])
        lane = lax.broadcasted_iota(jnp.int32, bd.shape, 1)
        bg_ref[...] = jnp.where(lane < DN_HEADS, beta, jnp.where(lane < 2 * DN_HEADS, g, 0.0))

    w3 = 3 * D_MODEL
    return _call(body, name, (_sds((t, D_MODEL)),) * 3 + (_sds((t, 128)),), (t // tm,),
                 [pl.BlockSpec((tm, w3), lambda i: (i, 0)), _prev_spec(tm, w3), _acc_spec(CONV_K, w3),
                  pl.BlockSpec((tm, 128), lambda i: (i, 8192 // 128)), _acc_spec(1, 128), _acc_spec(1, 128)],
                 (_row_spec(tm, D_MODEL),) * 3 + (_row_spec(tm, 128),),
                 scratch=[pltpu.VMEM((tm + 8, w3), f32)])(proj, proj, conv_w, proj, a_log, dt_bias)


def conv_qkv_bwd(name, proj, conv_w, a_log, dt_bias, dq, dk, dv, dbg):
    t = proj.shape[0]
    tm = _rows(t)
    scale = HEAD_DIM ** -0.5

    def body(cur_ref, prev_ref, w_ref, bd_ref, al_ref, db_ref, dq_ref, dk_ref, dv_ref, dbg_ref,
             dc_ref, dw_ref, dbd_ref, dab_ref, pad_ref):
        _fill_pad(pad_ref, prev_ref, cur_ref, tm)

        @pl.when(pl.program_id(0) == 0)
        def _():
            dw_ref[...] = jnp.zeros_like(dw_ref)

        for cb in range(3 * DN_HEADS):
            cs = slice(cb * 128, (cb + 1) * 128)
            hs = slice((cb % DN_HEADS) * 128, (cb % DN_HEADS + 1) * 128)
            taps = [pad_ref[pl.ds(8 - (CONV_K - 1) + j, tm), cs] for j in range(CONV_K)]
            c = taps[0] * w_ref[0:1, cs]
            for j in range(1, CONV_K):
                c = c + taps[j] * w_ref[j:j + 1, cs]
            sg = _sigmoid(c)
            s = c * sg
            if cb < 2 * DN_HEADS:
                dn = (dq_ref[:, hs] * scale) if cb < DN_HEADS else dk_ref[:, hs]
                r = lax.rsqrt(jnp.sum(s * s, axis=-1, keepdims=True) + EPS)
                ds = r * dn - s * (r * r * r) * jnp.sum(dn * s, axis=-1, keepdims=True)
            else:
                ds = dv_ref[:, hs]
            dc = ds * (sg * (1.0 + c * (1.0 - sg)))
            dc_ref[:, cs] = dc
            for j in range(CONV_K):
                dw_ref[j:j + 1, cs] += jnp.sum(dc * taps[j], axis=0, keepdims=True)
        bd = bd_ref[...]
        dbg_v = dbg_ref[...]
        lane = lax.broadcasted_iota(jnp.int32, bd.shape, 1)
        sg = _sigmoid(bd)
        ea = jnp.exp(al_ref[...])
        z = bd + db_ref[...]
        sp = _softplus(z)
        is_b = lane < DN_HEADS
        is_g = jnp.logical_and(lane >= DN_HEADS, lane < 2 * DN_HEADS)
        d_z = jnp.where(is_g, dbg_v * (-ea) * _sigmoid(z), 0.0)
        dbd_ref[...] = jnp.where(is_b, dbg_v * sg * (1.0 - sg), d_z)
        d_al = jnp.sum(jnp.where(is_g, dbg_v * (-ea) * sp, 0.0), axis=0, keepdims=True)
        d_db = jnp.sum(d_z, axis=0, keepdims=True)
        _accumulate(dab_ref, jnp.concatenate([d_al, d_db] + [jnp.zeros_like(d_al)] * 6, axis=0))

    w3 = 3 * D_MODEL
    return _call(body, name, (_sds((t, w3)), _sds((CONV_K, w3)), _sds((t, 128)), _sds((8, 128))), (t // tm,),
                 [pl.BlockSpec((tm, w3), lambda i: (i, 0)), _prev_spec(tm, w3), _acc_spec(CONV_K, w3),
                  pl.BlockSpec((tm, 128), lambda i: (i, 8192 // 128)), _acc_spec(1, 128), _acc_spec(1, 128),
                  _row_spec(tm, D_MODEL), _row_spec(tm, D_MODEL), _row_spec(tm, D_MODEL), _row_spec(tm, 128)],
                 (_row_spec(tm, w3), _acc_spec(CONV_K, w3), _row_spec(tm, 128), _acc_spec(8, 128)),
                 scratch=[pltpu.VMEM((tm + 8, w3), f32)])(proj, proj, conv_w, proj, a_log, dt_bias, dq, dk, dv, dbg)


def _bdg(a, b, ca, cb, prec=None):
    if prec is None:
        a, b = a.astype(bf16), b.astype(bf16)
    return lax.dot_general(a, b, (((ca,), (cb,)), ((0,), (0,))), precision=prec, preferred_element_type=f32)


def _bnn(a, b, prec=None):
    return _bdg(a, b, 2, 1, prec)


def _bnt(a, b, prec=None):
    return _bdg(a, b, 2, 2, prec)


def _btn(a, b, prec=None):
    return _bdg(a, b, 1, 1, prec)


def _delta_local(q, k, v, g_row, b_row, solved=None):
    c = CHUNK
    ii = lax.broadcasted_iota(jnp.int32, (c, c), 0)
    jj = lax.broadcasted_iota(jnp.int32, (c, c), 1)
    eye, lower, strict = ii == jj, ii >= jj, ii > jj
    shp = (q.shape[0], c, c)
    g_b = jnp.broadcast_to(g_row, shp)
    gc_col = jnp.sum(jnp.where(lower, g_b, 0.0), axis=2, keepdims=True)
    gc_row = jnp.sum(jnp.where(eye, jnp.broadcast_to(gc_col, shp), 0.0), axis=1, keepdims=True)
    b_col = jnp.sum(jnp.where(eye, jnp.broadcast_to(b_row, shp), 0.0), axis=2, keepdims=True)
    gl = jnp.sum(g_row, axis=2, keepdims=True)
    decay = jnp.exp(jnp.where(lower, gc_col - gc_row, -1e30))
    kb = k * b_col
    kk = _bnt(kb, k)
    lmat = jnp.where(strict, kk * decay, 0.0)
    egc = jnp.exp(gc_col)
    rhs_w = kb * egc
    if solved is None:
        tinv = eye.astype(f32) - lmat
        pw = lmat
        for _ in range(5):
            pw = _bnn(pw, pw, HIGH)
            tinv = tinv + _bnn(tinv, pw, HIGH)
        u = _bnn(tinv, v * b_col, HIGH)
        w = _bnn(tinv, rhs_w, HIGH)
    else:
        tinv, u, w = solved
    qk = _bnt(q, k)
    amat = jnp.where(lower, qk * decay, 0.0)
    q_dec = q * egc
    kdf = jnp.exp(gl - gc_col)
    k_dec = k * kdf
    return dict(eye=eye, lower=lower, strict=strict, gc_col=gc_col, b_col=b_col, gl=gl, decay=decay,
                kb=kb, kk=kk, tinv=tinv, egc=egc, rhs_w=rhs_w, u=u, w=w, qk=qk, amat=amat, q_dec=q_dec, kdf=kdf,
                k_dec=k_dec)


def _heads(ref, hb):
    return jnp.stack([ref[:, h * HEAD_DIM:(h + 1) * HEAD_DIM] for h in range(hb)])


def _put_heads(ref, val):
    for h in range(val.shape[0]):
        ref[:, h * HEAD_DIM:(h + 1) * HEAD_DIM] = val[h]


def delta_fwd(name, q, k, v, g_rows, b_rows, next_shard=None):
    t = q.shape[0]
    nc = t // CHUNK
    hb = DELTA_HB
    fused = next_shard is not None
    assert not fused or (hb == DN_HEADS and nc >= 2)

    def body(*refs):
        if fused:
            (q_ref, k_ref, v_ref, g_ref, b_ref, x_ref, o_ref, s_ref, t_ref, u_ref, w_ref, gath_ref, state, send_sems,
             recv_sems, local_sem) = refs
        else:
            q_ref, k_ref, v_ref, g_ref, b_ref, o_ref, s_ref, t_ref, u_ref, w_ref, state = refs
        n = pl.program_id(1)
        if fused:
            start, forward, finish = _gather_steps(x_ref, gath_ref, send_sems, recv_sems, local_sem)
            pl.when(n == 0)(start)

        @pl.when(n == 0)
        def _():
            state[...] = jnp.zeros_like(state)

        loc = _delta_local(_heads(q_ref, hb), _heads(k_ref, hb), _heads(v_ref, hb), g_ref[:, pl.ds(n, 1), :],
                           b_ref[:, pl.ds(n, 1), :])
        s0 = state[...]
        s_ref[...] = s0
        t_ref[...] = loc["tinv"]
        _put_heads(u_ref, loc["u"])
        _put_heads(w_ref, loc["w"])
        v_new = loc["u"] - _bnn(loc["w"], s0)
        _put_heads(o_ref, _bnn(loc["q_dec"], s0) + _bnn(loc["amat"], v_new))
        state[...] = s0 * jnp.exp(loc["gl"]) + _btn(loc["k_dec"], v_new)
        if fused:
            pl.when(n == (3 * nc) // 4)(forward)
            pl.when(n == nc - 1)(finish)

    tok = pl.BlockSpec((CHUNK, hb * HEAD_DIM), lambda h, n: (n, h))
    row = pl.BlockSpec((hb, nc, CHUNK), lambda h, n: (h, 0, 0))
    any_spec = pl.BlockSpec(memory_space=pl.ANY)
    outs = (_sds((t, D_MODEL)), _sds((DN_HEADS, nc, HEAD_DIM, HEAD_DIM)), _sds((DN_HEADS, nc, CHUNK, CHUNK)),
            _sds((t, D_MODEL)), _sds((t, D_MODEL)))
    out_specs = (tok, pl.BlockSpec((hb, None, HEAD_DIM, HEAD_DIM), lambda h, n: (h, n, 0, 0)),
                 pl.BlockSpec((hb, None, CHUNK, CHUNK), lambda h, n: (h, n, 0, 0)), tok, tok)
    in_specs, args = [tok, tok, tok, row, row], (q, k, v, g_rows, b_rows)
    scratch = [pltpu.VMEM((hb, HEAD_DIM, HEAD_DIM), f32)]
    if fused:
        outs += (_sds((N_DEV,) + next_shard.shape, next_shard.dtype),)
        out_specs += (any_spec,)
        in_specs, args = in_specs + [any_spec], args + (next_shard,)
        scratch = scratch + list(_GATHER_SEMS)
    return _call(body, name, outs, (DN_HEADS // hb, nc), in_specs, out_specs, scratch=scratch)(*args)


def delta_bwd(name, q, k, v, g_rows, b_rows, s_all, t_all, u_all, w_all, do, chip_part=None):
    t = q.shape[0]
    nc = t // CHUNK
    c = CHUNK
    hb = DELTA_HB

    fused = chip_part is not None
    assert not fused or hb == DN_HEADS

    def body(*refs):
        if fused:
            (q_ref, k_ref, v_ref, g_ref, b_ref, s_ref, t_ref, u_ref, w_ref, do_ref, x_ref, dq_ref, dk_ref, dv_ref,
             dg_ref, db_ref, slot_ref, dstate, send_sems, recv_sems, local_sem) = refs
        else:
            (q_ref, k_ref, v_ref, g_ref, b_ref, s_ref, t_ref, u_ref, w_ref, do_ref, dq_ref, dk_ref, dv_ref, dg_ref,
             db_ref, dstate) = refs
        step = pl.program_id(1)
        n = nc - 1 - step
        if fused:
            start, finish = _chip_exchange_steps(x_ref, slot_ref, send_sems, recv_sems, local_sem)
            pl.when(step == 0)(start)

        @pl.when(step == 0)
        def _():
            dstate[...] = jnp.zeros_like(dstate)

        qv, kv, vv = _heads(q_ref, hb), _heads(k_ref, hb), _heads(v_ref, hb)
        L = _delta_local(qv, kv, vv, g_ref[:, pl.ds(n, 1), :], b_ref[:, pl.ds(n, 1), :],
                         solved=(t_ref[...], _heads(u_ref, hb), _heads(w_ref, hb)))
        eye, lower, strict = L["eye"], L["lower"], L["strict"]
        shp = (hb, c, c)
        s0 = s_ref[...]
        dov = _heads(do_ref, hb)
        ds = dstate[...]
        eg = jnp.exp(L["gl"])
        v_new = L["u"] - _bnn(L["w"], s0)
        d_k_dec = _bnt(v_new, ds)
        d_v_new = _bnn(L["k_dec"], ds) + _btn(L["amat"], dov)
        d_eg = jnp.sum(jnp.sum(ds * s0, axis=2, keepdims=True), axis=1, keepdims=True)
        d_q_dec = _bnt(dov, s0)
        d_a = _bnt(dov, v_new)
        d_w = -_bnt(d_v_new, s0)
        dstate[...] = ds * eg + _btn(L["q_dec"], dov) - _btn(L["w"], d_v_new)
        d_am = jnp.where(lower, d_a * L["decay"], 0.0)
        dq = _bnn(d_am, kv) + d_q_dec * L["egc"]
        dk = _btn(d_am, qv) + d_k_dec * L["kdf"]
        e_col = jnp.sum(d_k_dec * L["k_dec"], axis=2, keepdims=True)
        d_gc_col = jnp.sum(d_q_dec * L["q_dec"], axis=2, keepdims=True) - e_col
        d_gl = jnp.sum(e_col, axis=1, keepdims=True) + d_eg * eg
        tinv = L["tinv"]
        d_rhs_u = _btn(tinv, d_v_new, HIGH)
        d_rhs_w = _btn(tinv, d_w, HIGH)
        d_l = -(_bnt(d_rhs_u, L["u"], HIGH) + _bnt(d_rhs_w, L["w"], HIGH))
        _put_heads(dv_ref, d_rhs_u * L["b_col"])
        d_b_col = jnp.sum(d_rhs_u * vv, axis=2, keepdims=True)
        d_gc_col = d_gc_col + jnp.sum(d_rhs_w * L["rhs_w"], axis=2, keepdims=True)
        d_lm = jnp.where(strict, d_l * L["decay"], 0.0)
        d_kb = d_rhs_w * L["egc"] + _bnn(d_lm, kv)
        dk = dk + _btn(d_lm, L["kb"]) + d_kb * L["b_col"]
        d_b_col = d_b_col + jnp.sum(d_kb * kv, axis=2, keepdims=True)
        m = d_am * L["qk"] + d_lm * L["kk"]
        d_gc_col = d_gc_col + jnp.sum(m, axis=2, keepdims=True)
        d_gc_row = (jnp.sum(jnp.where(eye, jnp.broadcast_to(d_gc_col, shp), 0.0), axis=1, keepdims=True)
                    - jnp.sum(m, axis=1, keepdims=True))
        lane = lax.broadcasted_iota(jnp.int32, (1, 1, c), 2)
        d_gc_row = d_gc_row + jnp.where(lane == c - 1, d_gl, 0.0)
        d_gc_tot = jnp.sum(jnp.where(eye, jnp.broadcast_to(d_gc_row, shp), 0.0), axis=2, keepdims=True)
        dg_ref[:, pl.ds(n, 1), :] = jnp.sum(jnp.where(lower, jnp.broadcast_to(d_gc_tot, shp), 0.0), axis=1,
                                            keepdims=True)
        db_ref[:, pl.ds(n, 1), :] = jnp.sum(jnp.where(eye, jnp.broadcast_to(d_b_col, shp), 0.0), axis=1,
                                            keepdims=True)
        _put_heads(dq_ref, dq)
        _put_heads(dk_ref, dk)
        if fused:
            pl.when(step == nc - 1)(finish)

    tok = pl.BlockSpec((CHUNK, hb * HEAD_DIM), lambda h, s: (nc - 1 - s, h))
    row = pl.BlockSpec((hb, nc, CHUNK), lambda h, s: (h, 0, 0))
    any_spec = pl.BlockSpec(memory_space=pl.ANY)
    in_specs = [tok, tok, tok, row, row,
                pl.BlockSpec((hb, None, HEAD_DIM, HEAD_DIM), lambda h, s: (h, nc - 1 - s, 0, 0)),
                pl.BlockSpec((hb, None, CHUNK, CHUNK), lambda h, s: (h, nc - 1 - s, 0, 0)), tok, tok, tok]
    args = (q, k, v, g_rows, b_rows, s_all, t_all, u_all, w_all, do)
    outs = (_sds((t, D_MODEL)),) * 3 + (_sds((DN_HEADS, nc, CHUNK)),) * 2
    out_specs = (tok, tok, tok, row, row)
    scratch = [pltpu.VMEM((hb, HEAD_DIM, HEAD_DIM), f32)]
    if fused:
        in_specs, args = in_specs + [any_spec], args + (chip_part,)
        outs += (_sds(chip_part.shape, chip_part.dtype),)
        out_specs += (any_spec,)
        scratch = scratch + list(_CHIP_SEMS)
    return _call(body, name, outs, (DN_HEADS // hb, nc), in_specs, out_specs, scratch=scratch)(*args)


SEG = 8


def _perm_rows(a):
    t, c = a.shape
    return a.reshape(SEG, t // SEG, c).transpose(1, 0, 2).reshape(t, c)


def _unperm_rows(a):
    t, c = a.shape
    return a.reshape(t // SEG, SEG, c).transpose(1, 0, 2).reshape(t, c)


def _cmul(ar, ai, br, bi):
    return ar * br - ai * bi, ar * bi + ai * br


def _segment_init(er, ei, lr, li, seg_len, reverse):
    w = er.shape[1]
    sub = lax.broadcasted_iota(jnp.int32, (SEG, w), 0)

    def shift(x, k):
        if reverse:
            return jnp.where(sub < SEG - k, pltpu.roll(x, SEG - k, 0), 0.0)
        return jnp.where(sub >= k, pltpu.roll(x, k, 0), 0.0)

    pr, pi = lr, li
    for _ in range(seg_len.bit_length() - 1):
        pr, pi = _cmul(pr, pi, pr, pi)
    fr, fi = shift(er, 1), shift(ei, 1)
    for k in (1, 2, 4):
        sr, si = shift(fr, k), shift(fi, k)
        mr, mi = _cmul(pr, pi, sr, si)
        fr, fi = fr + mr, fi + mi
        pr, pi = _cmul(pr, pi, pr, pi)
    return fr, fi


def s5_fwd(name, u_perm, bbd, cbd, lam):
    t = u_perm.shape[0]
    tt = _tile(t, (1024, 512, 256, 128))
    nt, ng, w = t // tt, tt // SEG, SG_STATE
    seg_len = t // SEG
    assert seg_len & (seg_len - 1) == 0 and tt % SEG == 0

    def body(u_ref, b_ref, c_ref, lam_ref, y_ref, h_ref, x_scr, state):
        p, i = pl.program_id(1), pl.program_id(2)
        lr1, li1 = lam_ref[:, 0:w], lam_ref[:, w:2 * w]
        lr, li = jnp.broadcast_to(lr1, (SEG, w)), jnp.broadcast_to(li1, (SEG, w))
        x_scr[...] = _bdot(u_ref[...], b_ref[...], ((1,), (0,)))

        @pl.when(jnp.logical_and(p == 0, i == 0))
        def _():
            state[...] = jnp.zeros_like(state)

        @pl.when(jnp.logical_and(p == 1, i == 0))
        def _():
            sr, si = _segment_init(state[:, 0:w], state[:, w:2 * w], lr1, li1, seg_len, False)
            state[:, 0:w] = sr
            state[:, w:2 * w] = si

        def run(store):
            def step(g, st):
                row = pl.multiple_of(g * SEG, SEG)
                xg = x_scr[pl.ds(row, SEG), :]
                nr = lr * st[0] - li * st[1] + xg[:, 0:w]
                ni = lr * st[1] + li * st[0] + xg[:, w:2 * w]
                if store:
                    h_ref[pl.ds(row, SEG), 0:w] = nr
                    h_ref[pl.ds(row, SEG), w:2 * w] = ni
                return nr, ni

            fin = lax.fori_loop(0, ng, step, (state[:, 0:w], state[:, w:2 * w]))
            state[:, 0:w] = fin[0]
            state[:, w:2 * w] = fin[1]

        @pl.when(p == 0)
        def _():
            run(False)

        @pl.when(p == 1)
        def _():
            run(True)
            y_ref[...] = _bdot(h_ref[...], c_ref[...], ((1,), (0,)))

    return _call(body, name, (_sds((t, D_MODEL)), _sds((t, N_SUPER * 2 * w))), (N_SUPER, 2, nt),
                 [pl.BlockSpec((tt, 128), lambda s, p, i: (i, s)),
                  pl.BlockSpec((None, 128, 2 * w), lambda s, p, i: (s, 0, 0)),
                  pl.BlockSpec((None, 2 * w, 128), lambda s, p, i: (s, 0, 0)),
                  pl.BlockSpec((None, 1, 2 * w), lambda s, p, i: (s, 0, 0))],
                 (pl.BlockSpec((tt, 128), lambda s, p, i: (i * p, s)),
                  pl.BlockSpec((tt, 2 * w), lambda s, p, i: (i * p, s))),
                 scratch=[pltpu.VMEM((tt, 2 * w), f32), pltpu.VMEM((SEG, 2 * w), f32)])(u_perm, bbd, cbd, lam)


def s5_bwd(name, dy_perm, u_perm, h_perm, hprev0, bbd, cbd, lam):
    t = u_perm.shape[0]
    tt = _tile(t, (1024, 512, 256, 128))
    nt, ng, w = t // tt, tt // SEG, SG_STATE
    seg_len = t // SEG

    def body(dy_ref, u_ref, h_ref, hp_ref, hp0_ref, b_ref, c_ref, lam_ref, du_ref, db_ref, dc_ref, dl_ref,
             g_scr, state, dl_acc):
        p, i = pl.program_id(1), pl.program_id(2)
        first_tile = jnp.logical_or(p == 0, i == nt - 1)
        lr1, li1 = lam_ref[:, 0:w], -lam_ref[:, w:2 * w]
        lr, li = jnp.broadcast_to(lr1, (SEG, w)), jnp.broadcast_to(li1, (SEG, w))
        g_scr[...] = _bdot(dy_ref[...], c_ref[...], ((1,), (1,)))

        @pl.when(jnp.logical_and(p == 0, i == 0))
        def _():
            state[...] = jnp.zeros_like(state)

        @pl.when(jnp.logical_and(p == 1, i == 0))
        def _():
            sr, si = _segment_init(state[:, 0:w], state[:, w:2 * w], lr1, li1, seg_len, True)
            state[:, 0:w] = sr
            state[:, w:2 * w] = si
            dl_acc[...] = jnp.zeros_like(dl_acc)

        def adj(g, st):
            row = pl.multiple_of(g * SEG, SEG)
            gg = g_scr[pl.ds(row, SEG), :]
            nr = lr * st[0] - li * st[1] + gg[:, 0:w]
            ni = lr * st[1] + li * st[0] + gg[:, w:2 * w]
            return row, nr, ni

        @pl.when(p == 0)
        def _():
            def step(k, st):
                _, nr, ni = adj(ng - 1 - k, st)
                return nr, ni

            fin = lax.fori_loop(0, ng, step, (state[:, 0:w], state[:, w:2 * w]))
            state[:, 0:w] = fin[0]
            state[:, w:2 * w] = fin[1]

        @pl.when(p == 1)
        def _():
            above = jnp.where(first_tile, hp0_ref[...], hp_ref[...])

            def step(k, st):
                g = ng - 1 - k
                row, nr, ni = adj(g, st)
                g_scr[pl.ds(row, SEG), 0:w] = nr
                g_scr[pl.ds(row, SEG), w:2 * w] = ni
                prow = pl.multiple_of(jnp.maximum(g - 1, 0) * SEG, SEG)
                hp = jnp.where(g > 0, h_ref[pl.ds(prow, SEG), :], above)
                pr, pi = hp[:, 0:w], hp[:, w:2 * w]
                return nr, ni, st[2] + nr * pr + ni * pi, st[3] + ni * pr - nr * pi

            fin = lax.fori_loop(0, ng, step, (state[:, 0:w], state[:, w:2 * w], dl_acc[:, 0:w], dl_acc[:, w:2 * w]))
            state[:, 0:w] = fin[0]
            state[:, w:2 * w] = fin[1]
            dl_acc[:, 0:w] = fin[2]
            dl_acc[:, w:2 * w] = fin[3]
            a = g_scr[...]
            du_ref[...] = _bdot(a, b_ref[...], ((1,), (1,)))
            d_b = _bdot(u_ref[...], a, ((0,), (0,)))
            d_c = _bdot(dy_ref[...], h_ref[...], ((0,), (0,)))

            @pl.when(i == 0)
            def _():
                db_ref[...] = d_b
                dc_ref[...] = d_c

            @pl.when(i > 0)
            def _():
                db_ref[...] += d_b
                dc_ref[...] += d_c

            @pl.when(i == nt - 1)
            def _():
                dl_ref[...] = jnp.sum(dl_acc[...], axis=0, keepdims=True)

    tile = lambda s, p, i: (nt - 1 - i, s)
    tile1 = lambda s, p, i: (nt - 1 - i * p, s)
    above = lambda s, p, i: (jnp.maximum((nt - 1 - i * p) * (tt // SEG) - 1, 0), s)
    per_s = lambda s, p, i: (s, 0, 0)
    return _call(body, name, (_sds((t, D_MODEL)), _sds((N_SUPER, 128, 2 * w)), _sds((N_SUPER, 128, 2 * w)),
                              _sds((N_SUPER, 1, 2 * w))), (N_SUPER, 2, nt),
                 [pl.BlockSpec((tt, 128), tile), pl.BlockSpec((tt, 128), tile1), pl.BlockSpec((tt, 2 * w), tile1),
                  pl.BlockSpec((SEG, 2 * w), above), pl.BlockSpec((SEG, 2 * w), lambda s, p, i: (0, s)),
                  pl.BlockSpec((None, 128, 2 * w), per_s), pl.BlockSpec((None, 2 * w, 128), per_s),
                  pl.BlockSpec((None, 1, 2 * w), per_s)],
                 (pl.BlockSpec((tt, 128), tile1), pl.BlockSpec((None, 128, 2 * w), per_s),
                  pl.BlockSpec((None, 128, 2 * w), per_s), pl.BlockSpec((None, 1, 2 * w), per_s)),
                 scratch=[pltpu.VMEM((tt, 2 * w), f32), pltpu.VMEM((SEG, 2 * w), f32),
                          pltpu.VMEM((SEG, 2 * w), f32)])(dy_perm, u_perm, h_perm, h_perm, hprev0, bbd, cbd, lam)


def _s5_disc(a_re, a_im, log_dt, br, bi):
    dt = jnp.exp(log_dt)
    mag = jnp.exp(a_re * dt)
    lr, li = mag * jnp.cos(a_im * dt), mag * jnp.sin(a_im * dt)
    den = a_re * a_re + a_im * a_im
    fr = ((lr - 1.0) * a_re + li * a_im) / den
    fi = (li * a_re - (lr - 1.0) * a_im) / den
    return lr, li, fr * br - fi * bi, fr * bi + fi * br


def s5_disc_fwd(name, a_re, a_im, log_dt, br, bi):
    g, n = SSM_GROUPS, SSM_STATE

    def body(ar, ai, ld, brr, bir, lr, li, bbr, bbi):
        o = _s5_disc(ar[...], ai[...], ld[...], brr[...], bir[...])
        lr[...], li[...], bbr[...], bbi[...] = o

    return _call(body, name, (_sds((g, 1, n)), _sds((g, 1, n)), _sds((g, SSM_GROUP, n)), _sds((g, SSM_GROUP, n))))(
        a_re, a_im, log_dt, br, bi)


def s5_disc_bwd(name, a_re, a_im, log_dt, br, bi, d_lr, d_li, d_bbr, d_bbi):
    g, n = SSM_GROUPS, SSM_STATE

    def body(ar, ai, ld, brr, bir, c1, c2, c3, c4, o1, o2, o3, o4, o5):
        _, vjp = jax.vjp(_s5_disc, ar[...], ai[...], ld[...], brr[...], bir[...])
        o1[...], o2[...], o3[...], o4[...], o5[...] = vjp((c1[...], c2[...], c3[...], c4[...]))

    return _call(body, name, (_sds((g, 1, n)), _sds((g, 1, n)), _sds((g, 1, 1)), _sds((g, SSM_GROUP, n)),
                              _sds((g, SSM_GROUP, n))))(a_re, a_im, log_dt, br, bi, d_lr, d_li, d_bbr, d_bbi)


def gelu_fwd(name, ys_lin, proj, d_skip):
    t, d = ys_lin.shape
    tm = _rows(t, light=True)

    def body(y_ref, u_ref, d_ref, o_ref):
        o_ref[...] = _gelu(y_ref[...] + d_ref[...] * u_ref[...])

    return _call(body, name, _sds((t, d)), (t // tm,),
                 [_row_spec(tm, d), pl.BlockSpec((tm, d), lambda i: (i, 4)), _acc_spec(1, d)],
                 _row_spec(tm, d))(ys_lin, proj, d_skip)


def _head_norm(o, hn):
    outs, ns, rs = [], [], []
    for h in range(DN_HEADS):
        oh = o[:, h * 128:(h + 1) * 128]
        r = lax.rsqrt(jnp.mean(oh * oh, axis=-1, keepdims=True) + EPS)
        n = oh * r
        ns.append(n)
        rs.append(r)
        outs.append(n * hn)
    return outs, ns, rs


def merge_fwd(name, proj, o, yg, glu_lin, head_norm, b_glu):
    t = o.shape[0]
    tm = _rows(t)
    d = D_MODEL

    def body(za_ref, zb_ref, ra_ref, rb_ref, o_ref, yg_ref, gl_ref, hn_ref, bg_ref, m_ref):
        on, _, _ = _head_norm(o_ref[...], hn_ref[...])
        y_a = jnp.concatenate(on, axis=1) * _silu(za_ref[...])
        y_b = yg_ref[...] * _sigmoid(gl_ref[...] + bg_ref[...]) * _silu(zb_ref[...])
        m_ref[...] = (_sigmoid(ra_ref[...]) * y_a + _sigmoid(rb_ref[...]) * y_b).astype(bf16)

    pc = lambda c: pl.BlockSpec((tm, d), lambda i: (i, c))
    return _call(body, name, _sds((t, d), bf16), (t // tm,),
                 [pc(3), pc(5), pc(6), pc(7), _row_spec(tm, d), _row_spec(tm, d), _row_spec(tm, d),
                  _acc_spec(1, 128), _acc_spec(1, d)], _row_spec(tm, d))(
        proj, proj, proj, proj, o, yg, glu_lin, head_norm, b_glu)


def merge_bwd(name, proj, o, yg, glu_lin, head_norm, b_glu, dm):
    t = o.shape[0]
    tm = _rows(t)
    d = D_MODEL

    def body(za_ref, zb_ref, ra_ref, rb_ref, o_ref, yg_ref, gl_ref, hn_ref, bg_ref, dm_ref,
             dza_ref, dzb_ref, dra_ref, drb_ref, do_ref, dgl_ref, dyg_ref, dhn_ref, dbg_ref):
        hn = hn_ref[...]
        za, zb, ra, rb = za_ref[...], zb_ref[...], ra_ref[...], rb_ref[...]
        on, ns, rs = _head_norm(o_ref[...], hn)
        onc = jnp.concatenate(on, axis=1)
        sza = _silu(za)
        y_a = onc * sza
        yg = yg_ref[...]
        sgl = _sigmoid(gl_ref[...] + bg_ref[...])
        y2 = yg * sgl
        szb = _silu(zb)
        y_b = y2 * szb
        sra, srb = _sigmoid(ra), _sigmoid(rb)
        dmv = dm_ref[...]
        dra_ref[...] = (dmv * y_a * sra * (1.0 - sra)).astype(bf16)
        drb_ref[...] = (dmv * y_b * srb * (1.0 - srb)).astype(bf16)
        d_ya = dmv * sra
        d_yb = dmv * srb
        dza_ref[...] = (d_ya * onc * _dsilu(za)).astype(bf16)
        dzb_ref[...] = (d_yb * y2 * _dsilu(zb)).astype(bf16)
        d_on = d_ya * sza
        d_y2 = d_yb * szb
        dyg_ref[...] = d_y2 * sgl
        d_gl = d_y2 * yg * sgl * (1.0 - sgl)
        dgl_ref[...] = d_gl.astype(bf16)
        _accumulate(dbg_ref, jnp.sum(d_gl, axis=0, keepdims=True))
        d_hn = jnp.zeros((1, 128), f32)
        for h in range(DN_HEADS):
            hs = slice(h * 128, (h + 1) * 128)
            dh = d_on[:, hs]
            d_hn = d_hn + jnp.sum(dh * ns[h], axis=0, keepdims=True)
            dn = dh * hn
            do_ref[:, hs] = rs[h] * (dn - ns[h] * jnp.mean(dn * ns[h], axis=-1, keepdims=True))
        _accumulate(dhn_ref, d_hn)

    pc = lambda c: pl.BlockSpec((tm, d), lambda i: (i, c))
    rs_ = _row_spec(tm, d)
    return _call(body, name, (_sds((t, d), bf16),) * 4 + (_sds((t, d)), _sds((t, d), bf16), _sds((t, d)),
                              _sds((1, 128)), _sds((1, d))),
                 (t // tm,), [pc(3), pc(5), pc(6), pc(7), rs_, rs_, rs_, _acc_spec(1, 128), _acc_spec(1, d), rs_],
                 (rs_,) * 7 + (_acc_spec(1, 128), _acc_spec(1, d)))(
        proj, proj, proj, proj, o, yg, glu_lin, head_norm, b_glu, dm)


def gelu_bwd(name, ys_lin, proj, d_skip, dyg_a, dyg_b):
    t, d = ys_lin.shape
    tm = _rows(t, light=True)

    def body(y_ref, u_ref, d_ref, a_ref, b_ref, dys_ref, du_ref, dd_ref):
        uv = u_ref[...]
        dys = (a_ref[...] + b_ref[...]) * _dgelu(y_ref[...] + d_ref[...] * uv)
        dys_ref[...] = dys
        du_ref[...] = dys * d_ref[...]
        _accumulate(dd_ref, jnp.sum(dys * uv, axis=0, keepdims=True))

    rs_ = _row_spec(tm, d)
    return _call(body, name, (_sds((t, d)), _sds((t, d)), _sds((1, d))), (t // tm,),
                 [rs_, pl.BlockSpec((tm, d), lambda i: (i, 4)), _acc_spec(1, d), rs_, rs_],
                 (rs_, rs_, _acc_spec(1, d)))(ys_lin, proj, d_skip, dyg_a, dyg_b)


def assemble_dproj(name, dc, conv_w, dza, du_a, du_b, dzb, dra, drb, dbd):
    t = dza.shape[0]
    tm = _rows(t)
    nt = t // tm
    d = D_MODEL
    w3 = 3 * d

    def body(cur_ref, nxt_ref, w_ref, za_ref, ua_ref, ub_ref, zb_ref, ra_ref, rb_ref, bd_ref, o_ref, pad_ref):
        keep = (pl.program_id(0) < nt - 1).astype(f32)
        pad_ref[0:tm, :] = cur_ref[...]
        pad_ref[tm:tm + 8, :] = nxt_ref[...] * keep
        for cb in range(w3 // 128):
            cs = slice(cb * 128, (cb + 1) * 128)
            acc = pad_ref[pl.ds(CONV_K - 1, tm), cs] * w_ref[0:1, cs]
            for j in range(1, CONV_K):
                acc = acc + pad_ref[pl.ds(CONV_K - 1 - j, tm), cs] * w_ref[j:j + 1, cs]
            o_ref[:, cs] = acc.astype(bf16)
        o_ref[:, 3 * d:4 * d] = za_ref[...].astype(bf16)
        o_ref[:, 4 * d:5 * d] = (ua_ref[...] + ub_ref[...]).astype(bf16)
        o_ref[:, 5 * d:6 * d] = zb_ref[...].astype(bf16)
        o_ref[:, 6 * d:7 * d] = ra_ref[...].astype(bf16)
        o_ref[:, 7 * d:8 * d] = rb_ref[...].astype(bf16)
        o_ref[:, 8 * d:8 * d + 128] = bd_ref[...].astype(bf16)
        o_ref[:, 8 * d + 128:W_PAD] = jnp.zeros((tm, W_PAD - 8 * d - 128), bf16)

    rs_ = _row_spec(tm, d)
    return _call(body, name, _sds((t, W_PAD), bf16), (nt,),
                 [_row_spec(tm, w3), _next_spec(tm, w3, t), _acc_spec(CONV_K, w3), rs_, rs_, rs_, rs_, rs_, rs_,
                  _row_spec(tm, 128)], _row_spec(tm, W_PAD),
                 scratch=[pltpu.VMEM((tm + 8, w3), f32)])(dc, dc, conv_w, dza, du_a, du_b, dzb, dra, drb, dbd)


def adamw(name, w, g, m, v):
    lead, (r, c) = w.shape[:-2], w.shape[-2:]
    tm = _tile(r, (512, 256, 128, 64, 32, 16, 8))
    c1 = 1.0 / (1.0 - ADAM_B1 ** ADAM_STEP)
    c2 = 1.0 / (1.0 - ADAM_B2 ** ADAM_STEP)

    def body(w_ref, g_ref, m_ref, v_ref, d_ref, nm_ref, nv_ref):
        gv = g_ref[...]
        nm = ADAM_B1 * m_ref[...] + (1.0 - ADAM_B1) * gv
        nv = ADAM_B2 * v_ref[...] + (1.0 - ADAM_B2) * (gv * gv)
        d_ref[...] = -ADAM_LR * ((nm * c1) / (jnp.sqrt(nv * c2) + ADAM_EPS) + ADAM_WD * w_ref[...])
        nm_ref[...] = nm
        nv_ref[...] = nv

    if lead:
        sp = pl.BlockSpec((None, tm, c), lambda l, i: (l, i, 0))
        grid = (lead[0], r // tm)
    else:
        sp = pl.BlockSpec((tm, c), lambda i: (i, 0))
        grid = (r // tm,)
    return _call(body, name, (_sds(w.shape),) * 3, grid, [sp] * 4, (sp,) * 3)(w, g, m, v)


def _coords():
    return lax.axis_index("x"), lax.axis_index("y"), lax.axis_index("c")


def _lin(dev):
    return 4 * dev[0] + 2 * dev[1] + dev[2]


def _chips(me):
    x, y, _ = me
    return [(1 - x, y), (x, 1 - y), (1 - x, 1 - y)]


def _gather_steps(x_ref, o_ref, send_sems, recv_sems, local_sem):
    me = _coords()
    x, y, cc = me
    sibling = (x, y, 1 - cc)
    chips = _chips(me)

    def copy(k, block, to, src=None):
        return pltpu.make_async_remote_copy(
            src_ref=o_ref.at[_lin(block)] if src is None else src, dst_ref=o_ref.at[_lin(block)],
            send_sem=send_sems.at[k], recv_sem=recv_sems.at[k], device_id=to, device_id_type=MESH)

    mine = pltpu.make_async_copy(x_ref, o_ref.at[_lin(me)], local_sem)
    first = [copy(0, me, sibling, src=x_ref)] + [copy(1 + j, me, (*chip, cc), src=x_ref)
                                                 for j, chip in enumerate(chips)]
    passed = [copy(4 + j, (*chip, cc), sibling) for j, chip in enumerate(chips)]

    def start():
        mine.start()
        for cp in first:
            cp.start()

    def forward():
        for j, chip in enumerate(chips):
            copy(1 + j, (*chip, cc), me).wait_recv()
            passed[j].start()

    def finish():
        copy(0, sibling, me).wait_recv()
        for j, chip in enumerate(chips):
            copy(4 + j, (*chip, 1 - cc), me).wait_recv()
        for cp in first + passed:
            cp.wait_send()
        mine.wait()

    return start, forward, finish


_GATHER_SEMS = [pltpu.SemaphoreType.DMA((N_DEV - 1,)), pltpu.SemaphoreType.DMA((N_DEV - 1,)),
                pltpu.SemaphoreType.DMA(())]


def all_gather(name, shard):
    r, c = shard.shape

    def body(x_ref, o_ref, send_sems, recv_sems, local_sem):
        for step in _gather_steps(x_ref, o_ref, send_sems, recv_sems, local_sem):
            step()

    any_spec = pl.BlockSpec(memory_space=pl.ANY)
    return _call(body, name, _sds((N_DEV, r, c), shard.dtype), in_specs=[any_spec], out_specs=any_spec,
                 scratch=list(_GATHER_SEMS))(shard)


def pair_exchange(name, blocks):
    _, _, r, c = blocks.shape

    def body(x_ref, o_ref, send_sems, recv_sems):
        x, y, cc = _coords()
        sibling = (x, y, 1 - cc)
        cps = [pltpu.make_async_remote_copy(src_ref=x_ref.at[ch, 1 - cc], dst_ref=o_ref.at[ch], send_sem=send_sems.at[ch],
                                            recv_sem=recv_sems.at[ch], device_id=sibling, device_id_type=MESH)
               for ch in range(4)]
        for cp in cps:
            cp.start()
        for cp in cps:
            cp.wait()

    any_spec = pl.BlockSpec(memory_space=pl.ANY)
    return _call(body, name, _sds((4, r, c), blocks.dtype), in_specs=[any_spec], out_specs=any_spec,
                 scratch=[pltpu.SemaphoreType.DMA((4,)), pltpu.SemaphoreType.DMA((4,))])(blocks)


def _chip_exchange_steps(x_ref, o_ref, send_sems, recv_sems, local_sem):
    me = _coords()
    x, y, cc = me
    my_chip = 2 * x + y
    mine = pltpu.make_async_copy(x_ref.at[my_chip], o_ref.at[my_chip], local_sem)
    peers = [(px, py, cc) for px, py in _chips(me)]
    sends = [pltpu.make_async_remote_copy(src_ref=x_ref.at[2 * px + py], dst_ref=o_ref.at[my_chip],
                                          send_sem=send_sems.at[j], recv_sem=recv_sems.at[j], device_id=(px, py, pc),
                                          device_id_type=MESH) for j, (px, py, pc) in enumerate(peers)]

    def start():
        mine.start()
        for cp in sends:
            cp.start()

    def finish():
        for j, (px, py, pc) in enumerate(peers):
            pltpu.make_async_remote_copy(src_ref=x_ref.at[my_chip], dst_ref=o_ref.at[2 * px + py],
                                         send_sem=send_sems.at[j], recv_sem=recv_sems.at[j], device_id=(px, py, pc),
                                         device_id_type=MESH).wait_recv()
        for cp in sends:
            cp.wait_send()
        mine.wait()

    return start, finish


_CHIP_SEMS = [pltpu.SemaphoreType.DMA((3,)), pltpu.SemaphoreType.DMA((3,)), pltpu.SemaphoreType.DMA(())]


def chip_exchange(name, blocks):
    def body(x_ref, o_ref, send_sems, recv_sems, local_sem):
        for step in _chip_exchange_steps(x_ref, o_ref, send_sems, recv_sems, local_sem):
            step()

    any_spec = pl.BlockSpec(memory_space=pl.ANY)
    return _call(body, name, _sds(blocks.shape, blocks.dtype), in_specs=[any_spec], out_specs=any_spec,
                 scratch=list(_CHIP_SEMS))(blocks)


def pair_sum(name, mine, theirs):
    _, r, c = mine.shape
    tm = max(d for d in range(16, 769, 16) if r % d == 0)

    def body(a_ref, b_ref, o_ref):
        o_ref[...] = (a_ref[...].astype(f32) + b_ref[...].astype(f32)).astype(o_ref.dtype)

    sp = pl.BlockSpec((None, tm, c), lambda ch, i: (ch, i, 0))
    return _call(body, name, _sds(mine.shape, mine.dtype), (4, r // tm), [sp, sp], sp)(mine, theirs)


def sum_slots(name, slots):
    n, r, c = slots.shape
    tm = max(d for d in range(16, 769, 16) if r % d == 0)

    def body(s_ref, o_ref):
        acc = s_ref[0].astype(f32)
        for d in range(1, n):
            acc = acc + s_ref[d].astype(f32)
        o_ref[...] = acc

    return _call(body, name, _sds((r, c)), (r // tm,), [pl.BlockSpec((n, tm, c), lambda i: (0, i, 0))],
                 pl.BlockSpec((tm, c), lambda i: (i, 0)))(slots)


def _rows_layout(col8, t):
    return col8.T.reshape(DN_HEADS, t // CHUNK, CHUNK)


def _blockdiag(m):
    g, a, b = m.shape
    m = m.reshape(N_SUPER, SUPER, a, b)
    out = jnp.einsum("sgab,gh->sgahb", m, jnp.eye(SUPER, dtype=m.dtype))
    return out.reshape(N_SUPER, SUPER * a, SUPER * b)


def _diag_blocks(m, a, b):
    m = m.reshape(N_SUPER, SUPER, a, SUPER, b)
    return jnp.einsum("sgahb,gh->sgab", m, jnp.eye(SUPER, dtype=m.dtype)).reshape(SSM_GROUPS, a, b)


def _s5_params(p, li):
    tag = f"l{li}"
    n = SSM_STATE
    a_re = p["ssm_a_re"].reshape(SSM_GROUPS, 1, n)
    a_im = p["ssm_a_im"].reshape(SSM_GROUPS, 1, n)
    log_dt = p["ssm_log_dt"].reshape(SSM_GROUPS, 1, 1)
    br = jnp.swapaxes(p["ssm_b_re"], 1, 2)
    bi = jnp.swapaxes(p["ssm_b_im"], 1, 2)
    lr, li_, bbr, bbi = s5_disc_fwd("s5_disc_" + tag, a_re, a_im, log_dt, br, bi)
    lam = jnp.concatenate([lr.reshape(N_SUPER, 1, SG_STATE), li_.reshape(N_SUPER, 1, SG_STATE)], axis=-1)
    bbd = jnp.concatenate([_blockdiag(bbr), _blockdiag(bbi)], axis=-1).astype(bf16)
    c_re = jnp.swapaxes(p["ssm_c_re"], 1, 2)
    c_im = jnp.swapaxes(p["ssm_c_im"], 1, 2)
    cbd = jnp.concatenate([_blockdiag(c_re), -_blockdiag(c_im)], axis=1).astype(bf16)
    return dict(a_re=a_re, a_im=a_im, log_dt=log_dt, br=br, bi=bi, lam=lam, bbd=bbd, cbd=cbd)


def layer_fwd(x, p, li, next_shard=None):
    tag = f"l{li}"
    t = x.shape[0]
    d = D_MODEL
    h = rmsnorm_fwd("norm_pre_" + tag, x, p["norm_pre"])
    proj = mm_nn("proj_" + tag, h, p["w_all"])
    q, k, v, bg = conv_qkv_fwd("conv_" + tag, proj, p["conv_w"], p["a_log"], p["dt_bias"])
    b_rows = _rows_layout(bg[:, 0:DN_HEADS], t)
    g_rows = _rows_layout(bg[:, DN_HEADS:2 * DN_HEADS], t)
    o, s_all, t_all, u_all, w_sol, *gathered = delta_fwd("delta_" + tag, q, k, v, g_rows, b_rows, next_shard)
    sp = _s5_params(p, li)
    u_perm = _perm_rows(proj[:, 4 * d:5 * d])
    ys_perm, hs = s5_fwd("s5_" + tag, u_perm, sp["bbd"], sp["cbd"], sp["lam"])
    ys_lin = _unperm_rows(ys_perm)
    yg = gelu_fwd("gelu_" + tag, ys_lin, proj, p["ssm_d"])
    glu_lin = mm_nn("glu_" + tag, yg, p["w_glu"])
    merged = merge_fwd("merge_" + tag, proj, o, yg, glu_lin, p["head_norm"], p["b_glu"])
    out = mm_nn("out_" + tag, merged, p["w_out"])
    y = residual_norm_fwd("norm_post_" + tag, x, out, p["norm_post"])
    saved = dict(x=x, h=h, proj=proj, q=q, k=k, v=v, g_rows=g_rows, b_rows=b_rows, o=o, s_all=s_all, t_all=t_all, u_all=u_all, w_sol=w_sol, sp=sp, u_perm=u_perm,
                 hs=hs, ys_lin=ys_lin, yg=yg, glu_lin=glu_lin, merged=merged, out=out)
    return y, saved, (gathered[0] if gathered else None)


def layer_bwd(dy, p, s, li, chip_part=None):
    tag = f"l{li}"
    t = dy.shape[0]
    d = D_MODEL
    sp = s["sp"]
    gr = {}
    d_out, gr["norm_post"] = post_norm_bwd("norm_post_b_" + tag, s["out"], p["norm_post"], dy)
    d_merged = mm_nt("out_b_" + tag, d_out, p["w_out"])
    gr["w_out"] = mm_tn("out_w_" + tag, s["merged"], d_out, bf16)
    (dza, dzb, dra, drb, d_o, d_glu, dyg_a, gr["head_norm"], gr["b_glu"]) = merge_bwd(
        "merge_b_" + tag, s["proj"], s["o"], s["yg"], s["glu_lin"], p["head_norm"], p["b_glu"], d_merged)
    dyg_b = mm_nt("glu_b_" + tag, d_glu, p["w_glu"])
    gr["w_glu"] = mm_tn("glu_w_" + tag, s["yg"], d_glu, bf16)
    d_ys, du_a, gr["ssm_d"] = gelu_bwd("gelu_b_" + tag, s["ys_lin"], s["proj"], p["ssm_d"], dyg_a, dyg_b)
    hprev0 = jnp.concatenate([jnp.zeros((1, s["hs"].shape[1]), f32), s["hs"][-SEG:-1]], axis=0)
    du_perm, d_bbd, d_cbd, d_lam = s5_bwd("s5_b_" + tag, _perm_rows(d_ys), s["u_perm"], s["hs"], hprev0, sp["bbd"],
                                          sp["cbd"], sp["lam"])
    du_b = _unperm_rows(du_perm)
    gr["ssm_c_re"] = _diag_blocks(d_cbd[:, :, 0:SG_STATE], SSM_GROUP, SSM_STATE)
    gr["ssm_c_im"] = -_diag_blocks(d_cbd[:, :, SG_STATE:], SSM_GROUP, SSM_STATE)
    d_bbr = _diag_blocks(d_bbd[:, :, 0:SG_STATE], SSM_GROUP, SSM_STATE)
    d_bbi = _diag_blocks(d_bbd[:, :, SG_STATE:], SSM_GROUP, SSM_STATE)
    d_lr = d_lam[:, :, 0:SG_STATE].reshape(SSM_GROUPS, 1, SSM_STATE)
    d_li = d_lam[:, :, SG_STATE:].reshape(SSM_GROUPS, 1, SSM_STATE)
    d_are, d_aim, d_ldt, d_br, d_bi = s5_disc_bwd("s5_disc_b_" + tag, sp["a_re"], sp["a_im"], sp["log_dt"], sp["br"],
                                                  sp["bi"], d_lr, d_li, d_bbr, d_bbi)
    gr["ssm_a_re"] = d_are.reshape(SSM_GROUPS, SSM_STATE)
    gr["ssm_a_im"] = d_aim.reshape(SSM_GROUPS, SSM_STATE)
    gr["ssm_log_dt"] = d_ldt.reshape(SSM_GROUPS)
    gr["ssm_b_re"] = jnp.swapaxes(d_br, 1, 2)
    gr["ssm_b_im"] = jnp.swapaxes(d_bi, 1, 2)
    dq, dk, dv, dg_rows, db_rows, *slots = delta_bwd("delta_b_" + tag, s["q"], s["k"], s["v"], s["g_rows"],
                                                     s["b_rows"], s["s_all"], s["t_all"], s["u_all"], s["w_sol"], d_o,
                                                     chip_part)
    dbg = jnp.concatenate([db_rows.reshape(DN_HEADS, t).T, dg_rows.reshape(DN_HEADS, t).T,
                           jnp.zeros((t, 128 - 2 * DN_HEADS), f32)], axis=1)
    dc, gr["conv_w"], dbd, dab = conv_qkv_bwd("conv_b_" + tag, s["proj"], p["conv_w"], p["a_log"], p["dt_bias"],
                                              dq, dk, dv, dbg)
    gr["a_log"] = dab[0, DN_HEADS:2 * DN_HEADS]
    gr["dt_bias"] = dab[1, DN_HEADS:2 * DN_HEADS]
    dproj = assemble_dproj("dproj_" + tag, dc, p["conv_w"], dza, du_a, du_b, dzb, dra, drb, dbd)
    d_h = mm_nt("proj_b_" + tag, dproj, p["w_all"])
    gr["w_all"] = mm_tn("proj_w_" + tag, s["h"], dproj, bf16)
    dx, gr["norm_pre"] = rmsnorm_bwd("norm_pre_b_" + tag, s["x"], p["norm_pre"], d_h, dy)
    return dx, gr, (slots[0] if slots else None)


REPL = ["norm_pre", "a_log", "dt_bias", "head_norm", "ssm_a_re", "ssm_a_im", "ssm_log_dt", "ssm_b_re", "ssm_b_im",
        "ssm_c_re", "ssm_c_im", "ssm_d", "b_glu", "norm_post"]
SHARDED = ["w_in", "conv_w", "w_glu", "w_out"]
ALL_W = ["norm_pre", "w_in", "conv_w", "a_log", "dt_bias", "head_norm", "ssm_a_re", "ssm_a_im", "ssm_log_dt",
         "ssm_b_re", "ssm_b_im", "ssm_c_re", "ssm_c_im", "ssm_d", "w_glu", "b_glu", "w_out", "norm_post"]
PACK_W = 1024


def _pack_flat(arrs, rows):
    flat = jnp.concatenate([a.reshape(-1) for a in arrs])
    return jnp.pad(flat, (0, rows * PACK_W - flat.shape[0])).reshape(rows, PACK_W)


def _flat_rows(arrs, mult=8):
    n = sum(math.prod(a.shape) for a in arrs)
    rows = -(-n // PACK_W)
    return -(-rows // mult) * mult


def _unpack(flat, shapes):
    out, off = [], 0
    for sh in shapes:
        n = math.prod(sh)
        out.append(flat[off:off + n].reshape(sh))
        off += n
    return out


def _repl_split(shapes):
    big = [n for n in REPL if math.prod(shapes[n]) % PACK_W == 0]
    small = [n for n in REPL if n not in big]
    return big, small


def _rows8(n):
    return -(-n // (8 * PACK_W)) * 8


def _repl_rows(shapes):
    big, small = _repl_split(shapes)
    rows = sum(_rows8(math.prod(shapes[n])) for n in big) + _flat_rows([_sds(shapes[n]) for n in small], 8)
    return -(-rows // (16 * N_DEV)) * (16 * N_DEV)


def _repl_pack(arrs, shapes):
    big, small = _repl_split(shapes)
    parts = []
    for n in big:
        a = arrs[n].reshape(-1, PACK_W)
        parts.append(jnp.pad(a, ((0, _rows8(a.size) - a.shape[0]), (0, 0))))
    parts.append(_pack_flat([arrs[n] for n in small], _flat_rows([_sds(shapes[n]) for n in small], 8)))
    used = sum(p.shape[0] for p in parts)
    parts.append(jnp.zeros((_repl_rows(shapes) - used, PACK_W), parts[0].dtype))
    return jnp.concatenate(parts, axis=0)


def _repl_unpack(packed, shapes):
    big, small = _repl_split(shapes)
    out, off = {}, 0
    for n in big:
        size = math.prod(shapes[n])
        out[n] = packed[off:off + size // PACK_W].reshape(shapes[n])
        off += _rows8(size)
    srows = _flat_rows([_sds(shapes[n]) for n in small], 8)
    out.update(zip(small, _unpack(packed[off:off + srows].reshape(-1), [shapes[n] for n in small])))
    return out


_COL_RUNS = ((0, 4096), (4112, W_COLS), (4096, 4112))


def _w_all(main, tails, wc):
    pieces = []
    for lo, hi in _COL_RUNS:
        for dv in range(N_DEV):
            a, b = max(lo, dv * wc) - dv * wc, min(hi, (dv + 1) * wc) - dv * wc
            if a < min(b, PACK_W):
                pieces.append(main[dv][:, :, a:min(b, PACK_W)])
            if b > max(a, PACK_W):
                pieces.append(tails[dv][:, :, max(a, PACK_W) - PACK_W:b - PACK_W])
    ll, rows = main[0].shape[0], main[0].shape[1]
    pieces.append(jnp.zeros((ll, rows, W_PAD - W_COLS), main[0].dtype))
    return jnp.concatenate(pieces, axis=2)


def _ref_cols(g, lo, hi):
    pieces, off = [], 0
    for a, b in _COL_RUNS:
        s, e = max(lo, a), min(hi, b)
        if s < e:
            pieces.append((s, g[..., off + s - a:off + e - a]))
        off += b - a
    pieces.sort(key=lambda t: t[0])
    return jnp.concatenate([p for _, p in pieces], axis=-1) if len(pieces) > 1 else pieces[0][1]


def kernel(x, norm_pre, w_in, conv_w, a_log, dt_bias, head_norm, ssm_a_re, ssm_a_im, ssm_log_dt, ssm_b_re, ssm_b_im, ssm_c_re, ssm_c_im, ssm_d, w_glu, b_glu, w_out, norm_post, loss_target, m_norm_pre, m_w_in, m_conv_w, m_a_log, m_dt_bias, m_head_norm, m_ssm_a_re, m_ssm_a_im, m_ssm_log_dt, m_ssm_b_re, m_ssm_b_im, m_ssm_c_re, m_ssm_c_im, m_ssm_d, m_w_glu, m_b_glu, m_w_out, m_norm_post, v_norm_pre, v_w_in, v_conv_w, v_a_log, v_dt_bias, v_head_norm, v_ssm_a_re, v_ssm_a_im, v_ssm_log_dt, v_ssm_b_re, v_ssm_b_im, v_ssm_c_re, v_ssm_c_im, v_ssm_d, v_w_glu, v_b_glu, v_w_out, v_norm_post):
    loc = dict(locals())
    w = {n: loc[n] for n in ALL_W}
    m = {n: loc["m_" + n] for n in ALL_W}
    v = {n: loc["v_" + n] for n in ALL_W}
    depth = w_in.shape[0]
    wc = w_in.shape[2]
    cc = conv_w.shape[2]
    wr = w_glu.shape[1]

    tail = wc - PACK_W
    conv_hi = conv_w.astype(bf16)
    conv_mid = (conv_w - conv_hi.astype(f32)).astype(bf16)
    conv_lo = (conv_w - conv_hi.astype(f32) - conv_mid.astype(f32)).astype(bf16)
    conv3 = jnp.stack([conv_hi, conv_mid, conv_lo], axis=1)
    w_in_b = w_in.astype(bf16)
    MISC = 16
    o_glu, o_out, o_tail, o_conv = D_MODEL, D_MODEL + wr, D_MODEL + 2 * wr, D_MODEL + 2 * wr + MISC

    def rows16(a):
        return jnp.pad(a, ((0, MISC - a.shape[0]), (0, PACK_W - a.shape[1])))

    def layer_shard(li):
        return jnp.concatenate([w_in_b[li, :, :PACK_W], w_glu[li].astype(bf16), w_out[li].astype(bf16),
                                rows16(w_in_b[li, :, PACK_W:].T), rows16(conv3[li].reshape(3 * CONV_K, cc))])

    def layer_weights(gathered):
        tails = gathered[:, o_tail:o_tail + tail].reshape(N_DEV * tail, PACK_W).T
        w_all = _w_all([gathered[dv, :D_MODEL][None] for dv in range(N_DEV)],
                       [tails[None, :, dv * tail:(dv + 1) * tail] for dv in range(N_DEV)], wc)[0]
        glu = gathered[:, o_glu:o_glu + wr].reshape(N_DEV * wr, D_MODEL)
        out = gathered[:, o_out:o_out + wr].reshape(N_DEV * wr, D_MODEL)
        conv = gathered[:, o_conv:o_conv + 3 * CONV_K, :cc].astype(f32).reshape(N_DEV, 3, CONV_K, cc)
        conv = (conv[:, 0] + conv[:, 1] + conv[:, 2]).transpose(1, 0, 2).reshape(CONV_K, N_DEV * cc)
        return dict(w_all=w_all, w_glu=glu, w_out=out, conv_w=conv)

    def layer_params(li, gathered):
        return dict(layer_weights(gathered), norm_pre=norm_pre[li].reshape(1, -1),
                    a_log=jnp.pad(a_log[li], (DN_HEADS, 128 - 2 * DN_HEADS)).reshape(1, 128),
                    dt_bias=jnp.pad(dt_bias[li], (DN_HEADS, 128 - 2 * DN_HEADS)).reshape(1, 128),
                    head_norm=head_norm[li].reshape(1, -1), ssm_a_re=ssm_a_re[li], ssm_a_im=ssm_a_im[li],
                    ssm_log_dt=ssm_log_dt[li], ssm_b_re=ssm_b_re[li], ssm_b_im=ssm_b_im[li], ssm_c_re=ssm_c_re[li],
                    ssm_c_im=ssm_c_im[li], ssm_d=ssm_d[li].reshape(1, -1),
                    b_glu=b_glu[li].reshape(1, -1), norm_post=norm_post[li].reshape(1, -1))

    act = x[0]
    saved, params = [], []
    gathered = all_gather("gather_weights", layer_shard(0))
    for li in range(depth):
        params.append(layer_params(li, gathered))
        act, sv, gathered = layer_fwd(act, params[li], li, layer_shard(li + 1) if li + 1 < depth else None)
        saved.append(sv)
    loss_part, dy = loss_head("loss_head", act, loss_target[0])
    repl_shapes = {n: w[n].shape for n in REPL}
    repl_rows = _repl_rows(repl_shapes)
    rr = repl_rows // N_DEV
    my_c = lax.axis_index("c")

    def chip_sums(li, gl, extra=None):
        g_conv = gl["conv_w"].astype(bf16)
        tails = jnp.concatenate([_ref_cols(gl["w_all"], dv * wc + PACK_W, (dv + 1) * wc) for dv in range(N_DEV)],
                                axis=1).T
        blocks = jnp.stack([jnp.concatenate(
            [_ref_cols(gl["w_all"], dv * wc, dv * wc + PACK_W), gl["w_glu"][dv * wr:(dv + 1) * wr],
             gl["w_out"][dv * wr:(dv + 1) * wr], rows16(tails[dv * tail:(dv + 1) * tail]),
             rows16(g_conv[:, dv * cc:(dv + 1) * cc])]
            + ([extra[dv * rr:(dv + 1) * rr]] if extra is not None else []))
            for dv in range(N_DEV)])
        blocks = blocks.reshape(4, 2, blocks.shape[1], PACK_W)
        from_sibling = pair_exchange(f"pair_grads_l{li}", blocks)
        own = lax.dynamic_index_in_dim(blocks, my_c, axis=1, keepdims=False)
        return pair_sum(f"pair_sum_grads_l{li}", own, from_sibling)

    grads, slots, pending = [None] * depth, [None] * depth, None
    for li in reversed(range(depth)):
        dy, grads[li], got = layer_bwd(dy, params[li], saved[li], li, pending)
        if pending is not None:
            slots[li + 1] = got
        g_repl = None
        if li == 0:
            g_repl = _repl_pack({n: jnp.stack([grads[l][n] for l in range(depth)]).reshape(w[n].shape) for n in REPL},
                                repl_shapes).astype(bf16)
        pending = chip_sums(li, grads[li], g_repl)
    slots[0] = chip_exchange("scatter_grads", pending)
    grad_x = dy[None]
    loss = lax.psum(loss_part[0, 0], ("x", "y", "c"))
    mine = [sum_slots(f"sum_grads_l{li}", slots[li]) for li in range(depth)]
    o_repl = o_conv + MISC
    gs_w_in = jnp.stack([jnp.concatenate([mi[:o_glu], mi[o_tail:o_tail + tail].T], axis=1) for mi in mine])
    gs_glu = jnp.stack([mi[o_glu:o_out] for mi in mine])
    gs_out = jnp.stack([mi[o_out:o_tail] for mi in mine])
    gs_conv = jnp.stack([mi[o_conv:o_conv + CONV_K, :cc] for mi in mine])
    g_repl_full = all_gather("gather_repl_grads", mine[0][o_repl:o_repl + rr]).reshape(repl_rows, PACK_W)
    g = _repl_unpack(g_repl_full, repl_shapes)
    g.update(w_in=gs_w_in, w_glu=gs_glu, w_out=gs_out, conv_w=gs_conv)

    delta, new_m, new_v = {}, {}, {}
    for n in SHARDED:
        delta[n], new_m[n], new_v[n] = adamw("adamw_" + n, w[n], g[n], m[n], v[n])
    outs = adamw("adamw_repl", *[_repl_pack({n: src[n] for n in REPL}, repl_shapes) for src in (w, g, m, v)])
    for dst, o in zip((delta, new_m, new_v), outs):
        dst.update(_repl_unpack(o, repl_shapes))
    return (loss, grad_x, *[g[n] for n in ALL_W], *[delta[n] for n in ALL_W], *[new_m[n] for n in ALL_W],
            *[new_v[n] for n in ALL_W])
```

```python
import math

import jax
import jax.numpy as jnp
from jax import lax
from jax.experimental import pallas as pl
from jax.experimental.pallas import tpu as pltpu

f32 = jnp.float32
bf16 = jnp.bfloat16

D_MODEL = 1024
N_DEV = 8
DN_HEADS = 8
HEAD_DIM = 128
CHUNK = 64
CONV_K = 4
SSM_GROUPS = 64
SSM_GROUP = 16
SSM_STATE = 64
SUPER = 8
N_SUPER = SSM_GROUPS // SUPER
SG_STATE = SUPER * SSM_STATE
EPS = 1e-6
W_COLS = 8208
W_PAD = 8448
ADAM_LR, ADAM_B1, ADAM_B2, ADAM_EPS, ADAM_WD, ADAM_STEP = 0.001, 0.9, 0.999, 1e-08, 0.01, 10
VMEM_LIMIT = 56 * 1024 * 1024
MESH = pl.DeviceIdType.MESH
HIGH = lax.Precision.HIGH
DELTA_HB = 8


def _call(body, name, out_shape, grid=None, in_specs=None, out_specs=None, scratch=(), **kw):
    args = dict(out_shape=out_shape, name=name, scratch_shapes=list(scratch),
                compiler_params=pltpu.CompilerParams(vmem_limit_bytes=VMEM_LIMIT, **kw))
    if grid is not None:
        args.update(grid=grid, in_specs=in_specs, out_specs=out_specs)
    else:
        if in_specs is not None:
            args.update(in_specs=in_specs)
        if out_specs is not None:
            args.update(out_specs=out_specs)
    return pl.pallas_call(body, **args)


def _sds(shape, dtype=f32):
    return jax.ShapeDtypeStruct(tuple(shape), dtype)


def _sigmoid(x):
    return 1.0 / (1.0 + jnp.exp(-x))


def _silu(x):
    return x * _sigmoid(x)


def _dsilu(x):
    s = _sigmoid(x)
    return s * (1.0 + x * (1.0 - s))


_GELU_C = math.sqrt(2.0 / math.pi)


def _gelu(x):
    return 0.5 * x * (1.0 + jnp.tanh(_GELU_C * (x + 0.044715 * x * x * x)))


def _dgelu(x):
    t = jnp.tanh(_GELU_C * (x + 0.044715 * x * x * x))
    return 0.5 * (1.0 + t) + 0.5 * x * (1.0 - t * t) * _GELU_C * (1.0 + 3 * 0.044715 * x * x)


def _softplus(x):
    return jnp.maximum(x, 0.0) + jnp.log(1.0 + jnp.exp(-jnp.abs(x)))


def _bdot(a, b, dn):
    return lax.dot_general(a.astype(bf16), b.astype(bf16), (dn, ((), ())), preferred_element_type=f32)


def _matmul(name, a, b, *, dn, grid, a_spec, b_spec, o_spec, o_shape, o_dtype=f32):
    nk = grid[-1]

    def body(a_ref, b_ref, o_ref, acc_ref):
        p = _bdot(a_ref[...], b_ref[...], dn)
        if nk == 1:
            o_ref[...] = p.astype(o_dtype)
        else:
            k = pl.program_id(len(grid) - 1)

            @pl.when(k == 0)
            def _():
                acc_ref[...] = p

            @pl.when(k > 0)
            def _():
                acc_ref[...] += p

            @pl.when(k == nk - 1)
            def _():
                o_ref[...] = acc_ref[...].astype(o_dtype)

    blk = tuple(d for d in o_spec.block_shape if d is not None)
    return _call(body, name, _sds(o_shape, o_dtype), grid, [a_spec, b_spec], o_spec,
                 scratch=[pltpu.VMEM(blk if nk > 1 else (8, 128), f32)])(a, b)


def _tile(n, pref):
    for t in pref:
        if n % t == 0:
            return t
    return n


def mm_nn(name, a, b):
    m, k = a.shape
    n = b.shape[1]
    tm, tn, tk = _tile(m, (1024, 512, 256)), _tile(n, (2816, 1024, 512)), _tile(k, (2816, 1024))
    return _matmul(name, a, b, dn=((1,), (0,)), grid=(m // tm, n // tn, k // tk),
                   a_spec=pl.BlockSpec((tm, tk), lambda i, j, l: (i, l)),
                   b_spec=pl.BlockSpec((tk, tn), lambda i, j, l: (l, j)),
                   o_spec=pl.BlockSpec((tm, tn), lambda i, j, l: (i, j)), o_shape=(m, n))


def mm_nt(name, a, b):
    m, k = a.shape
    n = b.shape[0]
    tm, tn, tk = _tile(m, (1024, 512, 256)), _tile(n, (1024, 512)), _tile(k, (2816, 1024))
    return _matmul(name, a, b, dn=((1,), (1,)), grid=(m // tm, n // tn, k // tk),
                   a_spec=pl.BlockSpec((tm, tk), lambda i, j, l: (i, l)),
                   b_spec=pl.BlockSpec((tn, tk), lambda i, j, l: (j, l)),
                   o_spec=pl.BlockSpec((tm, tn), lambda i, j, l: (i, j)), o_shape=(m, n))


def mm_tn(name, a, b, o_dtype=f32):
    k, m = a.shape
    n = b.shape[1]
    tm, tn, tk = _tile(m, (512,)), _tile(n, (2816, 1024, 512)), _tile(k, (512, 256))
    return _matmul(name, a, b, dn=((0,), (0,)), grid=(m // tm, n // tn, k // tk),
                   a_spec=pl.BlockSpec((tk, tm), lambda i, j, l: (l, i)),
                   b_spec=pl.BlockSpec((tk, tn), lambda i, j, l: (l, j)),
                   o_spec=pl.BlockSpec((tm, tn), lambda i, j, l: (i, j)), o_shape=(m, n), o_dtype=o_dtype)


def _rows(t, light=False):
    return _tile(t, (512, 256) if light else (256,))


def _row_spec(tm, w):
    return pl.BlockSpec((tm, w), lambda i: (i, 0))


def _acc_spec(r, w):
    return pl.BlockSpec((r, w), lambda i: (0, 0))


def _accumulate(ref, val):
    @pl.when(pl.program_id(0) == 0)
    def _():
        ref[...] = val

    @pl.when(pl.program_id(0) > 0)
    def _():
        ref[...] += val


def rmsnorm_fwd(name, x, gain):
    t, d = x.shape
    tm = _rows(t, light=True)

    def body(x_ref, g_ref, o_ref):
        xv = x_ref[...]
        r = lax.rsqrt(jnp.mean(xv * xv, axis=-1, keepdims=True) + EPS)
        o_ref[...] = (xv * r * g_ref[...]).astype(bf16)

    return _call(body, name, _sds((t, d), bf16), (t // tm,), [_row_spec(tm, d), _acc_spec(1, d)],
                 _row_spec(tm, d))(x, gain)


def rmsnorm_bwd(name, x, gain, dn, dres):
    t, d = x.shape
    tm = _rows(t, light=True)

    def body(x_ref, g_ref, dn_ref, dr_ref, dx_ref, dg_ref):
        xv = x_ref[...]
        r = lax.rsqrt(jnp.mean(xv * xv, axis=-1, keepdims=True) + EPS)
        n = xv * r
        dnv = dn_ref[...]
        _accumulate(dg_ref, jnp.sum(dnv * n, axis=0, keepdims=True))
        dng = dnv * g_ref[...]
        dx_ref[...] = dr_ref[...] + r * (dng - n * jnp.mean(dng * n, axis=-1, keepdims=True))

    return _call(body, name, (_sds((t, d)), _sds((1, d))), (t // tm,),
                 [_row_spec(tm, d), _acc_spec(1, d), _row_spec(tm, d), _row_spec(tm, d)],
                 (_row_spec(tm, d), _acc_spec(1, d)))(x, gain, dn, dres)


def residual_norm_fwd(name, x, out, gain):
    t, d = x.shape
    tm = _rows(t, light=True)

    def body(x_ref, o_ref, g_ref, y_ref):
        ov = o_ref[...]
        r = lax.rsqrt(jnp.mean(ov * ov, axis=-1, keepdims=True) + EPS)
        y_ref[...] = x_ref[...] + ov * r * g_ref[...]

    return _call(body, name, _sds((t, d)), (t // tm,), [_row_spec(tm, d), _row_spec(tm, d), _acc_spec(1, d)],
                 _row_spec(tm, d))(x, out, gain)


def post_norm_bwd(name, out, gain, dy):
    t, d = out.shape
    tm = _rows(t, light=True)

    def body(o_ref, g_ref, dy_ref, do_ref, dg_ref):
        ov = o_ref[...]
        r = lax.rsqrt(jnp.mean(ov * ov, axis=-1, keepdims=True) + EPS)
        n = ov * r
        dyv = dy_ref[...]
        _accumulate(dg_ref, jnp.sum(dyv * n, axis=0, keepdims=True))
        dng = dyv * g_ref[...]
        do_ref[...] = (r * (dng - n * jnp.mean(dng * n, axis=-1, keepdims=True))).astype(bf16)

    return _call(body, name, (_sds((t, d), bf16), _sds((1, d))), (t // tm,),
                 [_row_spec(tm, d), _acc_spec(1, d), _row_spec(tm, d)],
                 (_row_spec(tm, d), _acc_spec(1, d)))(out, gain, dy)


def loss_head(name, y, target):
    t, d = y.shape
    tm = _rows(t, light=True)

    def body(y_ref, t_ref, l_ref, dy_ref):
        e = y_ref[...] - t_ref[...]
        dy_ref[...] = e * (1.0 / d)
        s = jnp.sum(jnp.sum(e * e, axis=1, keepdims=True), axis=0, keepdims=True) * (0.5 / d)
        _accumulate(l_ref, s)

    return _call(body, name, (_sds((1, 1)), _sds((t, d))), (t // tm,),
                 [_row_spec(tm, d), _row_spec(tm, d)], (_acc_spec(1, 1), _row_spec(tm, d)))(y, target)


def _prev_spec(tm, w):
    return pl.BlockSpec((8, w), lambda i: (jnp.maximum(i * (tm // 8) - 1, 0), 0))


def _next_spec(tm, w, t):
    return pl.BlockSpec((8, w), lambda i: (jnp.minimum((i + 1) * (tm // 8), t // 8 - 1), 0))


def _fill_pad(pad_ref, prev_ref, cur_ref, tm):
    keep = (pl.program_id(0) > 0).astype(f32)
    pad_ref[0:8, :] = prev_ref[...] * keep
    pad_ref[8:8 + tm, :] = cur_ref[...]


def _conv_block(pad_ref, w_ref, cb, tm):
    cs = slice(cb * 128, (cb + 1) * 128)
    acc = pad_ref[pl.ds(8 - (CONV_K - 1), tm), cs] * w_ref[0:1, cs]
    for j in range(1, CONV_K):
        acc = acc + pad_ref[pl.ds(8 - (CONV_K - 1) + j, tm), cs] * w_ref[j:j + 1, cs]
    return acc


def conv_qkv_fwd(name, proj, conv_w, a_log, dt_bias):
    t = proj.shape[0]
    tm = _rows(t)
    scale = HEAD_DIM ** -0.5

    def body(cur_ref, prev_ref, w_ref, bd_ref, al_ref, db_ref, q_ref, k_ref, v_ref, bg_ref, pad_ref):
        _fill_pad(pad_ref, prev_ref, cur_ref, tm)
        for cb in range(3 * DN_HEADS):
            s = _silu(_conv_block(pad_ref, w_ref, cb, tm))
            hs = slice((cb % DN_HEADS) * 128, (cb % DN_HEADS + 1) * 128)
            if cb < DN_HEADS:
                q_ref[:, hs] = s * (lax.rsqrt(jnp.sum(s * s, axis=-1, keepdims=True) + EPS) * scale)
            elif cb < 2 * DN_HEADS:
                k_ref[:, hs] = s * lax.rsqrt(jnp.sum(s * s, axis=-1, keepdims=True) + EPS)
            else:
                v_ref[:, hs] = s
        bd = bd_ref[...]
        beta = _sigmoid(bd)
        g = -jnp.exp(al_ref[...]) * _softplus(bd + db_ref[...])
        lane = lax.broadcasted_iota(jnp.int32, bd.shape, 1)
        bg_ref[...] = jnp.where(lane < DN_HEADS, beta, jnp.where(lane < 2 * DN_HEADS, g, 0.0))

    w3 = 3 * D_MODEL
    return _call(body, name, (_sds((t, D_MODEL)),) * 3 + (_sds((t, 128)),), (t // tm,),
                 [pl.BlockSpec((tm, w3), lambda i: (i, 0)), _prev_spec(tm, w3), _acc_spec(CONV_K, w3),
                  pl.BlockSpec((tm, 128), lambda i: (i, 8192 // 128)), _acc_spec(1, 128), _acc_spec(1, 128)],
                 (_row_spec(tm, D_MODEL),) * 3 + (_row_spec(tm, 128),),
                 scratch=[pltpu.VMEM((tm + 8, w3), f32)])(proj, proj, conv_w, proj, a_log, dt_bias)


def conv_qkv_bwd(name, proj, conv_w, a_log, dt_bias, dq, dk, dv, dbg):
    t = proj.shape[0]
    tm = _rows(t)
    scale = HEAD_DIM ** -0.5

    def body(cur_ref, prev_ref, w_ref, bd_ref, al_ref, db_ref, dq_ref, dk_ref, dv_ref, dbg_ref,
             dc_ref, dw_ref, dbd_ref, dab_ref, pad_ref):
        _fill_pad(pad_ref, prev_ref, cur_ref, tm)

        @pl.when(pl.program_id(0) == 0)
        def _():
            dw_ref[...] = jnp.zeros_like(dw_ref)

        for cb in range(3 * DN_HEADS):
            cs = slice(cb * 128, (cb + 1) * 128)
            hs = slice((cb % DN_HEADS) * 128, (cb % DN_HEADS + 1) * 128)
            taps = [pad_ref[pl.ds(8 - (CONV_K - 1) + j, tm), cs] for j in range(CONV_K)]
            c = taps[0] * w_ref[0:1, cs]
            for j in range(1, CONV_K):
                c = c + taps[j] * w_ref[j:j + 1, cs]
            sg = _sigmoid(c)
            s = c * sg
            if cb < 2 * DN_HEADS:
                dn = (dq_ref[:, hs] * scale) if cb < DN_HEADS else dk_ref[:, hs]
                r = lax.rsqrt(jnp.sum(s * s, axis=-1, keepdims=True) + EPS)
                ds = r * dn - s * (r * r * r) * jnp.sum(dn * s, axis=-1, keepdims=True)
            else:
                ds = dv_ref[:, hs]
            dc = ds * (sg * (1.0 + c * (1.0 - sg)))
            dc_ref[:, cs] = dc
            for j in range(CONV_K):
                dw_ref[j:j + 1, cs] += jnp.sum(dc * taps[j], axis=0, keepdims=True)
        bd = bd_ref[...]
        dbg_v = dbg_ref[...]
        lane = lax.broadcasted_iota(jnp.int32, bd.shape, 1)
        sg = _sigmoid(bd)
        ea = jnp.exp(al_ref[...])
        z = bd + db_ref[...]
        sp = _softplus(z)
        is_b = lane < DN_HEADS
        is_g = jnp.logical_and(lane >= DN_HEADS, lane < 2 * DN_HEADS)
        d_z = jnp.where(is_g, dbg_v * (-ea) * _sigmoid(z), 0.0)
        dbd_ref[...] = jnp.where(is_b, dbg_v * sg * (1.0 - sg), d_z)
        d_al = jnp.sum(jnp.where(is_g, dbg_v * (-ea) * sp, 0.0), axis=0, keepdims=True)
        d_db = jnp.sum(d_z, axis=0, keepdims=True)
        _accumulate(dab_ref, jnp.concatenate([d_al, d_db] + [jnp.zeros_like(d_al)] * 6, axis=0))

    w3 = 3 * D_MODEL
    return _call(body, name, (_sds((t, w3)), _sds((CONV_K, w3)), _sds((t, 128)), _sds((8, 128))), (t // tm,),
                 [pl.BlockSpec((tm, w3), lambda i: (i, 0)), _prev_spec(tm, w3), _acc_spec(CONV_K, w3),
                  pl.BlockSpec((tm, 128), lambda i: (i, 8192 // 128)), _acc_spec(1, 128), _acc_spec(1, 128),
                  _row_spec(tm, D_MODEL), _row_spec(tm, D_MODEL), _row_spec(tm, D_MODEL), _row_spec(tm, 128)],
                 (_row_spec(tm, w3), _acc_spec(CONV_K, w3), _row_spec(tm, 128), _acc_spec(8, 128)),
                 scratch=[pltpu.VMEM((tm + 8, w3), f32)])(proj, proj, conv_w, proj, a_log, dt_bias, dq, dk, dv, dbg)


def _bdg(a, b, ca, cb, prec=None):
    if prec is None:
        a, b = a.astype(bf16), b.astype(bf16)
    return lax.dot_general(a, b, (((ca,), (cb,)), ((0,), (0,))), precision=prec, preferred_element_type=f32)


def _bnn(a, b, prec=None):
    return _bdg(a, b, 2, 1, prec)


def _bnt(a, b, prec=None):
    return _bdg(a, b, 2, 2, prec)


def _btn(a, b, prec=None):
    return _bdg(a, b, 1, 1, prec)


def _delta_local(q, k, v, g_row, b_row, solved=None):
    c = CHUNK
    ii = lax.broadcasted_iota(jnp.int32, (c, c), 0)
    jj = lax.broadcasted_iota(jnp.int32, (c, c), 1)
    eye, lower, strict = ii == jj, ii >= jj, ii > jj
    shp = (q.shape[0], c, c)
    g_b = jnp.broadcast_to(g_row, shp)
    gc_col = jnp.sum(jnp.where(lower, g_b, 0.0), axis=2, keepdims=True)
    gc_row = jnp.sum(jnp.where(eye, jnp.broadcast_to(gc_col, shp), 0.0), axis=1, keepdims=True)
    b_col = jnp.sum(jnp.where(eye, jnp.broadcast_to(b_row, shp), 0.0), axis=2, keepdims=True)
    gl = jnp.sum(g_row, axis=2, keepdims=True)
    decay = jnp.exp(jnp.where(lower, gc_col - gc_row, -1e30))
    kb = k * b_col
    kk = _bnt(kb, k)
    lmat = jnp.where(strict, kk * decay, 0.0)
    egc = jnp.exp(gc_col)
    rhs_w = kb * egc
    if solved is None:
        tinv = eye.astype(f32) - lmat
        pw = lmat
        for _ in range(5):
            pw = _bnn(pw, pw, HIGH)
            tinv = tinv + _bnn(tinv, pw, HIGH)
        uw = _bnn(tinv, jnp.concatenate([v * b_col, rhs_w], axis=2), HIGH)
        u, w = uw[:, :, 0:HEAD_DIM], uw[:, :, HEAD_DIM:]
    else:
        tinv, u, w = solved
    qk = _bnt(q, k)
    amat = jnp.where(lower, qk * decay, 0.0)
    q_dec = q * egc
    kdf = jnp.exp(gl - gc_col)
    k_dec = k * kdf
    return dict(eye=eye, lower=lower, strict=strict, gc_col=gc_col, b_col=b_col, gl=gl, decay=decay,
                kb=kb, kk=kk, tinv=tinv, egc=egc, rhs_w=rhs_w, u=u, w=w, qk=qk, amat=amat, q_dec=q_dec, kdf=kdf,
                k_dec=k_dec)


def _heads(ref, hb):
    return jnp.stack([ref[:, h * HEAD_DIM:(h + 1) * HEAD_DIM] for h in range(hb)])


def _put_heads(ref, val):
    for h in range(val.shape[0]):
        ref[:, h * HEAD_DIM:(h + 1) * HEAD_DIM] = val[h]


def delta_fwd(name, q, k, v, g_rows, b_rows, next_shard=None):
    t = q.shape[0]
    nc = t // CHUNK
    hb = DELTA_HB
    fused = next_shard is not None
    assert not fused or (hb == DN_HEADS and nc >= 2)

    def body(*refs):
        if fused:
            (q_ref, k_ref, v_ref, g_ref, b_ref, x_ref, o_ref, s_ref, t_ref, u_ref, w_ref, gath_ref, state, send_sems,
             recv_sems, local_sem) = refs
        else:
            q_ref, k_ref, v_ref, g_ref, b_ref, o_ref, s_ref, t_ref, u_ref, w_ref, state = refs
        n = pl.program_id(1)
        if fused:
            start, forward, finish = _gather_steps(x_ref, gath_ref, send_sems, recv_sems, local_sem)
            pl.when(n == 0)(start)

        @pl.when(n == 0)
        def _():
            state[...] = jnp.zeros_like(state)

        loc = _delta_local(_heads(q_ref, hb), _heads(k_ref, hb), _heads(v_ref, hb), g_ref[:, pl.ds(n, 1), :],
                           b_ref[:, pl.ds(n, 1), :])
        s0 = state[...]
        s_ref[...] = s0
        t_ref[...] = loc["tinv"]
        _put_heads(u_ref, loc["u"])
        _put_heads(w_ref, loc["w"])
        v_new = loc["u"] - _bnn(loc["w"], s0)
        _put_heads(o_ref, _bnn(loc["q_dec"], s0) + _bnn(loc["amat"], v_new))
        state[...] = s0 * jnp.exp(loc["gl"]) + _btn(loc["k_dec"], v_new)
        if fused:
            pl.when(n == (3 * nc) // 4)(forward)
            pl.when(n == nc - 1)(finish)

    tok = pl.BlockSpec((CHUNK, hb * HEAD_DIM), lambda h, n: (n, h))
    row = pl.BlockSpec((hb, nc, CHUNK), lambda h, n: (h, 0, 0))
    any_spec = pl.BlockSpec(memory_space=pl.ANY)
    outs = (_sds((t, D_MODEL)), _sds((DN_HEADS, nc, HEAD_DIM, HEAD_DIM)), _sds((DN_HEADS, nc, CHUNK, CHUNK)),
            _sds((t, D_MODEL)), _sds((t, D_MODEL)))
    out_specs = (tok, pl.BlockSpec((hb, None, HEAD_DIM, HEAD_DIM), lambda h, n: (h, n, 0, 0)),
                 pl.BlockSpec((hb, None, CHUNK, CHUNK), lambda h, n: (h, n, 0, 0)), tok, tok)
    in_specs, args = [tok, tok, tok, row, row], (q, k, v, g_rows, b_rows)
    scratch = [pltpu.VMEM((hb, HEAD_DIM, HEAD_DIM), f32)]
    if fused:
        outs += (_sds((N_DEV,) + next_shard.shape, next_shard.dtype),)
        out_specs += (any_spec,)
        in_specs, args = in_specs + [any_spec], args + (next_shard,)
        scratch = scratch + list(_GATHER_SEMS)
    return _call(body, name, outs, (DN_HEADS // hb, nc), in_specs, out_specs, scratch=scratch)(*args)


def delta_bwd(name, q, k, v, g_rows, b_rows, s_all, t_all, u_all, w_all, do, chip_part=None):
    t = q.shape[0]
    nc = t // CHUNK
    c = CHUNK
    hb = DELTA_HB

    fused = chip_part is not None
    assert not fused or hb == DN_HEADS

    def body(*refs):
        if fused:
            (q_ref, k_ref, v_ref, g_ref, b_ref, s_ref, t_ref, u_ref, w_ref, do_ref, x_ref, dq_ref, dk_ref, dv_ref,
             dg_ref, db_ref, slot_ref, dstate, send_sems, recv_sems, local_sem) = refs
        else:
            (q_ref, k_ref, v_ref, g_ref, b_ref, s_ref, t_ref, u_ref, w_ref, do_ref, dq_ref, dk_ref, dv_ref, dg_ref,
             db_ref, dstate) = refs
        step = pl.program_id(1)
        n = nc - 1 - step
        if fused:
            start, finish = _chip_exchange_steps(x_ref, slot_ref, send_sems, recv_sems, local_sem)
            pl.when(step == 0)(start)

        @pl.when(step == 0)
        def _():
            dstate[...] = jnp.zeros_like(dstate)

        qv, kv, vv = _heads(q_ref, hb), _heads(k_ref, hb), _heads(v_ref, hb)
        L = _delta_local(qv, kv, vv, g_ref[:, pl.ds(n, 1), :], b_ref[:, pl.ds(n, 1), :],
                         solved=(t_ref[...], _heads(u_ref, hb), _heads(w_ref, hb)))
        eye, lower, strict = L["eye"], L["lower"], L["strict"]
        shp = (hb, c, c)
        s0 = s_ref[...]
        dov = _heads(do_ref, hb)
        ds = dstate[...]
        eg = jnp.exp(L["gl"])
        v_new = L["u"] - _bnn(L["w"], s0)
        d_k_dec = _bnt(v_new, ds)
        d_v_new = _bnn(L["k_dec"], ds) + _btn(L["amat"], dov)
        d_eg = jnp.sum(jnp.sum(ds * s0, axis=2, keepdims=True), axis=1, keepdims=True)
        d_q_dec = _bnt(dov, s0)
        d_a = _bnt(dov, v_new)
        d_w = -_bnt(d_v_new, s0)
        dstate[...] = ds * eg + _btn(L["q_dec"], dov) - _btn(L["w"], d_v_new)
        d_am = jnp.where(lower, d_a * L["decay"], 0.0)
        dq = _bnn(d_am, kv) + d_q_dec * L["egc"]
        dk = _btn(d_am, qv) + d_k_dec * L["kdf"]
        e_col = jnp.sum(d_k_dec * L["k_dec"], axis=2, keepdims=True)
        d_gc_col = jnp.sum(d_q_dec * L["q_dec"], axis=2, keepdims=True) - e_col
        d_gl = jnp.sum(e_col, axis=1, keepdims=True) + d_eg * eg
        tinv = L["tinv"]
        d_rhs = _btn(tinv, jnp.concatenate([d_v_new, d_w], axis=2), HIGH)
        d_rhs_u, d_rhs_w = d_rhs[:, :, 0:HEAD_DIM], d_rhs[:, :, HEAD_DIM:]
        d_l = -_bnt(d_rhs, jnp.concatenate([L["u"], L["w"]], axis=2), HIGH)
        _put_heads(dv_ref, d_rhs_u * L["b_col"])
        d_b_col = jnp.sum(d_rhs_u * vv, axis=2, keepdims=True)
        d_gc_col = d_gc_col + jnp.sum(d_rhs_w * L["rhs_w"], axis=2, keepdims=True)
        d_lm = jnp.where(strict, d_l * L["decay"], 0.0)
        d_kb = d_rhs_w * L["egc"] + _bnn(d_lm, kv)
        dk = dk + _btn(d_lm, L["kb"]) + d_kb * L["b_col"]
        d_b_col = d_b_col + jnp.sum(d_kb * kv, axis=2, keepdims=True)
        m = d_am * L["qk"] + d_lm * L["kk"]
        d_gc_col = d_gc_col + jnp.sum(m, axis=2, keepdims=True)
        d_gc_row = (jnp.sum(jnp.where(eye, jnp.broadcast_to(d_gc_col, shp), 0.0), axis=1, keepdims=True)
                    - jnp.sum(m, axis=1, keepdims=True))
        lane = lax.broadcasted_iota(jnp.int32, (1, 1, c), 2)
        d_gc_row = d_gc_row + jnp.where(lane == c - 1, d_gl, 0.0)
        d_gc_tot = jnp.sum(jnp.where(eye, jnp.broadcast_to(d_gc_row, shp), 0.0), axis=2, keepdims=True)
        dg_ref[:, pl.ds(n, 1), :] = jnp.sum(jnp.where(lower, jnp.broadcast_to(d_gc_tot, shp), 0.0), axis=1,
                                            keepdims=True)
        db_ref[:, pl.ds(n, 1), :] = jnp.sum(jnp.where(eye, jnp.broadcast_to(d_b_col, shp), 0.0), axis=1,
                                            keepdims=True)
        _put_heads(dq_ref, dq)
        _put_heads(dk_ref, dk)
        if fused:
            pl.when(step == nc - 1)(finish)

    tok = pl.BlockSpec((CHUNK, hb * HEAD_DIM), lambda h, s: (nc - 1 - s, h))
    row = pl.BlockSpec((hb, nc, CHUNK), lambda h, s: (h, 0, 0))
    any_spec = pl.BlockSpec(memory_space=pl.ANY)
    in_specs = [tok, tok, tok, row, row,
                pl.BlockSpec((hb, None, HEAD_DIM, HEAD_DIM), lambda h, s: (h, nc - 1 - s, 0, 0)),
                pl.BlockSpec((hb, None, CHUNK, CHUNK), lambda h, s: (h, nc - 1 - s, 0, 0)), tok, tok, tok]
    args = (q, k, v, g_rows, b_rows, s_all, t_all, u_all, w_all, do)
    outs = (_sds((t, D_MODEL)),) * 3 + (_sds((DN_HEADS, nc, CHUNK)),) * 2
    out_specs = (tok, tok, tok, row, row)
    scratch = [pltpu.VMEM((hb, HEAD_DIM, HEAD_DIM), f32)]
    if fused:
        in_specs, args = in_specs + [any_spec], args + (chip_part,)
        outs += (_sds(chip_part.shape, chip_part.dtype),)
        out_specs += (any_spec,)
        scratch = scratch + list(_CHIP_SEMS)
    return _call(body, name, outs, (DN_HEADS // hb, nc), in_specs, out_specs, scratch=scratch)(*args)


SEG = 8


def _perm_rows(a):
    t, c = a.shape
    return a.reshape(SEG, t // SEG, c).transpose(1, 0, 2).reshape(t, c)


def _unperm_rows(a):
    t, c = a.shape
    return a.reshape(t // SEG, SEG, c).transpose(1, 0, 2).reshape(t, c)


def _cmul(ar, ai, br, bi):
    return ar * br - ai * bi, ar * bi + ai * br


def _segment_init(er, ei, lr, li, seg_len, reverse):
    w = er.shape[1]
    sub = lax.broadcasted_iota(jnp.int32, (SEG, w), 0)

    def shift(x, k):
        if reverse:
            return jnp.where(sub < SEG - k, pltpu.roll(x, SEG - k, 0), 0.0)
        return jnp.where(sub >= k, pltpu.roll(x, k, 0), 0.0)

    pr, pi = lr, li
    for _ in range(seg_len.bit_length() - 1):
        pr, pi = _cmul(pr, pi, pr, pi)
    fr, fi = shift(er, 1), shift(ei, 1)
    for k in (1, 2, 4):
        sr, si = shift(fr, k), shift(fi, k)
        mr, mi = _cmul(pr, pi, sr, si)
        fr, fi = fr + mr, fi + mi
        pr, pi = _cmul(pr, pi, pr, pi)
    return fr, fi


def s5_fwd(name, u_perm, bbd, cbd, lam):
    t = u_perm.shape[0]
    tt = _tile(t, (1024, 512, 256, 128))
    nt, ng, w = t // tt, tt // SEG, SG_STATE
    seg_len = t // SEG
    assert seg_len & (seg_len - 1) == 0 and tt % SEG == 0

    def body(u_ref, b_ref, c_ref, lam_ref, y_ref, h_ref, x_scr, state):
        p, i = pl.program_id(1), pl.program_id(2)
        lr1, li1 = lam_ref[:, 0:w], lam_ref[:, w:2 * w]
        lr, li = jnp.broadcast_to(lr1, (SEG, w)), jnp.broadcast_to(li1, (SEG, w))
        x_scr[...] = _bdot(u_ref[...], b_ref[...], ((1,), (0,)))

        @pl.when(jnp.logical_and(p == 0, i == 0))
        def _():
            state[...] = jnp.zeros_like(state)

        @pl.when(jnp.logical_and(p == 1, i == 0))
        def _():
            sr, si = _segment_init(state[:, 0:w], state[:, w:2 * w], lr1, li1, seg_len, False)
            state[:, 0:w] = sr
            state[:, w:2 * w] = si

        def run(store):
            def step(g, st):
                row = pl.multiple_of(g * SEG, SEG)
                xg = x_scr[pl.ds(row, SEG), :]
                nr = lr * st[0] - li * st[1] + xg[:, 0:w]
                ni = lr * st[1] + li * st[0] + xg[:, w:2 * w]
                if store:
                    h_ref[pl.ds(row, SEG), 0:w] = nr
                    h_ref[pl.ds(row, SEG), w:2 * w] = ni
                return nr, ni

            fin = lax.fori_loop(0, ng, step, (state[:, 0:w], state[:, w:2 * w]))
            state[:, 0:w] = fin[0]
            state[:, w:2 * w] = fin[1]

        @pl.when(p == 0)
        def _():
            run(False)

        @pl.when(p == 1)
        def _():
            run(True)
            y_ref[...] = _bdot(h_ref[...], c_ref[...], ((1,), (0,)))

    return _call(body, name, (_sds((t, D_MODEL)), _sds((t, N_SUPER * 2 * w))), (N_SUPER, 2, nt),
                 [pl.BlockSpec((tt, 128), lambda s, p, i: (i, s)),
                  pl.BlockSpec((None, 128, 2 * w), lambda s, p, i: (s, 0, 0)),
                  pl.BlockSpec((None, 2 * w, 128), lambda s, p, i: (s, 0, 0)),
                  pl.BlockSpec((None, 1, 2 * w), lambda s, p, i: (s, 0, 0))],
                 (pl.BlockSpec((tt, 128), lambda s, p, i: (i * p, s)),
                  pl.BlockSpec((tt, 2 * w), lambda s, p, i: (i * p, s))),
                 scratch=[pltpu.VMEM((tt, 2 * w), f32), pltpu.VMEM((SEG, 2 * w), f32)])(u_perm, bbd, cbd, lam)


def s5_bwd(name, dy_perm, u_perm, h_perm, hprev0, bbd, cbd, lam):
    t = u_perm.shape[0]
    tt = _tile(t, (1024, 512, 256, 128))
    nt, ng, w = t // tt, tt // SEG, SG_STATE
    seg_len = t // SEG

    def body(dy_ref, u_ref, h_ref, hp_ref, hp0_ref, b_ref, c_ref, lam_ref, du_ref, db_ref, dc_ref, dl_ref,
             g_scr, state, dl_acc):
        p, i = pl.program_id(1), pl.program_id(2)
        first_tile = jnp.logical_or(p == 0, i == nt - 1)
        lr1, li1 = lam_ref[:, 0:w], -lam_ref[:, w:2 * w]
        lr, li = jnp.broadcast_to(lr1, (SEG, w)), jnp.broadcast_to(li1, (SEG, w))
        g_scr[...] = _bdot(dy_ref[...], c_ref[...], ((1,), (1,)))

        @pl.when(jnp.logical_and(p == 0, i == 0))
        def _():
            state[...] = jnp.zeros_like(state)

        @pl.when(jnp.logical_and(p == 1, i == 0))
        def _():
            sr, si = _segment_init(state[:, 0:w], state[:, w:2 * w], lr1, li1, seg_len, True)
            state[:, 0:w] = sr
            state[:, w:2 * w] = si
            dl_acc[...] = jnp.zeros_like(dl_acc)

        def adj(g, st):
            row = pl.multiple_of(g * SEG, SEG)
            gg = g_scr[pl.ds(row, SEG), :]
            nr = lr * st[0] - li * st[1] + gg[:, 0:w]
            ni = lr * st[1] + li * st[0] + gg[:, w:2 * w]
            return row, nr, ni

        @pl.when(p == 0)
        def _():
            def step(k, st):
                _, nr, ni = adj(ng - 1 - k, st)
                return nr, ni

            fin = lax.fori_loop(0, ng, step, (state[:, 0:w], state[:, w:2 * w]))
            state[:, 0:w] = fin[0]
            state[:, w:2 * w] = fin[1]

        @pl.when(p == 1)
        def _():
            above = jnp.where(first_tile, hp0_ref[...], hp_ref[...])

            def step(k, st):
                g = ng - 1 - k
                row, nr, ni = adj(g, st)
                g_scr[pl.ds(row, SEG), 0:w] = nr
                g_scr[pl.ds(row, SEG), w:2 * w] = ni
                prow = pl.multiple_of(jnp.maximum(g - 1, 0) * SEG, SEG)
                hp = jnp.where(g > 0, h_ref[pl.ds(prow, SEG), :], above)
                pr, pi = hp[:, 0:w], hp[:, w:2 * w]
                return nr, ni, st[2] + nr * pr + ni * pi, st[3] + ni * pr - nr * pi

            fin = lax.fori_loop(0, ng, step, (state[:, 0:w], state[:, w:2 * w], dl_acc[:, 0:w], dl_acc[:, w:2 * w]))
            state[:, 0:w] = fin[0]
            state[:, w:2 * w] = fin[1]
            dl_acc[:, 0:w] = fin[2]
            dl_acc[:, w:2 * w] = fin[3]
            a = g_scr[...]
            du_ref[...] = _bdot(a, b_ref[...], ((1,), (1,)))
            d_b = _bdot(u_ref[...], a, ((0,), (0,)))
            d_c = _bdot(dy_ref[...], h_ref[...], ((0,), (0,)))

            @pl.when(i == 0)
            def _():
                db_ref[...] = d_b
                dc_ref[...] = d_c

            @pl.when(i > 0)
            def _():
                db_ref[...] += d_b
                dc_ref[...] += d_c

            @pl.when(i == nt - 1)
            def _():
                dl_ref[...] = jnp.sum(dl_acc[...], axis=0, keepdims=True)

    tile = lambda s, p, i: (nt - 1 - i, s)
    tile1 = lambda s, p, i: (nt - 1 - i * p, s)
    above = lambda s, p, i: (jnp.maximum((nt - 1 - i * p) * (tt // SEG) - 1, 0), s)
    per_s = lambda s, p, i: (s, 0, 0)
    return _call(body, name, (_sds((t, D_MODEL)), _sds((N_SUPER, 128, 2 * w)), _sds((N_SUPER, 128, 2 * w)),
                              _sds((N_SUPER, 1, 2 * w))), (N_SUPER, 2, nt),
                 [pl.BlockSpec((tt, 128), tile), pl.BlockSpec((tt, 128), tile1), pl.BlockSpec((tt, 2 * w), tile1),
                  pl.BlockSpec((SEG, 2 * w), above), pl.BlockSpec((SEG, 2 * w), lambda s, p, i: (0, s)),
                  pl.BlockSpec((None, 128, 2 * w), per_s), pl.BlockSpec((None, 2 * w, 128), per_s),
                  pl.BlockSpec((None, 1, 2 * w), per_s)],
                 (pl.BlockSpec((tt, 128), tile1), pl.BlockSpec((None, 128, 2 * w), per_s),
                  pl.BlockSpec((None, 128, 2 * w), per_s), pl.BlockSpec((None, 1, 2 * w), per_s)),
                 scratch=[pltpu.VMEM((tt, 2 * w), f32), pltpu.VMEM((SEG, 2 * w), f32),
                          pltpu.VMEM((SEG, 2 * w), f32)])(dy_perm, u_perm, h_perm, h_perm, hprev0, bbd, cbd, lam)


def _s5_disc(a_re, a_im, log_dt, br, bi):
    dt = jnp.exp(log_dt)
    mag = jnp.exp(a_re * dt)
    lr, li = mag * jnp.cos(a_im * dt), mag * jnp.sin(a_im * dt)
    den = a_re * a_re + a_im * a_im
    fr = ((lr - 1.0) * a_re + li * a_im) / den
    fi = (li * a_re - (lr - 1.0) * a_im) / den
    return lr, li, fr * br - fi * bi, fr * bi + fi * br


def s5_disc_fwd(name, a_re, a_im, log_dt, br, bi):
    g, n = SSM_GROUPS, SSM_STATE

    def body(ar, ai, ld, brr, bir, lr, li, bbr, bbi):
        o = _s5_disc(ar[...], ai[...], ld[...], brr[...], bir[...])
        lr[...], li[...], bbr[...], bbi[...] = o

    return _call(body, name, (_sds((g, 1, n)), _sds((g, 1, n)), _sds((g, SSM_GROUP, n)), _sds((g, SSM_GROUP, n))))(
        a_re, a_im, log_dt, br, bi)


def s5_disc_bwd(name, a_re, a_im, log_dt, br, bi, d_lr, d_li, d_bbr, d_bbi):
    g, n = SSM_GROUPS, SSM_STATE

    def body(ar, ai, ld, brr, bir, c1, c2, c3, c4, o1, o2, o3, o4, o5):
        _, vjp = jax.vjp(_s5_disc, ar[...], ai[...], ld[...], brr[...], bir[...])
        o1[...], o2[...], o3[...], o4[...], o5[...] = vjp((c1[...], c2[...], c3[...], c4[...]))

    return _call(body, name, (_sds((g, 1, n)), _sds((g, 1, n)), _sds((g, 1, 1)), _sds((g, SSM_GROUP, n)),
                              _sds((g, SSM_GROUP, n))))(a_re, a_im, log_dt, br, bi, d_lr, d_li, d_bbr, d_bbi)


def gelu_fwd(name, ys_lin, proj, d_skip):
    t, d = ys_lin.shape
    tm = _rows(t, light=True)

    def body(y_ref, u_ref, d_ref, o_ref):
        o_ref[...] = _gelu(y_ref[...] + d_ref[...] * u_ref[...])

    return _call(body, name, _sds((t, d)), (t // tm,),
                 [_row_spec(tm, d), pl.BlockSpec((tm, d), lambda i: (i, 4)), _acc_spec(1, d)],
                 _row_spec(tm, d))(ys_lin, proj, d_skip)


def _head_norm(o, hn):
    outs, ns, rs = [], [], []
    for h in range(DN_HEADS):
        oh = o[:, h * 128:(h + 1) * 128]
        r = lax.rsqrt(jnp.mean(oh * oh, axis=-1, keepdims=True) + EPS)
        n = oh * r
        ns.append(n)
        rs.append(r)
        outs.append(n * hn)
    return outs, ns, rs


def merge_fwd(name, proj, o, yg, glu_lin, head_norm, b_glu):
    t = o.shape[0]
    tm = _rows(t)
    d = D_MODEL

    def body(za_ref, zb_ref, ra_ref, rb_ref, o_ref, yg_ref, gl_ref, hn_ref, bg_ref, m_ref):
        on, _, _ = _head_norm(o_ref[...], hn_ref[...])
        y_a = jnp.concatenate(on, axis=1) * _silu(za_ref[...])
        y_b = yg_ref[...] * _sigmoid(gl_ref[...] + bg_ref[...]) * _silu(zb_ref[...])
        m_ref[...] = (_sigmoid(ra_ref[...]) * y_a + _sigmoid(rb_ref[...]) * y_b).astype(bf16)

    pc = lambda c: pl.BlockSpec((tm, d), lambda i: (i, c))
    return _call(body, name, _sds((t, d), bf16), (t // tm,),
                 [pc(3), pc(5), pc(6), pc(7), _row_spec(tm, d), _row_spec(tm, d), _row_spec(tm, d),
                  _acc_spec(1, 128), _acc_spec(1, d)], _row_spec(tm, d))(
        proj, proj, proj, proj, o, yg, glu_lin, head_norm, b_glu)


def merge_bwd(name, proj, o, yg, glu_lin, head_norm, b_glu, dm):
    t = o.shape[0]
    tm = _rows(t)
    d = D_MODEL

    def body(za_ref, zb_ref, ra_ref, rb_ref, o_ref, yg_ref, gl_ref, hn_ref, bg_ref, dm_ref,
             dza_ref, dzb_ref, dra_ref, drb_ref, do_ref, dgl_ref, dyg_ref, dhn_ref, dbg_ref):
        hn = hn_ref[...]
        za, zb, ra, rb = za_ref[...], zb_ref[...], ra_ref[...], rb_ref[...]
        on, ns, rs = _head_norm(o_ref[...], hn)
        onc = jnp.concatenate(on, axis=1)
        sza = _silu(za)
        y_a = onc * sza
        yg = yg_ref[...]
        sgl = _sigmoid(gl_ref[...] + bg_ref[...])
        y2 = yg * sgl
        szb = _silu(zb)
        y_b = y2 * szb
        sra, srb = _sigmoid(ra), _sigmoid(rb)
        dmv = dm_ref[...]
        dra_ref[...] = (dmv * y_a * sra * (1.0 - sra)).astype(bf16)
        drb_ref[...] = (dmv * y_b * srb * (1.0 - srb)).astype(bf16)
        d_ya = dmv * sra
        d_yb = dmv * srb
        dza_ref[...] = (d_ya * onc * _dsilu(za)).astype(bf16)
        dzb_ref[...] = (d_yb * y2 * _dsilu(zb)).astype(bf16)
        d_on = d_ya * sza
        d_y2 = d_yb * szb
        dyg_ref[...] = d_y2 * sgl
        d_gl = d_y2 * yg * sgl * (1.0 - sgl)
        dgl_ref[...] = d_gl.astype(bf16)
        _accumulate(dbg_ref, jnp.sum(d_gl, axis=0, keepdims=True))
        d_hn = jnp.zeros((1, 128), f32)
        for h in range(DN_HEADS):
            hs = slice(h * 128, (h + 1) * 128)
            dh = d_on[:, hs]
            d_hn = d_hn + jnp.sum(dh * ns[h], axis=0, keepdims=True)
            dn = dh * hn
            do_ref[:, hs] = rs[h] * (dn - ns[h] * jnp.mean(dn * ns[h], axis=-1, keepdims=True))
        _accumulate(dhn_ref, d_hn)

    pc = lambda c: pl.BlockSpec((tm, d), lambda i: (i, c))
    rs_ = _row_spec(tm, d)
    return _call(body, name, (_sds((t, d), bf16),) * 4 + (_sds((t, d)), _sds((t, d), bf16), _sds((t, d)),
                              _sds((1, 128)), _sds((1, d))),
                 (t // tm,), [pc(3), pc(5), pc(6), pc(7), rs_, rs_, rs_, _acc_spec(1, 128), _acc_spec(1, d), rs_],
                 (rs_,) * 7 + (_acc_spec(1, 128), _acc_spec(1, d)))(
        proj, proj, proj, proj, o, yg, glu_lin, head_norm, b_glu, dm)


def gelu_bwd(name, ys_lin, proj, d_skip, dyg_a, dyg_b):
    t, d = ys_lin.shape
    tm = _rows(t, light=True)

    def body(y_ref, u_ref, d_ref, a_ref, b_ref, dys_ref, du_ref, dd_ref):
        uv = u_ref[...]
        dys = (a_ref[...] + b_ref[...]) * _dgelu(y_ref[...] + d_ref[...] * uv)
        dys_ref[...] = dys
        du_ref[...] = dys * d_ref[...]
        _accumulate(dd_ref, jnp.sum(dys * uv, axis=0, keepdims=True))

    rs_ = _row_spec(tm, d)
    return _call(body, name, (_sds((t, d)), _sds((t, d)), _sds((1, d))), (t // tm,),
                 [rs_, pl.BlockSpec((tm, d), lambda i: (i, 4)), _acc_spec(1, d), rs_, rs_],
                 (rs_, rs_, _acc_spec(1, d)))(ys_lin, proj, d_skip, dyg_a, dyg_b)


def assemble_dproj(name, dc, conv_w, dza, du_a, du_b, dzb, dra, drb, dbd):
    t = dza.shape[0]
    tm = _rows(t)
    nt = t // tm
    d = D_MODEL
    w3 = 3 * d

    def body(cur_ref, nxt_ref, w_ref, za_ref, ua_ref, ub_ref, zb_ref, ra_ref, rb_ref, bd_ref, o_ref, pad_ref):
        keep = (pl.program_id(0) < nt - 1).astype(f32)
        pad_ref[0:tm, :] = cur_ref[...]
        pad_ref[tm:tm + 8, :] = nxt_ref[...] * keep
        for cb in range(w3 // 128):
            cs = slice(cb * 128, (cb + 1) * 128)
            acc = pad_ref[pl.ds(CONV_K - 1, tm), cs] * w_ref[0:1, cs]
            for j in range(1, CONV_K):
                acc = acc + pad_ref[pl.ds(CONV_K - 1 - j, tm), cs] * w_ref[j:j + 1, cs]
            o_ref[:, cs] = acc.astype(bf16)
        o_ref[:, 3 * d:4 * d] = za_ref[...].astype(bf16)
        o_ref[:, 4 * d:5 * d] = (ua_ref[...] + ub_ref[...]).astype(bf16)
        o_ref[:, 5 * d:6 * d] = zb_ref[...].astype(bf16)
        o_ref[:, 6 * d:7 * d] = ra_ref[...].astype(bf16)
        o_ref[:, 7 * d:8 * d] = rb_ref[...].astype(bf16)
        o_ref[:, 8 * d:8 * d + 128] = bd_ref[...].astype(bf16)
        o_ref[:, 8 * d + 128:W_PAD] = jnp.zeros((tm, W_PAD - 8 * d - 128), bf16)

    rs_ = _row_spec(tm, d)
    return _call(body, name, _sds((t, W_PAD), bf16), (nt,),
                 [_row_spec(tm, w3), _next_spec(tm, w3, t), _acc_spec(CONV_K, w3), rs_, rs_, rs_, rs_, rs_, rs_,
                  _row_spec(tm, 128)], _row_spec(tm, W_PAD),
                 scratch=[pltpu.VMEM((tm + 8, w3), f32)])(dc, dc, conv_w, dza, du_a, du_b, dzb, dra, drb, dbd)


def adamw(name, w, g, m, v):
    lead, (r, c) = w.shape[:-2], w.shape[-2:]
    tm = _tile(r, (512, 256, 128, 64, 32, 16, 8))
    c1 = 1.0 / (1.0 - ADAM_B1 ** ADAM_STEP)
    c2 = 1.0 / (1.0 - ADAM_B2 ** ADAM_STEP)

    def body(w_ref, g_ref, m_ref, v_ref, d_ref, nm_ref, nv_ref):
        gv = g_ref[...]
        nm = ADAM_B1 * m_ref[...] + (1.0 - ADAM_B1) * gv
        nv = ADAM_B2 * v_ref[...] + (1.0 - ADAM_B2) * (gv * gv)
        d_ref[...] = -ADAM_LR * ((nm * c1) / (jnp.sqrt(nv * c2) + ADAM_EPS) + ADAM_WD * w_ref[...])
        nm_ref[...] = nm
        nv_ref[...] = nv

    if lead:
        sp = pl.BlockSpec((None, tm, c), lambda l, i: (l, i, 0))
        grid = (lead[0], r // tm)
    else:
        sp = pl.BlockSpec((tm, c), lambda i: (i, 0))
        grid = (r // tm,)
    return _call(body, name, (_sds(w.shape),) * 3, grid, [sp] * 4, (sp,) * 3)(w, g, m, v)


def _coords():
    return lax.axis_index("x"), lax.axis_index("y"), lax.axis_index("c")


def _lin(dev):
    return 4 * dev[0] + 2 * dev[1] + dev[2]


def _chips(me):
    x, y, _ = me
    return [(1 - x, y), (x, 1 - y), (1 - x, 1 - y)]


def _gather_steps(x_ref, o_ref, send_sems, recv_sems, local_sem):
    me = _coords()
    x, y, cc = me
    sibling = (x, y, 1 - cc)
    chips = _chips(me)

    def copy(k, block, to, src=None):
        return pltpu.make_async_remote_copy(
            src_ref=o_ref.at[_lin(block)] if src is None else src, dst_ref=o_ref.at[_lin(block)],
            send_sem=send_sems.at[k], recv_sem=recv_sems.at[k], device_id=to, device_id_type=MESH)

    mine = pltpu.make_async_copy(x_ref, o_ref.at[_lin(me)], local_sem)
    first = [copy(0, me, sibling, src=x_ref)] + [copy(1 + j, me, (*chip, cc), src=x_ref)
                                                 for j, chip in enumerate(chips)]
    passed = [copy(4 + j, (*chip, cc), sibling) for j, chip in enumerate(chips)]

    def start():
        mine.start()
        for cp in first:
            cp.start()

    def forward():
        for j, chip in enumerate(chips):
            copy(1 + j, (*chip, cc), me).wait_recv()
            passed[j].start()

    def finish():
        copy(0, sibling, me).wait_recv()
        for j, chip in enumerate(chips):
            copy(4 + j, (*chip, 1 - cc), me).wait_recv()
        for cp in first + passed:
            cp.wait_send()
        mine.wait()

    return start, forward, finish


_GATHER_SEMS = [pltpu.SemaphoreType.DMA((N_DEV - 1,)), pltpu.SemaphoreType.DMA((N_DEV - 1,)),
                pltpu.SemaphoreType.DMA(())]


def all_gather(name, shard):
    r, c = shard.shape

    def body(x_ref, o_ref, send_sems, recv_sems, local_sem):
        for step in _gather_steps(x_ref, o_ref, send_sems, recv_sems, local_sem):
            step()

    any_spec = pl.BlockSpec(memory_space=pl.ANY)
    return _call(body, name, _sds((N_DEV, r, c), shard.dtype), in_specs=[any_spec], out_specs=any_spec,
                 scratch=list(_GATHER_SEMS))(shard)


def pair_exchange(name, blocks):
    _, _, r, c = blocks.shape

    def body(x_ref, o_ref, send_sems, recv_sems):
        x, y, cc = _coords()
        sibling = (x, y, 1 - cc)
        cps = [pltpu.make_async_remote_copy(src_ref=x_ref.at[ch, 1 - cc], dst_ref=o_ref.at[ch], send_sem=send_sems.at[ch],
                                            recv_sem=recv_sems.at[ch], device_id=sibling, device_id_type=MESH)
               for ch in range(4)]
        for cp in cps:
            cp.start()
        for cp in cps:
            cp.wait()

    any_spec = pl.BlockSpec(memory_space=pl.ANY)
    return _call(body, name, _sds((4, r, c), blocks.dtype), in_specs=[any_spec], out_specs=any_spec,
                 scratch=[pltpu.SemaphoreType.DMA((4,)), pltpu.SemaphoreType.DMA((4,))])(blocks)


def _chip_exchange_steps(x_ref, o_ref, send_sems, recv_sems, local_sem):
    me = _coords()
    x, y, cc = me
    my_chip = 2 * x + y
    mine = pltpu.make_async_copy(x_ref.at[my_chip], o_ref.at[my_chip], local_sem)
    peers = [(px, py, cc) for px, py in _chips(me)]
    sends = [pltpu.make_async_remote_copy(src_ref=x_ref.at[2 * px + py], dst_ref=o_ref.at[my_chip],
                                          send_sem=send_sems.at[j], recv_sem=recv_sems.at[j], device_id=(px, py, pc),
                                          device_id_type=MESH) for j, (px, py, pc) in enumerate(peers)]

    def start():
        mine.start()
        for cp in sends:
            cp.start()

    def finish():
        for j, (px, py, pc) in enumerate(peers):
            pltpu.make_async_remote_copy(src_ref=x_ref.at[my_chip], dst_ref=o_ref.at[2 * px + py],
                                         send_sem=send_sems.at[j], recv_sem=recv_sems.at[j], device_id=(px, py, pc),
                                         device_id_type=MESH).wait_recv()
        for cp in sends:
            cp.wait_send()
        mine.wait()

    return start, finish


_CHIP_SEMS = [pltpu.SemaphoreType.DMA((3,)), pltpu.SemaphoreType.DMA((3,)), pltpu.SemaphoreType.DMA(())]


def chip_exchange(name, blocks):
    def body(x_ref, o_ref, send_sems, recv_sems, local_sem):
        for step in _chip_exchange_steps(x_ref, o_ref, send_sems, recv_sems, local_sem):
            step()

    any_spec = pl.BlockSpec(memory_space=pl.ANY)
    return _call(body, name, _sds(blocks.shape, blocks.dtype), in_specs=[any_spec], out_specs=any_spec,
                 scratch=list(_CHIP_SEMS))(blocks)


def pair_sum(name, mine, theirs):
    _, r, c = mine.shape
    tm = max(d for d in range(16, 769, 16) if r % d == 0)

    def body(a_ref, b_ref, o_ref):
        o_ref[...] = (a_ref[...].astype(f32) + b_ref[...].astype(f32)).astype(o_ref.dtype)

    sp = pl.BlockSpec((None, tm, c), lambda ch, i: (ch, i, 0))
    return _call(body, name, _sds(mine.shape, mine.dtype), (4, r // tm), [sp, sp], sp)(mine, theirs)


def sum_slots(name, slots):
    n, r, c = slots.shape
    tm = max(d for d in range(16, 769, 16) if r % d == 0)

    def body(s_ref, o_ref):
        acc = s_ref[0].astype(f32)
        for d in range(1, n):
            acc = acc + s_ref[d].astype(f32)
        o_ref[...] = acc

    return _call(body, name, _sds((r, c)), (r // tm,), [pl.BlockSpec((n, tm, c), lambda i: (0, i, 0))],
                 pl.BlockSpec((tm, c), lambda i: (i, 0)))(slots)


def _rows_layout(col8, t):
    return col8.T.reshape(DN_HEADS, t // CHUNK, CHUNK)


def _blockdiag(m):
    g, a, b = m.shape
    m = m.reshape(N_SUPER, SUPER, a, b)
    out = jnp.einsum("sgab,gh->sgahb", m, jnp.eye(SUPER, dtype=m.dtype))
    return out.reshape(N_SUPER, SUPER * a, SUPER * b)


def _diag_blocks(m, a, b):
    m = m.reshape(N_SUPER, SUPER, a, SUPER, b)
    return jnp.einsum("sgahb,gh->sgab", m, jnp.eye(SUPER, dtype=m.dtype)).reshape(SSM_GROUPS, a, b)


def _s5_params(p, li):
    tag = f"l{li}"
    n = SSM_STATE
    a_re = p["ssm_a_re"].reshape(SSM_GROUPS, 1, n)
    a_im = p["ssm_a_im"].reshape(SSM_GROUPS, 1, n)
    log_dt = p["ssm_log_dt"].reshape(SSM_GROUPS, 1, 1)
    br = jnp.swapaxes(p["ssm_b_re"], 1, 2)
    bi = jnp.swapaxes(p["ssm_b_im"], 1, 2)
    lr, li_, bbr, bbi = s5_disc_fwd("s5_disc_" + tag, a_re, a_im, log_dt, br, bi)
    lam = jnp.concatenate([lr.reshape(N_SUPER, 1, SG_STATE), li_.reshape(N_SUPER, 1, SG_STATE)], axis=-1)
    bbd = jnp.concatenate([_blockdiag(bbr), _blockdiag(bbi)], axis=-1).astype(bf16)
    c_re = jnp.swapaxes(p["ssm_c_re"], 1, 2)
    c_im = jnp.swapaxes(p["ssm_c_im"], 1, 2)
    cbd = jnp.concatenate([_blockdiag(c_re), -_blockdiag(c_im)], axis=1).astype(bf16)
    return dict(a_re=a_re, a_im=a_im, log_dt=log_dt, br=br, bi=bi, lam=lam, bbd=bbd, cbd=cbd)


def layer_fwd(x, p, li, next_shard=None):
    tag = f"l{li}"
    t = x.shape[0]
    d = D_MODEL
    h = rmsnorm_fwd("norm_pre_" + tag, x, p["norm_pre"])
    proj = mm_nn("proj_" + tag, h, p["w_all"])
    q, k, v, bg = conv_qkv_fwd("conv_" + tag, proj, p["conv_w"], p["a_log"], p["dt_bias"])
    b_rows = _rows_layout(bg[:, 0:DN_HEADS], t)
    g_rows = _rows_layout(bg[:, DN_HEADS:2 * DN_HEADS], t)
    o, s_all, t_all, u_all, w_sol, *gathered = delta_fwd("delta_" + tag, q, k, v, g_rows, b_rows, next_shard)
    sp = _s5_params(p, li)
    u_perm = _perm_rows(proj[:, 4 * d:5 * d])
    ys_perm, hs = s5_fwd("s5_" + tag, u_perm, sp["bbd"], sp["cbd"], sp["lam"])
    ys_lin = _unperm_rows(ys_perm)
    yg = gelu_fwd("gelu_" + tag, ys_lin, proj, p["ssm_d"])
    glu_lin = mm_nn("glu_" + tag, yg, p["w_glu"])
    merged = merge_fwd("merge_" + tag, proj, o, yg, glu_lin, p["head_norm"], p["b_glu"])
    out = mm_nn("out_" + tag, merged, p["w_out"])
    y = residual_norm_fwd("norm_post_" + tag, x, out, p["norm_post"])
    saved = dict(x=x, h=h, proj=proj, q=q, k=k, v=v, g_rows=g_rows, b_rows=b_rows, o=o, s_all=s_all, t_all=t_all, u_all=u_all, w_sol=w_sol, sp=sp, u_perm=u_perm,
                 hs=hs, ys_lin=ys_lin, yg=yg, glu_lin=glu_lin, merged=merged, out=out)
    return y, saved, (gathered[0] if gathered else None)


def layer_bwd(dy, p, s, li, chip_part=None):
    tag = f"l{li}"
    t = dy.shape[0]
    d = D_MODEL
    sp = s["sp"]
    gr = {}
    d_out, gr["norm_post"] = post_norm_bwd("norm_post_b_" + tag, s["out"], p["norm_post"], dy)
    d_merged = mm_nt("out_b_" + tag, d_out, p["w_out"])
    gr["w_out"] = mm_tn("out_w_" + tag, s["merged"], d_out, bf16)
    (dza, dzb, dra, drb, d_o, d_glu, dyg_a, gr["head_norm"], gr["b_glu"]) = merge_bwd(
        "merge_b_" + tag, s["proj"], s["o"], s["yg"], s["glu_lin"], p["head_norm"], p["b_glu"], d_merged)
    dyg_b = mm_nt("glu_b_" + tag, d_glu, p["w_glu"])
    gr["w_glu"] = mm_tn("glu_w_" + tag, s["yg"], d_glu, bf16)
    d_ys, du_a, gr["ssm_d"] = gelu_bwd("gelu_b_" + tag, s["ys_lin"], s["proj"], p["ssm_d"], dyg_a, dyg_b)
    hprev0 = jnp.concatenate([jnp.zeros((1, s["hs"].shape[1]), f32), s["hs"][-SEG:-1]], axis=0)
    du_perm, d_bbd, d_cbd, d_lam = s5_bwd("s5_b_" + tag, _perm_rows(d_ys), s["u_perm"], s["hs"], hprev0, sp["bbd"],
                                          sp["cbd"], sp["lam"])
    du_b = _unperm_rows(du_perm)
    gr["ssm_c_re"] = _diag_blocks(d_cbd[:, :, 0:SG_STATE], SSM_GROUP, SSM_STATE)
    gr["ssm_c_im"] = -_diag_blocks(d_cbd[:, :, SG_STATE:], SSM_GROUP, SSM_STATE)
    d_bbr = _diag_blocks(d_bbd[:, :, 0:SG_STATE], SSM_GROUP, SSM_STATE)
    d_bbi = _diag_blocks(d_bbd[:, :, SG_STATE:], SSM_GROUP, SSM_STATE)
    d_lr = d_lam[:, :, 0:SG_STATE].reshape(SSM_GROUPS, 1, SSM_STATE)
    d_li = d_lam[:, :, SG_STATE:].reshape(SSM_GROUPS, 1, SSM_STATE)
    d_are, d_aim, d_ldt, d_br, d_bi = s5_disc_bwd("s5_disc_b_" + tag, sp["a_re"], sp["a_im"], sp["log_dt"], sp["br"],
                                                  sp["bi"], d_lr, d_li, d_bbr, d_bbi)
    gr["ssm_a_re"] = d_are.reshape(SSM_GROUPS, SSM_STATE)
    gr["ssm_a_im"] = d_aim.reshape(SSM_GROUPS, SSM_STATE)
    gr["ssm_log_dt"] = d_ldt.reshape(SSM_GROUPS)
    gr["ssm_b_re"] = jnp.swapaxes(d_br, 1, 2)
    gr["ssm_b_im"] = jnp.swapaxes(d_bi, 1, 2)
    dq, dk, dv, dg_rows, db_rows, *slots = delta_bwd("delta_b_" + tag, s["q"], s["k"], s["v"], s["g_rows"],
                                                     s["b_rows"], s["s_all"], s["t_all"], s["u_all"], s["w_sol"], d_o,
                                                     chip_part)
    dbg = jnp.concatenate([db_rows.reshape(DN_HEADS, t).T, dg_rows.reshape(DN_HEADS, t).T,
                           jnp.zeros((t, 128 - 2 * DN_HEADS), f32)], axis=1)
    dc, gr["conv_w"], dbd, dab = conv_qkv_bwd("conv_b_" + tag, s["proj"], p["conv_w"], p["a_log"], p["dt_bias"],
                                              dq, dk, dv, dbg)
    gr["a_log"] = dab[0, DN_HEADS:2 * DN_HEADS]
    gr["dt_bias"] = dab[1, DN_HEADS:2 * DN_HEADS]
    dproj = assemble_dproj("dproj_" + tag, dc, p["conv_w"], dza, du_a, du_b, dzb, dra, drb, dbd)
    d_h = mm_nt("proj_b_" + tag, dproj, p["w_all"])
    gr["w_all"] = mm_tn("proj_w_" + tag, s["h"], dproj, bf16)
    dx, gr["norm_pre"] = rmsnorm_bwd("norm_pre_b_" + tag, s["x"], p["norm_pre"], d_h, dy)
    return dx, gr, (slots[0] if slots else None)


REPL = ["norm_pre", "a_log", "dt_bias", "head_norm", "ssm_a_re", "ssm_a_im", "ssm_log_dt", "ssm_b_re", "ssm_b_im",
        "ssm_c_re", "ssm_c_im", "ssm_d", "b_glu", "norm_post"]
SHARDED = ["w_in", "conv_w", "w_glu", "w_out"]
ALL_W = ["norm_pre", "w_in", "conv_w", "a_log", "dt_bias", "head_norm", "ssm_a_re", "ssm_a_im", "ssm_log_dt",
         "ssm_b_re", "ssm_b_im", "ssm_c_re", "ssm_c_im", "ssm_d", "w_glu", "b_glu", "w_out", "norm_post"]
PACK_W = 1024


def _pack_flat(arrs, rows):
    flat = jnp.concatenate([a.reshape(-1) for a in arrs])
    return jnp.pad(flat, (0, rows * PACK_W - flat.shape[0])).reshape(rows, PACK_W)


def _flat_rows(arrs, mult=8):
    n = sum(math.prod(a.shape) for a in arrs)
    rows = -(-n // PACK_W)
    return -(-rows // mult) * mult


def _unpack(flat, shapes):
    out, off = [], 0
    for sh in shapes:
        n = math.prod(sh)
        out.append(flat[off:off + n].reshape(sh))
        off += n
    return out


def _repl_split(shapes):
    big = [n for n in REPL if math.prod(shapes[n]) % PACK_W == 0]
    small = [n for n in REPL if n not in big]
    return big, small


def _rows8(n):
    return -(-n // (8 * PACK_W)) * 8


def _repl_rows(shapes):
    big, small = _repl_split(shapes)
    rows = sum(_rows8(math.prod(shapes[n])) for n in big) + _flat_rows([_sds(shapes[n]) for n in small], 8)
    return -(-rows // (16 * N_DEV)) * (16 * N_DEV)


def _repl_pack(arrs, shapes):
    big, small = _repl_split(shapes)
    parts = []
    for n in big:
        a = arrs[n].reshape(-1, PACK_W)
        parts.append(jnp.pad(a, ((0, _rows8(a.size) - a.shape[0]), (0, 0))))
    parts.append(_pack_flat([arrs[n] for n in small], _flat_rows([_sds(shapes[n]) for n in small], 8)))
    used = sum(p.shape[0] for p in parts)
    parts.append(jnp.zeros((_repl_rows(shapes) - used, PACK_W), parts[0].dtype))
    return jnp.concatenate(parts, axis=0)


def _repl_unpack(packed, shapes):
    big, small = _repl_split(shapes)
    out, off = {}, 0
    for n in big:
        size = math.prod(shapes[n])
        out[n] = packed[off:off + size // PACK_W].reshape(shapes[n])
        off += _rows8(size)
    srows = _flat_rows([_sds(shapes[n]) for n in small], 8)
    out.update(zip(small, _unpack(packed[off:off + srows].reshape(-1), [shapes[n] for n in small])))
    return out


_COL_RUNS = ((0, 4096), (4112, W_COLS), (4096, 4112))


def _w_all(main, tails, wc):
    pieces = []
    for lo, hi in _COL_RUNS:
        for dv in range(N_DEV):
            a, b = max(lo, dv * wc) - dv * wc, min(hi, (dv + 1) * wc) - dv * wc
            if a < min(b, PACK_W):
                pieces.append(main[dv][:, :, a:min(b, PACK_W)])
            if b > max(a, PACK_W):
                pieces.append(tails[dv][:, :, max(a, PACK_W) - PACK_W:b - PACK_W])
    ll, rows = main[0].shape[0], main[0].shape[1]
    pieces.append(jnp.zeros((ll, rows, W_PAD - W_COLS), main[0].dtype))
    return jnp.concatenate(pieces, axis=2)


def _ref_cols(g, lo, hi):
    pieces, off = [], 0
    for a, b in _COL_RUNS:
        s, e = max(lo, a), min(hi, b)
        if s < e:
            pieces.append((s, g[..., off + s - a:off + e - a]))
        off += b - a
    pieces.sort(key=lambda t: t[0])
    return jnp.concatenate([p for _, p in pieces], axis=-1) if len(pieces) > 1 else pieces[0][1]


def kernel(x, norm_pre, w_in, conv_w, a_log, dt_bias, head_norm, ssm_a_re, ssm_a_im, ssm_log_dt, ssm_b_re, ssm_b_im, ssm_c_re, ssm_c_im, ssm_d, w_glu, b_glu, w_out, norm_post, loss_target, m_norm_pre, m_w_in, m_conv_w, m_a_log, m_dt_bias, m_head_norm, m_ssm_a_re, m_ssm_a_im, m_ssm_log_dt, m_ssm_b_re, m_ssm_b_im, m_ssm_c_re, m_ssm_c_im, m_ssm_d, m_w_glu, m_b_glu, m_w_out, m_norm_post, v_norm_pre, v_w_in, v_conv_w, v_a_log, v_dt_bias, v_head_norm, v_ssm_a_re, v_ssm_a_im, v_ssm_log_dt, v_ssm_b_re, v_ssm_b_im, v_ssm_c_re, v_ssm_c_im, v_ssm_d, v_w_glu, v_b_glu, v_w_out, v_norm_post):
    loc = dict(locals())
    w = {n: loc[n] for n in ALL_W}
    m = {n: loc["m_" + n] for n in ALL_W}
    v = {n: loc["v_" + n] for n in ALL_W}
    depth = w_in.shape[0]
    wc = w_in.shape[2]
    cc = conv_w.shape[2]
    wr = w_glu.shape[1]

    tail = wc - PACK_W
    conv_hi = conv_w.astype(bf16)
    conv_mid = (conv_w - conv_hi.astype(f32)).astype(bf16)
    conv_lo = (conv_w - conv_hi.astype(f32) - conv_mid.astype(f32)).astype(bf16)
    conv3 = jnp.stack([conv_hi, conv_mid, conv_lo], axis=1)
    w_in_b = w_in.astype(bf16)
    MISC = 16
    o_glu, o_out, o_tail, o_conv = D_MODEL, D_MODEL + wr, D_MODEL + 2 * wr, D_MODEL + 2 * wr + MISC

    def rows16(a):
        return jnp.pad(a, ((0, MISC - a.shape[0]), (0, PACK_W - a.shape[1])))

    def layer_shard(li):
        return jnp.concatenate([w_in_b[li, :, :PACK_W], w_glu[li].astype(bf16), w_out[li].astype(bf16),
                                rows16(w_in_b[li, :, PACK_W:].T), rows16(conv3[li].reshape(3 * CONV_K, cc))])

    def layer_weights(gathered):
        tails = gathered[:, o_tail:o_tail + tail].reshape(N_DEV * tail, PACK_W).T
        w_all = _w_all([gathered[dv, :D_MODEL][None] for dv in range(N_DEV)],
                       [tails[None, :, dv * tail:(dv + 1) * tail] for dv in range(N_DEV)], wc)[0]
        glu = gathered[:, o_glu:o_glu + wr].reshape(N_DEV * wr, D_MODEL)
        out = gathered[:, o_out:o_out + wr].reshape(N_DEV * wr, D_MODEL)
        conv = gathered[:, o_conv:o_conv + 3 * CONV_K, :cc].astype(f32).reshape(N_DEV, 3, CONV_K, cc)
        conv = (conv[:, 0] + conv[:, 1] + conv[:, 2]).transpose(1, 0, 2).reshape(CONV_K, N_DEV * cc)
        return dict(w_all=w_all, w_glu=glu, w_out=out, conv_w=conv)

    def layer_params(li, gathered):
        return dict(layer_weights(gathered), norm_pre=norm_pre[li].reshape(1, -1),
                    a_log=jnp.pad(a_log[li], (DN_HEADS, 128 - 2 * DN_HEADS)).reshape(1, 128),
                    dt_bias=jnp.pad(dt_bias[li], (DN_HEADS, 128 - 2 * DN_HEADS)).reshape(1, 128),
                    head_norm=head_norm[li].reshape(1, -1), ssm_a_re=ssm_a_re[li], ssm_a_im=ssm_a_im[li],
                    ssm_log_dt=ssm_log_dt[li], ssm_b_re=ssm_b_re[li], ssm_b_im=ssm_b_im[li], ssm_c_re=ssm_c_re[li],
                    ssm_c_im=ssm_c_im[li], ssm_d=ssm_d[li].reshape(1, -1),
                    b_glu=b_glu[li].reshape(1, -1), norm_post=norm_post[li].reshape(1, -1))

    act = x[0]
    saved, params = [], []
    gathered = all_gather("gather_weights", layer_shard(0))
    for li in range(depth):
        params.append(layer_params(li, gathered))
        act, sv, gathered = layer_fwd(act, params[li], li, layer_shard(li + 1) if li + 1 < depth else None)
        saved.append(sv)
    loss_part, dy = loss_head("loss_head", act, loss_target[0])
    repl_shapes = {n: w[n].shape for n in REPL}
    repl_rows = _repl_rows(repl_shapes)
    rr = repl_rows // N_DEV
    my_c = lax.axis_index("c")

    def chip_sums(li, gl, extra=None):
        g_conv = gl["conv_w"].astype(bf16)
        tails = jnp.concatenate([_ref_cols(gl["w_all"], dv * wc + PACK_W, (dv + 1) * wc) for dv in range(N_DEV)],
                                axis=1).T
        blocks = jnp.stack([jnp.concatenate(
            [_ref_cols(gl["w_all"], dv * wc, dv * wc + PACK_W), gl["w_glu"][dv * wr:(dv + 1) * wr],
             gl["w_out"][dv * wr:(dv + 1) * wr], rows16(tails[dv * tail:(dv + 1) * tail]),
             rows16(g_conv[:, dv * cc:(dv + 1) * cc])]
            + ([extra[dv * rr:(dv + 1) * rr]] if extra is not None else []))
            for dv in range(N_DEV)])
        blocks = blocks.reshape(4, 2, blocks.shape[1], PACK_W)
        from_sibling = pair_exchange(f"pair_grads_l{li}", blocks)
        own = lax.dynamic_index_in_dim(blocks, my_c, axis=1, keepdims=False)
        return pair_sum(f"pair_sum_grads_l{li}", own, from_sibling)

    grads, slots, pending = [None] * depth, [None] * depth, None
    for li in reversed(range(depth)):
        dy, grads[li], got = layer_bwd(dy, params[li], saved[li], li, pending)
        if pending is not None:
            slots[li + 1] = got
        g_repl = None
        if li == 0:
            g_repl = _repl_pack({n: jnp.stack([grads[l][n] for l in range(depth)]).reshape(w[n].shape) for n in REPL},
                                repl_shapes).astype(bf16)
        pending = chip_sums(li, grads[li], g_repl)
    slots[0] = chip_exchange("scatter_grads", pending)
    grad_x = dy[None]
    loss = lax.psum(loss_part[0, 0], ("x", "y", "c"))
    mine = [sum_slots(f"sum_grads_l{li}", slots[li]) for li in range(depth)]
    o_repl = o_conv + MISC
    gs_w_in = jnp.stack([jnp.concatenate([mi[:o_glu], mi[o_tail:o_tail + tail].T], axis=1) for mi in mine])
    gs_glu = jnp.stack([mi[o_glu:o_out] for mi in mine])
    gs_out = jnp.stack([mi[o_out:o_tail] for mi in mine])
    gs_conv = jnp.stack([mi[o_conv:o_conv + CONV_K, :cc] for mi in mine])
    g_repl_full = all_gather("gather_repl_grads", mine[0][o_repl:o_repl + rr]).reshape(repl_rows, PACK_W)
    g = _repl_unpack(g_repl_full, repl_shapes)
    g.update(w_in=gs_w_in, w_glu=gs_glu, w_out=gs_out, conv_w=gs_conv)

    delta, new_m, new_v = {}, {}, {}
    for n in SHARDED:
        delta[n], new_m[n], new_v[n] = adamw("adamw_" + n, w[n], g[n], m[n], v[n])
    outs = adamw("adamw_repl", *[_repl_pack({n: src[n] for n in REPL}, repl_shapes) for src in (w, g, m, v)])
    for dst, o in zip((delta, new_m, new_v), outs):
        dst.update(_repl_unpack(o, repl_shapes))
    return (loss, grad_x, *[g[n] for n in ALL_W], *[delta[n] for n in ALL_W], *[new_m[n] for n in ALL_W],
            *[new_v[n] for n in ALL_W])
```

```python
import math

import jax
import jax.numpy as jnp
from jax import lax
from jax.experimental import pallas as pl
from jax.experimental.pallas import tpu as pltpu

f32 = jnp.float32
bf16 = jnp.bfloat16

D_MODEL = 1024
N_DEV = 8
DN_HEADS = 8
HEAD_DIM = 128
CHUNK = 64
CONV_K = 4
SSM_GROUPS = 64
SSM_GROUP = 16
SSM_STATE = 64
SUPER = 8
N_SUPER = SSM_GROUPS // SUPER
SG_STATE = SUPER * SSM_STATE
EPS = 1e-6
W_COLS = 8208
W_PAD = 8448
ADAM_LR, ADAM_B1, ADAM_B2, ADAM_EPS, ADAM_WD, ADAM_STEP = 0.001, 0.9, 0.999, 1e-08, 0.01, 10
VMEM_LIMIT = 56 * 1024 * 1024
MESH = pl.DeviceIdType.MESH
HIGH = lax.Precision.HIGH
DELTA_HB = 8


def _call(body, name, out_shape, grid=None, in_specs=None, out_specs=None, scratch=(), **kw):
    args = dict(out_shape=out_shape, name=name, scratch_shapes=list(scratch),
                compiler_params=pltpu.CompilerParams(vmem_limit_bytes=VMEM_LIMIT, **kw))
    if grid is not None:
        args.update(grid=grid, in_specs=in_specs, out_specs=out_specs)
    else:
        if in_specs is not None:
            args.update(in_specs=in_specs)
        if out_specs is not None:
            args.update(out_specs=out_specs)
    return pl.pallas_call(body, **args)


def _sds(shape, dtype=f32):
    return jax.ShapeDtypeStruct(tuple(shape), dtype)


def _sigmoid(x):
    return 1.0 / (1.0 + jnp.exp(-x))


def _silu(x):
    return x * _sigmoid(x)


_GELU_C = math.sqrt(2.0 / math.pi)


def _gelu(x):
    return 0.5 * x * (1.0 + jnp.tanh(_GELU_C * (x + 0.044715 * x * x * x)))


def _dgelu(x):
    t = jnp.tanh(_GELU_C * (x + 0.044715 * x * x * x))
    return 0.5 * (1.0 + t) + 0.5 * x * (1.0 - t * t) * _GELU_C * (1.0 + 3 * 0.044715 * x * x)


def _softplus(x):
    return jnp.maximum(x, 0.0) + jnp.log(1.0 + jnp.exp(-jnp.abs(x)))


def _bdot(a, b, dn):
    return lax.dot_general(a.astype(bf16), b.astype(bf16), (dn, ((), ())), preferred_element_type=f32)


def _matmul(name, a, b, *, dn, grid, a_spec, b_spec, o_spec, o_shape, o_dtype=f32):
    nk = grid[-1]

    def body(a_ref, b_ref, o_ref, acc_ref):
        p = _bdot(a_ref[...], b_ref[...], dn)
        if nk == 1:
            o_ref[...] = p.astype(o_dtype)
        else:
            k = pl.program_id(len(grid) - 1)

            @pl.when(k == 0)
            def _():
                acc_ref[...] = p

            @pl.when(k > 0)
            def _():
                acc_ref[...] += p

            @pl.when(k == nk - 1)
            def _():
                o_ref[...] = acc_ref[...].astype(o_dtype)

    blk = tuple(d for d in o_spec.block_shape if d is not None)
    return _call(body, name, _sds(o_shape, o_dtype), grid, [a_spec, b_spec], o_spec,
                 scratch=[pltpu.VMEM(blk if nk > 1 else (8, 128), f32)])(a, b)


def _tile(n, pref):
    for t in pref:
        if n % t == 0:
            return t
    return n


def mm_nn(name, a, b, o_dtype=f32):
    m, k = a.shape
    n = b.shape[1]
    tm, tn, tk = _tile(m, (1024, 512, 256)), _tile(n, (2816, 1024, 512)), _tile(k, (2816, 1024))
    return _matmul(name, a, b, dn=((1,), (0,)), grid=(m // tm, n // tn, k // tk),
                   a_spec=pl.BlockSpec((tm, tk), lambda i, j, l: (i, l)),
                   b_spec=pl.BlockSpec((tk, tn), lambda i, j, l: (l, j)),
                   o_spec=pl.BlockSpec((tm, tn), lambda i, j, l: (i, j)), o_shape=(m, n), o_dtype=o_dtype)


def mm_nt(name, a, b):
    m, k = a.shape
    n = b.shape[0]
    tm, tn, tk = _tile(m, (1024, 512, 256)), _tile(n, (1024, 512)), _tile(k, (2816, 1024))
    return _matmul(name, a, b, dn=((1,), (1,)), grid=(m // tm, n // tn, k // tk),
                   a_spec=pl.BlockSpec((tm, tk), lambda i, j, l: (i, l)),
                   b_spec=pl.BlockSpec((tn, tk), lambda i, j, l: (j, l)),
                   o_spec=pl.BlockSpec((tm, tn), lambda i, j, l: (i, j)), o_shape=(m, n))


def mm_tn(name, a, b, o_dtype=f32):
    k, m = a.shape
    n = b.shape[1]
    tm, tn, tk = _tile(m, (512,)), _tile(n, (2816, 1024, 512)), _tile(k, (512, 256))
    return _matmul(name, a, b, dn=((0,), (0,)), grid=(m // tm, n // tn, k // tk),
                   a_spec=pl.BlockSpec((tk, tm), lambda i, j, l: (l, i)),
                   b_spec=pl.BlockSpec((tk, tn), lambda i, j, l: (l, j)),
                   o_spec=pl.BlockSpec((tm, tn), lambda i, j, l: (i, j)), o_shape=(m, n), o_dtype=o_dtype)


def _rows(t, light=False):
    return _tile(t, (512, 256) if light else (256,))


def _row_spec(tm, w):
    return pl.BlockSpec((tm, w), lambda i: (i, 0))


def _acc_spec(r, w):
    return pl.BlockSpec((r, w), lambda i: (0, 0))


def _accumulate(ref, val):
    @pl.when(pl.program_id(0) == 0)
    def _():
        ref[...] = val

    @pl.when(pl.program_id(0) > 0)
    def _():
        ref[...] += val


def rmsnorm_fwd(name, x, gain):
    t, d = x.shape
    tm = _rows(t, light=True)

    def body(x_ref, g_ref, o_ref):
        xv = x_ref[...]
        r = lax.rsqrt(jnp.mean(xv * xv, axis=-1, keepdims=True) + EPS)
        o_ref[...] = (xv * r * g_ref[...]).astype(bf16)

    return _call(body, name, _sds((t, d), bf16), (t // tm,), [_row_spec(tm, d), _acc_spec(1, d)],
                 _row_spec(tm, d))(x, gain)


def rmsnorm_bwd(name, x, gain, dn, dres):
    t, d = x.shape
    tm = _rows(t, light=True)

    def body(x_ref, g_ref, dn_ref, dr_ref, dx_ref, dg_ref):
        xv = x_ref[...]
        r = lax.rsqrt(jnp.mean(xv * xv, axis=-1, keepdims=True) + EPS)
        n = xv * r
        dnv = dn_ref[...]
        _accumulate(dg_ref, jnp.sum(dnv * n, axis=0, keepdims=True))
        dng = dnv * g_ref[...]
        dx_ref[...] = dr_ref[...] + r * (dng - n * jnp.mean(dng * n, axis=-1, keepdims=True))

    return _call(body, name, (_sds((t, d)), _sds((1, d))), (t // tm,),
                 [_row_spec(tm, d), _acc_spec(1, d), _row_spec(tm, d), _row_spec(tm, d)],
                 (_row_spec(tm, d), _acc_spec(1, d)))(x, gain, dn, dres)


def residual_norm_fwd(name, x, out, gain):
    t, d = x.shape
    tm = _rows(t, light=True)

    def body(x_ref, o_ref, g_ref, y_ref):
        ov = o_ref[...]
        r = lax.rsqrt(jnp.mean(ov * ov, axis=-1, keepdims=True) + EPS)
        y_ref[...] = x_ref[...] + ov * r * g_ref[...]

    return _call(body, name, _sds((t, d)), (t // tm,), [_row_spec(tm, d), _row_spec(tm, d), _acc_spec(1, d)],
                 _row_spec(tm, d))(x, out, gain)


def post_norm_bwd(name, out, gain, dy):
    t, d = out.shape
    tm = _rows(t, light=True)

    def body(o_ref, g_ref, dy_ref, do_ref, dg_ref):
        ov = o_ref[...]
        r = lax.rsqrt(jnp.mean(ov * ov, axis=-1, keepdims=True) + EPS)
        n = ov * r
        dyv = dy_ref[...]
        _accumulate(dg_ref, jnp.sum(dyv * n, axis=0, keepdims=True))
        dng = dyv * g_ref[...]
        do_ref[...] = (r * (dng - n * jnp.mean(dng * n, axis=-1, keepdims=True))).astype(bf16)

    return _call(body, name, (_sds((t, d), bf16), _sds((1, d))), (t // tm,),
                 [_row_spec(tm, d), _acc_spec(1, d), _row_spec(tm, d)],
                 (_row_spec(tm, d), _acc_spec(1, d)))(out, gain, dy)


def loss_head(name, y, target):
    t, d = y.shape
    tm = _rows(t, light=True)

    def body(y_ref, t_ref, l_ref, dy_ref):
        e = y_ref[...] - t_ref[...]
        dy_ref[...] = e * (1.0 / d)
        s = jnp.sum(jnp.sum(e * e, axis=1, keepdims=True), axis=0, keepdims=True) * (0.5 / d)
        _accumulate(l_ref, s)

    return _call(body, name, (_sds((1, 1)), _sds((t, d))), (t // tm,),
                 [_row_spec(tm, d), _row_spec(tm, d)], (_acc_spec(1, 1), _row_spec(tm, d)))(y, target)


def _prev_spec(tm, w):
    return pl.BlockSpec((8, w), lambda i: (jnp.maximum(i * (tm // 8) - 1, 0), 0))


def _next_spec(tm, w, t):
    return pl.BlockSpec((8, w), lambda i: (jnp.minimum((i + 1) * (tm // 8), t // 8 - 1), 0))


def _fill_pad(pad_ref, prev_ref, cur_ref, tm):
    keep = (pl.program_id(0) > 0).astype(f32)
    pad_ref[0:8, :] = prev_ref[...] * keep
    pad_ref[8:8 + tm, :] = cur_ref[...]


def _conv_block(pad_ref, w_ref, cb, tm):
    cs = slice(cb * 128, (cb + 1) * 128)
    acc = pad_ref[pl.ds(8 - (CONV_K - 1), tm), cs] * w_ref[0:1, cs]
    for j in range(1, CONV_K):
        acc = acc + pad_ref[pl.ds(8 - (CONV_K - 1) + j, tm), cs] * w_ref[j:j + 1, cs]
    return acc


def conv_qkv_fwd(name, proj, conv_w, a_log, dt_bias):
    t = proj.shape[0]
    tm = _rows(t)
    scale = HEAD_DIM ** -0.5

    def body(cur_ref, prev_ref, w_ref, bd_ref, al_ref, db_ref, q_ref, k_ref, v_ref, bg_ref, pad_ref):
        _fill_pad(pad_ref, prev_ref, cur_ref, tm)
        for cb in range(3 * DN_HEADS):
            s = _silu(_conv_block(pad_ref, w_ref, cb, tm))
            hs = slice((cb % DN_HEADS) * 128, (cb % DN_HEADS + 1) * 128)
            if cb < DN_HEADS:
                q_ref[:, hs] = s * (lax.rsqrt(jnp.sum(s * s, axis=-1, keepdims=True) + EPS) * scale)
            elif cb < 2 * DN_HEADS:
                k_ref[:, hs] = s * lax.rsqrt(jnp.sum(s * s, axis=-1, keepdims=True) + EPS)
            else:
                v_ref[:, hs] = s
        bd = bd_ref[...]
        beta = _sigmoid(bd)
        g = -jnp.exp(al_ref[...]) * _softplus(bd + db_ref[...])
        lane = lax.broadcasted_iota(jnp.int32, bd.shape, 1)
        bg_ref[...] = jnp.where(lane < DN_HEADS, beta, jnp.where(lane < 2 * DN_HEADS, g, 0.0))

    w3 = 3 * D_MODEL
    return _call(body, name, (_sds((t, D_MODEL)),) * 3 + (_sds((t, 128)),), (t // tm,),
                 [pl.BlockSpec((tm, w3), lambda i: (i, 0)), _prev_spec(tm, w3), _acc_spec(CONV_K, w3),
                  pl.BlockSpec((tm, 128), lambda i: (i, 8192 // 128)), _acc_spec(1, 128), _acc_spec(1, 128)],
                 (_row_spec(tm, D_MODEL),) * 3 + (_row_spec(tm, 128),),
                 scratch=[pltpu.VMEM((tm + 8, w3), f32)])(proj, proj, conv_w, proj, a_log, dt_bias)


def conv_qkv_bwd(name, proj, conv_w, a_log, dt_bias, dq, dk, dv, dbg):
    t = proj.shape[0]
    tm = _rows(t)
    scale = HEAD_DIM ** -0.5

    def body(cur_ref, prev_ref, w_ref, bd_ref, al_ref, db_ref, dq_ref, dk_ref, dv_ref, dbg_ref,
             dc_ref, dw_ref, dbd_ref, dab_ref, pad_ref):
        _fill_pad(pad_ref, prev_ref, cur_ref, tm)

        @pl.when(pl.program_id(0) == 0)
        def _():
            dw_ref[...] = jnp.zeros_like(dw_ref)

        for cb in range(3 * DN_HEADS):
            cs = slice(cb * 128, (cb + 1) * 128)
            hs = slice((cb % DN_HEADS) * 128, (cb % DN_HEADS + 1) * 128)
            taps = [pad_ref[pl.ds(8 - (CONV_K - 1) + j, tm), cs] for j in range(CONV_K)]
            c = taps[0] * w_ref[0:1, cs]
            for j in range(1, CONV_K):
                c = c + taps[j] * w_ref[j:j + 1, cs]
            sg = _sigmoid(c)
            s = c * sg
            if cb < 2 * DN_HEADS:
                dn = (dq_ref[:, hs] * scale) if cb < DN_HEADS else dk_ref[:, hs]
                r = lax.rsqrt(jnp.sum(s * s, axis=-1, keepdims=True) + EPS)
                ds = r * dn - s * (r * r * r) * jnp.sum(dn * s, axis=-1, keepdims=True)
            else:
                ds = dv_ref[:, hs]
            dc = ds * (sg * (1.0 + c * (1.0 - sg)))
            dc_ref[:, cs] = dc
            for j in range(CONV_K):
                dw_ref[j:j + 1, cs] += jnp.sum(dc * taps[j], axis=0, keepdims=True)
        bd = bd_ref[...]
        dbg_v = dbg_ref[...]
        lane = lax.broadcasted_iota(jnp.int32, bd.shape, 1)
        sg = _sigmoid(bd)
        ea = jnp.exp(al_ref[...])
        z = bd + db_ref[...]
        sp = _softplus(z)
        is_b = lane < DN_HEADS
        is_g = jnp.logical_and(lane >= DN_HEADS, lane < 2 * DN_HEADS)
        d_z = jnp.where(is_g, dbg_v * (-ea) * _sigmoid(z), 0.0)
        dbd_ref[...] = jnp.where(is_b, dbg_v * sg * (1.0 - sg), d_z)
        d_al = jnp.sum(jnp.where(is_g, dbg_v * (-ea) * sp, 0.0), axis=0, keepdims=True)
        d_db = jnp.sum(d_z, axis=0, keepdims=True)
        _accumulate(dab_ref, jnp.concatenate([d_al, d_db] + [jnp.zeros_like(d_al)] * 6, axis=0))

    w3 = 3 * D_MODEL
    return _call(body, name, (_sds((t, w3)), _sds((CONV_K, w3)), _sds((t, 128)), _sds((8, 128))), (t // tm,),
                 [pl.BlockSpec((tm, w3), lambda i: (i, 0)), _prev_spec(tm, w3), _acc_spec(CONV_K, w3),
                  pl.BlockSpec((tm, 128), lambda i: (i, 8192 // 128)), _acc_spec(1, 128), _acc_spec(1, 128),
                  _row_spec(tm, D_MODEL), _row_spec(tm, D_MODEL), _row_spec(tm, D_MODEL), _row_spec(tm, 128)],
                 (_row_spec(tm, w3), _acc_spec(CONV_K, w3), _row_spec(tm, 128), _acc_spec(8, 128)),
                 scratch=[pltpu.VMEM((tm + 8, w3), f32)])(proj, proj, conv_w, proj, a_log, dt_bias, dq, dk, dv, dbg)


def _bdg(a, b, ca, cb, prec=None):
    if prec is None:
        a, b = a.astype(bf16), b.astype(bf16)
    return lax.dot_general(a, b, (((ca,), (cb,)), ((0,), (0,))), precision=prec, preferred_element_type=f32)


def _bnn(a, b, prec=None):
    return _bdg(a, b, 2, 1, prec)


def _bnt(a, b, prec=None):
    return _bdg(a, b, 2, 2, prec)


def _btn(a, b, prec=None):
    return _bdg(a, b, 1, 1, prec)


def _delta_local(q, k, v, g_row, b_row, solved=None):
    c = CHUNK
    ii = lax.broadcasted_iota(jnp.int32, (c, c), 0)
    jj = lax.broadcasted_iota(jnp.int32, (c, c), 1)
    eye, lower, strict = ii == jj, ii >= jj, ii > jj
    shp = (q.shape[0], c, c)
    g_b = jnp.broadcast_to(g_row, shp)
    gc_col = jnp.sum(jnp.where(lower, g_b, 0.0), axis=2, keepdims=True)
    gc_row = jnp.sum(jnp.where(eye, jnp.broadcast_to(gc_col, shp), 0.0), axis=1, keepdims=True)
    b_col = jnp.sum(jnp.where(eye, jnp.broadcast_to(b_row, shp), 0.0), axis=2, keepdims=True)
    gl = jnp.sum(g_row, axis=2, keepdims=True)
    decay = jnp.exp(jnp.where(lower, gc_col - gc_row, -1e30))
    kb = k * b_col
    kk = _bnt(kb, k)
    lmat = jnp.where(strict, kk * decay, 0.0)
    egc = jnp.exp(gc_col)
    rhs_w = kb * egc
    if solved is None:
        tinv = eye.astype(f32) - lmat
        pw = lmat
        for _ in range(5):
            pw = _bnn(pw, pw, HIGH)
            tinv = tinv + _bnn(tinv, pw, HIGH)
        uw = _bnn(tinv, jnp.concatenate([v * b_col, rhs_w], axis=2), HIGH)
        u, w = uw[:, :, 0:HEAD_DIM], uw[:, :, HEAD_DIM:]
    else:
        tinv, u, w = solved
    qk = _bnt(q, k)
    amat = jnp.where(lower, qk * decay, 0.0)
    q_dec = q * egc
    kdf = jnp.exp(gl - gc_col)
    k_dec = k * kdf
    return dict(eye=eye, lower=lower, strict=strict, gc_col=gc_col, b_col=b_col, gl=gl, decay=decay,
                kb=kb, kk=kk, tinv=tinv, egc=egc, rhs_w=rhs_w, u=u, w=w, qk=qk, amat=amat, q_dec=q_dec, kdf=kdf,
                k_dec=k_dec)


def _heads(ref, hb):
    return jnp.stack([ref[:, h * HEAD_DIM:(h + 1) * HEAD_DIM] for h in range(hb)])


def _put_heads(ref, val):
    for h in range(val.shape[0]):
        ref[:, h * HEAD_DIM:(h + 1) * HEAD_DIM] = val[h]


def delta_fwd(name, q, k, v, g_rows, b_rows, next_shard=None):
    t = q.shape[0]
    nc = t // CHUNK
    hb = DELTA_HB
    fused = next_shard is not None
    assert not fused or (hb == DN_HEADS and nc >= 2)

    def body(*refs):
        if fused:
            (q_ref, k_ref, v_ref, g_ref, b_ref, x_ref, o_ref, s_ref, t_ref, u_ref, w_ref, gath_ref, state, send_sems,
             recv_sems, local_sem) = refs
        else:
            q_ref, k_ref, v_ref, g_ref, b_ref, o_ref, s_ref, t_ref, u_ref, w_ref, state = refs
        n = pl.program_id(1)
        if fused:
            start, forward, finish = _gather_steps(x_ref, gath_ref, send_sems, recv_sems, local_sem)
            pl.when(n == 0)(start)

        @pl.when(n == 0)
        def _():
            state[...] = jnp.zeros_like(state)

        loc = _delta_local(_heads(q_ref, hb), _heads(k_ref, hb), _heads(v_ref, hb), g_ref[:, pl.ds(n, 1), :],
                           b_ref[:, pl.ds(n, 1), :])
        s0 = state[...]
        s_ref[...] = s0
        t_ref[...] = loc["tinv"]
        _put_heads(u_ref, loc["u"])
        _put_heads(w_ref, loc["w"])
        v_new = loc["u"] - _bnn(loc["w"], s0)
        _put_heads(o_ref, _bnn(loc["q_dec"], s0) + _bnn(loc["amat"], v_new))
        state[...] = s0 * jnp.exp(loc["gl"]) + _btn(loc["k_dec"], v_new)
        if fused:
            pl.when(n == (3 * nc) // 4)(forward)
            pl.when(n == nc - 1)(finish)

    tok = pl.BlockSpec((CHUNK, hb * HEAD_DIM), lambda h, n: (n, h))
    row = pl.BlockSpec((hb, nc, CHUNK), lambda h, n: (h, 0, 0))
    any_spec = pl.BlockSpec(memory_space=pl.ANY)
    outs = (_sds((t, D_MODEL)), _sds((DN_HEADS, nc, HEAD_DIM, HEAD_DIM)), _sds((DN_HEADS, nc, CHUNK, CHUNK)),
            _sds((t, D_MODEL)), _sds((t, D_MODEL)))
    out_specs = (tok, pl.BlockSpec((hb, None, HEAD_DIM, HEAD_DIM), lambda h, n: (h, n, 0, 0)),
                 pl.BlockSpec((hb, None, CHUNK, CHUNK), lambda h, n: (h, n, 0, 0)), tok, tok)
    in_specs, args = [tok, tok, tok, row, row], (q, k, v, g_rows, b_rows)
    scratch = [pltpu.VMEM((hb, HEAD_DIM, HEAD_DIM), f32)]
    if fused:
        outs += (_sds((N_DEV,) + next_shard.shape, next_shard.dtype),)
        out_specs += (any_spec,)
        in_specs, args = in_specs + [any_spec], args + (next_shard,)
        scratch = scratch + list(_GATHER_SEMS)
    return _call(body, name, outs, (DN_HEADS // hb, nc), in_specs, out_specs, scratch=scratch)(*args)


def delta_bwd(name, q, k, v, g_rows, b_rows, s_all, t_all, u_all, w_all, do, chip_part=None):
    t = q.shape[0]
    nc = t // CHUNK
    c = CHUNK
    hb = DELTA_HB

    fused = chip_part is not None
    assert not fused or hb == DN_HEADS

    def body(*refs):
        if fused:
            (q_ref, k_ref, v_ref, g_ref, b_ref, s_ref, t_ref, u_ref, w_ref, do_ref, x_ref, dq_ref, dk_ref, dv_ref,
             dg_ref, db_ref, slot_ref, dstate, send_sems, recv_sems, local_sem) = refs
        else:
            (q_ref, k_ref, v_ref, g_ref, b_ref, s_ref, t_ref, u_ref, w_ref, do_ref, dq_ref, dk_ref, dv_ref, dg_ref,
             db_ref, dstate) = refs
        step = pl.program_id(1)
        n = nc - 1 - step
        if fused:
            start, finish = _chip_exchange_steps(x_ref, slot_ref, send_sems, recv_sems, local_sem)
            pl.when(step == 0)(start)

        @pl.when(step == 0)
        def _():
            dstate[...] = jnp.zeros_like(dstate)

        qv, kv, vv = _heads(q_ref, hb), _heads(k_ref, hb), _heads(v_ref, hb)
        L = _delta_local(qv, kv, vv, g_ref[:, pl.ds(n, 1), :], b_ref[:, pl.ds(n, 1), :],
                         solved=(t_ref[...], _heads(u_ref, hb), _heads(w_ref, hb)))
        eye, lower, strict = L["eye"], L["lower"], L["strict"]
        shp = (hb, c, c)
        s0 = s_ref[...]
        dov = _heads(do_ref, hb)
        ds = dstate[...]
        eg = jnp.exp(L["gl"])
        v_new = L["u"] - _bnn(L["w"], s0)
        d_k_dec = _bnt(v_new, ds)
        d_v_new = _bnn(L["k_dec"], ds) + _btn(L["amat"], dov)
        d_eg = jnp.sum(jnp.sum(ds * s0, axis=2, keepdims=True), axis=1, keepdims=True)
        d_q_dec = _bnt(dov, s0)
        d_a = _bnt(dov, v_new)
        d_w = -_bnt(d_v_new, s0)
        dstate[...] = ds * eg + _btn(L["q_dec"], dov) - _btn(L["w"], d_v_new)
        d_am = jnp.where(lower, d_a * L["decay"], 0.0)
        dq = _bnn(d_am, kv) + d_q_dec * L["egc"]
        dk = _btn(d_am, qv) + d_k_dec * L["kdf"]
        e_col = jnp.sum(d_k_dec * L["k_dec"], axis=2, keepdims=True)
        d_gc_col = jnp.sum(d_q_dec * L["q_dec"], axis=2, keepdims=True) - e_col
        d_gl = jnp.sum(e_col, axis=1, keepdims=True) + d_eg * eg
        tinv = L["tinv"]
        d_rhs = _btn(tinv, jnp.concatenate([d_v_new, d_w], axis=2), HIGH)
        d_rhs_u, d_rhs_w = d_rhs[:, :, 0:HEAD_DIM], d_rhs[:, :, HEAD_DIM:]
        d_l = -_bnt(d_rhs, jnp.concatenate([L["u"], L["w"]], axis=2), HIGH)
        _put_heads(dv_ref, d_rhs_u * L["b_col"])
        d_b_col = jnp.sum(d_rhs_u * vv, axis=2, keepdims=True)
        d_gc_col = d_gc_col + jnp.sum(d_rhs_w * L["rhs_w"], axis=2, keepdims=True)
        d_lm = jnp.where(strict, d_l * L["decay"], 0.0)
        d_kb = d_rhs_w * L["egc"] + _bnn(d_lm, kv)
        dk = dk + _btn(d_lm, L["kb"]) + d_kb * L["b_col"]
        d_b_col = d_b_col + jnp.sum(d_kb * kv, axis=2, keepdims=True)
        m = d_am * L["qk"] + d_lm * L["kk"]
        d_gc_col = d_gc_col + jnp.sum(m, axis=2, keepdims=True)
        d_gc_row = (jnp.sum(jnp.where(eye, jnp.broadcast_to(d_gc_col, shp), 0.0), axis=1, keepdims=True)
                    - jnp.sum(m, axis=1, keepdims=True))
        lane = lax.broadcasted_iota(jnp.int32, (1, 1, c), 2)
        d_gc_row = d_gc_row + jnp.where(lane == c - 1, d_gl, 0.0)
        d_gc_tot = jnp.sum(jnp.where(eye, jnp.broadcast_to(d_gc_row, shp), 0.0), axis=2, keepdims=True)
        dg_ref[:, pl.ds(n, 1), :] = jnp.sum(jnp.where(lower, jnp.broadcast_to(d_gc_tot, shp), 0.0), axis=1,
                                            keepdims=True)
        db_ref[:, pl.ds(n, 1), :] = jnp.sum(jnp.where(eye, jnp.broadcast_to(d_b_col, shp), 0.0), axis=1,
                                            keepdims=True)
        _put_heads(dq_ref, dq)
        _put_heads(dk_ref, dk)
        if fused:
            pl.when(step == nc - 1)(finish)

    tok = pl.BlockSpec((CHUNK, hb * HEAD_DIM), lambda h, s: (nc - 1 - s, h))
    row = pl.BlockSpec((hb, nc, CHUNK), lambda h, s: (h, 0, 0))
    any_spec = pl.BlockSpec(memory_space=pl.ANY)
    in_specs = [tok, tok, tok, row, row,
                pl.BlockSpec((hb, None, HEAD_DIM, HEAD_DIM), lambda h, s: (h, nc - 1 - s, 0, 0)),
                pl.BlockSpec((hb, None, CHUNK, CHUNK), lambda h, s: (h, nc - 1 - s, 0, 0)), tok, tok, tok]
    args = (q, k, v, g_rows, b_rows, s_all, t_all, u_all, w_all, do)
    outs = (_sds((t, D_MODEL)),) * 3 + (_sds((DN_HEADS, nc, CHUNK)),) * 2
    out_specs = (tok, tok, tok, row, row)
    scratch = [pltpu.VMEM((hb, HEAD_DIM, HEAD_DIM), f32)]
    if fused:
        in_specs, args = in_specs + [any_spec], args + (chip_part,)
        outs += (_sds(chip_part.shape, chip_part.dtype),)
        out_specs += (any_spec,)
        scratch = scratch + list(_CHIP_SEMS)
    return _call(body, name, outs, (DN_HEADS // hb, nc), in_specs, out_specs, scratch=scratch)(*args)


SEG = 8


def _perm_rows(a):
    t, c = a.shape
    return a.reshape(SEG, t // SEG, c).transpose(1, 0, 2).reshape(t, c)


def _unperm_rows(a):
    t, c = a.shape
    return a.reshape(t // SEG, SEG, c).transpose(1, 0, 2).reshape(t, c)


def _cmul(ar, ai, br, bi):
    return ar * br - ai * bi, ar * bi + ai * br


def _segment_init(er, ei, lr, li, seg_len, reverse):
    w = er.shape[1]
    sub = lax.broadcasted_iota(jnp.int32, (SEG, w), 0)

    def shift(x, k):
        if reverse:
            return jnp.where(sub < SEG - k, pltpu.roll(x, SEG - k, 0), 0.0)
        return jnp.where(sub >= k, pltpu.roll(x, k, 0), 0.0)

    pr, pi = lr, li
    for _ in range(seg_len.bit_length() - 1):
        pr, pi = _cmul(pr, pi, pr, pi)
    fr, fi = shift(er, 1), shift(ei, 1)
    for k in (1, 2, 4):
        sr, si = shift(fr, k), shift(fi, k)
        mr, mi = _cmul(pr, pi, sr, si)
        fr, fi = fr + mr, fi + mi
        pr, pi = _cmul(pr, pi, pr, pi)
    return fr, fi


def s5_fwd(name, u_perm, bbd, cbd, lam):
    t = u_perm.shape[0]
    tt = _tile(t, (1024, 512, 256, 128))
    nt, ng, w = t // tt, tt // SEG, SG_STATE
    seg_len = t // SEG
    assert seg_len & (seg_len - 1) == 0 and tt % SEG == 0

    def body(u_ref, b_ref, c_ref, lam_ref, y_ref, h_ref, x_scr, state):
        p, i = pl.program_id(1), pl.program_id(2)
        lr1, li1 = lam_ref[:, 0:w], lam_ref[:, w:2 * w]
        lr, li = jnp.broadcast_to(lr1, (SEG, w)), jnp.broadcast_to(li1, (SEG, w))
        x_scr[...] = _bdot(u_ref[...], b_ref[...], ((1,), (0,)))

        @pl.when(jnp.logical_and(p == 0, i == 0))
        def _():
            state[...] = jnp.zeros_like(state)

        @pl.when(jnp.logical_and(p == 1, i == 0))
        def _():
            sr, si = _segment_init(state[:, 0:w], state[:, w:2 * w], lr1, li1, seg_len, False)
            state[:, 0:w] = sr
            state[:, w:2 * w] = si

        def run(store):
            def step(g, st):
                row = pl.multiple_of(g * SEG, SEG)
                xg = x_scr[pl.ds(row, SEG), :]
                nr = lr * st[0] - li * st[1] + xg[:, 0:w]
                ni = lr * st[1] + li * st[0] + xg[:, w:2 * w]
                if store:
                    h_ref[pl.ds(row, SEG), 0:w] = nr
                    h_ref[pl.ds(row, SEG), w:2 * w] = ni
                return nr, ni

            fin = lax.fori_loop(0, ng, step, (state[:, 0:w], state[:, w:2 * w]))
            state[:, 0:w] = fin[0]
            state[:, w:2 * w] = fin[1]

        @pl.when(p == 0)
        def _():
            run(False)

        @pl.when(p == 1)
        def _():
            run(True)
            y_ref[...] = _bdot(h_ref[...], c_ref[...], ((1,), (0,)))

    return _call(body, name, (_sds((t, D_MODEL)), _sds((t, N_SUPER * 2 * w))), (N_SUPER, 2, nt),
                 [pl.BlockSpec((tt, 128), lambda s, p, i: (i, s)),
                  pl.BlockSpec((None, 128, 2 * w), lambda s, p, i: (s, 0, 0)),
                  pl.BlockSpec((None, 2 * w, 128), lambda s, p, i: (s, 0, 0)),
                  pl.BlockSpec((None, 1, 2 * w), lambda s, p, i: (s, 0, 0))],
                 (pl.BlockSpec((tt, 128), lambda s, p, i: (i * p, s)),
                  pl.BlockSpec((tt, 2 * w), lambda s, p, i: (i * p, s))),
                 scratch=[pltpu.VMEM((tt, 2 * w), f32), pltpu.VMEM((SEG, 2 * w), f32)])(u_perm, bbd, cbd, lam)


def s5_bwd(name, dy_perm, u_perm, h_perm, hprev0, bbd, cbd, lam):
    t = u_perm.shape[0]
    tt = _tile(t, (1024, 512, 256, 128))
    nt, ng, w = t // tt, tt // SEG, SG_STATE
    seg_len = t // SEG

    def body(dy_ref, u_ref, h_ref, hp_ref, hp0_ref, b_ref, c_ref, lam_ref, du_ref, db_ref, dc_ref, dl_ref,
             g_scr, state, dl_acc):
        p, i = pl.program_id(1), pl.program_id(2)
        first_tile = jnp.logical_or(p == 0, i == nt - 1)
        lr1, li1 = lam_ref[:, 0:w], -lam_ref[:, w:2 * w]
        lr, li = jnp.broadcast_to(lr1, (SEG, w)), jnp.broadcast_to(li1, (SEG, w))
        g_scr[...] = _bdot(dy_ref[...], c_ref[...], ((1,), (1,)))

        @pl.when(jnp.logical_and(p == 0, i == 0))
        def _():
            state[...] = jnp.zeros_like(state)

        @pl.when(jnp.logical_and(p == 1, i == 0))
        def _():
            sr, si = _segment_init(state[:, 0:w], state[:, w:2 * w], lr1, li1, seg_len, True)
            state[:, 0:w] = sr
            state[:, w:2 * w] = si
            dl_acc[...] = jnp.zeros_like(dl_acc)

        def adj(g, st):
            row = pl.multiple_of(g * SEG, SEG)
            gg = g_scr[pl.ds(row, SEG), :]
            nr = lr * st[0] - li * st[1] + gg[:, 0:w]
            ni = lr * st[1] + li * st[0] + gg[:, w:2 * w]
            return row, nr, ni

        @pl.when(p == 0)
        def _():
            def step(k, st):
                _, nr, ni = adj(ng - 1 - k, st)
                return nr, ni

            fin = lax.fori_loop(0, ng, step, (state[:, 0:w], state[:, w:2 * w]))
            state[:, 0:w] = fin[0]
            state[:, w:2 * w] = fin[1]

        @pl.when(p == 1)
        def _():
            above = jnp.where(first_tile, hp0_ref[...], hp_ref[...])

            def step(k, st):
                g = ng - 1 - k
                row, nr, ni = adj(g, st)
                g_scr[pl.ds(row, SEG), 0:w] = nr
                g_scr[pl.ds(row, SEG), w:2 * w] = ni
                prow = pl.multiple_of(jnp.maximum(g - 1, 0) * SEG, SEG)
                hp = jnp.where(g > 0, h_ref[pl.ds(prow, SEG), :], above)
                pr, pi = hp[:, 0:w], hp[:, w:2 * w]
                return nr, ni, st[2] + nr * pr + ni * pi, st[3] + ni * pr - nr * pi

            fin = lax.fori_loop(0, ng, step, (state[:, 0:w], state[:, w:2 * w], dl_acc[:, 0:w], dl_acc[:, w:2 * w]))
            state[:, 0:w] = fin[0]
            state[:, w:2 * w] = fin[1]
            dl_acc[:, 0:w] = fin[2]
            dl_acc[:, w:2 * w] = fin[3]
            a = g_scr[...]
            du_ref[...] = _bdot(a, b_ref[...], ((1,), (1,)))
            d_b = _bdot(u_ref[...], a, ((0,), (0,)))
            d_c = _bdot(dy_ref[...], h_ref[...], ((0,), (0,)))

            @pl.when(i == 0)
            def _():
                db_ref[...] = d_b
                dc_ref[...] = d_c

            @pl.when(i > 0)
            def _():
                db_ref[...] += d_b
                dc_ref[...] += d_c

            @pl.when(i == nt - 1)
            def _():
                dl_ref[...] = jnp.sum(dl_acc[...], axis=0, keepdims=True)

    tile = lambda s, p, i: (nt - 1 - i, s)
    tile1 = lambda s, p, i: (nt - 1 - i * p, s)
    above = lambda s, p, i: (jnp.maximum((nt - 1 - i * p) * (tt // SEG) - 1, 0), s)
    per_s = lambda s, p, i: (s, 0, 0)
    return _call(body, name, (_sds((t, D_MODEL)), _sds((N_SUPER, 128, 2 * w)), _sds((N_SUPER, 128, 2 * w)),
                              _sds((N_SUPER, 1, 2 * w))), (N_SUPER, 2, nt),
                 [pl.BlockSpec((tt, 128), tile), pl.BlockSpec((tt, 128), tile1), pl.BlockSpec((tt, 2 * w), tile1),
                  pl.BlockSpec((SEG, 2 * w), above), pl.BlockSpec((SEG, 2 * w), lambda s, p, i: (0, s)),
                  pl.BlockSpec((None, 128, 2 * w), per_s), pl.BlockSpec((None, 2 * w, 128), per_s),
                  pl.BlockSpec((None, 1, 2 * w), per_s)],
                 (pl.BlockSpec((tt, 128), tile1), pl.BlockSpec((None, 128, 2 * w), per_s),
                  pl.BlockSpec((None, 128, 2 * w), per_s), pl.BlockSpec((None, 1, 2 * w), per_s)),
                 scratch=[pltpu.VMEM((tt, 2 * w), f32), pltpu.VMEM((SEG, 2 * w), f32),
                          pltpu.VMEM((SEG, 2 * w), f32)])(dy_perm, u_perm, h_perm, h_perm, hprev0, bbd, cbd, lam)


def _s5_disc(a_re, a_im, log_dt, br, bi):
    dt = jnp.exp(log_dt)
    mag = jnp.exp(a_re * dt)
    lr, li = mag * jnp.cos(a_im * dt), mag * jnp.sin(a_im * dt)
    den = a_re * a_re + a_im * a_im
    fr = ((lr - 1.0) * a_re + li * a_im) / den
    fi = (li * a_re - (lr - 1.0) * a_im) / den
    return lr, li, fr * br - fi * bi, fr * bi + fi * br


def s5_disc_fwd(name, a_re, a_im, log_dt, br, bi):
    g, n = SSM_GROUPS, SSM_STATE

    def body(ar, ai, ld, brr, bir, lr, li, bbr, bbi):
        o = _s5_disc(ar[...], ai[...], ld[...], brr[...], bir[...])
        lr[...], li[...], bbr[...], bbi[...] = o

    return _call(body, name, (_sds((g, 1, n)), _sds((g, 1, n)), _sds((g, SSM_GROUP, n)), _sds((g, SSM_GROUP, n))))(
        a_re, a_im, log_dt, br, bi)


def s5_disc_bwd(name, a_re, a_im, log_dt, br, bi, d_lr, d_li, d_bbr, d_bbi):
    g, n = SSM_GROUPS, SSM_STATE

    def body(ar, ai, ld, brr, bir, c1, c2, c3, c4, o1, o2, o3, o4, o5):
        _, vjp = jax.vjp(_s5_disc, ar[...], ai[...], ld[...], brr[...], bir[...])
        o1[...], o2[...], o3[...], o4[...], o5[...] = vjp((c1[...], c2[...], c3[...], c4[...]))

    return _call(body, name, (_sds((g, 1, n)), _sds((g, 1, n)), _sds((g, 1, 1)), _sds((g, SSM_GROUP, n)),
                              _sds((g, SSM_GROUP, n))))(a_re, a_im, log_dt, br, bi, d_lr, d_li, d_bbr, d_bbi)


def gelu_fwd(name, ys_lin, proj, d_skip):
    t, d = ys_lin.shape
    tm = _rows(t, light=True)

    def body(y_ref, u_ref, d_ref, o_ref):
        o_ref[...] = _gelu(y_ref[...] + d_ref[...] * u_ref[...])

    return _call(body, name, _sds((t, d)), (t // tm,),
                 [_row_spec(tm, d), pl.BlockSpec((tm, d), lambda i: (i, 4)), _acc_spec(1, d)],
                 _row_spec(tm, d))(ys_lin, proj, d_skip)


def _head_norm(o, hn):
    outs, ns, rs = [], [], []
    for h in range(DN_HEADS):
        oh = o[:, h * 128:(h + 1) * 128]
        r = lax.rsqrt(jnp.mean(oh * oh, axis=-1, keepdims=True) + EPS)
        n = oh * r
        ns.append(n)
        rs.append(r)
        outs.append(n * hn)
    return outs, ns, rs


def merge_fwd(name, proj, o, yg, glu_lin, head_norm, b_glu):
    t = o.shape[0]
    tm = _rows(t)
    d = D_MODEL

    def body(za_ref, zb_ref, ra_ref, rb_ref, o_ref, yg_ref, gl_ref, hn_ref, bg_ref, m_ref):
        on, _, _ = _head_norm(o_ref[...], hn_ref[...])
        y_a = jnp.concatenate(on, axis=1) * _silu(za_ref[...])
        y_b = yg_ref[...] * _sigmoid(gl_ref[...] + bg_ref[...]) * _silu(zb_ref[...])
        m_ref[...] = (_sigmoid(ra_ref[...]) * y_a + _sigmoid(rb_ref[...]) * y_b).astype(bf16)

    pc = lambda c: pl.BlockSpec((tm, d), lambda i: (i, c))
    return _call(body, name, _sds((t, d), bf16), (t // tm,),
                 [pc(3), pc(5), pc(6), pc(7), _row_spec(tm, d), _row_spec(tm, d), _row_spec(tm, d),
                  _acc_spec(1, 128), _acc_spec(1, d)], _row_spec(tm, d))(
        proj, proj, proj, proj, o, yg, glu_lin, head_norm, b_glu)


def merge_bwd(name, proj, o, yg, glu_lin, head_norm, b_glu, dm):
    t = o.shape[0]
    tm = _rows(t)
    d = D_MODEL

    def body(za_ref, zb_ref, ra_ref, rb_ref, o_ref, yg_ref, gl_ref, hn_ref, bg_ref, dm_ref,
             dza_ref, dzb_ref, dra_ref, drb_ref, do_ref, dgl_ref, dyg_ref, dhn_ref, dbg_ref):
        hn = hn_ref[...]
        za, zb, ra, rb = za_ref[...], zb_ref[...], ra_ref[...], rb_ref[...]
        on, ns, rs = _head_norm(o_ref[...], hn)
        onc = jnp.concatenate(on, axis=1)
        sga, sgb = _sigmoid(za), _sigmoid(zb)
        sza = za * sga
        y_a = onc * sza
        yg = yg_ref[...]
        sgl = _sigmoid(gl_ref[...] + bg_ref[...])
        y2 = yg * sgl
        szb = zb * sgb
        y_b = y2 * szb
        sra, srb = _sigmoid(ra), _sigmoid(rb)
        dmv = dm_ref[...]
        dra_ref[...] = (dmv * y_a * sra * (1.0 - sra)).astype(bf16)
        drb_ref[...] = (dmv * y_b * srb * (1.0 - srb)).astype(bf16)
        d_ya = dmv * sra
        d_yb = dmv * srb
        dza_ref[...] = (d_ya * onc * (sga * (1.0 + za * (1.0 - sga)))).astype(bf16)
        dzb_ref[...] = (d_yb * y2 * (sgb * (1.0 + zb * (1.0 - sgb)))).astype(bf16)
        d_on = d_ya * sza
        d_y2 = d_yb * szb
        dyg_ref[...] = d_y2 * sgl
        d_gl = d_y2 * yg * sgl * (1.0 - sgl)
        dgl_ref[...] = d_gl.astype(bf16)
        _accumulate(dbg_ref, jnp.sum(d_gl, axis=0, keepdims=True))
        d_hn = jnp.zeros((1, 128), f32)
        for h in range(DN_HEADS):
            hs = slice(h * 128, (h + 1) * 128)
            dh = d_on[:, hs]
            d_hn = d_hn + jnp.sum(dh * ns[h], axis=0, keepdims=True)
            dn = dh * hn
            do_ref[:, hs] = rs[h] * (dn - ns[h] * jnp.mean(dn * ns[h], axis=-1, keepdims=True))
        _accumulate(dhn_ref, d_hn)

    pc = lambda c: pl.BlockSpec((tm, d), lambda i: (i, c))
    rs_ = _row_spec(tm, d)
    return _call(body, name, (_sds((t, d), bf16),) * 4 + (_sds((t, d)), _sds((t, d), bf16), _sds((t, d)),
                              _sds((1, 128)), _sds((1, d))),
                 (t // tm,), [pc(3), pc(5), pc(6), pc(7), rs_, rs_, rs_, _acc_spec(1, 128), _acc_spec(1, d), rs_],
                 (rs_,) * 7 + (_acc_spec(1, 128), _acc_spec(1, d)))(
        proj, proj, proj, proj, o, yg, glu_lin, head_norm, b_glu, dm)


def gelu_bwd(name, ys_lin, proj, d_skip, dyg_a, dyg_b):
    t, d = ys_lin.shape
    tm = _rows(t, light=True)

    def body(y_ref, u_ref, d_ref, a_ref, b_ref, dys_ref, du_ref, dd_ref):
        uv = u_ref[...]
        dys = (a_ref[...] + b_ref[...]) * _dgelu(y_ref[...] + d_ref[...] * uv)
        dys_ref[...] = dys
        du_ref[...] = dys * d_ref[...]
        _accumulate(dd_ref, jnp.sum(dys * uv, axis=0, keepdims=True))

    rs_ = _row_spec(tm, d)
    return _call(body, name, (_sds((t, d)), _sds((t, d)), _sds((1, d))), (t // tm,),
                 [rs_, pl.BlockSpec((tm, d), lambda i: (i, 4)), _acc_spec(1, d), rs_, rs_],
                 (rs_, rs_, _acc_spec(1, d)))(ys_lin, proj, d_skip, dyg_a, dyg_b)


def assemble_dproj(name, dc, conv_w, dza, du_a, du_b, dzb, dra, drb, dbd):
    t = dza.shape[0]
    tm = _rows(t)
    nt = t // tm
    d = D_MODEL
    w3 = 3 * d

    def body(cur_ref, nxt_ref, w_ref, za_ref, ua_ref, ub_ref, zb_ref, ra_ref, rb_ref, bd_ref, o_ref, pad_ref):
        keep = (pl.program_id(0) < nt - 1).astype(f32)
        pad_ref[0:tm, :] = cur_ref[...]
        pad_ref[tm:tm + 8, :] = nxt_ref[...] * keep
        for cb in range(w3 // 128):
            cs = slice(cb * 128, (cb + 1) * 128)
            acc = pad_ref[pl.ds(CONV_K - 1, tm), cs] * w_ref[0:1, cs]
            for j in range(1, CONV_K):
                acc = acc + pad_ref[pl.ds(CONV_K - 1 - j, tm), cs] * w_ref[j:j + 1, cs]
            o_ref[:, cs] = acc.astype(bf16)
        o_ref[:, 3 * d:4 * d] = za_ref[...].astype(bf16)
        o_ref[:, 4 * d:5 * d] = (ua_ref[...] + ub_ref[...]).astype(bf16)
        o_ref[:, 5 * d:6 * d] = zb_ref[...].astype(bf16)
        o_ref[:, 6 * d:7 * d] = ra_ref[...].astype(bf16)
        o_ref[:, 7 * d:8 * d] = rb_ref[...].astype(bf16)
        o_ref[:, 8 * d:8 * d + 128] = bd_ref[...].astype(bf16)
        o_ref[:, 8 * d + 128:W_PAD] = jnp.zeros((tm, W_PAD - 8 * d - 128), bf16)

    rs_ = _row_spec(tm, d)
    return _call(body, name, _sds((t, W_PAD), bf16), (nt,),
                 [_row_spec(tm, w3), _next_spec(tm, w3, t), _acc_spec(CONV_K, w3), rs_, rs_, rs_, rs_, rs_, rs_,
                  _row_spec(tm, 128)], _row_spec(tm, W_PAD),
                 scratch=[pltpu.VMEM((tm + 8, w3), f32)])(dc, dc, conv_w, dza, du_a, du_b, dzb, dra, drb, dbd)


def adamw(name, w, g, m, v):
    lead, (r, c) = w.shape[:-2], w.shape[-2:]
    tm = _tile(r, (512, 256, 128, 64, 32, 16, 8))
    c1 = 1.0 / (1.0 - ADAM_B1 ** ADAM_STEP)
    c2 = 1.0 / (1.0 - ADAM_B2 ** ADAM_STEP)

    def body(w_ref, g_ref, m_ref, v_ref, d_ref, nm_ref, nv_ref):
        gv = g_ref[...]
        nm = ADAM_B1 * m_ref[...] + (1.0 - ADAM_B1) * gv
        nv = ADAM_B2 * v_ref[...] + (1.0 - ADAM_B2) * (gv * gv)
        d_ref[...] = -ADAM_LR * ((nm * c1) / (jnp.sqrt(nv * c2) + ADAM_EPS) + ADAM_WD * w_ref[...])
        nm_ref[...] = nm
        nv_ref[...] = nv

    if lead:
        sp = pl.BlockSpec((None, tm, c), lambda l, i: (l, i, 0))
        grid = (lead[0], r // tm)
    else:
        sp = pl.BlockSpec((tm, c), lambda i: (i, 0))
        grid = (r // tm,)
    return _call(body, name, (_sds(w.shape),) * 3, grid, [sp] * 4, (sp,) * 3)(w, g, m, v)


def _coords():
    return lax.axis_index("x"), lax.axis_index("y"), lax.axis_index("c")


def _lin(dev):
    return 4 * dev[0] + 2 * dev[1] + dev[2]


def _chips(me):
    x, y, _ = me
    return [(1 - x, y), (x, 1 - y), (1 - x, 1 - y)]


def _gather_steps(x_ref, o_ref, send_sems, recv_sems, local_sem):
    me = _coords()
    x, y, cc = me
    sibling = (x, y, 1 - cc)
    chips = _chips(me)

    def copy(k, block, to, src=None):
        return pltpu.make_async_remote_copy(
            src_ref=o_ref.at[_lin(block)] if src is None else src, dst_ref=o_ref.at[_lin(block)],
            send_sem=send_sems.at[k], recv_sem=recv_sems.at[k], device_id=to, device_id_type=MESH)

    mine = pltpu.make_async_copy(x_ref, o_ref.at[_lin(me)], local_sem)
    first = [copy(0, me, sibling, src=x_ref)] + [copy(1 + j, me, (*chip, cc), src=x_ref)
                                                 for j, chip in enumerate(chips)]
    passed = [copy(4 + j, (*chip, cc), sibling) for j, chip in enumerate(chips)]

    def start():
        mine.start()
        for cp in first:
            cp.start()

    def forward():
        for j, chip in enumerate(chips):
            copy(1 + j, (*chip, cc), me).wait_recv()
            passed[j].start()

    def finish():
        copy(0, sibling, me).wait_recv()
        for j, chip in enumerate(chips):
            copy(4 + j, (*chip, 1 - cc), me).wait_recv()
        for cp in first + passed:
            cp.wait_send()
        mine.wait()

    return start, forward, finish


_GATHER_SEMS = [pltpu.SemaphoreType.DMA((N_DEV - 1,)), pltpu.SemaphoreType.DMA((N_DEV - 1,)),
                pltpu.SemaphoreType.DMA(())]


def all_gather(name, shard):
    r, c = shard.shape

    def body(x_ref, o_ref, send_sems, recv_sems, local_sem):
        for step in _gather_steps(x_ref, o_ref, send_sems, recv_sems, local_sem):
            step()

    any_spec = pl.BlockSpec(memory_space=pl.ANY)
    return _call(body, name, _sds((N_DEV, r, c), shard.dtype), in_specs=[any_spec], out_specs=any_spec,
                 scratch=list(_GATHER_SEMS))(shard)


def pair_exchange(name, blocks):
    _, _, r, c = blocks.shape

    def body(x_ref, o_ref, send_sems, recv_sems):
        x, y, cc = _coords()
        sibling = (x, y, 1 - cc)
        cps = [pltpu.make_async_remote_copy(src_ref=x_ref.at[ch, 1 - cc], dst_ref=o_ref.at[ch], send_sem=send_sems.at[ch],
                                            recv_sem=recv_sems.at[ch], device_id=sibling, device_id_type=MESH)
               for ch in range(4)]
        for cp in cps:
            cp.start()
        for cp in cps:
            cp.wait()

    any_spec = pl.BlockSpec(memory_space=pl.ANY)
    return _call(body, name, _sds((4, r, c), blocks.dtype), in_specs=[any_spec], out_specs=any_spec,
                 scratch=[pltpu.SemaphoreType.DMA((4,)), pltpu.SemaphoreType.DMA((4,))])(blocks)


def _chip_exchange_steps(x_ref, o_ref, send_sems, recv_sems, local_sem):
    me = _coords()
    x, y, cc = me
    my_chip = 2 * x + y
    mine = pltpu.make_async_copy(x_ref.at[my_chip], o_ref.at[my_chip], local_sem)
    peers = [(px, py, cc) for px, py in _chips(me)]
    sends = [pltpu.make_async_remote_copy(src_ref=x_ref.at[2 * px + py], dst_ref=o_ref.at[my_chip],
                                          send_sem=send_sems.at[j], recv_sem=recv_sems.at[j], device_id=(px, py, pc),
                                          device_id_type=MESH) for j, (px, py, pc) in enumerate(peers)]

    def start():
        mine.start()
        for cp in sends:
            cp.start()

    def finish():
        for j, (px, py, pc) in enumerate(peers):
            pltpu.make_async_remote_copy(src_ref=x_ref.at[my_chip], dst_ref=o_ref.at[2 * px + py],
                                         send_sem=send_sems.at[j], recv_sem=recv_sems.at[j], device_id=(px, py, pc),
                                         device_id_type=MESH).wait_recv()
        for cp in sends:
            cp.wait_send()
        mine.wait()

    return start, finish


_CHIP_SEMS = [pltpu.SemaphoreType.DMA((3,)), pltpu.SemaphoreType.DMA((3,)), pltpu.SemaphoreType.DMA(())]


def chip_exchange(name, blocks):
    def body(x_ref, o_ref, send_sems, recv_sems, local_sem):
        for step in _chip_exchange_steps(x_ref, o_ref, send_sems, recv_sems, local_sem):
            step()

    any_spec = pl.BlockSpec(memory_space=pl.ANY)
    return _call(body, name, _sds(blocks.shape, blocks.dtype), in_specs=[any_spec], out_specs=any_spec,
                 scratch=list(_CHIP_SEMS))(blocks)


def pair_sum(name, mine, theirs):
    _, r, c = mine.shape
    tm = max(d for d in range(16, 769, 16) if r % d == 0)

    def body(a_ref, b_ref, o_ref):
        o_ref[...] = (a_ref[...].astype(f32) + b_ref[...].astype(f32)).astype(o_ref.dtype)

    sp = pl.BlockSpec((None, tm, c), lambda ch, i: (ch, i, 0))
    return _call(body, name, _sds(mine.shape, mine.dtype), (4, r // tm), [sp, sp], sp)(mine, theirs)


def sum_slots(name, slots):
    n, r, c = slots.shape
    tm = max(d for d in range(16, 769, 16) if r % d == 0)

    def body(s_ref, o_ref):
        acc = s_ref[0].astype(f32)
        for d in range(1, n):
            acc = acc + s_ref[d].astype(f32)
        o_ref[...] = acc

    return _call(body, name, _sds((r, c)), (r // tm,), [pl.BlockSpec((n, tm, c), lambda i: (0, i, 0))],
                 pl.BlockSpec((tm, c), lambda i: (i, 0)))(slots)


def _rows_layout(col8, t):
    return col8.T.reshape(DN_HEADS, t // CHUNK, CHUNK)


def _blockdiag(m):
    g, a, b = m.shape
    m = m.reshape(N_SUPER, SUPER, a, b)
    out = jnp.einsum("sgab,gh->sgahb", m, jnp.eye(SUPER, dtype=m.dtype))
    return out.reshape(N_SUPER, SUPER * a, SUPER * b)


def _diag_blocks(m, a, b):
    m = m.reshape(N_SUPER, SUPER, a, SUPER, b)
    return jnp.einsum("sgahb,gh->sgab", m, jnp.eye(SUPER, dtype=m.dtype)).reshape(SSM_GROUPS, a, b)


def _s5_params(p, li):
    tag = f"l{li}"
    n = SSM_STATE
    a_re = p["ssm_a_re"].reshape(SSM_GROUPS, 1, n)
    a_im = p["ssm_a_im"].reshape(SSM_GROUPS, 1, n)
    log_dt = p["ssm_log_dt"].reshape(SSM_GROUPS, 1, 1)
    br = jnp.swapaxes(p["ssm_b_re"], 1, 2)
    bi = jnp.swapaxes(p["ssm_b_im"], 1, 2)
    lr, li_, bbr, bbi = s5_disc_fwd("s5_disc_" + tag, a_re, a_im, log_dt, br, bi)
    lam = jnp.concatenate([lr.reshape(N_SUPER, 1, SG_STATE), li_.reshape(N_SUPER, 1, SG_STATE)], axis=-1)
    bbd = jnp.concatenate([_blockdiag(bbr), _blockdiag(bbi)], axis=-1).astype(bf16)
    c_re = jnp.swapaxes(p["ssm_c_re"], 1, 2)
    c_im = jnp.swapaxes(p["ssm_c_im"], 1, 2)
    cbd = jnp.concatenate([_blockdiag(c_re), -_blockdiag(c_im)], axis=1).astype(bf16)
    return dict(a_re=a_re, a_im=a_im, log_dt=log_dt, br=br, bi=bi, lam=lam, bbd=bbd, cbd=cbd)


def layer_fwd(x, p, li, next_shard=None):
    tag = f"l{li}"
    t = x.shape[0]
    d = D_MODEL
    h = rmsnorm_fwd("norm_pre_" + tag, x, p["norm_pre"])
    proj = mm_nn("proj_" + tag, h, p["w_all"])
    q, k, v, bg = conv_qkv_fwd("conv_" + tag, proj, p["conv_w"], p["a_log"], p["dt_bias"])
    b_rows = _rows_layout(bg[:, 0:DN_HEADS], t)
    g_rows = _rows_layout(bg[:, DN_HEADS:2 * DN_HEADS], t)
    o, s_all, t_all, u_all, w_sol, *gathered = delta_fwd("delta_" + tag, q, k, v, g_rows, b_rows, next_shard)
    sp = _s5_params(p, li)
    u_perm = _perm_rows(proj[:, 4 * d:5 * d])
    ys_perm, hs = s5_fwd("s5_" + tag, u_perm, sp["bbd"], sp["cbd"], sp["lam"])
    ys_lin = _unperm_rows(ys_perm)
    yg = gelu_fwd("gelu_" + tag, ys_lin, proj, p["ssm_d"])
    glu_lin = mm_nn("glu_" + tag, yg, p["w_glu"])
    merged = merge_fwd("merge_" + tag, proj, o, yg, glu_lin, p["head_norm"], p["b_glu"])
    out = mm_nn("out_" + tag, merged, p["w_out"])
    y = residual_norm_fwd("norm_post_" + tag, x, out, p["norm_post"])
    saved = dict(x=x, h=h, proj=proj, q=q, k=k, v=v, g_rows=g_rows, b_rows=b_rows, o=o, s_all=s_all, t_all=t_all, u_all=u_all, w_sol=w_sol, sp=sp, u_perm=u_perm,
                 hs=hs, ys_lin=ys_lin, yg=yg, glu_lin=glu_lin, merged=merged, out=out)
    return y, saved, (gathered[0] if gathered else None)


def layer_bwd(dy, p, s, li, chip_part=None):
    tag = f"l{li}"
    t = dy.shape[0]
    d = D_MODEL
    sp = s["sp"]
    gr = {}
    d_out, gr["norm_post"] = post_norm_bwd("norm_post_b_" + tag, s["out"], p["norm_post"], dy)
    d_merged = mm_nt("out_b_" + tag, d_out, p["w_out"])
    gr["w_out"] = mm_tn("out_w_" + tag, s["merged"], d_out, bf16)
    (dza, dzb, dra, drb, d_o, d_glu, dyg_a, gr["head_norm"], gr["b_glu"]) = merge_bwd(
        "merge_b_" + tag, s["proj"], s["o"], s["yg"], s["glu_lin"], p["head_norm"], p["b_glu"], d_merged)
    dyg_b = mm_nt("glu_b_" + tag, d_glu, p["w_glu"])
    gr["w_glu"] = mm_tn("glu_w_" + tag, s["yg"], d_glu, bf16)
    d_ys, du_a, gr["ssm_d"] = gelu_bwd("gelu_b_" + tag, s["ys_lin"], s["proj"], p["ssm_d"], dyg_a, dyg_b)
    hprev0 = jnp.concatenate([jnp.zeros((1, s["hs"].shape[1]), f32), s["hs"][-SEG:-1]], axis=0)
    du_perm, d_bbd, d_cbd, d_lam = s5_bwd("s5_b_" + tag, _perm_rows(d_ys), s["u_perm"], s["hs"], hprev0, sp["bbd"],
                                          sp["cbd"], sp["lam"])
    du_b = _unperm_rows(du_perm)
    gr["ssm_c_re"] = _diag_blocks(d_cbd[:, :, 0:SG_STATE], SSM_GROUP, SSM_STATE)
    gr["ssm_c_im"] = -_diag_blocks(d_cbd[:, :, SG_STATE:], SSM_GROUP, SSM_STATE)
    d_bbr = _diag_blocks(d_bbd[:, :, 0:SG_STATE], SSM_GROUP, SSM_STATE)
    d_bbi = _diag_blocks(d_bbd[:, :, SG_STATE:], SSM_GROUP, SSM_STATE)
    d_lr = d_lam[:, :, 0:SG_STATE].reshape(SSM_GROUPS, 1, SSM_STATE)
    d_li = d_lam[:, :, SG_STATE:].reshape(SSM_GROUPS, 1, SSM_STATE)
    d_are, d_aim, d_ldt, d_br, d_bi = s5_disc_bwd("s5_disc_b_" + tag, sp["a_re"], sp["a_im"], sp["log_dt"], sp["br"],
                                                  sp["bi"], d_lr, d_li, d_bbr, d_bbi)
    gr["ssm_a_re"] = d_are.reshape(SSM_GROUPS, SSM_STATE)
    gr["ssm_a_im"] = d_aim.reshape(SSM_GROUPS, SSM_STATE)
    gr["ssm_log_dt"] = d_ldt.reshape(SSM_GROUPS)
    gr["ssm_b_re"] = jnp.swapaxes(d_br, 1, 2)
    gr["ssm_b_im"] = jnp.swapaxes(d_bi, 1, 2)
    dq, dk, dv, dg_rows, db_rows, *slots = delta_bwd("delta_b_" + tag, s["q"], s["k"], s["v"], s["g_rows"],
                                                     s["b_rows"], s["s_all"], s["t_all"], s["u_all"], s["w_sol"], d_o,
                                                     chip_part)
    dbg = jnp.concatenate([db_rows.reshape(DN_HEADS, t).T, dg_rows.reshape(DN_HEADS, t).T,
                           jnp.zeros((t, 128 - 2 * DN_HEADS), f32)], axis=1)
    dc, gr["conv_w"], dbd, dab = conv_qkv_bwd("conv_b_" + tag, s["proj"], p["conv_w"], p["a_log"], p["dt_bias"],
                                              dq, dk, dv, dbg)
    gr["a_log"] = dab[0, DN_HEADS:2 * DN_HEADS]
    gr["dt_bias"] = dab[1, DN_HEADS:2 * DN_HEADS]
    dproj = assemble_dproj("dproj_" + tag, dc, p["conv_w"], dza, du_a, du_b, dzb, dra, drb, dbd)
    d_h = mm_nt("proj_b_" + tag, dproj, p["w_all"])
    gr["w_all"] = mm_nn("proj_w_" + tag, s["h"].T, dproj, bf16)
    dx, gr["norm_pre"] = rmsnorm_bwd("norm_pre_b_" + tag, s["x"], p["norm_pre"], d_h, dy)
    return dx, gr, (slots[0] if slots else None)


REPL = ["norm_pre", "a_log", "dt_bias", "head_norm", "ssm_a_re", "ssm_a_im", "ssm_log_dt", "ssm_b_re", "ssm_b_im",
        "ssm_c_re", "ssm_c_im", "ssm_d", "b_glu", "norm_post"]
SHARDED = ["w_in", "conv_w", "w_glu", "w_out"]
ALL_W = ["norm_pre", "w_in", "conv_w", "a_log", "dt_bias", "head_norm", "ssm_a_re", "ssm_a_im", "ssm_log_dt",
         "ssm_b_re", "ssm_b_im", "ssm_c_re", "ssm_c_im", "ssm_d", "w_glu", "b_glu", "w_out", "norm_post"]
PACK_W = 1024


def _pack_flat(arrs, rows):
    flat = jnp.concatenate([a.reshape(-1) for a in arrs])
    return jnp.pad(flat, (0, rows * PACK_W - flat.shape[0])).reshape(rows, PACK_W)


def _flat_rows(arrs, mult=8):
    n = sum(math.prod(a.shape) for a in arrs)
    rows = -(-n // PACK_W)
    return -(-rows // mult) * mult


def _unpack(flat, shapes):
    out, off = [], 0
    for sh in shapes:
        n = math.prod(sh)
        out.append(flat[off:off + n].reshape(sh))
        off += n
    return out


def _repl_split(shapes):
    big = [n for n in REPL if math.prod(shapes[n]) % PACK_W == 0]
    small = [n for n in REPL if n not in big]
    return big, small


def _rows8(n):
    return -(-n // (8 * PACK_W)) * 8


def _repl_rows(shapes):
    big, small = _repl_split(shapes)
    rows = sum(_rows8(math.prod(shapes[n])) for n in big) + _flat_rows([_sds(shapes[n]) for n in small], 8)
    return -(-rows // (16 * N_DEV)) * (16 * N_DEV)


def _repl_pack(arrs, shapes):
    big, small = _repl_split(shapes)
    parts = []
    for n in big:
        a = arrs[n].reshape(-1, PACK_W)
        parts.append(jnp.pad(a, ((0, _rows8(a.size) - a.shape[0]), (0, 0))))
    parts.append(_pack_flat([arrs[n] for n in small], _flat_rows([_sds(shapes[n]) for n in small], 8)))
    used = sum(p.shape[0] for p in parts)
    parts.append(jnp.zeros((_repl_rows(shapes) - used, PACK_W), parts[0].dtype))
    return jnp.concatenate(parts, axis=0)


def _repl_unpack(packed, shapes):
    big, small = _repl_split(shapes)
    out, off = {}, 0
    for n in big:
        size = math.prod(shapes[n])
        out[n] = packed[off:off + size // PACK_W].reshape(shapes[n])
        off += _rows8(size)
    srows = _flat_rows([_sds(shapes[n]) for n in small], 8)
    out.update(zip(small, _unpack(packed[off:off + srows].reshape(-1), [shapes[n] for n in small])))
    return out


_COL_RUNS = ((0, 4096), (4112, W_COLS), (4096, 4112))


def _w_all(main, tails, wc):
    pieces = []
    for lo, hi in _COL_RUNS:
        for dv in range(N_DEV):
            a, b = max(lo, dv * wc) - dv * wc, min(hi, (dv + 1) * wc) - dv * wc
            if a < min(b, PACK_W):
                pieces.append(main[dv][:, :, a:min(b, PACK_W)])
            if b > max(a, PACK_W):
                pieces.append(tails[dv][:, :, max(a, PACK_W) - PACK_W:b - PACK_W])
    ll, rows = main[0].shape[0], main[0].shape[1]
    pieces.append(jnp.zeros((ll, rows, W_PAD - W_COLS), main[0].dtype))
    return jnp.concatenate(pieces, axis=2)


def _ref_cols(g, lo, hi):
    pieces, off = [], 0
    for a, b in _COL_RUNS:
        s, e = max(lo, a), min(hi, b)
        if s < e:
            pieces.append((s, g[..., off + s - a:off + e - a]))
        off += b - a
    pieces.sort(key=lambda t: t[0])
    return jnp.concatenate([p for _, p in pieces], axis=-1) if len(pieces) > 1 else pieces[0][1]


def kernel(x, norm_pre, w_in, conv_w, a_log, dt_bias, head_norm, ssm_a_re, ssm_a_im, ssm_log_dt, ssm_b_re, ssm_b_im, ssm_c_re, ssm_c_im, ssm_d, w_glu, b_glu, w_out, norm_post, loss_target, m_norm_pre, m_w_in, m_conv_w, m_a_log, m_dt_bias, m_head_norm, m_ssm_a_re, m_ssm_a_im, m_ssm_log_dt, m_ssm_b_re, m_ssm_b_im, m_ssm_c_re, m_ssm_c_im, m_ssm_d, m_w_glu, m_b_glu, m_w_out, m_norm_post, v_norm_pre, v_w_in, v_conv_w, v_a_log, v_dt_bias, v_head_norm, v_ssm_a_re, v_ssm_a_im, v_ssm_log_dt, v_ssm_b_re, v_ssm_b_im, v_ssm_c_re, v_ssm_c_im, v_ssm_d, v_w_glu, v_b_glu, v_w_out, v_norm_post):
    loc = dict(locals())
    w = {n: loc[n] for n in ALL_W}
    m = {n: loc["m_" + n] for n in ALL_W}
    v = {n: loc["v_" + n] for n in ALL_W}
    depth = w_in.shape[0]
    wc = w_in.shape[2]
    cc = conv_w.shape[2]
    wr = w_glu.shape[1]

    tail = wc - PACK_W
    conv_hi = conv_w.astype(bf16)
    conv_mid = (conv_w - conv_hi.astype(f32)).astype(bf16)
    conv_lo = (conv_w - conv_hi.astype(f32) - conv_mid.astype(f32)).astype(bf16)
    conv3 = jnp.stack([conv_hi, conv_mid, conv_lo], axis=1)
    w_in_b = w_in.astype(bf16)
    MISC = 16
    o_glu, o_out, o_tail, o_conv = D_MODEL, D_MODEL + wr, D_MODEL + 2 * wr, D_MODEL + 2 * wr + MISC

    def rows16(a):
        return jnp.pad(a, ((0, MISC - a.shape[0]), (0, PACK_W - a.shape[1])))

    def layer_shard(li):
        return jnp.concatenate([w_in_b[li, :, :PACK_W], w_glu[li].astype(bf16), w_out[li].astype(bf16),
                                rows16(w_in_b[li, :, PACK_W:].T), rows16(conv3[li].reshape(3 * CONV_K, cc))])

    def layer_weights(gathered):
        tails = gathered[:, o_tail:o_tail + tail].reshape(N_DEV * tail, PACK_W).T
        w_all = _w_all([gathered[dv, :D_MODEL][None] for dv in range(N_DEV)],
                       [tails[None, :, dv * tail:(dv + 1) * tail] for dv in range(N_DEV)], wc)[0]
        glu = gathered[:, o_glu:o_glu + wr].reshape(N_DEV * wr, D_MODEL)
        out = gathered[:, o_out:o_out + wr].reshape(N_DEV * wr, D_MODEL)
        conv = gathered[:, o_conv:o_conv + 3 * CONV_K, :cc].astype(f32).reshape(N_DEV, 3, CONV_K, cc)
        conv = (conv[:, 0] + conv[:, 1] + conv[:, 2]).transpose(1, 0, 2).reshape(CONV_K, N_DEV * cc)
        return dict(w_all=w_all, w_glu=glu, w_out=out, conv_w=conv)

    def layer_params(li, gathered):
        return dict(layer_weights(gathered), norm_pre=norm_pre[li].reshape(1, -1),
                    a_log=jnp.pad(a_log[li], (DN_HEADS, 128 - 2 * DN_HEADS)).reshape(1, 128),
                    dt_bias=jnp.pad(dt_bias[li], (DN_HEADS, 128 - 2 * DN_HEADS)).reshape(1, 128),
                    head_norm=head_norm[li].reshape(1, -1), ssm_a_re=ssm_a_re[li], ssm_a_im=ssm_a_im[li],
                    ssm_log_dt=ssm_log_dt[li], ssm_b_re=ssm_b_re[li], ssm_b_im=ssm_b_im[li], ssm_c_re=ssm_c_re[li],
                    ssm_c_im=ssm_c_im[li], ssm_d=ssm_d[li].reshape(1, -1),
                    b_glu=b_glu[li].reshape(1, -1), norm_post=norm_post[li].reshape(1, -1))

    act = x[0]
    saved, params = [], []
    gathered = all_gather("gather_weights", layer_shard(0))
    for li in range(depth):
        params.append(layer_params(li, gathered))
        act, sv, gathered = layer_fwd(act, params[li], li, layer_shard(li + 1) if li + 1 < depth else None)
        saved.append(sv)
    loss_part, dy = loss_head("loss_head", act, loss_target[0])
    repl_shapes = {n: w[n].shape for n in REPL}
    repl_rows = _repl_rows(repl_shapes)
    rr = repl_rows // N_DEV
    my_c = lax.axis_index("c")

    def chip_sums(li, gl, extra=None):
        g_conv = gl["conv_w"].astype(bf16)
        tails = jnp.concatenate([_ref_cols(gl["w_all"], dv * wc + PACK_W, (dv + 1) * wc) for dv in range(N_DEV)],
                                axis=1).T
        blocks = jnp.stack([jnp.concatenate(
            [_ref_cols(gl["w_all"], dv * wc, dv * wc + PACK_W), gl["w_glu"][dv * wr:(dv + 1) * wr],
             gl["w_out"][dv * wr:(dv + 1) * wr], rows16(tails[dv * tail:(dv + 1) * tail]),
             rows16(g_conv[:, dv * cc:(dv + 1) * cc])]
            + ([extra[dv * rr:(dv + 1) * rr]] if extra is not None else []))
            for dv in range(N_DEV)])
        blocks = blocks.reshape(4, 2, blocks.shape[1], PACK_W)
        from_sibling = pair_exchange(f"pair_grads_l{li}", blocks)
        own = lax.dynamic_index_in_dim(blocks, my_c, axis=1, keepdims=False)
        return pair_sum(f"pair_sum_grads_l{li}", own, from_sibling)

    grads, slots, pending = [None] * depth, [None] * depth, None
    for li in reversed(range(depth)):
        dy, grads[li], got = layer_bwd(dy, params[li], saved[li], li, pending)
        if pending is not None:
            slots[li + 1] = got
        g_repl = None
        if li == 0:
            g_repl = _repl_pack({n: jnp.stack([grads[l][n] for l in range(depth)]).reshape(w[n].shape) for n in REPL},
                                repl_shapes).astype(bf16)
        pending = chip_sums(li, grads[li], g_repl)
    slots[0] = chip_exchange("scatter_grads", pending)
    grad_x = dy[None]
    loss = lax.psum(loss_part[0, 0], ("x", "y", "c"))
    mine = [sum_slots(f"sum_grads_l{li}", slots[li]) for li in range(depth)]
    o_repl = o_conv + MISC
    gs_w_in = jnp.stack([jnp.concatenate([mi[:o_glu], mi[o_tail:o_tail + tail].T], axis=1) for mi in mine])
    gs_glu = jnp.stack([mi[o_glu:o_out] for mi in mine])
    gs_out = jnp.stack([mi[o_out:o_tail] for mi in mine])
    gs_conv = jnp.stack([mi[o_conv:o_conv + CONV_K, :cc] for mi in mine])
    g_repl_full = all_gather("gather_repl_grads", mine[0][o_repl:o_repl + rr]).reshape(repl_rows, PACK_W)
    g = _repl_unpack(g_repl_full, repl_shapes)
    g.update(w_in=gs_w_in, w_glu=gs_glu, w_out=gs_out, conv_w=gs_conv)

    delta, new_m, new_v = {}, {}, {}
    for n in SHARDED:
        delta[n], new_m[n], new_v[n] = adamw("adamw_" + n, w[n], g[n], m[n], v[n])
    outs = adamw("adamw_repl", *[_repl_pack({n: src[n] for n in REPL}, repl_shapes) for src in (w, g, m, v)])
    for dst, o in zip((delta, new_m, new_v), outs):
        dst.update(_repl_unpack(o, repl_shapes))
    return (loss, grad_x, *[g[n] for n in ALL_W], *[delta[n] for n in ALL_W], *[new_m[n] for n in ALL_W],
            *[new_v[n] for n in ALL_W])
```

```python
import math

import jax
import jax.numpy as jnp
from jax import lax
from jax.experimental import pallas as pl
from jax.experimental.pallas import tpu as pltpu

f32 = jnp.float32
bf16 = jnp.bfloat16

D_MODEL = 1024
N_DEV = 8
DN_HEADS = 8
HEAD_DIM = 128
CHUNK = 64
CONV_K = 4
SSM_GROUPS = 64
SSM_GROUP = 16
SSM_STATE = 64
SUPER = 8
N_SUPER = SSM_GROUPS // SUPER
SG_STATE = SUPER * SSM_STATE
EPS = 1e-6
W_COLS = 8208
W_PAD = 8448
ADAM_LR, ADAM_B1, ADAM_B2, ADAM_EPS, ADAM_WD, ADAM_STEP = 0.001, 0.9, 0.999, 1e-08, 0.01, 10
VMEM_LIMIT = 56 * 1024 * 1024
MESH = pl.DeviceIdType.MESH
HIGH = lax.Precision.HIGH
DELTA_HB = 8


def _call(body, name, out_shape, grid=None, in_specs=None, out_specs=None, scratch=(), **kw):
    args = dict(out_shape=out_shape, name=name, scratch_shapes=list(scratch),
                compiler_params=pltpu.CompilerParams(vmem_limit_bytes=VMEM_LIMIT, **kw))
    if grid is not None:
        args.update(grid=grid, in_specs=in_specs, out_specs=out_specs)
    else:
        if in_specs is not None:
            args.update(in_specs=in_specs)
        if out_specs is not None:
            args.update(out_specs=out_specs)
    return pl.pallas_call(body, **args)


def _sds(shape, dtype=f32):
    return jax.ShapeDtypeStruct(tuple(shape), dtype)


def _sigmoid(x):
    return 1.0 / (1.0 + jnp.exp(-x))


def _silu(x):
    return x * _sigmoid(x)


_GELU_C = math.sqrt(2.0 / math.pi)


def _gelu(x):
    return 0.5 * x * (1.0 + jnp.tanh(_GELU_C * (x + 0.044715 * x * x * x)))


def _dgelu(x):
    t = jnp.tanh(_GELU_C * (x + 0.044715 * x * x * x))
    return 0.5 * (1.0 + t) + 0.5 * x * (1.0 - t * t) * _GELU_C * (1.0 + 3 * 0.044715 * x * x)


def _softplus(x):
    return jnp.maximum(x, 0.0) + jnp.log(1.0 + jnp.exp(-jnp.abs(x)))


def _bdot(a, b, dn):
    return lax.dot_general(a.astype(bf16), b.astype(bf16), (dn, ((), ())), preferred_element_type=f32)


def _matmul(name, a, b, *, dn, grid, a_spec, b_spec, o_spec, o_shape, o_dtype=f32):
    nk = grid[-1]

    def body(a_ref, b_ref, o_ref, acc_ref):
        p = _bdot(a_ref[...], b_ref[...], dn)
        if nk == 1:
            o_ref[...] = p.astype(o_dtype)
        else:
            k = pl.program_id(len(grid) - 1)

            @pl.when(k == 0)
            def _():
                acc_ref[...] = p

            @pl.when(k > 0)
            def _():
                acc_ref[...] += p

            @pl.when(k == nk - 1)
            def _():
                o_ref[...] = acc_ref[...].astype(o_dtype)

    blk = tuple(d for d in o_spec.block_shape if d is not None)
    return _call(body, name, _sds(o_shape, o_dtype), grid, [a_spec, b_spec], o_spec,
                 scratch=[pltpu.VMEM(blk if nk > 1 else (8, 128), f32)])(a, b)


def _tile(n, pref):
    for t in pref:
        if n % t == 0:
            return t
    return n


def mm_nn(name, a, b, o_dtype=f32):
    m, k = a.shape
    n = b.shape[1]
    tm, tn, tk = _tile(m, (1024, 512, 256)), _tile(n, (2816, 1024, 512)), _tile(k, (2816, 1024))
    return _matmul(name, a, b, dn=((1,), (0,)), grid=(m // tm, n // tn, k // tk),
                   a_spec=pl.BlockSpec((tm, tk), lambda i, j, l: (i, l)),
                   b_spec=pl.BlockSpec((tk, tn), lambda i, j, l: (l, j)),
                   o_spec=pl.BlockSpec((tm, tn), lambda i, j, l: (i, j)), o_shape=(m, n), o_dtype=o_dtype)


def mm_nt(name, a, b):
    m, k = a.shape
    n = b.shape[0]
    tm, tn, tk = _tile(m, (1024, 512, 256)), _tile(n, (1024, 512)), _tile(k, (2816, 1024))
    return _matmul(name, a, b, dn=((1,), (1,)), grid=(m // tm, n // tn, k // tk),
                   a_spec=pl.BlockSpec((tm, tk), lambda i, j, l: (i, l)),
                   b_spec=pl.BlockSpec((tn, tk), lambda i, j, l: (j, l)),
                   o_spec=pl.BlockSpec((tm, tn), lambda i, j, l: (i, j)), o_shape=(m, n))


def mm_tn(name, a, b, o_dtype=f32):
    k, m = a.shape
    n = b.shape[1]
    tm, tn, tk = _tile(m, (512,)), _tile(n, (2816, 1024, 512)), _tile(k, (512, 256))
    return _matmul(name, a, b, dn=((0,), (0,)), grid=(m // tm, n // tn, k // tk),
                   a_spec=pl.BlockSpec((tk, tm), lambda i, j, l: (l, i)),
                   b_spec=pl.BlockSpec((tk, tn), lambda i, j, l: (l, j)),
                   o_spec=pl.BlockSpec((tm, tn), lambda i, j, l: (i, j)), o_shape=(m, n), o_dtype=o_dtype)


def _rows(t, light=False):
    return _tile(t, (512, 256) if light else (256,))


def _row_spec(tm, w):
    return pl.BlockSpec((tm, w), lambda i: (i, 0))


def _acc_spec(r, w):
    return pl.BlockSpec((r, w), lambda i: (0, 0))


def _accumulate(ref, val):
    @pl.when(pl.program_id(0) == 0)
    def _():
        ref[...] = val

    @pl.when(pl.program_id(0) > 0)
    def _():
        ref[...] += val


def rmsnorm_fwd(name, x, gain):
    t, d = x.shape
    tm = _rows(t, light=True)

    def body(x_ref, g_ref, o_ref):
        xv = x_ref[...]
        r = lax.rsqrt(jnp.mean(xv * xv, axis=-1, keepdims=True) + EPS)
        o_ref[...] = (xv * r * g_ref[...]).astype(bf16)

    return _call(body, name, _sds((t, d), bf16), (t // tm,), [_row_spec(tm, d), _acc_spec(1, d)],
                 _row_spec(tm, d))(x, gain)


def rmsnorm_bwd(name, x, gain, dn, dres):
    t, d = x.shape
    tm = _rows(t, light=True)

    def body(x_ref, g_ref, dn_ref, dr_ref, dx_ref, dg_ref):
        xv = x_ref[...]
        r = lax.rsqrt(jnp.mean(xv * xv, axis=-1, keepdims=True) + EPS)
        n = xv * r
        dnv = dn_ref[...]
        _accumulate(dg_ref, jnp.sum(dnv * n, axis=0, keepdims=True))
        dng = dnv * g_ref[...]
        dx_ref[...] = dr_ref[...] + r * (dng - n * jnp.mean(dng * n, axis=-1, keepdims=True))

    return _call(body, name, (_sds((t, d)), _sds((1, d))), (t // tm,),
                 [_row_spec(tm, d), _acc_spec(1, d), _row_spec(tm, d), _row_spec(tm, d)],
                 (_row_spec(tm, d), _acc_spec(1, d)))(x, gain, dn, dres)


def residual_norm_fwd(name, x, out, gain):
    t, d = x.shape
    tm = _rows(t, light=True)

    def body(x_ref, o_ref, g_ref, y_ref):
        ov = o_ref[...]
        r = lax.rsqrt(jnp.mean(ov * ov, axis=-1, keepdims=True) + EPS)
        y_ref[...] = x_ref[...] + ov * r * g_ref[...]

    return _call(body, name, _sds((t, d)), (t // tm,), [_row_spec(tm, d), _row_spec(tm, d), _acc_spec(1, d)],
                 _row_spec(tm, d))(x, out, gain)


def post_norm_bwd(name, out, gain, dy):
    t, d = out.shape
    tm = _rows(t, light=True)

    def body(o_ref, g_ref, dy_ref, do_ref, dg_ref):
        ov = o_ref[...]
        r = lax.rsqrt(jnp.mean(ov * ov, axis=-1, keepdims=True) + EPS)
        n = ov * r
        dyv = dy_ref[...]
        _accumulate(dg_ref, jnp.sum(dyv * n, axis=0, keepdims=True))
        dng = dyv * g_ref[...]
        do_ref[...] = (r * (dng - n * jnp.mean(dng * n, axis=-1, keepdims=True))).astype(bf16)

    return _call(body, name, (_sds((t, d), bf16), _sds((1, d))), (t // tm,),
                 [_row_spec(tm, d), _acc_spec(1, d), _row_spec(tm, d)],
                 (_row_spec(tm, d), _acc_spec(1, d)))(out, gain, dy)


def loss_head(name, y, target):
    t, d = y.shape
    tm = _rows(t, light=True)

    def body(y_ref, t_ref, l_ref, dy_ref):
        e = y_ref[...] - t_ref[...]
        dy_ref[...] = e * (1.0 / d)
        s = jnp.sum(jnp.sum(e * e, axis=1, keepdims=True), axis=0, keepdims=True) * (0.5 / d)
        _accumulate(l_ref, s)

    return _call(body, name, (_sds((1, 1)), _sds((t, d))), (t // tm,),
                 [_row_spec(tm, d), _row_spec(tm, d)], (_acc_spec(1, 1), _row_spec(tm, d)))(y, target)


def _prev_spec(tm, w):
    return pl.BlockSpec((8, w), lambda i: (jnp.maximum(i * (tm // 8) - 1, 0), 0))


def _next_spec(tm, w, t):
    return pl.BlockSpec((8, w), lambda i: (jnp.minimum((i + 1) * (tm // 8), t // 8 - 1), 0))


def _fill_pad(pad_ref, prev_ref, cur_ref, tm):
    keep = (pl.program_id(0) > 0).astype(f32)
    pad_ref[0:8, :] = prev_ref[...] * keep
    pad_ref[8:8 + tm, :] = cur_ref[...]


def _conv_block(pad_ref, w_ref, cb, tm):
    cs = slice(cb * 128, (cb + 1) * 128)
    acc = pad_ref[pl.ds(8 - (CONV_K - 1), tm), cs] * w_ref[0:1, cs]
    for j in range(1, CONV_K):
        acc = acc + pad_ref[pl.ds(8 - (CONV_K - 1) + j, tm), cs] * w_ref[j:j + 1, cs]
    return acc


def conv_qkv_fwd(name, proj, conv_w, a_log, dt_bias):
    t = proj.shape[0]
    tm = _rows(t)
    scale = HEAD_DIM ** -0.5

    def body(cur_ref, prev_ref, w_ref, bd_ref, al_ref, db_ref, q_ref, k_ref, v_ref, bg_ref, pad_ref):
        _fill_pad(pad_ref, prev_ref, cur_ref, tm)
        for cb in range(3 * DN_HEADS):
            s = _silu(_conv_block(pad_ref, w_ref, cb, tm))
            hs = slice((cb % DN_HEADS) * 128, (cb % DN_HEADS + 1) * 128)
            if cb < DN_HEADS:
                q_ref[:, hs] = s * (lax.rsqrt(jnp.sum(s * s, axis=-1, keepdims=True) + EPS) * scale)
            elif cb < 2 * DN_HEADS:
                k_ref[:, hs] = s * lax.rsqrt(jnp.sum(s * s, axis=-1, keepdims=True) + EPS)
            else:
                v_ref[:, hs] = s
        bd = bd_ref[...]
        beta = _sigmoid(bd)
        g = -jnp.exp(al_ref[...]) * _softplus(bd + db_ref[...])
        lane = lax.broadcasted_iota(jnp.int32, bd.shape, 1)
        bg_ref[...] = jnp.where(lane < DN_HEADS, beta, jnp.where(lane < 2 * DN_HEADS, g, 0.0))

    w3 = 3 * D_MODEL
    return _call(body, name, (_sds((t, D_MODEL)),) * 3 + (_sds((t, 128)),), (t // tm,),
                 [pl.BlockSpec((tm, w3), lambda i: (i, 0)), _prev_spec(tm, w3), _acc_spec(CONV_K, w3),
                  pl.BlockSpec((tm, 128), lambda i: (i, 8192 // 128)), _acc_spec(1, 128), _acc_spec(1, 128)],
                 (_row_spec(tm, D_MODEL),) * 3 + (_row_spec(tm, 128),),
                 scratch=[pltpu.VMEM((tm + 8, w3), f32)])(proj, proj, conv_w, proj, a_log, dt_bias)


def conv_qkv_bwd(name, proj, conv_w, a_log, dt_bias, dq, dk, dv, dbg):
    t = proj.shape[0]
    tm = _rows(t)
    scale = HEAD_DIM ** -0.5

    def body(cur_ref, prev_ref, w_ref, bd_ref, al_ref, db_ref, dq_ref, dk_ref, dv_ref, dbg_ref,
             dc_ref, dw_ref, dbd_ref, dab_ref, pad_ref):
        _fill_pad(pad_ref, prev_ref, cur_ref, tm)

        @pl.when(pl.program_id(0) == 0)
        def _():
            dw_ref[...] = jnp.zeros_like(dw_ref)

        for cb in range(3 * DN_HEADS):
            cs = slice(cb * 128, (cb + 1) * 128)
            hs = slice((cb % DN_HEADS) * 128, (cb % DN_HEADS + 1) * 128)
            taps = [pad_ref[pl.ds(8 - (CONV_K - 1) + j, tm), cs] for j in range(CONV_K)]
            c = taps[0] * w_ref[0:1, cs]
            for j in range(1, CONV_K):
                c = c + taps[j] * w_ref[j:j + 1, cs]
            sg = _sigmoid(c)
            s = c * sg
            if cb < 2 * DN_HEADS:
                dn = (dq_ref[:, hs] * scale) if cb < DN_HEADS else dk_ref[:, hs]
                r = lax.rsqrt(jnp.sum(s * s, axis=-1, keepdims=True) + EPS)
                ds = r * dn - s * (r * r * r) * jnp.sum(dn * s, axis=-1, keepdims=True)
            else:
                ds = dv_ref[:, hs]
            dc = ds * (sg * (1.0 + c * (1.0 - sg)))
            dc_ref[:, cs] = dc
            for j in range(CONV_K):
                dw_ref[j:j + 1, cs] += jnp.sum(dc * taps[j], axis=0, keepdims=True)
        bd = bd_ref[...]
        dbg_v = dbg_ref[...]
        lane = lax.broadcasted_iota(jnp.int32, bd.shape, 1)
        sg = _sigmoid(bd)
        ea = jnp.exp(al_ref[...])
        z = bd + db_ref[...]
        sp = _softplus(z)
        is_b = lane < DN_HEADS
        is_g = jnp.logical_and(lane >= DN_HEADS, lane < 2 * DN_HEADS)
        d_z = jnp.where(is_g, dbg_v * (-ea) * _sigmoid(z), 0.0)
        dbd_ref[...] = jnp.where(is_b, dbg_v * sg * (1.0 - sg), d_z)
        d_al = jnp.sum(jnp.where(is_g, dbg_v * (-ea) * sp, 0.0), axis=0, keepdims=True)
        d_db = jnp.sum(d_z, axis=0, keepdims=True)
        _accumulate(dab_ref, jnp.concatenate([d_al, d_db] + [jnp.zeros_like(d_al)] * 6, axis=0))

    w3 = 3 * D_MODEL
    return _call(body, name, (_sds((t, w3)), _sds((CONV_K, w3)), _sds((t, 128)), _sds((8, 128))), (t // tm,),
                 [pl.BlockSpec((tm, w3), lambda i: (i, 0)), _prev_spec(tm, w3), _acc_spec(CONV_K, w3),
                  pl.BlockSpec((tm, 128), lambda i: (i, 8192 // 128)), _acc_spec(1, 128), _acc_spec(1, 128),
                  _row_spec(tm, D_MODEL), _row_spec(tm, D_MODEL), _row_spec(tm, D_MODEL), _row_spec(tm, 128)],
                 (_row_spec(tm, w3), _acc_spec(CONV_K, w3), _row_spec(tm, 128), _acc_spec(8, 128)),
                 scratch=[pltpu.VMEM((tm + 8, w3), f32)])(proj, proj, conv_w, proj, a_log, dt_bias, dq, dk, dv, dbg)


def _bdg(a, b, ca, cb, prec=None):
    if prec is None:
        a, b = a.astype(bf16), b.astype(bf16)
    return lax.dot_general(a, b, (((ca,), (cb,)), ((0,), (0,))), precision=prec, preferred_element_type=f32)


def _bnn(a, b, prec=None):
    return _bdg(a, b, 2, 1, prec)


def _bnt(a, b, prec=None):
    return _bdg(a, b, 2, 2, prec)


def _btn(a, b, prec=None):
    return _bdg(a, b, 1, 1, prec)


def _delta_local(q, k, v, g_row, b_row, solved=None):
    c = CHUNK
    ii = lax.broadcasted_iota(jnp.int32, (c, c), 0)
    jj = lax.broadcasted_iota(jnp.int32, (c, c), 1)
    eye, lower, strict = ii == jj, ii >= jj, ii > jj
    shp = (q.shape[0], c, c)
    g_b = jnp.broadcast_to(g_row, shp)
    gc_col = jnp.sum(jnp.where(lower, g_b, 0.0), axis=2, keepdims=True)
    gc_row = jnp.sum(jnp.where(eye, jnp.broadcast_to(gc_col, shp), 0.0), axis=1, keepdims=True)
    b_col = jnp.sum(jnp.where(eye, jnp.broadcast_to(b_row, shp), 0.0), axis=2, keepdims=True)
    gl = jnp.sum(g_row, axis=2, keepdims=True)
    decay = jnp.exp(jnp.where(lower, gc_col - gc_row, -1e30))
    kb = k * b_col
    kk = _bnt(kb, k)
    lmat = jnp.where(strict, kk * decay, 0.0)
    egc = jnp.exp(gc_col)
    rhs_w = kb * egc
    if solved is None:
        tinv = eye.astype(f32) - lmat
        pw = lmat
        for _ in range(5):
            pw = _bnn(pw, pw, HIGH)
            tinv = tinv + _bnn(tinv, pw, HIGH)
        uw = _bnn(tinv, jnp.concatenate([v * b_col, rhs_w], axis=2), HIGH)
        u, w = uw[:, :, 0:HEAD_DIM], uw[:, :, HEAD_DIM:]
    else:
        tinv, u, w = solved
    qk = _bnt(q, k)
    amat = jnp.where(lower, qk * decay, 0.0)
    q_dec = q * egc
    kdf = jnp.exp(gl - gc_col)
    k_dec = k * kdf
    return dict(eye=eye, lower=lower, strict=strict, gc_col=gc_col, b_col=b_col, gl=gl, decay=decay,
                kb=kb, kk=kk, tinv=tinv, egc=egc, rhs_w=rhs_w, u=u, w=w, qk=qk, amat=amat, q_dec=q_dec, kdf=kdf,
                k_dec=k_dec)


def _heads(ref, hb):
    return jnp.stack([ref[:, h * HEAD_DIM:(h + 1) * HEAD_DIM] for h in range(hb)])


def _put_heads(ref, val):
    for h in range(val.shape[0]):
        ref[:, h * HEAD_DIM:(h + 1) * HEAD_DIM] = val[h]


def delta_fwd(name, q, k, v, g_rows, b_rows, next_shard=None):
    t = q.shape[0]
    nc = t // CHUNK
    hb = DELTA_HB
    fused = next_shard is not None
    assert not fused or (hb == DN_HEADS and nc >= 2)

    def body(*refs):
        if fused:
            (q_ref, k_ref, v_ref, g_ref, b_ref, x_ref, o_ref, s_ref, t_ref, u_ref, w_ref, gath_ref, state, send_sems,
             recv_sems, local_sem) = refs
        else:
            q_ref, k_ref, v_ref, g_ref, b_ref, o_ref, s_ref, t_ref, u_ref, w_ref, state = refs
        n = pl.program_id(1)
        if fused:
            start, forward, finish = _gather_steps(x_ref, gath_ref, send_sems, recv_sems, local_sem)
            pl.when(n == 0)(start)

        @pl.when(n == 0)
        def _():
            state[...] = jnp.zeros_like(state)

        loc = _delta_local(_heads(q_ref, hb), _heads(k_ref, hb), _heads(v_ref, hb), g_ref[:, pl.ds(n, 1), :],
                           b_ref[:, pl.ds(n, 1), :])
        s0 = state[...]
        s_ref[...] = s0
        t_ref[...] = loc["tinv"]
        _put_heads(u_ref, loc["u"])
        _put_heads(w_ref, loc["w"])
        v_new = loc["u"] - _bnn(loc["w"], s0)
        _put_heads(o_ref, _bnn(loc["q_dec"], s0) + _bnn(loc["amat"], v_new))
        state[...] = s0 * jnp.exp(loc["gl"]) + _btn(loc["k_dec"], v_new)
        if fused:
            pl.when(n == (3 * nc) // 4)(forward)
            pl.when(n == nc - 1)(finish)

    tok = pl.BlockSpec((CHUNK, hb * HEAD_DIM), lambda h, n: (n, h))
    row = pl.BlockSpec((hb, nc, CHUNK), lambda h, n: (h, 0, 0))
    any_spec = pl.BlockSpec(memory_space=pl.ANY)
    outs = (_sds((t, D_MODEL)), _sds((DN_HEADS, nc, HEAD_DIM, HEAD_DIM)), _sds((DN_HEADS, nc, CHUNK, CHUNK)),
            _sds((t, D_MODEL)), _sds((t, D_MODEL)))
    out_specs = (tok, pl.BlockSpec((hb, None, HEAD_DIM, HEAD_DIM), lambda h, n: (h, n, 0, 0)),
                 pl.BlockSpec((hb, None, CHUNK, CHUNK), lambda h, n: (h, n, 0, 0)), tok, tok)
    in_specs, args = [tok, tok, tok, row, row], (q, k, v, g_rows, b_rows)
    scratch = [pltpu.VMEM((hb, HEAD_DIM, HEAD_DIM), f32)]
    if fused:
        outs += (_sds((N_DEV,) + next_shard.shape, next_shard.dtype),)
        out_specs += (any_spec,)
        in_specs, args = in_specs + [any_spec], args + (next_shard,)
        scratch = scratch + list(_GATHER_SEMS)
    return _call(body, name, outs, (DN_HEADS // hb, nc), in_specs, out_specs, scratch=scratch)(*args)


def delta_bwd(name, q, k, v, g_rows, b_rows, s_all, t_all, u_all, w_all, do, chip_part=None):
    t = q.shape[0]
    nc = t // CHUNK
    c = CHUNK
    hb = DELTA_HB

    fused = chip_part is not None
    assert not fused or hb == DN_HEADS

    def body(*refs):
        if fused:
            (q_ref, k_ref, v_ref, g_ref, b_ref, s_ref, t_ref, u_ref, w_ref, do_ref, x_ref, dq_ref, dk_ref, dv_ref,
             dg_ref, db_ref, slot_ref, dstate, send_sems, recv_sems, local_sem) = refs
        else:
            (q_ref, k_ref, v_ref, g_ref, b_ref, s_ref, t_ref, u_ref, w_ref, do_ref, dq_ref, dk_ref, dv_ref, dg_ref,
             db_ref, dstate) = refs
        step = pl.program_id(1)
        n = nc - 1 - step
        if fused:
            start, finish = _chip_exchange_steps(x_ref, slot_ref, send_sems, recv_sems, local_sem)
            pl.when(step == 0)(start)

        @pl.when(step == 0)
        def _():
            dstate[...] = jnp.zeros_like(dstate)

        qv, kv, vv = _heads(q_ref, hb), _heads(k_ref, hb), _heads(v_ref, hb)
        L = _delta_local(qv, kv, vv, g_ref[:, pl.ds(n, 1), :], b_ref[:, pl.ds(n, 1), :],
                         solved=(t_ref[...], _heads(u_ref, hb), _heads(w_ref, hb)))
        eye, lower, strict = L["eye"], L["lower"], L["strict"]
        shp = (hb, c, c)
        s0 = s_ref[...]
        dov = _heads(do_ref, hb)
        ds = dstate[...]
        eg = jnp.exp(L["gl"])
        v_new = L["u"] - _bnn(L["w"], s0)
        d_k_dec = _bnt(v_new, ds)
        d_v_new = _bnn(L["k_dec"], ds) + _btn(L["amat"], dov)
        d_eg = jnp.sum(jnp.sum(ds * s0, axis=2, keepdims=True), axis=1, keepdims=True)
        d_q_dec = _bnt(dov, s0)
        d_a = _bnt(dov, v_new)
        d_w = -_bnt(d_v_new, s0)
        dstate[...] = ds * eg + _btn(L["q_dec"], dov) - _btn(L["w"], d_v_new)
        d_am = jnp.where(lower, d_a * L["decay"], 0.0)
        dq = _bnn(d_am, kv) + d_q_dec * L["egc"]
        dk = _btn(d_am, qv) + d_k_dec * L["kdf"]
        e_col = jnp.sum(d_k_dec * L["k_dec"], axis=2, keepdims=True)
        d_gc_col = jnp.sum(d_q_dec * L["q_dec"], axis=2, keepdims=True) - e_col
        d_gl = jnp.sum(e_col, axis=1, keepdims=True) + d_eg * eg
        tinv = L["tinv"]
        d_rhs = _btn(tinv, jnp.concatenate([d_v_new, d_w], axis=2), HIGH)
        d_rhs_u, d_rhs_w = d_rhs[:, :, 0:HEAD_DIM], d_rhs[:, :, HEAD_DIM:]
        d_l = -_bnt(d_rhs, jnp.concatenate([L["u"], L["w"]], axis=2), HIGH)
        _put_heads(dv_ref, d_rhs_u * L["b_col"])
        d_b_col = jnp.sum(d_rhs_u * vv, axis=2, keepdims=True)
        d_gc_col = d_gc_col + jnp.sum(d_rhs_w * L["rhs_w"], axis=2, keepdims=True)
        d_lm = jnp.where(strict, d_l * L["decay"], 0.0)
        d_kb = d_rhs_w * L["egc"] + _bnn(d_lm, kv)
        dk = dk + _btn(d_lm, L["kb"]) + d_kb * L["b_col"]
        d_b_col = d_b_col + jnp.sum(d_kb * kv, axis=2, keepdims=True)
        m = d_am * L["qk"] + d_lm * L["kk"]
        d_gc_col = d_gc_col + jnp.sum(m, axis=2, keepdims=True)
        d_gc_row = (jnp.sum(jnp.where(eye, jnp.broadcast_to(d_gc_col, shp), 0.0), axis=1, keepdims=True)
                    - jnp.sum(m, axis=1, keepdims=True))
        lane = lax.broadcasted_iota(jnp.int32, (1, 1, c), 2)
        d_gc_row = d_gc_row + jnp.where(lane == c - 1, d_gl, 0.0)
        d_gc_tot = jnp.sum(jnp.where(eye, jnp.broadcast_to(d_gc_row, shp), 0.0), axis=2, keepdims=True)
        dg_ref[:, pl.ds(n, 1), :] = jnp.sum(jnp.where(lower, jnp.broadcast_to(d_gc_tot, shp), 0.0), axis=1,
                                            keepdims=True)
        db_ref[:, pl.ds(n, 1), :] = jnp.sum(jnp.where(eye, jnp.broadcast_to(d_b_col, shp), 0.0), axis=1,
                                            keepdims=True)
        _put_heads(dq_ref, dq)
        _put_heads(dk_ref, dk)
        if fused:
            pl.when(step == nc - 1)(finish)

    tok = pl.BlockSpec((CHUNK, hb * HEAD_DIM), lambda h, s: (nc - 1 - s, h))
    row = pl.BlockSpec((hb, nc, CHUNK), lambda h, s: (h, 0, 0))
    any_spec = pl.BlockSpec(memory_space=pl.ANY)
    in_specs = [tok, tok, tok, row, row,
                pl.BlockSpec((hb, None, HEAD_DIM, HEAD_DIM), lambda h, s: (h, nc - 1 - s, 0, 0)),
                pl.BlockSpec((hb, None, CHUNK, CHUNK), lambda h, s: (h, nc - 1 - s, 0, 0)), tok, tok, tok]
    args = (q, k, v, g_rows, b_rows, s_all, t_all, u_all, w_all, do)
    outs = (_sds((t, D_MODEL)),) * 3 + (_sds((DN_HEADS, nc, CHUNK)),) * 2
    out_specs = (tok, tok, tok, row, row)
    scratch = [pltpu.VMEM((hb, HEAD_DIM, HEAD_DIM), f32)]
    if fused:
        in_specs, args = in_specs + [any_spec], args + (chip_part,)
        outs += (_sds(chip_part.shape, chip_part.dtype),)
        out_specs += (any_spec,)
        scratch = scratch + list(_CHIP_SEMS)
    return _call(body, name, outs, (DN_HEADS // hb, nc), in_specs, out_specs, scratch=scratch)(*args)


SEG = 8


def _perm_rows(a):
    t, c = a.shape
    return a.reshape(SEG, t // SEG, c).transpose(1, 0, 2).reshape(t, c)


def _unperm_rows(a):
    t, c = a.shape
    return a.reshape(t // SEG, SEG, c).transpose(1, 0, 2).reshape(t, c)


def _cmul(ar, ai, br, bi):
    return ar * br - ai * bi, ar * bi + ai * br


def _segment_init(er, ei, lr, li, seg_len, reverse):
    w = er.shape[1]
    sub = lax.broadcasted_iota(jnp.int32, (SEG, w), 0)

    def shift(x, k):
        if reverse:
            return jnp.where(sub < SEG - k, pltpu.roll(x, SEG - k, 0), 0.0)
        return jnp.where(sub >= k, pltpu.roll(x, k, 0), 0.0)

    pr, pi = lr, li
    for _ in range(seg_len.bit_length() - 1):
        pr, pi = _cmul(pr, pi, pr, pi)
    fr, fi = shift(er, 1), shift(ei, 1)
    for k in (1, 2, 4):
        sr, si = shift(fr, k), shift(fi, k)
        mr, mi = _cmul(pr, pi, sr, si)
        fr, fi = fr + mr, fi + mi
        pr, pi = _cmul(pr, pi, pr, pi)
    return fr, fi


def s5_fwd(name, u_perm, bbd, cbd, lam):
    t = u_perm.shape[0]
    tt = _tile(t, (1024, 512, 256, 128))
    nt, ng, w = t // tt, tt // SEG, SG_STATE
    seg_len = t // SEG
    assert seg_len & (seg_len - 1) == 0 and tt % SEG == 0

    def body(u_ref, b_ref, c_ref, lam_ref, y_ref, h_ref, x_scr, state):
        p, i = pl.program_id(1), pl.program_id(2)
        lr1, li1 = lam_ref[:, 0:w], lam_ref[:, w:2 * w]
        lr, li = jnp.broadcast_to(lr1, (SEG, w)), jnp.broadcast_to(li1, (SEG, w))
        base = pl.multiple_of(i * tt, tt)

        @pl.when(p == 0)
        def _():
            x_scr[pl.ds(base, tt), :] = _bdot(u_ref[...], b_ref[...], ((1,), (0,)))

        @pl.when(jnp.logical_and(p == 0, i == 0))
        def _():
            state[...] = jnp.zeros_like(state)

        @pl.when(jnp.logical_and(p == 1, i == 0))
        def _():
            sr, si = _segment_init(state[:, 0:w], state[:, w:2 * w], lr1, li1, seg_len, False)
            state[:, 0:w] = sr
            state[:, w:2 * w] = si

        def run(store):
            def step(g, st):
                row = pl.multiple_of(g * SEG, SEG)
                xg = x_scr[pl.ds(base + row, SEG), :]
                nr = lr * st[0] - li * st[1] + xg[:, 0:w]
                ni = lr * st[1] + li * st[0] + xg[:, w:2 * w]
                if store:
                    h_ref[pl.ds(row, SEG), 0:w] = nr
                    h_ref[pl.ds(row, SEG), w:2 * w] = ni
                return nr, ni

            fin = lax.fori_loop(0, ng, step, (state[:, 0:w], state[:, w:2 * w]))
            state[:, 0:w] = fin[0]
            state[:, w:2 * w] = fin[1]

        @pl.when(p == 0)
        def _():
            run(False)

        @pl.when(p == 1)
        def _():
            run(True)
            y_ref[...] = _bdot(h_ref[...], c_ref[...], ((1,), (0,)))

    return _call(body, name, (_sds((t, D_MODEL)), _sds((t, N_SUPER * 2 * w))), (N_SUPER, 2, nt),
                 [pl.BlockSpec((tt, 128), lambda s, p, i: (i * (1 - p) + (nt - 1) * p, s)),
                  pl.BlockSpec((None, 128, 2 * w), lambda s, p, i: (s, 0, 0)),
                  pl.BlockSpec((None, 2 * w, 128), lambda s, p, i: (s, 0, 0)),
                  pl.BlockSpec((None, 1, 2 * w), lambda s, p, i: (s, 0, 0))],
                 (pl.BlockSpec((tt, 128), lambda s, p, i: (i * p, s)),
                  pl.BlockSpec((tt, 2 * w), lambda s, p, i: (i * p, s))),
                 scratch=[pltpu.VMEM((t, 2 * w), f32), pltpu.VMEM((SEG, 2 * w), f32)])(u_perm, bbd, cbd, lam)


def s5_bwd(name, dy_perm, u_perm, h_perm, hprev0, bbd, cbd, lam):
    t = u_perm.shape[0]
    tt = _tile(t, (1024, 512, 256, 128))
    nt, ng, w = t // tt, tt // SEG, SG_STATE
    seg_len = t // SEG

    def body(dy_ref, u_ref, h_ref, hp_ref, hp0_ref, b_ref, c_ref, lam_ref, du_ref, db_ref, dc_ref, dl_ref,
             g_scr, state, dl_acc):
        p, i = pl.program_id(1), pl.program_id(2)
        first_tile = jnp.logical_or(p == 0, i == nt - 1)
        lr1, li1 = lam_ref[:, 0:w], -lam_ref[:, w:2 * w]
        lr, li = jnp.broadcast_to(lr1, (SEG, w)), jnp.broadcast_to(li1, (SEG, w))
        base = pl.multiple_of((nt - 1 - i) * tt, tt)

        @pl.when(p == 0)
        def _():
            g_scr[pl.ds(base, tt), :] = _bdot(dy_ref[...], c_ref[...], ((1,), (1,)))

        @pl.when(jnp.logical_and(p == 0, i == 0))
        def _():
            state[...] = jnp.zeros_like(state)

        @pl.when(jnp.logical_and(p == 1, i == 0))
        def _():
            sr, si = _segment_init(state[:, 0:w], state[:, w:2 * w], lr1, li1, seg_len, True)
            state[:, 0:w] = sr
            state[:, w:2 * w] = si
            dl_acc[...] = jnp.zeros_like(dl_acc)

        def adj(g, st):
            row = pl.multiple_of(g * SEG, SEG)
            gg = g_scr[pl.ds(base + row, SEG), :]
            nr = lr * st[0] - li * st[1] + gg[:, 0:w]
            ni = lr * st[1] + li * st[0] + gg[:, w:2 * w]
            return row, nr, ni

        @pl.when(p == 0)
        def _():
            def step(k, st):
                _, nr, ni = adj(ng - 1 - k, st)
                return nr, ni

            fin = lax.fori_loop(0, ng, step, (state[:, 0:w], state[:, w:2 * w]))
            state[:, 0:w] = fin[0]
            state[:, w:2 * w] = fin[1]

        @pl.when(p == 1)
        def _():
            above = jnp.where(first_tile, hp0_ref[...], hp_ref[...])

            def step(k, st):
                g = ng - 1 - k
                row, nr, ni = adj(g, st)
                g_scr[pl.ds(base + row, SEG), 0:w] = nr
                g_scr[pl.ds(base + row, SEG), w:2 * w] = ni
                prow = pl.multiple_of(jnp.maximum(g - 1, 0) * SEG, SEG)
                hp = jnp.where(g > 0, h_ref[pl.ds(prow, SEG), :], above)
                pr, pi = hp[:, 0:w], hp[:, w:2 * w]
                return nr, ni, st[2] + nr * pr + ni * pi, st[3] + ni * pr - nr * pi

            fin = lax.fori_loop(0, ng, step, (state[:, 0:w], state[:, w:2 * w], dl_acc[:, 0:w], dl_acc[:, w:2 * w]))
            state[:, 0:w] = fin[0]
            state[:, w:2 * w] = fin[1]
            dl_acc[:, 0:w] = fin[2]
            dl_acc[:, w:2 * w] = fin[3]
            a = g_scr[pl.ds(base, tt), :]
            du_ref[...] = _bdot(a, b_ref[...], ((1,), (1,)))
            d_b = _bdot(u_ref[...], a, ((0,), (0,)))
            d_c = _bdot(dy_ref[...], h_ref[...], ((0,), (0,)))

            @pl.when(i == 0)
            def _():
                db_ref[...] = d_b
                dc_ref[...] = d_c

            @pl.when(i > 0)
            def _():
                db_ref[...] += d_b
                dc_ref[...] += d_c

            @pl.when(i == nt - 1)
            def _():
                dl_ref[...] = jnp.sum(dl_acc[...], axis=0, keepdims=True)

    tile = lambda s, p, i: (nt - 1 - i, s)
    tile1 = lambda s, p, i: (nt - 1 - i * p, s)
    above = lambda s, p, i: (jnp.maximum((nt - 1 - i * p) * (tt // SEG) - 1, 0), s)
    per_s = lambda s, p, i: (s, 0, 0)
    return _call(body, name, (_sds((t, D_MODEL)), _sds((N_SUPER, 128, 2 * w)), _sds((N_SUPER, 128, 2 * w)),
                              _sds((N_SUPER, 1, 2 * w))), (N_SUPER, 2, nt),
                 [pl.BlockSpec((tt, 128), tile), pl.BlockSpec((tt, 128), tile1), pl.BlockSpec((tt, 2 * w), tile1),
                  pl.BlockSpec((SEG, 2 * w), above), pl.BlockSpec((SEG, 2 * w), lambda s, p, i: (0, s)),
                  pl.BlockSpec((None, 128, 2 * w), per_s), pl.BlockSpec((None, 2 * w, 128), per_s),
                  pl.BlockSpec((None, 1, 2 * w), per_s)],
                 (pl.BlockSpec((tt, 128), tile1), pl.BlockSpec((None, 128, 2 * w), per_s),
                  pl.BlockSpec((None, 128, 2 * w), per_s), pl.BlockSpec((None, 1, 2 * w), per_s)),
                 scratch=[pltpu.VMEM((t, 2 * w), f32), pltpu.VMEM((SEG, 2 * w), f32),
                          pltpu.VMEM((SEG, 2 * w), f32)])(dy_perm, u_perm, h_perm, h_perm, hprev0, bbd, cbd, lam)


def _s5_disc(a_re, a_im, log_dt, br, bi):
    dt = jnp.exp(log_dt)
    mag = jnp.exp(a_re * dt)
    lr, li = mag * jnp.cos(a_im * dt), mag * jnp.sin(a_im * dt)
    den = a_re * a_re + a_im * a_im
    fr = ((lr - 1.0) * a_re + li * a_im) / den
    fi = (li * a_re - (lr - 1.0) * a_im) / den
    return lr, li, fr * br - fi * bi, fr * bi + fi * br


def s5_disc_fwd(name, a_re, a_im, log_dt, br, bi):
    g, n = SSM_GROUPS, SSM_STATE

    def body(ar, ai, ld, brr, bir, lr, li, bbr, bbi):
        o = _s5_disc(ar[...], ai[...], ld[...], brr[...], bir[...])
        lr[...], li[...], bbr[...], bbi[...] = o

    return _call(body, name, (_sds((g, 1, n)), _sds((g, 1, n)), _sds((g, SSM_GROUP, n)), _sds((g, SSM_GROUP, n))))(
        a_re, a_im, log_dt, br, bi)


def s5_disc_bwd(name, a_re, a_im, log_dt, br, bi, d_lr, d_li, d_bbr, d_bbi):
    g, n = SSM_GROUPS, SSM_STATE

    def body(ar, ai, ld, brr, bir, c1, c2, c3, c4, o1, o2, o3, o4, o5):
        _, vjp = jax.vjp(_s5_disc, ar[...], ai[...], ld[...], brr[...], bir[...])
        o1[...], o2[...], o3[...], o4[...], o5[...] = vjp((c1[...], c2[...], c3[...], c4[...]))

    return _call(body, name, (_sds((g, 1, n)), _sds((g, 1, n)), _sds((g, 1, 1)), _sds((g, SSM_GROUP, n)),
                              _sds((g, SSM_GROUP, n))))(a_re, a_im, log_dt, br, bi, d_lr, d_li, d_bbr, d_bbi)


def gelu_fwd(name, ys_lin, proj, d_skip):
    t, d = ys_lin.shape
    tm = _rows(t, light=True)

    def body(y_ref, u_ref, d_ref, o_ref):
        o_ref[...] = _gelu(y_ref[...] + d_ref[...] * u_ref[...])

    return _call(body, name, _sds((t, d)), (t // tm,),
                 [_row_spec(tm, d), pl.BlockSpec((tm, d), lambda i: (i, 4)), _acc_spec(1, d)],
                 _row_spec(tm, d))(ys_lin, proj, d_skip)


def _head_norm(o, hn):
    outs, ns, rs = [], [], []
    for h in range(DN_HEADS):
        oh = o[:, h * 128:(h + 1) * 128]
        r = lax.rsqrt(jnp.mean(oh * oh, axis=-1, keepdims=True) + EPS)
        n = oh * r
        ns.append(n)
        rs.append(r)
        outs.append(n * hn)
    return outs, ns, rs


def merge_fwd(name, proj, o, yg, glu_lin, head_norm, b_glu):
    t = o.shape[0]
    tm = _rows(t)
    d = D_MODEL

    def body(za_ref, zb_ref, ra_ref, rb_ref, o_ref, yg_ref, gl_ref, hn_ref, bg_ref, m_ref):
        on, _, _ = _head_norm(o_ref[...], hn_ref[...])
        y_a = jnp.concatenate(on, axis=1) * _silu(za_ref[...])
        y_b = yg_ref[...] * _sigmoid(gl_ref[...] + bg_ref[...]) * _silu(zb_ref[...])
        m_ref[...] = (_sigmoid(ra_ref[...]) * y_a + _sigmoid(rb_ref[...]) * y_b).astype(bf16)

    pc = lambda c: pl.BlockSpec((tm, d), lambda i: (i, c))
    return _call(body, name, _sds((t, d), bf16), (t // tm,),
                 [pc(3), pc(5), pc(6), pc(7), _row_spec(tm, d), _row_spec(tm, d), _row_spec(tm, d),
                  _acc_spec(1, 128), _acc_spec(1, d)], _row_spec(tm, d))(
        proj, proj, proj, proj, o, yg, glu_lin, head_norm, b_glu)


def merge_bwd(name, proj, o, yg, glu_lin, head_norm, b_glu, dm):
    t = o.shape[0]
    tm = _rows(t)
    d = D_MODEL

    def body(za_ref, zb_ref, ra_ref, rb_ref, o_ref, yg_ref, gl_ref, hn_ref, bg_ref, dm_ref,
             dza_ref, dzb_ref, dra_ref, drb_ref, do_ref, dgl_ref, dyg_ref, dhn_ref, dbg_ref):
        hn = hn_ref[...]
        za, zb, ra, rb = za_ref[...], zb_ref[...], ra_ref[...], rb_ref[...]
        on, ns, rs = _head_norm(o_ref[...], hn)
        onc = jnp.concatenate(on, axis=1)
        sga, sgb = _sigmoid(za), _sigmoid(zb)
        sza = za * sga
        y_a = onc * sza
        yg = yg_ref[...]
        sgl = _sigmoid(gl_ref[...] + bg_ref[...])
        y2 = yg * sgl
        szb = zb * sgb
        y_b = y2 * szb
        sra, srb = _sigmoid(ra), _sigmoid(rb)
        dmv = dm_ref[...]
        dra_ref[...] = (dmv * y_a * sra * (1.0 - sra)).astype(bf16)
        drb_ref[...] = (dmv * y_b * srb * (1.0 - srb)).astype(bf16)
        d_ya = dmv * sra
        d_yb = dmv * srb
        dza_ref[...] = (d_ya * onc * (sga * (1.0 + za * (1.0 - sga)))).astype(bf16)
        dzb_ref[...] = (d_yb * y2 * (sgb * (1.0 + zb * (1.0 - sgb)))).astype(bf16)
        d_on = d_ya * sza
        d_y2 = d_yb * szb
        dyg_ref[...] = d_y2 * sgl
        d_gl = d_y2 * yg * sgl * (1.0 - sgl)
        dgl_ref[...] = d_gl.astype(bf16)
        _accumulate(dbg_ref, jnp.sum(d_gl, axis=0, keepdims=True))
        d_hn = jnp.zeros((1, 128), f32)
        for h in range(DN_HEADS):
            hs = slice(h * 128, (h + 1) * 128)
            dh = d_on[:, hs]
            d_hn = d_hn + jnp.sum(dh * ns[h], axis=0, keepdims=True)
            dn = dh * hn
            do_ref[:, hs] = rs[h] * (dn - ns[h] * jnp.mean(dn * ns[h], axis=-1, keepdims=True))
        _accumulate(dhn_ref, d_hn)

    pc = lambda c: pl.BlockSpec((tm, d), lambda i: (i, c))
    rs_ = _row_spec(tm, d)
    return _call(body, name, (_sds((t, d), bf16),) * 4 + (_sds((t, d)), _sds((t, d), bf16), _sds((t, d)),
                              _sds((1, 128)), _sds((1, d))),
                 (t // tm,), [pc(3), pc(5), pc(6), pc(7), rs_, rs_, rs_, _acc_spec(1, 128), _acc_spec(1, d), rs_],
                 (rs_,) * 7 + (_acc_spec(1, 128), _acc_spec(1, d)))(
        proj, proj, proj, proj, o, yg, glu_lin, head_norm, b_glu, dm)


def gelu_bwd(name, ys_lin, proj, d_skip, dyg_a, dyg_b):
    t, d = ys_lin.shape
    tm = _rows(t, light=True)

    def body(y_ref, u_ref, d_ref, a_ref, b_ref, dys_ref, du_ref, dd_ref):
        uv = u_ref[...]
        dys = (a_ref[...] + b_ref[...]) * _dgelu(y_ref[...] + d_ref[...] * uv)
        dys_ref[...] = dys
        du_ref[...] = dys * d_ref[...]
        _accumulate(dd_ref, jnp.sum(dys * uv, axis=0, keepdims=True))

    rs_ = _row_spec(tm, d)
    return _call(body, name, (_sds((t, d)), _sds((t, d)), _sds((1, d))), (t // tm,),
                 [rs_, pl.BlockSpec((tm, d), lambda i: (i, 4)), _acc_spec(1, d), rs_, rs_],
                 (rs_, rs_, _acc_spec(1, d)))(ys_lin, proj, d_skip, dyg_a, dyg_b)


def assemble_dproj(name, dc, conv_w, dza, du_a, du_b, dzb, dra, drb, dbd):
    t = dza.shape[0]
    tm = _rows(t)
    nt = t // tm
    d = D_MODEL
    w3 = 3 * d

    def body(cur_ref, nxt_ref, w_ref, za_ref, ua_ref, ub_ref, zb_ref, ra_ref, rb_ref, bd_ref, o_ref, pad_ref):
        keep = (pl.program_id(0) < nt - 1).astype(f32)
        pad_ref[0:tm, :] = cur_ref[...]
        pad_ref[tm:tm + 8, :] = nxt_ref[...] * keep
        for cb in range(w3 // 128):
            cs = slice(cb * 128, (cb + 1) * 128)
            acc = pad_ref[pl.ds(CONV_K - 1, tm), cs] * w_ref[0:1, cs]
            for j in range(1, CONV_K):
                acc = acc + pad_ref[pl.ds(CONV_K - 1 - j, tm), cs] * w_ref[j:j + 1, cs]
            o_ref[:, cs] = acc.astype(bf16)
        o_ref[:, 3 * d:4 * d] = za_ref[...].astype(bf16)
        o_ref[:, 4 * d:5 * d] = (ua_ref[...] + ub_ref[...]).astype(bf16)
        o_ref[:, 5 * d:6 * d] = zb_ref[...].astype(bf16)
        o_ref[:, 6 * d:7 * d] = ra_ref[...].astype(bf16)
        o_ref[:, 7 * d:8 * d] = rb_ref[...].astype(bf16)
        o_ref[:, 8 * d:8 * d + 128] = bd_ref[...].astype(bf16)
        o_ref[:, 8 * d + 128:W_PAD] = jnp.zeros((tm, W_PAD - 8 * d - 128), bf16)

    rs_ = _row_spec(tm, d)
    return _call(body, name, _sds((t, W_PAD), bf16), (nt,),
                 [_row_spec(tm, w3), _next_spec(tm, w3, t), _acc_spec(CONV_K, w3), rs_, rs_, rs_, rs_, rs_, rs_,
                  _row_spec(tm, 128)], _row_spec(tm, W_PAD),
                 scratch=[pltpu.VMEM((tm + 8, w3), f32)])(dc, dc, conv_w, dza, du_a, du_b, dzb, dra, drb, dbd)


def adamw(name, w, g, m, v):
    lead, (r, c) = w.shape[:-2], w.shape[-2:]
    tm = _tile(r, (512, 256, 128, 64, 32, 16, 8))
    c1 = 1.0 / (1.0 - ADAM_B1 ** ADAM_STEP)
    c2 = 1.0 / (1.0 - ADAM_B2 ** ADAM_STEP)

    def body(w_ref, g_ref, m_ref, v_ref, d_ref, nm_ref, nv_ref):
        gv = g_ref[...]
        nm = ADAM_B1 * m_ref[...] + (1.0 - ADAM_B1) * gv
        nv = ADAM_B2 * v_ref[...] + (1.0 - ADAM_B2) * (gv * gv)
        d_ref[...] = -ADAM_LR * ((nm * c1) / (jnp.sqrt(nv * c2) + ADAM_EPS) + ADAM_WD * w_ref[...])
        nm_ref[...] = nm
        nv_ref[...] = nv

    if lead:
        sp = pl.BlockSpec((None, tm, c), lambda l, i: (l, i, 0))
        grid = (lead[0], r // tm)
    else:
        sp = pl.BlockSpec((tm, c), lambda i: (i, 0))
        grid = (r // tm,)
    return _call(body, name, (_sds(w.shape),) * 3, grid, [sp] * 4, (sp,) * 3)(w, g, m, v)


def _coords():
    return lax.axis_index("x"), lax.axis_index("y"), lax.axis_index("c")


def _lin(dev):
    return 4 * dev[0] + 2 * dev[1] + dev[2]


def _chips(me):
    x, y, _ = me
    return [(1 - x, y), (x, 1 - y), (1 - x, 1 - y)]


def _gather_steps(x_ref, o_ref, send_sems, recv_sems, local_sem):
    me = _coords()
    x, y, cc = me
    sibling = (x, y, 1 - cc)
    chips = _chips(me)

    def copy(k, block, to, src=None):
        return pltpu.make_async_remote_copy(
            src_ref=o_ref.at[_lin(block)] if src is None else src, dst_ref=o_ref.at[_lin(block)],
            send_sem=send_sems.at[k], recv_sem=recv_sems.at[k], device_id=to, device_id_type=MESH)

    mine = pltpu.make_async_copy(x_ref, o_ref.at[_lin(me)], local_sem)
    first = [copy(0, me, sibling, src=x_ref)] + [copy(1 + j, me, (*chip, cc), src=x_ref)
                                                 for j, chip in enumerate(chips)]
    passed = [copy(4 + j, (*chip, cc), sibling) for j, chip in enumerate(chips)]

    def start():
        mine.start()
        for cp in first:
            cp.start()

    def forward():
        for j, chip in enumerate(chips):
            copy(1 + j, (*chip, cc), me).wait_recv()
            passed[j].start()

    def finish():
        copy(0, sibling, me).wait_recv()
        for j, chip in enumerate(chips):
            copy(4 + j, (*chip, 1 - cc), me).wait_recv()
        for cp in first + passed:
            cp.wait_send()
        mine.wait()

    return start, forward, finish


_GATHER_SEMS = [pltpu.SemaphoreType.DMA((N_DEV - 1,)), pltpu.SemaphoreType.DMA((N_DEV - 1,)),
                pltpu.SemaphoreType.DMA(())]


def all_gather(name, shard):
    r, c = shard.shape

    def body(x_ref, o_ref, send_sems, recv_sems, local_sem):
        for step in _gather_steps(x_ref, o_ref, send_sems, recv_sems, local_sem):
            step()

    any_spec = pl.BlockSpec(memory_space=pl.ANY)
    return _call(body, name, _sds((N_DEV, r, c), shard.dtype), in_specs=[any_spec], out_specs=any_spec,
                 scratch=list(_GATHER_SEMS))(shard)


def pair_exchange(name, blocks):
    _, _, r, c = blocks.shape

    def body(x_ref, o_ref, send_sems, recv_sems):
        x, y, cc = _coords()
        sibling = (x, y, 1 - cc)
        cps = [pltpu.make_async_remote_copy(src_ref=x_ref.at[ch, 1 - cc], dst_ref=o_ref.at[ch], send_sem=send_sems.at[ch],
                                            recv_sem=recv_sems.at[ch], device_id=sibling, device_id_type=MESH)
               for ch in range(4)]
        for cp in cps:
            cp.start()
        for cp in cps:
            cp.wait()

    any_spec = pl.BlockSpec(memory_space=pl.ANY)
    return _call(body, name, _sds((4, r, c), blocks.dtype), in_specs=[any_spec], out_specs=any_spec,
                 scratch=[pltpu.SemaphoreType.DMA((4,)), pltpu.SemaphoreType.DMA((4,))])(blocks)


def _chip_exchange_steps(x_ref, o_ref, send_sems, recv_sems, local_sem):
    me = _coords()
    x, y, cc = me
    my_chip = 2 * x + y
    mine = pltpu.make_async_copy(x_ref.at[my_chip], o_ref.at[my_chip], local_sem)
    peers = [(px, py, cc) for px, py in _chips(me)]
    sends = [pltpu.make_async_remote_copy(src_ref=x_ref.at[2 * px + py], dst_ref=o_ref.at[my_chip],
                                          send_sem=send_sems.at[j], recv_sem=recv_sems.at[j], device_id=(px, py, pc),
                                          device_id_type=MESH) for j, (px, py, pc) in enumerate(peers)]

    def start():
        mine.start()
        for cp in sends:
            cp.start()

    def finish():
        for j, (px, py, pc) in enumerate(peers):
            pltpu.make_async_remote_copy(src_ref=x_ref.at[my_chip], dst_ref=o_ref.at[2 * px + py],
                                         send_sem=send_sems.at[j], recv_sem=recv_sems.at[j], device_id=(px, py, pc),
                                         device_id_type=MESH).wait_recv()
        for cp in sends:
            cp.wait_send()
        mine.wait()

    return start, finish


_CHIP_SEMS = [pltpu.SemaphoreType.DMA((3,)), pltpu.SemaphoreType.DMA((3,)), pltpu.SemaphoreType.DMA(())]


def chip_exchange(name, blocks):
    def body(x_ref, o_ref, send_sems, recv_sems, local_sem):
        for step in _chip_exchange_steps(x_ref, o_ref, send_sems, recv_sems, local_sem):
            step()

    any_spec = pl.BlockSpec(memory_space=pl.ANY)
    return _call(body, name, _sds(blocks.shape, blocks.dtype), in_specs=[any_spec], out_specs=any_spec,
                 scratch=list(_CHIP_SEMS))(blocks)


def pair_sum(name, mine, theirs):
    _, r, c = mine.shape
    tm = max(d for d in range(16, 769, 16) if r % d == 0)

    def body(a_ref, b_ref, o_ref):
        o_ref[...] = (a_ref[...].astype(f32) + b_ref[...].astype(f32)).astype(o_ref.dtype)

    sp = pl.BlockSpec((None, tm, c), lambda ch, i: (ch, i, 0))
    return _call(body, name, _sds(mine.shape, mine.dtype), (4, r // tm), [sp, sp], sp)(mine, theirs)


def sum_slots(name, slots):
    n, r, c = slots.shape
    tm = max(d for d in range(16, 769, 16) if r % d == 0)

    def body(s_ref, o_ref):
        acc = s_ref[0].astype(f32)
        for d in range(1, n):
            acc = acc + s_ref[d].astype(f32)
        o_ref[...] = acc

    return _call(body, name, _sds((r, c)), (r // tm,), [pl.BlockSpec((n, tm, c), lambda i: (0, i, 0))],
                 pl.BlockSpec((tm, c), lambda i: (i, 0)))(slots)


def _rows_layout(col8, t):
    return col8.T.reshape(DN_HEADS, t // CHUNK, CHUNK)


def _blockdiag(m):
    g, a, b = m.shape
    m = m.reshape(N_SUPER, SUPER, a, b)
    out = jnp.einsum("sgab,gh->sgahb", m, jnp.eye(SUPER, dtype=m.dtype))
    return out.reshape(N_SUPER, SUPER * a, SUPER * b)


def _diag_blocks(m, a, b):
    m = m.reshape(N_SUPER, SUPER, a, SUPER, b)
    return jnp.einsum("sgahb,gh->sgab", m, jnp.eye(SUPER, dtype=m.dtype)).reshape(SSM_GROUPS, a, b)


def _s5_params(p, li):
    tag = f"l{li}"
    n = SSM_STATE
    a_re = p["ssm_a_re"].reshape(SSM_GROUPS, 1, n)
    a_im = p["ssm_a_im"].reshape(SSM_GROUPS, 1, n)
    log_dt = p["ssm_log_dt"].reshape(SSM_GROUPS, 1, 1)
    br = jnp.swapaxes(p["ssm_b_re"], 1, 2)
    bi = jnp.swapaxes(p["ssm_b_im"], 1, 2)
    lr, li_, bbr, bbi = s5_disc_fwd("s5_disc_" + tag, a_re, a_im, log_dt, br, bi)
    lam = jnp.concatenate([lr.reshape(N_SUPER, 1, SG_STATE), li_.reshape(N_SUPER, 1, SG_STATE)], axis=-1)
    bbd = jnp.concatenate([_blockdiag(bbr), _blockdiag(bbi)], axis=-1).astype(bf16)
    c_re = jnp.swapaxes(p["ssm_c_re"], 1, 2)
    c_im = jnp.swapaxes(p["ssm_c_im"], 1, 2)
    cbd = jnp.concatenate([_blockdiag(c_re), -_blockdiag(c_im)], axis=1).astype(bf16)
    return dict(a_re=a_re, a_im=a_im, log_dt=log_dt, br=br, bi=bi, lam=lam, bbd=bbd, cbd=cbd)


def layer_fwd(x, p, li, next_shard=None):
    tag = f"l{li}"
    t = x.shape[0]
    d = D_MODEL
    h = rmsnorm_fwd("norm_pre_" + tag, x, p["norm_pre"])
    proj = mm_nn("proj_" + tag, h, p["w_all"])
    q, k, v, bg = conv_qkv_fwd("conv_" + tag, proj, p["conv_w"], p["a_log"], p["dt_bias"])
    b_rows = _rows_layout(bg[:, 0:DN_HEADS], t)
    g_rows = _rows_layout(bg[:, DN_HEADS:2 * DN_HEADS], t)
    o, s_all, t_all, u_all, w_sol, *gathered = delta_fwd("delta_" + tag, q, k, v, g_rows, b_rows, next_shard)
    sp = _s5_params(p, li)
    u_perm = _perm_rows(proj[:, 4 * d:5 * d])
    ys_perm, hs = s5_fwd("s5_" + tag, u_perm, sp["bbd"], sp["cbd"], sp["lam"])
    ys_lin = _unperm_rows(ys_perm)
    yg = gelu_fwd("gelu_" + tag, ys_lin, proj, p["ssm_d"])
    glu_lin = mm_nn("glu_" + tag, yg, p["w_glu"])
    merged = merge_fwd("merge_" + tag, proj, o, yg, glu_lin, p["head_norm"], p["b_glu"])
    out = mm_nn("out_" + tag, merged, p["w_out"])
    y = residual_norm_fwd("norm_post_" + tag, x, out, p["norm_post"])
    saved = dict(x=x, h=h, proj=proj, q=q, k=k, v=v, g_rows=g_rows, b_rows=b_rows, o=o, s_all=s_all, t_all=t_all, u_all=u_all, w_sol=w_sol, sp=sp, u_perm=u_perm,
                 hs=hs, ys_lin=ys_lin, yg=yg, glu_lin=glu_lin, merged=merged, out=out)
    return y, saved, (gathered[0] if gathered else None)


def layer_bwd(dy, p, s, li, chip_part=None):
    tag = f"l{li}"
    t = dy.shape[0]
    d = D_MODEL
    sp = s["sp"]
    gr = {}
    d_out, gr["norm_post"] = post_norm_bwd("norm_post_b_" + tag, s["out"], p["norm_post"], dy)
    d_merged = mm_nt("out_b_" + tag, d_out, p["w_out"])
    gr["w_out"] = mm_tn("out_w_" + tag, s["merged"], d_out, bf16)
    (dza, dzb, dra, drb, d_o, d_glu, dyg_a, gr["head_norm"], gr["b_glu"]) = merge_bwd(
        "merge_b_" + tag, s["proj"], s["o"], s["yg"], s["glu_lin"], p["head_norm"], p["b_glu"], d_merged)
    dyg_b = mm_nt("glu_b_" + tag, d_glu, p["w_glu"])
    gr["w_glu"] = mm_tn("glu_w_" + tag, s["yg"], d_glu, bf16)
    d_ys, du_a, gr["ssm_d"] = gelu_bwd("gelu_b_" + tag, s["ys_lin"], s["proj"], p["ssm_d"], dyg_a, dyg_b)
    hprev0 = jnp.concatenate([jnp.zeros((1, s["hs"].shape[1]), f32), s["hs"][-SEG:-1]], axis=0)
    du_perm, d_bbd, d_cbd, d_lam = s5_bwd("s5_b_" + tag, _perm_rows(d_ys), s["u_perm"], s["hs"], hprev0, sp["bbd"],
                                          sp["cbd"], sp["lam"])
    du_b = _unperm_rows(du_perm)
    gr["ssm_c_re"] = _diag_blocks(d_cbd[:, :, 0:SG_STATE], SSM_GROUP, SSM_STATE)
    gr["ssm_c_im"] = -_diag_blocks(d_cbd[:, :, SG_STATE:], SSM_GROUP, SSM_STATE)
    d_bbr = _diag_blocks(d_bbd[:, :, 0:SG_STATE], SSM_GROUP, SSM_STATE)
    d_bbi = _diag_blocks(d_bbd[:, :, SG_STATE:], SSM_GROUP, SSM_STATE)
    d_lr = d_lam[:, :, 0:SG_STATE].reshape(SSM_GROUPS, 1, SSM_STATE)
    d_li = d_lam[:, :, SG_STATE:].reshape(SSM_GROUPS, 1, SSM_STATE)
    d_are, d_aim, d_ldt, d_br, d_bi = s5_disc_bwd("s5_disc_b_" + tag, sp["a_re"], sp["a_im"], sp["log_dt"], sp["br"],
                                                  sp["bi"], d_lr, d_li, d_bbr, d_bbi)
    gr["ssm_a_re"] = d_are.reshape(SSM_GROUPS, SSM_STATE)
    gr["ssm_a_im"] = d_aim.reshape(SSM_GROUPS, SSM_STATE)
    gr["ssm_log_dt"] = d_ldt.reshape(SSM_GROUPS)
    gr["ssm_b_re"] = jnp.swapaxes(d_br, 1, 2)
    gr["ssm_b_im"] = jnp.swapaxes(d_bi, 1, 2)
    dq, dk, dv, dg_rows, db_rows, *slots = delta_bwd("delta_b_" + tag, s["q"], s["k"], s["v"], s["g_rows"],
                                                     s["b_rows"], s["s_all"], s["t_all"], s["u_all"], s["w_sol"], d_o,
                                                     chip_part)
    dbg = jnp.concatenate([db_rows.reshape(DN_HEADS, t).T, dg_rows.reshape(DN_HEADS, t).T,
                           jnp.zeros((t, 128 - 2 * DN_HEADS), f32)], axis=1)
    dc, gr["conv_w"], dbd, dab = conv_qkv_bwd("conv_b_" + tag, s["proj"], p["conv_w"], p["a_log"], p["dt_bias"],
                                              dq, dk, dv, dbg)
    gr["a_log"] = dab[0, DN_HEADS:2 * DN_HEADS]
    gr["dt_bias"] = dab[1, DN_HEADS:2 * DN_HEADS]
    dproj = assemble_dproj("dproj_" + tag, dc, p["conv_w"], dza, du_a, du_b, dzb, dra, drb, dbd)
    d_h = mm_nt("proj_b_" + tag, dproj, p["w_all"])
    gr["w_all"] = mm_nn("proj_w_" + tag, s["h"].T, dproj, bf16)
    dx, gr["norm_pre"] = rmsnorm_bwd("norm_pre_b_" + tag, s["x"], p["norm_pre"], d_h, dy)
    return dx, gr, (slots[0] if slots else None)


REPL = ["norm_pre", "a_log", "dt_bias", "head_norm", "ssm_a_re", "ssm_a_im", "ssm_log_dt", "ssm_b_re", "ssm_b_im",
        "ssm_c_re", "ssm_c_im", "ssm_d", "b_glu", "norm_post"]
SHARDED = ["w_in", "conv_w", "w_glu", "w_out"]
ALL_W = ["norm_pre", "w_in", "conv_w", "a_log", "dt_bias", "head_norm", "ssm_a_re", "ssm_a_im", "ssm_log_dt",
         "ssm_b_re", "ssm_b_im", "ssm_c_re", "ssm_c_im", "ssm_d", "w_glu", "b_glu", "w_out", "norm_post"]
PACK_W = 1024


def _pack_flat(arrs, rows):
    flat = jnp.concatenate([a.reshape(-1) for a in arrs])
    return jnp.pad(flat, (0, rows * PACK_W - flat.shape[0])).reshape(rows, PACK_W)


def _flat_rows(arrs, mult=8):
    n = sum(math.prod(a.shape) for a in arrs)
    rows = -(-n // PACK_W)
    return -(-rows // mult) * mult


def _unpack(flat, shapes):
    out, off = [], 0
    for sh in shapes:
        n = math.prod(sh)
        out.append(flat[off:off + n].reshape(sh))
        off += n
    return out


def _repl_split(shapes):
    big = [n for n in REPL if math.prod(shapes[n]) % PACK_W == 0]
    small = [n for n in REPL if n not in big]
    return big, small


def _rows8(n):
    return -(-n // (8 * PACK_W)) * 8


def _repl_rows(shapes):
    big, small = _repl_split(shapes)
    rows = sum(_rows8(math.prod(shapes[n])) for n in big) + _flat_rows([_sds(shapes[n]) for n in small], 8)
    return -(-rows // (16 * N_DEV)) * (16 * N_DEV)


def _repl_pack(arrs, shapes):
    big, small = _repl_split(shapes)
    parts = []
    for n in big:
        a = arrs[n].reshape(-1, PACK_W)
        parts.append(jnp.pad(a, ((0, _rows8(a.size) - a.shape[0]), (0, 0))))
    parts.append(_pack_flat([arrs[n] for n in small], _flat_rows([_sds(shapes[n]) for n in small], 8)))
    used = sum(p.shape[0] for p in parts)
    parts.append(jnp.zeros((_repl_rows(shapes) - used, PACK_W), parts[0].dtype))
    return jnp.concatenate(parts, axis=0)


def _repl_unpack(packed, shapes):
    big, small = _repl_split(shapes)
    out, off = {}, 0
    for n in big:
        size = math.prod(shapes[n])
        out[n] = packed[off:off + size // PACK_W].reshape(shapes[n])
        off += _rows8(size)
    srows = _flat_rows([_sds(shapes[n]) for n in small], 8)
    out.update(zip(small, _unpack(packed[off:off + srows].reshape(-1), [shapes[n] for n in small])))
    return out


_COL_RUNS = ((0, 4096), (4112, W_COLS), (4096, 4112))


def _w_all(main, tails, wc):
    pieces = []
    for lo, hi in _COL_RUNS:
        for dv in range(N_DEV):
            a, b = max(lo, dv * wc) - dv * wc, min(hi, (dv + 1) * wc) - dv * wc
            if a < min(b, PACK_W):
                pieces.append(main[dv][:, :, a:min(b, PACK_W)])
            if b > max(a, PACK_W):
                pieces.append(tails[dv][:, :, max(a, PACK_W) - PACK_W:b - PACK_W])
    ll, rows = main[0].shape[0], main[0].shape[1]
    pieces.append(jnp.zeros((ll, rows, W_PAD - W_COLS), main[0].dtype))
    return jnp.concatenate(pieces, axis=2)


def _ref_cols(g, lo, hi):
    pieces, off = [], 0
    for a, b in _COL_RUNS:
        s, e = max(lo, a), min(hi, b)
        if s < e:
            pieces.append((s, g[..., off + s - a:off + e - a]))
        off += b - a
    pieces.sort(key=lambda t: t[0])
    return jnp.concatenate([p for _, p in pieces], axis=-1) if len(pieces) > 1 else pieces[0][1]


def kernel(x, norm_pre, w_in, conv_w, a_log, dt_bias, head_norm, ssm_a_re, ssm_a_im, ssm_log_dt, ssm_b_re, ssm_b_im, ssm_c_re, ssm_c_im, ssm_d, w_glu, b_glu, w_out, norm_post, loss_target, m_norm_pre, m_w_in, m_conv_w, m_a_log, m_dt_bias, m_head_norm, m_ssm_a_re, m_ssm_a_im, m_ssm_log_dt, m_ssm_b_re, m_ssm_b_im, m_ssm_c_re, m_ssm_c_im, m_ssm_d, m_w_glu, m_b_glu, m_w_out, m_norm_post, v_norm_pre, v_w_in, v_conv_w, v_a_log, v_dt_bias, v_head_norm, v_ssm_a_re, v_ssm_a_im, v_ssm_log_dt, v_ssm_b_re, v_ssm_b_im, v_ssm_c_re, v_ssm_c_im, v_ssm_d, v_w_glu, v_b_glu, v_w_out, v_norm_post):
    loc = dict(locals())
    w = {n: loc[n] for n in ALL_W}
    m = {n: loc["m_" + n] for n in ALL_W}
    v = {n: loc["v_" + n] for n in ALL_W}
    depth = w_in.shape[0]
    wc = w_in.shape[2]
    cc = conv_w.shape[2]
    wr = w_glu.shape[1]

    tail = wc - PACK_W
    conv_hi = conv_w.astype(bf16)
    conv_mid = (conv_w - conv_hi.astype(f32)).astype(bf16)
    conv_lo = (conv_w - conv_hi.astype(f32) - conv_mid.astype(f32)).astype(bf16)
    conv3 = jnp.stack([conv_hi, conv_mid, conv_lo], axis=1)
    w_in_b = w_in.astype(bf16)
    MISC = 16
    o_glu, o_out, o_tail, o_conv = D_MODEL, D_MODEL + wr, D_MODEL + 2 * wr, D_MODEL + 2 * wr + MISC

    def rows16(a):
        return jnp.pad(a, ((0, MISC - a.shape[0]), (0, PACK_W - a.shape[1])))

    def layer_shard(li):
        return jnp.concatenate([w_in_b[li, :, :PACK_W], w_glu[li].astype(bf16), w_out[li].astype(bf16),
                                rows16(w_in_b[li, :, PACK_W:].T), rows16(conv3[li].reshape(3 * CONV_K, cc))])

    def layer_weights(gathered):
        tails = gathered[:, o_tail:o_tail + tail].reshape(N_DEV * tail, PACK_W).T
        w_all = _w_all([gathered[dv, :D_MODEL][None] for dv in range(N_DEV)],
                       [tails[None, :, dv * tail:(dv + 1) * tail] for dv in range(N_DEV)], wc)[0]
        glu = gathered[:, o_glu:o_glu + wr].reshape(N_DEV * wr, D_MODEL)
        out = gathered[:, o_out:o_out + wr].reshape(N_DEV * wr, D_MODEL)
        conv = gathered[:, o_conv:o_conv + 3 * CONV_K, :cc].astype(f32).reshape(N_DEV, 3, CONV_K, cc)
        conv = (conv[:, 0] + conv[:, 1] + conv[:, 2]).transpose(1, 0, 2).reshape(CONV_K, N_DEV * cc)
        return dict(w_all=w_all, w_glu=glu, w_out=out, conv_w=conv)

    def layer_params(li, gathered):
        return dict(layer_weights(gathered), norm_pre=norm_pre[li].reshape(1, -1),
                    a_log=jnp.pad(a_log[li], (DN_HEADS, 128 - 2 * DN_HEADS)).reshape(1, 128),
                    dt_bias=jnp.pad(dt_bias[li], (DN_HEADS, 128 - 2 * DN_HEADS)).reshape(1, 128),
                    head_norm=head_norm[li].reshape(1, -1), ssm_a_re=ssm_a_re[li], ssm_a_im=ssm_a_im[li],
                    ssm_log_dt=ssm_log_dt[li], ssm_b_re=ssm_b_re[li], ssm_b_im=ssm_b_im[li], ssm_c_re=ssm_c_re[li],
                    ssm_c_im=ssm_c_im[li], ssm_d=ssm_d[li].reshape(1, -1),
                    b_glu=b_glu[li].reshape(1, -1), norm_post=norm_post[li].reshape(1, -1))

    act = x[0]
    saved, params = [], []
    gathered = all_gather("gather_weights", layer_shard(0))
    for li in range(depth):
        params.append(layer_params(li, gathered))
        act, sv, gathered = layer_fwd(act, params[li], li, layer_shard(li + 1) if li + 1 < depth else None)
        saved.append(sv)
    loss_part, dy = loss_head("loss_head", act, loss_target[0])
    repl_shapes = {n: w[n].shape for n in REPL}
    repl_rows = _repl_rows(repl_shapes)
    rr = repl_rows // N_DEV
    my_c = lax.axis_index("c")

    def chip_sums(li, gl, extra=None):
        g_conv = gl["conv_w"].astype(bf16)
        tails = jnp.concatenate([_ref_cols(gl["w_all"], dv * wc + PACK_W, (dv + 1) * wc) for dv in range(N_DEV)],
                                axis=1).T
        blocks = jnp.stack([jnp.concatenate(
            [_ref_cols(gl["w_all"], dv * wc, dv * wc + PACK_W), gl["w_glu"][dv * wr:(dv + 1) * wr],
             gl["w_out"][dv * wr:(dv + 1) * wr], rows16(tails[dv * tail:(dv + 1) * tail]),
             rows16(g_conv[:, dv * cc:(dv + 1) * cc])]
            + ([extra[dv * rr:(dv + 1) * rr]] if extra is not None else []))
            for dv in range(N_DEV)])
        blocks = blocks.reshape(4, 2, blocks.shape[1], PACK_W)
        from_sibling = pair_exchange(f"pair_grads_l{li}", blocks)
        own = lax.dynamic_index_in_dim(blocks, my_c, axis=1, keepdims=False)
        return pair_sum(f"pair_sum_grads_l{li}", own, from_sibling)

    grads, slots, pending = [None] * depth, [None] * depth, None
    for li in reversed(range(depth)):
        dy, grads[li], got = layer_bwd(dy, params[li], saved[li], li, pending)
        if pending is not None:
            slots[li + 1] = got
        g_repl = None
        if li == 0:
            g_repl = _repl_pack({n: jnp.stack([grads[l][n] for l in range(depth)]).reshape(w[n].shape) for n in REPL},
                                repl_shapes).astype(bf16)
        pending = chip_sums(li, grads[li], g_repl)
    slots[0] = chip_exchange("scatter_grads", pending)
    grad_x = dy[None]
    loss = lax.psum(loss_part[0, 0], ("x", "y", "c"))
    mine = [sum_slots(f"sum_grads_l{li}", slots[li]) for li in range(depth)]
    o_repl = o_conv + MISC
    gs_w_in = jnp.stack([jnp.concatenate([mi[:o_glu], mi[o_tail:o_tail + tail].T], axis=1) for mi in mine])
    gs_glu = jnp.stack([mi[o_glu:o_out] for mi in mine])
    gs_out = jnp.stack([mi[o_out:o_tail] for mi in mine])
    gs_conv = jnp.stack([mi[o_conv:o_conv + CONV_K, :cc] for mi in mine])
    g_repl_full = all_gather("gather_repl_grads", mine[0][o_repl:o_repl + rr]).reshape(repl_rows, PACK_W)
    g = _repl_unpack(g_repl_full, repl_shapes)
    g.update(w_in=gs_w_in, w_glu=gs_glu, w_out=gs_out, conv_w=gs_conv)

    delta, new_m, new_v = {}, {}, {}
    for n in SHARDED:
        delta[n], new_m[n], new_v[n] = adamw("adamw_" + n, w[n], g[n], m[n], v[n])
    outs = adamw("adamw_repl", *[_repl_pack({n: src[n] for n in REPL}, repl_shapes) for src in (w, g, m, v)])
    for dst, o in zip((delta, new_m, new_v), outs):
        dst.update(_repl_unpack(o, repl_shapes))
    return (loss, grad_x, *[g[n] for n in ALL_W], *[delta[n] for n in ALL_W], *[new_m[n] for n in ALL_W],
            *[new_v[n] for n in ALL_W])
```

```python
import math

import jax
import jax.numpy as jnp
from jax import lax
from jax.experimental import pallas as pl
from jax.experimental.pallas import tpu as pltpu

f32 = jnp.float32
bf16 = jnp.bfloat16

D_MODEL = 1024
N_DEV = 8
DN_HEADS = 8
HEAD_DIM = 128
CHUNK = 64
CONV_K = 4
SSM_GROUPS = 64
SSM_GROUP = 16
SSM_STATE = 64
SUPER = 8
N_SUPER = SSM_GROUPS // SUPER
SG_STATE = SUPER * SSM_STATE
EPS = 1e-6
W_COLS = 8208
W_PAD = 8448
ADAM_LR, ADAM_B1, ADAM_B2, ADAM_EPS, ADAM_WD, ADAM_STEP = 0.001, 0.9, 0.999, 1e-08, 0.01, 10
VMEM_LIMIT = 56 * 1024 * 1024
MESH = pl.DeviceIdType.MESH
HIGH = lax.Precision.HIGH
DELTA_HB = 8


def _call(body, name, out_shape, grid=None, in_specs=None, out_specs=None, scratch=(), **kw):
    args = dict(out_shape=out_shape, name=name, scratch_shapes=list(scratch),
                compiler_params=pltpu.CompilerParams(vmem_limit_bytes=VMEM_LIMIT, **kw))
    if grid is not None:
        args.update(grid=grid, in_specs=in_specs, out_specs=out_specs)
    else:
        if in_specs is not None:
            args.update(in_specs=in_specs)
        if out_specs is not None:
            args.update(out_specs=out_specs)
    return pl.pallas_call(body, **args)


def _sds(shape, dtype=f32):
    return jax.ShapeDtypeStruct(tuple(shape), dtype)


def _sigmoid(x):
    return 1.0 / (1.0 + jnp.exp(-x))


def _silu(x):
    return x * _sigmoid(x)


_GELU_C = math.sqrt(2.0 / math.pi)


def _gelu(x):
    return 0.5 * x * (1.0 + jnp.tanh(_GELU_C * (x + 0.044715 * x * x * x)))


def _dgelu(x):
    t = jnp.tanh(_GELU_C * (x + 0.044715 * x * x * x))
    return 0.5 * (1.0 + t) + 0.5 * x * (1.0 - t * t) * _GELU_C * (1.0 + 3 * 0.044715 * x * x)


def _softplus(x):
    return jnp.maximum(x, 0.0) + jnp.log(1.0 + jnp.exp(-jnp.abs(x)))


def _bdot(a, b, dn):
    return lax.dot_general(a.astype(bf16), b.astype(bf16), (dn, ((), ())), preferred_element_type=f32)


def _matmul(name, a, b, *, dn, grid, a_spec, b_spec, o_spec, o_shape, o_dtype=f32):
    nk = grid[-1]

    def body(a_ref, b_ref, o_ref, acc_ref):
        p = _bdot(a_ref[...], b_ref[...], dn)
        if nk == 1:
            o_ref[...] = p.astype(o_dtype)
        else:
            k = pl.program_id(len(grid) - 1)

            @pl.when(k == 0)
            def _():
                acc_ref[...] = p

            @pl.when(k > 0)
            def _():
                acc_ref[...] += p

            @pl.when(k == nk - 1)
            def _():
                o_ref[...] = acc_ref[...].astype(o_dtype)

    blk = tuple(d for d in o_spec.block_shape if d is not None)
    return _call(body, name, _sds(o_shape, o_dtype), grid, [a_spec, b_spec], o_spec,
                 scratch=[pltpu.VMEM(blk if nk > 1 else (8, 128), f32)])(a, b)


def _tile(n, pref):
    for t in pref:
        if n % t == 0:
            return t
    return n


def mm_nn(name, a, b, o_dtype=f32):
    m, k = a.shape
    n = b.shape[1]
    tm, tn, tk = _tile(m, (1024, 512, 256)), _tile(n, (2816, 1024, 512)), _tile(k, (2816, 1024))
    return _matmul(name, a, b, dn=((1,), (0,)), grid=(m // tm, n // tn, k // tk),
                   a_spec=pl.BlockSpec((tm, tk), lambda i, j, l: (i, l)),
                   b_spec=pl.BlockSpec((tk, tn), lambda i, j, l: (l, j)),
                   o_spec=pl.BlockSpec((tm, tn), lambda i, j, l: (i, j)), o_shape=(m, n), o_dtype=o_dtype)


def mm_nt(name, a, b):
    m, k = a.shape
    n = b.shape[0]
    tm, tn, tk = _tile(m, (1024, 512, 256)), _tile(n, (1024, 512)), _tile(k, (2816, 1024))
    return _matmul(name, a, b, dn=((1,), (1,)), grid=(m // tm, n // tn, k // tk),
                   a_spec=pl.BlockSpec((tm, tk), lambda i, j, l: (i, l)),
                   b_spec=pl.BlockSpec((tn, tk), lambda i, j, l: (j, l)),
                   o_spec=pl.BlockSpec((tm, tn), lambda i, j, l: (i, j)), o_shape=(m, n))


def mm_tn(name, a, b, o_dtype=f32):
    k, m = a.shape
    n = b.shape[1]
    tm, tn, tk = _tile(m, (512,)), _tile(n, (2816, 1024, 512)), _tile(k, (512, 256))
    return _matmul(name, a, b, dn=((0,), (0,)), grid=(m // tm, n // tn, k // tk),
                   a_spec=pl.BlockSpec((tk, tm), lambda i, j, l: (l, i)),
                   b_spec=pl.BlockSpec((tk, tn), lambda i, j, l: (l, j)),
                   o_spec=pl.BlockSpec((tm, tn), lambda i, j, l: (i, j)), o_shape=(m, n), o_dtype=o_dtype)


def _rows(t, light=False):
    return _tile(t, (512, 256) if light else (256,))


def _row_spec(tm, w):
    return pl.BlockSpec((tm, w), lambda i: (i, 0))


def _acc_spec(r, w):
    return pl.BlockSpec((r, w), lambda i: (0, 0))


def _accumulate(ref, val):
    @pl.when(pl.program_id(0) == 0)
    def _():
        ref[...] = val

    @pl.when(pl.program_id(0) > 0)
    def _():
        ref[...] += val


def rmsnorm_fwd(name, x, gain):
    t, d = x.shape
    tm = _rows(t, light=True)

    def body(x_ref, g_ref, o_ref):
        xv = x_ref[...]
        r = lax.rsqrt(jnp.mean(xv * xv, axis=-1, keepdims=True) + EPS)
        o_ref[...] = (xv * r * g_ref[...]).astype(bf16)

    return _call(body, name, _sds((t, d), bf16), (t // tm,), [_row_spec(tm, d), _acc_spec(1, d)],
                 _row_spec(tm, d))(x, gain)


def rmsnorm_bwd(name, x, gain, dn, dres):
    t, d = x.shape
    tm = _rows(t, light=True)

    def body(x_ref, g_ref, dn_ref, dr_ref, dx_ref, dg_ref):
        xv = x_ref[...]
        r = lax.rsqrt(jnp.mean(xv * xv, axis=-1, keepdims=True) + EPS)
        n = xv * r
        dnv = dn_ref[...]
        _accumulate(dg_ref, jnp.sum(dnv * n, axis=0, keepdims=True))
        dng = dnv * g_ref[...]
        dx_ref[...] = dr_ref[...] + r * (dng - n * jnp.mean(dng * n, axis=-1, keepdims=True))

    return _call(body, name, (_sds((t, d)), _sds((1, d))), (t // tm,),
                 [_row_spec(tm, d), _acc_spec(1, d), _row_spec(tm, d), _row_spec(tm, d)],
                 (_row_spec(tm, d), _acc_spec(1, d)))(x, gain, dn, dres)


def residual_norm_fwd(name, x, out, gain):
    t, d = x.shape
    tm = _rows(t, light=True)

    def body(x_ref, o_ref, g_ref, y_ref):
        ov = o_ref[...]
        r = lax.rsqrt(jnp.mean(ov * ov, axis=-1, keepdims=True) + EPS)
        y_ref[...] = x_ref[...] + ov * r * g_ref[...]

    return _call(body, name, _sds((t, d)), (t // tm,), [_row_spec(tm, d), _row_spec(tm, d), _acc_spec(1, d)],
                 _row_spec(tm, d))(x, out, gain)


def post_norm_bwd(name, out, gain, dy):
    t, d = out.shape
    tm = _rows(t, light=True)

    def body(o_ref, g_ref, dy_ref, do_ref, dg_ref):
        ov = o_ref[...]
        r = lax.rsqrt(jnp.mean(ov * ov, axis=-1, keepdims=True) + EPS)
        n = ov * r
        dyv = dy_ref[...]
        _accumulate(dg_ref, jnp.sum(dyv * n, axis=0, keepdims=True))
        dng = dyv * g_ref[...]
        do_ref[...] = (r * (dng - n * jnp.mean(dng * n, axis=-1, keepdims=True))).astype(bf16)

    return _call(body, name, (_sds((t, d), bf16), _sds((1, d))), (t // tm,),
                 [_row_spec(tm, d), _acc_spec(1, d), _row_spec(tm, d)],
                 (_row_spec(tm, d), _acc_spec(1, d)))(out, gain, dy)


def loss_head(name, y, target):
    t, d = y.shape
    tm = _rows(t, light=True)

    def body(y_ref, t_ref, l_ref, dy_ref):
        e = y_ref[...] - t_ref[...]
        dy_ref[...] = e * (1.0 / d)
        s = jnp.sum(jnp.sum(e * e, axis=1, keepdims=True), axis=0, keepdims=True) * (0.5 / d)
        _accumulate(l_ref, s)

    return _call(body, name, (_sds((1, 1)), _sds((t, d))), (t // tm,),
                 [_row_spec(tm, d), _row_spec(tm, d)], (_acc_spec(1, 1), _row_spec(tm, d)))(y, target)


def _prev_spec(tm, w):
    return pl.BlockSpec((8, w), lambda i: (jnp.maximum(i * (tm // 8) - 1, 0), 0))


def _next_spec(tm, w, t):
    return pl.BlockSpec((8, w), lambda i: (jnp.minimum((i + 1) * (tm // 8), t // 8 - 1), 0))


def _fill_pad(pad_ref, prev_ref, cur_ref, tm):
    keep = (pl.program_id(0) > 0).astype(f32)
    pad_ref[0:8, :] = prev_ref[...] * keep
    pad_ref[8:8 + tm, :] = cur_ref[...]


def _conv_block(pad_ref, w_ref, cb, tm):
    cs = slice(cb * 128, (cb + 1) * 128)
    acc = pad_ref[pl.ds(8 - (CONV_K - 1), tm), cs] * w_ref[0:1, cs]
    for j in range(1, CONV_K):
        acc = acc + pad_ref[pl.ds(8 - (CONV_K - 1) + j, tm), cs] * w_ref[j:j + 1, cs]
    return acc


def conv_qkv_fwd(name, proj, conv_w, a_log, dt_bias):
    t = proj.shape[0]
    tm = _rows(t)
    scale = HEAD_DIM ** -0.5

    def body(cur_ref, prev_ref, w_ref, bd_ref, al_ref, db_ref, q_ref, k_ref, v_ref, bg_ref, pad_ref):
        _fill_pad(pad_ref, prev_ref, cur_ref, tm)
        for cb in range(3 * DN_HEADS):
            s = _silu(_conv_block(pad_ref, w_ref, cb, tm))
            hs = slice((cb % DN_HEADS) * 128, (cb % DN_HEADS + 1) * 128)
            if cb < DN_HEADS:
                q_ref[:, hs] = s * (lax.rsqrt(jnp.sum(s * s, axis=-1, keepdims=True) + EPS) * scale)
            elif cb < 2 * DN_HEADS:
                k_ref[:, hs] = s * lax.rsqrt(jnp.sum(s * s, axis=-1, keepdims=True) + EPS)
            else:
                v_ref[:, hs] = s
        bd = bd_ref[...]
        beta = _sigmoid(bd)
        g = -jnp.exp(al_ref[...]) * _softplus(bd + db_ref[...])
        lane = lax.broadcasted_iota(jnp.int32, bd.shape, 1)
        bg_ref[...] = jnp.where(lane < DN_HEADS, beta, jnp.where(lane < 2 * DN_HEADS, g, 0.0))

    w3 = 3 * D_MODEL
    return _call(body, name, (_sds((t, D_MODEL)),) * 3 + (_sds((t, 128)),), (t // tm,),
                 [pl.BlockSpec((tm, w3), lambda i: (i, 0)), _prev_spec(tm, w3), _acc_spec(CONV_K, w3),
                  pl.BlockSpec((tm, 128), lambda i: (i, 8192 // 128)), _acc_spec(1, 128), _acc_spec(1, 128)],
                 (_row_spec(tm, D_MODEL),) * 3 + (_row_spec(tm, 128),),
                 scratch=[pltpu.VMEM((tm + 8, w3), f32)])(proj, proj, conv_w, proj, a_log, dt_bias)


def conv_qkv_bwd(name, proj, conv_w, a_log, dt_bias, dq, dk, dv, dbg):
    t = proj.shape[0]
    tm = _rows(t)
    scale = HEAD_DIM ** -0.5

    def body(cur_ref, prev_ref, w_ref, bd_ref, al_ref, db_ref, dq_ref, dk_ref, dv_ref, dbg_ref,
             dc_ref, dw_ref, dbd_ref, dab_ref, pad_ref):
        _fill_pad(pad_ref, prev_ref, cur_ref, tm)

        @pl.when(pl.program_id(0) == 0)
        def _():
            dw_ref[...] = jnp.zeros_like(dw_ref)

        for cb in range(3 * DN_HEADS):
            cs = slice(cb * 128, (cb + 1) * 128)
            hs = slice((cb % DN_HEADS) * 128, (cb % DN_HEADS + 1) * 128)
            taps = [pad_ref[pl.ds(8 - (CONV_K - 1) + j, tm), cs] for j in range(CONV_K)]
            c = taps[0] * w_ref[0:1, cs]
            for j in range(1, CONV_K):
                c = c + taps[j] * w_ref[j:j + 1, cs]
            sg = _sigmoid(c)
            s = c * sg
            if cb < 2 * DN_HEADS:
                dn = (dq_ref[:, hs] * scale) if cb < DN_HEADS else dk_ref[:, hs]
                r = lax.rsqrt(jnp.sum(s * s, axis=-1, keepdims=True) + EPS)
                ds = r * dn - s * (r * r * r) * jnp.sum(dn * s, axis=-1, keepdims=True)
            else:
                ds = dv_ref[:, hs]
            dc = ds * (sg * (1.0 + c * (1.0 - sg)))
            dc_ref[:, cs] = dc
            for j in range(CONV_K):
                dw_ref[j:j + 1, cs] += jnp.sum(dc * taps[j], axis=0, keepdims=True)
        bd = bd_ref[...]
        dbg_v = dbg_ref[...]
        lane = lax.broadcasted_iota(jnp.int32, bd.shape, 1)
        sg = _sigmoid(bd)
        ea = jnp.exp(al_ref[...])
        z = bd + db_ref[...]
        sp = _softplus(z)
        is_b = lane < DN_HEADS
        is_g = jnp.logical_and(lane >= DN_HEADS, lane < 2 * DN_HEADS)
        d_z = jnp.where(is_g, dbg_v * (-ea) * _sigmoid(z), 0.0)
        dbd_ref[...] = jnp.where(is_b, dbg_v * sg * (1.0 - sg), d_z)
        d_al = jnp.sum(jnp.where(is_g, dbg_v * (-ea) * sp, 0.0), axis=0, keepdims=True)
        d_db = jnp.sum(d_z, axis=0, keepdims=True)
        _accumulate(dab_ref, jnp.concatenate([d_al, d_db] + [jnp.zeros_like(d_al)] * 6, axis=0))

    w3 = 3 * D_MODEL
    return _call(body, name, (_sds((t, w3)), _sds((CONV_K, w3)), _sds((t, 128)), _sds((8, 128))), (t // tm,),
                 [pl.BlockSpec((tm, w3), lambda i: (i, 0)), _prev_spec(tm, w3), _acc_spec(CONV_K, w3),
                  pl.BlockSpec((tm, 128), lambda i: (i, 8192 // 128)), _acc_spec(1, 128), _acc_spec(1, 128),
                  _row_spec(tm, D_MODEL), _row_spec(tm, D_MODEL), _row_spec(tm, D_MODEL), _row_spec(tm, 128)],
                 (_row_spec(tm, w3), _acc_spec(CONV_K, w3), _row_spec(tm, 128), _acc_spec(8, 128)),
                 scratch=[pltpu.VMEM((tm + 8, w3), f32)])(proj, proj, conv_w, proj, a_log, dt_bias, dq, dk, dv, dbg)


def _bdg(a, b, ca, cb, prec=None):
    if prec is None:
        a, b = a.astype(bf16), b.astype(bf16)
    return lax.dot_general(a, b, (((ca,), (cb,)), ((0,), (0,))), precision=prec, preferred_element_type=f32)


def _bnn(a, b, prec=None):
    return _bdg(a, b, 2, 1, prec)


def _bnt(a, b, prec=None):
    return _bdg(a, b, 2, 2, prec)


def _btn(a, b, prec=None):
    return _bdg(a, b, 1, 1, prec)


def _delta_local(q, k, v, g_row, b_row, solved=None):
    c = CHUNK
    ii = lax.broadcasted_iota(jnp.int32, (c, c), 0)
    jj = lax.broadcasted_iota(jnp.int32, (c, c), 1)
    eye, lower, strict = ii == jj, ii >= jj, ii > jj
    shp = (q.shape[0], c, c)
    g_b = jnp.broadcast_to(g_row, shp)
    gc_col = jnp.sum(jnp.where(lower, g_b, 0.0), axis=2, keepdims=True)
    gc_row = jnp.sum(jnp.where(eye, jnp.broadcast_to(gc_col, shp), 0.0), axis=1, keepdims=True)
    b_col = jnp.sum(jnp.where(eye, jnp.broadcast_to(b_row, shp), 0.0), axis=2, keepdims=True)
    gl = jnp.sum(g_row, axis=2, keepdims=True)
    decay = jnp.exp(jnp.where(lower, gc_col - gc_row, -1e30))
    kb = k * b_col
    kk = _bnt(kb, k)
    lmat = jnp.where(strict, kk * decay, 0.0)
    egc = jnp.exp(gc_col)
    rhs_w = kb * egc
    if solved is None:
        tinv = eye.astype(f32) - lmat
        pw = lmat
        for _ in range(5):
            pw = _bnn(pw, pw, HIGH)
            tinv = tinv + _bnn(tinv, pw, HIGH)
        uw = _bnn(tinv, jnp.concatenate([v * b_col, rhs_w], axis=2), HIGH)
        u, w = uw[:, :, 0:HEAD_DIM], uw[:, :, HEAD_DIM:]
    else:
        tinv, u, w = solved
    qk = _bnt(q, k)
    amat = jnp.where(lower, qk * decay, 0.0)
    q_dec = q * egc
    kdf = jnp.exp(gl - gc_col)
    k_dec = k * kdf
    return dict(eye=eye, lower=lower, strict=strict, gc_col=gc_col, b_col=b_col, gl=gl, decay=decay,
                kb=kb, kk=kk, tinv=tinv, egc=egc, rhs_w=rhs_w, u=u, w=w, qk=qk, amat=amat, q_dec=q_dec, kdf=kdf,
                k_dec=k_dec)


def _heads(ref, hb):
    return jnp.stack([ref[:, h * HEAD_DIM:(h + 1) * HEAD_DIM] for h in range(hb)])


def _put_heads(ref, val):
    for h in range(val.shape[0]):
        ref[:, h * HEAD_DIM:(h + 1) * HEAD_DIM] = val[h]


def delta_fwd(name, q, k, v, g_rows, b_rows, next_shard=None):
    t = q.shape[0]
    nc = t // CHUNK
    hb = DELTA_HB
    fused = next_shard is not None
    assert not fused or (hb == DN_HEADS and nc >= 2)

    def body(*refs):
        if fused:
            (q_ref, k_ref, v_ref, g_ref, b_ref, x_ref, o_ref, s_ref, t_ref, u_ref, w_ref, gath_ref, state, send_sems,
             recv_sems, local_sem) = refs
        else:
            q_ref, k_ref, v_ref, g_ref, b_ref, o_ref, s_ref, t_ref, u_ref, w_ref, state = refs
        n = pl.program_id(1)
        if fused:
            start, forward, finish = _gather_steps(x_ref, gath_ref, send_sems, recv_sems, local_sem)
            pl.when(n == 0)(start)

        @pl.when(n == 0)
        def _():
            state[...] = jnp.zeros_like(state)

        loc = _delta_local(_heads(q_ref, hb), _heads(k_ref, hb), _heads(v_ref, hb), g_ref[:, pl.ds(n, 1), :],
                           b_ref[:, pl.ds(n, 1), :])
        s0 = state[...]
        s_ref[...] = s0
        t_ref[...] = loc["tinv"]
        _put_heads(u_ref, loc["u"])
        _put_heads(w_ref, loc["w"])
        v_new = loc["u"] - _bnn(loc["w"], s0)
        _put_heads(o_ref, _bnn(loc["q_dec"], s0) + _bnn(loc["amat"], v_new))
        state[...] = s0 * jnp.exp(loc["gl"]) + _btn(loc["k_dec"], v_new)
        if fused:
            pl.when(n == (3 * nc) // 4)(forward)
            pl.when(n == nc - 1)(finish)

    tok = pl.BlockSpec((CHUNK, hb * HEAD_DIM), lambda h, n: (n, h))
    row = pl.BlockSpec((hb, nc, CHUNK), lambda h, n: (h, 0, 0))
    any_spec = pl.BlockSpec(memory_space=pl.ANY)
    outs = (_sds((t, D_MODEL)), _sds((DN_HEADS, nc, HEAD_DIM, HEAD_DIM)), _sds((DN_HEADS, nc, CHUNK, CHUNK)),
            _sds((t, D_MODEL)), _sds((t, D_MODEL)))
    out_specs = (tok, pl.BlockSpec((hb, None, HEAD_DIM, HEAD_DIM), lambda h, n: (h, n, 0, 0)),
                 pl.BlockSpec((hb, None, CHUNK, CHUNK), lambda h, n: (h, n, 0, 0)), tok, tok)
    in_specs, args = [tok, tok, tok, row, row], (q, k, v, g_rows, b_rows)
    scratch = [pltpu.VMEM((hb, HEAD_DIM, HEAD_DIM), f32)]
    if fused:
        outs += (_sds((N_DEV,) + next_shard.shape, next_shard.dtype),)
        out_specs += (any_spec,)
        in_specs, args = in_specs + [any_spec], args + (next_shard,)
        scratch = scratch + list(_GATHER_SEMS)
    return _call(body, name, outs, (DN_HEADS // hb, nc), in_specs, out_specs, scratch=scratch)(*args)


def delta_bwd(name, q, k, v, g_rows, b_rows, s_all, t_all, u_all, w_all, do, chip_part=None):
    t = q.shape[0]
    nc = t // CHUNK
    c = CHUNK
    hb = DELTA_HB

    fused = chip_part is not None
    assert not fused or hb == DN_HEADS

    def body(*refs):
        if fused:
            (q_ref, k_ref, v_ref, g_ref, b_ref, s_ref, t_ref, u_ref, w_ref, do_ref, x_ref, dq_ref, dk_ref, dv_ref,
             dg_ref, db_ref, slot_ref, dstate, send_sems, recv_sems, local_sem) = refs
        else:
            (q_ref, k_ref, v_ref, g_ref, b_ref, s_ref, t_ref, u_ref, w_ref, do_ref, dq_ref, dk_ref, dv_ref, dg_ref,
             db_ref, dstate) = refs
        step = pl.program_id(1)
        n = nc - 1 - step
        if fused:
            start, finish = _chip_exchange_steps(x_ref, slot_ref, send_sems, recv_sems, local_sem)
            pl.when(step == 0)(start)

        @pl.when(step == 0)
        def _():
            dstate[...] = jnp.zeros_like(dstate)

        qv, kv, vv = _heads(q_ref, hb), _heads(k_ref, hb), _heads(v_ref, hb)
        L = _delta_local(qv, kv, vv, g_ref[:, pl.ds(n, 1), :], b_ref[:, pl.ds(n, 1), :],
                         solved=(t_ref[...], _heads(u_ref, hb), _heads(w_ref, hb)))
        eye, lower, strict = L["eye"], L["lower"], L["strict"]
        shp = (hb, c, c)
        s0 = s_ref[...]
        dov = _heads(do_ref, hb)
        ds = dstate[...]
        eg = jnp.exp(L["gl"])
        v_new = L["u"] - _bnn(L["w"], s0)
        d_k_dec = _bnt(v_new, ds)
        d_v_new = _bnn(L["k_dec"], ds) + _btn(L["amat"], dov)
        d_eg = jnp.sum(jnp.sum(ds * s0, axis=2, keepdims=True), axis=1, keepdims=True)
        d_q_dec = _bnt(dov, s0)
        d_a = _bnt(dov, v_new)
        d_w = -_bnt(d_v_new, s0)
        dstate[...] = ds * eg + _btn(L["q_dec"], dov) - _btn(L["w"], d_v_new)
        d_am = jnp.where(lower, d_a * L["decay"], 0.0)
        dq = _bnn(d_am, kv) + d_q_dec * L["egc"]
        dk = _btn(d_am, qv) + d_k_dec * L["kdf"]
        e_col = jnp.sum(d_k_dec * L["k_dec"], axis=2, keepdims=True)
        d_gc_col = jnp.sum(d_q_dec * L["q_dec"], axis=2, keepdims=True) - e_col
        d_gl = jnp.sum(e_col, axis=1, keepdims=True) + d_eg * eg
        tinv = L["tinv"]
        d_rhs = _btn(tinv, jnp.concatenate([d_v_new, d_w], axis=2), HIGH)
        d_rhs_u, d_rhs_w = d_rhs[:, :, 0:HEAD_DIM], d_rhs[:, :, HEAD_DIM:]
        d_l = -_bnt(d_rhs, jnp.concatenate([L["u"], L["w"]], axis=2), HIGH)
        _put_heads(dv_ref, d_rhs_u * L["b_col"])
        d_b_col = jnp.sum(d_rhs_u * vv, axis=2, keepdims=True)
        d_gc_col = d_gc_col + jnp.sum(d_rhs_w * L["rhs_w"], axis=2, keepdims=True)
        d_lm = jnp.where(strict, d_l * L["decay"], 0.0)
        d_kb = d_rhs_w * L["egc"] + _bnn(d_lm, kv)
        dk = dk + _btn(d_lm, L["kb"]) + d_kb * L["b_col"]
        d_b_col = d_b_col + jnp.sum(d_kb * kv, axis=2, keepdims=True)
        m = d_am * L["qk"] + d_lm * L["kk"]
        d_gc_col = d_gc_col + jnp.sum(m, axis=2, keepdims=True)
        d_gc_row = (jnp.sum(jnp.where(eye, jnp.broadcast_to(d_gc_col, shp), 0.0), axis=1, keepdims=True)
                    - jnp.sum(m, axis=1, keepdims=True))
        lane = lax.broadcasted_iota(jnp.int32, (1, 1, c), 2)
        d_gc_row = d_gc_row + jnp.where(lane == c - 1, d_gl, 0.0)
        d_gc_tot = jnp.sum(jnp.where(eye, jnp.broadcast_to(d_gc_row, shp), 0.0), axis=2, keepdims=True)
        dg_ref[:, pl.ds(n, 1), :] = jnp.sum(jnp.where(lower, jnp.broadcast_to(d_gc_tot, shp), 0.0), axis=1,
                                            keepdims=True)
        db_ref[:, pl.ds(n, 1), :] = jnp.sum(jnp.where(eye, jnp.broadcast_to(d_b_col, shp), 0.0), axis=1,
                                            keepdims=True)
        _put_heads(dq_ref, dq)
        _put_heads(dk_ref, dk)
        if fused:
            pl.when(step == nc - 1)(finish)

    tok = pl.BlockSpec((CHUNK, hb * HEAD_DIM), lambda h, s: (nc - 1 - s, h))
    row = pl.BlockSpec((hb, nc, CHUNK), lambda h, s: (h, 0, 0))
    any_spec = pl.BlockSpec(memory_space=pl.ANY)
    in_specs = [tok, tok, tok, row, row,
                pl.BlockSpec((hb, None, HEAD_DIM, HEAD_DIM), lambda h, s: (h, nc - 1 - s, 0, 0)),
                pl.BlockSpec((hb, None, CHUNK, CHUNK), lambda h, s: (h, nc - 1 - s, 0, 0)), tok, tok, tok]
    args = (q, k, v, g_rows, b_rows, s_all, t_all, u_all, w_all, do)
    outs = (_sds((t, D_MODEL)),) * 3 + (_sds((DN_HEADS, nc, CHUNK)),) * 2
    out_specs = (tok, tok, tok, row, row)
    scratch = [pltpu.VMEM((hb, HEAD_DIM, HEAD_DIM), f32)]
    if fused:
        in_specs, args = in_specs + [any_spec], args + (chip_part,)
        outs += (_sds(chip_part.shape, chip_part.dtype),)
        out_specs += (any_spec,)
        scratch = scratch + list(_CHIP_SEMS)
    return _call(body, name, outs, (DN_HEADS // hb, nc), in_specs, out_specs, scratch=scratch)(*args)


SEG = 8


def _perm_rows(a):
    t, c = a.shape
    return a.reshape(SEG, t // SEG, c).transpose(1, 0, 2).reshape(t, c)


def _unperm_rows(a):
    t, c = a.shape
    return a.reshape(t // SEG, SEG, c).transpose(1, 0, 2).reshape(t, c)


def _cmul(ar, ai, br, bi):
    return ar * br - ai * bi, ar * bi + ai * br


def _segment_init(er, ei, lr, li, seg_len, reverse):
    w = er.shape[1]
    sub = lax.broadcasted_iota(jnp.int32, (SEG, w), 0)

    def shift(x, k):
        if reverse:
            return jnp.where(sub < SEG - k, pltpu.roll(x, SEG - k, 0), 0.0)
        return jnp.where(sub >= k, pltpu.roll(x, k, 0), 0.0)

    pr, pi = lr, li
    for _ in range(seg_len.bit_length() - 1):
        pr, pi = _cmul(pr, pi, pr, pi)
    fr, fi = shift(er, 1), shift(ei, 1)
    for k in (1, 2, 4):
        sr, si = shift(fr, k), shift(fi, k)
        mr, mi = _cmul(pr, pi, sr, si)
        fr, fi = fr + mr, fi + mi
        pr, pi = _cmul(pr, pi, pr, pi)
    return fr, fi


def s5_fwd(name, u_perm, bbd, cbd, lam):
    t = u_perm.shape[0]
    tt = _tile(t, (2048, 1024, 512, 256, 128))
    nt, ng, w = t // tt, tt // SEG, SG_STATE
    seg_len = t // SEG
    assert seg_len & (seg_len - 1) == 0 and tt % SEG == 0

    def body(u_ref, b_ref, c_ref, lam_ref, y_ref, h_ref, x_scr, state):
        p, i = pl.program_id(1), pl.program_id(2)
        lr1, li1 = lam_ref[:, 0:w], lam_ref[:, w:2 * w]
        lr, li = jnp.broadcast_to(lr1, (SEG, w)), jnp.broadcast_to(li1, (SEG, w))
        base = pl.multiple_of(i * tt, tt)

        @pl.when(p == 0)
        def _():
            x_scr[pl.ds(base, tt), :] = _bdot(u_ref[...], b_ref[...], ((1,), (0,)))

        @pl.when(jnp.logical_and(p == 0, i == 0))
        def _():
            state[...] = jnp.zeros_like(state)

        @pl.when(jnp.logical_and(p == 1, i == 0))
        def _():
            sr, si = _segment_init(state[:, 0:w], state[:, w:2 * w], lr1, li1, seg_len, False)
            state[:, 0:w] = sr
            state[:, w:2 * w] = si

        def run(store):
            def step(g, st):
                row = pl.multiple_of(g * SEG, SEG)
                xg = x_scr[pl.ds(base + row, SEG), :]
                nr = lr * st[0] - li * st[1] + xg[:, 0:w]
                ni = lr * st[1] + li * st[0] + xg[:, w:2 * w]
                if store:
                    h_ref[pl.ds(row, SEG), 0:w] = nr
                    h_ref[pl.ds(row, SEG), w:2 * w] = ni
                return nr, ni

            fin = lax.fori_loop(0, ng, step, (state[:, 0:w], state[:, w:2 * w]))
            state[:, 0:w] = fin[0]
            state[:, w:2 * w] = fin[1]

        @pl.when(p == 0)
        def _():
            run(False)

        @pl.when(p == 1)
        def _():
            run(True)
            y_ref[...] = _bdot(h_ref[...], c_ref[...], ((1,), (0,)))

    return _call(body, name, (_sds((t, D_MODEL)), _sds((t, N_SUPER * 2 * w))), (N_SUPER, 2, nt),
                 [pl.BlockSpec((tt, 128), lambda s, p, i: (i * (1 - p) + (nt - 1) * p, s)),
                  pl.BlockSpec((None, 128, 2 * w), lambda s, p, i: (s, 0, 0)),
                  pl.BlockSpec((None, 2 * w, 128), lambda s, p, i: (s, 0, 0)),
                  pl.BlockSpec((None, 1, 2 * w), lambda s, p, i: (s, 0, 0))],
                 (pl.BlockSpec((tt, 128), lambda s, p, i: (i * p, s)),
                  pl.BlockSpec((tt, 2 * w), lambda s, p, i: (i * p, s))),
                 scratch=[pltpu.VMEM((t, 2 * w), f32), pltpu.VMEM((SEG, 2 * w), f32)])(u_perm, bbd, cbd, lam)


def s5_bwd(name, dy_perm, u_perm, h_perm, hprev0, bbd, cbd, lam):
    t = u_perm.shape[0]
    tt = _tile(t, (2048, 1024, 512, 256, 128))
    nt, ng, w = t // tt, tt // SEG, SG_STATE
    seg_len = t // SEG

    def body(dy_ref, u_ref, h_ref, hp_ref, hp0_ref, b_ref, c_ref, lam_ref, du_ref, db_ref, dc_ref, dl_ref,
             g_scr, state, dl_acc):
        p, i = pl.program_id(1), pl.program_id(2)
        first_tile = jnp.logical_or(p == 0, i == nt - 1)
        lr1, li1 = lam_ref[:, 0:w], -lam_ref[:, w:2 * w]
        lr, li = jnp.broadcast_to(lr1, (SEG, w)), jnp.broadcast_to(li1, (SEG, w))
        base = pl.multiple_of((nt - 1 - i) * tt, tt)

        @pl.when(p == 0)
        def _():
            g_scr[pl.ds(base, tt), :] = _bdot(dy_ref[...], c_ref[...], ((1,), (1,)))

        @pl.when(jnp.logical_and(p == 0, i == 0))
        def _():
            state[...] = jnp.zeros_like(state)

        @pl.when(jnp.logical_and(p == 1, i == 0))
        def _():
            sr, si = _segment_init(state[:, 0:w], state[:, w:2 * w], lr1, li1, seg_len, True)
            state[:, 0:w] = sr
            state[:, w:2 * w] = si
            dl_acc[...] = jnp.zeros_like(dl_acc)

        def adj(g, st):
            row = pl.multiple_of(g * SEG, SEG)
            gg = g_scr[pl.ds(base + row, SEG), :]
            nr = lr * st[0] - li * st[1] + gg[:, 0:w]
            ni = lr * st[1] + li * st[0] + gg[:, w:2 * w]
            return row, nr, ni

        @pl.when(p == 0)
        def _():
            def step(k, st):
                _, nr, ni = adj(ng - 1 - k, st)
                return nr, ni

            fin = lax.fori_loop(0, ng, step, (state[:, 0:w], state[:, w:2 * w]))
            state[:, 0:w] = fin[0]
            state[:, w:2 * w] = fin[1]

        @pl.when(p == 1)
        def _():
            above = jnp.where(first_tile, hp0_ref[...], hp_ref[...])

            def step(k, st):
                g = ng - 1 - k
                row, nr, ni = adj(g, st)
                g_scr[pl.ds(base + row, SEG), 0:w] = nr
                g_scr[pl.ds(base + row, SEG), w:2 * w] = ni
                prow = pl.multiple_of(jnp.maximum(g - 1, 0) * SEG, SEG)
                hp = jnp.where(g > 0, h_ref[pl.ds(prow, SEG), :], above)
                pr, pi = hp[:, 0:w], hp[:, w:2 * w]
                return nr, ni, st[2] + nr * pr + ni * pi, st[3] + ni * pr - nr * pi

            fin = lax.fori_loop(0, ng, step, (state[:, 0:w], state[:, w:2 * w], dl_acc[:, 0:w], dl_acc[:, w:2 * w]))
            state[:, 0:w] = fin[0]
            state[:, w:2 * w] = fin[1]
            dl_acc[:, 0:w] = fin[2]
            dl_acc[:, w:2 * w] = fin[3]
            a = g_scr[pl.ds(base, tt), :]
            du_ref[...] = _bdot(a, b_ref[...], ((1,), (1,)))
            d_b = _bdot(u_ref[...], a, ((0,), (0,)))
            d_c = _bdot(dy_ref[...], h_ref[...], ((0,), (0,)))

            @pl.when(i == 0)
            def _():
                db_ref[...] = d_b
                dc_ref[...] = d_c

            @pl.when(i > 0)
            def _():
                db_ref[...] += d_b
                dc_ref[...] += d_c

            @pl.when(i == nt - 1)
            def _():
                dl_ref[...] = jnp.sum(dl_acc[...], axis=0, keepdims=True)

    tile = lambda s, p, i: (nt - 1 - i, s)
    tile1 = lambda s, p, i: (nt - 1 - i * p, s)
    above = lambda s, p, i: (jnp.maximum((nt - 1 - i * p) * (tt // SEG) - 1, 0), s)
    per_s = lambda s, p, i: (s, 0, 0)
    return _call(body, name, (_sds((t, D_MODEL)), _sds((N_SUPER, 128, 2 * w)), _sds((N_SUPER, 128, 2 * w)),
                              _sds((N_SUPER, 1, 2 * w))), (N_SUPER, 2, nt),
                 [pl.BlockSpec((tt, 128), tile), pl.BlockSpec((tt, 128), tile1), pl.BlockSpec((tt, 2 * w), tile1),
                  pl.BlockSpec((SEG, 2 * w), above), pl.BlockSpec((SEG, 2 * w), lambda s, p, i: (0, s)),
                  pl.BlockSpec((None, 128, 2 * w), per_s), pl.BlockSpec((None, 2 * w, 128), per_s),
                  pl.BlockSpec((None, 1, 2 * w), per_s)],
                 (pl.BlockSpec((tt, 128), tile1), pl.BlockSpec((None, 128, 2 * w), per_s),
                  pl.BlockSpec((None, 128, 2 * w), per_s), pl.BlockSpec((None, 1, 2 * w), per_s)),
                 scratch=[pltpu.VMEM((t, 2 * w), f32), pltpu.VMEM((SEG, 2 * w), f32),
                          pltpu.VMEM((SEG, 2 * w), f32)])(dy_perm, u_perm, h_perm, h_perm, hprev0, bbd, cbd, lam)


def _s5_disc(a_re, a_im, log_dt, br, bi):
    dt = jnp.exp(log_dt)
    mag = jnp.exp(a_re * dt)
    lr, li = mag * jnp.cos(a_im * dt), mag * jnp.sin(a_im * dt)
    den = a_re * a_re + a_im * a_im
    fr = ((lr - 1.0) * a_re + li * a_im) / den
    fi = (li * a_re - (lr - 1.0) * a_im) / den
    return lr, li, fr * br - fi * bi, fr * bi + fi * br


def s5_disc_fwd(name, a_re, a_im, log_dt, br, bi):
    g, n = SSM_GROUPS, SSM_STATE

    def body(ar, ai, ld, brr, bir, lr, li, bbr, bbi):
        o = _s5_disc(ar[...], ai[...], ld[...], brr[...], bir[...])
        lr[...], li[...], bbr[...], bbi[...] = o

    return _call(body, name, (_sds((g, 1, n)), _sds((g, 1, n)), _sds((g, SSM_GROUP, n)), _sds((g, SSM_GROUP, n))))(
        a_re, a_im, log_dt, br, bi)


def s5_disc_bwd(name, a_re, a_im, log_dt, br, bi, d_lr, d_li, d_bbr, d_bbi):
    g, n = SSM_GROUPS, SSM_STATE

    def body(ar, ai, ld, brr, bir, c1, c2, c3, c4, o1, o2, o3, o4, o5):
        _, vjp = jax.vjp(_s5_disc, ar[...], ai[...], ld[...], brr[...], bir[...])
        o1[...], o2[...], o3[...], o4[...], o5[...] = vjp((c1[...], c2[...], c3[...], c4[...]))

    return _call(body, name, (_sds((g, 1, n)), _sds((g, 1, n)), _sds((g, 1, 1)), _sds((g, SSM_GROUP, n)),
                              _sds((g, SSM_GROUP, n))))(a_re, a_im, log_dt, br, bi, d_lr, d_li, d_bbr, d_bbi)


def gelu_fwd(name, ys_lin, proj, d_skip):
    t, d = ys_lin.shape
    tm = _rows(t, light=True)

    def body(y_ref, u_ref, d_ref, o_ref):
        o_ref[...] = _gelu(y_ref[...] + d_ref[...] * u_ref[...])

    return _call(body, name, _sds((t, d)), (t // tm,),
                 [_row_spec(tm, d), pl.BlockSpec((tm, d), lambda i: (i, 4)), _acc_spec(1, d)],
                 _row_spec(tm, d))(ys_lin, proj, d_skip)


def _head_norm(o, hn):
    outs, ns, rs = [], [], []
    for h in range(DN_HEADS):
        oh = o[:, h * 128:(h + 1) * 128]
        r = lax.rsqrt(jnp.mean(oh * oh, axis=-1, keepdims=True) + EPS)
        n = oh * r
        ns.append(n)
        rs.append(r)
        outs.append(n * hn)
    return outs, ns, rs


def merge_fwd(name, proj, o, yg, glu_lin, head_norm, b_glu):
    t = o.shape[0]
    tm = _rows(t)
    d = D_MODEL

    def body(za_ref, zb_ref, ra_ref, rb_ref, o_ref, yg_ref, gl_ref, hn_ref, bg_ref, m_ref):
        on, _, _ = _head_norm(o_ref[...], hn_ref[...])
        y_a = jnp.concatenate(on, axis=1) * _silu(za_ref[...])
        y_b = yg_ref[...] * _sigmoid(gl_ref[...] + bg_ref[...]) * _silu(zb_ref[...])
        m_ref[...] = (_sigmoid(ra_ref[...]) * y_a + _sigmoid(rb_ref[...]) * y_b).astype(bf16)

    pc = lambda c: pl.BlockSpec((tm, d), lambda i: (i, c))
    return _call(body, name, _sds((t, d), bf16), (t // tm,),
                 [pc(3), pc(5), pc(6), pc(7), _row_spec(tm, d), _row_spec(tm, d), _row_spec(tm, d),
                  _acc_spec(1, 128), _acc_spec(1, d)], _row_spec(tm, d))(
        proj, proj, proj, proj, o, yg, glu_lin, head_norm, b_glu)


def merge_bwd(name, proj, o, yg, glu_lin, head_norm, b_glu, dm):
    t = o.shape[0]
    tm = _rows(t)
    d = D_MODEL

    def body(za_ref, zb_ref, ra_ref, rb_ref, o_ref, yg_ref, gl_ref, hn_ref, bg_ref, dm_ref,
             dza_ref, dzb_ref, dra_ref, drb_ref, do_ref, dgl_ref, dyg_ref, dhn_ref, dbg_ref):
        hn = hn_ref[...]
        za, zb, ra, rb = za_ref[...], zb_ref[...], ra_ref[...], rb_ref[...]
        on, ns, rs = _head_norm(o_ref[...], hn)
        onc = jnp.concatenate(on, axis=1)
        sga, sgb = _sigmoid(za), _sigmoid(zb)
        sza = za * sga
        y_a = onc * sza
        yg = yg_ref[...]
        sgl = _sigmoid(gl_ref[...] + bg_ref[...])
        y2 = yg * sgl
        szb = zb * sgb
        y_b = y2 * szb
        sra, srb = _sigmoid(ra), _sigmoid(rb)
        dmv = dm_ref[...]
        dra_ref[...] = (dmv * y_a * sra * (1.0 - sra)).astype(bf16)
        drb_ref[...] = (dmv * y_b * srb * (1.0 - srb)).astype(bf16)
        d_ya = dmv * sra
        d_yb = dmv * srb
        dza_ref[...] = (d_ya * onc * (sga * (1.0 + za * (1.0 - sga)))).astype(bf16)
        dzb_ref[...] = (d_yb * y2 * (sgb * (1.0 + zb * (1.0 - sgb)))).astype(bf16)
        d_on = d_ya * sza
        d_y2 = d_yb * szb
        dyg_ref[...] = d_y2 * sgl
        d_gl = d_y2 * yg * sgl * (1.0 - sgl)
        dgl_ref[...] = d_gl.astype(bf16)
        _accumulate(dbg_ref, jnp.sum(d_gl, axis=0, keepdims=True))
        d_hn = jnp.zeros((1, 128), f32)
        for h in range(DN_HEADS):
            hs = slice(h * 128, (h + 1) * 128)
            dh = d_on[:, hs]
            d_hn = d_hn + jnp.sum(dh * ns[h], axis=0, keepdims=True)
            dn = dh * hn
            do_ref[:, hs] = rs[h] * (dn - ns[h] * jnp.mean(dn * ns[h], axis=-1, keepdims=True))
        _accumulate(dhn_ref, d_hn)

    pc = lambda c: pl.BlockSpec((tm, d), lambda i: (i, c))
    rs_ = _row_spec(tm, d)
    return _call(body, name, (_sds((t, d), bf16),) * 4 + (_sds((t, d)), _sds((t, d), bf16), _sds((t, d)),
                              _sds((1, 128)), _sds((1, d))),
                 (t // tm,), [pc(3), pc(5), pc(6), pc(7), rs_, rs_, rs_, _acc_spec(1, 128), _acc_spec(1, d), rs_],
                 (rs_,) * 7 + (_acc_spec(1, 128), _acc_spec(1, d)))(
        proj, proj, proj, proj, o, yg, glu_lin, head_norm, b_glu, dm)


def gelu_bwd(name, ys_lin, proj, d_skip, dyg_a, dyg_b):
    t, d = ys_lin.shape
    tm = _rows(t, light=True)

    def body(y_ref, u_ref, d_ref, a_ref, b_ref, dys_ref, du_ref, dd_ref):
        uv = u_ref[...]
        dys = (a_ref[...] + b_ref[...]) * _dgelu(y_ref[...] + d_ref[...] * uv)
        dys_ref[...] = dys
        du_ref[...] = dys * d_ref[...]
        _accumulate(dd_ref, jnp.sum(dys * uv, axis=0, keepdims=True))

    rs_ = _row_spec(tm, d)
    return _call(body, name, (_sds((t, d)), _sds((t, d)), _sds((1, d))), (t // tm,),
                 [rs_, pl.BlockSpec((tm, d), lambda i: (i, 4)), _acc_spec(1, d), rs_, rs_],
                 (rs_, rs_, _acc_spec(1, d)))(ys_lin, proj, d_skip, dyg_a, dyg_b)


def assemble_dproj(name, dc, conv_w, dza, du_a, du_b, dzb, dra, drb, dbd):
    t = dza.shape[0]
    tm = _rows(t)
    nt = t // tm
    d = D_MODEL
    w3 = 3 * d

    def body(cur_ref, nxt_ref, w_ref, za_ref, ua_ref, ub_ref, zb_ref, ra_ref, rb_ref, bd_ref, o_ref, pad_ref):
        keep = (pl.program_id(0) < nt - 1).astype(f32)
        pad_ref[0:tm, :] = cur_ref[...]
        pad_ref[tm:tm + 8, :] = nxt_ref[...] * keep
        for cb in range(w3 // 128):
            cs = slice(cb * 128, (cb + 1) * 128)
            acc = pad_ref[pl.ds(CONV_K - 1, tm), cs] * w_ref[0:1, cs]
            for j in range(1, CONV_K):
                acc = acc + pad_ref[pl.ds(CONV_K - 1 - j, tm), cs] * w_ref[j:j + 1, cs]
            o_ref[:, cs] = acc.astype(bf16)
        o_ref[:, 3 * d:4 * d] = za_ref[...].astype(bf16)
        o_ref[:, 4 * d:5 * d] = (ua_ref[...] + ub_ref[...]).astype(bf16)
        o_ref[:, 5 * d:6 * d] = zb_ref[...].astype(bf16)
        o_ref[:, 6 * d:7 * d] = ra_ref[...].astype(bf16)
        o_ref[:, 7 * d:8 * d] = rb_ref[...].astype(bf16)
        o_ref[:, 8 * d:8 * d + 128] = bd_ref[...].astype(bf16)
        o_ref[:, 8 * d + 128:W_PAD] = jnp.zeros((tm, W_PAD - 8 * d - 128), bf16)

    rs_ = _row_spec(tm, d)
    return _call(body, name, _sds((t, W_PAD), bf16), (nt,),
                 [_row_spec(tm, w3), _next_spec(tm, w3, t), _acc_spec(CONV_K, w3), rs_, rs_, rs_, rs_, rs_, rs_,
                  _row_spec(tm, 128)], _row_spec(tm, W_PAD),
                 scratch=[pltpu.VMEM((tm + 8, w3), f32)])(dc, dc, conv_w, dza, du_a, du_b, dzb, dra, drb, dbd)


def adamw(name, w, g, m, v):
    lead, (r, c) = w.shape[:-2], w.shape[-2:]
    tm = _tile(r, (512, 256, 128, 64, 32, 16, 8))
    c1 = 1.0 / (1.0 - ADAM_B1 ** ADAM_STEP)
    c2 = 1.0 / (1.0 - ADAM_B2 ** ADAM_STEP)

    def body(w_ref, g_ref, m_ref, v_ref, d_ref, nm_ref, nv_ref):
        gv = g_ref[...]
        nm = ADAM_B1 * m_ref[...] + (1.0 - ADAM_B1) * gv
        nv = ADAM_B2 * v_ref[...] + (1.0 - ADAM_B2) * (gv * gv)
        d_ref[...] = -ADAM_LR * ((nm * c1) / (jnp.sqrt(nv * c2) + ADAM_EPS) + ADAM_WD * w_ref[...])
        nm_ref[...] = nm
        nv_ref[...] = nv

    if lead:
        sp = pl.BlockSpec((None, tm, c), lambda l, i: (l, i, 0))
        grid = (lead[0], r // tm)
    else:
        sp = pl.BlockSpec((tm, c), lambda i: (i, 0))
        grid = (r // tm,)
    return _call(body, name, (_sds(w.shape),) * 3, grid, [sp] * 4, (sp,) * 3)(w, g, m, v)


def _coords():
    return lax.axis_index("x"), lax.axis_index("y"), lax.axis_index("c")


def _lin(dev):
    return 4 * dev[0] + 2 * dev[1] + dev[2]


def _chips(me):
    x, y, _ = me
    return [(1 - x, y), (x, 1 - y), (1 - x, 1 - y)]


def _gather_steps(x_ref, o_ref, send_sems, recv_sems, local_sem):
    me = _coords()
    x, y, cc = me
    sibling = (x, y, 1 - cc)
    chips = _chips(me)

    def copy(k, block, to, src=None):
        return pltpu.make_async_remote_copy(
            src_ref=o_ref.at[_lin(block)] if src is None else src, dst_ref=o_ref.at[_lin(block)],
            send_sem=send_sems.at[k], recv_sem=recv_sems.at[k], device_id=to, device_id_type=MESH)

    mine = pltpu.make_async_copy(x_ref, o_ref.at[_lin(me)], local_sem)
    first = [copy(0, me, sibling, src=x_ref)] + [copy(1 + j, me, (*chip, cc), src=x_ref)
                                                 for j, chip in enumerate(chips)]
    passed = [copy(4 + j, (*chip, cc), sibling) for j, chip in enumerate(chips)]

    def start():
        mine.start()
        for cp in first:
            cp.start()

    def forward():
        for j, chip in enumerate(chips):
            copy(1 + j, (*chip, cc), me).wait_recv()
            passed[j].start()

    def finish():
        copy(0, sibling, me).wait_recv()
        for j, chip in enumerate(chips):
            copy(4 + j, (*chip, 1 - cc), me).wait_recv()
        for cp in first + passed:
            cp.wait_send()
        mine.wait()

    return start, forward, finish


_GATHER_SEMS = [pltpu.SemaphoreType.DMA((N_DEV - 1,)), pltpu.SemaphoreType.DMA((N_DEV - 1,)),
                pltpu.SemaphoreType.DMA(())]


def all_gather(name, shard):
    r, c = shard.shape

    def body(x_ref, o_ref, send_sems, recv_sems, local_sem):
        for step in _gather_steps(x_ref, o_ref, send_sems, recv_sems, local_sem):
            step()

    any_spec = pl.BlockSpec(memory_space=pl.ANY)
    return _call(body, name, _sds((N_DEV, r, c), shard.dtype), in_specs=[any_spec], out_specs=any_spec,
                 scratch=list(_GATHER_SEMS))(shard)


def pair_exchange(name, blocks):
    _, _, r, c = blocks.shape

    def body(x_ref, o_ref, send_sems, recv_sems):
        x, y, cc = _coords()
        sibling = (x, y, 1 - cc)
        cps = [pltpu.make_async_remote_copy(src_ref=x_ref.at[ch, 1 - cc], dst_ref=o_ref.at[ch], send_sem=send_sems.at[ch],
                                            recv_sem=recv_sems.at[ch], device_id=sibling, device_id_type=MESH)
               for ch in range(4)]
        for cp in cps:
            cp.start()
        for cp in cps:
            cp.wait()

    any_spec = pl.BlockSpec(memory_space=pl.ANY)
    return _call(body, name, _sds((4, r, c), blocks.dtype), in_specs=[any_spec], out_specs=any_spec,
                 scratch=[pltpu.SemaphoreType.DMA((4,)), pltpu.SemaphoreType.DMA((4,))])(blocks)


def _chip_exchange_steps(x_ref, o_ref, send_sems, recv_sems, local_sem):
    me = _coords()
    x, y, cc = me
    my_chip = 2 * x + y
    mine = pltpu.make_async_copy(x_ref.at[my_chip], o_ref.at[my_chip], local_sem)
    peers = [(px, py, cc) for px, py in _chips(me)]
    sends = [pltpu.make_async_remote_copy(src_ref=x_ref.at[2 * px + py], dst_ref=o_ref.at[my_chip],
                                          send_sem=send_sems.at[j], recv_sem=recv_sems.at[j], device_id=(px, py, pc),
                                          device_id_type=MESH) for j, (px, py, pc) in enumerate(peers)]

    def start():
        mine.start()
        for cp in sends:
            cp.start()

    def finish():
        for j, (px, py, pc) in enumerate(peers):
            pltpu.make_async_remote_copy(src_ref=x_ref.at[my_chip], dst_ref=o_ref.at[2 * px + py],
                                         send_sem=send_sems.at[j], recv_sem=recv_sems.at[j], device_id=(px, py, pc),
                                         device_id_type=MESH).wait_recv()
        for cp in sends:
            cp.wait_send()
        mine.wait()

    return start, finish


_CHIP_SEMS = [pltpu.SemaphoreType.DMA((3,)), pltpu.SemaphoreType.DMA((3,)), pltpu.SemaphoreType.DMA(())]


def chip_exchange(name, blocks):
    def body(x_ref, o_ref, send_sems, recv_sems, local_sem):
        for step in _chip_exchange_steps(x_ref, o_ref, send_sems, recv_sems, local_sem):
            step()

    any_spec = pl.BlockSpec(memory_space=pl.ANY)
    return _call(body, name, _sds(blocks.shape, blocks.dtype), in_specs=[any_spec], out_specs=any_spec,
                 scratch=list(_CHIP_SEMS))(blocks)


def pair_sum(name, mine, theirs):
    _, r, c = mine.shape
    tm = max(d for d in range(16, 769, 16) if r % d == 0)

    def body(a_ref, b_ref, o_ref):
        o_ref[...] = (a_ref[...].astype(f32) + b_ref[...].astype(f32)).astype(o_ref.dtype)

    sp = pl.BlockSpec((None, tm, c), lambda ch, i: (ch, i, 0))
    return _call(body, name, _sds(mine.shape, mine.dtype), (4, r // tm), [sp, sp], sp)(mine, theirs)


def sum_slots(name, slots):
    n, r, c = slots.shape
    tm = max(d for d in range(16, 769, 16) if r % d == 0)

    def body(s_ref, o_ref):
        acc = s_ref[0].astype(f32)
        for d in range(1, n):
            acc = acc + s_ref[d].astype(f32)
        o_ref[...] = acc

    return _call(body, name, _sds((r, c)), (r // tm,), [pl.BlockSpec((n, tm, c), lambda i: (0, i, 0))],
                 pl.BlockSpec((tm, c), lambda i: (i, 0)))(slots)


def _rows_layout(col8, t):
    return col8.T.reshape(DN_HEADS, t // CHUNK, CHUNK)


def _blockdiag(m):
    g, a, b = m.shape
    m = m.reshape(N_SUPER, SUPER, a, b)
    out = jnp.einsum("sgab,gh->sgahb", m, jnp.eye(SUPER, dtype=m.dtype))
    return out.reshape(N_SUPER, SUPER * a, SUPER * b)


def _diag_blocks(m, a, b):
    m = m.reshape(N_SUPER, SUPER, a, SUPER, b)
    return jnp.einsum("sgahb,gh->sgab", m, jnp.eye(SUPER, dtype=m.dtype)).reshape(SSM_GROUPS, a, b)


def _s5_params(p, li):
    tag = f"l{li}"
    n = SSM_STATE
    a_re = p["ssm_a_re"].reshape(SSM_GROUPS, 1, n)
    a_im = p["ssm_a_im"].reshape(SSM_GROUPS, 1, n)
    log_dt = p["ssm_log_dt"].reshape(SSM_GROUPS, 1, 1)
    br = jnp.swapaxes(p["ssm_b_re"], 1, 2)
    bi = jnp.swapaxes(p["ssm_b_im"], 1, 2)
    lr, li_, bbr, bbi = s5_disc_fwd("s5_disc_" + tag, a_re, a_im, log_dt, br, bi)
    lam = jnp.concatenate([lr.reshape(N_SUPER, 1, SG_STATE), li_.reshape(N_SUPER, 1, SG_STATE)], axis=-1)
    bbd = jnp.concatenate([_blockdiag(bbr), _blockdiag(bbi)], axis=-1).astype(bf16)
    c_re = jnp.swapaxes(p["ssm_c_re"], 1, 2)
    c_im = jnp.swapaxes(p["ssm_c_im"], 1, 2)
    cbd = jnp.concatenate([_blockdiag(c_re), -_blockdiag(c_im)], axis=1).astype(bf16)
    return dict(a_re=a_re, a_im=a_im, log_dt=log_dt, br=br, bi=bi, lam=lam, bbd=bbd, cbd=cbd)


def layer_fwd(x, p, li, next_shard=None):
    tag = f"l{li}"
    t = x.shape[0]
    d = D_MODEL
    h = rmsnorm_fwd("norm_pre_" + tag, x, p["norm_pre"])
    proj = mm_nn("proj_" + tag, h, p["w_all"])
    q, k, v, bg = conv_qkv_fwd("conv_" + tag, proj, p["conv_w"], p["a_log"], p["dt_bias"])
    b_rows = _rows_layout(bg[:, 0:DN_HEADS], t)
    g_rows = _rows_layout(bg[:, DN_HEADS:2 * DN_HEADS], t)
    o, s_all, t_all, u_all, w_sol, *gathered = delta_fwd("delta_" + tag, q, k, v, g_rows, b_rows, next_shard)
    sp = _s5_params(p, li)
    u_perm = _perm_rows(proj[:, 4 * d:5 * d])
    ys_perm, hs = s5_fwd("s5_" + tag, u_perm, sp["bbd"], sp["cbd"], sp["lam"])
    ys_lin = _unperm_rows(ys_perm)
    yg = gelu_fwd("gelu_" + tag, ys_lin, proj, p["ssm_d"])
    glu_lin = mm_nn("glu_" + tag, yg, p["w_glu"])
    merged = merge_fwd("merge_" + tag, proj, o, yg, glu_lin, p["head_norm"], p["b_glu"])
    out = mm_nn("out_" + tag, merged, p["w_out"])
    y = residual_norm_fwd("norm_post_" + tag, x, out, p["norm_post"])
    saved = dict(x=x, h=h, proj=proj, q=q, k=k, v=v, g_rows=g_rows, b_rows=b_rows, o=o, s_all=s_all, t_all=t_all, u_all=u_all, w_sol=w_sol, sp=sp, u_perm=u_perm,
                 hs=hs, ys_lin=ys_lin, yg=yg, glu_lin=glu_lin, merged=merged, out=out)
    return y, saved, (gathered[0] if gathered else None)


def layer_bwd(dy, p, s, li, chip_part=None):
    tag = f"l{li}"
    t = dy.shape[0]
    d = D_MODEL
    sp = s["sp"]
    gr = {}
    d_out, gr["norm_post"] = post_norm_bwd("norm_post_b_" + tag, s["out"], p["norm_post"], dy)
    d_merged = mm_nt("out_b_" + tag, d_out, p["w_out"])
    gr["w_out"] = mm_tn("out_w_" + tag, s["merged"], d_out, bf16)
    (dza, dzb, dra, drb, d_o, d_glu, dyg_a, gr["head_norm"], gr["b_glu"]) = merge_bwd(
        "merge_b_" + tag, s["proj"], s["o"], s["yg"], s["glu_lin"], p["head_norm"], p["b_glu"], d_merged)
    dyg_b = mm_nt("glu_b_" + tag, d_glu, p["w_glu"])
    gr["w_glu"] = mm_tn("glu_w_" + tag, s["yg"], d_glu, bf16)
    d_ys, du_a, gr["ssm_d"] = gelu_bwd("gelu_b_" + tag, s["ys_lin"], s["proj"], p["ssm_d"], dyg_a, dyg_b)
    hprev0 = jnp.concatenate([jnp.zeros((1, s["hs"].shape[1]), f32), s["hs"][-SEG:-1]], axis=0)
    du_perm, d_bbd, d_cbd, d_lam = s5_bwd("s5_b_" + tag, _perm_rows(d_ys), s["u_perm"], s["hs"], hprev0, sp["bbd"],
                                          sp["cbd"], sp["lam"])
    du_b = _unperm_rows(du_perm)
    gr["ssm_c_re"] = _diag_blocks(d_cbd[:, :, 0:SG_STATE], SSM_GROUP, SSM_STATE)
    gr["ssm_c_im"] = -_diag_blocks(d_cbd[:, :, SG_STATE:], SSM_GROUP, SSM_STATE)
    d_bbr = _diag_blocks(d_bbd[:, :, 0:SG_STATE], SSM_GROUP, SSM_STATE)
    d_bbi = _diag_blocks(d_bbd[:, :, SG_STATE:], SSM_GROUP, SSM_STATE)
    d_lr = d_lam[:, :, 0:SG_STATE].reshape(SSM_GROUPS, 1, SSM_STATE)
    d_li = d_lam[:, :, SG_STATE:].reshape(SSM_GROUPS, 1, SSM_STATE)
    d_are, d_aim, d_ldt, d_br, d_bi = s5_disc_bwd("s5_disc_b_" + tag, sp["a_re"], sp["a_im"], sp["log_dt"], sp["br"],
                                                  sp["bi"], d_lr, d_li, d_bbr, d_bbi)
    gr["ssm_a_re"] = d_are.reshape(SSM_GROUPS, SSM_STATE)
    gr["ssm_a_im"] = d_aim.reshape(SSM_GROUPS, SSM_STATE)
    gr["ssm_log_dt"] = d_ldt.reshape(SSM_GROUPS)
    gr["ssm_b_re"] = jnp.swapaxes(d_br, 1, 2)
    gr["ssm_b_im"] = jnp.swapaxes(d_bi, 1, 2)
    dq, dk, dv, dg_rows, db_rows, *slots = delta_bwd("delta_b_" + tag, s["q"], s["k"], s["v"], s["g_rows"],
                                                     s["b_rows"], s["s_all"], s["t_all"], s["u_all"], s["w_sol"], d_o,
                                                     chip_part)
    dbg = jnp.concatenate([db_rows.reshape(DN_HEADS, t).T, dg_rows.reshape(DN_HEADS, t).T,
                           jnp.zeros((t, 128 - 2 * DN_HEADS), f32)], axis=1)
    dc, gr["conv_w"], dbd, dab = conv_qkv_bwd("conv_b_" + tag, s["proj"], p["conv_w"], p["a_log"], p["dt_bias"],
                                              dq, dk, dv, dbg)
    gr["a_log"] = dab[0, DN_HEADS:2 * DN_HEADS]
    gr["dt_bias"] = dab[1, DN_HEADS:2 * DN_HEADS]
    dproj = assemble_dproj("dproj_" + tag, dc, p["conv_w"], dza, du_a, du_b, dzb, dra, drb, dbd)
    d_h = mm_nt("proj_b_" + tag, dproj, p["w_all"])
    gr["w_all"] = mm_nn("proj_w_" + tag, s["h"].T, dproj, bf16)
    dx, gr["norm_pre"] = rmsnorm_bwd("norm_pre_b_" + tag, s["x"], p["norm_pre"], d_h, dy)
    return dx, gr, (slots[0] if slots else None)


REPL = ["norm_pre", "a_log", "dt_bias", "head_norm", "ssm_a_re", "ssm_a_im", "ssm_log_dt", "ssm_b_re", "ssm_b_im",
        "ssm_c_re", "ssm_c_im", "ssm_d", "b_glu", "norm_post"]
SHARDED = ["w_in", "conv_w", "w_glu", "w_out"]
ALL_W = ["norm_pre", "w_in", "conv_w", "a_log", "dt_bias", "head_norm", "ssm_a_re", "ssm_a_im", "ssm_log_dt",
         "ssm_b_re", "ssm_b_im", "ssm_c_re", "ssm_c_im", "ssm_d", "w_glu", "b_glu", "w_out", "norm_post"]
PACK_W = 1024


def _pack_flat(arrs, rows):
    flat = jnp.concatenate([a.reshape(-1) for a in arrs])
    return jnp.pad(flat, (0, rows * PACK_W - flat.shape[0])).reshape(rows, PACK_W)


def _flat_rows(arrs, mult=8):
    n = sum(math.prod(a.shape) for a in arrs)
    rows = -(-n // PACK_W)
    return -(-rows // mult) * mult


def _unpack(flat, shapes):
    out, off = [], 0
    for sh in shapes:
        n = math.prod(sh)
        out.append(flat[off:off + n].reshape(sh))
        off += n
    return out


def _repl_split(shapes):
    big = [n for n in REPL if math.prod(shapes[n]) % PACK_W == 0]
    small = [n for n in REPL if n not in big]
    return big, small


def _rows8(n):
    return -(-n // (8 * PACK_W)) * 8


def _repl_rows(shapes):
    big, small = _repl_split(shapes)
    rows = sum(_rows8(math.prod(shapes[n])) for n in big) + _flat_rows([_sds(shapes[n]) for n in small], 8)
    return -(-rows // (16 * N_DEV)) * (16 * N_DEV)


def _repl_pack(arrs, shapes):
    big, small = _repl_split(shapes)
    parts = []
    for n in big:
        a = arrs[n].reshape(-1, PACK_W)
        parts.append(jnp.pad(a, ((0, _rows8(a.size) - a.shape[0]), (0, 0))))
    parts.append(_pack_flat([arrs[n] for n in small], _flat_rows([_sds(shapes[n]) for n in small], 8)))
    used = sum(p.shape[0] for p in parts)
    parts.append(jnp.zeros((_repl_rows(shapes) - used, PACK_W), parts[0].dtype))
    return jnp.concatenate(parts, axis=0)


def _repl_unpack(packed, shapes):
    big, small = _repl_split(shapes)
    out, off = {}, 0
    for n in big:
        size = math.prod(shapes[n])
        out[n] = packed[off:off + size // PACK_W].reshape(shapes[n])
        off += _rows8(size)
    srows = _flat_rows([_sds(shapes[n]) for n in small], 8)
    out.update(zip(small, _unpack(packed[off:off + srows].reshape(-1), [shapes[n] for n in small])))
    return out


_COL_RUNS = ((0, 4096), (4112, W_COLS), (4096, 4112))


def _w_all(main, tails, wc):
    pieces = []
    for lo, hi in _COL_RUNS:
        for dv in range(N_DEV):
            a, b = max(lo, dv * wc) - dv * wc, min(hi, (dv + 1) * wc) - dv * wc
            if a < min(b, PACK_W):
                pieces.append(main[dv][:, :, a:min(b, PACK_W)])
            if b > max(a, PACK_W):
                pieces.append(tails[dv][:, :, max(a, PACK_W) - PACK_W:b - PACK_W])
    ll, rows = main[0].shape[0], main[0].shape[1]
    pieces.append(jnp.zeros((ll, rows, W_PAD - W_COLS), main[0].dtype))
    return jnp.concatenate(pieces, axis=2)


def _ref_cols(g, lo, hi):
    pieces, off = [], 0
    for a, b in _COL_RUNS:
        s, e = max(lo, a), min(hi, b)
        if s < e:
            pieces.append((s, g[..., off + s - a:off + e - a]))
        off += b - a
    pieces.sort(key=lambda t: t[0])
    return jnp.concatenate([p for _, p in pieces], axis=-1) if len(pieces) > 1 else pieces[0][1]


def kernel(x, norm_pre, w_in, conv_w, a_log, dt_bias, head_norm, ssm_a_re, ssm_a_im, ssm_log_dt, ssm_b_re, ssm_b_im, ssm_c_re, ssm_c_im, ssm_d, w_glu, b_glu, w_out, norm_post, loss_target, m_norm_pre, m_w_in, m_conv_w, m_a_log, m_dt_bias, m_head_norm, m_ssm_a_re, m_ssm_a_im, m_ssm_log_dt, m_ssm_b_re, m_ssm_b_im, m_ssm_c_re, m_ssm_c_im, m_ssm_d, m_w_glu, m_b_glu, m_w_out, m_norm_post, v_norm_pre, v_w_in, v_conv_w, v_a_log, v_dt_bias, v_head_norm, v_ssm_a_re, v_ssm_a_im, v_ssm_log_dt, v_ssm_b_re, v_ssm_b_im, v_ssm_c_re, v_ssm_c_im, v_ssm_d, v_w_glu, v_b_glu, v_w_out, v_norm_post):
    loc = dict(locals())
    w = {n: loc[n] for n in ALL_W}
    m = {n: loc["m_" + n] for n in ALL_W}
    v = {n: loc["v_" + n] for n in ALL_W}
    depth = w_in.shape[0]
    wc = w_in.shape[2]
    cc = conv_w.shape[2]
    wr = w_glu.shape[1]

    tail = wc - PACK_W
    conv_hi = conv_w.astype(bf16)
    conv_mid = (conv_w - conv_hi.astype(f32)).astype(bf16)
    conv_lo = (conv_w - conv_hi.astype(f32) - conv_mid.astype(f32)).astype(bf16)
    conv3 = jnp.stack([conv_hi, conv_mid, conv_lo], axis=1)
    w_in_b = w_in.astype(bf16)
    MISC = 16
    o_glu, o_out, o_tail, o_conv = D_MODEL, D_MODEL + wr, D_MODEL + 2 * wr, D_MODEL + 2 * wr + MISC

    def rows16(a):
        return jnp.pad(a, ((0, MISC - a.shape[0]), (0, PACK_W - a.shape[1])))

    def layer_shard(li):
        return jnp.concatenate([w_in_b[li, :, :PACK_W], w_glu[li].astype(bf16), w_out[li].astype(bf16),
                                rows16(w_in_b[li, :, PACK_W:].T), rows16(conv3[li].reshape(3 * CONV_K, cc))])

    def layer_weights(gathered):
        tails = gathered[:, o_tail:o_tail + tail].reshape(N_DEV * tail, PACK_W).T
        w_all = _w_all([gathered[dv, :D_MODEL][None] for dv in range(N_DEV)],
                       [tails[None, :, dv * tail:(dv + 1) * tail] for dv in range(N_DEV)], wc)[0]
        glu = gathered[:, o_glu:o_glu + wr].reshape(N_DEV * wr, D_MODEL)
        out = gathered[:, o_out:o_out + wr].reshape(N_DEV * wr, D_MODEL)
        conv = gathered[:, o_conv:o_conv + 3 * CONV_K, :cc].astype(f32).reshape(N_DEV, 3, CONV_K, cc)
        conv = (conv[:, 0] + conv[:, 1] + conv[:, 2]).transpose(1, 0, 2).reshape(CONV_K, N_DEV * cc)
        return dict(w_all=w_all, w_glu=glu, w_out=out, conv_w=conv)

    def layer_params(li, gathered):
        return dict(layer_weights(gathered), norm_pre=norm_pre[li].reshape(1, -1),
                    a_log=jnp.pad(a_log[li], (DN_HEADS, 128 - 2 * DN_HEADS)).reshape(1, 128),
                    dt_bias=jnp.pad(dt_bias[li], (DN_HEADS, 128 - 2 * DN_HEADS)).reshape(1, 128),
                    head_norm=head_norm[li].reshape(1, -1), ssm_a_re=ssm_a_re[li], ssm_a_im=ssm_a_im[li],
                    ssm_log_dt=ssm_log_dt[li], ssm_b_re=ssm_b_re[li], ssm_b_im=ssm_b_im[li], ssm_c_re=ssm_c_re[li],
                    ssm_c_im=ssm_c_im[li], ssm_d=ssm_d[li].reshape(1, -1),
                    b_glu=b_glu[li].reshape(1, -1), norm_post=norm_post[li].reshape(1, -1))

    act = x[0]
    saved, params = [], []
    gathered = all_gather("gather_weights", layer_shard(0))
    for li in range(depth):
        params.append(layer_params(li, gathered))
        act, sv, gathered = layer_fwd(act, params[li], li, layer_shard(li + 1) if li + 1 < depth else None)
        saved.append(sv)
    loss_part, dy = loss_head("loss_head", act, loss_target[0])
    repl_shapes = {n: w[n].shape for n in REPL}
    repl_rows = _repl_rows(repl_shapes)
    rr = repl_rows // N_DEV
    my_c = lax.axis_index("c")

    def chip_sums(li, gl, extra=None):
        g_conv = gl["conv_w"].astype(bf16)
        tails = jnp.concatenate([_ref_cols(gl["w_all"], dv * wc + PACK_W, (dv + 1) * wc) for dv in range(N_DEV)],
                                axis=1).T
        blocks = jnp.stack([jnp.concatenate(
            [_ref_cols(gl["w_all"], dv * wc, dv * wc + PACK_W), gl["w_glu"][dv * wr:(dv + 1) * wr],
             gl["w_out"][dv * wr:(dv + 1) * wr], rows16(tails[dv * tail:(dv + 1) * tail]),
             rows16(g_conv[:, dv * cc:(dv + 1) * cc])]
            + ([extra[dv * rr:(dv + 1) * rr]] if extra is not None else []))
            for dv in range(N_DEV)])
        blocks = blocks.reshape(4, 2, blocks.shape[1], PACK_W)
        from_sibling = pair_exchange(f"pair_grads_l{li}", blocks)
        own = lax.dynamic_index_in_dim(blocks, my_c, axis=1, keepdims=False)
        return pair_sum(f"pair_sum_grads_l{li}", own, from_sibling)

    grads, slots, pending = [None] * depth, [None] * depth, None
    for li in reversed(range(depth)):
        dy, grads[li], got = layer_bwd(dy, params[li], saved[li], li, pending)
        if pending is not None:
            slots[li + 1] = got
        g_repl = None
        if li == 0:
            g_repl = _repl_pack({n: jnp.stack([grads[l][n] for l in range(depth)]).reshape(w[n].shape) for n in REPL},
                                repl_shapes).astype(bf16)
        pending = chip_sums(li, grads[li], g_repl)
    slots[0] = chip_exchange("scatter_grads", pending)
    grad_x = dy[None]
    loss = lax.psum(loss_part[0, 0], ("x", "y", "c"))
    mine = [sum_slots(f"sum_grads_l{li}", slots[li]) for li in range(depth)]
    o_repl = o_conv + MISC
    gs_w_in = jnp.stack([jnp.concatenate([mi[:o_glu], mi[o_tail:o_tail + tail].T], axis=1) for mi in mine])
    gs_glu = jnp.stack([mi[o_glu:o_out] for mi in mine])
    gs_out = jnp.stack([mi[o_out:o_tail] for mi in mine])
    gs_conv = jnp.stack([mi[o_conv:o_conv + CONV_K, :cc] for mi in mine])
    g_repl_full = all_gather("gather_repl_grads", mine[0][o_repl:o_repl + rr]).reshape(repl_rows, PACK_W)
    g = _repl_unpack(g_repl_full, repl_shapes)
    g.update(w_in=gs_w_in, w_glu=gs_glu, w_out=gs_out, conv_w=gs_conv)

    delta, new_m, new_v = {}, {}, {}
    for n in SHARDED:
        delta[n], new_m[n], new_v[n] = adamw("adamw_" + n, w[n], g[n], m[n], v[n])
    outs = adamw("adamw_repl", *[_repl_pack({n: src[n] for n in REPL}, repl_shapes) for src in (w, g, m, v)])
    for dst, o in zip((delta, new_m, new_v), outs):
        dst.update(_repl_unpack(o, repl_shapes))
    return (loss, grad_x, *[g[n] for n in ALL_W], *[delta[n] for n in ALL_W], *[new_m[n] for n in ALL_W],
            *[new_v[n] for n in ALL_W])
```

```python
import math

import jax
import jax.numpy as jnp
from jax import lax
from jax.experimental import pallas as pl
from jax.experimental.pallas import tpu as pltpu

f32 = jnp.float32
bf16 = jnp.bfloat16

D_MODEL = 1024
N_DEV = 8
DN_HEADS = 8
HEAD_DIM = 128
CHUNK = 64
CONV_K = 4
SSM_GROUPS = 64
SSM_GROUP = 16
SSM_STATE = 64
SUPER = 8
N_SUPER = SSM_GROUPS // SUPER
SG_STATE = SUPER * SSM_STATE
EPS = 1e-6
W_COLS = 8208
W_PAD = 8448
ADAM_LR, ADAM_B1, ADAM_B2, ADAM_EPS, ADAM_WD, ADAM_STEP = 0.001, 0.9, 0.999, 1e-08, 0.01, 10
VMEM_LIMIT = 56 * 1024 * 1024
MESH = pl.DeviceIdType.MESH
HIGH = lax.Precision.HIGH
DELTA_HB = 8


def _call(body, name, out_shape, grid=None, in_specs=None, out_specs=None, scratch=(), **kw):
    args = dict(out_shape=out_shape, name=name, scratch_shapes=list(scratch),
                compiler_params=pltpu.CompilerParams(vmem_limit_bytes=VMEM_LIMIT, **kw))
    if grid is not None:
        args.update(grid=grid, in_specs=in_specs, out_specs=out_specs)
    else:
        if in_specs is not None:
            args.update(in_specs=in_specs)
        if out_specs is not None:
            args.update(out_specs=out_specs)
    return pl.pallas_call(body, **args)


def _sds(shape, dtype=f32):
    return jax.ShapeDtypeStruct(tuple(shape), dtype)


def _sigmoid(x):
    return 1.0 / (1.0 + jnp.exp(-x))


def _silu(x):
    return x * _sigmoid(x)


_GELU_C = math.sqrt(2.0 / math.pi)


def _gelu(x):
    return 0.5 * x * (1.0 + jnp.tanh(_GELU_C * (x + 0.044715 * x * x * x)))


def _dgelu(x):
    t = jnp.tanh(_GELU_C * (x + 0.044715 * x * x * x))
    return 0.5 * (1.0 + t) + 0.5 * x * (1.0 - t * t) * _GELU_C * (1.0 + 3 * 0.044715 * x * x)


def _softplus(x):
    return jnp.maximum(x, 0.0) + jnp.log(1.0 + jnp.exp(-jnp.abs(x)))


def _bdot(a, b, dn):
    return lax.dot_general(a.astype(bf16), b.astype(bf16), (dn, ((), ())), preferred_element_type=f32)


def _matmul(name, a, b, *, dn, grid, a_spec, b_spec, o_spec, o_shape, o_dtype=f32):
    nk = grid[-1]

    def body(a_ref, b_ref, o_ref, acc_ref):
        p = _bdot(a_ref[...], b_ref[...], dn)
        if nk == 1:
            o_ref[...] = p.astype(o_dtype)
        else:
            k = pl.program_id(len(grid) - 1)

            @pl.when(k == 0)
            def _():
                acc_ref[...] = p

            @pl.when(k > 0)
            def _():
                acc_ref[...] += p

            @pl.when(k == nk - 1)
            def _():
                o_ref[...] = acc_ref[...].astype(o_dtype)

    blk = tuple(d for d in o_spec.block_shape if d is not None)
    return _call(body, name, _sds(o_shape, o_dtype), grid, [a_spec, b_spec], o_spec,
                 scratch=[pltpu.VMEM(blk if nk > 1 else (8, 128), f32)])(a, b)


def _tile(n, pref):
    for t in pref:
        if n % t == 0:
            return t
    return n


def mm_nn(name, a, b, o_dtype=f32):
    m, k = a.shape
    n = b.shape[1]
    tm, tn, tk = _tile(m, (1024, 512, 256)), _tile(n, (2816, 1024, 512)), _tile(k, (2816, 1024))
    return _matmul(name, a, b, dn=((1,), (0,)), grid=(m // tm, n // tn, k // tk),
                   a_spec=pl.BlockSpec((tm, tk), lambda i, j, l: (i, l)),
                   b_spec=pl.BlockSpec((tk, tn), lambda i, j, l: (l, j)),
                   o_spec=pl.BlockSpec((tm, tn), lambda i, j, l: (i, j)), o_shape=(m, n), o_dtype=o_dtype)


def mm_nt(name, a, b):
    m, k = a.shape
    n = b.shape[0]
    tm, tn, tk = _tile(m, (1024, 512, 256)), _tile(n, (1024, 512)), _tile(k, (2816, 1024))
    return _matmul(name, a, b, dn=((1,), (1,)), grid=(m // tm, n // tn, k // tk),
                   a_spec=pl.BlockSpec((tm, tk), lambda i, j, l: (i, l)),
                   b_spec=pl.BlockSpec((tn, tk), lambda i, j, l: (j, l)),
                   o_spec=pl.BlockSpec((tm, tn), lambda i, j, l: (i, j)), o_shape=(m, n))


def mm_tn(name, a, b, o_dtype=f32):
    k, m = a.shape
    n = b.shape[1]
    tm, tn, tk = _tile(m, (512,)), _tile(n, (2816, 1024, 512)), _tile(k, (512, 256))
    return _matmul(name, a, b, dn=((0,), (0,)), grid=(m // tm, n // tn, k // tk),
                   a_spec=pl.BlockSpec((tk, tm), lambda i, j, l: (l, i)),
                   b_spec=pl.BlockSpec((tk, tn), lambda i, j, l: (l, j)),
                   o_spec=pl.BlockSpec((tm, tn), lambda i, j, l: (i, j)), o_shape=(m, n), o_dtype=o_dtype)


def _rows(t, light=False):
    return _tile(t, (512, 256) if light else (256,))


def _row_spec(tm, w):
    return pl.BlockSpec((tm, w), lambda i: (i, 0))


def _acc_spec(r, w):
    return pl.BlockSpec((r, w), lambda i: (0, 0))


def _accumulate(ref, val):
    @pl.when(pl.program_id(0) == 0)
    def _():
        ref[...] = val

    @pl.when(pl.program_id(0) > 0)
    def _():
        ref[...] += val


def rmsnorm_fwd(name, x, gain):
    t, d = x.shape
    tm = _rows(t, light=True)

    def body(x_ref, g_ref, o_ref):
        xv = x_ref[...]
        r = lax.rsqrt(jnp.mean(xv * xv, axis=-1, keepdims=True) + EPS)
        o_ref[...] = (xv * r * g_ref[...]).astype(bf16)

    return _call(body, name, _sds((t, d), bf16), (t // tm,), [_row_spec(tm, d), _acc_spec(1, d)],
                 _row_spec(tm, d))(x, gain)


def rmsnorm_bwd(name, x, gain, dn, dres):
    t, d = x.shape
    tm = _rows(t, light=True)

    def body(x_ref, g_ref, dn_ref, dr_ref, dx_ref, dg_ref):
        xv = x_ref[...]
        r = lax.rsqrt(jnp.mean(xv * xv, axis=-1, keepdims=True) + EPS)
        n = xv * r
        dnv = dn_ref[...]
        _accumulate(dg_ref, jnp.sum(dnv * n, axis=0, keepdims=True))
        dng = dnv * g_ref[...]
        dx_ref[...] = dr_ref[...] + r * (dng - n * jnp.mean(dng * n, axis=-1, keepdims=True))

    return _call(body, name, (_sds((t, d)), _sds((1, d))), (t // tm,),
                 [_row_spec(tm, d), _acc_spec(1, d), _row_spec(tm, d), _row_spec(tm, d)],
                 (_row_spec(tm, d), _acc_spec(1, d)))(x, gain, dn, dres)


def residual_norm_fwd(name, x, out, gain):
    t, d = x.shape
    tm = _rows(t, light=True)

    def body(x_ref, o_ref, g_ref, y_ref):
        ov = o_ref[...]
        r = lax.rsqrt(jnp.mean(ov * ov, axis=-1, keepdims=True) + EPS)
        y_ref[...] = x_ref[...] + ov * r * g_ref[...]

    return _call(body, name, _sds((t, d)), (t // tm,), [_row_spec(tm, d), _row_spec(tm, d), _acc_spec(1, d)],
                 _row_spec(tm, d))(x, out, gain)


def post_norm_bwd(name, out, gain, dy):
    t, d = out.shape
    tm = _rows(t, light=True)

    def body(o_ref, g_ref, dy_ref, do_ref, dg_ref):
        ov = o_ref[...]
        r = lax.rsqrt(jnp.mean(ov * ov, axis=-1, keepdims=True) + EPS)
        n = ov * r
        dyv = dy_ref[...]
        _accumulate(dg_ref, jnp.sum(dyv * n, axis=0, keepdims=True))
        dng = dyv * g_ref[...]
        do_ref[...] = (r * (dng - n * jnp.mean(dng * n, axis=-1, keepdims=True))).astype(bf16)

    return _call(body, name, (_sds((t, d), bf16), _sds((1, d))), (t // tm,),
                 [_row_spec(tm, d), _acc_spec(1, d), _row_spec(tm, d)],
                 (_row_spec(tm, d), _acc_spec(1, d)))(out, gain, dy)


def loss_head(name, y, target):
    t, d = y.shape
    tm = _rows(t, light=True)

    def body(y_ref, t_ref, l_ref, dy_ref):
        e = y_ref[...] - t_ref[...]
        dy_ref[...] = e * (1.0 / d)
        s = jnp.sum(jnp.sum(e * e, axis=1, keepdims=True), axis=0, keepdims=True) * (0.5 / d)
        _accumulate(l_ref, s)

    return _call(body, name, (_sds((1, 1)), _sds((t, d))), (t // tm,),
                 [_row_spec(tm, d), _row_spec(tm, d)], (_acc_spec(1, 1), _row_spec(tm, d)))(y, target)


def _prev_spec(tm, w):
    return pl.BlockSpec((8, w), lambda i: (jnp.maximum(i * (tm // 8) - 1, 0), 0))


def _next_spec(tm, w, t):
    return pl.BlockSpec((8, w), lambda i: (jnp.minimum((i + 1) * (tm // 8), t // 8 - 1), 0))


def _fill_pad(pad_ref, prev_ref, cur_ref, tm):
    keep = (pl.program_id(0) > 0).astype(f32)
    pad_ref[0:8, :] = prev_ref[...] * keep
    pad_ref[8:8 + tm, :] = cur_ref[...]


def _conv_block(pad_ref, w_ref, cb, tm):
    cs = slice(cb * 128, (cb + 1) * 128)
    acc = pad_ref[pl.ds(8 - (CONV_K - 1), tm), cs] * w_ref[0:1, cs]
    for j in range(1, CONV_K):
        acc = acc + pad_ref[pl.ds(8 - (CONV_K - 1) + j, tm), cs] * w_ref[j:j + 1, cs]
    return acc


def conv_qkv_fwd(name, proj, conv_w, a_log, dt_bias):
    t = proj.shape[0]
    tm = _rows(t)
    scale = HEAD_DIM ** -0.5

    def body(cur_ref, prev_ref, w_ref, bd_ref, al_ref, db_ref, q_ref, k_ref, v_ref, bg_ref, pad_ref):
        _fill_pad(pad_ref, prev_ref, cur_ref, tm)
        for cb in range(3 * DN_HEADS):
            s = _silu(_conv_block(pad_ref, w_ref, cb, tm))
            hs = slice((cb % DN_HEADS) * 128, (cb % DN_HEADS + 1) * 128)
            if cb < DN_HEADS:
                q_ref[:, hs] = s * (lax.rsqrt(jnp.sum(s * s, axis=-1, keepdims=True) + EPS) * scale)
            elif cb < 2 * DN_HEADS:
                k_ref[:, hs] = s * lax.rsqrt(jnp.sum(s * s, axis=-1, keepdims=True) + EPS)
            else:
                v_ref[:, hs] = s
        bd = bd_ref[...]
        beta = _sigmoid(bd)
        g = -jnp.exp(al_ref[...]) * _softplus(bd + db_ref[...])
        lane = lax.broadcasted_iota(jnp.int32, bd.shape, 1)
        bg_ref[...] = jnp.where(lane < DN_HEADS, beta, jnp.where(lane < 2 * DN_HEADS, g, 0.0))

    w3 = 3 * D_MODEL
    return _call(body, name, (_sds((t, D_MODEL)),) * 3 + (_sds((t, 128)),), (t // tm,),
                 [pl.BlockSpec((tm, w3), lambda i: (i, 0)), _prev_spec(tm, w3), _acc_spec(CONV_K, w3),
                  pl.BlockSpec((tm, 128), lambda i: (i, 8192 // 128)), _acc_spec(1, 128), _acc_spec(1, 128)],
                 (_row_spec(tm, D_MODEL),) * 3 + (_row_spec(tm, 128),),
                 scratch=[pltpu.VMEM((tm + 8, w3), f32)])(proj, proj, conv_w, proj, a_log, dt_bias)


def conv_qkv_bwd(name, proj, conv_w, a_log, dt_bias, dq, dk, dv, dbg):
    t = proj.shape[0]
    tm = _rows(t)
    scale = HEAD_DIM ** -0.5

    def body(cur_ref, prev_ref, w_ref, bd_ref, al_ref, db_ref, dq_ref, dk_ref, dv_ref, dbg_ref,
             dc_ref, dw_ref, dbd_ref, dab_ref, pad_ref):
        _fill_pad(pad_ref, prev_ref, cur_ref, tm)

        @pl.when(pl.program_id(0) == 0)
        def _():
            dw_ref[...] = jnp.zeros_like(dw_ref)

        for cb in range(3 * DN_HEADS):
            cs = slice(cb * 128, (cb + 1) * 128)
            hs = slice((cb % DN_HEADS) * 128, (cb % DN_HEADS + 1) * 128)
            taps = [pad_ref[pl.ds(8 - (CONV_K - 1) + j, tm), cs] for j in range(CONV_K)]
            c = taps[0] * w_ref[0:1, cs]
            for j in range(1, CONV_K):
                c = c + taps[j] * w_ref[j:j + 1, cs]
            sg = _sigmoid(c)
            s = c * sg
            if cb < 2 * DN_HEADS:
                dn = (dq_ref[:, hs] * scale) if cb < DN_HEADS else dk_ref[:, hs]
                r = lax.rsqrt(jnp.sum(s * s, axis=-1, keepdims=True) + EPS)
                ds = r * dn - s * (r * r * r) * jnp.sum(dn * s, axis=-1, keepdims=True)
            else:
                ds = dv_ref[:, hs]
            dc = ds * (sg * (1.0 + c * (1.0 - sg)))
            dc_ref[:, cs] = dc
            for j in range(CONV_K):
                dw_ref[j:j + 1, cs] += jnp.sum(dc * taps[j], axis=0, keepdims=True)
        bd = bd_ref[...]
        dbg_v = dbg_ref[...]
        lane = lax.broadcasted_iota(jnp.int32, bd.shape, 1)
        sg = _sigmoid(bd)
        ea = jnp.exp(al_ref[...])
        z = bd + db_ref[...]
        sp = _softplus(z)
        is_b = lane < DN_HEADS
        is_g = jnp.logical_and(lane >= DN_HEADS, lane < 2 * DN_HEADS)
        d_z = jnp.where(is_g, dbg_v * (-ea) * _sigmoid(z), 0.0)
        dbd_ref[...] = jnp.where(is_b, dbg_v * sg * (1.0 - sg), d_z)
        d_al = jnp.sum(jnp.where(is_g, dbg_v * (-ea) * sp, 0.0), axis=0, keepdims=True)
        d_db = jnp.sum(d_z, axis=0, keepdims=True)
        _accumulate(dab_ref, jnp.concatenate([d_al, d_db] + [jnp.zeros_like(d_al)] * 6, axis=0))

    w3 = 3 * D_MODEL
    return _call(body, name, (_sds((t, w3)), _sds((CONV_K, w3)), _sds((t, 128)), _sds((8, 128))), (t // tm,),
                 [pl.BlockSpec((tm, w3), lambda i: (i, 0)), _prev_spec(tm, w3), _acc_spec(CONV_K, w3),
                  pl.BlockSpec((tm, 128), lambda i: (i, 8192 // 128)), _acc_spec(1, 128), _acc_spec(1, 128),
                  _row_spec(tm, D_MODEL), _row_spec(tm, D_MODEL), _row_spec(tm, D_MODEL), _row_spec(tm, 128)],
                 (_row_spec(tm, w3), _acc_spec(CONV_K, w3), _row_spec(tm, 128), _acc_spec(8, 128)),
                 scratch=[pltpu.VMEM((tm + 8, w3), f32)])(proj, proj, conv_w, proj, a_log, dt_bias, dq, dk, dv, dbg)


def _bdg(a, b, ca, cb, prec=None):
    if prec is None:
        a, b = a.astype(bf16), b.astype(bf16)
    return lax.dot_general(a, b, (((ca,), (cb,)), ((0,), (0,))), precision=prec, preferred_element_type=f32)


def _bnn(a, b, prec=None):
    return _bdg(a, b, 2, 1, prec)


def _bnt(a, b, prec=None):
    return _bdg(a, b, 2, 2, prec)


def _btn(a, b, prec=None):
    return _bdg(a, b, 1, 1, prec)


def _delta_local(q, k, v, g_row, b_row, solved=None):
    c = CHUNK
    ii = lax.broadcasted_iota(jnp.int32, (c, c), 0)
    jj = lax.broadcasted_iota(jnp.int32, (c, c), 1)
    eye, lower, strict = ii == jj, ii >= jj, ii > jj
    shp = (q.shape[0], c, c)
    g_b = jnp.broadcast_to(g_row, shp)
    gc_col = jnp.sum(jnp.where(lower, g_b, 0.0), axis=2, keepdims=True)
    gc_row = jnp.sum(jnp.where(eye, jnp.broadcast_to(gc_col, shp), 0.0), axis=1, keepdims=True)
    b_col = jnp.sum(jnp.where(eye, jnp.broadcast_to(b_row, shp), 0.0), axis=2, keepdims=True)
    gl = jnp.sum(g_row, axis=2, keepdims=True)
    decay = jnp.exp(jnp.where(lower, gc_col - gc_row, -1e30))
    kb = k * b_col
    kk = _bnt(kb, k)
    lmat = jnp.where(strict, kk * decay, 0.0)
    egc = jnp.exp(gc_col)
    rhs_w = kb * egc
    if solved is None:
        tinv = eye.astype(f32) - lmat
        pw = lmat
        for _ in range(5):
            pw = _bnn(pw, pw, HIGH)
            tinv = tinv + _bnn(tinv, pw, HIGH)
        uw = _bnn(tinv, jnp.concatenate([v * b_col, rhs_w], axis=2), HIGH)
        u, w = uw[:, :, 0:HEAD_DIM], uw[:, :, HEAD_DIM:]
    else:
        tinv, u, w = solved
    qk = _bnt(q, k)
    amat = jnp.where(lower, qk * decay, 0.0)
    q_dec = q * egc
    kdf = jnp.exp(gl - gc_col)
    k_dec = k * kdf
    return dict(eye=eye, lower=lower, strict=strict, gc_col=gc_col, b_col=b_col, gl=gl, decay=decay,
                kb=kb, kk=kk, tinv=tinv, egc=egc, rhs_w=rhs_w, u=u, w=w, qk=qk, amat=amat, q_dec=q_dec, kdf=kdf,
                k_dec=k_dec)


def _heads(ref, hb):
    return jnp.stack([ref[:, h * HEAD_DIM:(h + 1) * HEAD_DIM] for h in range(hb)])


def _put_heads(ref, val):
    for h in range(val.shape[0]):
        ref[:, h * HEAD_DIM:(h + 1) * HEAD_DIM] = val[h]


def delta_fwd(name, q, k, v, g_rows, b_rows, next_shard=None):
    t = q.shape[0]
    nc = t // CHUNK
    hb = DELTA_HB
    fused = next_shard is not None
    assert not fused or (hb == DN_HEADS and nc >= 2)

    def body(*refs):
        if fused:
            (q_ref, k_ref, v_ref, g_ref, b_ref, x_ref, o_ref, s_ref, t_ref, u_ref, w_ref, gath_ref, state, send_sems,
             recv_sems, local_sem) = refs
        else:
            q_ref, k_ref, v_ref, g_ref, b_ref, o_ref, s_ref, t_ref, u_ref, w_ref, state = refs
        n = pl.program_id(1)
        if fused:
            start, forward, finish = _gather_steps(x_ref, gath_ref, send_sems, recv_sems, local_sem)
            pl.when(n == 0)(start)

        @pl.when(n == 0)
        def _():
            state[...] = jnp.zeros_like(state)

        loc = _delta_local(_heads(q_ref, hb), _heads(k_ref, hb), _heads(v_ref, hb), g_ref[:, pl.ds(n, 1), :],
                           b_ref[:, pl.ds(n, 1), :])
        s0 = state[...]
        s_ref[...] = s0
        t_ref[...] = loc["tinv"]
        _put_heads(u_ref, loc["u"])
        _put_heads(w_ref, loc["w"])
        v_new = loc["u"] - _bnn(loc["w"], s0)
        _put_heads(o_ref, _bnn(loc["q_dec"], s0) + _bnn(loc["amat"], v_new))
        state[...] = s0 * jnp.exp(loc["gl"]) + _btn(loc["k_dec"], v_new)
        if fused:
            pl.when(n == (3 * nc) // 4)(forward)
            pl.when(n == nc - 1)(finish)

    tok = pl.BlockSpec((CHUNK, hb * HEAD_DIM), lambda h, n: (n, h))
    row = pl.BlockSpec((hb, nc, CHUNK), lambda h, n: (h, 0, 0))
    any_spec = pl.BlockSpec(memory_space=pl.ANY)
    outs = (_sds((t, D_MODEL)), _sds((DN_HEADS, nc, HEAD_DIM, HEAD_DIM)), _sds((DN_HEADS, nc, CHUNK, CHUNK)),
            _sds((t, D_MODEL)), _sds((t, D_MODEL)))
    out_specs = (tok, pl.BlockSpec((hb, None, HEAD_DIM, HEAD_DIM), lambda h, n: (h, n, 0, 0)),
                 pl.BlockSpec((hb, None, CHUNK, CHUNK), lambda h, n: (h, n, 0, 0)), tok, tok)
    in_specs, args = [tok, tok, tok, row, row], (q, k, v, g_rows, b_rows)
    scratch = [pltpu.VMEM((hb, HEAD_DIM, HEAD_DIM), f32)]
    if fused:
        outs += (_sds((N_DEV,) + next_shard.shape, next_shard.dtype),)
        out_specs += (any_spec,)
        in_specs, args = in_specs + [any_spec], args + (next_shard,)
        scratch = scratch + list(_GATHER_SEMS)
    return _call(body, name, outs, (DN_HEADS // hb, nc), in_specs, out_specs, scratch=scratch)(*args)


def delta_bwd(name, q, k, v, g_rows, b_rows, s_all, t_all, u_all, w_all, do, chip_part=None):
    t = q.shape[0]
    nc = t // CHUNK
    c = CHUNK
    hb = DELTA_HB

    fused = chip_part is not None
    assert not fused or hb == DN_HEADS

    def body(*refs):
        if fused:
            (q_ref, k_ref, v_ref, g_ref, b_ref, s_ref, t_ref, u_ref, w_ref, do_ref, x_ref, dq_ref, dk_ref, dv_ref,
             dg_ref, db_ref, slot_ref, dstate, send_sems, recv_sems, local_sem) = refs
        else:
            (q_ref, k_ref, v_ref, g_ref, b_ref, s_ref, t_ref, u_ref, w_ref, do_ref, dq_ref, dk_ref, dv_ref, dg_ref,
             db_ref, dstate) = refs
        step = pl.program_id(1)
        n = nc - 1 - step
        if fused:
            start, finish = _chip_exchange_steps(x_ref, slot_ref, send_sems, recv_sems, local_sem)
            pl.when(step == 0)(start)

        @pl.when(step == 0)
        def _():
            dstate[...] = jnp.zeros_like(dstate)

        qv, kv, vv = _heads(q_ref, hb), _heads(k_ref, hb), _heads(v_ref, hb)
        L = _delta_local(qv, kv, vv, g_ref[:, pl.ds(n, 1), :], b_ref[:, pl.ds(n, 1), :],
                         solved=(t_ref[...], _heads(u_ref, hb), _heads(w_ref, hb)))
        eye, lower, strict = L["eye"], L["lower"], L["strict"]
        shp = (hb, c, c)
        s0 = s_ref[...]
        dov = _heads(do_ref, hb)
        ds = dstate[...]
        eg = jnp.exp(L["gl"])
        v_new = L["u"] - _bnn(L["w"], s0)
        d_k_dec = _bnt(v_new, ds)
        d_v_new = _bnn(L["k_dec"], ds) + _btn(L["amat"], dov)
        d_eg = jnp.sum(jnp.sum(ds * s0, axis=2, keepdims=True), axis=1, keepdims=True)
        d_q_dec = _bnt(dov, s0)
        d_a = _bnt(dov, v_new)
        d_w = -_bnt(d_v_new, s0)
        dstate[...] = ds * eg + _btn(L["q_dec"], dov) - _btn(L["w"], d_v_new)
        d_am = jnp.where(lower, d_a * L["decay"], 0.0)
        dq = _bnn(d_am, kv) + d_q_dec * L["egc"]
        dk = _btn(d_am, qv) + d_k_dec * L["kdf"]
        e_col = jnp.sum(d_k_dec * L["k_dec"], axis=2, keepdims=True)
        d_gc_col = jnp.sum(d_q_dec * L["q_dec"], axis=2, keepdims=True) - e_col
        d_gl = jnp.sum(e_col, axis=1, keepdims=True) + d_eg * eg
        tinv = L["tinv"]
        d_rhs = _btn(tinv, jnp.concatenate([d_v_new, d_w], axis=2), HIGH)
        d_rhs_u, d_rhs_w = d_rhs[:, :, 0:HEAD_DIM], d_rhs[:, :, HEAD_DIM:]
        d_l = -_bnt(d_rhs, jnp.concatenate([L["u"], L["w"]], axis=2), HIGH)
        _put_heads(dv_ref, d_rhs_u * L["b_col"])
        d_b_col = jnp.sum(d_rhs_u * vv, axis=2, keepdims=True)
        d_gc_col = d_gc_col + jnp.sum(d_rhs_w * L["rhs_w"], axis=2, keepdims=True)
        d_lm = jnp.where(strict, d_l * L["decay"], 0.0)
        d_kb = d_rhs_w * L["egc"] + _bnn(d_lm, kv)
        dk = dk + _btn(d_lm, L["kb"]) + d_kb * L["b_col"]
        d_b_col = d_b_col + jnp.sum(d_kb * kv, axis=2, keepdims=True)
        m = d_am * L["qk"] + d_lm * L["kk"]
        d_gc_col = d_gc_col + jnp.sum(m, axis=2, keepdims=True)
        d_gc_row = (jnp.sum(jnp.where(eye, jnp.broadcast_to(d_gc_col, shp), 0.0), axis=1, keepdims=True)
                    - jnp.sum(m, axis=1, keepdims=True))
        lane = lax.broadcasted_iota(jnp.int32, (1, 1, c), 2)
        d_gc_row = d_gc_row + jnp.where(lane == c - 1, d_gl, 0.0)
        d_gc_tot = jnp.sum(jnp.where(eye, jnp.broadcast_to(d_gc_row, shp), 0.0), axis=2, keepdims=True)
        dg_ref[:, pl.ds(n, 1), :] = jnp.sum(jnp.where(lower, jnp.broadcast_to(d_gc_tot, shp), 0.0), axis=1,
                                            keepdims=True)
        db_ref[:, pl.ds(n, 1), :] = jnp.sum(jnp.where(eye, jnp.broadcast_to(d_b_col, shp), 0.0), axis=1,
                                            keepdims=True)
        _put_heads(dq_ref, dq)
        _put_heads(dk_ref, dk)
        if fused:
            pl.when(step == nc - 1)(finish)

    tok = pl.BlockSpec((CHUNK, hb * HEAD_DIM), lambda h, s: (nc - 1 - s, h))
    row = pl.BlockSpec((hb, nc, CHUNK), lambda h, s: (h, 0, 0))
    any_spec = pl.BlockSpec(memory_space=pl.ANY)
    in_specs = [tok, tok, tok, row, row,
                pl.BlockSpec((hb, None, HEAD_DIM, HEAD_DIM), lambda h, s: (h, nc - 1 - s, 0, 0)),
                pl.BlockSpec((hb, None, CHUNK, CHUNK), lambda h, s: (h, nc - 1 - s, 0, 0)), tok, tok, tok]
    args = (q, k, v, g_rows, b_rows, s_all, t_all, u_all, w_all, do)
    outs = (_sds((t, D_MODEL)),) * 3 + (_sds((DN_HEADS, nc, CHUNK)),) * 2
    out_specs = (tok, tok, tok, row, row)
    scratch = [pltpu.VMEM((hb, HEAD_DIM, HEAD_DIM), f32)]
    if fused:
        in_specs, args = in_specs + [any_spec], args + (chip_part,)
        outs += (_sds(chip_part.shape, chip_part.dtype),)
        out_specs += (any_spec,)
        scratch = scratch + list(_CHIP_SEMS)
    return _call(body, name, outs, (DN_HEADS // hb, nc), in_specs, out_specs, scratch=scratch)(*args)


SEG = 8


def _perm_rows(a):
    t, c = a.shape
    return a.reshape(SEG, t // SEG, c).transpose(1, 0, 2).reshape(t, c)


def _unperm_rows(a):
    t, c = a.shape
    return a.reshape(t // SEG, SEG, c).transpose(1, 0, 2).reshape(t, c)


def _cmul(ar, ai, br, bi):
    return ar * br - ai * bi, ar * bi + ai * br


def _segment_init(er, ei, lr, li, seg_len, reverse):
    w = er.shape[1]
    sub = lax.broadcasted_iota(jnp.int32, (SEG, w), 0)

    def shift(x, k):
        if reverse:
            return jnp.where(sub < SEG - k, pltpu.roll(x, SEG - k, 0), 0.0)
        return jnp.where(sub >= k, pltpu.roll(x, k, 0), 0.0)

    pr, pi = lr, li
    for _ in range(seg_len.bit_length() - 1):
        pr, pi = _cmul(pr, pi, pr, pi)
    fr, fi = shift(er, 1), shift(ei, 1)
    for k in (1, 2, 4):
        sr, si = shift(fr, k), shift(fi, k)
        mr, mi = _cmul(pr, pi, sr, si)
        fr, fi = fr + mr, fi + mi
        pr, pi = _cmul(pr, pi, pr, pi)
    return fr, fi


def s5_fwd(name, u_perm, bbd, cbd, lam):
    t = u_perm.shape[0]
    tt = _tile(t, (2048, 1024, 512, 256, 128))
    nt, ng, w = t // tt, tt // SEG, SG_STATE
    seg_len = t // SEG
    assert seg_len & (seg_len - 1) == 0 and tt % SEG == 0

    def body(u_ref, b_ref, c_ref, lam_ref, y_ref, h_ref, x_scr, state):
        p, i = pl.program_id(1), pl.program_id(2)
        lr1, li1 = lam_ref[:, 0:w], lam_ref[:, w:2 * w]
        lr, li = jnp.broadcast_to(lr1, (SEG, w)), jnp.broadcast_to(li1, (SEG, w))
        base = pl.multiple_of(i * tt, tt)

        @pl.when(p == 0)
        def _():
            x_scr[pl.ds(base, tt), :] = _bdot(u_ref[...], b_ref[...], ((1,), (0,)))

        @pl.when(jnp.logical_and(p == 0, i == 0))
        def _():
            state[...] = jnp.zeros_like(state)

        @pl.when(jnp.logical_and(p == 1, i == 0))
        def _():
            sr, si = _segment_init(state[:, 0:w], state[:, w:2 * w], lr1, li1, seg_len, False)
            state[:, 0:w] = sr
            state[:, w:2 * w] = si

        def run(store):
            def step(g, st):
                row = pl.multiple_of(g * SEG, SEG)
                xg = x_scr[pl.ds(base + row, SEG), :]
                nr = lr * st[0] - li * st[1] + xg[:, 0:w]
                ni = lr * st[1] + li * st[0] + xg[:, w:2 * w]
                if store:
                    h_ref[pl.ds(row, SEG), 0:w] = nr
                    h_ref[pl.ds(row, SEG), w:2 * w] = ni
                return nr, ni

            fin = lax.fori_loop(0, ng, step, (state[:, 0:w], state[:, w:2 * w]))
            state[:, 0:w] = fin[0]
            state[:, w:2 * w] = fin[1]

        @pl.when(p == 0)
        def _():
            run(False)

        @pl.when(p == 1)
        def _():
            run(True)
            y_ref[...] = _bdot(h_ref[...], c_ref[...], ((1,), (0,)))

    return _call(body, name, (_sds((t, D_MODEL)), _sds((t, N_SUPER * 2 * w))), (N_SUPER, 2, nt),
                 [pl.BlockSpec((tt, 128), lambda s, p, i: (i * (1 - p) + (nt - 1) * p, s)),
                  pl.BlockSpec((None, 128, 2 * w), lambda s, p, i: (s, 0, 0)),
                  pl.BlockSpec((None, 2 * w, 128), lambda s, p, i: (s, 0, 0)),
                  pl.BlockSpec((None, 1, 2 * w), lambda s, p, i: (s, 0, 0))],
                 (pl.BlockSpec((tt, 128), lambda s, p, i: (i * p, s)),
                  pl.BlockSpec((tt, 2 * w), lambda s, p, i: (i * p, s))),
                 scratch=[pltpu.VMEM((t, 2 * w), f32), pltpu.VMEM((SEG, 2 * w), f32)])(u_perm, bbd, cbd, lam)


def s5_bwd(name, dy_perm, u_perm, h_perm, hprev0, bbd, cbd, lam, pair_blocks=None):
    t = u_perm.shape[0]
    tt = _tile(t, (2048, 1024, 512, 256, 128))
    nt, ng, w = t // tt, tt // SEG, SG_STATE
    seg_len = t // SEG
    fused = pair_blocks is not None

    def body(*refs):
        if not fused:
            return compute(*refs)
        x_ref, sib_ref, send_sems, recv_sems = refs[8], refs[13], refs[17], refs[18]
        s_, p_, i_ = pl.program_id(0), pl.program_id(1), pl.program_id(2)
        x, y, cc = _coords()
        cps = [pltpu.make_async_remote_copy(src_ref=x_ref.at[ch, 1 - cc], dst_ref=sib_ref.at[ch],
                                            send_sem=send_sems.at[ch], recv_sem=recv_sems.at[ch],
                                            device_id=(x, y, 1 - cc), device_id_type=MESH) for ch in range(4)]

        @pl.when(jnp.logical_and(s_ == 0, jnp.logical_and(p_ == 0, i_ == 0)))
        def _():
            for cp in cps:
                cp.start()

        compute(*refs[:8], *refs[9:13], *refs[14:17])

        @pl.when(jnp.logical_and(s_ == N_SUPER - 1, jnp.logical_and(p_ == 1, i_ == nt - 1)))
        def _():
            for cp in cps:
                cp.wait()

    def compute(dy_ref, u_ref, h_ref, hp_ref, hp0_ref, b_ref, c_ref, lam_ref, du_ref, db_ref, dc_ref, dl_ref,
                g_scr, state, dl_acc):
        p, i = pl.program_id(1), pl.program_id(2)
        first_tile = jnp.logical_or(p == 0, i == nt - 1)
        lr1, li1 = lam_ref[:, 0:w], -lam_ref[:, w:2 * w]
        lr, li = jnp.broadcast_to(lr1, (SEG, w)), jnp.broadcast_to(li1, (SEG, w))
        base = pl.multiple_of((nt - 1 - i) * tt, tt)

        @pl.when(p == 0)
        def _():
            g_scr[pl.ds(base, tt), :] = _bdot(dy_ref[...], c_ref[...], ((1,), (1,)))

        @pl.when(jnp.logical_and(p == 0, i == 0))
        def _():
            state[...] = jnp.zeros_like(state)

        @pl.when(jnp.logical_and(p == 1, i == 0))
        def _():
            sr, si = _segment_init(state[:, 0:w], state[:, w:2 * w], lr1, li1, seg_len, True)
            state[:, 0:w] = sr
            state[:, w:2 * w] = si
            dl_acc[...] = jnp.zeros_like(dl_acc)

        def adj(g, st):
            row = pl.multiple_of(g * SEG, SEG)
            gg = g_scr[pl.ds(base + row, SEG), :]
            nr = lr * st[0] - li * st[1] + gg[:, 0:w]
            ni = lr * st[1] + li * st[0] + gg[:, w:2 * w]
            return row, nr, ni

        @pl.when(p == 0)
        def _():
            def step(k, st):
                _, nr, ni = adj(ng - 1 - k, st)
                return nr, ni

            fin = lax.fori_loop(0, ng, step, (state[:, 0:w], state[:, w:2 * w]))
            state[:, 0:w] = fin[0]
            state[:, w:2 * w] = fin[1]

        @pl.when(p == 1)
        def _():
            above = jnp.where(first_tile, hp0_ref[...], hp_ref[...])

            def step(k, st):
                g = ng - 1 - k
                row, nr, ni = adj(g, st)
                g_scr[pl.ds(base + row, SEG), 0:w] = nr
                g_scr[pl.ds(base + row, SEG), w:2 * w] = ni
                prow = pl.multiple_of(jnp.maximum(g - 1, 0) * SEG, SEG)
                hp = jnp.where(g > 0, h_ref[pl.ds(prow, SEG), :], above)
                pr, pi = hp[:, 0:w], hp[:, w:2 * w]
                return nr, ni, st[2] + nr * pr + ni * pi, st[3] + ni * pr - nr * pi

            fin = lax.fori_loop(0, ng, step, (state[:, 0:w], state[:, w:2 * w], dl_acc[:, 0:w], dl_acc[:, w:2 * w]))
            state[:, 0:w] = fin[0]
            state[:, w:2 * w] = fin[1]
            dl_acc[:, 0:w] = fin[2]
            dl_acc[:, w:2 * w] = fin[3]
            a = g_scr[pl.ds(base, tt), :]
            du_ref[...] = _bdot(a, b_ref[...], ((1,), (1,)))
            d_b = _bdot(u_ref[...], a, ((0,), (0,)))
            d_c = _bdot(dy_ref[...], h_ref[...], ((0,), (0,)))

            @pl.when(i == 0)
            def _():
                db_ref[...] = d_b
                dc_ref[...] = d_c

            @pl.when(i > 0)
            def _():
                db_ref[...] += d_b
                dc_ref[...] += d_c

            @pl.when(i == nt - 1)
            def _():
                dl_ref[...] = jnp.sum(dl_acc[...], axis=0, keepdims=True)

    tile = lambda s, p, i: (nt - 1 - i, s)
    tile1 = lambda s, p, i: (nt - 1 - i * p, s)
    above = lambda s, p, i: (jnp.maximum((nt - 1 - i * p) * (tt // SEG) - 1, 0), s)
    per_s = lambda s, p, i: (s, 0, 0)
    any_spec = pl.BlockSpec(memory_space=pl.ANY)
    outs = (_sds((t, D_MODEL)), _sds((N_SUPER, 128, 2 * w)), _sds((N_SUPER, 128, 2 * w)), _sds((N_SUPER, 1, 2 * w)))
    in_specs = [pl.BlockSpec((tt, 128), tile), pl.BlockSpec((tt, 128), tile1), pl.BlockSpec((tt, 2 * w), tile1),
                pl.BlockSpec((SEG, 2 * w), above), pl.BlockSpec((SEG, 2 * w), lambda s, p, i: (0, s)),
                pl.BlockSpec((None, 128, 2 * w), per_s), pl.BlockSpec((None, 2 * w, 128), per_s),
                pl.BlockSpec((None, 1, 2 * w), per_s)]
    out_specs = (pl.BlockSpec((tt, 128), tile1), pl.BlockSpec((None, 128, 2 * w), per_s),
                 pl.BlockSpec((None, 128, 2 * w), per_s), pl.BlockSpec((None, 1, 2 * w), per_s))
    args = (dy_perm, u_perm, h_perm, h_perm, hprev0, bbd, cbd, lam)
    scratch = [pltpu.VMEM((t, 2 * w), f32), pltpu.VMEM((SEG, 2 * w), f32), pltpu.VMEM((SEG, 2 * w), f32)]
    if fused:
        in_specs, args = in_specs + [any_spec], args + (pair_blocks,)
        outs += (_sds((4,) + pair_blocks.shape[2:], pair_blocks.dtype),)
        out_specs += (any_spec,)
        scratch = scratch + [pltpu.SemaphoreType.DMA((4,)), pltpu.SemaphoreType.DMA((4,))]
    return _call(body, name, outs, (N_SUPER, 2, nt), in_specs, out_specs, scratch=scratch)(*args)


def _s5_disc(a_re, a_im, log_dt, br, bi):
    dt = jnp.exp(log_dt)
    mag = jnp.exp(a_re * dt)
    lr, li = mag * jnp.cos(a_im * dt), mag * jnp.sin(a_im * dt)
    den = a_re * a_re + a_im * a_im
    fr = ((lr - 1.0) * a_re + li * a_im) / den
    fi = (li * a_re - (lr - 1.0) * a_im) / den
    return lr, li, fr * br - fi * bi, fr * bi + fi * br


def s5_disc_fwd(name, a_re, a_im, log_dt, br, bi):
    g, n = SSM_GROUPS, SSM_STATE

    def body(ar, ai, ld, brr, bir, lr, li, bbr, bbi):
        o = _s5_disc(ar[...], ai[...], ld[...], brr[...], bir[...])
        lr[...], li[...], bbr[...], bbi[...] = o

    return _call(body, name, (_sds((g, 1, n)), _sds((g, 1, n)), _sds((g, SSM_GROUP, n)), _sds((g, SSM_GROUP, n))))(
        a_re, a_im, log_dt, br, bi)


def s5_disc_bwd(name, a_re, a_im, log_dt, br, bi, d_lr, d_li, d_bbr, d_bbi):
    g, n = SSM_GROUPS, SSM_STATE

    def body(ar, ai, ld, brr, bir, c1, c2, c3, c4, o1, o2, o3, o4, o5):
        _, vjp = jax.vjp(_s5_disc, ar[...], ai[...], ld[...], brr[...], bir[...])
        o1[...], o2[...], o3[...], o4[...], o5[...] = vjp((c1[...], c2[...], c3[...], c4[...]))

    return _call(body, name, (_sds((g, 1, n)), _sds((g, 1, n)), _sds((g, 1, 1)), _sds((g, SSM_GROUP, n)),
                              _sds((g, SSM_GROUP, n))))(a_re, a_im, log_dt, br, bi, d_lr, d_li, d_bbr, d_bbi)


def gelu_fwd(name, ys_lin, proj, d_skip):
    t, d = ys_lin.shape
    tm = _rows(t, light=True)

    def body(y_ref, u_ref, d_ref, o_ref):
        o_ref[...] = _gelu(y_ref[...] + d_ref[...] * u_ref[...])

    return _call(body, name, _sds((t, d)), (t // tm,),
                 [_row_spec(tm, d), pl.BlockSpec((tm, d), lambda i: (i, 4)), _acc_spec(1, d)],
                 _row_spec(tm, d))(ys_lin, proj, d_skip)


def _head_norm(o, hn):
    outs, ns, rs = [], [], []
    for h in range(DN_HEADS):
        oh = o[:, h * 128:(h + 1) * 128]
        r = lax.rsqrt(jnp.mean(oh * oh, axis=-1, keepdims=True) + EPS)
        n = oh * r
        ns.append(n)
        rs.append(r)
        outs.append(n * hn)
    return outs, ns, rs


def merge_fwd(name, proj, o, yg, glu_lin, head_norm, b_glu):
    t = o.shape[0]
    tm = _rows(t)
    d = D_MODEL

    def body(za_ref, zb_ref, ra_ref, rb_ref, o_ref, yg_ref, gl_ref, hn_ref, bg_ref, m_ref):
        on, _, _ = _head_norm(o_ref[...], hn_ref[...])
        y_a = jnp.concatenate(on, axis=1) * _silu(za_ref[...])
        y_b = yg_ref[...] * _sigmoid(gl_ref[...] + bg_ref[...]) * _silu(zb_ref[...])
        m_ref[...] = (_sigmoid(ra_ref[...]) * y_a + _sigmoid(rb_ref[...]) * y_b).astype(bf16)

    pc = lambda c: pl.BlockSpec((tm, d), lambda i: (i, c))
    return _call(body, name, _sds((t, d), bf16), (t // tm,),
                 [pc(3), pc(5), pc(6), pc(7), _row_spec(tm, d), _row_spec(tm, d), _row_spec(tm, d),
                  _acc_spec(1, 128), _acc_spec(1, d)], _row_spec(tm, d))(
        proj, proj, proj, proj, o, yg, glu_lin, head_norm, b_glu)


def merge_bwd(name, proj, o, yg, glu_lin, head_norm, b_glu, dm):
    t = o.shape[0]
    tm = _rows(t)
    d = D_MODEL

    def body(za_ref, zb_ref, ra_ref, rb_ref, o_ref, yg_ref, gl_ref, hn_ref, bg_ref, dm_ref,
             dza_ref, dzb_ref, dra_ref, drb_ref, do_ref, dgl_ref, dyg_ref, dhn_ref, dbg_ref):
        hn = hn_ref[...]
        za, zb, ra, rb = za_ref[...], zb_ref[...], ra_ref[...], rb_ref[...]
        on, ns, rs = _head_norm(o_ref[...], hn)
        onc = jnp.concatenate(on, axis=1)
        sga, sgb = _sigmoid(za), _sigmoid(zb)
        sza = za * sga
        y_a = onc * sza
        yg = yg_ref[...]
        sgl = _sigmoid(gl_ref[...] + bg_ref[...])
        y2 = yg * sgl
        szb = zb * sgb
        y_b = y2 * szb
        sra, srb = _sigmoid(ra), _sigmoid(rb)
        dmv = dm_ref[...]
        dra_ref[...] = (dmv * y_a * sra * (1.0 - sra)).astype(bf16)
        drb_ref[...] = (dmv * y_b * srb * (1.0 - srb)).astype(bf16)
        d_ya = dmv * sra
        d_yb = dmv * srb
        dza_ref[...] = (d_ya * onc * (sga * (1.0 + za * (1.0 - sga)))).astype(bf16)
        dzb_ref[...] = (d_yb * y2 * (sgb * (1.0 + zb * (1.0 - sgb)))).astype(bf16)
        d_on = d_ya * sza
        d_y2 = d_yb * szb
        dyg_ref[...] = d_y2 * sgl
        d_gl = d_y2 * yg * sgl * (1.0 - sgl)
        dgl_ref[...] = d_gl.astype(bf16)
        _accumulate(dbg_ref, jnp.sum(d_gl, axis=0, keepdims=True))
        d_hn = jnp.zeros((1, 128), f32)
        for h in range(DN_HEADS):
            hs = slice(h * 128, (h + 1) * 128)
            dh = d_on[:, hs]
            d_hn = d_hn + jnp.sum(dh * ns[h], axis=0, keepdims=True)
            dn = dh * hn
            do_ref[:, hs] = rs[h] * (dn - ns[h] * jnp.mean(dn * ns[h], axis=-1, keepdims=True))
        _accumulate(dhn_ref, d_hn)

    pc = lambda c: pl.BlockSpec((tm, d), lambda i: (i, c))
    rs_ = _row_spec(tm, d)
    return _call(body, name, (_sds((t, d), bf16),) * 4 + (_sds((t, d)), _sds((t, d), bf16), _sds((t, d)),
                              _sds((1, 128)), _sds((1, d))),
                 (t // tm,), [pc(3), pc(5), pc(6), pc(7), rs_, rs_, rs_, _acc_spec(1, 128), _acc_spec(1, d), rs_],
                 (rs_,) * 7 + (_acc_spec(1, 128), _acc_spec(1, d)))(
        proj, proj, proj, proj, o, yg, glu_lin, head_norm, b_glu, dm)


def gelu_bwd(name, ys_lin, proj, d_skip, dyg_a, dyg_b):
    t, d = ys_lin.shape
    tm = _rows(t, light=True)

    def body(y_ref, u_ref, d_ref, a_ref, b_ref, dys_ref, du_ref, dd_ref):
        uv = u_ref[...]
        dys = (a_ref[...] + b_ref[...]) * _dgelu(y_ref[...] + d_ref[...] * uv)
        dys_ref[...] = dys
        du_ref[...] = dys * d_ref[...]
        _accumulate(dd_ref, jnp.sum(dys * uv, axis=0, keepdims=True))

    rs_ = _row_spec(tm, d)
    return _call(body, name, (_sds((t, d)), _sds((t, d)), _sds((1, d))), (t // tm,),
                 [rs_, pl.BlockSpec((tm, d), lambda i: (i, 4)), _acc_spec(1, d), rs_, rs_],
                 (rs_, rs_, _acc_spec(1, d)))(ys_lin, proj, d_skip, dyg_a, dyg_b)


def assemble_dproj(name, dc, conv_w, dza, du_a, du_b, dzb, dra, drb, dbd):
    t = dza.shape[0]
    tm = _rows(t)
    nt = t // tm
    d = D_MODEL
    w3 = 3 * d

    def body(cur_ref, nxt_ref, w_ref, za_ref, ua_ref, ub_ref, zb_ref, ra_ref, rb_ref, bd_ref, o_ref, pad_ref):
        keep = (pl.program_id(0) < nt - 1).astype(f32)
        pad_ref[0:tm, :] = cur_ref[...]
        pad_ref[tm:tm + 8, :] = nxt_ref[...] * keep
        for cb in range(w3 // 128):
            cs = slice(cb * 128, (cb + 1) * 128)
            acc = pad_ref[pl.ds(CONV_K - 1, tm), cs] * w_ref[0:1, cs]
            for j in range(1, CONV_K):
                acc = acc + pad_ref[pl.ds(CONV_K - 1 - j, tm), cs] * w_ref[j:j + 1, cs]
            o_ref[:, cs] = acc.astype(bf16)
        o_ref[:, 3 * d:4 * d] = za_ref[...].astype(bf16)
        o_ref[:, 4 * d:5 * d] = (ua_ref[...] + ub_ref[...]).astype(bf16)
        o_ref[:, 5 * d:6 * d] = zb_ref[...].astype(bf16)
        o_ref[:, 6 * d:7 * d] = ra_ref[...].astype(bf16)
        o_ref[:, 7 * d:8 * d] = rb_ref[...].astype(bf16)
        o_ref[:, 8 * d:8 * d + 128] = bd_ref[...].astype(bf16)
        o_ref[:, 8 * d + 128:W_PAD] = jnp.zeros((tm, W_PAD - 8 * d - 128), bf16)

    rs_ = _row_spec(tm, d)
    return _call(body, name, _sds((t, W_PAD), bf16), (nt,),
                 [_row_spec(tm, w3), _next_spec(tm, w3, t), _acc_spec(CONV_K, w3), rs_, rs_, rs_, rs_, rs_, rs_,
                  _row_spec(tm, 128)], _row_spec(tm, W_PAD),
                 scratch=[pltpu.VMEM((tm + 8, w3), f32)])(dc, dc, conv_w, dza, du_a, du_b, dzb, dra, drb, dbd)


def adamw(name, w, g, m, v):
    lead, (r, c) = w.shape[:-2], w.shape[-2:]
    tm = _tile(r, (512, 256, 128, 64, 32, 16, 8))
    c1 = 1.0 / (1.0 - ADAM_B1 ** ADAM_STEP)
    c2 = 1.0 / (1.0 - ADAM_B2 ** ADAM_STEP)

    def body(w_ref, g_ref, m_ref, v_ref, d_ref, nm_ref, nv_ref):
        gv = g_ref[...]
        nm = ADAM_B1 * m_ref[...] + (1.0 - ADAM_B1) * gv
        nv = ADAM_B2 * v_ref[...] + (1.0 - ADAM_B2) * (gv * gv)
        d_ref[...] = -ADAM_LR * ((nm * c1) / (jnp.sqrt(nv * c2) + ADAM_EPS) + ADAM_WD * w_ref[...])
        nm_ref[...] = nm
        nv_ref[...] = nv

    if lead:
        sp = pl.BlockSpec((None, tm, c), lambda l, i: (l, i, 0))
        grid = (lead[0], r // tm)
    else:
        sp = pl.BlockSpec((tm, c), lambda i: (i, 0))
        grid = (r // tm,)
    return _call(body, name, (_sds(w.shape),) * 3, grid, [sp] * 4, (sp,) * 3)(w, g, m, v)


def _coords():
    return lax.axis_index("x"), lax.axis_index("y"), lax.axis_index("c")


def _lin(dev):
    return 4 * dev[0] + 2 * dev[1] + dev[2]


def _chips(me):
    x, y, _ = me
    return [(1 - x, y), (x, 1 - y), (1 - x, 1 - y)]


def _gather_steps(x_ref, o_ref, send_sems, recv_sems, local_sem):
    me = _coords()
    x, y, cc = me
    sibling = (x, y, 1 - cc)
    chips = _chips(me)

    def copy(k, block, to, src=None):
        return pltpu.make_async_remote_copy(
            src_ref=o_ref.at[_lin(block)] if src is None else src, dst_ref=o_ref.at[_lin(block)],
            send_sem=send_sems.at[k], recv_sem=recv_sems.at[k], device_id=to, device_id_type=MESH)

    mine = pltpu.make_async_copy(x_ref, o_ref.at[_lin(me)], local_sem)
    first = [copy(0, me, sibling, src=x_ref)] + [copy(1 + j, me, (*chip, cc), src=x_ref)
                                                 for j, chip in enumerate(chips)]
    passed = [copy(4 + j, (*chip, cc), sibling) for j, chip in enumerate(chips)]

    def start():
        mine.start()
        for cp in first:
            cp.start()

    def forward():
        for j, chip in enumerate(chips):
            copy(1 + j, (*chip, cc), me).wait_recv()
            passed[j].start()

    def finish():
        copy(0, sibling, me).wait_recv()
        for j, chip in enumerate(chips):
            copy(4 + j, (*chip, 1 - cc), me).wait_recv()
        for cp in first + passed:
            cp.wait_send()
        mine.wait()

    return start, forward, finish


_GATHER_SEMS = [pltpu.SemaphoreType.DMA((N_DEV - 1,)), pltpu.SemaphoreType.DMA((N_DEV - 1,)),
                pltpu.SemaphoreType.DMA(())]


def all_gather(name, shard):
    r, c = shard.shape

    def body(x_ref, o_ref, send_sems, recv_sems, local_sem):
        for step in _gather_steps(x_ref, o_ref, send_sems, recv_sems, local_sem):
            step()

    any_spec = pl.BlockSpec(memory_space=pl.ANY)
    return _call(body, name, _sds((N_DEV, r, c), shard.dtype), in_specs=[any_spec], out_specs=any_spec,
                 scratch=list(_GATHER_SEMS))(shard)


def pair_exchange(name, blocks):
    _, _, r, c = blocks.shape

    def body(x_ref, o_ref, send_sems, recv_sems):
        x, y, cc = _coords()
        sibling = (x, y, 1 - cc)
        cps = [pltpu.make_async_remote_copy(src_ref=x_ref.at[ch, 1 - cc], dst_ref=o_ref.at[ch], send_sem=send_sems.at[ch],
                                            recv_sem=recv_sems.at[ch], device_id=sibling, device_id_type=MESH)
               for ch in range(4)]
        for cp in cps:
            cp.start()
        for cp in cps:
            cp.wait()

    any_spec = pl.BlockSpec(memory_space=pl.ANY)
    return _call(body, name, _sds((4, r, c), blocks.dtype), in_specs=[any_spec], out_specs=any_spec,
                 scratch=[pltpu.SemaphoreType.DMA((4,)), pltpu.SemaphoreType.DMA((4,))])(blocks)


def _chip_exchange_steps(x_ref, o_ref, send_sems, recv_sems, local_sem):
    me = _coords()
    x, y, cc = me
    my_chip = 2 * x + y
    mine = pltpu.make_async_copy(x_ref.at[my_chip], o_ref.at[my_chip], local_sem)
    peers = [(px, py, cc) for px, py in _chips(me)]
    sends = [pltpu.make_async_remote_copy(src_ref=x_ref.at[2 * px + py], dst_ref=o_ref.at[my_chip],
                                          send_sem=send_sems.at[j], recv_sem=recv_sems.at[j], device_id=(px, py, pc),
                                          device_id_type=MESH) for j, (px, py, pc) in enumerate(peers)]

    def start():
        mine.start()
        for cp in sends:
            cp.start()

    def finish():
        for j, (px, py, pc) in enumerate(peers):
            pltpu.make_async_remote_copy(src_ref=x_ref.at[my_chip], dst_ref=o_ref.at[2 * px + py],
                                         send_sem=send_sems.at[j], recv_sem=recv_sems.at[j], device_id=(px, py, pc),
                                         device_id_type=MESH).wait_recv()
        for cp in sends:
            cp.wait_send()
        mine.wait()

    return start, finish


_CHIP_SEMS = [pltpu.SemaphoreType.DMA((3,)), pltpu.SemaphoreType.DMA((3,)), pltpu.SemaphoreType.DMA(())]


def chip_exchange(name, blocks):
    def body(x_ref, o_ref, send_sems, recv_sems, local_sem):
        for step in _chip_exchange_steps(x_ref, o_ref, send_sems, recv_sems, local_sem):
            step()

    any_spec = pl.BlockSpec(memory_space=pl.ANY)
    return _call(body, name, _sds(blocks.shape, blocks.dtype), in_specs=[any_spec], out_specs=any_spec,
                 scratch=list(_CHIP_SEMS))(blocks)


def pair_sum(name, mine, theirs):
    _, r, c = mine.shape
    tm = max(d for d in range(16, 769, 16) if r % d == 0)

    def body(a_ref, b_ref, o_ref):
        o_ref[...] = (a_ref[...].astype(f32) + b_ref[...].astype(f32)).astype(o_ref.dtype)

    sp = pl.BlockSpec((None, tm, c), lambda ch, i: (ch, i, 0))
    return _call(body, name, _sds(mine.shape, mine.dtype), (4, r // tm), [sp, sp], sp)(mine, theirs)


def sum_slots(name, slots):
    n, r, c = slots.shape
    tm = max(d for d in range(16, 769, 16) if r % d == 0)

    def body(s_ref, o_ref):
        acc = s_ref[0].astype(f32)
        for d in range(1, n):
            acc = acc + s_ref[d].astype(f32)
        o_ref[...] = acc

    return _call(body, name, _sds((r, c)), (r // tm,), [pl.BlockSpec((n, tm, c), lambda i: (0, i, 0))],
                 pl.BlockSpec((tm, c), lambda i: (i, 0)))(slots)


def _rows_layout(col8, t):
    return col8.T.reshape(DN_HEADS, t // CHUNK, CHUNK)


def _blockdiag(m):
    g, a, b = m.shape
    m = m.reshape(N_SUPER, SUPER, a, b)
    out = jnp.einsum("sgab,gh->sgahb", m, jnp.eye(SUPER, dtype=m.dtype))
    return out.reshape(N_SUPER, SUPER * a, SUPER * b)


def _diag_blocks(m, a, b):
    m = m.reshape(N_SUPER, SUPER, a, SUPER, b)
    return jnp.einsum("sgahb,gh->sgab", m, jnp.eye(SUPER, dtype=m.dtype)).reshape(SSM_GROUPS, a, b)


def _s5_params(p, li):
    tag = f"l{li}"
    n = SSM_STATE
    a_re = p["ssm_a_re"].reshape(SSM_GROUPS, 1, n)
    a_im = p["ssm_a_im"].reshape(SSM_GROUPS, 1, n)
    log_dt = p["ssm_log_dt"].reshape(SSM_GROUPS, 1, 1)
    br = jnp.swapaxes(p["ssm_b_re"], 1, 2)
    bi = jnp.swapaxes(p["ssm_b_im"], 1, 2)
    lr, li_, bbr, bbi = s5_disc_fwd("s5_disc_" + tag, a_re, a_im, log_dt, br, bi)
    lam = jnp.concatenate([lr.reshape(N_SUPER, 1, SG_STATE), li_.reshape(N_SUPER, 1, SG_STATE)], axis=-1)
    bbd = jnp.concatenate([_blockdiag(bbr), _blockdiag(bbi)], axis=-1).astype(bf16)
    c_re = jnp.swapaxes(p["ssm_c_re"], 1, 2)
    c_im = jnp.swapaxes(p["ssm_c_im"], 1, 2)
    cbd = jnp.concatenate([_blockdiag(c_re), -_blockdiag(c_im)], axis=1).astype(bf16)
    return dict(a_re=a_re, a_im=a_im, log_dt=log_dt, br=br, bi=bi, lam=lam, bbd=bbd, cbd=cbd)


def layer_fwd(x, p, li, next_shard=None):
    tag = f"l{li}"
    t = x.shape[0]
    d = D_MODEL
    h = rmsnorm_fwd("norm_pre_" + tag, x, p["norm_pre"])
    proj = mm_nn("proj_" + tag, h, p["w_all"])
    q, k, v, bg = conv_qkv_fwd("conv_" + tag, proj, p["conv_w"], p["a_log"], p["dt_bias"])
    b_rows = _rows_layout(bg[:, 0:DN_HEADS], t)
    g_rows = _rows_layout(bg[:, DN_HEADS:2 * DN_HEADS], t)
    o, s_all, t_all, u_all, w_sol, *gathered = delta_fwd("delta_" + tag, q, k, v, g_rows, b_rows, next_shard)
    sp = _s5_params(p, li)
    u_perm = _perm_rows(proj[:, 4 * d:5 * d])
    ys_perm, hs = s5_fwd("s5_" + tag, u_perm, sp["bbd"], sp["cbd"], sp["lam"])
    ys_lin = _unperm_rows(ys_perm)
    yg = gelu_fwd("gelu_" + tag, ys_lin, proj, p["ssm_d"])
    glu_lin = mm_nn("glu_" + tag, yg, p["w_glu"])
    merged = merge_fwd("merge_" + tag, proj, o, yg, glu_lin, p["head_norm"], p["b_glu"])
    out = mm_nn("out_" + tag, merged, p["w_out"])
    y = residual_norm_fwd("norm_post_" + tag, x, out, p["norm_post"])
    saved = dict(x=x, h=h, proj=proj, q=q, k=k, v=v, g_rows=g_rows, b_rows=b_rows, o=o, s_all=s_all, t_all=t_all, u_all=u_all, w_sol=w_sol, sp=sp, u_perm=u_perm,
                 hs=hs, ys_lin=ys_lin, yg=yg, glu_lin=glu_lin, merged=merged, out=out)
    return y, saved, (gathered[0] if gathered else None)


def layer_bwd(dy, p, s, li, pair_blocks=None):
    tag = f"l{li}"
    t = dy.shape[0]
    d = D_MODEL
    sp = s["sp"]
    gr = {}
    d_out, gr["norm_post"] = post_norm_bwd("norm_post_b_" + tag, s["out"], p["norm_post"], dy)
    d_merged = mm_nt("out_b_" + tag, d_out, p["w_out"])
    gr["w_out"] = mm_tn("out_w_" + tag, s["merged"], d_out, bf16)
    (dza, dzb, dra, drb, d_o, d_glu, dyg_a, gr["head_norm"], gr["b_glu"]) = merge_bwd(
        "merge_b_" + tag, s["proj"], s["o"], s["yg"], s["glu_lin"], p["head_norm"], p["b_glu"], d_merged)
    dyg_b = mm_nt("glu_b_" + tag, d_glu, p["w_glu"])
    gr["w_glu"] = mm_tn("glu_w_" + tag, s["yg"], d_glu, bf16)
    d_ys, du_a, gr["ssm_d"] = gelu_bwd("gelu_b_" + tag, s["ys_lin"], s["proj"], p["ssm_d"], dyg_a, dyg_b)
    hprev0 = jnp.concatenate([jnp.zeros((1, s["hs"].shape[1]), f32), s["hs"][-SEG:-1]], axis=0)
    du_perm, d_bbd, d_cbd, d_lam, *sib = s5_bwd("s5_b_" + tag, _perm_rows(d_ys), s["u_perm"], s["hs"], hprev0,
                                                sp["bbd"], sp["cbd"], sp["lam"], pair_blocks)
    chip_part = None
    if sib:
        own = lax.dynamic_index_in_dim(pair_blocks, lax.axis_index("c"), axis=1, keepdims=False)
        chip_part = pair_sum(f"pair_sum_grads_l{li + 1}", own, sib[0])
    du_b = _unperm_rows(du_perm)
    gr["ssm_c_re"] = _diag_blocks(d_cbd[:, :, 0:SG_STATE], SSM_GROUP, SSM_STATE)
    gr["ssm_c_im"] = -_diag_blocks(d_cbd[:, :, SG_STATE:], SSM_GROUP, SSM_STATE)
    d_bbr = _diag_blocks(d_bbd[:, :, 0:SG_STATE], SSM_GROUP, SSM_STATE)
    d_bbi = _diag_blocks(d_bbd[:, :, SG_STATE:], SSM_GROUP, SSM_STATE)
    d_lr = d_lam[:, :, 0:SG_STATE].reshape(SSM_GROUPS, 1, SSM_STATE)
    d_li = d_lam[:, :, SG_STATE:].reshape(SSM_GROUPS, 1, SSM_STATE)
    d_are, d_aim, d_ldt, d_br, d_bi = s5_disc_bwd("s5_disc_b_" + tag, sp["a_re"], sp["a_im"], sp["log_dt"], sp["br"],
                                                  sp["bi"], d_lr, d_li, d_bbr, d_bbi)
    gr["ssm_a_re"] = d_are.reshape(SSM_GROUPS, SSM_STATE)
    gr["ssm_a_im"] = d_aim.reshape(SSM_GROUPS, SSM_STATE)
    gr["ssm_log_dt"] = d_ldt.reshape(SSM_GROUPS)
    gr["ssm_b_re"] = jnp.swapaxes(d_br, 1, 2)
    gr["ssm_b_im"] = jnp.swapaxes(d_bi, 1, 2)
    dq, dk, dv, dg_rows, db_rows, *slots = delta_bwd("delta_b_" + tag, s["q"], s["k"], s["v"], s["g_rows"],
                                                     s["b_rows"], s["s_all"], s["t_all"], s["u_all"], s["w_sol"], d_o,
                                                     chip_part)
    dbg = jnp.concatenate([db_rows.reshape(DN_HEADS, t).T, dg_rows.reshape(DN_HEADS, t).T,
                           jnp.zeros((t, 128 - 2 * DN_HEADS), f32)], axis=1)
    dc, gr["conv_w"], dbd, dab = conv_qkv_bwd("conv_b_" + tag, s["proj"], p["conv_w"], p["a_log"], p["dt_bias"],
                                              dq, dk, dv, dbg)
    gr["a_log"] = dab[0, DN_HEADS:2 * DN_HEADS]
    gr["dt_bias"] = dab[1, DN_HEADS:2 * DN_HEADS]
    dproj = assemble_dproj("dproj_" + tag, dc, p["conv_w"], dza, du_a, du_b, dzb, dra, drb, dbd)
    d_h = mm_nt("proj_b_" + tag, dproj, p["w_all"])
    gr["w_all"] = mm_nn("proj_w_" + tag, s["h"].T, dproj, bf16)
    dx, gr["norm_pre"] = rmsnorm_bwd("norm_pre_b_" + tag, s["x"], p["norm_pre"], d_h, dy)
    return dx, gr, (slots[0] if slots else None)


REPL = ["norm_pre", "a_log", "dt_bias", "head_norm", "ssm_a_re", "ssm_a_im", "ssm_log_dt", "ssm_b_re", "ssm_b_im",
        "ssm_c_re", "ssm_c_im", "ssm_d", "b_glu", "norm_post"]
SHARDED = ["w_in", "conv_w", "w_glu", "w_out"]
ALL_W = ["norm_pre", "w_in", "conv_w", "a_log", "dt_bias", "head_norm", "ssm_a_re", "ssm_a_im", "ssm_log_dt",
         "ssm_b_re", "ssm_b_im", "ssm_c_re", "ssm_c_im", "ssm_d", "w_glu", "b_glu", "w_out", "norm_post"]
PACK_W = 1024


def _pack_flat(arrs, rows):
    flat = jnp.concatenate([a.reshape(-1) for a in arrs])
    return jnp.pad(flat, (0, rows * PACK_W - flat.shape[0])).reshape(rows, PACK_W)


def _flat_rows(arrs, mult=8):
    n = sum(math.prod(a.shape) for a in arrs)
    rows = -(-n // PACK_W)
    return -(-rows // mult) * mult


def _unpack(flat, shapes):
    out, off = [], 0
    for sh in shapes:
        n = math.prod(sh)
        out.append(flat[off:off + n].reshape(sh))
        off += n
    return out


def _repl_split(shapes):
    big = [n for n in REPL if math.prod(shapes[n]) % PACK_W == 0]
    small = [n for n in REPL if n not in big]
    return big, small


def _rows8(n):
    return -(-n // (8 * PACK_W)) * 8


def _repl_rows(shapes):
    big, small = _repl_split(shapes)
    rows = sum(_rows8(math.prod(shapes[n])) for n in big) + _flat_rows([_sds(shapes[n]) for n in small], 8)
    return -(-rows // (16 * N_DEV)) * (16 * N_DEV)


def _repl_pack(arrs, shapes):
    big, small = _repl_split(shapes)
    parts = []
    for n in big:
        a = arrs[n].reshape(-1, PACK_W)
        parts.append(jnp.pad(a, ((0, _rows8(a.size) - a.shape[0]), (0, 0))))
    parts.append(_pack_flat([arrs[n] for n in small], _flat_rows([_sds(shapes[n]) for n in small], 8)))
    used = sum(p.shape[0] for p in parts)
    parts.append(jnp.zeros((_repl_rows(shapes) - used, PACK_W), parts[0].dtype))
    return jnp.concatenate(parts, axis=0)


def _repl_unpack(packed, shapes):
    big, small = _repl_split(shapes)
    out, off = {}, 0
    for n in big:
        size = math.prod(shapes[n])
        out[n] = packed[off:off + size // PACK_W].reshape(shapes[n])
        off += _rows8(size)
    srows = _flat_rows([_sds(shapes[n]) for n in small], 8)
    out.update(zip(small, _unpack(packed[off:off + srows].reshape(-1), [shapes[n] for n in small])))
    return out


_COL_RUNS = ((0, 4096), (4112, W_COLS), (4096, 4112))


def _w_all(main, tails, wc):
    pieces = []
    for lo, hi in _COL_RUNS:
        for dv in range(N_DEV):
            a, b = max(lo, dv * wc) - dv * wc, min(hi, (dv + 1) * wc) - dv * wc
            if a < min(b, PACK_W):
                pieces.append(main[dv][:, :, a:min(b, PACK_W)])
            if b > max(a, PACK_W):
                pieces.append(tails[dv][:, :, max(a, PACK_W) - PACK_W:b - PACK_W])
    ll, rows = main[0].shape[0], main[0].shape[1]
    pieces.append(jnp.zeros((ll, rows, W_PAD - W_COLS), main[0].dtype))
    return jnp.concatenate(pieces, axis=2)


def _ref_cols(g, lo, hi):
    pieces, off = [], 0
    for a, b in _COL_RUNS:
        s, e = max(lo, a), min(hi, b)
        if s < e:
            pieces.append((s, g[..., off + s - a:off + e - a]))
        off += b - a
    pieces.sort(key=lambda t: t[0])
    return jnp.concatenate([p for _, p in pieces], axis=-1) if len(pieces) > 1 else pieces[0][1]


def kernel(x, norm_pre, w_in, conv_w, a_log, dt_bias, head_norm, ssm_a_re, ssm_a_im, ssm_log_dt, ssm_b_re, ssm_b_im, ssm_c_re, ssm_c_im, ssm_d, w_glu, b_glu, w_out, norm_post, loss_target, m_norm_pre, m_w_in, m_conv_w, m_a_log, m_dt_bias, m_head_norm, m_ssm_a_re, m_ssm_a_im, m_ssm_log_dt, m_ssm_b_re, m_ssm_b_im, m_ssm_c_re, m_ssm_c_im, m_ssm_d, m_w_glu, m_b_glu, m_w_out, m_norm_post, v_norm_pre, v_w_in, v_conv_w, v_a_log, v_dt_bias, v_head_norm, v_ssm_a_re, v_ssm_a_im, v_ssm_log_dt, v_ssm_b_re, v_ssm_b_im, v_ssm_c_re, v_ssm_c_im, v_ssm_d, v_w_glu, v_b_glu, v_w_out, v_norm_post):
    loc = dict(locals())
    w = {n: loc[n] for n in ALL_W}
    m = {n: loc["m_" + n] for n in ALL_W}
    v = {n: loc["v_" + n] for n in ALL_W}
    depth = w_in.shape[0]
    wc = w_in.shape[2]
    cc = conv_w.shape[2]
    wr = w_glu.shape[1]

    tail = wc - PACK_W
    conv_hi = conv_w.astype(bf16)
    conv_mid = (conv_w - conv_hi.astype(f32)).astype(bf16)
    conv_lo = (conv_w - conv_hi.astype(f32) - conv_mid.astype(f32)).astype(bf16)
    conv3 = jnp.stack([conv_hi, conv_mid, conv_lo], axis=1)
    w_in_b = w_in.astype(bf16)
    MISC = 16
    o_glu, o_out, o_tail, o_conv = D_MODEL, D_MODEL + wr, D_MODEL + 2 * wr, D_MODEL + 2 * wr + MISC

    def rows16(a):
        return jnp.pad(a, ((0, MISC - a.shape[0]), (0, PACK_W - a.shape[1])))

    def layer_shard(li):
        return jnp.concatenate([w_in_b[li, :, :PACK_W], w_glu[li].astype(bf16), w_out[li].astype(bf16),
                                rows16(w_in_b[li, :, PACK_W:].T), rows16(conv3[li].reshape(3 * CONV_K, cc))])

    def layer_weights(gathered):
        tails = gathered[:, o_tail:o_tail + tail].reshape(N_DEV * tail, PACK_W).T
        w_all = _w_all([gathered[dv, :D_MODEL][None] for dv in range(N_DEV)],
                       [tails[None, :, dv * tail:(dv + 1) * tail] for dv in range(N_DEV)], wc)[0]
        glu = gathered[:, o_glu:o_glu + wr].reshape(N_DEV * wr, D_MODEL)
        out = gathered[:, o_out:o_out + wr].reshape(N_DEV * wr, D_MODEL)
        conv = gathered[:, o_conv:o_conv + 3 * CONV_K, :cc].astype(f32).reshape(N_DEV, 3, CONV_K, cc)
        conv = (conv[:, 0] + conv[:, 1] + conv[:, 2]).transpose(1, 0, 2).reshape(CONV_K, N_DEV * cc)
        return dict(w_all=w_all, w_glu=glu, w_out=out, conv_w=conv)

    def layer_params(li, gathered):
        return dict(layer_weights(gathered), norm_pre=norm_pre[li].reshape(1, -1),
                    a_log=jnp.pad(a_log[li], (DN_HEADS, 128 - 2 * DN_HEADS)).reshape(1, 128),
                    dt_bias=jnp.pad(dt_bias[li], (DN_HEADS, 128 - 2 * DN_HEADS)).reshape(1, 128),
                    head_norm=head_norm[li].reshape(1, -1), ssm_a_re=ssm_a_re[li], ssm_a_im=ssm_a_im[li],
                    ssm_log_dt=ssm_log_dt[li], ssm_b_re=ssm_b_re[li], ssm_b_im=ssm_b_im[li], ssm_c_re=ssm_c_re[li],
                    ssm_c_im=ssm_c_im[li], ssm_d=ssm_d[li].reshape(1, -1),
                    b_glu=b_glu[li].reshape(1, -1), norm_post=norm_post[li].reshape(1, -1))

    act = x[0]
    saved, params = [], []
    gathered = all_gather("gather_weights", layer_shard(0))
    for li in range(depth):
        params.append(layer_params(li, gathered))
        act, sv, gathered = layer_fwd(act, params[li], li, layer_shard(li + 1) if li + 1 < depth else None)
        saved.append(sv)
    loss_part, dy = loss_head("loss_head", act, loss_target[0])
    repl_shapes = {n: w[n].shape for n in REPL}
    repl_rows = _repl_rows(repl_shapes)
    rr = repl_rows // N_DEV
    my_c = lax.axis_index("c")

    def grad_blocks(gl, extra=None):
        g_conv = gl["conv_w"].astype(bf16)
        tails = jnp.concatenate([_ref_cols(gl["w_all"], dv * wc + PACK_W, (dv + 1) * wc) for dv in range(N_DEV)],
                                axis=1).T
        blocks = jnp.stack([jnp.concatenate(
            [_ref_cols(gl["w_all"], dv * wc, dv * wc + PACK_W), gl["w_glu"][dv * wr:(dv + 1) * wr],
             gl["w_out"][dv * wr:(dv + 1) * wr], rows16(tails[dv * tail:(dv + 1) * tail]),
             rows16(g_conv[:, dv * cc:(dv + 1) * cc])]
            + ([extra[dv * rr:(dv + 1) * rr]] if extra is not None else []))
            for dv in range(N_DEV)])
        return blocks.reshape(4, 2, blocks.shape[1], PACK_W)

    grads, slots, pending = [None] * depth, [None] * depth, None
    for li in reversed(range(depth)):
        dy, grads[li], got = layer_bwd(dy, params[li], saved[li], li, pending)
        if pending is not None:
            slots[li + 1] = got
        g_repl = None
        if li == 0:
            g_repl = _repl_pack({n: jnp.stack([grads[l][n] for l in range(depth)]).reshape(w[n].shape) for n in REPL},
                                repl_shapes).astype(bf16)
        pending = grad_blocks(grads[li], g_repl)
    own = lax.dynamic_index_in_dim(pending, my_c, axis=1, keepdims=False)
    slots[0] = chip_exchange("scatter_grads", pair_sum("pair_sum_grads_l0", own, pair_exchange("pair_grads_l0", pending)))
    grad_x = dy[None]
    loss = lax.psum(loss_part[0, 0], ("x", "y", "c"))
    mine = [sum_slots(f"sum_grads_l{li}", slots[li]) for li in range(depth)]
    o_repl = o_conv + MISC
    gs_w_in = jnp.stack([jnp.concatenate([mi[:o_glu], mi[o_tail:o_tail + tail].T], axis=1) for mi in mine])
    gs_glu = jnp.stack([mi[o_glu:o_out] for mi in mine])
    gs_out = jnp.stack([mi[o_out:o_tail] for mi in mine])
    gs_conv = jnp.stack([mi[o_conv:o_conv + CONV_K, :cc] for mi in mine])
    g_repl_full = all_gather("gather_repl_grads", mine[0][o_repl:o_repl + rr]).reshape(repl_rows, PACK_W)
    g = _repl_unpack(g_repl_full, repl_shapes)
    g.update(w_in=gs_w_in, w_glu=gs_glu, w_out=gs_out, conv_w=gs_conv)

    delta, new_m, new_v = {}, {}, {}
    for n in SHARDED:
        delta[n], new_m[n], new_v[n] = adamw("adamw_" + n, w[n], g[n], m[n], v[n])
    outs = adamw("adamw_repl", *[_repl_pack({n: src[n] for n in REPL}, repl_shapes) for src in (w, g, m, v)])
    for dst, o in zip((delta, new_m, new_v), outs):
        dst.update(_repl_unpack(o, repl_shapes))
    return (loss, grad_x, *[g[n] for n in ALL_W], *[delta[n] for n in ALL_W], *[new_m[n] for n in ALL_W],
            *[new_v[n] for n in ALL_W])
```

```python
import math

import jax
import jax.numpy as jnp
from jax import lax
from jax.experimental import pallas as pl
from jax.experimental.pallas import tpu as pltpu

f32 = jnp.float32
bf16 = jnp.bfloat16

D_MODEL = 1024
N_DEV = 8
DN_HEADS = 8
HEAD_DIM = 128
CHUNK = 64
CONV_K = 4
SSM_GROUPS = 64
SSM_GROUP = 16
SSM_STATE = 64
SUPER = 8
N_SUPER = SSM_GROUPS // SUPER
SG_STATE = SUPER * SSM_STATE
EPS = 1e-6
W_COLS = 8208
W_PAD = 8448
ADAM_LR, ADAM_B1, ADAM_B2, ADAM_EPS, ADAM_WD, ADAM_STEP = 0.001, 0.9, 0.999, 1e-08, 0.01, 10
VMEM_LIMIT = 56 * 1024 * 1024
MESH = pl.DeviceIdType.MESH
HIGH = lax.Precision.HIGH
DELTA_HB = 8


def _call(body, name, out_shape, grid=None, in_specs=None, out_specs=None, scratch=(), **kw):
    args = dict(out_shape=out_shape, name=name, scratch_shapes=list(scratch),
                compiler_params=pltpu.CompilerParams(vmem_limit_bytes=VMEM_LIMIT, **kw))
    if grid is not None:
        args.update(grid=grid, in_specs=in_specs, out_specs=out_specs)
    else:
        if in_specs is not None:
            args.update(in_specs=in_specs)
        if out_specs is not None:
            args.update(out_specs=out_specs)
    return pl.pallas_call(body, **args)


def _sds(shape, dtype=f32):
    return jax.ShapeDtypeStruct(tuple(shape), dtype)


def _sigmoid(x):
    return 1.0 / (1.0 + jnp.exp(-x))


def _silu(x):
    return x * _sigmoid(x)


_GELU_C = math.sqrt(2.0 / math.pi)


def _gelu(x):
    return 0.5 * x * (1.0 + jnp.tanh(_GELU_C * (x + 0.044715 * x * x * x)))


def _dgelu(x):
    t = jnp.tanh(_GELU_C * (x + 0.044715 * x * x * x))
    return 0.5 * (1.0 + t) + 0.5 * x * (1.0 - t * t) * _GELU_C * (1.0 + 3 * 0.044715 * x * x)


def _softplus(x):
    return jnp.maximum(x, 0.0) + jnp.log(1.0 + jnp.exp(-jnp.abs(x)))


def _bdot(a, b, dn):
    return lax.dot_general(a.astype(bf16), b.astype(bf16), (dn, ((), ())), preferred_element_type=f32)


def _matmul(name, a, b, *, dn, grid, a_spec, b_spec, o_spec, o_shape, o_dtype=f32):
    nk = grid[-1]

    def body(a_ref, b_ref, o_ref, acc_ref):
        p = _bdot(a_ref[...], b_ref[...], dn)
        if nk == 1:
            o_ref[...] = p.astype(o_dtype)
        else:
            k = pl.program_id(len(grid) - 1)

            @pl.when(k == 0)
            def _():
                acc_ref[...] = p

            @pl.when(k > 0)
            def _():
                acc_ref[...] += p

            @pl.when(k == nk - 1)
            def _():
                o_ref[...] = acc_ref[...].astype(o_dtype)

    blk = tuple(d for d in o_spec.block_shape if d is not None)
    return _call(body, name, _sds(o_shape, o_dtype), grid, [a_spec, b_spec], o_spec,
                 scratch=[pltpu.VMEM(blk if nk > 1 else (8, 128), f32)])(a, b)


def _tile(n, pref):
    for t in pref:
        if n % t == 0:
            return t
    return n


def mm_nn(name, a, b, o_dtype=f32):
    m, k = a.shape
    n = b.shape[1]
    tm, tn, tk = _tile(m, (1024, 512, 256)), _tile(n, (2816, 1024, 512)), _tile(k, (2816, 1024))
    return _matmul(name, a, b, dn=((1,), (0,)), grid=(m // tm, n // tn, k // tk),
                   a_spec=pl.BlockSpec((tm, tk), lambda i, j, l: (i, l)),
                   b_spec=pl.BlockSpec((tk, tn), lambda i, j, l: (l, j)),
                   o_spec=pl.BlockSpec((tm, tn), lambda i, j, l: (i, j)), o_shape=(m, n), o_dtype=o_dtype)


def mm_nt(name, a, b):
    m, k = a.shape
    n = b.shape[0]
    tm, tn, tk = _tile(m, (1024, 512, 256)), _tile(n, (1024, 512)), _tile(k, (2816, 1024))
    return _matmul(name, a, b, dn=((1,), (1,)), grid=(m // tm, n // tn, k // tk),
                   a_spec=pl.BlockSpec((tm, tk), lambda i, j, l: (i, l)),
                   b_spec=pl.BlockSpec((tn, tk), lambda i, j, l: (j, l)),
                   o_spec=pl.BlockSpec((tm, tn), lambda i, j, l: (i, j)), o_shape=(m, n))


def mm_tn(name, a, b, o_dtype=f32):
    k, m = a.shape
    n = b.shape[1]
    tm, tn, tk = _tile(m, (512,)), _tile(n, (2816, 1024, 512)), _tile(k, (512, 256))
    return _matmul(name, a, b, dn=((0,), (0,)), grid=(m // tm, n // tn, k // tk),
                   a_spec=pl.BlockSpec((tk, tm), lambda i, j, l: (l, i)),
                   b_spec=pl.BlockSpec((tk, tn), lambda i, j, l: (l, j)),
                   o_spec=pl.BlockSpec((tm, tn), lambda i, j, l: (i, j)), o_shape=(m, n), o_dtype=o_dtype)


def _rows(t, light=False):
    return _tile(t, (512, 256) if light else (256,))


def _row_spec(tm, w):
    return pl.BlockSpec((tm, w), lambda i: (i, 0))


def _acc_spec(r, w):
    return pl.BlockSpec((r, w), lambda i: (0, 0))


def _accumulate(ref, val):
    @pl.when(pl.program_id(0) == 0)
    def _():
        ref[...] = val

    @pl.when(pl.program_id(0) > 0)
    def _():
        ref[...] += val


def rmsnorm_fwd(name, x, gain):
    t, d = x.shape
    tm = _rows(t, light=True)

    def body(x_ref, g_ref, o_ref):
        xv = x_ref[...]
        r = lax.rsqrt(jnp.mean(xv * xv, axis=-1, keepdims=True) + EPS)
        o_ref[...] = (xv * r * g_ref[...]).astype(bf16)

    return _call(body, name, _sds((t, d), bf16), (t // tm,), [_row_spec(tm, d), _acc_spec(1, d)],
                 _row_spec(tm, d))(x, gain)


def rmsnorm_bwd(name, x, gain, dn, dres):
    t, d = x.shape
    tm = _rows(t, light=True)

    def body(x_ref, g_ref, dn_ref, dr_ref, dx_ref, dg_ref):
        xv = x_ref[...]
        r = lax.rsqrt(jnp.mean(xv * xv, axis=-1, keepdims=True) + EPS)
        n = xv * r
        dnv = dn_ref[...]
        _accumulate(dg_ref, jnp.sum(dnv * n, axis=0, keepdims=True))
        dng = dnv * g_ref[...]
        dx_ref[...] = dr_ref[...] + r * (dng - n * jnp.mean(dng * n, axis=-1, keepdims=True))

    return _call(body, name, (_sds((t, d)), _sds((1, d))), (t // tm,),
                 [_row_spec(tm, d), _acc_spec(1, d), _row_spec(tm, d), _row_spec(tm, d)],
                 (_row_spec(tm, d), _acc_spec(1, d)))(x, gain, dn, dres)


def residual_norm_fwd(name, x, out, gain):
    t, d = x.shape
    tm = _rows(t, light=True)

    def body(x_ref, o_ref, g_ref, y_ref):
        ov = o_ref[...]
        r = lax.rsqrt(jnp.mean(ov * ov, axis=-1, keepdims=True) + EPS)
        y_ref[...] = x_ref[...] + ov * r * g_ref[...]

    return _call(body, name, _sds((t, d)), (t // tm,), [_row_spec(tm, d), _row_spec(tm, d), _acc_spec(1, d)],
                 _row_spec(tm, d))(x, out, gain)


def post_norm_bwd(name, out, gain, dy):
    t, d = out.shape
    tm = _rows(t, light=True)

    def body(o_ref, g_ref, dy_ref, do_ref, dg_ref):
        ov = o_ref[...]
        r = lax.rsqrt(jnp.mean(ov * ov, axis=-1, keepdims=True) + EPS)
        n = ov * r
        dyv = dy_ref[...]
        _accumulate(dg_ref, jnp.sum(dyv * n, axis=0, keepdims=True))
        dng = dyv * g_ref[...]
        do_ref[...] = (r * (dng - n * jnp.mean(dng * n, axis=-1, keepdims=True))).astype(bf16)

    return _call(body, name, (_sds((t, d), bf16), _sds((1, d))), (t // tm,),
                 [_row_spec(tm, d), _acc_spec(1, d), _row_spec(tm, d)],
                 (_row_spec(tm, d), _acc_spec(1, d)))(out, gain, dy)


def loss_head(name, y, target):
    t, d = y.shape
    tm = _rows(t, light=True)

    def body(y_ref, t_ref, l_ref, dy_ref):
        e = y_ref[...] - t_ref[...]
        dy_ref[...] = e * (1.0 / d)
        s = jnp.sum(jnp.sum(e * e, axis=1, keepdims=True), axis=0, keepdims=True) * (0.5 / d)
        _accumulate(l_ref, s)

    return _call(body, name, (_sds((1, 1)), _sds((t, d))), (t // tm,),
                 [_row_spec(tm, d), _row_spec(tm, d)], (_acc_spec(1, 1), _row_spec(tm, d)))(y, target)


def _prev_spec(tm, w):
    return pl.BlockSpec((8, w), lambda i: (jnp.maximum(i * (tm // 8) - 1, 0), 0))


def _next_spec(tm, w, t):
    return pl.BlockSpec((8, w), lambda i: (jnp.minimum((i + 1) * (tm // 8), t // 8 - 1), 0))


def _fill_pad(pad_ref, prev_ref, cur_ref, tm):
    keep = (pl.program_id(0) > 0).astype(f32)
    pad_ref[0:8, :] = prev_ref[...] * keep
    pad_ref[8:8 + tm, :] = cur_ref[...]


def _conv_block(pad_ref, w_ref, cb, tm):
    cs = slice(cb * 128, (cb + 1) * 128)
    acc = pad_ref[pl.ds(8 - (CONV_K - 1), tm), cs] * w_ref[0:1, cs]
    for j in range(1, CONV_K):
        acc = acc + pad_ref[pl.ds(8 - (CONV_K - 1) + j, tm), cs] * w_ref[j:j + 1, cs]
    return acc


def conv_qkv_fwd(name, proj, conv_w, a_log, dt_bias):
    t = proj.shape[0]
    tm = _rows(t)
    scale = HEAD_DIM ** -0.5

    def body(cur_ref, prev_ref, w_ref, bd_ref, al_ref, db_ref, q_ref, k_ref, v_ref, bg_ref, pad_ref):
        _fill_pad(pad_ref, prev_ref, cur_ref, tm)
        for cb in range(3 * DN_HEADS):
            s = _silu(_conv_block(pad_ref, w_ref, cb, tm))
            hs = slice((cb % DN_HEADS) * 128, (cb % DN_HEADS + 1) * 128)
            if cb < DN_HEADS:
                q_ref[:, hs] = s * (lax.rsqrt(jnp.sum(s * s, axis=-1, keepdims=True) + EPS) * scale)
            elif cb < 2 * DN_HEADS:
                k_ref[:, hs] = s * lax.rsqrt(jnp.sum(s * s, axis=-1, keepdims=True) + EPS)
            else:
                v_ref[:, hs] = s
        bd = bd_ref[...]
        beta = _sigmoid(bd)
        g = -jnp.exp(al_ref[...]) * _softplus(bd + db_ref[...])
        lane = lax.broadcasted_iota(jnp.int32, bd.shape, 1)
        bg_ref[...] = jnp.where(lane < DN_HEADS, beta, jnp.where(lane < 2 * DN_HEADS, g, 0.0))

    w3 = 3 * D_MODEL
    return _call(body, name, (_sds((t, D_MODEL)),) * 3 + (_sds((t, 128)),), (t // tm,),
                 [pl.BlockSpec((tm, w3), lambda i: (i, 0)), _prev_spec(tm, w3), _acc_spec(CONV_K, w3),
                  pl.BlockSpec((tm, 128), lambda i: (i, 8192 // 128)), _acc_spec(1, 128), _acc_spec(1, 128)],
                 (_row_spec(tm, D_MODEL),) * 3 + (_row_spec(tm, 128),),
                 scratch=[pltpu.VMEM((tm + 8, w3), f32)])(proj, proj, conv_w, proj, a_log, dt_bias)


def conv_qkv_bwd(name, proj, conv_w, a_log, dt_bias, dq, dk, dv, dbg):
    t = proj.shape[0]
    tm = _rows(t)
    scale = HEAD_DIM ** -0.5

    def body(cur_ref, prev_ref, w_ref, bd_ref, al_ref, db_ref, dq_ref, dk_ref, dv_ref, dbg_ref,
             dc_ref, dw_ref, dbd_ref, dab_ref, pad_ref):
        _fill_pad(pad_ref, prev_ref, cur_ref, tm)

        @pl.when(pl.program_id(0) == 0)
        def _():
            dw_ref[...] = jnp.zeros_like(dw_ref)

        for cb in range(3 * DN_HEADS):
            cs = slice(cb * 128, (cb + 1) * 128)
            hs = slice((cb % DN_HEADS) * 128, (cb % DN_HEADS + 1) * 128)
            taps = [pad_ref[pl.ds(8 - (CONV_K - 1) + j, tm), cs] for j in range(CONV_K)]
            c = taps[0] * w_ref[0:1, cs]
            for j in range(1, CONV_K):
                c = c + taps[j] * w_ref[j:j + 1, cs]
            sg = _sigmoid(c)
            s = c * sg
            if cb < 2 * DN_HEADS:
                dn = (dq_ref[:, hs] * scale) if cb < DN_HEADS else dk_ref[:, hs]
                r = lax.rsqrt(jnp.sum(s * s, axis=-1, keepdims=True) + EPS)
                ds = r * dn - s * (r * r * r) * jnp.sum(dn * s, axis=-1, keepdims=True)
            else:
                ds = dv_ref[:, hs]
            dc = ds * (sg * (1.0 + c * (1.0 - sg)))
            dc_ref[:, cs] = dc
            for j in range(CONV_K):
                dw_ref[j:j + 1, cs] += jnp.sum(dc * taps[j], axis=0, keepdims=True)
        bd = bd_ref[...]
        dbg_v = dbg_ref[...]
        lane = lax.broadcasted_iota(jnp.int32, bd.shape, 1)
        sg = _sigmoid(bd)
        ea = jnp.exp(al_ref[...])
        z = bd + db_ref[...]
        sp = _softplus(z)
        is_b = lane < DN_HEADS
        is_g = jnp.logical_and(lane >= DN_HEADS, lane < 2 * DN_HEADS)
        d_z = jnp.where(is_g, dbg_v * (-ea) * _sigmoid(z), 0.0)
        dbd_ref[...] = jnp.where(is_b, dbg_v * sg * (1.0 - sg), d_z)
        d_al = jnp.sum(jnp.where(is_g, dbg_v * (-ea) * sp, 0.0), axis=0, keepdims=True)
        d_db = jnp.sum(d_z, axis=0, keepdims=True)
        _accumulate(dab_ref, jnp.concatenate([d_al, d_db] + [jnp.zeros_like(d_al)] * 6, axis=0))

    w3 = 3 * D_MODEL
    return _call(body, name, (_sds((t, w3)), _sds((CONV_K, w3)), _sds((t, 128)), _sds((8, 128))), (t // tm,),
                 [pl.BlockSpec((tm, w3), lambda i: (i, 0)), _prev_spec(tm, w3), _acc_spec(CONV_K, w3),
                  pl.BlockSpec((tm, 128), lambda i: (i, 8192 // 128)), _acc_spec(1, 128), _acc_spec(1, 128),
                  _row_spec(tm, D_MODEL), _row_spec(tm, D_MODEL), _row_spec(tm, D_MODEL), _row_spec(tm, 128)],
                 (_row_spec(tm, w3), _acc_spec(CONV_K, w3), _row_spec(tm, 128), _acc_spec(8, 128)),
                 scratch=[pltpu.VMEM((tm + 8, w3), f32)])(proj, proj, conv_w, proj, a_log, dt_bias, dq, dk, dv, dbg)


def _bdg(a, b, ca, cb, prec=None):
    if prec is None:
        a, b = a.astype(bf16), b.astype(bf16)
    return lax.dot_general(a, b, (((ca,), (cb,)), ((0,), (0,))), precision=prec, preferred_element_type=f32)


def _bnn(a, b, prec=None):
    return _bdg(a, b, 2, 1, prec)


def _bnt(a, b, prec=None):
    return _bdg(a, b, 2, 2, prec)


def _btn(a, b, prec=None):
    return _bdg(a, b, 1, 1, prec)


def _delta_local(q, k, v, g_row, b_row, solved=None):
    c = CHUNK
    ii = lax.broadcasted_iota(jnp.int32, (c, c), 0)
    jj = lax.broadcasted_iota(jnp.int32, (c, c), 1)
    eye, lower, strict = ii == jj, ii >= jj, ii > jj
    shp = (q.shape[0], c, c)
    g_b = jnp.broadcast_to(g_row, shp)
    gc_col = jnp.sum(jnp.where(lower, g_b, 0.0), axis=2, keepdims=True)
    gc_row = jnp.sum(jnp.where(eye, jnp.broadcast_to(gc_col, shp), 0.0), axis=1, keepdims=True)
    b_col = jnp.sum(jnp.where(eye, jnp.broadcast_to(b_row, shp), 0.0), axis=2, keepdims=True)
    gl = jnp.sum(g_row, axis=2, keepdims=True)
    decay = jnp.exp(jnp.where(lower, gc_col - gc_row, -1e30))
    kb = k * b_col
    kk = _bnt(kb, k)
    lmat = jnp.where(strict, kk * decay, 0.0)
    egc = jnp.exp(gc_col)
    rhs_w = kb * egc
    if solved is None:
        tinv = eye.astype(f32) - lmat
        pw = lmat
        for _ in range(5):
            pw = _bnn(pw, pw, HIGH)
            tinv = tinv + _bnn(tinv, pw, HIGH)
        uw = _bnn(tinv, jnp.concatenate([v * b_col, rhs_w], axis=2), HIGH)
        u, w = uw[:, :, 0:HEAD_DIM], uw[:, :, HEAD_DIM:]
    else:
        tinv, u, w = solved
    qk = _bnt(q, k)
    amat = jnp.where(lower, qk * decay, 0.0)
    q_dec = q * egc
    kdf = jnp.exp(gl - gc_col)
    k_dec = k * kdf
    return dict(eye=eye, lower=lower, strict=strict, gc_col=gc_col, b_col=b_col, gl=gl, decay=decay,
                kb=kb, kk=kk, tinv=tinv, egc=egc, rhs_w=rhs_w, u=u, w=w, qk=qk, amat=amat, q_dec=q_dec, kdf=kdf,
                k_dec=k_dec)


def _heads(ref, hb):
    return jnp.stack([ref[:, h * HEAD_DIM:(h + 1) * HEAD_DIM] for h in range(hb)])


def _put_heads(ref, val):
    for h in range(val.shape[0]):
        ref[:, h * HEAD_DIM:(h + 1) * HEAD_DIM] = val[h]


def delta_fwd(name, q, k, v, g_rows, b_rows, next_shard=None):
    t = q.shape[0]
    nc = t // CHUNK
    hb = DELTA_HB
    fused = next_shard is not None
    assert not fused or (hb == DN_HEADS and nc >= 2)

    def body(*refs):
        if fused:
            (q_ref, k_ref, v_ref, g_ref, b_ref, x_ref, o_ref, s_ref, t_ref, u_ref, w_ref, gath_ref, state, send_sems,
             recv_sems, local_sem) = refs
        else:
            q_ref, k_ref, v_ref, g_ref, b_ref, o_ref, s_ref, t_ref, u_ref, w_ref, state = refs
        n = pl.program_id(1)
        if fused:
            start, forward, finish = _gather_steps(x_ref, gath_ref, send_sems, recv_sems, local_sem)
            pl.when(n == 0)(start)

        @pl.when(n == 0)
        def _():
            state[...] = jnp.zeros_like(state)

        loc = _delta_local(_heads(q_ref, hb), _heads(k_ref, hb), _heads(v_ref, hb), g_ref[:, pl.ds(n, 1), :],
                           b_ref[:, pl.ds(n, 1), :])
        s0 = state[...]
        s_ref[...] = s0
        t_ref[...] = loc["tinv"]
        _put_heads(u_ref, loc["u"])
        _put_heads(w_ref, loc["w"])
        v_new = loc["u"] - _bnn(loc["w"], s0)
        _put_heads(o_ref, _bnn(loc["q_dec"], s0) + _bnn(loc["amat"], v_new))
        state[...] = s0 * jnp.exp(loc["gl"]) + _btn(loc["k_dec"], v_new)
        if fused:
            pl.when(n == (3 * nc) // 4)(forward)
            pl.when(n == nc - 1)(finish)

    tok = pl.BlockSpec((CHUNK, hb * HEAD_DIM), lambda h, n: (n, h))
    row = pl.BlockSpec((hb, nc, CHUNK), lambda h, n: (h, 0, 0))
    any_spec = pl.BlockSpec(memory_space=pl.ANY)
    outs = (_sds((t, D_MODEL)), _sds((DN_HEADS, nc, HEAD_DIM, HEAD_DIM)), _sds((DN_HEADS, nc, CHUNK, CHUNK)),
            _sds((t, D_MODEL)), _sds((t, D_MODEL)))
    out_specs = (tok, pl.BlockSpec((hb, None, HEAD_DIM, HEAD_DIM), lambda h, n: (h, n, 0, 0)),
                 pl.BlockSpec((hb, None, CHUNK, CHUNK), lambda h, n: (h, n, 0, 0)), tok, tok)
    in_specs, args = [tok, tok, tok, row, row], (q, k, v, g_rows, b_rows)
    scratch = [pltpu.VMEM((hb, HEAD_DIM, HEAD_DIM), f32)]
    if fused:
        outs += (_sds((N_DEV,) + next_shard.shape, next_shard.dtype),)
        out_specs += (any_spec,)
        in_specs, args = in_specs + [any_spec], args + (next_shard,)
        scratch = scratch + list(_GATHER_SEMS)
    return _call(body, name, outs, (DN_HEADS // hb, nc), in_specs, out_specs, scratch=scratch)(*args)


def delta_bwd(name, q, k, v, g_rows, b_rows, s_all, t_all, u_all, w_all, do, chip_part=None):
    t = q.shape[0]
    nc = t // CHUNK
    c = CHUNK
    hb = DELTA_HB

    fused = chip_part is not None
    assert not fused or hb == DN_HEADS

    def body(*refs):
        if fused:
            (q_ref, k_ref, v_ref, g_ref, b_ref, s_ref, t_ref, u_ref, w_ref, do_ref, x_ref, dq_ref, dk_ref, dv_ref,
             dg_ref, db_ref, slot_ref, dstate, send_sems, recv_sems, local_sem) = refs
        else:
            (q_ref, k_ref, v_ref, g_ref, b_ref, s_ref, t_ref, u_ref, w_ref, do_ref, dq_ref, dk_ref, dv_ref, dg_ref,
             db_ref, dstate) = refs
        step = pl.program_id(1)
        n = nc - 1 - step
        if fused:
            start, finish = _chip_exchange_steps(x_ref, slot_ref, send_sems, recv_sems, local_sem)
            pl.when(step == 0)(start)

        @pl.when(step == 0)
        def _():
            dstate[...] = jnp.zeros_like(dstate)

        qv, kv, vv = _heads(q_ref, hb), _heads(k_ref, hb), _heads(v_ref, hb)
        L = _delta_local(qv, kv, vv, g_ref[:, pl.ds(n, 1), :], b_ref[:, pl.ds(n, 1), :],
                         solved=(t_ref[...], _heads(u_ref, hb), _heads(w_ref, hb)))
        eye, lower, strict = L["eye"], L["lower"], L["strict"]
        shp = (hb, c, c)
        s0 = s_ref[...]
        dov = _heads(do_ref, hb)
        ds = dstate[...]
        eg = jnp.exp(L["gl"])
        v_new = L["u"] - _bnn(L["w"], s0)
        d_k_dec = _bnt(v_new, ds)
        d_v_new = _bnn(L["k_dec"], ds) + _btn(L["amat"], dov)
        d_eg = jnp.sum(jnp.sum(ds * s0, axis=2, keepdims=True), axis=1, keepdims=True)
        d_q_dec = _bnt(dov, s0)
        d_a = _bnt(dov, v_new)
        d_w = -_bnt(d_v_new, s0)
        dstate[...] = ds * eg + _btn(L["q_dec"], dov) - _btn(L["w"], d_v_new)
        d_am = jnp.where(lower, d_a * L["decay"], 0.0)
        dq = _bnn(d_am, kv) + d_q_dec * L["egc"]
        dk = _btn(d_am, qv) + d_k_dec * L["kdf"]
        e_col = jnp.sum(d_k_dec * L["k_dec"], axis=2, keepdims=True)
        d_gc_col = jnp.sum(d_q_dec * L["q_dec"], axis=2, keepdims=True) - e_col
        d_gl = jnp.sum(e_col, axis=1, keepdims=True) + d_eg * eg
        tinv = L["tinv"]
        d_rhs = _btn(tinv, jnp.concatenate([d_v_new, d_w], axis=2), HIGH)
        d_rhs_u, d_rhs_w = d_rhs[:, :, 0:HEAD_DIM], d_rhs[:, :, HEAD_DIM:]
        d_l = -_bnt(d_rhs, jnp.concatenate([L["u"], L["w"]], axis=2), HIGH)
        _put_heads(dv_ref, d_rhs_u * L["b_col"])
        d_b_col = jnp.sum(d_rhs_u * vv, axis=2, keepdims=True)
        d_gc_col = d_gc_col + jnp.sum(d_rhs_w * L["rhs_w"], axis=2, keepdims=True)
        d_lm = jnp.where(strict, d_l * L["decay"], 0.0)
        d_kb = d_rhs_w * L["egc"] + _bnn(d_lm, kv)
        dk = dk + _btn(d_lm, L["kb"]) + d_kb * L["b_col"]
        d_b_col = d_b_col + jnp.sum(d_kb * kv, axis=2, keepdims=True)
        m = d_am * L["qk"] + d_lm * L["kk"]
        d_gc_col = d_gc_col + jnp.sum(m, axis=2, keepdims=True)
        d_gc_row = (jnp.sum(jnp.where(eye, jnp.broadcast_to(d_gc_col, shp), 0.0), axis=1, keepdims=True)
                    - jnp.sum(m, axis=1, keepdims=True))
        lane = lax.broadcasted_iota(jnp.int32, (1, 1, c), 2)
        d_gc_row = d_gc_row + jnp.where(lane == c - 1, d_gl, 0.0)
        d_gc_tot = jnp.sum(jnp.where(eye, jnp.broadcast_to(d_gc_row, shp), 0.0), axis=2, keepdims=True)
        dg_ref[:, pl.ds(n, 1), :] = jnp.sum(jnp.where(lower, jnp.broadcast_to(d_gc_tot, shp), 0.0), axis=1,
                                            keepdims=True)
        db_ref[:, pl.ds(n, 1), :] = jnp.sum(jnp.where(eye, jnp.broadcast_to(d_b_col, shp), 0.0), axis=1,
                                            keepdims=True)
        _put_heads(dq_ref, dq)
        _put_heads(dk_ref, dk)
        if fused:
            pl.when(step == nc - 1)(finish)

    tok = pl.BlockSpec((CHUNK, hb * HEAD_DIM), lambda h, s: (nc - 1 - s, h))
    row = pl.BlockSpec((hb, nc, CHUNK), lambda h, s: (h, 0, 0))
    any_spec = pl.BlockSpec(memory_space=pl.ANY)
    in_specs = [tok, tok, tok, row, row,
                pl.BlockSpec((hb, None, HEAD_DIM, HEAD_DIM), lambda h, s: (h, nc - 1 - s, 0, 0)),
                pl.BlockSpec((hb, None, CHUNK, CHUNK), lambda h, s: (h, nc - 1 - s, 0, 0)), tok, tok, tok]
    args = (q, k, v, g_rows, b_rows, s_all, t_all, u_all, w_all, do)
    outs = (_sds((t, D_MODEL)),) * 3 + (_sds((DN_HEADS, nc, CHUNK)),) * 2
    out_specs = (tok, tok, tok, row, row)
    scratch = [pltpu.VMEM((hb, HEAD_DIM, HEAD_DIM), f32)]
    if fused:
        in_specs, args = in_specs + [any_spec], args + (chip_part,)
        outs += (_sds(chip_part.shape, chip_part.dtype),)
        out_specs += (any_spec,)
        scratch = scratch + list(_CHIP_SEMS)
    return _call(body, name, outs, (DN_HEADS // hb, nc), in_specs, out_specs, scratch=scratch)(*args)


SEG = 8


def _perm_rows(a):
    t, c = a.shape
    return a.reshape(SEG, t // SEG, c).transpose(1, 0, 2).reshape(t, c)


def _unperm_rows(a):
    t, c = a.shape
    return a.reshape(t // SEG, SEG, c).transpose(1, 0, 2).reshape(t, c)


def _cmul(ar, ai, br, bi):
    return ar * br - ai * bi, ar * bi + ai * br


def _segment_init(er, ei, lr, li, seg_len, reverse):
    w = er.shape[1]
    sub = lax.broadcasted_iota(jnp.int32, (SEG, w), 0)

    def shift(x, k):
        if reverse:
            return jnp.where(sub < SEG - k, pltpu.roll(x, SEG - k, 0), 0.0)
        return jnp.where(sub >= k, pltpu.roll(x, k, 0), 0.0)

    pr, pi = lr, li
    for _ in range(seg_len.bit_length() - 1):
        pr, pi = _cmul(pr, pi, pr, pi)
    fr, fi = shift(er, 1), shift(ei, 1)
    for k in (1, 2, 4):
        sr, si = shift(fr, k), shift(fi, k)
        mr, mi = _cmul(pr, pi, sr, si)
        fr, fi = fr + mr, fi + mi
        pr, pi = _cmul(pr, pi, pr, pi)
    return fr, fi


def s5_fwd(name, u_perm, bbd, cbd, lam):
    t = u_perm.shape[0]
    tt = _tile(t, (2048, 1024, 512, 256, 128))
    nt, ng, w = t // tt, tt // SEG, SG_STATE
    seg_len = t // SEG
    assert seg_len & (seg_len - 1) == 0 and tt % SEG == 0

    def body(u_ref, b_ref, c_ref, lam_ref, y_ref, h_ref, x_scr, state):
        p, i = pl.program_id(1), pl.program_id(2)
        lr1, li1 = lam_ref[:, 0:w], lam_ref[:, w:2 * w]
        lr, li = jnp.broadcast_to(lr1, (SEG, w)), jnp.broadcast_to(li1, (SEG, w))
        base = pl.multiple_of(i * tt, tt)

        @pl.when(p == 0)
        def _():
            x_scr[pl.ds(base, tt), :] = _bdot(u_ref[...], b_ref[...], ((1,), (0,)))

        @pl.when(jnp.logical_and(p == 0, i == 0))
        def _():
            state[...] = jnp.zeros_like(state)

        @pl.when(jnp.logical_and(p == 1, i == 0))
        def _():
            sr, si = _segment_init(state[:, 0:w], state[:, w:2 * w], lr1, li1, seg_len, False)
            state[:, 0:w] = sr
            state[:, w:2 * w] = si

        def run(store):
            def step(g, st):
                row = pl.multiple_of(g * SEG, SEG)
                xg = x_scr[pl.ds(base + row, SEG), :]
                nr = lr * st[0] - li * st[1] + xg[:, 0:w]
                ni = lr * st[1] + li * st[0] + xg[:, w:2 * w]
                if store:
                    h_ref[pl.ds(row, SEG), 0:w] = nr
                    h_ref[pl.ds(row, SEG), w:2 * w] = ni
                return nr, ni

            fin = lax.fori_loop(0, ng, step, (state[:, 0:w], state[:, w:2 * w]))
            state[:, 0:w] = fin[0]
            state[:, w:2 * w] = fin[1]

        @pl.when(p == 0)
        def _():
            run(False)

        @pl.when(p == 1)
        def _():
            run(True)
            y_ref[...] = _bdot(h_ref[...], c_ref[...], ((1,), (0,)))

    return _call(body, name, (_sds((t, D_MODEL)), _sds((t, N_SUPER * 2 * w))), (N_SUPER, 2, nt),
                 [pl.BlockSpec((tt, 128), lambda s, p, i: (i * (1 - p) + (nt - 1) * p, s)),
                  pl.BlockSpec((None, 128, 2 * w), lambda s, p, i: (s, 0, 0)),
                  pl.BlockSpec((None, 2 * w, 128), lambda s, p, i: (s, 0, 0)),
                  pl.BlockSpec((None, 1, 2 * w), lambda s, p, i: (s, 0, 0))],
                 (pl.BlockSpec((tt, 128), lambda s, p, i: (i * p, s)),
                  pl.BlockSpec((tt, 2 * w), lambda s, p, i: (i * p, s))),
                 scratch=[pltpu.VMEM((t, 2 * w), f32), pltpu.VMEM((SEG, 2 * w), f32)])(u_perm, bbd, cbd, lam)


def s5_bwd(name, dy_perm, u_perm, h_perm, hprev0, bbd, cbd, lam):
    t = u_perm.shape[0]
    tt = _tile(t, (2048, 1024, 512, 256, 128))
    nt, ng, w = t // tt, tt // SEG, SG_STATE
    seg_len = t // SEG

    def body(dy_ref, u_ref, h_ref, hp_ref, hp0_ref, b_ref, c_ref, lam_ref, du_ref, db_ref, dc_ref, dl_ref,
             g_scr, state, dl_acc):
        p, i = pl.program_id(1), pl.program_id(2)
        first_tile = jnp.logical_or(p == 0, i == nt - 1)
        lr1, li1 = lam_ref[:, 0:w], -lam_ref[:, w:2 * w]
        lr, li = jnp.broadcast_to(lr1, (SEG, w)), jnp.broadcast_to(li1, (SEG, w))
        base = pl.multiple_of((nt - 1 - i) * tt, tt)

        @pl.when(p == 0)
        def _():
            g_scr[pl.ds(base, tt), :] = _bdot(dy_ref[...], c_ref[...], ((1,), (1,)))

        @pl.when(jnp.logical_and(p == 0, i == 0))
        def _():
            state[...] = jnp.zeros_like(state)

        @pl.when(jnp.logical_and(p == 1, i == 0))
        def _():
            sr, si = _segment_init(state[:, 0:w], state[:, w:2 * w], lr1, li1, seg_len, True)
            state[:, 0:w] = sr
            state[:, w:2 * w] = si
            dl_acc[...] = jnp.zeros_like(dl_acc)

        def adj(g, st):
            row = pl.multiple_of(g * SEG, SEG)
            gg = g_scr[pl.ds(base + row, SEG), :]
            nr = lr * st[0] - li * st[1] + gg[:, 0:w]
            ni = lr * st[1] + li * st[0] + gg[:, w:2 * w]
            return row, nr, ni

        @pl.when(p == 0)
        def _():
            def step(k, st):
                _, nr, ni = adj(ng - 1 - k, st)
                return nr, ni

            fin = lax.fori_loop(0, ng, step, (state[:, 0:w], state[:, w:2 * w]))
            state[:, 0:w] = fin[0]
            state[:, w:2 * w] = fin[1]

        @pl.when(p == 1)
        def _():
            above = jnp.where(first_tile, hp0_ref[...], hp_ref[...])

            def step(k, st):
                g = ng - 1 - k
                row, nr, ni = adj(g, st)
                g_scr[pl.ds(base + row, SEG), 0:w] = nr
                g_scr[pl.ds(base + row, SEG), w:2 * w] = ni
                prow = pl.multiple_of(jnp.maximum(g - 1, 0) * SEG, SEG)
                hp = jnp.where(g > 0, h_ref[pl.ds(prow, SEG), :], above)
                pr, pi = hp[:, 0:w], hp[:, w:2 * w]
                return nr, ni, st[2] + nr * pr + ni * pi, st[3] + ni * pr - nr * pi

            fin = lax.fori_loop(0, ng, step, (state[:, 0:w], state[:, w:2 * w], dl_acc[:, 0:w], dl_acc[:, w:2 * w]))
            state[:, 0:w] = fin[0]
            state[:, w:2 * w] = fin[1]
            dl_acc[:, 0:w] = fin[2]
            dl_acc[:, w:2 * w] = fin[3]
            a = g_scr[pl.ds(base, tt), :]
            du_ref[...] = _bdot(a, b_ref[...], ((1,), (1,)))
            d_b = _bdot(u_ref[...], a, ((0,), (0,)))
            d_c = _bdot(dy_ref[...], h_ref[...], ((0,), (0,)))

            @pl.when(i == 0)
            def _():
                db_ref[...] = d_b
                dc_ref[...] = d_c

            @pl.when(i > 0)
            def _():
                db_ref[...] += d_b
                dc_ref[...] += d_c

            @pl.when(i == nt - 1)
            def _():
                dl_ref[...] = jnp.sum(dl_acc[...], axis=0, keepdims=True)

    tile = lambda s, p, i: (nt - 1 - i, s)
    tile1 = lambda s, p, i: (nt - 1 - i * p, s)
    above = lambda s, p, i: (jnp.maximum((nt - 1 - i * p) * (tt // SEG) - 1, 0), s)
    per_s = lambda s, p, i: (s, 0, 0)
    return _call(body, name, (_sds((t, D_MODEL)), _sds((N_SUPER, 128, 2 * w)), _sds((N_SUPER, 128, 2 * w)),
                              _sds((N_SUPER, 1, 2 * w))), (N_SUPER, 2, nt),
                 [pl.BlockSpec((tt, 128), tile), pl.BlockSpec((tt, 128), tile1), pl.BlockSpec((tt, 2 * w), tile1),
                  pl.BlockSpec((SEG, 2 * w), above), pl.BlockSpec((SEG, 2 * w), lambda s, p, i: (0, s)),
                  pl.BlockSpec((None, 128, 2 * w), per_s), pl.BlockSpec((None, 2 * w, 128), per_s),
                  pl.BlockSpec((None, 1, 2 * w), per_s)],
                 (pl.BlockSpec((tt, 128), tile1), pl.BlockSpec((None, 128, 2 * w), per_s),
                  pl.BlockSpec((None, 128, 2 * w), per_s), pl.BlockSpec((None, 1, 2 * w), per_s)),
                 scratch=[pltpu.VMEM((t, 2 * w), f32), pltpu.VMEM((SEG, 2 * w), f32),
                          pltpu.VMEM((SEG, 2 * w), f32)])(dy_perm, u_perm, h_perm, h_perm, hprev0, bbd, cbd, lam)


def _s5_disc(a_re, a_im, log_dt, br, bi):
    dt = jnp.exp(log_dt)
    mag = jnp.exp(a_re * dt)
    lr, li = mag * jnp.cos(a_im * dt), mag * jnp.sin(a_im * dt)
    den = a_re * a_re + a_im * a_im
    fr = ((lr - 1.0) * a_re + li * a_im) / den
    fi = (li * a_re - (lr - 1.0) * a_im) / den
    return lr, li, fr * br - fi * bi, fr * bi + fi * br


def s5_disc_fwd(name, a_re, a_im, log_dt, br, bi):
    g, n = SSM_GROUPS, SSM_STATE

    def body(ar, ai, ld, brr, bir, lr, li, bbr, bbi):
        o = _s5_disc(ar[...], ai[...], ld[...], brr[...], bir[...])
        lr[...], li[...], bbr[...], bbi[...] = o

    return _call(body, name, (_sds((g, 1, n)), _sds((g, 1, n)), _sds((g, SSM_GROUP, n)), _sds((g, SSM_GROUP, n))))(
        a_re, a_im, log_dt, br, bi)


def s5_disc_bwd(name, a_re, a_im, log_dt, br, bi, d_lr, d_li, d_bbr, d_bbi):
    g, n = SSM_GROUPS, SSM_STATE

    def body(ar, ai, ld, brr, bir, c1, c2, c3, c4, o1, o2, o3, o4, o5):
        _, vjp = jax.vjp(_s5_disc, ar[...], ai[...], ld[...], brr[...], bir[...])
        o1[...], o2[...], o3[...], o4[...], o5[...] = vjp((c1[...], c2[...], c3[...], c4[...]))

    return _call(body, name, (_sds((g, 1, n)), _sds((g, 1, n)), _sds((g, 1, 1)), _sds((g, SSM_GROUP, n)),
                              _sds((g, SSM_GROUP, n))))(a_re, a_im, log_dt, br, bi, d_lr, d_li, d_bbr, d_bbi)


def gelu_fwd(name, ys_lin, proj, d_skip):
    t, d = ys_lin.shape
    tm = _rows(t, light=True)

    def body(y_ref, u_ref, d_ref, o_ref):
        o_ref[...] = _gelu(y_ref[...] + d_ref[...] * u_ref[...])

    return _call(body, name, _sds((t, d)), (t // tm,),
                 [_row_spec(tm, d), pl.BlockSpec((tm, d), lambda i: (i, 4)), _acc_spec(1, d)],
                 _row_spec(tm, d))(ys_lin, proj, d_skip)


def _head_norm(o, hn):
    outs, ns, rs = [], [], []
    for h in range(DN_HEADS):
        oh = o[:, h * 128:(h + 1) * 128]
        r = lax.rsqrt(jnp.mean(oh * oh, axis=-1, keepdims=True) + EPS)
        n = oh * r
        ns.append(n)
        rs.append(r)
        outs.append(n * hn)
    return outs, ns, rs


def merge_fwd(name, proj, o, yg, glu_lin, head_norm, b_glu):
    t = o.shape[0]
    tm = _rows(t)
    d = D_MODEL

    def body(za_ref, zb_ref, ra_ref, rb_ref, o_ref, yg_ref, gl_ref, hn_ref, bg_ref, m_ref):
        on, _, _ = _head_norm(o_ref[...], hn_ref[...])
        y_a = jnp.concatenate(on, axis=1) * _silu(za_ref[...])
        y_b = yg_ref[...] * _sigmoid(gl_ref[...] + bg_ref[...]) * _silu(zb_ref[...])
        m_ref[...] = (_sigmoid(ra_ref[...]) * y_a + _sigmoid(rb_ref[...]) * y_b).astype(bf16)

    pc = lambda c: pl.BlockSpec((tm, d), lambda i: (i, c))
    return _call(body, name, _sds((t, d), bf16), (t // tm,),
                 [pc(3), pc(5), pc(6), pc(7), _row_spec(tm, d), _row_spec(tm, d), _row_spec(tm, d),
                  _acc_spec(1, 128), _acc_spec(1, d)], _row_spec(tm, d))(
        proj, proj, proj, proj, o, yg, glu_lin, head_norm, b_glu)


def merge_bwd(name, proj, o, yg, glu_lin, head_norm, b_glu, dm):
    t = o.shape[0]
    tm = _rows(t)
    d = D_MODEL

    def body(za_ref, zb_ref, ra_ref, rb_ref, o_ref, yg_ref, gl_ref, hn_ref, bg_ref, dm_ref,
             dza_ref, dzb_ref, dra_ref, drb_ref, do_ref, dgl_ref, dyg_ref, dhn_ref, dbg_ref):
        hn = hn_ref[...]
        za, zb, ra, rb = za_ref[...], zb_ref[...], ra_ref[...], rb_ref[...]
        on, ns, rs = _head_norm(o_ref[...], hn)
        onc = jnp.concatenate(on, axis=1)
        sga, sgb = _sigmoid(za), _sigmoid(zb)
        sza = za * sga
        y_a = onc * sza
        yg = yg_ref[...]
        sgl = _sigmoid(gl_ref[...] + bg_ref[...])
        y2 = yg * sgl
        szb = zb * sgb
        y_b = y2 * szb
        sra, srb = _sigmoid(ra), _sigmoid(rb)
        dmv = dm_ref[...]
        dra_ref[...] = (dmv * y_a * sra * (1.0 - sra)).astype(bf16)
        drb_ref[...] = (dmv * y_b * srb * (1.0 - srb)).astype(bf16)
        d_ya = dmv * sra
        d_yb = dmv * srb
        dza_ref[...] = (d_ya * onc * (sga * (1.0 + za * (1.0 - sga)))).astype(bf16)
        dzb_ref[...] = (d_yb * y2 * (sgb * (1.0 + zb * (1.0 - sgb)))).astype(bf16)
        d_on = d_ya * sza
        d_y2 = d_yb * szb
        dyg_ref[...] = d_y2 * sgl
        d_gl = d_y2 * yg * sgl * (1.0 - sgl)
        dgl_ref[...] = d_gl.astype(bf16)
        _accumulate(dbg_ref, jnp.sum(d_gl, axis=0, keepdims=True))
        d_hn = jnp.zeros((1, 128), f32)
        for h in range(DN_HEADS):
            hs = slice(h * 128, (h + 1) * 128)
            dh = d_on[:, hs]
            d_hn = d_hn + jnp.sum(dh * ns[h], axis=0, keepdims=True)
            dn = dh * hn
            do_ref[:, hs] = rs[h] * (dn - ns[h] * jnp.mean(dn * ns[h], axis=-1, keepdims=True))
        _accumulate(dhn_ref, d_hn)

    pc = lambda c: pl.BlockSpec((tm, d), lambda i: (i, c))
    rs_ = _row_spec(tm, d)
    return _call(body, name, (_sds((t, d), bf16),) * 4 + (_sds((t, d)), _sds((t, d), bf16), _sds((t, d)),
                              _sds((1, 128)), _sds((1, d))),
                 (t // tm,), [pc(3), pc(5), pc(6), pc(7), rs_, rs_, rs_, _acc_spec(1, 128), _acc_spec(1, d), rs_],
                 (rs_,) * 7 + (_acc_spec(1, 128), _acc_spec(1, d)))(
        proj, proj, proj, proj, o, yg, glu_lin, head_norm, b_glu, dm)


def gelu_bwd(name, ys_lin, proj, d_skip, dyg_a, dyg_b):
    t, d = ys_lin.shape
    tm = _rows(t, light=True)

    def body(y_ref, u_ref, d_ref, a_ref, b_ref, dys_ref, du_ref, dd_ref):
        uv = u_ref[...]
        dys = (a_ref[...] + b_ref[...]) * _dgelu(y_ref[...] + d_ref[...] * uv)
        dys_ref[...] = dys
        du_ref[...] = dys * d_ref[...]
        _accumulate(dd_ref, jnp.sum(dys * uv, axis=0, keepdims=True))

    rs_ = _row_spec(tm, d)
    return _call(body, name, (_sds((t, d)), _sds((t, d)), _sds((1, d))), (t // tm,),
                 [rs_, pl.BlockSpec((tm, d), lambda i: (i, 4)), _acc_spec(1, d), rs_, rs_],
                 (rs_, rs_, _acc_spec(1, d)))(ys_lin, proj, d_skip, dyg_a, dyg_b)


def assemble_dproj(name, dc, conv_w, dza, du_a, du_b, dzb, dra, drb, dbd):
    t = dza.shape[0]
    tm = _rows(t)
    nt = t // tm
    d = D_MODEL
    w3 = 3 * d

    def body(cur_ref, nxt_ref, w_ref, za_ref, ua_ref, ub_ref, zb_ref, ra_ref, rb_ref, bd_ref, o_ref, pad_ref):
        keep = (pl.program_id(0) < nt - 1).astype(f32)
        pad_ref[0:tm, :] = cur_ref[...]
        pad_ref[tm:tm + 8, :] = nxt_ref[...] * keep
        for cb in range(w3 // 128):
            cs = slice(cb * 128, (cb + 1) * 128)
            acc = pad_ref[pl.ds(CONV_K - 1, tm), cs] * w_ref[0:1, cs]
            for j in range(1, CONV_K):
                acc = acc + pad_ref[pl.ds(CONV_K - 1 - j, tm), cs] * w_ref[j:j + 1, cs]
            o_ref[:, cs] = acc.astype(bf16)
        o_ref[:, 3 * d:4 * d] = za_ref[...].astype(bf16)
        o_ref[:, 4 * d:5 * d] = (ua_ref[...] + ub_ref[...]).astype(bf16)
        o_ref[:, 5 * d:6 * d] = zb_ref[...].astype(bf16)
        o_ref[:, 6 * d:7 * d] = ra_ref[...].astype(bf16)
        o_ref[:, 7 * d:8 * d] = rb_ref[...].astype(bf16)
        o_ref[:, 8 * d:8 * d + 128] = bd_ref[...].astype(bf16)
        o_ref[:, 8 * d + 128:W_PAD] = jnp.zeros((tm, W_PAD - 8 * d - 128), bf16)

    rs_ = _row_spec(tm, d)
    return _call(body, name, _sds((t, W_PAD), bf16), (nt,),
                 [_row_spec(tm, w3), _next_spec(tm, w3, t), _acc_spec(CONV_K, w3), rs_, rs_, rs_, rs_, rs_, rs_,
                  _row_spec(tm, 128)], _row_spec(tm, W_PAD),
                 scratch=[pltpu.VMEM((tm + 8, w3), f32)])(dc, dc, conv_w, dza, du_a, du_b, dzb, dra, drb, dbd)


def adamw(name, w, g, m, v):
    lead, (r, c) = w.shape[:-2], w.shape[-2:]
    tm = _tile(r, (512, 256, 128, 64, 32, 16, 8))
    c1 = 1.0 / (1.0 - ADAM_B1 ** ADAM_STEP)
    c2 = 1.0 / (1.0 - ADAM_B2 ** ADAM_STEP)

    def body(w_ref, g_ref, m_ref, v_ref, d_ref, nm_ref, nv_ref):
        gv = g_ref[...]
        nm = ADAM_B1 * m_ref[...] + (1.0 - ADAM_B1) * gv
        nv = ADAM_B2 * v_ref[...] + (1.0 - ADAM_B2) * (gv * gv)
        d_ref[...] = -ADAM_LR * ((nm * c1) / (jnp.sqrt(nv * c2) + ADAM_EPS) + ADAM_WD * w_ref[...])
        nm_ref[...] = nm
        nv_ref[...] = nv

    if lead:
        sp = pl.BlockSpec((None, tm, c), lambda l, i: (l, i, 0))
        grid = (lead[0], r // tm)
    else:
        sp = pl.BlockSpec((tm, c), lambda i: (i, 0))
        grid = (r // tm,)
    return _call(body, name, (_sds(w.shape),) * 3, grid, [sp] * 4, (sp,) * 3)(w, g, m, v)


def _coords():
    return lax.axis_index("x"), lax.axis_index("y"), lax.axis_index("c")


def _lin(dev):
    return 4 * dev[0] + 2 * dev[1] + dev[2]


def _chips(me):
    x, y, _ = me
    return [(1 - x, y), (x, 1 - y), (1 - x, 1 - y)]


def _gather_steps(x_ref, o_ref, send_sems, recv_sems, local_sem):
    me = _coords()
    x, y, cc = me
    sibling = (x, y, 1 - cc)
    chips = _chips(me)

    def copy(k, block, to, src=None):
        return pltpu.make_async_remote_copy(
            src_ref=o_ref.at[_lin(block)] if src is None else src, dst_ref=o_ref.at[_lin(block)],
            send_sem=send_sems.at[k], recv_sem=recv_sems.at[k], device_id=to, device_id_type=MESH)

    mine = pltpu.make_async_copy(x_ref, o_ref.at[_lin(me)], local_sem)
    first = [copy(0, me, sibling, src=x_ref)] + [copy(1 + j, me, (*chip, cc), src=x_ref)
                                                 for j, chip in enumerate(chips)]
    passed = [copy(4 + j, (*chip, cc), sibling) for j, chip in enumerate(chips)]

    def start():
        mine.start()
        for cp in first:
            cp.start()

    def forward():
        for j, chip in enumerate(chips):
            copy(1 + j, (*chip, cc), me).wait_recv()
            passed[j].start()

    def finish():
        copy(0, sibling, me).wait_recv()
        for j, chip in enumerate(chips):
            copy(4 + j, (*chip, 1 - cc), me).wait_recv()
        for cp in first + passed:
            cp.wait_send()
        mine.wait()

    return start, forward, finish


_GATHER_SEMS = [pltpu.SemaphoreType.DMA((N_DEV - 1,)), pltpu.SemaphoreType.DMA((N_DEV - 1,)),
                pltpu.SemaphoreType.DMA(())]


def all_gather(name, shard):
    r, c = shard.shape

    def body(x_ref, o_ref, send_sems, recv_sems, local_sem):
        for step in _gather_steps(x_ref, o_ref, send_sems, recv_sems, local_sem):
            step()

    any_spec = pl.BlockSpec(memory_space=pl.ANY)
    return _call(body, name, _sds((N_DEV, r, c), shard.dtype), in_specs=[any_spec], out_specs=any_spec,
                 scratch=list(_GATHER_SEMS))(shard)


def pair_exchange(name, blocks):
    _, _, r, c = blocks.shape

    def body(x_ref, o_ref, send_sems, recv_sems):
        x, y, cc = _coords()
        sibling = (x, y, 1 - cc)
        cps = [pltpu.make_async_remote_copy(src_ref=x_ref.at[ch, 1 - cc], dst_ref=o_ref.at[ch], send_sem=send_sems.at[ch],
                                            recv_sem=recv_sems.at[ch], device_id=sibling, device_id_type=MESH)
               for ch in range(4)]
        for cp in cps:
            cp.start()
        for cp in cps:
            cp.wait()

    any_spec = pl.BlockSpec(memory_space=pl.ANY)
    return _call(body, name, _sds((4, r, c), blocks.dtype), in_specs=[any_spec], out_specs=any_spec,
                 scratch=[pltpu.SemaphoreType.DMA((4,)), pltpu.SemaphoreType.DMA((4,))])(blocks)


def _chip_exchange_steps(x_ref, o_ref, send_sems, recv_sems, local_sem):
    me = _coords()
    x, y, cc = me
    my_chip = 2 * x + y
    mine = pltpu.make_async_copy(x_ref.at[my_chip], o_ref.at[my_chip], local_sem)
    peers = [(px, py, cc) for px, py in _chips(me)]
    sends = [pltpu.make_async_remote_copy(src_ref=x_ref.at[2 * px + py], dst_ref=o_ref.at[my_chip],
                                          send_sem=send_sems.at[j], recv_sem=recv_sems.at[j], device_id=(px, py, pc),
                                          device_id_type=MESH) for j, (px, py, pc) in enumerate(peers)]

    def start():
        mine.start()
        for cp in sends:
            cp.start()

    def finish():
        for j, (px, py, pc) in enumerate(peers):
            pltpu.make_async_remote_copy(src_ref=x_ref.at[my_chip], dst_ref=o_ref.at[2 * px + py],
                                         send_sem=send_sems.at[j], recv_sem=recv_sems.at[j], device_id=(px, py, pc),
                                         device_id_type=MESH).wait_recv()
        for cp in sends:
            cp.wait_send()
        mine.wait()

    return start, finish


_CHIP_SEMS = [pltpu.SemaphoreType.DMA((3,)), pltpu.SemaphoreType.DMA((3,)), pltpu.SemaphoreType.DMA(())]


def chip_exchange(name, blocks):
    def body(x_ref, o_ref, send_sems, recv_sems, local_sem):
        for step in _chip_exchange_steps(x_ref, o_ref, send_sems, recv_sems, local_sem):
            step()

    any_spec = pl.BlockSpec(memory_space=pl.ANY)
    return _call(body, name, _sds(blocks.shape, blocks.dtype), in_specs=[any_spec], out_specs=any_spec,
                 scratch=list(_CHIP_SEMS))(blocks)


def pair_sum(name, mine, theirs):
    _, r, c = mine.shape
    tm = max(d for d in range(16, 769, 16) if r % d == 0)

    def body(a_ref, b_ref, o_ref):
        o_ref[...] = (a_ref[...].astype(f32) + b_ref[...].astype(f32)).astype(o_ref.dtype)

    sp = pl.BlockSpec((None, tm, c), lambda ch, i: (ch, i, 0))
    return _call(body, name, _sds(mine.shape, mine.dtype), (4, r // tm), [sp, sp], sp)(mine, theirs)


def sum_slots(name, slots):
    n, r, c = slots.shape
    tm = max(d for d in range(16, 769, 16) if r % d == 0)

    def body(s_ref, o_ref):
        acc = s_ref[0].astype(f32)
        for d in range(1, n):
            acc = acc + s_ref[d].astype(f32)
        o_ref[...] = acc

    return _call(body, name, _sds((r, c)), (r // tm,), [pl.BlockSpec((n, tm, c), lambda i: (0, i, 0))],
                 pl.BlockSpec((tm, c), lambda i: (i, 0)))(slots)


def _rows_layout(col8, t):
    return col8.T.reshape(DN_HEADS, t // CHUNK, CHUNK)


def _blockdiag(m):
    g, a, b = m.shape
    m = m.reshape(N_SUPER, SUPER, a, b)
    out = jnp.einsum("sgab,gh->sgahb", m, jnp.eye(SUPER, dtype=m.dtype))
    return out.reshape(N_SUPER, SUPER * a, SUPER * b)


def _diag_blocks(m, a, b):
    m = m.reshape(N_SUPER, SUPER, a, SUPER, b)
    return jnp.einsum("sgahb,gh->sgab", m, jnp.eye(SUPER, dtype=m.dtype)).reshape(SSM_GROUPS, a, b)


def _s5_params(p, li):
    tag = f"l{li}"
    n = SSM_STATE
    a_re = p["ssm_a_re"].reshape(SSM_GROUPS, 1, n)
    a_im = p["ssm_a_im"].reshape(SSM_GROUPS, 1, n)
    log_dt = p["ssm_log_dt"].reshape(SSM_GROUPS, 1, 1)
    br = jnp.swapaxes(p["ssm_b_re"], 1, 2)
    bi = jnp.swapaxes(p["ssm_b_im"], 1, 2)
    lr, li_, bbr, bbi = s5_disc_fwd("s5_disc_" + tag, a_re, a_im, log_dt, br, bi)
    lam = jnp.concatenate([lr.reshape(N_SUPER, 1, SG_STATE), li_.reshape(N_SUPER, 1, SG_STATE)], axis=-1)
    bbd = jnp.concatenate([_blockdiag(bbr), _blockdiag(bbi)], axis=-1).astype(bf16)
    c_re = jnp.swapaxes(p["ssm_c_re"], 1, 2)
    c_im = jnp.swapaxes(p["ssm_c_im"], 1, 2)
    cbd = jnp.concatenate([_blockdiag(c_re), -_blockdiag(c_im)], axis=1).astype(bf16)
    return dict(a_re=a_re, a_im=a_im, log_dt=log_dt, br=br, bi=bi, lam=lam, bbd=bbd, cbd=cbd)


def layer_fwd(x, p, li, next_shard=None):
    tag = f"l{li}"
    t = x.shape[0]
    d = D_MODEL
    h = rmsnorm_fwd("norm_pre_" + tag, x, p["norm_pre"])
    proj = mm_nn("proj_" + tag, h, p["w_all"])
    q, k, v, bg = conv_qkv_fwd("conv_" + tag, proj, p["conv_w"], p["a_log"], p["dt_bias"])
    b_rows = _rows_layout(bg[:, 0:DN_HEADS], t)
    g_rows = _rows_layout(bg[:, DN_HEADS:2 * DN_HEADS], t)
    o, s_all, t_all, u_all, w_sol, *gathered = delta_fwd("delta_" + tag, q, k, v, g_rows, b_rows, next_shard)
    sp = _s5_params(p, li)
    u_perm = _perm_rows(proj[:, 4 * d:5 * d])
    ys_perm, hs = s5_fwd("s5_" + tag, u_perm, sp["bbd"], sp["cbd"], sp["lam"])
    ys_lin = _unperm_rows(ys_perm)
    yg = gelu_fwd("gelu_" + tag, ys_lin, proj, p["ssm_d"])
    glu_lin = mm_nn("glu_" + tag, yg, p["w_glu"])
    merged = merge_fwd("merge_" + tag, proj, o, yg, glu_lin, p["head_norm"], p["b_glu"])
    out = mm_nn("out_" + tag, merged, p["w_out"])
    y = residual_norm_fwd("norm_post_" + tag, x, out, p["norm_post"])
    saved = dict(x=x, h=h, proj=proj, q=q, k=k, v=v, g_rows=g_rows, b_rows=b_rows, o=o, s_all=s_all, t_all=t_all, u_all=u_all, w_sol=w_sol, sp=sp, u_perm=u_perm,
                 hs=hs, ys_lin=ys_lin, yg=yg, glu_lin=glu_lin, merged=merged, out=out)
    return y, saved, (gathered[0] if gathered else None)


def layer_bwd(dy, p, s, li, chip_part=None):
    tag = f"l{li}"
    t = dy.shape[0]
    d = D_MODEL
    sp = s["sp"]
    gr = {}
    d_out, gr["norm_post"] = post_norm_bwd("norm_post_b_" + tag, s["out"], p["norm_post"], dy)
    d_merged = mm_nt("out_b_" + tag, d_out, p["w_out"])
    gr["w_out"] = mm_nn("out_w_" + tag, s["merged"].T, d_out, bf16)
    (dza, dzb, dra, drb, d_o, d_glu, dyg_a, gr["head_norm"], gr["b_glu"]) = merge_bwd(
        "merge_b_" + tag, s["proj"], s["o"], s["yg"], s["glu_lin"], p["head_norm"], p["b_glu"], d_merged)
    dyg_b = mm_nt("glu_b_" + tag, d_glu, p["w_glu"])
    gr["w_glu"] = mm_nn("glu_w_" + tag, s["yg"].astype(bf16).T, d_glu, bf16)
    d_ys, du_a, gr["ssm_d"] = gelu_bwd("gelu_b_" + tag, s["ys_lin"], s["proj"], p["ssm_d"], dyg_a, dyg_b)
    hprev0 = jnp.concatenate([jnp.zeros((1, s["hs"].shape[1]), f32), s["hs"][-SEG:-1]], axis=0)
    du_perm, d_bbd, d_cbd, d_lam = s5_bwd("s5_b_" + tag, _perm_rows(d_ys), s["u_perm"], s["hs"], hprev0, sp["bbd"],
                                          sp["cbd"], sp["lam"])
    du_b = _unperm_rows(du_perm)
    gr["ssm_c_re"] = _diag_blocks(d_cbd[:, :, 0:SG_STATE], SSM_GROUP, SSM_STATE)
    gr["ssm_c_im"] = -_diag_blocks(d_cbd[:, :, SG_STATE:], SSM_GROUP, SSM_STATE)
    d_bbr = _diag_blocks(d_bbd[:, :, 0:SG_STATE], SSM_GROUP, SSM_STATE)
    d_bbi = _diag_blocks(d_bbd[:, :, SG_STATE:], SSM_GROUP, SSM_STATE)
    d_lr = d_lam[:, :, 0:SG_STATE].reshape(SSM_GROUPS, 1, SSM_STATE)
    d_li = d_lam[:, :, SG_STATE:].reshape(SSM_GROUPS, 1, SSM_STATE)
    d_are, d_aim, d_ldt, d_br, d_bi = s5_disc_bwd("s5_disc_b_" + tag, sp["a_re"], sp["a_im"], sp["log_dt"], sp["br"],
                                                  sp["bi"], d_lr, d_li, d_bbr, d_bbi)
    gr["ssm_a_re"] = d_are.reshape(SSM_GROUPS, SSM_STATE)
    gr["ssm_a_im"] = d_aim.reshape(SSM_GROUPS, SSM_STATE)
    gr["ssm_log_dt"] = d_ldt.reshape(SSM_GROUPS)
    gr["ssm_b_re"] = jnp.swapaxes(d_br, 1, 2)
    gr["ssm_b_im"] = jnp.swapaxes(d_bi, 1, 2)
    dq, dk, dv, dg_rows, db_rows, *slots = delta_bwd("delta_b_" + tag, s["q"], s["k"], s["v"], s["g_rows"],
                                                     s["b_rows"], s["s_all"], s["t_all"], s["u_all"], s["w_sol"], d_o,
                                                     chip_part)
    dbg = jnp.concatenate([db_rows.reshape(DN_HEADS, t).T, dg_rows.reshape(DN_HEADS, t).T,
                           jnp.zeros((t, 128 - 2 * DN_HEADS), f32)], axis=1)
    dc, gr["conv_w"], dbd, dab = conv_qkv_bwd("conv_b_" + tag, s["proj"], p["conv_w"], p["a_log"], p["dt_bias"],
                                              dq, dk, dv, dbg)
    gr["a_log"] = dab[0, DN_HEADS:2 * DN_HEADS]
    gr["dt_bias"] = dab[1, DN_HEADS:2 * DN_HEADS]
    dproj = assemble_dproj("dproj_" + tag, dc, p["conv_w"], dza, du_a, du_b, dzb, dra, drb, dbd)
    d_h = mm_nt("proj_b_" + tag, dproj, p["w_all"])
    gr["w_all"] = mm_nn("proj_w_" + tag, s["h"].T, dproj, bf16)
    dx, gr["norm_pre"] = rmsnorm_bwd("norm_pre_b_" + tag, s["x"], p["norm_pre"], d_h, dy)
    return dx, gr, (slots[0] if slots else None)


REPL = ["norm_pre", "a_log", "dt_bias", "head_norm", "ssm_a_re", "ssm_a_im", "ssm_log_dt", "ssm_b_re", "ssm_b_im",
        "ssm_c_re", "ssm_c_im", "ssm_d", "b_glu", "norm_post"]
SHARDED = ["w_in", "conv_w", "w_glu", "w_out"]
ALL_W = ["norm_pre", "w_in", "conv_w", "a_log", "dt_bias", "head_norm", "ssm_a_re", "ssm_a_im", "ssm_log_dt",
         "ssm_b_re", "ssm_b_im", "ssm_c_re", "ssm_c_im", "ssm_d", "w_glu", "b_glu", "w_out", "norm_post"]
PACK_W = 1024


def _pack_flat(arrs, rows):
    flat = jnp.concatenate([a.reshape(-1) for a in arrs])
    return jnp.pad(flat, (0, rows * PACK_W - flat.shape[0])).reshape(rows, PACK_W)


def _flat_rows(arrs, mult=8):
    n = sum(math.prod(a.shape) for a in arrs)
    rows = -(-n // PACK_W)
    return -(-rows // mult) * mult


def _unpack(flat, shapes):
    out, off = [], 0
    for sh in shapes:
        n = math.prod(sh)
        out.append(flat[off:off + n].reshape(sh))
        off += n
    return out


def _repl_split(shapes):
    big = [n for n in REPL if math.prod(shapes[n]) % PACK_W == 0]
    small = [n for n in REPL if n not in big]
    return big, small


def _rows8(n):
    return -(-n // (8 * PACK_W)) * 8


def _repl_rows(shapes):
    big, small = _repl_split(shapes)
    rows = sum(_rows8(math.prod(shapes[n])) for n in big) + _flat_rows([_sds(shapes[n]) for n in small], 8)
    return -(-rows // (16 * N_DEV)) * (16 * N_DEV)


def _repl_pack(arrs, shapes):
    big, small = _repl_split(shapes)
    parts = []
    for n in big:
        a = arrs[n].reshape(-1, PACK_W)
        parts.append(jnp.pad(a, ((0, _rows8(a.size) - a.shape[0]), (0, 0))))
    parts.append(_pack_flat([arrs[n] for n in small], _flat_rows([_sds(shapes[n]) for n in small], 8)))
    used = sum(p.shape[0] for p in parts)
    parts.append(jnp.zeros((_repl_rows(shapes) - used, PACK_W), parts[0].dtype))
    return jnp.concatenate(parts, axis=0)


def _repl_unpack(packed, shapes):
    big, small = _repl_split(shapes)
    out, off = {}, 0
    for n in big:
        size = math.prod(shapes[n])
        out[n] = packed[off:off + size // PACK_W].reshape(shapes[n])
        off += _rows8(size)
    srows = _flat_rows([_sds(shapes[n]) for n in small], 8)
    out.update(zip(small, _unpack(packed[off:off + srows].reshape(-1), [shapes[n] for n in small])))
    return out


_COL_RUNS = ((0, 4096), (4112, W_COLS), (4096, 4112))


def _w_all(main, tails, wc):
    pieces = []
    for lo, hi in _COL_RUNS:
        for dv in range(N_DEV):
            a, b = max(lo, dv * wc) - dv * wc, min(hi, (dv + 1) * wc) - dv * wc
            if a < min(b, PACK_W):
                pieces.append(main[dv][:, :, a:min(b, PACK_W)])
            if b > max(a, PACK_W):
                pieces.append(tails[dv][:, :, max(a, PACK_W) - PACK_W:b - PACK_W])
    ll, rows = main[0].shape[0], main[0].shape[1]
    pieces.append(jnp.zeros((ll, rows, W_PAD - W_COLS), main[0].dtype))
    return jnp.concatenate(pieces, axis=2)


def _ref_cols(g, lo, hi):
    pieces, off = [], 0
    for a, b in _COL_RUNS:
        s, e = max(lo, a), min(hi, b)
        if s < e:
            pieces.append((s, g[..., off + s - a:off + e - a]))
        off += b - a
    pieces.sort(key=lambda t: t[0])
    return jnp.concatenate([p for _, p in pieces], axis=-1) if len(pieces) > 1 else pieces[0][1]


def kernel(x, norm_pre, w_in, conv_w, a_log, dt_bias, head_norm, ssm_a_re, ssm_a_im, ssm_log_dt, ssm_b_re, ssm_b_im, ssm_c_re, ssm_c_im, ssm_d, w_glu, b_glu, w_out, norm_post, loss_target, m_norm_pre, m_w_in, m_conv_w, m_a_log, m_dt_bias, m_head_norm, m_ssm_a_re, m_ssm_a_im, m_ssm_log_dt, m_ssm_b_re, m_ssm_b_im, m_ssm_c_re, m_ssm_c_im, m_ssm_d, m_w_glu, m_b_glu, m_w_out, m_norm_post, v_norm_pre, v_w_in, v_conv_w, v_a_log, v_dt_bias, v_head_norm, v_ssm_a_re, v_ssm_a_im, v_ssm_log_dt, v_ssm_b_re, v_ssm_b_im, v_ssm_c_re, v_ssm_c_im, v_ssm_d, v_w_glu, v_b_glu, v_w_out, v_norm_post):
    loc = dict(locals())
    w = {n: loc[n] for n in ALL_W}
    m = {n: loc["m_" + n] for n in ALL_W}
    v = {n: loc["v_" + n] for n in ALL_W}
    depth = w_in.shape[0]
    wc = w_in.shape[2]
    cc = conv_w.shape[2]
    wr = w_glu.shape[1]

    tail = wc - PACK_W
    conv_hi = conv_w.astype(bf16)
    conv_mid = (conv_w - conv_hi.astype(f32)).astype(bf16)
    conv_lo = (conv_w - conv_hi.astype(f32) - conv_mid.astype(f32)).astype(bf16)
    conv3 = jnp.stack([conv_hi, conv_mid, conv_lo], axis=1)
    w_in_b = w_in.astype(bf16)
    MISC = 16
    o_glu, o_out, o_tail, o_conv = D_MODEL, D_MODEL + wr, D_MODEL + 2 * wr, D_MODEL + 2 * wr + MISC

    def rows16(a):
        return jnp.pad(a, ((0, MISC - a.shape[0]), (0, PACK_W - a.shape[1])))

    def layer_shard(li):
        return jnp.concatenate([w_in_b[li, :, :PACK_W], w_glu[li].astype(bf16), w_out[li].astype(bf16),
                                rows16(w_in_b[li, :, PACK_W:].T), rows16(conv3[li].reshape(3 * CONV_K, cc))])

    def layer_weights(gathered):
        tails = gathered[:, o_tail:o_tail + tail].reshape(N_DEV * tail, PACK_W).T
        w_all = _w_all([gathered[dv, :D_MODEL][None] for dv in range(N_DEV)],
                       [tails[None, :, dv * tail:(dv + 1) * tail] for dv in range(N_DEV)], wc)[0]
        glu = gathered[:, o_glu:o_glu + wr].reshape(N_DEV * wr, D_MODEL)
        out = gathered[:, o_out:o_out + wr].reshape(N_DEV * wr, D_MODEL)
        conv = gathered[:, o_conv:o_conv + 3 * CONV_K, :cc].astype(f32).reshape(N_DEV, 3, CONV_K, cc)
        conv = (conv[:, 0] + conv[:, 1] + conv[:, 2]).transpose(1, 0, 2).reshape(CONV_K, N_DEV * cc)
        return dict(w_all=w_all, w_glu=glu, w_out=out, conv_w=conv)

    def layer_params(li, gathered):
        return dict(layer_weights(gathered), norm_pre=norm_pre[li].reshape(1, -1),
                    a_log=jnp.pad(a_log[li], (DN_HEADS, 128 - 2 * DN_HEADS)).reshape(1, 128),
                    dt_bias=jnp.pad(dt_bias[li], (DN_HEADS, 128 - 2 * DN_HEADS)).reshape(1, 128),
                    head_norm=head_norm[li].reshape(1, -1), ssm_a_re=ssm_a_re[li], ssm_a_im=ssm_a_im[li],
                    ssm_log_dt=ssm_log_dt[li], ssm_b_re=ssm_b_re[li], ssm_b_im=ssm_b_im[li], ssm_c_re=ssm_c_re[li],
                    ssm_c_im=ssm_c_im[li], ssm_d=ssm_d[li].reshape(1, -1),
                    b_glu=b_glu[li].reshape(1, -1), norm_post=norm_post[li].reshape(1, -1))

    act = x[0]
    saved, params = [], []
    gathered = all_gather("gather_weights", layer_shard(0))
    for li in range(depth):
        params.append(layer_params(li, gathered))
        act, sv, gathered = layer_fwd(act, params[li], li, layer_shard(li + 1) if li + 1 < depth else None)
        saved.append(sv)
    loss_part, dy = loss_head("loss_head", act, loss_target[0])
    repl_shapes = {n: w[n].shape for n in REPL}
    repl_rows = _repl_rows(repl_shapes)
    rr = repl_rows // N_DEV
    my_c = lax.axis_index("c")

    def chip_sums(li, gl, extra=None):
        g_conv = gl["conv_w"].astype(bf16)
        tails = jnp.concatenate([_ref_cols(gl["w_all"], dv * wc + PACK_W, (dv + 1) * wc) for dv in range(N_DEV)],
                                axis=1).T
        blocks = jnp.stack([jnp.concatenate(
            [_ref_cols(gl["w_all"], dv * wc, dv * wc + PACK_W), gl["w_glu"][dv * wr:(dv + 1) * wr],
             gl["w_out"][dv * wr:(dv + 1) * wr], rows16(tails[dv * tail:(dv + 1) * tail]),
             rows16(g_conv[:, dv * cc:(dv + 1) * cc])]
            + ([extra[dv * rr:(dv + 1) * rr]] if extra is not None else []))
            for dv in range(N_DEV)])
        blocks = blocks.reshape(4, 2, blocks.shape[1], PACK_W)
        from_sibling = pair_exchange(f"pair_grads_l{li}", blocks)
        own = lax.dynamic_index_in_dim(blocks, my_c, axis=1, keepdims=False)
        return pair_sum(f"pair_sum_grads_l{li}", own, from_sibling)

    grads, slots, pending = [None] * depth, [None] * depth, None
    for li in reversed(range(depth)):
        dy, grads[li], got = layer_bwd(dy, params[li], saved[li], li, pending)
        if pending is not None:
            slots[li + 1] = got
        g_repl = None
        if li == 0:
            g_repl = _repl_pack({n: jnp.stack([grads[l][n] for l in range(depth)]).reshape(w[n].shape) for n in REPL},
                                repl_shapes).astype(bf16)
        pending = chip_sums(li, grads[li], g_repl)
    slots[0] = chip_exchange("scatter_grads", pending)
    grad_x = dy[None]
    loss = lax.psum(loss_part[0, 0], ("x", "y", "c"))
    mine = [sum_slots(f"sum_grads_l{li}", slots[li]) for li in range(depth)]
    o_repl = o_conv + MISC
    gs_w_in = jnp.stack([jnp.concatenate([mi[:o_glu], mi[o_tail:o_tail + tail].T], axis=1) for mi in mine])
    gs_glu = jnp.stack([mi[o_glu:o_out] for mi in mine])
    gs_out = jnp.stack([mi[o_out:o_tail] for mi in mine])
    gs_conv = jnp.stack([mi[o_conv:o_conv + CONV_K, :cc] for mi in mine])
    g_repl_full = all_gather("gather_repl_grads", mine[0][o_repl:o_repl + rr]).reshape(repl_rows, PACK_W)
    g = _repl_unpack(g_repl_full, repl_shapes)
    g.update(w_in=gs_w_in, w_glu=gs_glu, w_out=gs_out, conv_w=gs_conv)

    delta, new_m, new_v = {}, {}, {}
    for n in SHARDED:
        delta[n], new_m[n], new_v[n] = adamw("adamw_" + n, w[n], g[n], m[n], v[n])
    outs = adamw("adamw_repl", *[_repl_pack({n: src[n] for n in REPL}, repl_shapes) for src in (w, g, m, v)])
    for dst, o in zip((delta, new_m, new_v), outs):
        dst.update(_repl_unpack(o, repl_shapes))
    return (loss, grad_x, *[g[n] for n in ALL_W], *[delta[n] for n in ALL_W], *[new_m[n] for n in ALL_W],
            *[new_v[n] for n in ALL_W])
```
